```python
import math
import jax, jax.numpy as jnp
from jax import lax
import numpy as np

D_MODEL = 1024
BATCH = 4
SEQ = 8192
DEPTH = 2
DEC_BATCH = 1
DEC_SEQ = 16384
PAST_LEN = 128

GRID_W = 64
EPS = 1e-6
N_BRANCH = 4
F32 = jnp.float32

MLA_HEADS = 4
MLA_Q_RANK = 256
MLA_KV_RANK = 128
MLA_NOPE = 64
MLA_ROPE = 32
MLA_V = 64
ROPE_THETA = 10000.0
Q_BLOCK = 128

DIL_GROUPS = ((128, 1), (512, 4), (2048, 16))
DIL_HEADS = 4
DIL_HD = 64
T5_BUCKETS = 32
T5_MAX_DIST = 1024

SSM_HEADS = 8
SSM_HD = 64
SSM_INNER = SSM_HEADS * SSM_HD
SSM_GROUPS = 2
SSM_STATE = 128
SSM_CONV = 7
SSM_CHUNK = 128
CONV_CH = SSM_INNER + 2 * SSM_GROUPS * SSM_STATE

NA_HEADS = 4
NA_HD = 64
NA_ROWS = 8
NA_COLS = 16
NA_QCB = 16
NA_KCB = NA_QCB + NA_COLS

PEER_HEADS = 8
PEER_KEYS = 128
PEER_EXPERTS = PEER_KEYS * PEER_KEYS
PEER_QDIM = 256
PEER_TOPK = 16
PEER_TOK_BLOCK = 128

BRANCH_WIDTHS = (MLA_HEADS * MLA_V, DIL_HEADS * DIL_HD, SSM_INNER, NA_HEADS * NA_HD)
BRANCH_ROWS = tuple(sum(BRANCH_WIDTHS[:i]) for i in range(N_BRANCH + 1))
MIX_WIDTH = sum(BRANCH_WIDTHS)
IN_SIZES = (N_BRANCH * D_MODEL, MLA_Q_RANK, MLA_KV_RANK, MLA_ROPE,
            len(DIL_GROUPS) * 3 * DIL_HEADS * DIL_HD,
            SSM_INNER, CONV_CH, 2 * SSM_HEADS,
            3 * NA_HEADS * NA_HD)
IN_SPLITS = tuple(sum(IN_SIZES[:i + 1]) for i in range(len(IN_SIZES) - 1))
IN_WIDTH = sum(IN_SIZES)

kernel_name = "hybrid_bidir_encoder_mla_dilated_ssd_natten_peer"


def rms_norm(x, g):
    x32 = x.astype(F32)
    y = x32 * lax.rsqrt(jnp.mean(x32 * x32, axis=-1, keepdims=True) + EPS)
    return (y * g.astype(F32)).astype(x.dtype)


def apply_rope(x, pos):
    half = x.shape[-1] // 2
    inv = ROPE_THETA ** (-jnp.arange(half, dtype=F32) / half)
    ang = pos.astype(F32)[:, None] * inv[None, :]
    ang = ang.reshape((ang.shape[0],) + (1,) * (x.ndim - 3) + (half,))
    cos, sin = jnp.cos(ang), jnp.sin(ang)
    x1, x2 = x[..., :half].astype(F32), x[..., half:].astype(F32)
    return jnp.concatenate([x1 * cos - x2 * sin, x1 * sin + x2 * cos], axis=-1).astype(x.dtype)


def mla_mixer(c_q, c_kv, k_rope, q_norm, w_qb, kv_norm, w_kvb):
    b, s, _ = c_q.shape
    pos = jnp.arange(s)
    q = (rms_norm(c_q, q_norm) @ w_qb).reshape(b, s, MLA_HEADS, MLA_NOPE + MLA_ROPE)
    q_nope = q[..., :MLA_NOPE]
    q_rope = apply_rope(q[..., MLA_NOPE:], pos)
    kv = (rms_norm(c_kv, kv_norm) @ w_kvb).reshape(b, s, MLA_HEADS, MLA_NOPE + MLA_V)
    k_nope, v = kv[..., :MLA_NOPE], kv[..., MLA_NOPE:]
    k_rope = apply_rope(k_rope, pos)
    scale = (MLA_NOPE + MLA_ROPE) ** -0.5
    nb = s // Q_BLOCK
    qn = q_nope.reshape(b, nb, Q_BLOCK, MLA_HEADS, MLA_NOPE).transpose(1, 0, 2, 3, 4)
    qr = q_rope.reshape(b, nb, Q_BLOCK, MLA_HEADS, MLA_ROPE).transpose(1, 0, 2, 3, 4)

    def block(args):
        qn_b, qr_b = args
        logits = (jnp.einsum('bqhd,bkhd->bhqk', qn_b, k_nope)
                  + jnp.einsum('bqhr,bkr->bhqk', qr_b, k_rope)).astype(F32) * scale
        p = jax.nn.softmax(logits, axis=-1)
        return jnp.einsum('bhqk,bkhd->bqhd', p.astype(v.dtype), v)

    o = lax.map(block, (qn, qr))
    return o.transpose(1, 0, 2, 3, 4).reshape(b, s, MLA_HEADS * MLA_V)


def t5_bucket(rel):
    nb = T5_BUCKETS // 2
    ret = np.where(rel > 0, nb, 0)
    n = np.abs(rel)
    max_exact = nb // 2
    large = max_exact + (np.log(np.maximum(n, 1) / max_exact) / np.log(T5_MAX_DIST / max_exact)
                         * (nb - max_exact)).astype(np.int64)
    large = np.minimum(large, nb - 1)
    return (ret + np.where(n < max_exact, n, large)).astype(np.int32)


def dilated_group(q, k, v, dil, half, bias_tab):
    b, s, h, hd = q.shape
    L = s // dil
    nb = -(-L // half)
    Lp = nb * half
    bb = b * dil

    def to_res(t):
        t = t.reshape(b, L, dil, h, hd).transpose(0, 2, 1, 3, 4).reshape(bb, L, h, hd)
        return jnp.pad(t, ((0, 0), (0, Lp - L), (0, 0), (0, 0)))

    def band(t):
        tp = jnp.pad(t, ((0, 0), (half, half), (0, 0), (0, 0)))
        return jnp.concatenate([tp[:, i * half:i * half + Lp].reshape(bb, nb, half, h, hd)
                                for i in range(3)], axis=2)

    qb = to_res(q).reshape(bb, nb, half, h, hd)
    kb, vb = band(to_res(k)), band(to_res(v))
    rel = np.arange(3 * half)[None, :] - half - np.arange(half)[:, None]
    kpos = np.arange(nb)[:, None] * half - half + np.arange(3 * half)[None, :]
    valid = (np.abs(rel) <= half)[None] & ((kpos >= 0) & (kpos < L))[:, None, :]
    bias = bias_tab[t5_bucket(rel * dil)].transpose(2, 0, 1).astype(F32)
    logits = jnp.einsum('bnqhd,bnkhd->bnhqk', qb, kb).astype(F32) * hd ** -0.5 + bias[None, None]
    logits = jnp.where(valid[None, :, None], logits, -jnp.inf)
    lse = jax.nn.logsumexp(logits, axis=-1)
    p = jnp.exp(logits - lse[..., None])
    o = jnp.einsum('bnhqk,bnkhd->bnqhd', p.astype(v.dtype), vb)

    def from_res(t):
        tail = t.shape[2:]
        t = t[:, :L].reshape((b, dil, L) + tail)
        return jnp.swapaxes(t, 1, 2).reshape((b, s) + tail)

    o = from_res(o.reshape(bb, Lp, h, hd))
    lse = from_res(lse.transpose(0, 1, 3, 2).reshape(bb, Lp, h))
    return o, lse


def dilated_mixer(qkv, t5_table):
    b, s, _ = qkv.shape
    qkv = qkv.reshape(b, s, len(DIL_GROUPS), 3, DIL_HEADS, DIL_HD)
    outs, lses = [], []
    for gi, (win, dil) in enumerate(DIL_GROUPS):
        o, lse = dilated_group(qkv[:, :, gi, 0], qkv[:, :, gi, 1], qkv[:, :, gi, 2], dil, win // (2 * dil),
                               t5_table[:, gi * DIL_HEADS:(gi + 1) * DIL_HEADS])
        outs.append(o)
        lses.append(lse)
    w = jax.nn.softmax(jnp.stack(lses, axis=0), axis=0)
    o = jnp.einsum('gbsh,gbshd->bshd', w.astype(outs[0].dtype), jnp.stack(outs, axis=0))
    return o.reshape(b, s, DIL_HEADS * DIL_HD)


def depthwise_conv(x, w, bias):
    k = w.shape[0]
    y = lax.conv_general_dilated(x, w[:, None, :], window_strides=(1,), padding=[(k // 2, k // 2)],
                                 dimension_numbers=('NWC', 'WIO', 'NWC'), feature_group_count=x.shape[-1])
    return y + bias


def segsum(a):
    q = a.shape[-1]
    cs = jnp.cumsum(a, axis=-1)
    diff = cs[..., :, None] - cs[..., None, :]
    return jnp.where(np.tril(np.ones((q, q), dtype=bool)), diff, -jnp.inf)


def ssd_scan(x, dt, A, Bm, Cm):
    b, l, nh, p = x.shape
    g, n = Bm.shape[2], Bm.shape[3]
    r = nh // g
    c = l // SSM_CHUNK
    Q = SSM_CHUNK
    a = (dt * A).reshape(b, c, Q, g, r).transpose(0, 3, 4, 1, 2)
    dtx = (x.astype(F32) * dt[..., None]).reshape(b, c, Q, g, r, p)
    Bc = Bm.astype(F32).reshape(b, c, Q, g, n)
    Cc = Cm.astype(F32).reshape(b, c, Q, g, n)
    a_cs = jnp.cumsum(a, axis=-1)
    Lmat = jnp.exp(segsum(a))
    CB = jnp.einsum('bcign,bcjgn->bgcij', Cc, Bc)
    y_diag = jnp.einsum('bgcij,bgrcij,bcjgrp->bcigrp', CB, Lmat, dtx)
    decay_st = jnp.exp(a_cs[..., -1:] - a_cs)
    states = jnp.einsum('bcjgn,bgrcj,bcjgrp->bcgrpn', Bc, decay_st, dtx)
    chunk_decay = jnp.exp(a_cs[..., -1])

    def step(st, inp):
        new, dec = inp
        return st * dec[..., None, None] + new, st

    s0 = jnp.zeros((b, g, r, p, n), F32)
    _, s_in = lax.scan(step, s0, (jnp.moveaxis(states, 1, 0), jnp.moveaxis(chunk_decay, 3, 0)))
    y_off = jnp.einsum('bcign,cbgrpn,bgrci->bcigrp', Cc, s_in, jnp.exp(a_cs))
    return (y_diag + y_off).reshape(b, l, nh, p)


def mamba2_mixer(z, xbc, dt_raw, conv_w, conv_b, A_log, dt_bias, D_skip, norm_g):
    b, l, _ = z.shape
    xbc = jax.nn.silu(depthwise_conv(xbc, conv_w, conv_b))
    xs = xbc[..., :SSM_INNER].reshape(b, l, SSM_HEADS, SSM_HD)
    Bm = xbc[..., SSM_INNER:SSM_INNER + SSM_GROUPS * SSM_STATE].reshape(b, l, SSM_GROUPS, SSM_STATE)
    Cm = xbc[..., SSM_INNER + SSM_GROUPS * SSM_STATE:].reshape(b, l, SSM_GROUPS, SSM_STATE)
    dt = jax.nn.softplus(dt_raw.astype(F32).reshape(b, l, 2, SSM_HEADS) + dt_bias.astype(F32))
    A = -jnp.exp(A_log.astype(F32))
    flip = lambda t: jnp.flip(t, axis=1)
    y_f = ssd_scan(xs, dt[:, :, 0], A[0], Bm, Cm)
    y_b = flip(ssd_scan(flip(xs), flip(dt[:, :, 1]), A[1], flip(Bm), flip(Cm)))
    y = y_f + y_b + xs.astype(F32) * D_skip.astype(F32)[:, None]
    y = y.reshape(b, l, SSM_INNER).astype(z.dtype)
    return rms_norm(y * jax.nn.silu(z), norm_g)


def na_mixer(qkv, rpb):
    b, s, _ = qkv.shape
    qkv = qkv.reshape(b, s, 3, NA_HEADS, NA_HD)
    q, k, v = qkv[:, :, 0], qkv[:, :, 1], qkv[:, :, 2]
    rows = s // GRID_W
    kh = min(NA_ROWS, rows)
    ncb = GRID_W // NA_QCB
    r = np.arange(rows)
    key_rows = np.clip(r - kh // 2, 0, rows - kh)[:, None] + np.arange(kh)[None, :]
    cb = np.arange(ncb)
    key_cols = np.clip(cb * NA_QCB - NA_COLS // 2, 0, GRID_W - NA_KCB)[:, None] + np.arange(NA_KCB)[None, :]
    flat_idx = (key_rows[:, :, None, None] * GRID_W + key_cols[None, None]).astype(np.int32)
    qcol = cb[:, None] * NA_QCB + np.arange(NA_QCB)[None, :]
    cs = np.clip(qcol - NA_COLS // 2, 0, GRID_W - NA_COLS)
    col_ok = (key_cols[:, None, :] >= cs[..., None]) & (key_cols[:, None, :] < cs[..., None] + NA_COLS)
    dr_idx = (key_rows - r[:, None]) + NA_ROWS - 1
    dc_idx = np.clip(key_cols[:, None, :] - qcol[..., None] + NA_COLS - 1, 0, 2 * NA_COLS - 2)
    bias = rpb[:, dr_idx[:, None, None, :, None], dc_idx[None, :, :, None, :]].astype(F32)
    kg = k[:, flat_idx]
    vg = v[:, flat_idx]
    qg = q.reshape(b, rows, ncb, NA_QCB, NA_HEADS, NA_HD)
    logits = jnp.einsum('brcqhd,brwckhd->bhrcqwk', qg, kg).astype(F32) * NA_HD ** -0.5 + bias[None]
    logits = jnp.where(col_ok[:, :, None, :], logits, -jnp.inf)
    shp = logits.shape
    p = jax.nn.softmax(logits.reshape(shp[:-2] + (shp[-2] * shp[-1],)), axis=-1).reshape(shp)
    o = jnp.einsum('bhrcqwk,brwckhd->brcqhd', p.astype(v.dtype), vg)
    return o.reshape(b, s, NA_HEADS * NA_HD)


def peer_ffn(x, w_q, keys, u, v):
    b, s, d = x.shape
    t = x.reshape(b * s, d)
    n_tok = b * s
    q = (t @ w_q).reshape(n_tok, PEER_HEADS, 2, PEER_QDIM // 2)
    sc = jnp.einsum('thid,hikd->thik', q, keys).astype(F32)
    s1, i1 = lax.top_k(sc[:, :, 0], PEER_TOPK)
    s2, i2 = lax.top_k(sc[:, :, 1], PEER_TOPK)
    cand = (s1[..., :, None] + s2[..., None, :]).reshape(n_tok, PEER_HEADS, PEER_TOPK * PEER_TOPK)
    cand_idx = (i1[..., :, None] * PEER_KEYS + i2[..., None, :]).reshape(n_tok, PEER_HEADS, PEER_TOPK * PEER_TOPK)
    top_s, sel = lax.top_k(cand, PEER_TOPK)
    idx = jnp.take_along_axis(cand_idx, sel, axis=-1)
    gate = jax.nn.softmax(top_s, axis=-1).astype(x.dtype)
    nblk = n_tok // PEER_TOK_BLOCK
    ne = PEER_HEADS * PEER_TOPK

    def blk(args):
        tb, ib, gb = args
        hid = jnp.einsum('tkd,td->tk', u[ib], tb)
        act = jax.nn.gelu(hid, approximate=False) * gb
        return jnp.einsum('tk,tkd->td', act, v[ib])

    out = lax.map(blk, (t.reshape(nblk, PEER_TOK_BLOCK, d), idx.reshape(nblk, PEER_TOK_BLOCK, ne),
                        gate.reshape(nblk, PEER_TOK_BLOCK, ne)))
    return out.reshape(b, s, d)


def encoder(x, norm1_g, w_in, b_gate, mla_q_norm, mla_w_qb, mla_kv_norm, mla_w_kvb, t5_table,
            ssm_conv_w, ssm_conv_b, ssm_A_log, ssm_dt_bias, ssm_D, ssm_norm_g, na_rpb,
            w_branch, w_out, norm2_g, peer_wq, peer_keys, peer_u, peer_v, final_g):
    b, s, _ = x.shape
    for l in range(DEPTH):
        h = rms_norm(x, norm1_g[l])
        gate, a_cq, a_ckv, a_kr, b_qkv, c_z, c_xbc, c_dt, d_qkv = jnp.split(h @ w_in[l], IN_SPLITS, axis=-1)
        y_a = mla_mixer(a_cq, a_ckv, a_kr, mla_q_norm[l], mla_w_qb[l], mla_kv_norm[l], mla_w_kvb[l])
        y_b = dilated_mixer(b_qkv, t5_table)
        y_c = mamba2_mixer(c_z, c_xbc, c_dt, ssm_conv_w[l], ssm_conv_b[l], ssm_A_log[l], ssm_dt_bias[l],
                           ssm_D[l], ssm_norm_g[l])
        y_d = na_mixer(d_qkv, na_rpb[l])
        gates = jax.nn.sigmoid((gate + b_gate[l]).astype(F32)).astype(x.dtype).reshape(b, s, N_BRANCH, D_MODEL)
        merged = jnp.zeros_like(x)
        for i, y_i in enumerate((y_a, y_b, y_c, y_d)):
            proj = y_i.astype(x.dtype) @ w_branch[l, BRANCH_ROWS[i]:BRANCH_ROWS[i + 1]]
            merged = merged + gates[:, :, i] * proj
        x = x + merged @ w_out[l]
        x = x + peer_ffn(rms_norm(x, norm2_g[l]), peer_wq[l], peer_keys[l], peer_u[l], peer_v[l])
    return rms_norm(x, final_g)


def setup_inputs(seed: int = 0) -> dict:
    key = jax.random.key(seed)
    ks = iter(jax.random.split(key, 40))

    def nrm(shape, scale):
        return jax.random.normal(next(ks), shape, F32) * scale

    def gain(shape):
        return 1.0 + 0.02 * jax.random.normal(next(ks), shape, F32)

    dt0 = jnp.exp(jax.random.uniform(next(ks), (DEPTH, 2, SSM_HEADS), F32, math.log(1e-3), math.log(1e-1)))
    return {
        'x_prompt': nrm((BATCH, SEQ, D_MODEL), 1.0),
        'x_sample': nrm((DEC_BATCH, DEC_SEQ, D_MODEL), 1.0),
        'norm1_g': gain((DEPTH, D_MODEL)),
        'w_in': nrm((DEPTH, D_MODEL, IN_WIDTH), D_MODEL ** -0.5),
        'b_gate': nrm((DEPTH, N_BRANCH * D_MODEL), 0.01),
        'mla_q_norm': gain((DEPTH, MLA_Q_RANK)),
        'mla_w_qb': nrm((DEPTH, MLA_Q_RANK, MLA_HEADS * (MLA_NOPE + MLA_ROPE)), MLA_Q_RANK ** -0.5),
        'mla_kv_norm': gain((DEPTH, MLA_KV_RANK)),
        'mla_w_kvb': nrm((DEPTH, MLA_KV_RANK, MLA_HEADS * (MLA_NOPE + MLA_V)), MLA_KV_RANK ** -0.5),
        't5_table': nrm((T5_BUCKETS, len(DIL_GROUPS) * DIL_HEADS), 0.1),
        'ssm_conv_w': nrm((DEPTH, SSM_CONV, CONV_CH), SSM_CONV ** -0.5),
        'ssm_conv_b': nrm((DEPTH, CONV_CH), 0.01),
        'ssm_A_log': jnp.log(jax.random.uniform(next(ks), (DEPTH, 2, SSM_HEADS), F32, 1.0, 16.0)),
        'ssm_dt_bias': dt0 + jnp.log(-jnp.expm1(-dt0)),
        'ssm_D': gain((DEPTH, SSM_HEADS)),
        'ssm_norm_g': gain((DEPTH, SSM_INNER)),
        'na_rpb': nrm((DEPTH, NA_HEADS, 2 * NA_ROWS - 1, 2 * NA_COLS - 1), 0.1),
        'w_branch': jnp.concatenate([nrm((DEPTH, wd, D_MODEL), wd ** -0.5) for wd in BRANCH_WIDTHS], axis=1),
        'w_out': nrm((DEPTH, D_MODEL, D_MODEL), D_MODEL ** -0.5),
        'norm2_g': gain((DEPTH, D_MODEL)),
        'peer_wq': nrm((DEPTH, D_MODEL, PEER_HEADS * PEER_QDIM), D_MODEL ** -0.5),
        'peer_keys': nrm((DEPTH, PEER_HEADS, 2, PEER_KEYS, PEER_QDIM // 2), (PEER_QDIM // 2) ** -0.5),
        'peer_u': nrm((DEPTH, PEER_EXPERTS, D_MODEL), D_MODEL ** -0.5),
        'peer_v': nrm((DEPTH, PEER_EXPERTS, D_MODEL), (PEER_HEADS * PEER_TOPK) ** -0.5),
        'final_g': gain((D_MODEL,)),
    }


def reference(x_prompt, x_sample, norm1_g, w_in, b_gate, mla_q_norm, mla_w_qb, mla_kv_norm, mla_w_kvb,
              t5_table, ssm_conv_w, ssm_conv_b, ssm_A_log, ssm_dt_bias, ssm_D, ssm_norm_g, na_rpb,
              w_branch, w_out, norm2_g, peer_wq, peer_keys, peer_u, peer_v, final_g):
    shared = (norm1_g, w_in, b_gate, mla_q_norm, mla_w_qb, mla_kv_norm, mla_w_kvb, t5_table,
              ssm_conv_w, ssm_conv_b, ssm_A_log, ssm_dt_bias, ssm_D, ssm_norm_g, na_rpb,
              w_branch, w_out, norm2_g, peer_wq, peer_keys, peer_u, peer_v, final_g)
    y_prompt = encoder(x_prompt, *shared)
    y_sample = encoder(x_sample, *shared)
    return (y_prompt, y_sample)
```

```python
import functools
import math

import numpy as np
import jax
import jax.numpy as jnp
from jax import lax
from jax.experimental import pallas as pl
from jax.experimental.pallas import tpu as pltpu

F32 = jnp.float32
BF16 = jnp.bfloat16
MXU_DTYPE = BF16

D_MODEL = 1024
DEPTH = 2
GRID_W = 64
EPS = 1e-6
N_BRANCH = 4

MLA_HEADS = 4
MLA_Q_RANK = 256
MLA_KV_RANK = 128
MLA_NOPE = 64
MLA_ROPE = 32
MLA_V = 64
ROPE_THETA = 10000.0
Q_BLOCK = 128

DIL_GROUPS = ((128, 1), (512, 4), (2048, 16))
DIL_HEADS = 4
DIL_HD = 64
T5_BUCKETS = 32
T5_MAX_DIST = 1024

SSM_HEADS = 8
SSM_HD = 64
SSM_INNER = SSM_HEADS * SSM_HD
SSM_GROUPS = 2
SSM_STATE = 128
SSM_CONV = 7
SSM_CHUNK = 128
CONV_CH = SSM_INNER + 2 * SSM_GROUPS * SSM_STATE

NA_HEADS = 4
NA_HD = 64
NA_ROWS = 8
NA_COLS = 16
NA_QCB = 16
NA_KCB = NA_QCB + NA_COLS

PEER_HEADS = 8
PEER_KEYS = 128
PEER_EXPERTS = PEER_KEYS * PEER_KEYS
PEER_QDIM = 256
PEER_TOPK = 16
PEER_TOK_BLOCK = 128

BRANCH_WIDTHS = (MLA_HEADS * MLA_V, DIL_HEADS * DIL_HD, SSM_INNER, NA_HEADS * NA_HD)
BRANCH_ROWS = tuple(sum(BRANCH_WIDTHS[:i]) for i in range(N_BRANCH + 1))
IN_SIZES = (N_BRANCH * D_MODEL, MLA_Q_RANK, MLA_KV_RANK, MLA_ROPE,
            len(DIL_GROUPS) * 3 * DIL_HEADS * DIL_HD,
            SSM_INNER, CONV_CH, 2 * SSM_HEADS,
            3 * NA_HEADS * NA_HD)
IN_SPLITS = tuple(sum(IN_SIZES[:i + 1]) for i in range(len(IN_SIZES) - 1))

VMEM_LIMIT_BYTES = 56 * 1024 * 1024
LANES = 128
SUBLANES = 8


def rms_norm(x, g):
    x32 = x.astype(F32)
    y = x32 * lax.rsqrt(jnp.mean(x32 * x32, axis=-1, keepdims=True) + EPS)
    return (y * g.astype(F32)).astype(x.dtype)


PEER_ROUTE_TB = 256
PEER_TB = 256
PEER_EC = 1024
PEER_GATE_ROWS = 32
PEER_HALF = PEER_QDIM // 2
PEER_CAND_ROWS = 2 * SUBLANES + 7 * SUBLANES + SUBLANES


def _gelu_exact(x):
    return 0.5 * x * (1.0 + lax.erf(x * np.float32(math.sqrt(0.5))))


def _extract_desc(vals, n_out, out_ref, row0):
    for k in range(n_out):
        m = jnp.max(vals, axis=0, keepdims=True)
        out_ref[pl.ds(row0 + k, 1), :] = m
        vals = jnp.where(vals == m, -jnp.inf, vals)


def _peer_route_kernel(x_ref, g_ref, wqt_ref, keys_ref,
                       xnt_ref, s1_ref, s2_ref, e1_ref, e2_ref, tau_ref,
                       qt_scr, top_scr, cand_scr, tops_scr):
    x = x_ref[...]
    xn = x * lax.rsqrt(jnp.mean(x * x, axis=-1, keepdims=True) + EPS) * g_ref[...]
    xnt = xn.T.astype(MXU_DTYPE)
    xnt_ref[...] = xnt
    qt_scr[...] = jnp.dot(wqt_ref[...], xnt, preferred_element_type=F32).astype(MXU_DTYPE)

    def head(h, carry):
        q1 = qt_scr[pl.ds(pl.multiple_of(h * PEER_QDIM, PEER_QDIM), PEER_HALF), :]
        q2 = qt_scr[pl.ds(pl.multiple_of(h * PEER_QDIM + PEER_HALF, PEER_HALF), PEER_HALF), :]
        s1 = jnp.dot(keys_ref[2 * h], q1, preferred_element_type=F32)
        s2 = jnp.dot(keys_ref[2 * h + 1], q2, preferred_element_type=F32)
        _extract_desc(s1, PEER_TOPK, top_scr, 0)
        _extract_desc(s2, PEER_TOPK, top_scr, PEER_TOPK)
        t1 = top_scr[0:PEER_TOPK, :]
        t2 = top_scr[PEER_TOPK:2 * PEER_TOPK, :]
        cand_scr[0:2 * SUBLANES, :] = t1[0:1, :] + t2
        for a in range(1, SUBLANES):
            cand_scr[(a + 1) * SUBLANES:(a + 2) * SUBLANES, :] = t1[a:a + 1, :] + t2[0:SUBLANES, :]
        cand_scr[9 * SUBLANES:10 * SUBLANES, :] = t1[SUBLANES:2 * SUBLANES, :] + t2[0:1, :]
        _extract_desc(cand_scr[...], PEER_TOPK, tops_scr, 0)
        top_s = tops_scr[...]
        z = jnp.sum(jnp.exp(top_s - top_s[0:1, :]), axis=0, keepdims=True)
        tau_ref[pl.ds(h, 1), :] = top_s[PEER_TOPK - 1:PEER_TOPK, :]
        s1_ref[h] = s1
        s2_ref[h] = s2
        e1_ref[h] = jnp.exp(s1 - t1[0:1, :]) / z
        e2_ref[h] = jnp.exp(s2 - t2[0:1, :])
        return carry

    lax.fori_loop(0, PEER_HEADS, head, 0)


def _peer_route(x2d, g, wqt, keys):
    n_tok = x2d.shape[0]
    tb = PEER_ROUTE_TB
    rt_shape = jax.ShapeDtypeStruct((PEER_HEADS, PEER_KEYS, n_tok), F32)
    rt_spec = pl.BlockSpec((PEER_HEADS, PEER_KEYS, tb), lambda i: (0, 0, i))
    return pl.pallas_call(
        _peer_route_kernel,
        grid=(n_tok // tb,),
        in_specs=[
            pl.BlockSpec((tb, D_MODEL), lambda i: (i, 0)),
            pl.BlockSpec((1, D_MODEL), lambda i: (0, 0)),
            pl.BlockSpec((PEER_HEADS * PEER_QDIM, D_MODEL), lambda i: (0, 0)),
            pl.BlockSpec((2 * PEER_HEADS, PEER_KEYS, PEER_HALF), lambda i: (0, 0, 0)),
        ],
        out_specs=[
            pl.BlockSpec((D_MODEL, tb), lambda i: (0, i)),
            rt_spec, rt_spec, rt_spec, rt_spec,
            pl.BlockSpec((PEER_HEADS, tb), lambda i: (0, i)),
        ],
        out_shape=[
            jax.ShapeDtypeStruct((D_MODEL, n_tok), MXU_DTYPE),
            rt_shape, rt_shape, rt_shape, rt_shape,
            jax.ShapeDtypeStruct((PEER_HEADS, n_tok), F32),
        ],
        scratch_shapes=[
            pltpu.VMEM((PEER_HEADS * PEER_QDIM, tb), MXU_DTYPE),
            pltpu.VMEM((2 * PEER_TOPK, tb), F32),
            pltpu.VMEM((PEER_CAND_ROWS, tb), F32),
            pltpu.VMEM((PEER_TOPK, tb), F32),
        ],
        compiler_params=pltpu.CompilerParams(
            dimension_semantics=("arbitrary",), vmem_limit_bytes=VMEM_LIMIT_BYTES),
        name="peer_route",
    )(x2d, g, wqt, keys)


def _peer_expert_kernel(x_ref, xnt_ref, s1_ref, s2_ref, e1_ref, e2_ref, tau_ref, u_ref, vt_ref,
                        out_ref, acc_scr, w_scr):
    c = pl.program_id(1)
    n_chunks = pl.num_programs(1) - 1
    n_i1 = PEER_EC // PEER_KEYS
    tb = acc_scr.shape[1]

    @pl.when(c == 0)
    def _():
        acc_scr[...] = jnp.zeros_like(acc_scr)
        w_scr[1] = jnp.zeros(w_scr.shape[1:], w_scr.dtype)

    acc_scr[...] += jnp.dot(vt_ref[...], w_scr[(c + 1) % 2], preferred_element_type=F32)
    hid = jnp.dot(u_ref[...], xnt_ref[...], preferred_element_type=F32)
    slot = c % 2
    n_groups = PEER_KEYS // SUBLANES
    for i1l in range(n_i1):
        for l0 in range(0, tb, LANES):
            lanes = slice(l0, l0 + LANES)
            gates = [jnp.zeros((SUBLANES, LANES), F32) for _ in range(n_groups)]
            for h in range(PEER_HEADS):
                s1b = jnp.broadcast_to(s1_ref[h, i1l:i1l + 1, lanes], (SUBLANES, LANES))
                e1b = jnp.broadcast_to(e1_ref[h, i1l:i1l + 1, lanes], (SUBLANES, LANES))
                taub = jnp.broadcast_to(tau_ref[h:h + 1, lanes], (SUBLANES, LANES))
                for k in range(n_groups):
                    rows = slice(k * SUBLANES, (k + 1) * SUBLANES)
                    s = s2_ref[h, rows, lanes] + s1b
                    w = e2_ref[h, rows, lanes] * e1b
                    gates[k] = gates[k] + jnp.where(s >= taub, w, 0.0)
            for k in range(0, n_groups, 2):
                e0 = i1l * PEER_KEYS + k * SUBLANES
                act = _gelu_exact(hid[e0:e0 + 2 * SUBLANES, lanes])
                gate = jnp.concatenate([gates[k], gates[k + 1]], axis=0)
                w_scr[slot, e0:e0 + 2 * SUBLANES, lanes] = (act * gate).astype(MXU_DTYPE)

    @pl.when(c == n_chunks)
    def _():
        out_ref[...] = x_ref[...] + acc_scr[...].T


def _peer_experts(x2d, xnt, s1, s2, e1, e2, tau, u, vt):
    n_tok = x2d.shape[0]
    tb, ec = PEER_TB, PEER_EC
    n_chunks = PEER_EXPERTS // ec
    rt_spec = pl.BlockSpec((PEER_HEADS, PEER_KEYS, tb), lambda j, c: (0, 0, j))
    row_spec = pl.BlockSpec((PEER_HEADS, ec // PEER_KEYS, tb),
                            lambda j, c: (0, jnp.minimum(c, n_chunks - 1), j))
    return pl.pallas_call(
        _peer_expert_kernel,
        grid=(n_tok // tb, n_chunks + 1),
        in_specs=[
            pl.BlockSpec((tb, D_MODEL), lambda j, c: (j, 0)),
            pl.BlockSpec((D_MODEL, tb), lambda j, c: (0, j)),
            row_spec, rt_spec, row_spec, rt_spec,
            pl.BlockSpec((PEER_HEADS, tb), lambda j, c: (0, j)),
            pl.BlockSpec((ec, D_MODEL), lambda j, c: (jnp.minimum(c, n_chunks - 1), 0)),
            pl.BlockSpec((D_MODEL, ec), lambda j, c: (0, jnp.maximum(c - 1, 0))),
        ],
        out_specs=pl.BlockSpec((tb, D_MODEL), lambda j, c: (j, 0)),
        out_shape=jax.ShapeDtypeStruct((n_tok, D_MODEL), F32),
        scratch_shapes=[
            pltpu.VMEM((D_MODEL, tb), F32),
            pltpu.VMEM((2, ec, tb), MXU_DTYPE),
        ],
        compiler_params=pltpu.CompilerParams(
            dimension_semantics=("arbitrary", "arbitrary"), vmem_limit_bytes=VMEM_LIMIT_BYTES),
        name="peer_experts",
    )(x2d, xnt, s1, s2, e1, e2, tau, u, vt)


def peer_block(x2d, norm_g, w_q, keys, u, v):
    wqt = w_q.T.astype(MXU_DTYPE)
    keys2 = keys.reshape(2 * PEER_HEADS, PEER_KEYS, PEER_HALF).astype(MXU_DTYPE)
    xnt, s1, s2, e1, e2, tau = _peer_route(x2d, norm_g.reshape(1, D_MODEL), wqt, keys2)
    return _peer_experts(x2d, xnt, s1, s2, e1, e2, tau, u, v)


def apply_rope(x, pos):
    half = x.shape[-1] // 2
    inv = ROPE_THETA ** (-jnp.arange(half, dtype=F32) / half)
    ang = pos.astype(F32)[:, None] * inv[None, :]
    ang = ang.reshape((ang.shape[0],) + (1,) * (x.ndim - 3) + (half,))
    cos, sin = jnp.cos(ang), jnp.sin(ang)
    x1, x2 = x[..., :half].astype(F32), x[..., half:].astype(F32)
    return jnp.concatenate([x1 * cos - x2 * sin, x1 * sin + x2 * cos], axis=-1).astype(x.dtype)


def mla_mixer(c_q, c_kv, k_rope, q_norm, w_qb, kv_norm, w_kvb):
    b, s, _ = c_q.shape
    pos = jnp.arange(s)
    q = (rms_norm(c_q, q_norm) @ w_qb).reshape(b, s, MLA_HEADS, MLA_NOPE + MLA_ROPE)
    q_nope = q[..., :MLA_NOPE]
    q_rope = apply_rope(q[..., MLA_NOPE:], pos)
    kv = (rms_norm(c_kv, kv_norm) @ w_kvb).reshape(b, s, MLA_HEADS, MLA_NOPE + MLA_V)
    k_nope, v = kv[..., :MLA_NOPE], kv[..., MLA_NOPE:]
    k_rope = apply_rope(k_rope, pos)
    scale = (MLA_NOPE + MLA_ROPE) ** -0.5
    nb = s // Q_BLOCK
    qn = q_nope.reshape(b, nb, Q_BLOCK, MLA_HEADS, MLA_NOPE).transpose(1, 0, 2, 3, 4)
    qr = q_rope.reshape(b, nb, Q_BLOCK, MLA_HEADS, MLA_ROPE).transpose(1, 0, 2, 3, 4)

    def block(args):
        qn_b, qr_b = args
        logits = (jnp.einsum('bqhd,bkhd->bhqk', qn_b, k_nope)
                  + jnp.einsum('bqhr,bkr->bhqk', qr_b, k_rope)).astype(F32) * scale
        p = jax.nn.softmax(logits, axis=-1)
        return jnp.einsum('bhqk,bkhd->bqhd', p.astype(v.dtype), v)

    o = lax.map(block, (qn, qr))
    return o.transpose(1, 0, 2, 3, 4).reshape(b, s, MLA_HEADS * MLA_V)


def t5_bucket(rel):
    nb = T5_BUCKETS // 2
    ret = np.where(rel > 0, nb, 0)
    n = np.abs(rel)
    max_exact = nb // 2
    large = max_exact + (np.log(np.maximum(n, 1) / max_exact) / np.log(T5_MAX_DIST / max_exact)
                         * (nb - max_exact)).astype(np.int64)
    large = np.minimum(large, nb - 1)
    return (ret + np.where(n < max_exact, n, large)).astype(np.int32)


def dilated_group(q, k, v, dil, half, bias_tab):
    b, s, h, hd = q.shape
    L = s // dil
    nb = -(-L // half)
    Lp = nb * half
    bb = b * dil

    def to_res(t):
        t = t.reshape(b, L, dil, h, hd).transpose(0, 2, 1, 3, 4).reshape(bb, L, h, hd)
        return jnp.pad(t, ((0, 0), (0, Lp - L), (0, 0), (0, 0)))

    def band(t):
        tp = jnp.pad(t, ((0, 0), (half, half), (0, 0), (0, 0)))
        return jnp.concatenate([tp[:, i * half:i * half + Lp].reshape(bb, nb, half, h, hd)
                                for i in range(3)], axis=2)

    qb = to_res(q).reshape(bb, nb, half, h, hd)
    kb, vb = band(to_res(k)), band(to_res(v))
    rel = np.arange(3 * half)[None, :] - half - np.arange(half)[:, None]
    kpos = np.arange(nb)[:, None] * half - half + np.arange(3 * half)[None, :]
    valid = (np.abs(rel) <= half)[None] & ((kpos >= 0) & (kpos < L))[:, None, :]
    bias = bias_tab[t5_bucket(rel * dil)].transpose(2, 0, 1).astype(F32)
    logits = jnp.einsum('bnqhd,bnkhd->bnhqk', qb, kb).astype(F32) * hd ** -0.5 + bias[None, None]
    logits = jnp.where(valid[None, :, None], logits, -jnp.inf)
    lse = jax.nn.logsumexp(logits, axis=-1)
    p = jnp.exp(logits - lse[..., None])
    o = jnp.einsum('bnhqk,bnkhd->bnqhd', p.astype(v.dtype), vb)

    def from_res(t):
        tail = t.shape[2:]
        t = t[:, :L].reshape((b, dil, L) + tail)
        return jnp.swapaxes(t, 1, 2).reshape((b, s) + tail)

    o = from_res(o.reshape(bb, Lp, h, hd))
    lse = from_res(lse.transpose(0, 1, 3, 2).reshape(bb, Lp, h))
    return o, lse


def dilated_mixer(qkv, t5_table):
    b, s, _ = qkv.shape
    qkv = qkv.reshape(b, s, len(DIL_GROUPS), 3, DIL_HEADS, DIL_HD)
    outs, lses = [], []
    for gi, (win, dil) in enumerate(DIL_GROUPS):
        o, lse = dilated_group(qkv[:, :, gi, 0], qkv[:, :, gi, 1], qkv[:, :, gi, 2], dil, win // (2 * dil),
                               t5_table[:, gi * DIL_HEADS:(gi + 1) * DIL_HEADS])
        outs.append(o)
        lses.append(lse)
    w = jax.nn.softmax(jnp.stack(lses, axis=0), axis=0)
    o = jnp.einsum('gbsh,gbshd->bshd', w.astype(outs[0].dtype), jnp.stack(outs, axis=0))
    return o.reshape(b, s, DIL_HEADS * DIL_HD)


def depthwise_conv(x, w, bias):
    k = w.shape[0]
    y = lax.conv_general_dilated(x, w[:, None, :], window_strides=(1,), padding=[(k // 2, k // 2)],
                                 dimension_numbers=('NWC', 'WIO', 'NWC'), feature_group_count=x.shape[-1])
    return y + bias


def segsum(a):
    q = a.shape[-1]
    cs = jnp.cumsum(a, axis=-1)
    diff = cs[..., :, None] - cs[..., None, :]
    return jnp.where(np.tril(np.ones((q, q), dtype=bool)), diff, -jnp.inf)


def ssd_scan(x, dt, A, Bm, Cm):
    b, l, nh, p = x.shape
    g, n = Bm.shape[2], Bm.shape[3]
    r = nh // g
    c = l // SSM_CHUNK
    Q = SSM_CHUNK
    a = (dt * A).reshape(b, c, Q, g, r).transpose(0, 3, 4, 1, 2)
    dtx = (x.astype(F32) * dt[..., None]).reshape(b, c, Q, g, r, p)
    Bc = Bm.astype(F32).reshape(b, c, Q, g, n)
    Cc = Cm.astype(F32).reshape(b, c, Q, g, n)
    a_cs = jnp.cumsum(a, axis=-1)
    Lmat = jnp.exp(segsum(a))
    CB = jnp.einsum('bcign,bcjgn->bgcij', Cc, Bc)
    y_diag = jnp.einsum('bgcij,bgrcij,bcjgrp->bcigrp', CB, Lmat, dtx)
    decay_st = jnp.exp(a_cs[..., -1:] - a_cs)
    states = jnp.einsum('bcjgn,bgrcj,bcjgrp->bcgrpn', Bc, decay_st, dtx)
    chunk_decay = jnp.exp(a_cs[..., -1])

    def step(st, inp):
        new, dec = inp
        return st * dec[..., None, None] + new, st

    s0 = jnp.zeros((b, g, r, p, n), F32)
    _, s_in = lax.scan(step, s0, (jnp.moveaxis(states, 1, 0), jnp.moveaxis(chunk_decay, 3, 0)))
    y_off = jnp.einsum('bcign,cbgrpn,bgrci->bcigrp', Cc, s_in, jnp.exp(a_cs))
    return (y_diag + y_off).reshape(b, l, nh, p)


def mamba2_mixer(z, xbc, dt_raw, conv_w, conv_b, A_log, dt_bias, D_skip, norm_g):
    b, l, _ = z.shape
    xbc = jax.nn.silu(depthwise_conv(xbc, conv_w, conv_b))
    xs = xbc[..., :SSM_INNER].reshape(b, l, SSM_HEADS, SSM_HD)
    Bm = xbc[..., SSM_INNER:SSM_INNER + SSM_GROUPS * SSM_STATE].reshape(b, l, SSM_GROUPS, SSM_STATE)
    Cm = xbc[..., SSM_INNER + SSM_GROUPS * SSM_STATE:].reshape(b, l, SSM_GROUPS, SSM_STATE)
    dt = jax.nn.softplus(dt_raw.astype(F32).reshape(b, l, 2, SSM_HEADS) + dt_bias.astype(F32))
    A = -jnp.exp(A_log.astype(F32))
    flip = lambda t: jnp.flip(t, axis=1)
    y_f = ssd_scan(xs, dt[:, :, 0], A[0], Bm, Cm)
    y_b = flip(ssd_scan(flip(xs), flip(dt[:, :, 1]), A[1], flip(Bm), flip(Cm)))
    y = y_f + y_b + xs.astype(F32) * D_skip.astype(F32)[:, None]
    y = y.reshape(b, l, SSM_INNER).astype(z.dtype)
    return rms_norm(y * jax.nn.silu(z), norm_g)


def na_mixer(qkv, rpb):
    b, s, _ = qkv.shape
    qkv = qkv.reshape(b, s, 3, NA_HEADS, NA_HD)
    q, k, v = qkv[:, :, 0], qkv[:, :, 1], qkv[:, :, 2]
    rows = s // GRID_W
    kh = min(NA_ROWS, rows)
    ncb = GRID_W // NA_QCB
    r = np.arange(rows)
    key_rows = np.clip(r - kh // 2, 0, rows - kh)[:, None] + np.arange(kh)[None, :]
    cb = np.arange(ncb)
    key_cols = np.clip(cb * NA_QCB - NA_COLS // 2, 0, GRID_W - NA_KCB)[:, None] + np.arange(NA_KCB)[None, :]
    flat_idx = (key_rows[:, :, None, None] * GRID_W + key_cols[None, None]).astype(np.int32)
    qcol = cb[:, None] * NA_QCB + np.arange(NA_QCB)[None, :]
    cs = np.clip(qcol - NA_COLS // 2, 0, GRID_W - NA_COLS)
    col_ok = (key_cols[:, None, :] >= cs[..., None]) & (key_cols[:, None, :] < cs[..., None] + NA_COLS)
    dr_idx = (key_rows - r[:, None]) + NA_ROWS - 1
    dc_idx = np.clip(key_cols[:, None, :] - qcol[..., None] + NA_COLS - 1, 0, 2 * NA_COLS - 2)
    bias = rpb[:, dr_idx[:, None, None, :, None], dc_idx[None, :, :, None, :]].astype(F32)
    kg = k[:, flat_idx]
    vg = v[:, flat_idx]
    qg = q.reshape(b, rows, ncb, NA_QCB, NA_HEADS, NA_HD)
    logits = jnp.einsum('brcqhd,brwckhd->bhrcqwk', qg, kg).astype(F32) * NA_HD ** -0.5 + bias[None]
    logits = jnp.where(col_ok[:, :, None, :], logits, -jnp.inf)
    shp = logits.shape
    p = jax.nn.softmax(logits.reshape(shp[:-2] + (shp[-2] * shp[-1],)), axis=-1).reshape(shp)
    o = jnp.einsum('bhrcqwk,brwckhd->brcqhd', p.astype(v.dtype), vg)
    return o.reshape(b, s, NA_HEADS * NA_HD)


def encoder(x, norm1_g, w_in, b_gate, mla_q_norm, mla_w_qb, mla_kv_norm, mla_w_kvb, t5_table,
            ssm_conv_w, ssm_conv_b, ssm_A_log, ssm_dt_bias, ssm_D, ssm_norm_g, na_rpb,
            w_branch, w_out, norm2_g, peer_wq, peer_keys, peer_u, peer_vt, final_g):
    b, s, _ = x.shape
    for l in range(DEPTH):
        h = rms_norm(x, norm1_g[l])
        gate, a_cq, a_ckv, a_kr, b_qkv, c_z, c_xbc, c_dt, d_qkv = jnp.split(h @ w_in[l], IN_SPLITS, axis=-1)
        y_a = mla_mixer(a_cq, a_ckv, a_kr, mla_q_norm[l], mla_w_qb[l], mla_kv_norm[l], mla_w_kvb[l])
        y_b = dilated_mixer(b_qkv, t5_table)
        y_c = mamba2_mixer(c_z, c_xbc, c_dt, ssm_conv_w[l], ssm_conv_b[l], ssm_A_log[l], ssm_dt_bias[l],
                           ssm_D[l], ssm_norm_g[l])
        y_d = na_mixer(d_qkv, na_rpb[l])
        gates = jax.nn.sigmoid((gate + b_gate[l]).astype(F32)).astype(x.dtype).reshape(b, s, N_BRANCH, D_MODEL)
        merged = jnp.zeros_like(x)
        for i, y_i in enumerate((y_a, y_b, y_c, y_d)):
            proj = y_i.astype(x.dtype) @ w_branch[l, BRANCH_ROWS[i]:BRANCH_ROWS[i + 1]]
            merged = merged + gates[:, :, i] * proj
        x = x + merged @ w_out[l]
        x = peer_block(x.reshape(b * s, D_MODEL), norm2_g[l], peer_wq[l], peer_keys[l],
                       peer_u[l], peer_vt[l]).reshape(b, s, D_MODEL)
    return rms_norm(x, final_g)


def kernel(x_prompt, x_sample, norm1_g, w_in, b_gate, mla_q_norm, mla_w_qb, mla_kv_norm, mla_w_kvb, t5_table, ssm_conv_w, ssm_conv_b, ssm_A_log, ssm_dt_bias, ssm_D, ssm_norm_g, na_rpb, w_branch, w_out, norm2_g, peer_wq, peer_keys, peer_u, peer_v, final_g):
    peer_u16 = peer_u.astype(MXU_DTYPE)
    peer_vt16 = jnp.swapaxes(peer_v, 1, 2).astype(MXU_DTYPE)
    shared = (norm1_g, w_in, b_gate, mla_q_norm, mla_w_qb, mla_kv_norm, mla_w_kvb, t5_table,
              ssm_conv_w, ssm_conv_b, ssm_A_log, ssm_dt_bias, ssm_D, ssm_norm_g, na_rpb,
              w_branch, w_out, norm2_g, peer_wq, peer_keys, peer_u16, peer_vt16, final_g)
    y_prompt = encoder(x_prompt, *shared)
    y_sample = encoder(x_sample, *shared)
    return (y_prompt, y_sample)
```

```python
import functools
import math

import numpy as np
import jax
import jax.numpy as jnp
from jax import lax
from jax.experimental import pallas as pl
from jax.experimental.pallas import tpu as pltpu

F32 = jnp.float32
BF16 = jnp.bfloat16
MXU_DTYPE = BF16

D_MODEL = 1024
DEPTH = 2
GRID_W = 64
EPS = 1e-6
N_BRANCH = 4

MLA_HEADS = 4
MLA_Q_RANK = 256
MLA_KV_RANK = 128
MLA_NOPE = 64
MLA_ROPE = 32
MLA_V = 64
ROPE_THETA = 10000.0
Q_BLOCK = 128

DIL_GROUPS = ((128, 1), (512, 4), (2048, 16))
DIL_HEADS = 4
DIL_HD = 64
T5_BUCKETS = 32
T5_MAX_DIST = 1024

SSM_HEADS = 8
SSM_HD = 64
SSM_INNER = SSM_HEADS * SSM_HD
SSM_GROUPS = 2
SSM_STATE = 128
SSM_CONV = 7
SSM_CHUNK = 128
CONV_CH = SSM_INNER + 2 * SSM_GROUPS * SSM_STATE

NA_HEADS = 4
NA_HD = 64
NA_ROWS = 8
NA_COLS = 16
NA_QCB = 16
NA_KCB = NA_QCB + NA_COLS

PEER_HEADS = 8
PEER_KEYS = 128
PEER_EXPERTS = PEER_KEYS * PEER_KEYS
PEER_QDIM = 256
PEER_TOPK = 16
PEER_TOK_BLOCK = 128

BRANCH_WIDTHS = (MLA_HEADS * MLA_V, DIL_HEADS * DIL_HD, SSM_INNER, NA_HEADS * NA_HD)
BRANCH_ROWS = tuple(sum(BRANCH_WIDTHS[:i]) for i in range(N_BRANCH + 1))
IN_SIZES = (N_BRANCH * D_MODEL, MLA_Q_RANK, MLA_KV_RANK, MLA_ROPE,
            len(DIL_GROUPS) * 3 * DIL_HEADS * DIL_HD,
            SSM_INNER, CONV_CH, 2 * SSM_HEADS,
            3 * NA_HEADS * NA_HD)
IN_SPLITS = tuple(sum(IN_SIZES[:i + 1]) for i in range(len(IN_SIZES) - 1))

VMEM_LIMIT_BYTES = 56 * 1024 * 1024
LANES = 128
SUBLANES = 8


def rms_norm(x, g):
    x32 = x.astype(F32)
    y = x32 * lax.rsqrt(jnp.mean(x32 * x32, axis=-1, keepdims=True) + EPS)
    return (y * g.astype(F32)).astype(x.dtype)


PEER_ROUTE_TB = 256
PEER_TB = 256
PEER_EC = 1024
PEER_GATE_ROWS = 32
PEER_HALF = PEER_QDIM // 2
PEER_CAND_ROWS = 2 * SUBLANES + 7 * SUBLANES + SUBLANES


def _gelu_exact(x):
    return 0.5 * x * (1.0 + lax.erf(x * np.float32(math.sqrt(0.5))))


def _extract_desc(vals, n_out, out_ref, row0):
    for k in range(n_out):
        m = jnp.max(vals, axis=0, keepdims=True)
        out_ref[pl.ds(row0 + k, 1), :] = m
        vals = jnp.where(vals == m, -jnp.inf, vals)


def _peer_route_kernel(x_ref, g_ref, wqt_ref, keys_ref,
                       xnt_ref, s1_ref, s2_ref, e1_ref, e2_ref, tau_ref,
                       qt_scr, top_scr, cand_scr, tops_scr):
    x = x_ref[...]
    xn = x * lax.rsqrt(jnp.mean(x * x, axis=-1, keepdims=True) + EPS) * g_ref[...]
    xnt = xn.T.astype(MXU_DTYPE)
    xnt_ref[...] = xnt
    qt_scr[...] = jnp.dot(wqt_ref[...], xnt, preferred_element_type=F32).astype(MXU_DTYPE)

    def head(h, carry):
        q1 = qt_scr[pl.ds(pl.multiple_of(h * PEER_QDIM, PEER_QDIM), PEER_HALF), :]
        q2 = qt_scr[pl.ds(pl.multiple_of(h * PEER_QDIM + PEER_HALF, PEER_HALF), PEER_HALF), :]
        s1 = jnp.dot(keys_ref[2 * h], q1, preferred_element_type=F32)
        s2 = jnp.dot(keys_ref[2 * h + 1], q2, preferred_element_type=F32)
        _extract_desc(s1, PEER_TOPK, top_scr, 0)
        _extract_desc(s2, PEER_TOPK, top_scr, PEER_TOPK)
        t1 = top_scr[0:PEER_TOPK, :]
        t2 = top_scr[PEER_TOPK:2 * PEER_TOPK, :]
        cand_scr[0:2 * SUBLANES, :] = t1[0:1, :] + t2
        for a in range(1, SUBLANES):
            cand_scr[(a + 1) * SUBLANES:(a + 2) * SUBLANES, :] = t1[a:a + 1, :] + t2[0:SUBLANES, :]
        cand_scr[9 * SUBLANES:10 * SUBLANES, :] = t1[SUBLANES:2 * SUBLANES, :] + t2[0:1, :]
        _extract_desc(cand_scr[...], PEER_TOPK, tops_scr, 0)
        top_s = tops_scr[...]
        z = jnp.sum(jnp.exp(top_s - top_s[0:1, :]), axis=0, keepdims=True)
        tau_ref[pl.ds(h, 1), :] = top_s[PEER_TOPK - 1:PEER_TOPK, :]
        s1_ref[h] = s1
        s2_ref[h] = s2
        e1_ref[h] = jnp.exp(s1 - t1[0:1, :]) / z
        e2_ref[h] = jnp.exp(s2 - t2[0:1, :])
        return carry

    lax.fori_loop(0, PEER_HEADS, head, 0)


def _peer_route(x2d, g, wqt, keys):
    n_tok = x2d.shape[0]
    tb = PEER_ROUTE_TB
    rt_shape = jax.ShapeDtypeStruct((PEER_HEADS, PEER_KEYS, n_tok), F32)
    rt_spec = pl.BlockSpec((PEER_HEADS, PEER_KEYS, tb), lambda i: (0, 0, i))
    return pl.pallas_call(
        _peer_route_kernel,
        grid=(n_tok // tb,),
        in_specs=[
            pl.BlockSpec((tb, D_MODEL), lambda i: (i, 0)),
            pl.BlockSpec((1, D_MODEL), lambda i: (0, 0)),
            pl.BlockSpec((PEER_HEADS * PEER_QDIM, D_MODEL), lambda i: (0, 0)),
            pl.BlockSpec((2 * PEER_HEADS, PEER_KEYS, PEER_HALF), lambda i: (0, 0, 0)),
        ],
        out_specs=[
            pl.BlockSpec((D_MODEL, tb), lambda i: (0, i)),
            rt_spec, rt_spec, rt_spec, rt_spec,
            pl.BlockSpec((PEER_HEADS, tb), lambda i: (0, i)),
        ],
        out_shape=[
            jax.ShapeDtypeStruct((D_MODEL, n_tok), MXU_DTYPE),
            rt_shape, rt_shape, rt_shape, rt_shape,
            jax.ShapeDtypeStruct((PEER_HEADS, n_tok), F32),
        ],
        scratch_shapes=[
            pltpu.VMEM((PEER_HEADS * PEER_QDIM, tb), MXU_DTYPE),
            pltpu.VMEM((2 * PEER_TOPK, tb), F32),
            pltpu.VMEM((PEER_CAND_ROWS, tb), F32),
            pltpu.VMEM((PEER_TOPK, tb), F32),
        ],
        compiler_params=pltpu.CompilerParams(
            dimension_semantics=("arbitrary",), vmem_limit_bytes=VMEM_LIMIT_BYTES),
        name="peer_route",
    )(x2d, g, wqt, keys)


def _peer_expert_kernel(x_ref, xnt_ref, s1_ref, s2_ref, e1_ref, e2_ref, tau_ref, u_ref, vt_ref,
                        out_ref, acc_scr, w_scr):
    c = pl.program_id(1)
    n_chunks = pl.num_programs(1) - 1
    n_i1 = PEER_EC // PEER_KEYS
    tb = acc_scr.shape[1]

    @pl.when(c == 0)
    def _():
        acc_scr[...] = jnp.zeros_like(acc_scr)
        w_scr[1] = jnp.zeros(w_scr.shape[1:], w_scr.dtype)

    acc_scr[...] += jnp.dot(vt_ref[...], w_scr[(c + 1) % 2], preferred_element_type=F32)
    hid = jnp.dot(u_ref[...], xnt_ref[...], preferred_element_type=F32)
    slot = c % 2
    n_groups = PEER_KEYS // SUBLANES
    for i1l in range(n_i1):
        for l0 in range(0, tb, LANES):
            lanes = slice(l0, l0 + LANES)
            gates = [jnp.zeros((SUBLANES, LANES), F32) for _ in range(n_groups)]
            for h in range(PEER_HEADS):
                s1b = jnp.broadcast_to(s1_ref[h, i1l:i1l + 1, lanes], (SUBLANES, LANES))
                e1b = jnp.broadcast_to(e1_ref[h, i1l:i1l + 1, lanes], (SUBLANES, LANES))
                taub = jnp.broadcast_to(tau_ref[h:h + 1, lanes], (SUBLANES, LANES))
                for k in range(n_groups):
                    rows = slice(k * SUBLANES, (k + 1) * SUBLANES)
                    s = s2_ref[h, rows, lanes] + s1b
                    w = e2_ref[h, rows, lanes] * e1b
                    gates[k] = gates[k] + jnp.where(s >= taub, w, 0.0)
            for k in range(0, n_groups, 2):
                e0 = i1l * PEER_KEYS + k * SUBLANES
                act = _gelu_exact(hid[e0:e0 + 2 * SUBLANES, lanes])
                gate = jnp.concatenate([gates[k], gates[k + 1]], axis=0)
                w_scr[slot, e0:e0 + 2 * SUBLANES, lanes] = (act * gate).astype(MXU_DTYPE)

    @pl.when(c == n_chunks)
    def _():
        out_ref[...] = x_ref[...] + acc_scr[...].T


def _peer_experts(x2d, xnt, s1, s2, e1, e2, tau, u, vt):
    n_tok = x2d.shape[0]
    tb, ec = PEER_TB, PEER_EC
    n_chunks = PEER_EXPERTS // ec
    rt_spec = pl.BlockSpec((PEER_HEADS, PEER_KEYS, tb), lambda j, c: (0, 0, j))
    row_spec = pl.BlockSpec((PEER_HEADS, ec // PEER_KEYS, tb),
                            lambda j, c: (0, jnp.minimum(c, n_chunks - 1), j))
    return pl.pallas_call(
        _peer_expert_kernel,
        grid=(n_tok // tb, n_chunks + 1),
        in_specs=[
            pl.BlockSpec((tb, D_MODEL), lambda j, c: (j, 0)),
            pl.BlockSpec((D_MODEL, tb), lambda j, c: (0, j)),
            row_spec, rt_spec, row_spec, rt_spec,
            pl.BlockSpec((PEER_HEADS, tb), lambda j, c: (0, j)),
            pl.BlockSpec((ec, D_MODEL), lambda j, c: (jnp.minimum(c, n_chunks - 1), 0)),
            pl.BlockSpec((D_MODEL, ec), lambda j, c: (0, jnp.maximum(c - 1, 0))),
        ],
        out_specs=pl.BlockSpec((tb, D_MODEL), lambda j, c: (j, 0)),
        out_shape=jax.ShapeDtypeStruct((n_tok, D_MODEL), F32),
        scratch_shapes=[
            pltpu.VMEM((D_MODEL, tb), F32),
            pltpu.VMEM((2, ec, tb), MXU_DTYPE),
        ],
        compiler_params=pltpu.CompilerParams(
            dimension_semantics=("arbitrary", "arbitrary"), vmem_limit_bytes=VMEM_LIMIT_BYTES),
        name="peer_experts",
    )(x2d, xnt, s1, s2, e1, e2, tau, u, vt)


def peer_block(x2d, norm_g, w_q, keys, u, v):
    wqt = w_q.T.astype(MXU_DTYPE)
    keys2 = keys.reshape(2 * PEER_HEADS, PEER_KEYS, PEER_HALF).astype(MXU_DTYPE)
    xnt, s1, s2, e1, e2, tau = _peer_route(x2d, norm_g.reshape(1, D_MODEL), wqt, keys2)
    return _peer_experts(x2d, xnt, s1, s2, e1, e2, tau, u, v)


MLA_HG = LANES
MLA_QK_W = MLA_HEADS * MLA_HG
MLA_V_W = MLA_HEADS * MLA_V
MLA_PREP_TOK = 512
MLA_TQ = 512
MLA_TK = 512
MLA_RHALF = MLA_ROPE // 2


def _mla_rope_tables(s):
    inv = ROPE_THETA ** (-jnp.arange(MLA_RHALF, dtype=F32) / MLA_RHALF)
    ang = jnp.arange(s).astype(F32)[:, None] * inv[None, :]
    cos, sin = jnp.cos(ang), jnp.sin(ang)
    zero_pad = jnp.zeros((s, MLA_HG - MLA_NOPE - MLA_ROPE), F32)
    cos_rot = jnp.concatenate([cos, cos, zero_pad], axis=1)
    sin_rot = jnp.concatenate([-sin, sin, zero_pad], axis=1)
    scale = np.float32((MLA_NOPE + MLA_ROPE) ** -0.5)
    q_cos = scale * jnp.concatenate([jnp.ones((s, MLA_NOPE), F32), cos_rot], axis=1)
    q_sin = scale * jnp.concatenate([jnp.zeros((s, MLA_NOPE), F32), sin_rot], axis=1)
    k_cos = jnp.concatenate([jnp.zeros((s, MLA_NOPE), F32), cos_rot], axis=1)
    k_sin = jnp.concatenate([jnp.zeros((s, MLA_NOPE), F32), sin_rot], axis=1)
    return q_cos, q_sin, k_cos, k_sin


def _mla_pack_weights(w_qb, w_kvb):
    hd_q = MLA_NOPE + MLA_ROPE
    wq = w_qb.reshape(MLA_Q_RANK, MLA_HEADS, hd_q)
    rot = wq[:, :, MLA_NOPE:]
    rot_sw = jnp.concatenate([rot[:, :, MLA_RHALF:], rot[:, :, :MLA_RHALF]], axis=2)
    pad = jnp.zeros((MLA_Q_RANK, MLA_HEADS, MLA_HG - hd_q), F32)
    wq_a = jnp.concatenate([wq, pad], axis=2).reshape(MLA_Q_RANK, MLA_QK_W)
    wq_b = jnp.concatenate([jnp.zeros_like(wq[:, :, :MLA_NOPE]), rot_sw, pad], axis=2).reshape(MLA_Q_RANK, MLA_QK_W)
    wkv = w_kvb.reshape(MLA_KV_RANK, MLA_HEADS, MLA_NOPE + MLA_V)
    wk = jnp.concatenate([wkv[:, :, :MLA_NOPE], jnp.zeros((MLA_KV_RANK, MLA_HEADS, MLA_HG - MLA_NOPE), F32)],
                         axis=2).reshape(MLA_KV_RANK, MLA_QK_W)
    wv = jnp.concatenate([wkv[:, :, MLA_NOPE:], jnp.zeros((MLA_KV_RANK, MLA_HEADS, MLA_HG - MLA_V), F32)],
                         axis=2).reshape(MLA_KV_RANK, MLA_QK_W)
    return (wq_a.astype(MXU_DTYPE), wq_b.astype(MXU_DTYPE), wk.astype(MXU_DTYPE), wv.astype(MXU_DTYPE))


def _mla_prep_kernel(cq_ref, ckv_ref, kr_ref, qn_ref, kvn_ref, wqa_ref, wqb_ref, wk_ref, wv_ref,
                     qcos_ref, qsin_ref, kcos_ref, ksin_ref, q_out, kt_out, v_out):
    cq = cq_ref[0]
    cqn = (cq * lax.rsqrt(jnp.mean(cq * cq, axis=-1, keepdims=True) + EPS) * qn_ref[...]).astype(MXU_DTYPE)
    qa = jnp.dot(cqn, wqa_ref[...], preferred_element_type=F32)
    qb = jnp.dot(cqn, wqb_ref[...], preferred_element_type=F32)
    ckv = ckv_ref[0]
    ckvn = (ckv * lax.rsqrt(jnp.mean(ckv * ckv, axis=-1, keepdims=True) + EPS) * kvn_ref[...]).astype(MXU_DTYPE)
    ka = jnp.dot(ckvn, wk_ref[...], preferred_element_type=F32)
    one_lane = (lax.broadcasted_iota(jnp.int32, (1, MLA_QK_W), 1) % MLA_HG == MLA_V).astype(F32)
    v_out[0] = (jnp.dot(ckvn, wv_ref[...], preferred_element_type=F32) + one_lane).astype(MXU_DTYPE)
    kr = kr_ref[0]
    k_rot = kr[:, 0:MLA_HG] * kcos_ref[...] + kr[:, MLA_HG:2 * MLA_HG] * ksin_ref[...]
    for h in range(MLA_HEADS):
        lanes = slice(h * MLA_HG, (h + 1) * MLA_HG)
        q_out[0, :, lanes] = (qa[:, lanes] * qcos_ref[...] + qb[:, lanes] * qsin_ref[...]).astype(MXU_DTYPE)
        kt_out[0, lanes, :] = (ka[:, lanes] + k_rot).T.astype(MXU_DTYPE)


def _mla_flash_kernel(q_ref, kt_ref, v_ref, out_ref, m_scr, acc_scr):
    ki = pl.program_id(2)

    @pl.when(ki == 0)
    def _():
        m_scr[...] = jnp.full(m_scr.shape, -jnp.inf, F32)
        acc_scr[...] = jnp.zeros(acc_scr.shape, F32)

    def logits(h):
        lanes = slice(h * MLA_HG, (h + 1) * MLA_HG)
        return jnp.dot(q_ref[0, :, lanes], kt_ref[0, lanes, :], preferred_element_type=F32)

    n_rep = kt_ref.shape[2] // MLA_HG
    s_next = logits(0)
    for h in range(MLA_HEADS):
        s = s_next
        if h + 1 < MLA_HEADS:
            s_next = logits(h + 1)
        m_old = m_scr[h]
        m_new = jnp.maximum(m_old, jnp.max(s, axis=-1, keepdims=True))
        p = jnp.exp(s - jnp.tile(m_new, (1, n_rep))).astype(MXU_DTYPE)
        acc_scr[h] = jnp.exp(m_old - m_new) * acc_scr[h] + jnp.dot(
            p, v_ref[0, :, h * MLA_HG:(h + 1) * MLA_HG], preferred_element_type=F32)
        m_scr[h] = m_new

    @pl.when(ki == pl.num_programs(2) - 1)
    def _():
        low = lax.broadcasted_iota(jnp.int32, (acc_scr.shape[1], MLA_HG), 1) < MLA_V
        outs = []
        for h in range(MLA_HEADS):
            acc = acc_scr[h]
            outs.append(acc / acc[:, MLA_V:MLA_V + 1])
        for hp in range(MLA_HEADS // 2):
            odd = pltpu.roll(outs[2 * hp + 1], MLA_V, axis=1)
            out_ref[0, :, hp * MLA_HG:(hp + 1) * MLA_HG] = jnp.where(low, outs[2 * hp], odd)


def mla_mixer(c_q, c_kv, k_rope, q_norm, w_qb, kv_norm, w_kvb):
    b, s, _ = c_q.shape
    tt = MLA_PREP_TOK
    wqa, wqb, wk, wv = _mla_pack_weights(w_qb, w_kvb)
    q_cos, q_sin, k_cos, k_sin = _mla_rope_tables(s)
    zl = jnp.zeros((b, s, MLA_NOPE), F32)
    zr = jnp.zeros((b, s, MLA_HG - MLA_NOPE - MLA_ROPE), F32)
    kr_sw = jnp.concatenate([k_rope[..., MLA_RHALF:], k_rope[..., :MLA_RHALF]], axis=-1)
    kr2 = jnp.concatenate([zl, k_rope, zr, zl, kr_sw, zr], axis=-1)
    tok = lambda w: pl.BlockSpec((1, tt, w), lambda i, j: (i, j, 0))
    full = lambda r, c: pl.BlockSpec((r, c), lambda i, j: (0, 0))
    tab = pl.BlockSpec((tt, MLA_HG), lambda i, j: (j, 0))
    q, kt, v = pl.pallas_call(
        _mla_prep_kernel,
        grid=(b, s // tt),
        in_specs=[tok(MLA_Q_RANK), tok(MLA_KV_RANK), tok(2 * MLA_HG),
                  full(1, MLA_Q_RANK), full(1, MLA_KV_RANK),
                  full(MLA_Q_RANK, MLA_QK_W), full(MLA_Q_RANK, MLA_QK_W),
                  full(MLA_KV_RANK, MLA_QK_W), full(MLA_KV_RANK, MLA_QK_W),
                  tab, tab, tab, tab],
        out_specs=[tok(MLA_QK_W), pl.BlockSpec((1, MLA_QK_W, tt), lambda i, j: (i, 0, j)), tok(MLA_QK_W)],
        out_shape=[jax.ShapeDtypeStruct((b, s, MLA_QK_W), MXU_DTYPE),
                   jax.ShapeDtypeStruct((b, MLA_QK_W, s), MXU_DTYPE),
                   jax.ShapeDtypeStruct((b, s, MLA_QK_W), MXU_DTYPE)],
        compiler_params=pltpu.CompilerParams(
            dimension_semantics=("arbitrary", "arbitrary"), vmem_limit_bytes=VMEM_LIMIT_BYTES),
        name="mla_prep",
    )(c_q, c_kv, kr2, q_norm.reshape(1, -1), kv_norm.reshape(1, -1), wqa, wqb, wk, wv,
      q_cos, q_sin, k_cos, k_sin)
    tq, tk = MLA_TQ, MLA_TK
    return pl.pallas_call(
        _mla_flash_kernel,
        grid=(b, s // tq, s // tk),
        in_specs=[pl.BlockSpec((1, tq, MLA_QK_W), lambda i, j, kk: (i, j, 0)),
                  pl.BlockSpec((1, MLA_QK_W, tk), lambda i, j, kk: (i, 0, kk)),
                  pl.BlockSpec((1, tk, MLA_QK_W), lambda i, j, kk: (i, kk, 0))],
        out_specs=pl.BlockSpec((1, tq, MLA_V_W), lambda i, j, kk: (i, j, 0)),
        out_shape=jax.ShapeDtypeStruct((b, s, MLA_V_W), F32),
        scratch_shapes=[pltpu.VMEM((MLA_HEADS, tq, MLA_HG), F32),
                        pltpu.VMEM((MLA_HEADS, tq, MLA_HG), F32)],
        compiler_params=pltpu.CompilerParams(
            dimension_semantics=("arbitrary", "arbitrary", "arbitrary"), vmem_limit_bytes=VMEM_LIMIT_BYTES),
        name="mla_flash",
    )(q, kt, v)


def t5_bucket(rel):
    nb = T5_BUCKETS // 2
    ret = np.where(rel > 0, nb, 0)
    n = np.abs(rel)
    max_exact = nb // 2
    large = max_exact + (np.log(np.maximum(n, 1) / max_exact) / np.log(T5_MAX_DIST / max_exact)
                         * (nb - max_exact)).astype(np.int64)
    large = np.minimum(large, nb - 1)
    return (ret + np.where(n < max_exact, n, large)).astype(np.int32)


def dilated_group(q, k, v, dil, half, bias_tab):
    b, s, h, hd = q.shape
    L = s // dil
    nb = -(-L // half)
    Lp = nb * half
    bb = b * dil

    def to_res(t):
        t = t.reshape(b, L, dil, h, hd).transpose(0, 2, 1, 3, 4).reshape(bb, L, h, hd)
        return jnp.pad(t, ((0, 0), (0, Lp - L), (0, 0), (0, 0)))

    def band(t):
        tp = jnp.pad(t, ((0, 0), (half, half), (0, 0), (0, 0)))
        return jnp.concatenate([tp[:, i * half:i * half + Lp].reshape(bb, nb, half, h, hd)
                                for i in range(3)], axis=2)

    qb = to_res(q).reshape(bb, nb, half, h, hd)
    kb, vb = band(to_res(k)), band(to_res(v))
    rel = np.arange(3 * half)[None, :] - half - np.arange(half)[:, None]
    kpos = np.arange(nb)[:, None] * half - half + np.arange(3 * half)[None, :]
    valid = (np.abs(rel) <= half)[None] & ((kpos >= 0) & (kpos < L))[:, None, :]
    bias = bias_tab[t5_bucket(rel * dil)].transpose(2, 0, 1).astype(F32)
    logits = jnp.einsum('bnqhd,bnkhd->bnhqk', qb, kb).astype(F32) * hd ** -0.5 + bias[None, None]
    logits = jnp.where(valid[None, :, None], logits, -jnp.inf)
    lse = jax.nn.logsumexp(logits, axis=-1)
    p = jnp.exp(logits - lse[..., None])
    o = jnp.einsum('bnhqk,bnkhd->bnqhd', p.astype(v.dtype), vb)

    def from_res(t):
        tail = t.shape[2:]
        t = t[:, :L].reshape((b, dil, L) + tail)
        return jnp.swapaxes(t, 1, 2).reshape((b, s) + tail)

    o = from_res(o.reshape(bb, Lp, h, hd))
    lse = from_res(lse.transpose(0, 1, 3, 2).reshape(bb, Lp, h))
    return o, lse


def dilated_mixer(qkv, t5_table):
    b, s, _ = qkv.shape
    qkv = qkv.reshape(b, s, len(DIL_GROUPS), 3, DIL_HEADS, DIL_HD)
    outs, lses = [], []
    for gi, (win, dil) in enumerate(DIL_GROUPS):
        o, lse = dilated_group(qkv[:, :, gi, 0], qkv[:, :, gi, 1], qkv[:, :, gi, 2], dil, win // (2 * dil),
                               t5_table[:, gi * DIL_HEADS:(gi + 1) * DIL_HEADS])
        outs.append(o)
        lses.append(lse)
    w = jax.nn.softmax(jnp.stack(lses, axis=0), axis=0)
    o = jnp.einsum('gbsh,gbshd->bshd', w.astype(outs[0].dtype), jnp.stack(outs, axis=0))
    return o.reshape(b, s, DIL_HEADS * DIL_HD)


def depthwise_conv(x, w, bias):
    k = w.shape[0]
    y = lax.conv_general_dilated(x, w[:, None, :], window_strides=(1,), padding=[(k // 2, k // 2)],
                                 dimension_numbers=('NWC', 'WIO', 'NWC'), feature_group_count=x.shape[-1])
    return y + bias


def segsum(a):
    q = a.shape[-1]
    cs = jnp.cumsum(a, axis=-1)
    diff = cs[..., :, None] - cs[..., None, :]
    return jnp.where(np.tril(np.ones((q, q), dtype=bool)), diff, -jnp.inf)


def ssd_scan(x, dt, A, Bm, Cm):
    b, l, nh, p = x.shape
    g, n = Bm.shape[2], Bm.shape[3]
    r = nh // g
    c = l // SSM_CHUNK
    Q = SSM_CHUNK
    a = (dt * A).reshape(b, c, Q, g, r).transpose(0, 3, 4, 1, 2)
    dtx = (x.astype(F32) * dt[..., None]).reshape(b, c, Q, g, r, p)
    Bc = Bm.astype(F32).reshape(b, c, Q, g, n)
    Cc = Cm.astype(F32).reshape(b, c, Q, g, n)
    a_cs = jnp.cumsum(a, axis=-1)
    Lmat = jnp.exp(segsum(a))
    CB = jnp.einsum('bcign,bcjgn->bgcij', Cc, Bc)
    y_diag = jnp.einsum('bgcij,bgrcij,bcjgrp->bcigrp', CB, Lmat, dtx)
    decay_st = jnp.exp(a_cs[..., -1:] - a_cs)
    states = jnp.einsum('bcjgn,bgrcj,bcjgrp->bcgrpn', Bc, decay_st, dtx)
    chunk_decay = jnp.exp(a_cs[..., -1])

    def step(st, inp):
        new, dec = inp
        return st * dec[..., None, None] + new, st

    s0 = jnp.zeros((b, g, r, p, n), F32)
    _, s_in = lax.scan(step, s0, (jnp.moveaxis(states, 1, 0), jnp.moveaxis(chunk_decay, 3, 0)))
    y_off = jnp.einsum('bcign,cbgrpn,bgrci->bcigrp', Cc, s_in, jnp.exp(a_cs))
    return (y_diag + y_off).reshape(b, l, nh, p)


def mamba2_mixer(z, xbc, dt_raw, conv_w, conv_b, A_log, dt_bias, D_skip, norm_g):
    b, l, _ = z.shape
    xbc = jax.nn.silu(depthwise_conv(xbc, conv_w, conv_b))
    xs = xbc[..., :SSM_INNER].reshape(b, l, SSM_HEADS, SSM_HD)
    Bm = xbc[..., SSM_INNER:SSM_INNER + SSM_GROUPS * SSM_STATE].reshape(b, l, SSM_GROUPS, SSM_STATE)
    Cm = xbc[..., SSM_INNER + SSM_GROUPS * SSM_STATE:].reshape(b, l, SSM_GROUPS, SSM_STATE)
    dt = jax.nn.softplus(dt_raw.astype(F32).reshape(b, l, 2, SSM_HEADS) + dt_bias.astype(F32))
    A = -jnp.exp(A_log.astype(F32))
    flip = lambda t: jnp.flip(t, axis=1)
    y_f = ssd_scan(xs, dt[:, :, 0], A[0], Bm, Cm)
    y_b = flip(ssd_scan(flip(xs), flip(dt[:, :, 1]), A[1], flip(Bm), flip(Cm)))
    y = y_f + y_b + xs.astype(F32) * D_skip.astype(F32)[:, None]
    y = y.reshape(b, l, SSM_INNER).astype(z.dtype)
    return rms_norm(y * jax.nn.silu(z), norm_g)


NA_DIM = NA_HEADS * NA_HD
NA_ROWS_PER_STEP = 8
NA_WIN = NA_ROWS * GRID_W
NA_STEP_TOK = NA_ROWS_PER_STEP * GRID_W


def _na_bias_table(rpb):
    delta = np.arange(NA_ROWS)[:, None, None, None]
    w = np.arange(NA_ROWS)[None, None, :, None]
    qc = np.arange(GRID_W)[None, :, None, None]
    kc = np.arange(GRID_W)[None, None, None, :]
    cs = np.clip(qc - NA_COLS // 2, 0, GRID_W - NA_COLS)
    ok = (kc >= cs) & (kc < cs + NA_COLS)
    dr = np.broadcast_to(w - delta + NA_ROWS - 1, (NA_ROWS, GRID_W, NA_ROWS, GRID_W))
    dc = np.broadcast_to(np.clip(kc - qc + NA_COLS - 1, 0, 2 * NA_COLS - 2), dr.shape)
    ok = np.broadcast_to(ok, dr.shape)
    bias = rpb.astype(F32)[:, dr, dc]
    bias = jnp.where(ok[None], bias, -jnp.inf)
    return bias.transpose(1, 0, 2, 3, 4).reshape(NA_ROWS, NA_HEADS * GRID_W, NA_WIN)


def _na_kernel(prev_ref, cur_ref, next_ref, tab_ref, out_ref, k_scr, v_scr, *, n_rows):
    step = pl.program_id(1)
    for i, ref in enumerate((prev_ref, cur_ref, next_ref)):
        k_scr[i * NA_STEP_TOK:(i + 1) * NA_STEP_TOK, :] = ref[0, :, NA_DIM:2 * NA_DIM].astype(MXU_DTYPE)
        v_scr[i * NA_STEP_TOK:(i + 1) * NA_STEP_TOK, :] = ref[0, :, 2 * NA_DIM:3 * NA_DIM].astype(MXU_DTYPE)
    lane_head = lax.broadcasted_iota(jnp.int32, (GRID_W, NA_DIM), 1) // NA_HD
    row0 = step * NA_ROWS_PER_STEP
    for j in range(NA_ROWS_PER_STEP):
        r = row0 + j
        r0 = jnp.clip(r - NA_ROWS // 2, 0, n_rows - NA_ROWS)
        start = pl.multiple_of((r0 - row0 + NA_ROWS_PER_STEP) * GRID_W, GRID_W)
        q = cur_ref[0, j * GRID_W:(j + 1) * GRID_W, 0:NA_DIM] * np.float32(NA_HD ** -0.5)
        qs = jnp.concatenate([jnp.where(lane_head == h, q, 0.0) for h in range(NA_HEADS)], axis=0)
        kw = k_scr[pl.ds(start, NA_WIN), :]
        vw = v_scr[pl.ds(start, NA_WIN), :]
        logits = lax.dot_general(qs.astype(MXU_DTYPE), kw, (((1,), (1,)), ((), ())),
                                 preferred_element_type=F32) + tab_ref[r - r0]
        m = jnp.max(logits, axis=-1, keepdims=True)
        p = jnp.exp(logits - m)
        denom = jnp.sum(p, axis=-1, keepdims=True)
        o_all = jnp.dot(p.astype(MXU_DTYPE), vw, preferred_element_type=F32) / denom
        o = jnp.zeros((GRID_W, NA_DIM), F32)
        for h in range(NA_HEADS):
            o = o + jnp.where(lane_head == h, o_all[h * GRID_W:(h + 1) * GRID_W, :], 0.0)
        out_ref[0, j * GRID_W:(j + 1) * GRID_W, :] = o


def na_mixer(qkv, rpb):
    b, s, _ = qkv.shape
    n_rows = s // GRID_W
    assert n_rows >= NA_ROWS and n_rows % NA_ROWS_PER_STEP == 0
    n_steps = n_rows // NA_ROWS_PER_STEP
    blk = (1, NA_STEP_TOK, 3 * NA_DIM)
    return pl.pallas_call(
        functools.partial(_na_kernel, n_rows=n_rows),
        grid=(b, n_steps),
        in_specs=[
            pl.BlockSpec(blk, lambda i, j: (i, jnp.maximum(j - 1, 0), 0)),
            pl.BlockSpec(blk, lambda i, j: (i, j, 0)),
            pl.BlockSpec(blk, lambda i, j: (i, jnp.minimum(j + 1, n_steps - 1), 0)),
            pl.BlockSpec((NA_ROWS, NA_HEADS * GRID_W, NA_WIN), lambda i, j: (0, 0, 0)),
        ],
        out_specs=pl.BlockSpec((1, NA_STEP_TOK, NA_DIM), lambda i, j: (i, j, 0)),
        out_shape=jax.ShapeDtypeStruct((b, s, NA_DIM), F32),
        scratch_shapes=[
            pltpu.VMEM((3 * NA_STEP_TOK, NA_DIM), MXU_DTYPE),
            pltpu.VMEM((3 * NA_STEP_TOK, NA_DIM), MXU_DTYPE),
        ],
        compiler_params=pltpu.CompilerParams(
            dimension_semantics=("arbitrary", "arbitrary"), vmem_limit_bytes=VMEM_LIMIT_BYTES),
        name="na_attention",
    )(qkv, qkv, qkv, _na_bias_table(rpb))


def encoder(x, norm1_g, w_in, b_gate, mla_q_norm, mla_w_qb, mla_kv_norm, mla_w_kvb, t5_table,
            ssm_conv_w, ssm_conv_b, ssm_A_log, ssm_dt_bias, ssm_D, ssm_norm_g, na_rpb,
            w_branch, w_out, norm2_g, peer_wq, peer_keys, peer_u, peer_vt, final_g):
    b, s, _ = x.shape
    for l in range(DEPTH):
        h = rms_norm(x, norm1_g[l])
        gate, a_cq, a_ckv, a_kr, b_qkv, c_z, c_xbc, c_dt, d_qkv = jnp.split(h @ w_in[l], IN_SPLITS, axis=-1)
        y_a = mla_mixer(a_cq, a_ckv, a_kr, mla_q_norm[l], mla_w_qb[l], mla_kv_norm[l], mla_w_kvb[l])
        y_b = dilated_mixer(b_qkv, t5_table)
        y_c = mamba2_mixer(c_z, c_xbc, c_dt, ssm_conv_w[l], ssm_conv_b[l], ssm_A_log[l], ssm_dt_bias[l],
                           ssm_D[l], ssm_norm_g[l])
        y_d = na_mixer(d_qkv, na_rpb[l])
        gates = jax.nn.sigmoid((gate + b_gate[l]).astype(F32)).astype(x.dtype).reshape(b, s, N_BRANCH, D_MODEL)
        merged = jnp.zeros_like(x)
        for i, y_i in enumerate((y_a, y_b, y_c, y_d)):
            proj = y_i.astype(x.dtype) @ w_branch[l, BRANCH_ROWS[i]:BRANCH_ROWS[i + 1]]
            merged = merged + gates[:, :, i] * proj
        x = x + merged @ w_out[l]
        x = peer_block(x.reshape(b * s, D_MODEL), norm2_g[l], peer_wq[l], peer_keys[l],
                       peer_u[l], peer_vt[l]).reshape(b, s, D_MODEL)
    return rms_norm(x, final_g)


def kernel(x_prompt, x_sample, norm1_g, w_in, b_gate, mla_q_norm, mla_w_qb, mla_kv_norm, mla_w_kvb, t5_table, ssm_conv_w, ssm_conv_b, ssm_A_log, ssm_dt_bias, ssm_D, ssm_norm_g, na_rpb, w_branch, w_out, norm2_g, peer_wq, peer_keys, peer_u, peer_v, final_g):
    peer_u16 = peer_u.astype(MXU_DTYPE)
    peer_vt16 = jnp.swapaxes(peer_v, 1, 2).astype(MXU_DTYPE)
    shared = (norm1_g, w_in, b_gate, mla_q_norm, mla_w_qb, mla_kv_norm, mla_w_kvb, t5_table,
              ssm_conv_w, ssm_conv_b, ssm_A_log, ssm_dt_bias, ssm_D, ssm_norm_g, na_rpb,
              w_branch, w_out, norm2_g, peer_wq, peer_keys, peer_u16, peer_vt16, final_g)
    y_prompt = encoder(x_prompt, *shared)
    y_sample = encoder(x_sample, *shared)
    return (y_prompt, y_sample)
```

```python
import functools
import math

import numpy as np
import jax
import jax.numpy as jnp
from jax import lax
from jax.experimental import pallas as pl
from jax.experimental.pallas import tpu as pltpu

F32 = jnp.float32
BF16 = jnp.bfloat16
MXU_DTYPE = BF16

D_MODEL = 1024
DEPTH = 2
GRID_W = 64
EPS = 1e-6
N_BRANCH = 4

MLA_HEADS = 4
MLA_Q_RANK = 256
MLA_KV_RANK = 128
MLA_NOPE = 64
MLA_ROPE = 32
MLA_V = 64
ROPE_THETA = 10000.0
Q_BLOCK = 128

DIL_GROUPS = ((128, 1), (512, 4), (2048, 16))
DIL_HEADS = 4
DIL_HD = 64
T5_BUCKETS = 32
T5_MAX_DIST = 1024

SSM_HEADS = 8
SSM_HD = 64
SSM_INNER = SSM_HEADS * SSM_HD
SSM_GROUPS = 2
SSM_STATE = 128
SSM_CONV = 7
SSM_CHUNK = 128
CONV_CH = SSM_INNER + 2 * SSM_GROUPS * SSM_STATE

NA_HEADS = 4
NA_HD = 64
NA_ROWS = 8
NA_COLS = 16
NA_QCB = 16
NA_KCB = NA_QCB + NA_COLS

PEER_HEADS = 8
PEER_KEYS = 128
PEER_EXPERTS = PEER_KEYS * PEER_KEYS
PEER_QDIM = 256
PEER_TOPK = 16
PEER_TOK_BLOCK = 128

BRANCH_WIDTHS = (MLA_HEADS * MLA_V, DIL_HEADS * DIL_HD, SSM_INNER, NA_HEADS * NA_HD)
BRANCH_ROWS = tuple(sum(BRANCH_WIDTHS[:i]) for i in range(N_BRANCH + 1))
IN_SIZES = (N_BRANCH * D_MODEL, MLA_Q_RANK, MLA_KV_RANK, MLA_ROPE,
            len(DIL_GROUPS) * 3 * DIL_HEADS * DIL_HD,
            SSM_INNER, CONV_CH, 2 * SSM_HEADS,
            3 * NA_HEADS * NA_HD)
IN_SPLITS = tuple(sum(IN_SIZES[:i + 1]) for i in range(len(IN_SIZES) - 1))

VMEM_LIMIT_BYTES = 56 * 1024 * 1024
LANES = 128
SUBLANES = 8


def rms_norm(x, g):
    x32 = x.astype(F32)
    y = x32 * lax.rsqrt(jnp.mean(x32 * x32, axis=-1, keepdims=True) + EPS)
    return (y * g.astype(F32)).astype(x.dtype)


PEER_ROUTE_TB = 256
PEER_TB = 256
PEER_EC = 1024
PEER_GATE_ROWS = 32
PEER_HALF = PEER_QDIM // 2
PEER_CAND_ROWS = 2 * SUBLANES + 7 * SUBLANES + SUBLANES


def _gelu_exact(x):
    return 0.5 * x * (1.0 + lax.erf(x * np.float32(math.sqrt(0.5))))


def _extract_desc(vals, n_out, out_ref, row0):
    for k in range(n_out):
        m = jnp.max(vals, axis=0, keepdims=True)
        out_ref[pl.ds(row0 + k, 1), :] = m
        vals = jnp.where(vals == m, -jnp.inf, vals)


def _peer_route_kernel(x_ref, g_ref, wqt_ref, keys_ref,
                       xnt_ref, s1_ref, s2_ref, e1_ref, e2_ref, tau_ref,
                       qt_scr, top_scr, cand_scr, tops_scr):
    x = x_ref[...]
    xn = x * lax.rsqrt(jnp.mean(x * x, axis=-1, keepdims=True) + EPS) * g_ref[...]
    xnt = xn.T.astype(MXU_DTYPE)
    xnt_ref[...] = xnt
    qt_scr[...] = jnp.dot(wqt_ref[...], xnt, preferred_element_type=F32).astype(MXU_DTYPE)

    def head(h, carry):
        q1 = qt_scr[pl.ds(pl.multiple_of(h * PEER_QDIM, PEER_QDIM), PEER_HALF), :]
        q2 = qt_scr[pl.ds(pl.multiple_of(h * PEER_QDIM + PEER_HALF, PEER_HALF), PEER_HALF), :]
        s1 = jnp.dot(keys_ref[2 * h], q1, preferred_element_type=F32)
        s2 = jnp.dot(keys_ref[2 * h + 1], q2, preferred_element_type=F32)
        _extract_desc(s1, PEER_TOPK, top_scr, 0)
        _extract_desc(s2, PEER_TOPK, top_scr, PEER_TOPK)
        t1 = top_scr[0:PEER_TOPK, :]
        t2 = top_scr[PEER_TOPK:2 * PEER_TOPK, :]
        cand_scr[0:2 * SUBLANES, :] = t1[0:1, :] + t2
        for a in range(1, SUBLANES):
            cand_scr[(a + 1) * SUBLANES:(a + 2) * SUBLANES, :] = t1[a:a + 1, :] + t2[0:SUBLANES, :]
        cand_scr[9 * SUBLANES:10 * SUBLANES, :] = t1[SUBLANES:2 * SUBLANES, :] + t2[0:1, :]
        _extract_desc(cand_scr[...], PEER_TOPK, tops_scr, 0)
        top_s = tops_scr[...]
        z = jnp.sum(jnp.exp(top_s - top_s[0:1, :]), axis=0, keepdims=True)
        tau_ref[pl.ds(h, 1), :] = top_s[PEER_TOPK - 1:PEER_TOPK, :]
        s1_ref[h] = s1
        s2_ref[h] = s2
        e1_ref[h] = jnp.exp(s1 - t1[0:1, :]) / z
        e2_ref[h] = jnp.exp(s2 - t2[0:1, :])
        return carry

    lax.fori_loop(0, PEER_HEADS, head, 0)


def _peer_route(x2d, g, wqt, keys):
    n_tok = x2d.shape[0]
    tb = PEER_ROUTE_TB
    rt_shape = jax.ShapeDtypeStruct((PEER_HEADS, PEER_KEYS, n_tok), F32)
    rt_spec = pl.BlockSpec((PEER_HEADS, PEER_KEYS, tb), lambda i: (0, 0, i))
    return pl.pallas_call(
        _peer_route_kernel,
        grid=(n_tok // tb,),
        in_specs=[
            pl.BlockSpec((tb, D_MODEL), lambda i: (i, 0)),
            pl.BlockSpec((1, D_MODEL), lambda i: (0, 0)),
            pl.BlockSpec((PEER_HEADS * PEER_QDIM, D_MODEL), lambda i: (0, 0)),
            pl.BlockSpec((2 * PEER_HEADS, PEER_KEYS, PEER_HALF), lambda i: (0, 0, 0)),
        ],
        out_specs=[
            pl.BlockSpec((D_MODEL, tb), lambda i: (0, i)),
            rt_spec, rt_spec, rt_spec, rt_spec,
            pl.BlockSpec((PEER_HEADS, tb), lambda i: (0, i)),
        ],
        out_shape=[
            jax.ShapeDtypeStruct((D_MODEL, n_tok), MXU_DTYPE),
            rt_shape, rt_shape, rt_shape, rt_shape,
            jax.ShapeDtypeStruct((PEER_HEADS, n_tok), F32),
        ],
        scratch_shapes=[
            pltpu.VMEM((PEER_HEADS * PEER_QDIM, tb), MXU_DTYPE),
            pltpu.VMEM((2 * PEER_TOPK, tb), F32),
            pltpu.VMEM((PEER_CAND_ROWS, tb), F32),
            pltpu.VMEM((PEER_TOPK, tb), F32),
        ],
        compiler_params=pltpu.CompilerParams(
            dimension_semantics=("arbitrary",), vmem_limit_bytes=VMEM_LIMIT_BYTES),
        name="peer_route",
    )(x2d, g, wqt, keys)


def _peer_expert_kernel(x_ref, xnt_ref, s1_ref, s2_ref, e1_ref, e2_ref, tau_ref, u_ref, vt_ref, fg_ref,
                        out_ref, acc_scr, w_scr, *, final_norm):
    c = pl.program_id(1)
    n_chunks = pl.num_programs(1) - 1
    n_i1 = PEER_EC // PEER_KEYS
    tb = acc_scr.shape[1]

    @pl.when(c == 0)
    def _():
        acc_scr[...] = jnp.zeros_like(acc_scr)
        w_scr[1] = jnp.zeros(w_scr.shape[1:], w_scr.dtype)

    acc_scr[...] += jnp.dot(vt_ref[...], w_scr[(c + 1) % 2], preferred_element_type=F32)
    hid = jnp.dot(u_ref[...], xnt_ref[...], preferred_element_type=F32)
    slot = c % 2
    n_groups = PEER_KEYS // SUBLANES
    for i1l in range(n_i1):
        for l0 in range(0, tb, LANES):
            lanes = slice(l0, l0 + LANES)
            gates = [jnp.zeros((SUBLANES, LANES), F32) for _ in range(n_groups)]
            for h in range(PEER_HEADS):
                s1b = jnp.broadcast_to(s1_ref[h, i1l:i1l + 1, lanes], (SUBLANES, LANES))
                e1b = jnp.broadcast_to(e1_ref[h, i1l:i1l + 1, lanes], (SUBLANES, LANES))
                taub = jnp.broadcast_to(tau_ref[h:h + 1, lanes], (SUBLANES, LANES))
                for k in range(n_groups):
                    rows = slice(k * SUBLANES, (k + 1) * SUBLANES)
                    s = s2_ref[h, rows, lanes] + s1b
                    w = e2_ref[h, rows, lanes] * e1b
                    gates[k] = gates[k] + jnp.where(s >= taub, w, 0.0)
            for k in range(0, n_groups, 2):
                e0 = i1l * PEER_KEYS + k * SUBLANES
                act = _gelu_exact(hid[e0:e0 + 2 * SUBLANES, lanes])
                gate = jnp.concatenate([gates[k], gates[k + 1]], axis=0)
                w_scr[slot, e0:e0 + 2 * SUBLANES, lanes] = (act * gate).astype(MXU_DTYPE)

    @pl.when(c == n_chunks)
    def _():
        y = x_ref[...] + acc_scr[...].T
        if final_norm:
            y = y * lax.rsqrt(jnp.mean(y * y, axis=-1, keepdims=True) + EPS) * fg_ref[...]
        out_ref[...] = y


def _peer_experts(x2d, xnt, s1, s2, e1, e2, tau, u, vt, final_g, final_norm):
    n_tok = x2d.shape[0]
    tb, ec = PEER_TB, PEER_EC
    n_chunks = PEER_EXPERTS // ec
    rt_spec = pl.BlockSpec((PEER_HEADS, PEER_KEYS, tb), lambda j, c: (0, 0, j))
    row_spec = pl.BlockSpec((PEER_HEADS, ec // PEER_KEYS, tb),
                            lambda j, c: (0, jnp.minimum(c, n_chunks - 1), j))
    return pl.pallas_call(
        functools.partial(_peer_expert_kernel, final_norm=final_norm),
        grid=(n_tok // tb, n_chunks + 1),
        in_specs=[
            pl.BlockSpec((tb, D_MODEL), lambda j, c: (j, 0)),
            pl.BlockSpec((D_MODEL, tb), lambda j, c: (0, j)),
            row_spec, rt_spec, row_spec, rt_spec,
            pl.BlockSpec((PEER_HEADS, tb), lambda j, c: (0, j)),
            pl.BlockSpec((ec, D_MODEL), lambda j, c: (jnp.minimum(c, n_chunks - 1), 0)),
            pl.BlockSpec((D_MODEL, ec), lambda j, c: (0, jnp.maximum(c - 1, 0))),
            pl.BlockSpec((1, D_MODEL), lambda j, c: (0, 0)),
        ],
        out_specs=pl.BlockSpec((tb, D_MODEL), lambda j, c: (j, 0)),
        out_shape=jax.ShapeDtypeStruct((n_tok, D_MODEL), F32),
        scratch_shapes=[
            pltpu.VMEM((D_MODEL, tb), F32),
            pltpu.VMEM((2, ec, tb), MXU_DTYPE),
        ],
        compiler_params=pltpu.CompilerParams(
            dimension_semantics=("arbitrary", "arbitrary"), vmem_limit_bytes=VMEM_LIMIT_BYTES),
        name="peer_experts",
    )(x2d, xnt, s1, s2, e1, e2, tau, u, vt, final_g.reshape(1, D_MODEL))


def peer_block(x2d, norm_g, w_q, keys, u, v, final_g, final_norm=False):
    wqt = w_q.T.astype(MXU_DTYPE)
    keys2 = keys.reshape(2 * PEER_HEADS, PEER_KEYS, PEER_HALF).astype(MXU_DTYPE)
    xnt, s1, s2, e1, e2, tau = _peer_route(x2d, norm_g.reshape(1, D_MODEL), wqt, keys2)
    return _peer_experts(x2d, xnt, s1, s2, e1, e2, tau, u, v, final_g, final_norm)


MLA_HG = LANES
MLA_QK_W = MLA_HEADS * MLA_HG
MLA_V_W = MLA_HEADS * MLA_V
MLA_PREP_TOK = 512
MLA_TQ = 512
MLA_TK = 512
MLA_RHALF = MLA_ROPE // 2


def _mla_rope_tables(s):
    inv = ROPE_THETA ** (-jnp.arange(MLA_RHALF, dtype=F32) / MLA_RHALF)
    ang = jnp.arange(s).astype(F32)[:, None] * inv[None, :]
    cos, sin = jnp.cos(ang), jnp.sin(ang)
    zero_pad = jnp.zeros((s, MLA_HG - MLA_NOPE - MLA_ROPE), F32)
    cos_rot = jnp.concatenate([cos, cos, zero_pad], axis=1)
    sin_rot = jnp.concatenate([-sin, sin, zero_pad], axis=1)
    scale = np.float32((MLA_NOPE + MLA_ROPE) ** -0.5)
    q_cos = scale * jnp.concatenate([jnp.ones((s, MLA_NOPE), F32), cos_rot], axis=1)
    q_sin = scale * jnp.concatenate([jnp.zeros((s, MLA_NOPE), F32), sin_rot], axis=1)
    k_cos = jnp.concatenate([jnp.zeros((s, MLA_NOPE), F32), cos_rot], axis=1)
    k_sin = jnp.concatenate([jnp.zeros((s, MLA_NOPE), F32), sin_rot], axis=1)
    return q_cos, q_sin, k_cos, k_sin


def _mla_pack_weights(w_qb, w_kvb):
    hd_q = MLA_NOPE + MLA_ROPE
    wq = w_qb.reshape(MLA_Q_RANK, MLA_HEADS, hd_q)
    rot = wq[:, :, MLA_NOPE:]
    rot_sw = jnp.concatenate([rot[:, :, MLA_RHALF:], rot[:, :, :MLA_RHALF]], axis=2)
    pad = jnp.zeros((MLA_Q_RANK, MLA_HEADS, MLA_HG - hd_q), F32)
    wq_a = jnp.concatenate([wq, pad], axis=2).reshape(MLA_Q_RANK, MLA_QK_W)
    wq_b = jnp.concatenate([jnp.zeros_like(wq[:, :, :MLA_NOPE]), rot_sw, pad], axis=2).reshape(MLA_Q_RANK, MLA_QK_W)
    wkv = w_kvb.reshape(MLA_KV_RANK, MLA_HEADS, MLA_NOPE + MLA_V)
    wk = jnp.concatenate([wkv[:, :, :MLA_NOPE], jnp.zeros((MLA_KV_RANK, MLA_HEADS, MLA_HG - MLA_NOPE), F32)],
                         axis=2).reshape(MLA_KV_RANK, MLA_QK_W)
    wv = jnp.concatenate([wkv[:, :, MLA_NOPE:], jnp.zeros((MLA_KV_RANK, MLA_HEADS, MLA_HG - MLA_V), F32)],
                         axis=2).reshape(MLA_KV_RANK, MLA_QK_W)
    return (wq_a.astype(MXU_DTYPE), wq_b.astype(MXU_DTYPE), wk.astype(MXU_DTYPE), wv.astype(MXU_DTYPE))


def _mla_prep_kernel(cq_ref, ckv_ref, kr_ref, qn_ref, kvn_ref, wqa_ref, wqb_ref, wk_ref, wv_ref,
                     qcos_ref, qsin_ref, kcos_ref, ksin_ref, q_out, kt_out, v_out):
    cq = cq_ref[0]
    cqn = (cq * lax.rsqrt(jnp.mean(cq * cq, axis=-1, keepdims=True) + EPS) * qn_ref[...]).astype(MXU_DTYPE)
    qa = jnp.dot(cqn, wqa_ref[...], preferred_element_type=F32)
    qb = jnp.dot(cqn, wqb_ref[...], preferred_element_type=F32)
    ckv = ckv_ref[0]
    ckvn = (ckv * lax.rsqrt(jnp.mean(ckv * ckv, axis=-1, keepdims=True) + EPS) * kvn_ref[...]).astype(MXU_DTYPE)
    ka = jnp.dot(ckvn, wk_ref[...], preferred_element_type=F32)
    one_lane = (lax.broadcasted_iota(jnp.int32, (1, MLA_QK_W), 1) % MLA_HG == MLA_V).astype(F32)
    v_out[0] = (jnp.dot(ckvn, wv_ref[...], preferred_element_type=F32) + one_lane).astype(MXU_DTYPE)
    kr = kr_ref[0]
    k_rot = kr[:, 0:MLA_HG] * kcos_ref[...] + kr[:, MLA_HG:2 * MLA_HG] * ksin_ref[...]
    for h in range(MLA_HEADS):
        lanes = slice(h * MLA_HG, (h + 1) * MLA_HG)
        q_out[0, :, lanes] = (qa[:, lanes] * qcos_ref[...] + qb[:, lanes] * qsin_ref[...]).astype(MXU_DTYPE)
        kt_out[0, lanes, :] = (ka[:, lanes] + k_rot).T.astype(MXU_DTYPE)


def _mla_flash_kernel(q_ref, kt_ref, v_ref, out_ref, m_scr, acc_scr):
    ki = pl.program_id(2)

    @pl.when(ki == 0)
    def _():
        m_scr[...] = jnp.full(m_scr.shape, -jnp.inf, F32)
        acc_scr[...] = jnp.zeros(acc_scr.shape, F32)

    def logits(h):
        lanes = slice(h * MLA_HG, (h + 1) * MLA_HG)
        return jnp.dot(q_ref[0, :, lanes], kt_ref[0, lanes, :], preferred_element_type=F32)

    n_rep = kt_ref.shape[2] // MLA_HG
    s_next = logits(0)
    for h in range(MLA_HEADS):
        s = s_next
        if h + 1 < MLA_HEADS:
            s_next = logits(h + 1)
        m_old = m_scr[h]
        m_new = jnp.maximum(m_old, jnp.max(s, axis=-1, keepdims=True))
        p = jnp.exp(s - jnp.tile(m_new, (1, n_rep))).astype(MXU_DTYPE)
        acc_scr[h] = jnp.exp(m_old - m_new) * acc_scr[h] + jnp.dot(
            p, v_ref[0, :, h * MLA_HG:(h + 1) * MLA_HG], preferred_element_type=F32)
        m_scr[h] = m_new

    @pl.when(ki == pl.num_programs(2) - 1)
    def _():
        low = lax.broadcasted_iota(jnp.int32, (acc_scr.shape[1], MLA_HG), 1) < MLA_V
        outs = []
        for h in range(MLA_HEADS):
            acc = acc_scr[h]
            outs.append(acc / acc[:, MLA_V:MLA_V + 1])
        for hp in range(MLA_HEADS // 2):
            odd = pltpu.roll(outs[2 * hp + 1], MLA_V, axis=1)
            out_ref[0, :, hp * MLA_HG:(hp + 1) * MLA_HG] = jnp.where(low, outs[2 * hp], odd)


def mla_rotary_key_columns(w):
    zl = jnp.zeros(w.shape[:-1] + (MLA_NOPE,), w.dtype)
    zr = jnp.zeros(w.shape[:-1] + (MLA_HG - MLA_NOPE - MLA_ROPE,), w.dtype)
    w_sw = jnp.concatenate([w[..., MLA_RHALF:], w[..., :MLA_RHALF]], axis=-1)
    return jnp.concatenate([zl, w, zr, zl, w_sw, zr], axis=-1)


def mla_mixer(arr, q_norm, w_qb, kv_norm, w_kvb, col_q=0, col_kv=MLA_Q_RANK, col_kr=MLA_Q_RANK + MLA_KV_RANK):
    b, s, _ = arr.shape
    tt = MLA_PREP_TOK
    assert col_q % MLA_Q_RANK == 0 and col_kv % MLA_KV_RANK == 0 and col_kr % (2 * MLA_HG) == 0
    wqa, wqb, wk, wv = _mla_pack_weights(w_qb, w_kvb)
    q_cos, q_sin, k_cos, k_sin = _mla_rope_tables(s)
    tok = lambda w, c=0: pl.BlockSpec((1, tt, w), lambda i, j: (i, j, c // w))
    full = lambda r, c: pl.BlockSpec((r, c), lambda i, j: (0, 0))
    tab = pl.BlockSpec((tt, MLA_HG), lambda i, j: (j, 0))
    q, kt, v = pl.pallas_call(
        _mla_prep_kernel,
        grid=(b, s // tt),
        in_specs=[tok(MLA_Q_RANK, col_q), tok(MLA_KV_RANK, col_kv), tok(2 * MLA_HG, col_kr),
                  full(1, MLA_Q_RANK), full(1, MLA_KV_RANK),
                  full(MLA_Q_RANK, MLA_QK_W), full(MLA_Q_RANK, MLA_QK_W),
                  full(MLA_KV_RANK, MLA_QK_W), full(MLA_KV_RANK, MLA_QK_W),
                  tab, tab, tab, tab],
        out_specs=[tok(MLA_QK_W), pl.BlockSpec((1, MLA_QK_W, tt), lambda i, j: (i, 0, j)), tok(MLA_QK_W)],
        out_shape=[jax.ShapeDtypeStruct((b, s, MLA_QK_W), MXU_DTYPE),
                   jax.ShapeDtypeStruct((b, MLA_QK_W, s), MXU_DTYPE),
                   jax.ShapeDtypeStruct((b, s, MLA_QK_W), MXU_DTYPE)],
        compiler_params=pltpu.CompilerParams(
            dimension_semantics=("arbitrary", "arbitrary"), vmem_limit_bytes=VMEM_LIMIT_BYTES),
        name="mla_prep",
    )(arr, arr, arr, q_norm.reshape(1, -1), kv_norm.reshape(1, -1), wqa, wqb, wk, wv,
      q_cos, q_sin, k_cos, k_sin)
    tq, tk = MLA_TQ, MLA_TK
    return pl.pallas_call(
        _mla_flash_kernel,
        grid=(b, s // tq, s // tk),
        in_specs=[pl.BlockSpec((1, tq, MLA_QK_W), lambda i, j, kk: (i, j, 0)),
                  pl.BlockSpec((1, MLA_QK_W, tk), lambda i, j, kk: (i, 0, kk)),
                  pl.BlockSpec((1, tk, MLA_QK_W), lambda i, j, kk: (i, kk, 0))],
        out_specs=pl.BlockSpec((1, tq, MLA_V_W), lambda i, j, kk: (i, j, 0)),
        out_shape=jax.ShapeDtypeStruct((b, s, MLA_V_W), F32),
        scratch_shapes=[pltpu.VMEM((MLA_HEADS, tq, MLA_HG), F32),
                        pltpu.VMEM((MLA_HEADS, tq, MLA_HG), F32)],
        compiler_params=pltpu.CompilerParams(
            dimension_semantics=("arbitrary", "arbitrary", "arbitrary"), vmem_limit_bytes=VMEM_LIMIT_BYTES),
        name="mla_flash",
    )(q, kt, v)


def t5_bucket(rel):
    nb = T5_BUCKETS // 2
    ret = np.where(rel > 0, nb, 0)
    n = np.abs(rel)
    max_exact = nb // 2
    large = max_exact + (np.log(np.maximum(n, 1) / max_exact) / np.log(T5_MAX_DIST / max_exact)
                         * (nb - max_exact)).astype(np.int64)
    large = np.minimum(large, nb - 1)
    return (ret + np.where(n < max_exact, n, large)).astype(np.int32)


DIL_DIM = DIL_HEADS * DIL_HD
DIL_HALF = 64
DIL_QB = 128
DIL_KW = DIL_QB + 2 * DIL_HALF
DIL_TL = 512
DIL_N_GROUPS = len(DIL_GROUPS)


def _dil_bias_table(t5_table, gi, dil):
    rel = np.arange(DIL_KW)[None, :] - DIL_HALF - np.arange(DIL_QB)[:, None]
    bias = t5_table[:, gi * DIL_HEADS:(gi + 1) * DIL_HEADS][t5_bucket(rel * dil)].astype(F32)
    bias = jnp.where((np.abs(rel) <= DIL_HALF)[:, :, None], bias, -jnp.inf)
    return bias.transpose(2, 0, 1).reshape(DIL_HEADS * DIL_QB, DIL_KW)


def _dil_kernel(prev_ref, cur_ref, next_ref, tab_ref, o_ref, lse_ref, k_scr, v_scr, *, seq_len):
    step = pl.program_id(2)
    for i, ref in enumerate((prev_ref, cur_ref, next_ref)):
        k_scr[i * DIL_TL:(i + 1) * DIL_TL, :] = ref[0, :, DIL_DIM:2 * DIL_DIM].astype(MXU_DTYPE)
        v_scr[i * DIL_TL:(i + 1) * DIL_TL, :] = ref[0, :, 2 * DIL_DIM:3 * DIL_DIM].astype(MXU_DTYPE)
    lane_head = lax.broadcasted_iota(jnp.int32, (DIL_QB, DIL_DIM), 1) // DIL_HD
    key_off = lax.broadcasted_iota(jnp.int32, (1, DIL_KW), 1)
    for n in range(DIL_TL // DIL_QB):
        w0 = DIL_TL + n * DIL_QB - DIL_HALF
        kpos = step * DIL_TL + (n * DIL_QB - DIL_HALF) + key_off
        valid = (kpos >= 0) & (kpos < seq_len)
        q = cur_ref[0, n * DIL_QB:(n + 1) * DIL_QB, 0:DIL_DIM] * np.float32(DIL_HD ** -0.5)
        qs = jnp.concatenate([jnp.where(lane_head == h, q, 0.0) for h in range(DIL_HEADS)], axis=0)
        logits = lax.dot_general(qs.astype(MXU_DTYPE), k_scr[w0:w0 + DIL_KW, :], (((1,), (1,)), ((), ())),
                                 preferred_element_type=F32) + tab_ref[...]
        logits = jnp.where(valid, logits, -jnp.inf)
        m = jnp.max(logits, axis=-1, keepdims=True)
        p = jnp.exp(logits - m)
        denom = jnp.sum(p, axis=-1, keepdims=True)
        o_all = jnp.dot(p.astype(MXU_DTYPE), v_scr[w0:w0 + DIL_KW, :], preferred_element_type=F32) / denom
        lse_all = m + jnp.log(denom)
        o = jnp.zeros((DIL_QB, DIL_DIM), F32)
        lse = jnp.zeros((DIL_QB, DIL_DIM), F32)
        for h in range(DIL_HEADS):
            rows = slice(h * DIL_QB, (h + 1) * DIL_QB)
            o = o + jnp.where(lane_head == h, o_all[rows, :], 0.0)
            lse = lse + jnp.where(lane_head == h, lse_all[rows, :], 0.0)
        o_ref[0, n * DIL_QB:(n + 1) * DIL_QB, :] = o
        lse_ref[0, n * DIL_QB:(n + 1) * DIL_QB, :] = lse


def _dil_group(qkv, t5_table, gi, dil, col):
    b, s, width = qkv.shape
    seq_len = s // dil
    assert seq_len % DIL_TL == 0 and width % (3 * DIL_DIM) == 0 and col % (3 * DIL_DIM) == 0
    n_steps = seq_len // DIL_TL
    n_col = width // (3 * DIL_DIM)
    gi_col = col // (3 * DIL_DIM) + gi
    view = qkv.reshape(b, seq_len, dil * width)
    blk = (1, DIL_TL, 3 * DIL_DIM)
    out_shape = jax.ShapeDtypeStruct((b, seq_len, dil * DIL_DIM), F32)
    out_spec = pl.BlockSpec((1, DIL_TL, DIL_DIM), lambda i, r, j: (i, j, r))
    o, lse = pl.pallas_call(
        functools.partial(_dil_kernel, seq_len=seq_len),
        grid=(b, dil, n_steps),
        in_specs=[
            pl.BlockSpec(blk, lambda i, r, j: (i, jnp.maximum(j - 1, 0), r * n_col + gi_col)),
            pl.BlockSpec(blk, lambda i, r, j: (i, j, r * n_col + gi_col)),
            pl.BlockSpec(blk, lambda i, r, j: (i, jnp.minimum(j + 1, n_steps - 1), r * n_col + gi_col)),
            pl.BlockSpec((DIL_HEADS * DIL_QB, DIL_KW), lambda i, r, j: (0, 0)),
        ],
        out_specs=[out_spec, out_spec],
        out_shape=[out_shape, out_shape],
        scratch_shapes=[pltpu.VMEM((3 * DIL_TL, DIL_DIM), MXU_DTYPE),
                        pltpu.VMEM((3 * DIL_TL, DIL_DIM), MXU_DTYPE)],
        compiler_params=pltpu.CompilerParams(
            dimension_semantics=("arbitrary", "arbitrary", "arbitrary"), vmem_limit_bytes=VMEM_LIMIT_BYTES),
        name=f"dilated_attention_g{gi}",
    )(view, view, view, _dil_bias_table(t5_table, gi, dil))
    return o.reshape(b, s, DIL_DIM), lse.reshape(b, s, DIL_DIM)


def _dil_combine_kernel(*refs):
    o_refs, lse_refs, out_ref = refs[:DIL_N_GROUPS], refs[DIL_N_GROUPS:2 * DIL_N_GROUPS], refs[-1]
    lses = [r[...] for r in lse_refs]
    m = functools.reduce(jnp.maximum, lses)
    ws = [jnp.exp(l - m) for l in lses]
    total = functools.reduce(jnp.add, ws)
    acc = functools.reduce(jnp.add, [w * r[...] for w, r in zip(ws, o_refs)])
    out_ref[...] = acc / total


def dilated_mixer(qkv, t5_table, col=0):
    b, s, _ = qkv.shape
    outs, lses = [], []
    for gi, (win, dil) in enumerate(DIL_GROUPS):
        assert win // (2 * dil) == DIL_HALF
        o, lse = _dil_group(qkv, t5_table, gi, dil, col)
        outs.append(o.reshape(b * s, DIL_DIM))
        lses.append(lse.reshape(b * s, DIL_DIM))
    tm = 1024
    spec = pl.BlockSpec((tm, DIL_DIM), lambda i: (i, 0))
    out = pl.pallas_call(
        _dil_combine_kernel,
        grid=(b * s // tm,),
        in_specs=[spec] * (2 * DIL_N_GROUPS),
        out_specs=spec,
        out_shape=jax.ShapeDtypeStruct((b * s, DIL_DIM), F32),
        compiler_params=pltpu.CompilerParams(
            dimension_semantics=("arbitrary",), vmem_limit_bytes=VMEM_LIMIT_BYTES),
        name="dilated_combine",
    )(*outs, *lses)
    return out.reshape(b, s, DIL_DIM)


SSM_CONV_TOK = 512
SSM_HALO = SUBLANES
SSM_HEADS_PER_GROUP = SSM_HEADS // SSM_GROUPS
SSM_GROUP_W = SSM_HEADS_PER_GROUP * SSM_HD
SSM_BC_W = SSM_GROUPS * SSM_STATE


def _softplus(x):
    return jnp.maximum(x, 0.0) + jnp.log1p(jnp.exp(-jnp.abs(x)))


def _ssm_conv_kernel(prev_ref, cur_ref, next_ref, w_ref, b_ref, out_ref, cat_scr):
    j = pl.program_id(1)
    tl = cur_ref.shape[1]
    cat_scr[0:SSM_HALO, :] = jnp.where(j > 0, prev_ref[0], 0.0)
    cat_scr[SSM_HALO:SSM_HALO + tl, :] = cur_ref[0]
    cat_scr[SSM_HALO + tl:, :] = jnp.where(j < pl.num_programs(1) - 1, next_ref[0], 0.0)
    acc = jnp.zeros((tl, CONV_CH), F32) + b_ref[...]
    for k in range(SSM_CONV):
        off = SSM_HALO + k - SSM_CONV // 2
        acc = acc + cat_scr[off:off + tl, :] * w_ref[k:k + 1, :]
    out_ref[0] = acc * jax.nn.sigmoid(acc)


def _ssm_conv(xbc, conv_w, conv_b, col):
    b, l, _ = xbc.shape
    tl = SSM_CONV_TOK
    n_steps = l // tl
    per = tl // SSM_HALO
    assert col % CONV_CH == 0
    cb = col // CONV_CH
    return pl.pallas_call(
        _ssm_conv_kernel,
        grid=(b, n_steps),
        in_specs=[
            pl.BlockSpec((1, SSM_HALO, CONV_CH), lambda i, j: (i, jnp.maximum(j * per - 1, 0), cb)),
            pl.BlockSpec((1, tl, CONV_CH), lambda i, j: (i, j, cb)),
            pl.BlockSpec((1, SSM_HALO, CONV_CH), lambda i, j: (i, jnp.minimum((j + 1) * per, n_steps * per - 1), cb)),
            pl.BlockSpec((SSM_CONV, CONV_CH), lambda i, j: (0, 0)),
            pl.BlockSpec((1, CONV_CH), lambda i, j: (0, 0)),
        ],
        out_specs=pl.BlockSpec((1, tl, CONV_CH), lambda i, j: (i, j, 0)),
        out_shape=jax.ShapeDtypeStruct((b, l, CONV_CH), F32),
        scratch_shapes=[pltpu.VMEM((tl + 2 * SSM_HALO, CONV_CH), F32)],
        compiler_params=pltpu.CompilerParams(
            dimension_semantics=("arbitrary", "arbitrary"), vmem_limit_bytes=VMEM_LIMIT_BYTES),
        name="ssm_conv",
    )(xbc, xbc, xbc, conv_w, conv_b.reshape(1, CONV_CH))


def _ssd_kernel(xbc_ref, dt_ref, dtt_ref, expand_ref, bias_e_ref, a_e_ref, bias_t_ref, a_t_ref,
                y_ref, state_scr, *, reverse):
    q = SSM_CHUNK

    @pl.when(pl.program_id(1) == 0)
    def _():
        state_scr[...] = jnp.zeros(state_scr.shape, F32)

    hi = lax.Precision.HIGHEST
    xbc = xbc_ref[0]
    xs = xbc[:, 0:SSM_INNER]
    dt_e = _softplus(jnp.dot(dt_ref[0], expand_ref[...], precision=hi, preferred_element_type=F32)
                     + bias_e_ref[...])
    a_e = dt_e * a_e_ref[...]
    ri = lax.broadcasted_iota(jnp.int32, (q, q), 0)
    ci = lax.broadcasted_iota(jnp.int32, (q, q), 1)
    seen = (ci >= ri) if reverse else (ci <= ri)
    cs_e = jnp.dot(seen.astype(F32), a_e, precision=hi, preferred_element_type=F32)
    dt_t = _softplus(dtt_ref[0] + bias_t_ref[...])
    cs_t = jnp.dot(dt_t * a_t_ref[...], seen.T.astype(F32), precision=hi, preferred_element_type=F32)
    dtx = xs * dt_e
    last = 0 if reverse else q - 1
    total = cs_e[last:last + 1, :]
    dtx_decayed = jnp.exp(total - cs_e) * dtx
    grow = jnp.exp(cs_e)
    chunk_decay = jnp.exp(total)
    lane_head = lax.broadcasted_iota(jnp.int32, (q, SSM_GROUP_W), 1) // SSM_HD
    for g in range(SSM_GROUPS):
        xl = slice(g * SSM_GROUP_W, (g + 1) * SSM_GROUP_W)
        bg = xbc[:, SSM_INNER + g * SSM_STATE:SSM_INNER + (g + 1) * SSM_STATE]
        cg = xbc[:, SSM_INNER + SSM_BC_W + g * SSM_STATE:SSM_INNER + SSM_BC_W + (g + 1) * SSM_STATE]
        cb = lax.dot_general(cg.astype(MXU_DTYPE), bg.astype(MXU_DTYPE), (((1,), (1,)), ((), ())),
                             preferred_element_type=F32)
        ms = []
        for r in range(SSM_HEADS_PER_GROUP):
            h = g * SSM_HEADS_PER_GROUP + r
            col = cs_e[:, h * SSM_HD:h * SSM_HD + 1]
            row = cs_t[h:h + 1, :]
            ms.append(cb * jnp.where(seen, jnp.exp(col - row), 0.0))
        y_all = jnp.dot(jnp.concatenate(ms, axis=0).astype(MXU_DTYPE), dtx[:, xl].astype(MXU_DTYPE),
                        preferred_element_type=F32)
        y_diag = jnp.zeros((q, SSM_GROUP_W), F32)
        for r in range(SSM_HEADS_PER_GROUP):
            y_diag = y_diag + jnp.where(lane_head == r, y_all[r * q:(r + 1) * q, :], 0.0)
        s_in = state_scr[g]
        y_off = jnp.dot(cg.astype(MXU_DTYPE), s_in.astype(MXU_DTYPE), preferred_element_type=F32) * grow[:, xl]
        y_ref[0, :, xl] = y_diag + y_off
        new = jnp.dot(bg.T.astype(MXU_DTYPE), dtx_decayed[:, xl].astype(MXU_DTYPE), preferred_element_type=F32)
        state_scr[g] = s_in * chunk_decay[:, xl] + new


def _ssd_direction(xbc_act, dt_arr, dt_t, a_log, dt_bias, direction, col_dt, dt_w):
    b, l, _ = xbc_act.shape
    assert col_dt % dt_w == 0 and dt_w >= 2 * SSM_HEADS
    q = SSM_CHUNK
    nc = l // q
    reverse = direction == 1
    a = -jnp.exp(a_log[direction].astype(F32))
    bias = dt_bias[direction].astype(F32)
    head_of_lane = np.arange(SSM_INNER) // SSM_HD
    expand = (np.arange(dt_w)[:, None] == direction * SSM_HEADS + head_of_lane[None, :]).astype(np.float32)
    chunk = (lambda j: nc - 1 - j) if reverse else (lambda j: j)
    full = lambda r, c: pl.BlockSpec((r, c), lambda i, j: (0, 0))
    return pl.pallas_call(
        functools.partial(_ssd_kernel, reverse=reverse),
        grid=(b, nc),
        in_specs=[
            pl.BlockSpec((1, q, CONV_CH), lambda i, j: (i, chunk(j), 0)),
            pl.BlockSpec((1, q, dt_w), lambda i, j: (i, chunk(j), col_dt // dt_w)),
            pl.BlockSpec((1, SSM_HEADS, q), lambda i, j: (i * 2 + direction, 0, chunk(j))),
            full(dt_w, SSM_INNER), full(1, SSM_INNER), full(1, SSM_INNER),
            full(SSM_HEADS, 1), full(SSM_HEADS, 1),
        ],
        out_specs=pl.BlockSpec((1, q, SSM_INNER), lambda i, j: (i, chunk(j), 0)),
        out_shape=jax.ShapeDtypeStruct((b, l, SSM_INNER), F32),
        scratch_shapes=[pltpu.VMEM((SSM_GROUPS, SSM_STATE, SSM_GROUP_W), F32)],
        compiler_params=pltpu.CompilerParams(
            dimension_semantics=("arbitrary", "arbitrary"), vmem_limit_bytes=VMEM_LIMIT_BYTES),
        name=f"ssd_dir{direction}",
    )(xbc_act, dt_arr, dt_t, jnp.asarray(expand), bias[head_of_lane].reshape(1, SSM_INNER),
      a[head_of_lane].reshape(1, SSM_INNER), bias.reshape(SSM_HEADS, 1), a.reshape(SSM_HEADS, 1))


def _ssm_gate_kernel(yf_ref, yb_ref, xbc_ref, z_ref, d_ref, g_ref, out_ref):
    z = z_ref[...]
    y = (yf_ref[...] + yb_ref[...] + xbc_ref[...] * d_ref[...]) * (z * jax.nn.sigmoid(z))
    out_ref[...] = y * lax.rsqrt(jnp.mean(y * y, axis=-1, keepdims=True) + EPS) * g_ref[...]


def mamba2_mixer(arr, conv_w, conv_b, A_log, dt_bias, D_skip, norm_g,
                 col_z=0, col_xbc=CONV_CH, col_dt=SSM_INNER + CONV_CH, dt_w=2 * SSM_HEADS):
    b, l, width = arr.shape
    assert col_z % SSM_INNER == 0
    xbc_act = _ssm_conv(arr, conv_w, conv_b, col_xbc)
    dt_t = jnp.swapaxes(arr[:, :, col_dt:col_dt + 2 * SSM_HEADS], 1, 2).reshape(b * 2, SSM_HEADS, l)
    y_f = _ssd_direction(xbc_act, arr, dt_t, A_log, dt_bias, 0, col_dt, dt_w)
    y_b = _ssd_direction(xbc_act, arr, dt_t, A_log, dt_bias, 1, col_dt, dt_w)
    tm = 1024
    tok = lambda cb: pl.BlockSpec((tm, SSM_INNER), lambda i: (i, cb))
    row = pl.BlockSpec((1, SSM_INNER), lambda i: (0, 0))
    d_e = D_skip.astype(F32)[np.arange(SSM_INNER) // SSM_HD].reshape(1, SSM_INNER)
    out = pl.pallas_call(
        _ssm_gate_kernel,
        grid=(b * l // tm,),
        in_specs=[tok(0), tok(0), tok(0), tok(col_z // SSM_INNER), row, row],
        out_specs=tok(0),
        out_shape=jax.ShapeDtypeStruct((b * l, SSM_INNER), F32),
        compiler_params=pltpu.CompilerParams(
            dimension_semantics=("arbitrary",), vmem_limit_bytes=VMEM_LIMIT_BYTES),
        name="ssm_gate",
    )(y_f.reshape(b * l, SSM_INNER), y_b.reshape(b * l, SSM_INNER), xbc_act.reshape(b * l, CONV_CH),
      arr.reshape(b * l, width), d_e, norm_g.reshape(1, SSM_INNER))
    return out.reshape(b, l, SSM_INNER)


NA_DIM = NA_HEADS * NA_HD
NA_ROWS_PER_STEP = 8
NA_WIN = NA_ROWS * GRID_W
NA_STEP_TOK = NA_ROWS_PER_STEP * GRID_W


def _na_bias_table(rpb):
    delta = np.arange(NA_ROWS)[:, None, None, None]
    w = np.arange(NA_ROWS)[None, None, :, None]
    qc = np.arange(GRID_W)[None, :, None, None]
    kc = np.arange(GRID_W)[None, None, None, :]
    cs = np.clip(qc - NA_COLS // 2, 0, GRID_W - NA_COLS)
    ok = (kc >= cs) & (kc < cs + NA_COLS)
    dr = np.broadcast_to(w - delta + NA_ROWS - 1, (NA_ROWS, GRID_W, NA_ROWS, GRID_W))
    dc = np.broadcast_to(np.clip(kc - qc + NA_COLS - 1, 0, 2 * NA_COLS - 2), dr.shape)
    ok = np.broadcast_to(ok, dr.shape)
    bias = rpb.astype(F32)[:, dr, dc]
    bias = jnp.where(ok[None], bias, -jnp.inf)
    return bias.transpose(1, 0, 2, 3, 4).reshape(NA_ROWS, NA_HEADS * GRID_W, NA_WIN)


def _na_kernel(prev_ref, cur_ref, next_ref, tab_ref, out_ref, k_scr, v_scr, *, n_rows):
    step = pl.program_id(1)
    for i, ref in enumerate((prev_ref, cur_ref, next_ref)):
        k_scr[i * NA_STEP_TOK:(i + 1) * NA_STEP_TOK, :] = ref[0, :, NA_DIM:2 * NA_DIM].astype(MXU_DTYPE)
        v_scr[i * NA_STEP_TOK:(i + 1) * NA_STEP_TOK, :] = ref[0, :, 2 * NA_DIM:3 * NA_DIM].astype(MXU_DTYPE)
    lane_head = lax.broadcasted_iota(jnp.int32, (GRID_W, NA_DIM), 1) // NA_HD
    row0 = step * NA_ROWS_PER_STEP
    for j in range(NA_ROWS_PER_STEP):
        r = row0 + j
        r0 = jnp.clip(r - NA_ROWS // 2, 0, n_rows - NA_ROWS)
        start = pl.multiple_of((r0 - row0 + NA_ROWS_PER_STEP) * GRID_W, GRID_W)
        q = cur_ref[0, j * GRID_W:(j + 1) * GRID_W, 0:NA_DIM] * np.float32(NA_HD ** -0.5)
        qs = jnp.concatenate([jnp.where(lane_head == h, q, 0.0) for h in range(NA_HEADS)], axis=0)
        kw = k_scr[pl.ds(start, NA_WIN), :]
        vw = v_scr[pl.ds(start, NA_WIN), :]
        logits = lax.dot_general(qs.astype(MXU_DTYPE), kw, (((1,), (1,)), ((), ())),
                                 preferred_element_type=F32) + tab_ref[r - r0]
        m = jnp.max(logits, axis=-1, keepdims=True)
        p = jnp.exp(logits - m)
        denom = jnp.sum(p, axis=-1, keepdims=True)
        o_all = jnp.dot(p.astype(MXU_DTYPE), vw, preferred_element_type=F32) / denom
        o = jnp.zeros((GRID_W, NA_DIM), F32)
        for h in range(NA_HEADS):
            o = o + jnp.where(lane_head == h, o_all[h * GRID_W:(h + 1) * GRID_W, :], 0.0)
        out_ref[0, j * GRID_W:(j + 1) * GRID_W, :] = o


def na_mixer(qkv, rpb, col=0):
    b, s, _ = qkv.shape
    n_rows = s // GRID_W
    assert n_rows >= NA_ROWS and n_rows % NA_ROWS_PER_STEP == 0 and col % (3 * NA_DIM) == 0
    n_steps = n_rows // NA_ROWS_PER_STEP
    blk = (1, NA_STEP_TOK, 3 * NA_DIM)
    cb = col // (3 * NA_DIM)
    return pl.pallas_call(
        functools.partial(_na_kernel, n_rows=n_rows),
        grid=(b, n_steps),
        in_specs=[
            pl.BlockSpec(blk, lambda i, j: (i, jnp.maximum(j - 1, 0), cb)),
            pl.BlockSpec(blk, lambda i, j: (i, j, cb)),
            pl.BlockSpec(blk, lambda i, j: (i, jnp.minimum(j + 1, n_steps - 1), cb)),
            pl.BlockSpec((NA_ROWS, NA_HEADS * GRID_W, NA_WIN), lambda i, j: (0, 0, 0)),
        ],
        out_specs=pl.BlockSpec((1, NA_STEP_TOK, NA_DIM), lambda i, j: (i, j, 0)),
        out_shape=jax.ShapeDtypeStruct((b, s, NA_DIM), F32),
        scratch_shapes=[
            pltpu.VMEM((3 * NA_STEP_TOK, NA_DIM), MXU_DTYPE),
            pltpu.VMEM((3 * NA_STEP_TOK, NA_DIM), MXU_DTYPE),
        ],
        compiler_params=pltpu.CompilerParams(
            dimension_semantics=("arbitrary", "arbitrary"), vmem_limit_bytes=VMEM_LIMIT_BYTES),
        name="na_attention",
    )(qkv, qkv, qkv, _na_bias_table(rpb))


PK_GATE = 0
PK_XBC = PK_GATE + N_BRANCH * D_MODEL
PK_CQ = PK_XBC + CONV_CH
PK_DIL = PK_CQ + MLA_Q_RANK
PK_NA = PK_DIL + DIL_N_GROUPS * 3 * DIL_DIM
PK_KR = PK_NA + 3 * NA_DIM
PK_Z = PK_KR + 2 * MLA_HG
PK_CKV = PK_Z + SSM_INNER
PK_DT = PK_CKV + MLA_KV_RANK
PK_DT_W = LANES
PK_WIDTH = 13 * 3 * DIL_DIM
INPROJ_TM = 512
INPROJ_TN = PK_WIDTH // 2
MERGE_TM = 512


def _pack_w_in(w_in_l):
    gate, a_cq, a_ckv, a_kr, b_qkv, c_z, c_xbc, c_dt, d_qkv = jnp.split(w_in_l, IN_SPLITS, axis=-1)
    zeros = lambda n: jnp.zeros((D_MODEL, n), w_in_l.dtype)
    cols = [gate, c_xbc, a_cq, b_qkv, d_qkv, mla_rotary_key_columns(a_kr), c_z, a_ckv,
            c_dt, zeros(PK_DT_W - 2 * SSM_HEADS)]
    packed = jnp.concatenate(cols, axis=-1)
    assert packed.shape[1] == PK_DT + PK_DT_W
    return jnp.concatenate([packed, zeros(PK_WIDTH - packed.shape[1])], axis=-1).astype(MXU_DTYPE)


def _inproj_kernel(x_ref, g_ref, w_ref, out_ref, h_scr):
    @pl.when(pl.program_id(1) == 0)
    def _():
        x = x_ref[...]
        h_scr[...] = (x * lax.rsqrt(jnp.mean(x * x, axis=-1, keepdims=True) + EPS) * g_ref[...]).astype(MXU_DTYPE)

    out_ref[...] = jnp.dot(h_scr[...], w_ref[...], preferred_element_type=F32)


def _inproj(x2d, norm_g, w_packed):
    n_tok = x2d.shape[0]
    tm, tn = INPROJ_TM, INPROJ_TN
    return pl.pallas_call(
        _inproj_kernel,
        grid=(n_tok // tm, PK_WIDTH // tn),
        in_specs=[pl.BlockSpec((tm, D_MODEL), lambda i, j: (i, 0)),
                  pl.BlockSpec((1, D_MODEL), lambda i, j: (0, 0)),
                  pl.BlockSpec((D_MODEL, tn), lambda i, j: (0, j))],
        out_specs=pl.BlockSpec((tm, tn), lambda i, j: (i, j)),
        out_shape=jax.ShapeDtypeStruct((n_tok, PK_WIDTH), F32),
        scratch_shapes=[pltpu.VMEM((tm, D_MODEL), MXU_DTYPE)],
        compiler_params=pltpu.CompilerParams(
            dimension_semantics=("arbitrary", "arbitrary"), vmem_limit_bytes=VMEM_LIMIT_BYTES),
        name="in_projection",
    )(x2d, norm_g.reshape(1, D_MODEL), w_packed)


def _merge_kernel(x_ref, gate_ref, bg_ref, ya_ref, yb_ref, yc_ref, yd_ref, wb_ref, wo_ref, out_ref):
    merged = jnp.zeros(x_ref.shape, F32)
    for i, y_ref in enumerate((ya_ref, yb_ref, yc_ref, yd_ref)):
        proj = jnp.dot(y_ref[...].astype(MXU_DTYPE), wb_ref[BRANCH_ROWS[i]:BRANCH_ROWS[i + 1], :],
                       preferred_element_type=F32)
        lanes = slice(i * D_MODEL, (i + 1) * D_MODEL)
        merged = merged + jax.nn.sigmoid(gate_ref[:, lanes] + bg_ref[:, lanes]) * proj
    out_ref[...] = x_ref[...] + jnp.dot(merged.astype(MXU_DTYPE), wo_ref[...], preferred_element_type=F32)


def _merge(x2d, packed, b_gate, ys, w_branch, w_out):
    n_tok = x2d.shape[0]
    tm = MERGE_TM
    tok = lambda w: pl.BlockSpec((tm, w), lambda i: (i, 0))
    full = lambda r, c: pl.BlockSpec((r, c), lambda i: (0, 0))
    n_gate = N_BRANCH * D_MODEL
    return pl.pallas_call(
        _merge_kernel,
        grid=(n_tok // tm,),
        in_specs=[tok(D_MODEL), tok(n_gate), full(1, n_gate)] + [tok(w) for w in BRANCH_WIDTHS]
                 + [full(BRANCH_ROWS[-1], D_MODEL), full(D_MODEL, D_MODEL)],
        out_specs=tok(D_MODEL),
        out_shape=jax.ShapeDtypeStruct((n_tok, D_MODEL), F32),
        compiler_params=pltpu.CompilerParams(
            dimension_semantics=("arbitrary",), vmem_limit_bytes=VMEM_LIMIT_BYTES),
        name="branch_merge",
    )(x2d, packed, b_gate.reshape(1, n_gate), *[y.reshape(n_tok, -1) for y in ys], w_branch, w_out)


def encoder(x, norm1_g, w_in_packed, b_gate, mla_q_norm, mla_w_qb, mla_kv_norm, mla_w_kvb, t5_table,
            ssm_conv_w, ssm_conv_b, ssm_A_log, ssm_dt_bias, ssm_D, ssm_norm_g, na_rpb,
            w_branch, w_out, norm2_g, peer_wq, peer_keys, peer_u, peer_vt, final_g):
    b, s, _ = x.shape
    x2d = x.reshape(b * s, D_MODEL)
    for l in range(DEPTH):
        packed = _inproj(x2d, norm1_g[l], w_in_packed[l])
        p3 = packed.reshape(b, s, PK_WIDTH)
        y_a = mla_mixer(p3, mla_q_norm[l], mla_w_qb[l], mla_kv_norm[l], mla_w_kvb[l],
                        col_q=PK_CQ, col_kv=PK_CKV, col_kr=PK_KR)
        y_b = dilated_mixer(p3, t5_table, col=PK_DIL)
        y_c = mamba2_mixer(p3, ssm_conv_w[l], ssm_conv_b[l], ssm_A_log[l], ssm_dt_bias[l], ssm_D[l],
                           ssm_norm_g[l], col_z=PK_Z, col_xbc=PK_XBC, col_dt=PK_DT, dt_w=PK_DT_W)
        y_d = na_mixer(p3, na_rpb[l], col=PK_NA)
        x2d = _merge(x2d, packed, b_gate[l], (y_a, y_b, y_c, y_d), w_branch[l], w_out[l])
        x2d = peer_block(x2d, norm2_g[l], peer_wq[l], peer_keys[l], peer_u[l], peer_vt[l],
                         final_g, final_norm=(l == DEPTH - 1))
    return x2d.reshape(b, s, D_MODEL)


def kernel(x_prompt, x_sample, norm1_g, w_in, b_gate, mla_q_norm, mla_w_qb, mla_kv_norm, mla_w_kvb, t5_table, ssm_conv_w, ssm_conv_b, ssm_A_log, ssm_dt_bias, ssm_D, ssm_norm_g, na_rpb, w_branch, w_out, norm2_g, peer_wq, peer_keys, peer_u, peer_v, final_g):
    peer_u16 = peer_u.astype(MXU_DTYPE)
    peer_vt16 = jnp.swapaxes(peer_v, 1, 2).astype(MXU_DTYPE)
    w_in_packed = jnp.stack([_pack_w_in(w_in[l]) for l in range(DEPTH)])
    shared = (norm1_g, w_in_packed, b_gate, mla_q_norm, mla_w_qb, mla_kv_norm, mla_w_kvb, t5_table,
              ssm_conv_w, ssm_conv_b, ssm_A_log, ssm_dt_bias, ssm_D, ssm_norm_g, na_rpb,
              w_branch.astype(MXU_DTYPE), w_out.astype(MXU_DTYPE), norm2_g, peer_wq, peer_keys,
              peer_u16, peer_vt16, final_g)
    y_prompt = encoder(x_prompt, *shared)
    y_sample = encoder(x_sample, *shared)
    return (y_prompt, y_sample)
```

```python
import functools
import math

import numpy as np
import jax
import jax.numpy as jnp
from jax import lax
from jax.experimental import pallas as pl
from jax.experimental.pallas import tpu as pltpu

F32 = jnp.float32
BF16 = jnp.bfloat16
MXU_DTYPE = BF16

D_MODEL = 1024
DEPTH = 2
GRID_W = 64
EPS = 1e-6
N_BRANCH = 4

MLA_HEADS = 4
MLA_Q_RANK = 256
MLA_KV_RANK = 128
MLA_NOPE = 64
MLA_ROPE = 32
MLA_V = 64
ROPE_THETA = 10000.0
Q_BLOCK = 128

DIL_GROUPS = ((128, 1), (512, 4), (2048, 16))
DIL_HEADS = 4
DIL_HD = 64
T5_BUCKETS = 32
T5_MAX_DIST = 1024

SSM_HEADS = 8
SSM_HD = 64
SSM_INNER = SSM_HEADS * SSM_HD
SSM_GROUPS = 2
SSM_STATE = 128
SSM_CONV = 7
SSM_CHUNK = 128
CONV_CH = SSM_INNER + 2 * SSM_GROUPS * SSM_STATE

NA_HEADS = 4
NA_HD = 64
NA_ROWS = 8
NA_COLS = 16
NA_QCB = 16
NA_KCB = NA_QCB + NA_COLS

PEER_HEADS = 8
PEER_KEYS = 128
PEER_EXPERTS = PEER_KEYS * PEER_KEYS
PEER_QDIM = 256
PEER_TOPK = 16
PEER_TOK_BLOCK = 128

BRANCH_WIDTHS = (MLA_HEADS * MLA_V, DIL_HEADS * DIL_HD, SSM_INNER, NA_HEADS * NA_HD)
BRANCH_ROWS = tuple(sum(BRANCH_WIDTHS[:i]) for i in range(N_BRANCH + 1))
IN_SIZES = (N_BRANCH * D_MODEL, MLA_Q_RANK, MLA_KV_RANK, MLA_ROPE,
            len(DIL_GROUPS) * 3 * DIL_HEADS * DIL_HD,
            SSM_INNER, CONV_CH, 2 * SSM_HEADS,
            3 * NA_HEADS * NA_HD)
IN_SPLITS = tuple(sum(IN_SIZES[:i + 1]) for i in range(len(IN_SIZES) - 1))

VMEM_LIMIT_BYTES = 56 * 1024 * 1024
LANES = 128
SUBLANES = 8


def rms_norm(x, g):
    x32 = x.astype(F32)
    y = x32 * lax.rsqrt(jnp.mean(x32 * x32, axis=-1, keepdims=True) + EPS)
    return (y * g.astype(F32)).astype(x.dtype)


PEER_ROUTE_TB = 256
PEER_TB = 256
PEER_EC = 1024
PEER_GATE_DTYPE = BF16
PEER_GATE_ROWS = 16
PEER_HALF = PEER_QDIM // 2
PEER_CAND_ROWS = 2 * SUBLANES + 7 * SUBLANES + SUBLANES


def _gelu_exact(x):
    return 0.5 * x * (1.0 + lax.erf(x * np.float32(math.sqrt(0.5))))


def _extract_desc(vals, n_out, out_ref, row0, with_rank=False):
    rank = jnp.full(vals.shape, float(n_out), F32) if with_rank else None
    for k in range(n_out):
        m = jnp.max(vals, axis=0, keepdims=True)
        out_ref[pl.ds(row0 + k, 1), :] = m
        hit = vals == m
        if with_rank:
            rank = jnp.where(hit, float(k), rank)
        vals = jnp.where(hit, -jnp.inf, vals)
    return rank


def _peer_route_kernel(x_ref, g_ref, wqt_ref, keys_ref,
                       xnt_ref, cnt_ref, rank_ref, e1_ref, e2_ref,
                       qt_scr, top_scr, cand_scr, tops_scr):
    x = x_ref[...]
    xn = x * lax.rsqrt(jnp.mean(x * x, axis=-1, keepdims=True) + EPS) * g_ref[...]
    xnt = xn.T.astype(MXU_DTYPE)
    xnt_ref[...] = xnt
    qt_scr[...] = jnp.dot(wqt_ref[...], xnt, preferred_element_type=F32).astype(MXU_DTYPE)

    def head(h, carry):
        q1 = qt_scr[pl.ds(pl.multiple_of(h * PEER_QDIM, PEER_QDIM), PEER_HALF), :]
        q2 = qt_scr[pl.ds(pl.multiple_of(h * PEER_QDIM + PEER_HALF, PEER_HALF), PEER_HALF), :]
        s1 = jnp.dot(keys_ref[2 * h], q1, preferred_element_type=F32)
        s2 = jnp.dot(keys_ref[2 * h + 1], q2, preferred_element_type=F32)
        _extract_desc(s1, PEER_TOPK, top_scr, 0)
        rank2 = _extract_desc(s2, PEER_TOPK, top_scr, PEER_TOPK, with_rank=True)
        t1 = top_scr[0:PEER_TOPK, :]
        t2 = top_scr[PEER_TOPK:2 * PEER_TOPK, :]
        cand_scr[0:2 * SUBLANES, :] = t1[0:1, :] + t2
        for a in range(1, SUBLANES):
            cand_scr[(a + 1) * SUBLANES:(a + 2) * SUBLANES, :] = t1[a:a + 1, :] + t2[0:SUBLANES, :]
        cand_scr[9 * SUBLANES:10 * SUBLANES, :] = t1[SUBLANES:2 * SUBLANES, :] + t2[0:1, :]
        _extract_desc(cand_scr[...], PEER_TOPK, tops_scr, 0)
        top_s = tops_scr[...]
        z = jnp.sum(jnp.exp(top_s - top_s[0:1, :]), axis=0, keepdims=True)
        tau = top_s[PEER_TOPK - 1:PEER_TOPK, :]
        cnt = jnp.zeros(s1.shape, F32)
        for b in range(PEER_TOPK):
            cnt = cnt + jnp.where(s1 + t2[b:b + 1, :] >= tau, 1.0, 0.0)
        cnt_ref[h] = cnt
        rank_ref[h] = rank2.astype(rank_ref.dtype)
        e1_ref[h] = jnp.exp(s1 - t1[0:1, :]) / z
        e2_ref[h] = jnp.exp(s2 - t2[0:1, :]).astype(e2_ref.dtype)
        return carry

    lax.fori_loop(0, PEER_HEADS, head, 0)


def _peer_route(x2d, g, wqt, keys):
    n_tok = x2d.shape[0]
    tb = PEER_ROUTE_TB
    rt_shape = jax.ShapeDtypeStruct((PEER_HEADS, PEER_KEYS, n_tok), F32)
    gate_shape = jax.ShapeDtypeStruct((PEER_HEADS, PEER_KEYS, n_tok), PEER_GATE_DTYPE)
    rt_spec = pl.BlockSpec((PEER_HEADS, PEER_KEYS, tb), lambda i: (0, 0, i))
    return pl.pallas_call(
        _peer_route_kernel,
        grid=(n_tok // tb,),
        in_specs=[
            pl.BlockSpec((tb, D_MODEL), lambda i: (i, 0)),
            pl.BlockSpec((1, D_MODEL), lambda i: (0, 0)),
            pl.BlockSpec((PEER_HEADS * PEER_QDIM, D_MODEL), lambda i: (0, 0)),
            pl.BlockSpec((2 * PEER_HEADS, PEER_KEYS, PEER_HALF), lambda i: (0, 0, 0)),
        ],
        out_specs=[
            pl.BlockSpec((D_MODEL, tb), lambda i: (0, i)),
            rt_spec, rt_spec, rt_spec, rt_spec,
        ],
        out_shape=[
            jax.ShapeDtypeStruct((D_MODEL, n_tok), MXU_DTYPE),
            rt_shape, gate_shape, rt_shape, gate_shape,
        ],
        scratch_shapes=[
            pltpu.VMEM((PEER_HEADS * PEER_QDIM, tb), MXU_DTYPE),
            pltpu.VMEM((2 * PEER_TOPK, tb), F32),
            pltpu.VMEM((PEER_CAND_ROWS, tb), F32),
            pltpu.VMEM((PEER_TOPK, tb), F32),
        ],
        compiler_params=pltpu.CompilerParams(
            dimension_semantics=("arbitrary",), vmem_limit_bytes=VMEM_LIMIT_BYTES),
        name="peer_route",
    )(x2d, g, wqt, keys)


def _peer_expert_kernel(x_ref, xnt_ref, cnt_ref, rank_ref, e1_ref, e2_ref, u_ref, vt_ref, fg_ref,
                        out_ref, acc_scr, w_scr, *, final_norm):
    c = pl.program_id(1)
    n_chunks = pl.num_programs(1) - 1
    n_i1 = PEER_EC // PEER_KEYS
    tb = acc_scr.shape[1]
    gdt = PEER_GATE_DTYPE

    @pl.when(c == 0)
    def _():
        acc_scr[...] = jnp.zeros_like(acc_scr)
        w_scr[1] = jnp.zeros(w_scr.shape[1:], w_scr.dtype)

    acc_scr[...] += jnp.dot(vt_ref[...], w_scr[(c + 1) % 2], preferred_element_type=F32)
    hid = jnp.dot(u_ref[...], xnt_ref[...], preferred_element_type=F32)
    slot = c % 2
    n_groups = PEER_KEYS // PEER_GATE_ROWS
    for i1l in range(n_i1):
        for l0 in range(0, tb, LANES):
            lanes = slice(l0, l0 + LANES)
            gates = [jnp.zeros((PEER_GATE_ROWS, LANES), gdt) for _ in range(n_groups)]
            for h in range(PEER_HEADS):
                cntb = jnp.broadcast_to(cnt_ref[h, i1l:i1l + 1, lanes], (PEER_GATE_ROWS, LANES)).astype(gdt)
                e1b = jnp.broadcast_to(e1_ref[h, i1l:i1l + 1, lanes], (PEER_GATE_ROWS, LANES)).astype(gdt)
                for k in range(n_groups):
                    rows = slice(k * PEER_GATE_ROWS, (k + 1) * PEER_GATE_ROWS)
                    w = e2_ref[h, rows, lanes] * e1b
                    gates[k] = gates[k] + jnp.where(rank_ref[h, rows, lanes] < cntb, w, jnp.zeros_like(w))
            for k in range(n_groups):
                e0 = i1l * PEER_KEYS + k * PEER_GATE_ROWS
                act = _gelu_exact(hid[e0:e0 + PEER_GATE_ROWS, lanes]).astype(gdt)
                w_scr[slot, e0:e0 + PEER_GATE_ROWS, lanes] = (act * gates[k]).astype(MXU_DTYPE)

    @pl.when(c == n_chunks)
    def _():
        y = x_ref[...] + acc_scr[...].T
        if final_norm:
            y = y * lax.rsqrt(jnp.mean(y * y, axis=-1, keepdims=True) + EPS) * fg_ref[...]
        out_ref[...] = y


def _peer_experts(x2d, xnt, cnt, rank2, e1, e2, u, vt, final_g, final_norm):
    n_tok = x2d.shape[0]
    tb, ec = PEER_TB, PEER_EC
    n_chunks = PEER_EXPERTS // ec
    rt_spec = pl.BlockSpec((PEER_HEADS, PEER_KEYS, tb), lambda j, c: (0, 0, j))
    row_spec = pl.BlockSpec((PEER_HEADS, ec // PEER_KEYS, tb),
                            lambda j, c: (0, jnp.minimum(c, n_chunks - 1), j))
    return pl.pallas_call(
        functools.partial(_peer_expert_kernel, final_norm=final_norm),
        grid=(n_tok // tb, n_chunks + 1),
        in_specs=[
            pl.BlockSpec((tb, D_MODEL), lambda j, c: (j, 0)),
            pl.BlockSpec((D_MODEL, tb), lambda j, c: (0, j)),
            row_spec, rt_spec, row_spec, rt_spec,
            pl.BlockSpec((ec, D_MODEL), lambda j, c: (jnp.minimum(c, n_chunks - 1), 0)),
            pl.BlockSpec((D_MODEL, ec), lambda j, c: (0, jnp.maximum(c - 1, 0))),
            pl.BlockSpec((1, D_MODEL), lambda j, c: (0, 0)),
        ],
        out_specs=pl.BlockSpec((tb, D_MODEL), lambda j, c: (j, 0)),
        out_shape=jax.ShapeDtypeStruct((n_tok, D_MODEL), F32),
        scratch_shapes=[
            pltpu.VMEM((D_MODEL, tb), F32),
            pltpu.VMEM((2, ec, tb), MXU_DTYPE),
        ],
        compiler_params=pltpu.CompilerParams(
            dimension_semantics=("arbitrary", "arbitrary"), vmem_limit_bytes=VMEM_LIMIT_BYTES),
        name="peer_experts",
    )(x2d, xnt, cnt, rank2, e1, e2, u, vt, final_g.reshape(1, D_MODEL))


def peer_block(x2d, norm_g, w_q, keys, u, v, final_g, final_norm=False):
    wqt = w_q.T.astype(MXU_DTYPE)
    keys2 = keys.reshape(2 * PEER_HEADS, PEER_KEYS, PEER_HALF).astype(MXU_DTYPE)
    xnt, cnt, rank2, e1, e2 = _peer_route(x2d, norm_g.reshape(1, D_MODEL), wqt, keys2)
    return _peer_experts(x2d, xnt, cnt, rank2, e1, e2, u, v, final_g, final_norm)


MLA_HG = LANES
MLA_QK_W = MLA_HEADS * MLA_HG
MLA_V_W = MLA_HEADS * MLA_V
MLA_PREP_TOK = 512
MLA_TQ = 512
MLA_TK = 512
MLA_RHALF = MLA_ROPE // 2


def _mla_rope_tables(s):
    inv = ROPE_THETA ** (-jnp.arange(MLA_RHALF, dtype=F32) / MLA_RHALF)
    ang = jnp.arange(s).astype(F32)[:, None] * inv[None, :]
    cos, sin = jnp.cos(ang), jnp.sin(ang)
    zero_pad = jnp.zeros((s, MLA_HG - MLA_NOPE - MLA_ROPE), F32)
    cos_rot = jnp.concatenate([cos, cos, zero_pad], axis=1)
    sin_rot = jnp.concatenate([-sin, sin, zero_pad], axis=1)
    scale = np.float32((MLA_NOPE + MLA_ROPE) ** -0.5)
    q_cos = scale * jnp.concatenate([jnp.ones((s, MLA_NOPE), F32), cos_rot], axis=1)
    q_sin = scale * jnp.concatenate([jnp.zeros((s, MLA_NOPE), F32), sin_rot], axis=1)
    k_cos = jnp.concatenate([jnp.zeros((s, MLA_NOPE), F32), cos_rot], axis=1)
    k_sin = jnp.concatenate([jnp.zeros((s, MLA_NOPE), F32), sin_rot], axis=1)
    return q_cos, q_sin, k_cos, k_sin


def _mla_pack_weights(w_qb, w_kvb):
    hd_q = MLA_NOPE + MLA_ROPE
    wq = w_qb.reshape(MLA_Q_RANK, MLA_HEADS, hd_q)
    rot = wq[:, :, MLA_NOPE:]
    rot_sw = jnp.concatenate([rot[:, :, MLA_RHALF:], rot[:, :, :MLA_RHALF]], axis=2)
    pad = jnp.zeros((MLA_Q_RANK, MLA_HEADS, MLA_HG - hd_q), F32)
    wq_a = jnp.concatenate([wq, pad], axis=2).reshape(MLA_Q_RANK, MLA_QK_W)
    wq_b = jnp.concatenate([jnp.zeros_like(wq[:, :, :MLA_NOPE]), rot_sw, pad], axis=2).reshape(MLA_Q_RANK, MLA_QK_W)
    wkv = w_kvb.reshape(MLA_KV_RANK, MLA_HEADS, MLA_NOPE + MLA_V)
    wk = jnp.concatenate([wkv[:, :, :MLA_NOPE], jnp.zeros((MLA_KV_RANK, MLA_HEADS, MLA_HG - MLA_NOPE), F32)],
                         axis=2).reshape(MLA_KV_RANK, MLA_QK_W)
    wv = jnp.concatenate([wkv[:, :, MLA_NOPE:], jnp.zeros((MLA_KV_RANK, MLA_HEADS, MLA_HG - MLA_V), F32)],
                         axis=2).reshape(MLA_KV_RANK, MLA_QK_W)
    return (wq_a.astype(MXU_DTYPE), wq_b.astype(MXU_DTYPE), wk.astype(MXU_DTYPE), wv.astype(MXU_DTYPE))


def _mla_prep_kernel(cq_ref, ckv_ref, kr_ref, qn_ref, kvn_ref, wqa_ref, wqb_ref, wk_ref, wv_ref,
                     qcos_ref, qsin_ref, kcos_ref, ksin_ref, q_out, kt_out, v_out):
    cq = cq_ref[0]
    cqn = (cq * lax.rsqrt(jnp.mean(cq * cq, axis=-1, keepdims=True) + EPS) * qn_ref[...]).astype(MXU_DTYPE)
    qa = jnp.dot(cqn, wqa_ref[...], preferred_element_type=F32)
    qb = jnp.dot(cqn, wqb_ref[...], preferred_element_type=F32)
    ckv = ckv_ref[0]
    ckvn = (ckv * lax.rsqrt(jnp.mean(ckv * ckv, axis=-1, keepdims=True) + EPS) * kvn_ref[...]).astype(MXU_DTYPE)
    ka = jnp.dot(ckvn, wk_ref[...], preferred_element_type=F32)
    one_lane = (lax.broadcasted_iota(jnp.int32, (1, MLA_QK_W), 1) % MLA_HG == MLA_V).astype(F32)
    v_out[0] = (jnp.dot(ckvn, wv_ref[...], preferred_element_type=F32) + one_lane).astype(MXU_DTYPE)
    kr = kr_ref[0]
    k_rot = kr[:, 0:MLA_HG] * kcos_ref[...] + kr[:, MLA_HG:2 * MLA_HG] * ksin_ref[...]
    for h in range(MLA_HEADS):
        lanes = slice(h * MLA_HG, (h + 1) * MLA_HG)
        q_out[0, :, lanes] = (qa[:, lanes] * qcos_ref[...] + qb[:, lanes] * qsin_ref[...]).astype(MXU_DTYPE)
        kt_out[0, lanes, :] = (ka[:, lanes] + k_rot).T.astype(MXU_DTYPE)


def _mla_flash_kernel(q_ref, kt_ref, v_ref, out_ref, m_scr, acc_scr):
    ki = pl.program_id(2)

    @pl.when(ki == 0)
    def _():
        m_scr[...] = jnp.full(m_scr.shape, -jnp.inf, F32)
        acc_scr[...] = jnp.zeros(acc_scr.shape, F32)

    def logits(h):
        lanes = slice(h * MLA_HG, (h + 1) * MLA_HG)
        return jnp.dot(q_ref[0, :, lanes], kt_ref[0, lanes, :], preferred_element_type=F32)

    n_rep = kt_ref.shape[2] // MLA_HG
    s_next = logits(0)
    for h in range(MLA_HEADS):
        s = s_next
        if h + 1 < MLA_HEADS:
            s_next = logits(h + 1)
        m_old = m_scr[h]
        m_new = jnp.maximum(m_old, jnp.max(s, axis=-1, keepdims=True))
        p = jnp.exp(s - jnp.tile(m_new, (1, n_rep))).astype(MXU_DTYPE)
        acc_scr[h] = jnp.exp(m_old - m_new) * acc_scr[h] + jnp.dot(
            p, v_ref[0, :, h * MLA_HG:(h + 1) * MLA_HG], preferred_element_type=F32)
        m_scr[h] = m_new

    @pl.when(ki == pl.num_programs(2) - 1)
    def _():
        low = lax.broadcasted_iota(jnp.int32, (acc_scr.shape[1], MLA_HG), 1) < MLA_V
        outs = []
        for h in range(MLA_HEADS):
            acc = acc_scr[h]
            outs.append(acc / acc[:, MLA_V:MLA_V + 1])
        for hp in range(MLA_HEADS // 2):
            odd = pltpu.roll(outs[2 * hp + 1], MLA_V, axis=1)
            out_ref[0, :, hp * MLA_HG:(hp + 1) * MLA_HG] = jnp.where(low, outs[2 * hp], odd)


def mla_rotary_key_columns(w):
    zl = jnp.zeros(w.shape[:-1] + (MLA_NOPE,), w.dtype)
    zr = jnp.zeros(w.shape[:-1] + (MLA_HG - MLA_NOPE - MLA_ROPE,), w.dtype)
    w_sw = jnp.concatenate([w[..., MLA_RHALF:], w[..., :MLA_RHALF]], axis=-1)
    return jnp.concatenate([zl, w, zr, zl, w_sw, zr], axis=-1)


def mla_mixer(arr, q_norm, w_qb, kv_norm, w_kvb, col_q=0, col_kv=MLA_Q_RANK, col_kr=MLA_Q_RANK + MLA_KV_RANK):
    b, s, _ = arr.shape
    tt = MLA_PREP_TOK
    assert col_q % MLA_Q_RANK == 0 and col_kv % MLA_KV_RANK == 0 and col_kr % (2 * MLA_HG) == 0
    wqa, wqb, wk, wv = _mla_pack_weights(w_qb, w_kvb)
    q_cos, q_sin, k_cos, k_sin = _mla_rope_tables(s)
    tok = lambda w, c=0: pl.BlockSpec((1, tt, w), lambda i, j: (i, j, c // w))
    full = lambda r, c: pl.BlockSpec((r, c), lambda i, j: (0, 0))
    tab = pl.BlockSpec((tt, MLA_HG), lambda i, j: (j, 0))
    q, kt, v = pl.pallas_call(
        _mla_prep_kernel,
        grid=(b, s // tt),
        in_specs=[tok(MLA_Q_RANK, col_q), tok(MLA_KV_RANK, col_kv), tok(2 * MLA_HG, col_kr),
                  full(1, MLA_Q_RANK), full(1, MLA_KV_RANK),
                  full(MLA_Q_RANK, MLA_QK_W), full(MLA_Q_RANK, MLA_QK_W),
                  full(MLA_KV_RANK, MLA_QK_W), full(MLA_KV_RANK, MLA_QK_W),
                  tab, tab, tab, tab],
        out_specs=[tok(MLA_QK_W), pl.BlockSpec((1, MLA_QK_W, tt), lambda i, j: (i, 0, j)), tok(MLA_QK_W)],
        out_shape=[jax.ShapeDtypeStruct((b, s, MLA_QK_W), MXU_DTYPE),
                   jax.ShapeDtypeStruct((b, MLA_QK_W, s), MXU_DTYPE),
                   jax.ShapeDtypeStruct((b, s, MLA_QK_W), MXU_DTYPE)],
        compiler_params=pltpu.CompilerParams(
            dimension_semantics=("arbitrary", "arbitrary"), vmem_limit_bytes=VMEM_LIMIT_BYTES),
        name="mla_prep",
    )(arr, arr, arr, q_norm.reshape(1, -1), kv_norm.reshape(1, -1), wqa, wqb, wk, wv,
      q_cos, q_sin, k_cos, k_sin)
    tq, tk = MLA_TQ, MLA_TK
    return pl.pallas_call(
        _mla_flash_kernel,
        grid=(b, s // tq, s // tk),
        in_specs=[pl.BlockSpec((1, tq, MLA_QK_W), lambda i, j, kk: (i, j, 0)),
                  pl.BlockSpec((1, MLA_QK_W, tk), lambda i, j, kk: (i, 0, kk)),
                  pl.BlockSpec((1, tk, MLA_QK_W), lambda i, j, kk: (i, kk, 0))],
        out_specs=pl.BlockSpec((1, tq, MLA_V_W), lambda i, j, kk: (i, j, 0)),
        out_shape=jax.ShapeDtypeStruct((b, s, MLA_V_W), F32),
        scratch_shapes=[pltpu.VMEM((MLA_HEADS, tq, MLA_HG), F32),
                        pltpu.VMEM((MLA_HEADS, tq, MLA_HG), F32)],
        compiler_params=pltpu.CompilerParams(
            dimension_semantics=("arbitrary", "arbitrary", "arbitrary"), vmem_limit_bytes=VMEM_LIMIT_BYTES),
        name="mla_flash",
    )(q, kt, v)


def t5_bucket(rel):
    nb = T5_BUCKETS // 2
    ret = np.where(rel > 0, nb, 0)
    n = np.abs(rel)
    max_exact = nb // 2
    large = max_exact + (np.log(np.maximum(n, 1) / max_exact) / np.log(T5_MAX_DIST / max_exact)
                         * (nb - max_exact)).astype(np.int64)
    large = np.minimum(large, nb - 1)
    return (ret + np.where(n < max_exact, n, large)).astype(np.int32)


DIL_DIM = DIL_HEADS * DIL_HD
DIL_HALF = 64
DIL_QB = 128
DIL_KW = DIL_QB + 2 * DIL_HALF
DIL_TL = 512
DIL_N_GROUPS = len(DIL_GROUPS)


def _dil_bias_table(t5_table, gi, dil):
    rel = np.arange(DIL_KW)[None, :] - DIL_HALF - np.arange(DIL_QB)[:, None]
    bias = t5_table[:, gi * DIL_HEADS:(gi + 1) * DIL_HEADS][t5_bucket(rel * dil)].astype(F32)
    bias = jnp.where((np.abs(rel) <= DIL_HALF)[:, :, None], bias, -jnp.inf)
    return bias.transpose(2, 0, 1).reshape(DIL_HEADS * DIL_QB, DIL_KW)


def _dil_kernel(prev_ref, cur_ref, next_ref, tab_ref, o_ref, lse_ref, k_scr, v_scr, *, seq_len):
    step = pl.program_id(2)
    for i, ref in enumerate((prev_ref, cur_ref, next_ref)):
        k_scr[i * DIL_TL:(i + 1) * DIL_TL, :] = ref[0, :, DIL_DIM:2 * DIL_DIM].astype(MXU_DTYPE)
        v_scr[i * DIL_TL:(i + 1) * DIL_TL, :] = ref[0, :, 2 * DIL_DIM:3 * DIL_DIM].astype(MXU_DTYPE)
    lane_head = lax.broadcasted_iota(jnp.int32, (DIL_QB, DIL_DIM), 1) // DIL_HD
    key_off = lax.broadcasted_iota(jnp.int32, (1, DIL_KW), 1)
    for n in range(DIL_TL // DIL_QB):
        w0 = DIL_TL + n * DIL_QB - DIL_HALF
        kpos = step * DIL_TL + (n * DIL_QB - DIL_HALF) + key_off
        valid = (kpos >= 0) & (kpos < seq_len)
        q = cur_ref[0, n * DIL_QB:(n + 1) * DIL_QB, 0:DIL_DIM] * np.float32(DIL_HD ** -0.5)
        qs = jnp.concatenate([jnp.where(lane_head == h, q, 0.0) for h in range(DIL_HEADS)], axis=0)
        logits = lax.dot_general(qs.astype(MXU_DTYPE), k_scr[w0:w0 + DIL_KW, :], (((1,), (1,)), ((), ())),
                                 preferred_element_type=F32) + tab_ref[...]
        logits = jnp.where(valid, logits, -jnp.inf)
        m = jnp.max(logits, axis=-1, keepdims=True)
        p = jnp.exp(logits - m)
        denom = jnp.sum(p, axis=-1, keepdims=True)
        o_all = jnp.dot(p.astype(MXU_DTYPE), v_scr[w0:w0 + DIL_KW, :], preferred_element_type=F32) / denom
        lse_all = m + jnp.log(denom)
        o = jnp.zeros((DIL_QB, DIL_DIM), F32)
        lse = jnp.zeros((DIL_QB, DIL_DIM), F32)
        for h in range(DIL_HEADS):
            rows = slice(h * DIL_QB, (h + 1) * DIL_QB)
            o = o + jnp.where(lane_head == h, o_all[rows, :], 0.0)
            lse = lse + jnp.where(lane_head == h, lse_all[rows, :], 0.0)
        o_ref[0, n * DIL_QB:(n + 1) * DIL_QB, :] = o
        lse_ref[0, n * DIL_QB:(n + 1) * DIL_QB, :] = lse


def _dil_group(qkv, t5_table, gi, dil, col):
    b, s, width = qkv.shape
    seq_len = s // dil
    assert seq_len % DIL_TL == 0 and width % (3 * DIL_DIM) == 0 and col % (3 * DIL_DIM) == 0
    n_steps = seq_len // DIL_TL
    gi_col = col // (3 * DIL_DIM) + gi
    if dil > 1:
        qkv = qkv[:, :, gi_col * 3 * DIL_DIM:(gi_col + 1) * 3 * DIL_DIM]
        width, gi_col = 3 * DIL_DIM, 0
    n_col = width // (3 * DIL_DIM)
    view = qkv.reshape(b, seq_len, dil * width)
    blk = (1, DIL_TL, 3 * DIL_DIM)
    out_shape = jax.ShapeDtypeStruct((b, seq_len, dil * DIL_DIM), F32)
    out_spec = pl.BlockSpec((1, DIL_TL, DIL_DIM), lambda i, r, j: (i, j, r))
    o, lse = pl.pallas_call(
        functools.partial(_dil_kernel, seq_len=seq_len),
        grid=(b, dil, n_steps),
        in_specs=[
            pl.BlockSpec(blk, lambda i, r, j: (i, jnp.maximum(j - 1, 0), r * n_col + gi_col)),
            pl.BlockSpec(blk, lambda i, r, j: (i, j, r * n_col + gi_col)),
            pl.BlockSpec(blk, lambda i, r, j: (i, jnp.minimum(j + 1, n_steps - 1), r * n_col + gi_col)),
            pl.BlockSpec((DIL_HEADS * DIL_QB, DIL_KW), lambda i, r, j: (0, 0)),
        ],
        out_specs=[out_spec, out_spec],
        out_shape=[out_shape, out_shape],
        scratch_shapes=[pltpu.VMEM((3 * DIL_TL, DIL_DIM), MXU_DTYPE),
                        pltpu.VMEM((3 * DIL_TL, DIL_DIM), MXU_DTYPE)],
        compiler_params=pltpu.CompilerParams(
            dimension_semantics=("arbitrary", "arbitrary", "arbitrary"), vmem_limit_bytes=VMEM_LIMIT_BYTES),
        name=f"dilated_attention_g{gi}",
    )(view, view, view, _dil_bias_table(t5_table, gi, dil))
    return o.reshape(b, s, DIL_DIM), lse.reshape(b, s, DIL_DIM)


def _dil_combine_kernel(*refs):
    o_refs, lse_refs, out_ref = refs[:DIL_N_GROUPS], refs[DIL_N_GROUPS:2 * DIL_N_GROUPS], refs[-1]
    lses = [r[...] for r in lse_refs]
    m = functools.reduce(jnp.maximum, lses)
    ws = [jnp.exp(l - m) for l in lses]
    total = functools.reduce(jnp.add, ws)
    acc = functools.reduce(jnp.add, [w * r[...] for w, r in zip(ws, o_refs)])
    out_ref[...] = acc / total


def dilated_mixer(qkv, t5_table, col=0):
    b, s, _ = qkv.shape
    outs, lses = [], []
    for gi, (win, dil) in enumerate(DIL_GROUPS):
        assert win // (2 * dil) == DIL_HALF
        o, lse = _dil_group(qkv, t5_table, gi, dil, col)
        outs.append(o.reshape(b * s, DIL_DIM))
        lses.append(lse.reshape(b * s, DIL_DIM))
    tm = 1024
    spec = pl.BlockSpec((tm, DIL_DIM), lambda i: (i, 0))
    out = pl.pallas_call(
        _dil_combine_kernel,
        grid=(b * s // tm,),
        in_specs=[spec] * (2 * DIL_N_GROUPS),
        out_specs=spec,
        out_shape=jax.ShapeDtypeStruct((b * s, DIL_DIM), F32),
        compiler_params=pltpu.CompilerParams(
            dimension_semantics=("arbitrary",), vmem_limit_bytes=VMEM_LIMIT_BYTES),
        name="dilated_combine",
    )(*outs, *lses)
    return out.reshape(b, s, DIL_DIM)


SSM_CONV_TOK = 512
SSM_HALO = SUBLANES
SSM_HEADS_PER_GROUP = SSM_HEADS // SSM_GROUPS
SSM_GROUP_W = SSM_HEADS_PER_GROUP * SSM_HD
SSM_BC_W = SSM_GROUPS * SSM_STATE


def _softplus(x):
    return jnp.maximum(x, 0.0) + jnp.log1p(jnp.exp(-jnp.abs(x)))


def _ssm_conv_kernel(prev_ref, cur_ref, next_ref, w_ref, b_ref, out_ref, cat_scr):
    j = pl.program_id(1)
    tl = cur_ref.shape[1]
    cat_scr[0:SSM_HALO, :] = jnp.where(j > 0, prev_ref[0], 0.0)
    cat_scr[SSM_HALO:SSM_HALO + tl, :] = cur_ref[0]
    cat_scr[SSM_HALO + tl:, :] = jnp.where(j < pl.num_programs(1) - 1, next_ref[0], 0.0)
    acc = jnp.zeros((tl, CONV_CH), F32) + b_ref[...]
    for k in range(SSM_CONV):
        off = SSM_HALO + k - SSM_CONV // 2
        acc = acc + cat_scr[off:off + tl, :] * w_ref[k:k + 1, :]
    out_ref[0] = acc * jax.nn.sigmoid(acc)


def _ssm_conv(xbc, conv_w, conv_b, col):
    b, l, _ = xbc.shape
    tl = SSM_CONV_TOK
    n_steps = l // tl
    per = tl // SSM_HALO
    assert col % CONV_CH == 0
    cb = col // CONV_CH
    return pl.pallas_call(
        _ssm_conv_kernel,
        grid=(b, n_steps),
        in_specs=[
            pl.BlockSpec((1, SSM_HALO, CONV_CH), lambda i, j: (i, jnp.maximum(j * per - 1, 0), cb)),
            pl.BlockSpec((1, tl, CONV_CH), lambda i, j: (i, j, cb)),
            pl.BlockSpec((1, SSM_HALO, CONV_CH), lambda i, j: (i, jnp.minimum((j + 1) * per, n_steps * per - 1), cb)),
            pl.BlockSpec((SSM_CONV, CONV_CH), lambda i, j: (0, 0)),
            pl.BlockSpec((1, CONV_CH), lambda i, j: (0, 0)),
        ],
        out_specs=pl.BlockSpec((1, tl, CONV_CH), lambda i, j: (i, j, 0)),
        out_shape=jax.ShapeDtypeStruct((b, l, CONV_CH), F32),
        scratch_shapes=[pltpu.VMEM((tl + 2 * SSM_HALO, CONV_CH), F32)],
        compiler_params=pltpu.CompilerParams(
            dimension_semantics=("arbitrary", "arbitrary"), vmem_limit_bytes=VMEM_LIMIT_BYTES),
        name="ssm_conv",
    )(xbc, xbc, xbc, conv_w, conv_b.reshape(1, CONV_CH))


def _ssd_kernel(xbc_ref, dt_ref, dtt_ref, expand_ref, bias_e_ref, a_e_ref, bias_t_ref, a_t_ref,
                y_ref, state_scr, *, reverse):
    q = SSM_CHUNK

    @pl.when(pl.program_id(1) == 0)
    def _():
        state_scr[...] = jnp.zeros(state_scr.shape, F32)

    hi = lax.Precision.HIGHEST
    xbc = xbc_ref[0]
    xs = xbc[:, 0:SSM_INNER]
    dt_e = _softplus(jnp.dot(dt_ref[0], expand_ref[...], precision=hi, preferred_element_type=F32)
                     + bias_e_ref[...])
    a_e = dt_e * a_e_ref[...]
    ri = lax.broadcasted_iota(jnp.int32, (q, q), 0)
    ci = lax.broadcasted_iota(jnp.int32, (q, q), 1)
    seen = (ci >= ri) if reverse else (ci <= ri)
    cs_e = jnp.dot(seen.astype(F32), a_e, precision=hi, preferred_element_type=F32)
    dt_t = _softplus(dtt_ref[0] + bias_t_ref[...])
    cs_t = jnp.dot(dt_t * a_t_ref[...], seen.T.astype(F32), precision=hi, preferred_element_type=F32)
    dtx = xs * dt_e
    last = 0 if reverse else q - 1
    total = cs_e[last:last + 1, :]
    dtx_decayed = jnp.exp(total - cs_e) * dtx
    grow = jnp.exp(cs_e)
    chunk_decay = jnp.exp(total)
    lane_head = lax.broadcasted_iota(jnp.int32, (q, SSM_GROUP_W), 1) // SSM_HD
    for g in range(SSM_GROUPS):
        xl = slice(g * SSM_GROUP_W, (g + 1) * SSM_GROUP_W)
        bg = xbc[:, SSM_INNER + g * SSM_STATE:SSM_INNER + (g + 1) * SSM_STATE]
        cg = xbc[:, SSM_INNER + SSM_BC_W + g * SSM_STATE:SSM_INNER + SSM_BC_W + (g + 1) * SSM_STATE]
        cb = lax.dot_general(cg.astype(MXU_DTYPE), bg.astype(MXU_DTYPE), (((1,), (1,)), ((), ())),
                             preferred_element_type=F32)
        ms = []
        for r in range(SSM_HEADS_PER_GROUP):
            h = g * SSM_HEADS_PER_GROUP + r
            col = cs_e[:, h * SSM_HD:h * SSM_HD + 1]
            row = cs_t[h:h + 1, :]
            ms.append(cb * jnp.where(seen, jnp.exp(col - row), 0.0))
        y_all = jnp.dot(jnp.concatenate(ms, axis=0).astype(MXU_DTYPE), dtx[:, xl].astype(MXU_DTYPE),
                        preferred_element_type=F32)
        y_diag = jnp.zeros((q, SSM_GROUP_W), F32)
        for r in range(SSM_HEADS_PER_GROUP):
            y_diag = y_diag + jnp.where(lane_head == r, y_all[r * q:(r + 1) * q, :], 0.0)
        s_in = state_scr[g]
        y_off = jnp.dot(cg.astype(MXU_DTYPE), s_in.astype(MXU_DTYPE), preferred_element_type=F32) * grow[:, xl]
        y_ref[0, :, xl] = y_diag + y_off
        new = jnp.dot(bg.T.astype(MXU_DTYPE), dtx_decayed[:, xl].astype(MXU_DTYPE), preferred_element_type=F32)
        state_scr[g] = s_in * chunk_decay[:, xl] + new


def _ssd_direction(xbc_act, dt_arr, dt_t, a_log, dt_bias, direction, col_dt, dt_w):
    b, l, _ = xbc_act.shape
    assert col_dt % dt_w == 0 and dt_w >= 2 * SSM_HEADS
    q = SSM_CHUNK
    nc = l // q
    reverse = direction == 1
    a = -jnp.exp(a_log[direction].astype(F32))
    bias = dt_bias[direction].astype(F32)
    head_of_lane = np.arange(SSM_INNER) // SSM_HD
    expand = (np.arange(dt_w)[:, None] == direction * SSM_HEADS + head_of_lane[None, :]).astype(np.float32)
    chunk = (lambda j: nc - 1 - j) if reverse else (lambda j: j)
    full = lambda r, c: pl.BlockSpec((r, c), lambda i, j: (0, 0))
    return pl.pallas_call(
        functools.partial(_ssd_kernel, reverse=reverse),
        grid=(b, nc),
        in_specs=[
            pl.BlockSpec((1, q, CONV_CH), lambda i, j: (i, chunk(j), 0)),
            pl.BlockSpec((1, q, dt_w), lambda i, j: (i, chunk(j), col_dt // dt_w)),
            pl.BlockSpec((1, SSM_HEADS, q), lambda i, j: (i * 2 + direction, 0, chunk(j))),
            full(dt_w, SSM_INNER), full(1, SSM_INNER), full(1, SSM_INNER),
            full(SSM_HEADS, 1), full(SSM_HEADS, 1),
        ],
        out_specs=pl.BlockSpec((1, q, SSM_INNER), lambda i, j: (i, chunk(j), 0)),
        out_shape=jax.ShapeDtypeStruct((b, l, SSM_INNER), F32),
        scratch_shapes=[pltpu.VMEM((SSM_GROUPS, SSM_STATE, SSM_GROUP_W), F32)],
        compiler_params=pltpu.CompilerParams(
            dimension_semantics=("arbitrary", "arbitrary"), vmem_limit_bytes=VMEM_LIMIT_BYTES),
        name=f"ssd_dir{direction}",
    )(xbc_act, dt_arr, dt_t, jnp.asarray(expand), bias[head_of_lane].reshape(1, SSM_INNER),
      a[head_of_lane].reshape(1, SSM_INNER), bias.reshape(SSM_HEADS, 1), a.reshape(SSM_HEADS, 1))


def _ssm_gate_kernel(yf_ref, yb_ref, xbc_ref, z_ref, d_ref, g_ref, out_ref):
    z = z_ref[...]
    y = (yf_ref[...] + yb_ref[...] + xbc_ref[...] * d_ref[...]) * (z * jax.nn.sigmoid(z))
    out_ref[...] = y * lax.rsqrt(jnp.mean(y * y, axis=-1, keepdims=True) + EPS) * g_ref[...]


def mamba2_mixer(arr, conv_w, conv_b, A_log, dt_bias, D_skip, norm_g,
                 col_z=0, col_xbc=CONV_CH, col_dt=SSM_INNER + CONV_CH, dt_w=2 * SSM_HEADS):
    b, l, width = arr.shape
    assert col_z % SSM_INNER == 0
    xbc_act = _ssm_conv(arr, conv_w, conv_b, col_xbc)
    dt_t = jnp.swapaxes(arr[:, :, col_dt:col_dt + 2 * SSM_HEADS], 1, 2).reshape(b * 2, SSM_HEADS, l)
    y_f = _ssd_direction(xbc_act, arr, dt_t, A_log, dt_bias, 0, col_dt, dt_w)
    y_b = _ssd_direction(xbc_act, arr, dt_t, A_log, dt_bias, 1, col_dt, dt_w)
    tm = 1024
    tok = lambda cb: pl.BlockSpec((tm, SSM_INNER), lambda i: (i, cb))
    row = pl.BlockSpec((1, SSM_INNER), lambda i: (0, 0))
    d_e = D_skip.astype(F32)[np.arange(SSM_INNER) // SSM_HD].reshape(1, SSM_INNER)
    out = pl.pallas_call(
        _ssm_gate_kernel,
        grid=(b * l // tm,),
        in_specs=[tok(0), tok(0), tok(0), tok(col_z // SSM_INNER), row, row],
        out_specs=tok(0),
        out_shape=jax.ShapeDtypeStruct((b * l, SSM_INNER), F32),
        compiler_params=pltpu.CompilerParams(
            dimension_semantics=("arbitrary",), vmem_limit_bytes=VMEM_LIMIT_BYTES),
        name="ssm_gate",
    )(y_f.reshape(b * l, SSM_INNER), y_b.reshape(b * l, SSM_INNER), xbc_act.reshape(b * l, CONV_CH),
      arr.reshape(b * l, width), d_e, norm_g.reshape(1, SSM_INNER))
    return out.reshape(b, l, SSM_INNER)


NA_DIM = NA_HEADS * NA_HD
NA_ROWS_PER_STEP = 8
NA_WIN = NA_ROWS * GRID_W
NA_STEP_TOK = NA_ROWS_PER_STEP * GRID_W


def _na_bias_table(rpb):
    n_dc = 2 * NA_COLS - 1
    edge_l = jnp.repeat(rpb[:, :, :1], GRID_W, axis=2)
    edge_r = jnp.repeat(rpb[:, :, -1:], GRID_W, axis=2)
    ext = jnp.concatenate([edge_l, rpb.astype(F32), edge_r], axis=2)
    by_col = jnp.stack([ext[:, :, GRID_W + NA_COLS - 1 - qc:2 * GRID_W + NA_COLS - 1 - qc]
                        for qc in range(GRID_W)], axis=2)
    qc = np.arange(GRID_W)[:, None]
    kc = np.arange(GRID_W)[None, :]
    cs = np.clip(qc - NA_COLS // 2, 0, GRID_W - NA_COLS)
    ok = (kc >= cs) & (kc < cs + NA_COLS)
    by_col = jnp.where(ok[None, None], by_col, -jnp.inf)
    tabs = []
    for delta in range(NA_ROWS):
        rows = by_col[:, NA_ROWS - 1 - delta:2 * NA_ROWS - 1 - delta]
        tabs.append(rows.transpose(0, 2, 1, 3).reshape(NA_HEADS * GRID_W, NA_WIN))
    assert n_dc == rpb.shape[2]
    return jnp.stack(tabs, axis=0)


def _na_kernel(prev_ref, cur_ref, next_ref, tab_ref, out_ref, k_scr, v_scr, *, n_rows):
    step = pl.program_id(1)
    for i, ref in enumerate((prev_ref, cur_ref, next_ref)):
        k_scr[i * NA_STEP_TOK:(i + 1) * NA_STEP_TOK, :] = ref[0, :, NA_DIM:2 * NA_DIM].astype(MXU_DTYPE)
        v_scr[i * NA_STEP_TOK:(i + 1) * NA_STEP_TOK, :] = ref[0, :, 2 * NA_DIM:3 * NA_DIM].astype(MXU_DTYPE)
    lane_head = lax.broadcasted_iota(jnp.int32, (GRID_W, NA_DIM), 1) // NA_HD
    row0 = step * NA_ROWS_PER_STEP
    for j in range(NA_ROWS_PER_STEP):
        r = row0 + j
        r0 = jnp.clip(r - NA_ROWS // 2, 0, n_rows - NA_ROWS)
        start = pl.multiple_of((r0 - row0 + NA_ROWS_PER_STEP) * GRID_W, GRID_W)
        q = cur_ref[0, j * GRID_W:(j + 1) * GRID_W, 0:NA_DIM] * np.float32(NA_HD ** -0.5)
        qs = jnp.concatenate([jnp.where(lane_head == h, q, 0.0) for h in range(NA_HEADS)], axis=0)
        kw = k_scr[pl.ds(start, NA_WIN), :]
        vw = v_scr[pl.ds(start, NA_WIN), :]
        logits = lax.dot_general(qs.astype(MXU_DTYPE), kw, (((1,), (1,)), ((), ())),
                                 preferred_element_type=F32) + tab_ref[r - r0]
        m = jnp.max(logits, axis=-1, keepdims=True)
        p = jnp.exp(logits - m)
        denom = jnp.sum(p, axis=-1, keepdims=True)
        o_all = jnp.dot(p.astype(MXU_DTYPE), vw, preferred_element_type=F32) / denom
        o = jnp.zeros((GRID_W, NA_DIM), F32)
        for h in range(NA_HEADS):
            o = o + jnp.where(lane_head == h, o_all[h * GRID_W:(h + 1) * GRID_W, :], 0.0)
        out_ref[0, j * GRID_W:(j + 1) * GRID_W, :] = o


def na_mixer(qkv, rpb, col=0):
    b, s, _ = qkv.shape
    n_rows = s // GRID_W
    assert n_rows >= NA_ROWS and n_rows % NA_ROWS_PER_STEP == 0 and col % (3 * NA_DIM) == 0
    n_steps = n_rows // NA_ROWS_PER_STEP
    blk = (1, NA_STEP_TOK, 3 * NA_DIM)
    cb = col // (3 * NA_DIM)
    return pl.pallas_call(
        functools.partial(_na_kernel, n_rows=n_rows),
        grid=(b, n_steps),
        in_specs=[
            pl.BlockSpec(blk, lambda i, j: (i, jnp.maximum(j - 1, 0), cb)),
            pl.BlockSpec(blk, lambda i, j: (i, j, cb)),
            pl.BlockSpec(blk, lambda i, j: (i, jnp.minimum(j + 1, n_steps - 1), cb)),
            pl.BlockSpec((NA_ROWS, NA_HEADS * GRID_W, NA_WIN), lambda i, j: (0, 0, 0)),
        ],
        out_specs=pl.BlockSpec((1, NA_STEP_TOK, NA_DIM), lambda i, j: (i, j, 0)),
        out_shape=jax.ShapeDtypeStruct((b, s, NA_DIM), F32),
        scratch_shapes=[
            pltpu.VMEM((3 * NA_STEP_TOK, NA_DIM), MXU_DTYPE),
            pltpu.VMEM((3 * NA_STEP_TOK, NA_DIM), MXU_DTYPE),
        ],
        compiler_params=pltpu.CompilerParams(
            dimension_semantics=("arbitrary", "arbitrary"), vmem_limit_bytes=VMEM_LIMIT_BYTES),
        name="na_attention",
    )(qkv, qkv, qkv, _na_bias_table(rpb))


PK_GATE = 0
PK_XBC = PK_GATE + N_BRANCH * D_MODEL
PK_CQ = PK_XBC + CONV_CH
PK_DIL = PK_CQ + MLA_Q_RANK
PK_NA = PK_DIL + DIL_N_GROUPS * 3 * DIL_DIM
PK_KR = PK_NA + 3 * NA_DIM
PK_Z = PK_KR + 2 * MLA_HG
PK_CKV = PK_Z + SSM_INNER
PK_DT = PK_CKV + MLA_KV_RANK
PK_DT_W = LANES
PK_WIDTH = 13 * 3 * DIL_DIM
INPROJ_TM = 512
INPROJ_TN = PK_WIDTH // 2
MERGE_TM = 512


def _pack_w_in(w_in_l):
    gate, a_cq, a_ckv, a_kr, b_qkv, c_z, c_xbc, c_dt, d_qkv = jnp.split(w_in_l, IN_SPLITS, axis=-1)
    zeros = lambda n: jnp.zeros((D_MODEL, n), w_in_l.dtype)
    cols = [gate, c_xbc, a_cq, b_qkv, d_qkv, mla_rotary_key_columns(a_kr), c_z, a_ckv,
            c_dt, zeros(PK_DT_W - 2 * SSM_HEADS)]
    packed = jnp.concatenate(cols, axis=-1)
    assert packed.shape[1] == PK_DT + PK_DT_W
    return jnp.concatenate([packed, zeros(PK_WIDTH - packed.shape[1])], axis=-1).astype(MXU_DTYPE)


def _inproj_kernel(x_ref, g_ref, w_ref, out_ref, h_scr):
    @pl.when(pl.program_id(1) == 0)
    def _():
        x = x_ref[...]
        h_scr[...] = (x * lax.rsqrt(jnp.mean(x * x, axis=-1, keepdims=True) + EPS) * g_ref[...]).astype(MXU_DTYPE)

    out_ref[...] = jnp.dot(h_scr[...], w_ref[...], preferred_element_type=F32)


def _inproj(x2d, norm_g, w_packed):
    n_tok = x2d.shape[0]
    tm, tn = INPROJ_TM, INPROJ_TN
    return pl.pallas_call(
        _inproj_kernel,
        grid=(n_tok // tm, PK_WIDTH // tn),
        in_specs=[pl.BlockSpec((tm, D_MODEL), lambda i, j: (i, 0)),
                  pl.BlockSpec((1, D_MODEL), lambda i, j: (0, 0)),
                  pl.BlockSpec((D_MODEL, tn), lambda i, j: (0, j))],
        out_specs=pl.BlockSpec((tm, tn), lambda i, j: (i, j)),
        out_shape=jax.ShapeDtypeStruct((n_tok, PK_WIDTH), F32),
        scratch_shapes=[pltpu.VMEM((tm, D_MODEL), MXU_DTYPE)],
        compiler_params=pltpu.CompilerParams(
            dimension_semantics=("arbitrary", "arbitrary"), vmem_limit_bytes=VMEM_LIMIT_BYTES),
        name="in_projection",
    )(x2d, norm_g.reshape(1, D_MODEL), w_packed)


def _merge_kernel(x_ref, gate_ref, bg_ref, ya_ref, yb_ref, yc_ref, yd_ref, wb_ref, wo_ref, out_ref):
    merged = jnp.zeros(x_ref.shape, F32)
    for i, y_ref in enumerate((ya_ref, yb_ref, yc_ref, yd_ref)):
        proj = jnp.dot(y_ref[...].astype(MXU_DTYPE), wb_ref[BRANCH_ROWS[i]:BRANCH_ROWS[i + 1], :],
                       preferred_element_type=F32)
        lanes = slice(i * D_MODEL, (i + 1) * D_MODEL)
        merged = merged + jax.nn.sigmoid(gate_ref[:, lanes] + bg_ref[:, lanes]) * proj
    out_ref[...] = x_ref[...] + jnp.dot(merged.astype(MXU_DTYPE), wo_ref[...], preferred_element_type=F32)


def _merge(x2d, packed, b_gate, ys, w_branch, w_out):
    n_tok = x2d.shape[0]
    tm = MERGE_TM
    tok = lambda w: pl.BlockSpec((tm, w), lambda i: (i, 0))
    full = lambda r, c: pl.BlockSpec((r, c), lambda i: (0, 0))
    n_gate = N_BRANCH * D_MODEL
    return pl.pallas_call(
        _merge_kernel,
        grid=(n_tok // tm,),
        in_specs=[tok(D_MODEL), tok(n_gate), full(1, n_gate)] + [tok(w) for w in BRANCH_WIDTHS]
                 + [full(BRANCH_ROWS[-1], D_MODEL), full(D_MODEL, D_MODEL)],
        out_specs=tok(D_MODEL),
        out_shape=jax.ShapeDtypeStruct((n_tok, D_MODEL), F32),
        compiler_params=pltpu.CompilerParams(
            dimension_semantics=("arbitrary",), vmem_limit_bytes=VMEM_LIMIT_BYTES),
        name="branch_merge",
    )(x2d, packed, b_gate.reshape(1, n_gate), *[y.reshape(n_tok, -1) for y in ys], w_branch, w_out)


def encoder(x, norm1_g, w_in_packed, b_gate, mla_q_norm, mla_w_qb, mla_kv_norm, mla_w_kvb, t5_table,
            ssm_conv_w, ssm_conv_b, ssm_A_log, ssm_dt_bias, ssm_D, ssm_norm_g, na_rpb,
            w_branch, w_out, norm2_g, peer_wq, peer_keys, peer_u, peer_vt, final_g):
    b, s, _ = x.shape
    x2d = x.reshape(b * s, D_MODEL)
    for l in range(DEPTH):
        packed = _inproj(x2d, norm1_g[l], w_in_packed[l])
        p3 = packed.reshape(b, s, PK_WIDTH)
        y_a = mla_mixer(p3, mla_q_norm[l], mla_w_qb[l], mla_kv_norm[l], mla_w_kvb[l],
                        col_q=PK_CQ, col_kv=PK_CKV, col_kr=PK_KR)
        y_b = dilated_mixer(p3, t5_table, col=PK_DIL)
        y_c = mamba2_mixer(p3, ssm_conv_w[l], ssm_conv_b[l], ssm_A_log[l], ssm_dt_bias[l], ssm_D[l],
                           ssm_norm_g[l], col_z=PK_Z, col_xbc=PK_XBC, col_dt=PK_DT, dt_w=PK_DT_W)
        y_d = na_mixer(p3, na_rpb[l], col=PK_NA)
        x2d = _merge(x2d, packed, b_gate[l], (y_a, y_b, y_c, y_d), w_branch[l], w_out[l])
        x2d = peer_block(x2d, norm2_g[l], peer_wq[l], peer_keys[l], peer_u[l], peer_vt[l],
                         final_g, final_norm=(l == DEPTH - 1))
    return x2d.reshape(b, s, D_MODEL)


def kernel(x_prompt, x_sample, norm1_g, w_in, b_gate, mla_q_norm, mla_w_qb, mla_kv_norm, mla_w_kvb, t5_table, ssm_conv_w, ssm_conv_b, ssm_A_log, ssm_dt_bias, ssm_D, ssm_norm_g, na_rpb, w_branch, w_out, norm2_g, peer_wq, peer_keys, peer_u, peer_v, final_g):
    peer_u16 = peer_u.astype(MXU_DTYPE)
    peer_vt16 = jnp.swapaxes(peer_v, 1, 2).astype(MXU_DTYPE)
    w_in_packed = jnp.stack([_pack_w_in(w_in[l]) for l in range(DEPTH)])
    shared = (norm1_g, w_in_packed, b_gate, mla_q_norm, mla_w_qb, mla_kv_norm, mla_w_kvb, t5_table,
              ssm_conv_w, ssm_conv_b, ssm_A_log, ssm_dt_bias, ssm_D, ssm_norm_g, na_rpb,
              w_branch.astype(MXU_DTYPE), w_out.astype(MXU_DTYPE), norm2_g, peer_wq, peer_keys,
              peer_u16, peer_vt16, final_g)
    y_prompt = encoder(x_prompt, *shared)
    y_sample = encoder(x_sample, *shared)
    return (y_prompt, y_sample)
```

```python
import functools
import math

import numpy as np
import jax
import jax.numpy as jnp
from jax import lax
from jax.experimental import pallas as pl
from jax.experimental.pallas import tpu as pltpu

F32 = jnp.float32
BF16 = jnp.bfloat16
MXU_DTYPE = BF16

D_MODEL = 1024
DEPTH = 2
GRID_W = 64
EPS = 1e-6
N_BRANCH = 4

MLA_HEADS = 4
MLA_Q_RANK = 256
MLA_KV_RANK = 128
MLA_NOPE = 64
MLA_ROPE = 32
MLA_V = 64
ROPE_THETA = 10000.0
Q_BLOCK = 128

DIL_GROUPS = ((128, 1), (512, 4), (2048, 16))
DIL_HEADS = 4
DIL_HD = 64
T5_BUCKETS = 32
T5_MAX_DIST = 1024

SSM_HEADS = 8
SSM_HD = 64
SSM_INNER = SSM_HEADS * SSM_HD
SSM_GROUPS = 2
SSM_STATE = 128
SSM_CONV = 7
SSM_CHUNK = 128
CONV_CH = SSM_INNER + 2 * SSM_GROUPS * SSM_STATE

NA_HEADS = 4
NA_HD = 64
NA_ROWS = 8
NA_COLS = 16
NA_QCB = 16
NA_KCB = NA_QCB + NA_COLS

PEER_HEADS = 8
PEER_KEYS = 128
PEER_EXPERTS = PEER_KEYS * PEER_KEYS
PEER_QDIM = 256
PEER_TOPK = 16
PEER_TOK_BLOCK = 128

BRANCH_WIDTHS = (MLA_HEADS * MLA_V, DIL_HEADS * DIL_HD, SSM_INNER, NA_HEADS * NA_HD)
BRANCH_ROWS = tuple(sum(BRANCH_WIDTHS[:i]) for i in range(N_BRANCH + 1))
IN_SIZES = (N_BRANCH * D_MODEL, MLA_Q_RANK, MLA_KV_RANK, MLA_ROPE,
            len(DIL_GROUPS) * 3 * DIL_HEADS * DIL_HD,
            SSM_INNER, CONV_CH, 2 * SSM_HEADS,
            3 * NA_HEADS * NA_HD)
IN_SPLITS = tuple(sum(IN_SIZES[:i + 1]) for i in range(len(IN_SIZES) - 1))

VMEM_LIMIT_BYTES = 56 * 1024 * 1024
LANES = 128
SUBLANES = 8


def rms_norm(x, g):
    x32 = x.astype(F32)
    y = x32 * lax.rsqrt(jnp.mean(x32 * x32, axis=-1, keepdims=True) + EPS)
    return (y * g.astype(F32)).astype(x.dtype)


PEER_ROUTE_TB = 256
PEER_TB = 512
PEER_EC = 1024
PEER_GATE_DTYPE = BF16
PEER_GATE_ROWS = 16
PEER_PIECE_ROWS = 256
PEER_HALF = PEER_QDIM // 2
PEER_CAND_ROWS = 2 * SUBLANES + 7 * SUBLANES + SUBLANES


def _gelu_exact(x):
    return 0.5 * x * (1.0 + lax.erf(x * np.float32(math.sqrt(0.5))))


def _extract_desc(vals, n_out, out_ref, row0, with_rank=False):
    rank = jnp.full(vals.shape, float(n_out), F32) if with_rank else None
    for k in range(n_out):
        m = jnp.max(vals, axis=0, keepdims=True)
        out_ref[pl.ds(row0 + k, 1), :] = m
        hit = vals == m
        if with_rank:
            rank = jnp.where(hit, float(k), rank)
        vals = jnp.where(hit, -jnp.inf, vals)
    return rank


def _peer_route_kernel(x_ref, g_ref, wqt_ref, keys_ref,
                       xnt_ref, cnt_ref, rank_ref, e1_ref, e2_ref,
                       qt_scr, top_scr, cand_scr, tops_scr):
    x = x_ref[...]
    xn = x * lax.rsqrt(jnp.mean(x * x, axis=-1, keepdims=True) + EPS) * g_ref[...]
    xnt = xn.T.astype(MXU_DTYPE)
    xnt_ref[...] = xnt
    qt_scr[...] = jnp.dot(wqt_ref[...], xnt, preferred_element_type=F32).astype(MXU_DTYPE)

    def head(h, carry):
        q1 = qt_scr[pl.ds(pl.multiple_of(h * PEER_QDIM, PEER_QDIM), PEER_HALF), :]
        q2 = qt_scr[pl.ds(pl.multiple_of(h * PEER_QDIM + PEER_HALF, PEER_HALF), PEER_HALF), :]
        s1 = jnp.dot(keys_ref[2 * h], q1, preferred_element_type=F32)
        s2 = jnp.dot(keys_ref[2 * h + 1], q2, preferred_element_type=F32)
        _extract_desc(s1, PEER_TOPK, top_scr, 0)
        rank2 = _extract_desc(s2, PEER_TOPK, top_scr, PEER_TOPK, with_rank=True)
        t1 = top_scr[0:PEER_TOPK, :]
        t2 = top_scr[PEER_TOPK:2 * PEER_TOPK, :]
        cand_scr[0:2 * SUBLANES, :] = t1[0:1, :] + t2
        for a in range(1, SUBLANES):
            cand_scr[(a + 1) * SUBLANES:(a + 2) * SUBLANES, :] = t1[a:a + 1, :] + t2[0:SUBLANES, :]
        cand_scr[9 * SUBLANES:10 * SUBLANES, :] = t1[SUBLANES:2 * SUBLANES, :] + t2[0:1, :]
        _extract_desc(cand_scr[...], PEER_TOPK, tops_scr, 0)
        top_s = tops_scr[...]
        z = jnp.sum(jnp.exp(top_s - top_s[0:1, :]), axis=0, keepdims=True)
        tau = top_s[PEER_TOPK - 1:PEER_TOPK, :]
        cnt = jnp.zeros(s1.shape, F32)
        for b in range(PEER_TOPK):
            cnt = cnt + jnp.where(s1 + t2[b:b + 1, :] >= tau, 1.0, 0.0)
        cnt_ref[h] = cnt
        rank_ref[h] = rank2.astype(rank_ref.dtype)
        e1_ref[h] = jnp.exp(s1 - t1[0:1, :]) / z
        e2_ref[h] = jnp.exp(s2 - t2[0:1, :]).astype(e2_ref.dtype)
        return carry

    lax.fori_loop(0, PEER_HEADS, head, 0)


def _peer_route(x2d, g, wqt, keys):
    n_tok = x2d.shape[0]
    tb = PEER_ROUTE_TB
    rt_shape = jax.ShapeDtypeStruct((PEER_HEADS, PEER_KEYS, n_tok), F32)
    gate_shape = jax.ShapeDtypeStruct((PEER_HEADS, PEER_KEYS, n_tok), PEER_GATE_DTYPE)
    rt_spec = pl.BlockSpec((PEER_HEADS, PEER_KEYS, tb), lambda i: (0, 0, i))
    return pl.pallas_call(
        _peer_route_kernel,
        grid=(n_tok // tb,),
        in_specs=[
            pl.BlockSpec((tb, D_MODEL), lambda i: (i, 0)),
            pl.BlockSpec((1, D_MODEL), lambda i: (0, 0)),
            pl.BlockSpec((PEER_HEADS * PEER_QDIM, D_MODEL), lambda i: (0, 0)),
            pl.BlockSpec((2 * PEER_HEADS, PEER_KEYS, PEER_HALF), lambda i: (0, 0, 0)),
        ],
        out_specs=[
            pl.BlockSpec((D_MODEL, tb), lambda i: (0, i)),
            rt_spec, rt_spec, rt_spec, rt_spec,
        ],
        out_shape=[
            jax.ShapeDtypeStruct((D_MODEL, n_tok), MXU_DTYPE),
            rt_shape, gate_shape, rt_shape, gate_shape,
        ],
        scratch_shapes=[
            pltpu.VMEM((PEER_HEADS * PEER_QDIM, tb), MXU_DTYPE),
            pltpu.VMEM((2 * PEER_TOPK, tb), F32),
            pltpu.VMEM((PEER_CAND_ROWS, tb), F32),
            pltpu.VMEM((PEER_TOPK, tb), F32),
        ],
        compiler_params=pltpu.CompilerParams(
            dimension_semantics=("arbitrary",), vmem_limit_bytes=VMEM_LIMIT_BYTES),
        name="peer_route",
    )(x2d, g, wqt, keys)


def _peer_expert_kernel(x_ref, xnt_ref, cnt_ref, rank_ref, e1_ref, e2_ref, u_ref, vt_ref, fg_ref,
                        out_ref, acc_scr, w_scr, *, final_norm):
    c = pl.program_id(1)
    n_chunks = pl.num_programs(1) - 1
    n_i1 = PEER_EC // PEER_KEYS
    tb = acc_scr.shape[1]
    gdt = PEER_GATE_DTYPE

    @pl.when(c == 0)
    def _():
        acc_scr[...] = jnp.zeros_like(acc_scr)
        w_scr[1] = jnp.zeros(w_scr.shape[1:], w_scr.dtype)

    slot = c % 2
    n_groups = PEER_KEYS // PEER_GATE_ROWS
    w_prev = w_scr.at[(c + 1) % 2]
    pr = PEER_PIECE_ROWS

    def hidden(i):
        return jnp.dot(u_ref[i * pr:(i + 1) * pr, :], xnt_ref[...], preferred_element_type=F32)

    def contract(j):
        rows = slice(j * pr, (j + 1) * pr)
        acc_scr[rows, :] += jnp.dot(vt_ref[rows, :], w_prev[...], preferred_element_type=F32)

    def gate_block(i1l, l0, hid, hoff):
        lanes = slice(l0, l0 + LANES)
        gates = [jnp.zeros((PEER_GATE_ROWS, LANES), gdt) for _ in range(n_groups)]
        for h in range(PEER_HEADS):
            cntb = jnp.broadcast_to(cnt_ref[h, i1l:i1l + 1, lanes], (PEER_GATE_ROWS, LANES)).astype(gdt)
            e1b = jnp.broadcast_to(e1_ref[h, i1l:i1l + 1, lanes], (PEER_GATE_ROWS, LANES)).astype(gdt)
            for k in range(n_groups):
                rows = slice(k * PEER_GATE_ROWS, (k + 1) * PEER_GATE_ROWS)
                w = e2_ref[h, rows, lanes] * e1b
                gates[k] = gates[k] + jnp.where(rank_ref[h, rows, lanes] < cntb, w, jnp.zeros_like(w))
        for k in range(n_groups):
            r0 = k * PEER_GATE_ROWS
            act = _gelu_exact(hid[hoff + r0:hoff + r0 + PEER_GATE_ROWS, lanes]).astype(gdt)
            e0 = i1l * PEER_KEYS + r0
            w_scr[slot, e0:e0 + PEER_GATE_ROWS, lanes] = (act * gates[k]).astype(MXU_DTYPE)

    n_lane_tiles = tb // LANES
    i1_per_piece = pr // PEER_KEYS
    n_contract = D_MODEL // pr
    hid_next = hidden(0)
    done = 0
    for i1l in range(n_i1):
        if i1l % i1_per_piece == 0:
            hid = hid_next
            if i1l + i1_per_piece < n_i1:
                hid_next = hidden(i1l // i1_per_piece + 1)
        for lt in range(n_lane_tiles):
            gate_block(i1l, lt * LANES, hid, (i1l % i1_per_piece) * PEER_KEYS)
            target = ((i1l * n_lane_tiles + lt + 1) * n_contract) // (n_i1 * n_lane_tiles)
            while done < target:
                contract(done)
                done += 1

    @pl.when(c == n_chunks)
    def _():
        y = x_ref[...] + acc_scr[...].T
        if final_norm:
            y = y * lax.rsqrt(jnp.mean(y * y, axis=-1, keepdims=True) + EPS) * fg_ref[...]
        out_ref[...] = y


def _peer_experts(x2d, xnt, cnt, rank2, e1, e2, u, vt, final_g, final_norm):
    n_tok = x2d.shape[0]
    tb, ec = PEER_TB, PEER_EC
    n_chunks = PEER_EXPERTS // ec
    rt_spec = pl.BlockSpec((PEER_HEADS, PEER_KEYS, tb), lambda j, c: (0, 0, j))
    row_spec = pl.BlockSpec((PEER_HEADS, ec // PEER_KEYS, tb),
                            lambda j, c: (0, jnp.minimum(c, n_chunks - 1), j))
    return pl.pallas_call(
        functools.partial(_peer_expert_kernel, final_norm=final_norm),
        grid=(n_tok // tb, n_chunks + 1),
        in_specs=[
            pl.BlockSpec((tb, D_MODEL), lambda j, c: (j, 0)),
            pl.BlockSpec((D_MODEL, tb), lambda j, c: (0, j)),
            row_spec, rt_spec, row_spec, rt_spec,
            pl.BlockSpec((ec, D_MODEL), lambda j, c: (jnp.minimum(c, n_chunks - 1), 0)),
            pl.BlockSpec((D_MODEL, ec), lambda j, c: (0, jnp.maximum(c - 1, 0))),
            pl.BlockSpec((1, D_MODEL), lambda j, c: (0, 0)),
        ],
        out_specs=pl.BlockSpec((tb, D_MODEL), lambda j, c: (j, 0)),
        out_shape=jax.ShapeDtypeStruct((n_tok, D_MODEL), F32),
        scratch_shapes=[
            pltpu.VMEM((D_MODEL, tb), F32),
            pltpu.VMEM((2, ec, tb), MXU_DTYPE),
        ],
        compiler_params=pltpu.CompilerParams(
            dimension_semantics=("arbitrary", "arbitrary"), vmem_limit_bytes=VMEM_LIMIT_BYTES),
        name="peer_experts",
    )(x2d, xnt, cnt, rank2, e1, e2, u, vt, final_g.reshape(1, D_MODEL))


def peer_block(x2d, norm_g, w_q, keys, u, v, final_g, final_norm=False):
    wqt = w_q.T.astype(MXU_DTYPE)
    keys2 = keys.reshape(2 * PEER_HEADS, PEER_KEYS, PEER_HALF).astype(MXU_DTYPE)
    xnt, cnt, rank2, e1, e2 = _peer_route(x2d, norm_g.reshape(1, D_MODEL), wqt, keys2)
    return _peer_experts(x2d, xnt, cnt, rank2, e1, e2, u, v, final_g, final_norm)


MLA_HG = LANES
MLA_QK_W = MLA_HEADS * MLA_HG
MLA_V_W = MLA_HEADS * MLA_V
MLA_PREP_TOK = 512
MLA_TQ = 512
MLA_TK = 2048
MLA_RHALF = MLA_ROPE // 2


def _mla_rope_tables(s):
    inv = ROPE_THETA ** (-jnp.arange(MLA_RHALF, dtype=F32) / MLA_RHALF)
    ang = jnp.arange(s).astype(F32)[:, None] * inv[None, :]
    cos, sin = jnp.cos(ang), jnp.sin(ang)
    zero_pad = jnp.zeros((s, MLA_HG - MLA_NOPE - MLA_ROPE), F32)
    cos_rot = jnp.concatenate([cos, cos, zero_pad], axis=1)
    sin_rot = jnp.concatenate([-sin, sin, zero_pad], axis=1)
    scale = np.float32((MLA_NOPE + MLA_ROPE) ** -0.5)
    q_cos = scale * jnp.concatenate([jnp.ones((s, MLA_NOPE), F32), cos_rot], axis=1)
    q_sin = scale * jnp.concatenate([jnp.zeros((s, MLA_NOPE), F32), sin_rot], axis=1)
    k_cos = jnp.concatenate([jnp.zeros((s, MLA_NOPE), F32), cos_rot], axis=1)
    k_sin = jnp.concatenate([jnp.zeros((s, MLA_NOPE), F32), sin_rot], axis=1)
    return q_cos, q_sin, k_cos, k_sin


def _mla_pack_weights(w_qb, w_kvb):
    hd_q = MLA_NOPE + MLA_ROPE
    wq = w_qb.reshape(MLA_Q_RANK, MLA_HEADS, hd_q)
    rot = wq[:, :, MLA_NOPE:]
    rot_sw = jnp.concatenate([rot[:, :, MLA_RHALF:], rot[:, :, :MLA_RHALF]], axis=2)
    pad = jnp.zeros((MLA_Q_RANK, MLA_HEADS, MLA_HG - hd_q), F32)
    wq_a = jnp.concatenate([wq, pad], axis=2).reshape(MLA_Q_RANK, MLA_QK_W)
    wq_b = jnp.concatenate([jnp.zeros_like(wq[:, :, :MLA_NOPE]), rot_sw, pad], axis=2).reshape(MLA_Q_RANK, MLA_QK_W)
    wkv = w_kvb.reshape(MLA_KV_RANK, MLA_HEADS, MLA_NOPE + MLA_V)
    wk = jnp.concatenate([wkv[:, :, :MLA_NOPE], jnp.zeros((MLA_KV_RANK, MLA_HEADS, MLA_HG - MLA_NOPE), F32)],
                         axis=2).reshape(MLA_KV_RANK, MLA_QK_W)
    wv = jnp.concatenate([wkv[:, :, MLA_NOPE:], jnp.zeros((MLA_KV_RANK, MLA_HEADS, MLA_HG - MLA_V), F32)],
                         axis=2).reshape(MLA_KV_RANK, MLA_QK_W)
    return (wq_a.astype(MXU_DTYPE), wq_b.astype(MXU_DTYPE), wk.astype(MXU_DTYPE), wv.astype(MXU_DTYPE))


def _mla_prep_kernel(cq_ref, ckv_ref, kr_ref, qn_ref, kvn_ref, wqa_ref, wqb_ref, wk_ref, wv_ref,
                     qcos_ref, qsin_ref, kcos_ref, ksin_ref, q_out, kt_out, v_out):
    cq = cq_ref[0]
    cqn = (cq * lax.rsqrt(jnp.mean(cq * cq, axis=-1, keepdims=True) + EPS) * qn_ref[...]).astype(MXU_DTYPE)
    qa = jnp.dot(cqn, wqa_ref[...], preferred_element_type=F32)
    qb = jnp.dot(cqn, wqb_ref[...], preferred_element_type=F32)
    ckv = ckv_ref[0]
    ckvn = (ckv * lax.rsqrt(jnp.mean(ckv * ckv, axis=-1, keepdims=True) + EPS) * kvn_ref[...]).astype(MXU_DTYPE)
    ka = jnp.dot(ckvn, wk_ref[...], preferred_element_type=F32)
    one_lane = (lax.broadcasted_iota(jnp.int32, (1, MLA_QK_W), 1) % MLA_HG == MLA_V).astype(F32)
    v_out[0] = (jnp.dot(ckvn, wv_ref[...], preferred_element_type=F32) + one_lane).astype(MXU_DTYPE)
    kr = kr_ref[0]
    k_rot = kr[:, 0:MLA_HG] * kcos_ref[...] + kr[:, MLA_HG:2 * MLA_HG] * ksin_ref[...]
    for h in range(MLA_HEADS):
        lanes = slice(h * MLA_HG, (h + 1) * MLA_HG)
        q_out[0, :, lanes] = (qa[:, lanes] * qcos_ref[...] + qb[:, lanes] * qsin_ref[...]).astype(MXU_DTYPE)
        kt_out[0, lanes, :] = (ka[:, lanes] + k_rot).T.astype(MXU_DTYPE)


def _mla_flash_kernel(q_ref, kt_ref, v_ref, out_ref, m_scr, acc_scr):
    ki = pl.program_id(2)

    @pl.when(ki == 0)
    def _():
        m_scr[...] = jnp.full(m_scr.shape, -jnp.inf, F32)
        acc_scr[...] = jnp.zeros(acc_scr.shape, F32)

    def logits(h):
        lanes = slice(h * MLA_HG, (h + 1) * MLA_HG)
        return jnp.dot(q_ref[0, :, lanes], kt_ref[0, lanes, :], preferred_element_type=F32)

    n_rep = kt_ref.shape[2] // MLA_HG
    s_next = logits(0)
    for h in range(MLA_HEADS):
        s = s_next
        if h + 1 < MLA_HEADS:
            s_next = logits(h + 1)
        m_old = m_scr[h]
        m_new = jnp.maximum(m_old, jnp.max(s, axis=-1, keepdims=True))
        p = jnp.exp(s - jnp.tile(m_new, (1, n_rep))).astype(MXU_DTYPE)
        acc_scr[h] = jnp.exp(m_old - m_new) * acc_scr[h] + jnp.dot(
            p, v_ref[0, :, h * MLA_HG:(h + 1) * MLA_HG], preferred_element_type=F32)
        m_scr[h] = m_new

    @pl.when(ki == pl.num_programs(2) - 1)
    def _():
        low = lax.broadcasted_iota(jnp.int32, (acc_scr.shape[1], MLA_HG), 1) < MLA_V
        outs = []
        for h in range(MLA_HEADS):
            acc = acc_scr[h]
            outs.append(acc / acc[:, MLA_V:MLA_V + 1])
        for hp in range(MLA_HEADS // 2):
            odd = pltpu.roll(outs[2 * hp + 1], MLA_V, axis=1)
            out_ref[0, :, hp * MLA_HG:(hp + 1) * MLA_HG] = jnp.where(low, outs[2 * hp], odd)


def mla_rotary_key_columns(w):
    zl = jnp.zeros(w.shape[:-1] + (MLA_NOPE,), w.dtype)
    zr = jnp.zeros(w.shape[:-1] + (MLA_HG - MLA_NOPE - MLA_ROPE,), w.dtype)
    w_sw = jnp.concatenate([w[..., MLA_RHALF:], w[..., :MLA_RHALF]], axis=-1)
    return jnp.concatenate([zl, w, zr, zl, w_sw, zr], axis=-1)


def mla_mixer(arr, q_norm, w_qb, kv_norm, w_kvb, col_q=0, col_kv=MLA_Q_RANK, col_kr=MLA_Q_RANK + MLA_KV_RANK):
    b, s, _ = arr.shape
    tt = MLA_PREP_TOK
    assert col_q % MLA_Q_RANK == 0 and col_kv % MLA_KV_RANK == 0 and col_kr % (2 * MLA_HG) == 0
    wqa, wqb, wk, wv = _mla_pack_weights(w_qb, w_kvb)
    q_cos, q_sin, k_cos, k_sin = _mla_rope_tables(s)
    tok = lambda w, c=0: pl.BlockSpec((1, tt, w), lambda i, j: (i, j, c // w))
    full = lambda r, c: pl.BlockSpec((r, c), lambda i, j: (0, 0))
    tab = pl.BlockSpec((tt, MLA_HG), lambda i, j: (j, 0))
    q, kt, v = pl.pallas_call(
        _mla_prep_kernel,
        grid=(b, s // tt),
        in_specs=[tok(MLA_Q_RANK, col_q), tok(MLA_KV_RANK, col_kv), tok(2 * MLA_HG, col_kr),
                  full(1, MLA_Q_RANK), full(1, MLA_KV_RANK),
                  full(MLA_Q_RANK, MLA_QK_W), full(MLA_Q_RANK, MLA_QK_W),
                  full(MLA_KV_RANK, MLA_QK_W), full(MLA_KV_RANK, MLA_QK_W),
                  tab, tab, tab, tab],
        out_specs=[tok(MLA_QK_W), pl.BlockSpec((1, MLA_QK_W, tt), lambda i, j: (i, 0, j)), tok(MLA_QK_W)],
        out_shape=[jax.ShapeDtypeStruct((b, s, MLA_QK_W), MXU_DTYPE),
                   jax.ShapeDtypeStruct((b, MLA_QK_W, s), MXU_DTYPE),
                   jax.ShapeDtypeStruct((b, s, MLA_QK_W), MXU_DTYPE)],
        compiler_params=pltpu.CompilerParams(
            dimension_semantics=("arbitrary", "arbitrary"), vmem_limit_bytes=VMEM_LIMIT_BYTES),
        name="mla_prep",
    )(arr, arr, arr, q_norm.reshape(1, -1), kv_norm.reshape(1, -1), wqa, wqb, wk, wv,
      q_cos, q_sin, k_cos, k_sin)
    tq, tk = min(MLA_TQ, s), min(MLA_TK, s)
    assert s % tq == 0 and s % tk == 0 and s % tt == 0
    return pl.pallas_call(
        _mla_flash_kernel,
        grid=(b, s // tq, s // tk),
        in_specs=[pl.BlockSpec((1, tq, MLA_QK_W), lambda i, j, kk: (i, j, 0)),
                  pl.BlockSpec((1, MLA_QK_W, tk), lambda i, j, kk: (i, 0, kk)),
                  pl.BlockSpec((1, tk, MLA_QK_W), lambda i, j, kk: (i, kk, 0))],
        out_specs=pl.BlockSpec((1, tq, MLA_V_W), lambda i, j, kk: (i, j, 0)),
        out_shape=jax.ShapeDtypeStruct((b, s, MLA_V_W), F32),
        scratch_shapes=[pltpu.VMEM((MLA_HEADS, tq, MLA_HG), F32),
                        pltpu.VMEM((MLA_HEADS, tq, MLA_HG), F32)],
        compiler_params=pltpu.CompilerParams(
            dimension_semantics=("arbitrary", "arbitrary", "arbitrary"), vmem_limit_bytes=VMEM_LIMIT_BYTES),
        name="mla_flash",
    )(q, kt, v)


def t5_bucket(rel):
    nb = T5_BUCKETS // 2
    ret = np.where(rel > 0, nb, 0)
    n = np.abs(rel)
    max_exact = nb // 2
    large = max_exact + (np.log(np.maximum(n, 1) / max_exact) / np.log(T5_MAX_DIST / max_exact)
                         * (nb - max_exact)).astype(np.int64)
    large = np.minimum(large, nb - 1)
    return (ret + np.where(n < max_exact, n, large)).astype(np.int32)


DIL_DIM = DIL_HEADS * DIL_HD
DIL_HALF = 64
DIL_QB = 128
DIL_KW = DIL_QB + 2 * DIL_HALF
DIL_TL = 512
DIL_N_GROUPS = len(DIL_GROUPS)


def _dil_bias_table(t5_table, gi, dil):
    rel = np.arange(DIL_KW)[None, :] - DIL_HALF - np.arange(DIL_QB)[:, None]
    bias = t5_table[:, gi * DIL_HEADS:(gi + 1) * DIL_HEADS][t5_bucket(rel * dil)].astype(F32)
    bias = jnp.where((np.abs(rel) <= DIL_HALF)[:, :, None], bias, -jnp.inf)
    return bias.transpose(2, 0, 1).reshape(DIL_HEADS * DIL_QB, DIL_KW)


def _dil_kernel(prev_ref, cur_ref, next_ref, tab_ref, o_ref, lse_ref, k_scr, v_scr, *, seq_len):
    step = pl.program_id(2)
    for i, ref in enumerate((prev_ref, cur_ref, next_ref)):
        k_scr[i * DIL_TL:(i + 1) * DIL_TL, :] = ref[0, :, DIL_DIM:2 * DIL_DIM].astype(MXU_DTYPE)
        v_scr[i * DIL_TL:(i + 1) * DIL_TL, :] = ref[0, :, 2 * DIL_DIM:3 * DIL_DIM].astype(MXU_DTYPE)
    lane_head = lax.broadcasted_iota(jnp.int32, (DIL_QB, DIL_DIM), 1) // DIL_HD
    key_off = lax.broadcasted_iota(jnp.int32, (1, DIL_KW), 1)
    for n in range(DIL_TL // DIL_QB):
        w0 = DIL_TL + n * DIL_QB - DIL_HALF
        kpos = step * DIL_TL + (n * DIL_QB - DIL_HALF) + key_off
        valid = (kpos >= 0) & (kpos < seq_len)
        q = cur_ref[0, n * DIL_QB:(n + 1) * DIL_QB, 0:DIL_DIM] * np.float32(DIL_HD ** -0.5)
        qs = jnp.concatenate([jnp.where(lane_head == h, q, 0.0) for h in range(DIL_HEADS)], axis=0)
        logits = lax.dot_general(qs.astype(MXU_DTYPE), k_scr[w0:w0 + DIL_KW, :], (((1,), (1,)), ((), ())),
                                 preferred_element_type=F32) + tab_ref[...]
        logits = jnp.where(valid, logits, -jnp.inf)
        m = jnp.max(logits, axis=-1, keepdims=True)
        p = jnp.exp(logits - m)
        denom = jnp.sum(p, axis=-1, keepdims=True)
        o_all = jnp.dot(p.astype(MXU_DTYPE), v_scr[w0:w0 + DIL_KW, :], preferred_element_type=F32) / denom
        lse_all = m + jnp.log(denom)
        o = jnp.zeros((DIL_QB, DIL_DIM), F32)
        lse = jnp.zeros((DIL_QB, DIL_DIM), F32)
        for h in range(DIL_HEADS):
            rows = slice(h * DIL_QB, (h + 1) * DIL_QB)
            o = o + jnp.where(lane_head == h, o_all[rows, :], 0.0)
            lse = lse + jnp.where(lane_head == h, lse_all[rows, :], 0.0)
        o_ref[0, n * DIL_QB:(n + 1) * DIL_QB, :] = o
        lse_ref[0, n * DIL_QB:(n + 1) * DIL_QB, :] = lse


def _dil_group(qkv, t5_table, gi, dil, col):
    b, s, width = qkv.shape
    seq_len = s // dil
    assert seq_len % DIL_TL == 0 and width % (3 * DIL_DIM) == 0 and col % (3 * DIL_DIM) == 0
    n_steps = seq_len // DIL_TL
    gi_col = col // (3 * DIL_DIM) + gi
    if dil > 1:
        qkv = qkv[:, :, gi_col * 3 * DIL_DIM:(gi_col + 1) * 3 * DIL_DIM]
        width, gi_col = 3 * DIL_DIM, 0
    n_col = width // (3 * DIL_DIM)
    view = qkv.reshape(b, seq_len, dil * width)
    blk = (1, DIL_TL, 3 * DIL_DIM)
    out_shape = jax.ShapeDtypeStruct((b, seq_len, dil * DIL_DIM), F32)
    out_spec = pl.BlockSpec((1, DIL_TL, DIL_DIM), lambda i, r, j: (i, j, r))
    o, lse = pl.pallas_call(
        functools.partial(_dil_kernel, seq_len=seq_len),
        grid=(b, dil, n_steps),
        in_specs=[
            pl.BlockSpec(blk, lambda i, r, j: (i, jnp.maximum(j - 1, 0), r * n_col + gi_col)),
            pl.BlockSpec(blk, lambda i, r, j: (i, j, r * n_col + gi_col)),
            pl.BlockSpec(blk, lambda i, r, j: (i, jnp.minimum(j + 1, n_steps - 1), r * n_col + gi_col)),
            pl.BlockSpec((DIL_HEADS * DIL_QB, DIL_KW), lambda i, r, j: (0, 0)),
        ],
        out_specs=[out_spec, out_spec],
        out_shape=[out_shape, out_shape],
        scratch_shapes=[pltpu.VMEM((3 * DIL_TL, DIL_DIM), MXU_DTYPE),
                        pltpu.VMEM((3 * DIL_TL, DIL_DIM), MXU_DTYPE)],
        compiler_params=pltpu.CompilerParams(
            dimension_semantics=("arbitrary", "arbitrary", "arbitrary"), vmem_limit_bytes=VMEM_LIMIT_BYTES),
        name=f"dilated_attention_g{gi}",
    )(view, view, view, _dil_bias_table(t5_table, gi, dil))
    return o.reshape(b, s, DIL_DIM), lse.reshape(b, s, DIL_DIM)


def _dil_combine_kernel(*refs):
    o_refs, lse_refs, out_ref = refs[:DIL_N_GROUPS], refs[DIL_N_GROUPS:2 * DIL_N_GROUPS], refs[-1]
    lses = [r[...] for r in lse_refs]
    m = functools.reduce(jnp.maximum, lses)
    ws = [jnp.exp(l - m) for l in lses]
    total = functools.reduce(jnp.add, ws)
    acc = functools.reduce(jnp.add, [w * r[...] for w, r in zip(ws, o_refs)])
    out_ref[...] = acc / total


def dilated_mixer(qkv, t5_table, col=0):
    b, s, _ = qkv.shape
    outs, lses = [], []
    for gi, (win, dil) in enumerate(DIL_GROUPS):
        assert win // (2 * dil) == DIL_HALF
        o, lse = _dil_group(qkv, t5_table, gi, dil, col)
        outs.append(o.reshape(b * s, DIL_DIM))
        lses.append(lse.reshape(b * s, DIL_DIM))
    tm = 1024
    spec = pl.BlockSpec((tm, DIL_DIM), lambda i: (i, 0))
    out = pl.pallas_call(
        _dil_combine_kernel,
        grid=(b * s // tm,),
        in_specs=[spec] * (2 * DIL_N_GROUPS),
        out_specs=spec,
        out_shape=jax.ShapeDtypeStruct((b * s, DIL_DIM), F32),
        compiler_params=pltpu.CompilerParams(
            dimension_semantics=("arbitrary",), vmem_limit_bytes=VMEM_LIMIT_BYTES),
        name="dilated_combine",
    )(*outs, *lses)
    return out.reshape(b, s, DIL_DIM)


SSM_CONV_TOK = 512
SSM_HALO = SUBLANES
SSM_HEADS_PER_GROUP = SSM_HEADS // SSM_GROUPS
SSM_GROUP_W = SSM_HEADS_PER_GROUP * SSM_HD
SSM_BC_W = SSM_GROUPS * SSM_STATE


def _softplus(x):
    return jnp.maximum(x, 0.0) + jnp.log1p(jnp.exp(-jnp.abs(x)))


def _ssm_conv_kernel(prev_ref, cur_ref, next_ref, w_ref, b_ref, out_ref, cat_scr):
    j = pl.program_id(1)
    tl = cur_ref.shape[1]
    cat_scr[0:SSM_HALO, :] = jnp.where(j > 0, prev_ref[0], 0.0)
    cat_scr[SSM_HALO:SSM_HALO + tl, :] = cur_ref[0]
    cat_scr[SSM_HALO + tl:, :] = jnp.where(j < pl.num_programs(1) - 1, next_ref[0], 0.0)
    acc = jnp.zeros((tl, CONV_CH), F32) + b_ref[...]
    for k in range(SSM_CONV):
        off = SSM_HALO + k - SSM_CONV // 2
        acc = acc + cat_scr[off:off + tl, :] * w_ref[k:k + 1, :]
    out_ref[0] = acc * jax.nn.sigmoid(acc)


def _ssm_conv(xbc, conv_w, conv_b, col):
    b, l, _ = xbc.shape
    tl = SSM_CONV_TOK
    n_steps = l // tl
    per = tl // SSM_HALO
    assert col % CONV_CH == 0
    cb = col // CONV_CH
    return pl.pallas_call(
        _ssm_conv_kernel,
        grid=(b, n_steps),
        in_specs=[
            pl.BlockSpec((1, SSM_HALO, CONV_CH), lambda i, j: (i, jnp.maximum(j * per - 1, 0), cb)),
            pl.BlockSpec((1, tl, CONV_CH), lambda i, j: (i, j, cb)),
            pl.BlockSpec((1, SSM_HALO, CONV_CH), lambda i, j: (i, jnp.minimum((j + 1) * per, n_steps * per - 1), cb)),
            pl.BlockSpec((SSM_CONV, CONV_CH), lambda i, j: (0, 0)),
            pl.BlockSpec((1, CONV_CH), lambda i, j: (0, 0)),
        ],
        out_specs=pl.BlockSpec((1, tl, CONV_CH), lambda i, j: (i, j, 0)),
        out_shape=jax.ShapeDtypeStruct((b, l, CONV_CH), F32),
        scratch_shapes=[pltpu.VMEM((tl + 2 * SSM_HALO, CONV_CH), F32)],
        compiler_params=pltpu.CompilerParams(
            dimension_semantics=("arbitrary", "arbitrary"), vmem_limit_bytes=VMEM_LIMIT_BYTES),
        name="ssm_conv",
    )(xbc, xbc, xbc, conv_w, conv_b.reshape(1, CONV_CH))


def _ssd_kernel(xbc_ref, dt_ref, dtt_ref, expand_ref, bias_e_ref, a_e_ref, bias_t_ref, a_t_ref,
                y_ref, state_scr, *, reverse):
    q = SSM_CHUNK

    @pl.when(pl.program_id(1) == 0)
    def _():
        state_scr[...] = jnp.zeros(state_scr.shape, F32)

    hi = lax.Precision.HIGHEST
    xbc = xbc_ref[0]
    xs = xbc[:, 0:SSM_INNER]
    dt_e = _softplus(jnp.dot(dt_ref[0], expand_ref[...], precision=hi, preferred_element_type=F32)
                     + bias_e_ref[...])
    a_e = dt_e * a_e_ref[...]
    ri = lax.broadcasted_iota(jnp.int32, (q, q), 0)
    ci = lax.broadcasted_iota(jnp.int32, (q, q), 1)
    seen = (ci >= ri) if reverse else (ci <= ri)
    cs_e = jnp.dot(seen.astype(F32), a_e, precision=hi, preferred_element_type=F32)
    dt_t = _softplus(dtt_ref[0] + bias_t_ref[...])
    cs_t = jnp.dot(dt_t * a_t_ref[...], seen.T.astype(F32), precision=hi, preferred_element_type=F32)
    dtx = xs * dt_e
    last = 0 if reverse else q - 1
    total = cs_e[last:last + 1, :]
    dtx_decayed = jnp.exp(total - cs_e) * dtx
    grow = jnp.exp(cs_e)
    chunk_decay = jnp.exp(total)
    lane_head = lax.broadcasted_iota(jnp.int32, (q, SSM_GROUP_W), 1) // SSM_HD
    for g in range(SSM_GROUPS):
        xl = slice(g * SSM_GROUP_W, (g + 1) * SSM_GROUP_W)
        bg = xbc[:, SSM_INNER + g * SSM_STATE:SSM_INNER + (g + 1) * SSM_STATE]
        cg = xbc[:, SSM_INNER + SSM_BC_W + g * SSM_STATE:SSM_INNER + SSM_BC_W + (g + 1) * SSM_STATE]
        cb = lax.dot_general(cg.astype(MXU_DTYPE), bg.astype(MXU_DTYPE), (((1,), (1,)), ((), ())),
                             preferred_element_type=F32)
        ms = []
        for r in range(SSM_HEADS_PER_GROUP):
            h = g * SSM_HEADS_PER_GROUP + r
            col = cs_e[:, h * SSM_HD:h * SSM_HD + 1]
            row = cs_t[h:h + 1, :]
            ms.append(cb * jnp.where(seen, jnp.exp(col - row), 0.0))
        y_all = jnp.dot(jnp.concatenate(ms, axis=0).astype(MXU_DTYPE), dtx[:, xl].astype(MXU_DTYPE),
                        preferred_element_type=F32)
        y_diag = jnp.zeros((q, SSM_GROUP_W), F32)
        for r in range(SSM_HEADS_PER_GROUP):
            y_diag = y_diag + jnp.where(lane_head == r, y_all[r * q:(r + 1) * q, :], 0.0)
        s_in = state_scr[g]
        y_off = jnp.dot(cg.astype(MXU_DTYPE), s_in.astype(MXU_DTYPE), preferred_element_type=F32) * grow[:, xl]
        y_ref[0, :, xl] = y_diag + y_off
        new = jnp.dot(bg.T.astype(MXU_DTYPE), dtx_decayed[:, xl].astype(MXU_DTYPE), preferred_element_type=F32)
        state_scr[g] = s_in * chunk_decay[:, xl] + new


def _ssd_direction(xbc_act, dt_arr, dt_t, a_log, dt_bias, direction, col_dt, dt_w):
    b, l, _ = xbc_act.shape
    assert col_dt % dt_w == 0 and dt_w >= 2 * SSM_HEADS
    q = SSM_CHUNK
    nc = l // q
    reverse = direction == 1
    a = -jnp.exp(a_log[direction].astype(F32))
    bias = dt_bias[direction].astype(F32)
    head_of_lane = np.arange(SSM_INNER) // SSM_HD
    expand = (np.arange(dt_w)[:, None] == direction * SSM_HEADS + head_of_lane[None, :]).astype(np.float32)
    chunk = (lambda j: nc - 1 - j) if reverse else (lambda j: j)
    full = lambda r, c: pl.BlockSpec((r, c), lambda i, j: (0, 0))
    return pl.pallas_call(
        functools.partial(_ssd_kernel, reverse=reverse),
        grid=(b, nc),
        in_specs=[
            pl.BlockSpec((1, q, CONV_CH), lambda i, j: (i, chunk(j), 0)),
            pl.BlockSpec((1, q, dt_w), lambda i, j: (i, chunk(j), col_dt // dt_w)),
            pl.BlockSpec((1, SSM_HEADS, q), lambda i, j: (i * 2 + direction, 0, chunk(j))),
            full(dt_w, SSM_INNER), full(1, SSM_INNER), full(1, SSM_INNER),
            full(SSM_HEADS, 1), full(SSM_HEADS, 1),
        ],
        out_specs=pl.BlockSpec((1, q, SSM_INNER), lambda i, j: (i, chunk(j), 0)),
        out_shape=jax.ShapeDtypeStruct((b, l, SSM_INNER), F32),
        scratch_shapes=[pltpu.VMEM((SSM_GROUPS, SSM_STATE, SSM_GROUP_W), F32)],
        compiler_params=pltpu.CompilerParams(
            dimension_semantics=("arbitrary", "arbitrary"), vmem_limit_bytes=VMEM_LIMIT_BYTES),
        name=f"ssd_dir{direction}",
    )(xbc_act, dt_arr, dt_t, jnp.asarray(expand), bias[head_of_lane].reshape(1, SSM_INNER),
      a[head_of_lane].reshape(1, SSM_INNER), bias.reshape(SSM_HEADS, 1), a.reshape(SSM_HEADS, 1))


def _ssm_gate_kernel(yf_ref, yb_ref, xbc_ref, z_ref, d_ref, g_ref, out_ref):
    z = z_ref[...]
    y = (yf_ref[...] + yb_ref[...] + xbc_ref[...] * d_ref[...]) * (z * jax.nn.sigmoid(z))
    out_ref[...] = y * lax.rsqrt(jnp.mean(y * y, axis=-1, keepdims=True) + EPS) * g_ref[...]


def mamba2_mixer(arr, conv_w, conv_b, A_log, dt_bias, D_skip, norm_g,
                 col_z=0, col_xbc=CONV_CH, col_dt=SSM_INNER + CONV_CH, dt_w=2 * SSM_HEADS):
    b, l, width = arr.shape
    assert col_z % SSM_INNER == 0
    xbc_act = _ssm_conv(arr, conv_w, conv_b, col_xbc)
    dt_t = jnp.swapaxes(arr[:, :, col_dt:col_dt + 2 * SSM_HEADS], 1, 2).reshape(b * 2, SSM_HEADS, l)
    y_f = _ssd_direction(xbc_act, arr, dt_t, A_log, dt_bias, 0, col_dt, dt_w)
    y_b = _ssd_direction(xbc_act, arr, dt_t, A_log, dt_bias, 1, col_dt, dt_w)
    tm = 1024
    tok = lambda cb: pl.BlockSpec((tm, SSM_INNER), lambda i: (i, cb))
    row = pl.BlockSpec((1, SSM_INNER), lambda i: (0, 0))
    d_e = D_skip.astype(F32)[np.arange(SSM_INNER) // SSM_HD].reshape(1, SSM_INNER)
    out = pl.pallas_call(
        _ssm_gate_kernel,
        grid=(b * l // tm,),
        in_specs=[tok(0), tok(0), tok(0), tok(col_z // SSM_INNER), row, row],
        out_specs=tok(0),
        out_shape=jax.ShapeDtypeStruct((b * l, SSM_INNER), F32),
        compiler_params=pltpu.CompilerParams(
            dimension_semantics=("arbitrary",), vmem_limit_bytes=VMEM_LIMIT_BYTES),
        name="ssm_gate",
    )(y_f.reshape(b * l, SSM_INNER), y_b.reshape(b * l, SSM_INNER), xbc_act.reshape(b * l, CONV_CH),
      arr.reshape(b * l, width), d_e, norm_g.reshape(1, SSM_INNER))
    return out.reshape(b, l, SSM_INNER)


NA_DIM = NA_HEADS * NA_HD
NA_ROWS_PER_STEP = 8
NA_WIN = NA_ROWS * GRID_W
NA_STEP_TOK = NA_ROWS_PER_STEP * GRID_W


def _na_bias_table(rpb):
    n_dc = 2 * NA_COLS - 1
    edge_l = jnp.repeat(rpb[:, :, :1], GRID_W, axis=2)
    edge_r = jnp.repeat(rpb[:, :, -1:], GRID_W, axis=2)
    ext = jnp.concatenate([edge_l, rpb.astype(F32), edge_r], axis=2)
    by_col = jnp.stack([ext[:, :, GRID_W + NA_COLS - 1 - qc:2 * GRID_W + NA_COLS - 1 - qc]
                        for qc in range(GRID_W)], axis=2)
    qc = np.arange(GRID_W)[:, None]
    kc = np.arange(GRID_W)[None, :]
    cs = np.clip(qc - NA_COLS // 2, 0, GRID_W - NA_COLS)
    ok = (kc >= cs) & (kc < cs + NA_COLS)
    by_col = jnp.where(ok[None, None], by_col, -jnp.inf)
    tabs = []
    for delta in range(NA_ROWS):
        rows = by_col[:, NA_ROWS - 1 - delta:2 * NA_ROWS - 1 - delta]
        tabs.append(rows.transpose(0, 2, 1, 3).reshape(NA_HEADS * GRID_W, NA_WIN))
    assert n_dc == rpb.shape[2]
    return jnp.stack(tabs, axis=0)


def _na_kernel(prev_ref, cur_ref, next_ref, tab_ref, out_ref, k_scr, v_scr, *, n_rows):
    step = pl.program_id(1)
    for i, ref in enumerate((prev_ref, cur_ref, next_ref)):
        k_scr[i * NA_STEP_TOK:(i + 1) * NA_STEP_TOK, :] = ref[0, :, NA_DIM:2 * NA_DIM].astype(MXU_DTYPE)
        v_scr[i * NA_STEP_TOK:(i + 1) * NA_STEP_TOK, :] = ref[0, :, 2 * NA_DIM:3 * NA_DIM].astype(MXU_DTYPE)
    lane_head = lax.broadcasted_iota(jnp.int32, (GRID_W, NA_DIM), 1) // NA_HD
    row0 = step * NA_ROWS_PER_STEP
    for j in range(NA_ROWS_PER_STEP):
        r = row0 + j
        r0 = jnp.clip(r - NA_ROWS // 2, 0, n_rows - NA_ROWS)
        start = pl.multiple_of((r0 - row0 + NA_ROWS_PER_STEP) * GRID_W, GRID_W)
        q = cur_ref[0, j * GRID_W:(j + 1) * GRID_W, 0:NA_DIM] * np.float32(NA_HD ** -0.5)
        qs = jnp.concatenate([jnp.where(lane_head == h, q, 0.0) for h in range(NA_HEADS)], axis=0)
        kw = k_scr[pl.ds(start, NA_WIN), :]
        vw = v_scr[pl.ds(start, NA_WIN), :]
        logits = lax.dot_general(qs.astype(MXU_DTYPE), kw, (((1,), (1,)), ((), ())),
                                 preferred_element_type=F32) + tab_ref[r - r0]
        m = jnp.max(logits, axis=-1, keepdims=True)
        p = jnp.exp(logits - m)
        denom = jnp.sum(p, axis=-1, keepdims=True)
        o_all = jnp.dot(p.astype(MXU_DTYPE), vw, preferred_element_type=F32) / denom
        o = jnp.zeros((GRID_W, NA_DIM), F32)
        for h in range(NA_HEADS):
            o = o + jnp.where(lane_head == h, o_all[h * GRID_W:(h + 1) * GRID_W, :], 0.0)
        out_ref[0, j * GRID_W:(j + 1) * GRID_W, :] = o


def na_mixer(qkv, rpb, col=0):
    b, s, _ = qkv.shape
    n_rows = s // GRID_W
    assert n_rows >= NA_ROWS and n_rows % NA_ROWS_PER_STEP == 0 and col % (3 * NA_DIM) == 0
    n_steps = n_rows // NA_ROWS_PER_STEP
    blk = (1, NA_STEP_TOK, 3 * NA_DIM)
    cb = col // (3 * NA_DIM)
    return pl.pallas_call(
        functools.partial(_na_kernel, n_rows=n_rows),
        grid=(b, n_steps),
        in_specs=[
            pl.BlockSpec(blk, lambda i, j: (i, jnp.maximum(j - 1, 0), cb)),
            pl.BlockSpec(blk, lambda i, j: (i, j, cb)),
            pl.BlockSpec(blk, lambda i, j: (i, jnp.minimum(j + 1, n_steps - 1), cb)),
            pl.BlockSpec((NA_ROWS, NA_HEADS * GRID_W, NA_WIN), lambda i, j: (0, 0, 0)),
        ],
        out_specs=pl.BlockSpec((1, NA_STEP_TOK, NA_DIM), lambda i, j: (i, j, 0)),
        out_shape=jax.ShapeDtypeStruct((b, s, NA_DIM), F32),
        scratch_shapes=[
            pltpu.VMEM((3 * NA_STEP_TOK, NA_DIM), MXU_DTYPE),
            pltpu.VMEM((3 * NA_STEP_TOK, NA_DIM), MXU_DTYPE),
        ],
        compiler_params=pltpu.CompilerParams(
            dimension_semantics=("arbitrary", "arbitrary"), vmem_limit_bytes=VMEM_LIMIT_BYTES),
        name="na_attention",
    )(qkv, qkv, qkv, _na_bias_table(rpb))


PK_GATE = 0
PK_XBC = PK_GATE + N_BRANCH * D_MODEL
PK_CQ = PK_XBC + CONV_CH
PK_DIL = PK_CQ + MLA_Q_RANK
PK_NA = PK_DIL + DIL_N_GROUPS * 3 * DIL_DIM
PK_KR = PK_NA + 3 * NA_DIM
PK_Z = PK_KR + 2 * MLA_HG
PK_CKV = PK_Z + SSM_INNER
PK_DT = PK_CKV + MLA_KV_RANK
PK_DT_W = LANES
PK_WIDTH = 13 * 3 * DIL_DIM
INPROJ_TM = 512
INPROJ_TN = PK_WIDTH // 2
MERGE_TM = 512


def _pack_w_in(w_in_l):
    gate, a_cq, a_ckv, a_kr, b_qkv, c_z, c_xbc, c_dt, d_qkv = jnp.split(w_in_l, IN_SPLITS, axis=-1)
    zeros = lambda n: jnp.zeros((D_MODEL, n), w_in_l.dtype)
    cols = [gate, c_xbc, a_cq, b_qkv, d_qkv, mla_rotary_key_columns(a_kr), c_z, a_ckv,
            c_dt, zeros(PK_DT_W - 2 * SSM_HEADS)]
    packed = jnp.concatenate(cols, axis=-1)
    assert packed.shape[1] == PK_DT + PK_DT_W
    return jnp.concatenate([packed, zeros(PK_WIDTH - packed.shape[1])], axis=-1).astype(MXU_DTYPE)


def _inproj_kernel(x_ref, g_ref, w_ref, out_ref, h_scr):
    @pl.when(pl.program_id(1) == 0)
    def _():
        x = x_ref[...]
        h_scr[...] = (x * lax.rsqrt(jnp.mean(x * x, axis=-1, keepdims=True) + EPS) * g_ref[...]).astype(MXU_DTYPE)

    out_ref[...] = jnp.dot(h_scr[...], w_ref[...], preferred_element_type=F32)


def _inproj(x2d, norm_g, w_packed):
    n_tok = x2d.shape[0]
    tm, tn = INPROJ_TM, INPROJ_TN
    return pl.pallas_call(
        _inproj_kernel,
        grid=(n_tok // tm, PK_WIDTH // tn),
        in_specs=[pl.BlockSpec((tm, D_MODEL), lambda i, j: (i, 0)),
                  pl.BlockSpec((1, D_MODEL), lambda i, j: (0, 0)),
                  pl.BlockSpec((D_MODEL, tn), lambda i, j: (0, j))],
        out_specs=pl.BlockSpec((tm, tn), lambda i, j: (i, j)),
        out_shape=jax.ShapeDtypeStruct((n_tok, PK_WIDTH), F32),
        scratch_shapes=[pltpu.VMEM((tm, D_MODEL), MXU_DTYPE)],
        compiler_params=pltpu.CompilerParams(
            dimension_semantics=("arbitrary", "arbitrary"), vmem_limit_bytes=VMEM_LIMIT_BYTES),
        name="in_projection",
    )(x2d, norm_g.reshape(1, D_MODEL), w_packed)


def _merge_kernel(x_ref, gate_ref, bg_ref, ya_ref, yb_ref, yc_ref, yd_ref, wb_ref, wo_ref, out_ref):
    merged = jnp.zeros(x_ref.shape, F32)
    for i, y_ref in enumerate((ya_ref, yb_ref, yc_ref, yd_ref)):
        proj = jnp.dot(y_ref[...].astype(MXU_DTYPE), wb_ref[BRANCH_ROWS[i]:BRANCH_ROWS[i + 1], :],
                       preferred_element_type=F32)
        lanes = slice(i * D_MODEL, (i + 1) * D_MODEL)
        merged = merged + jax.nn.sigmoid(gate_ref[:, lanes] + bg_ref[:, lanes]) * proj
    out_ref[...] = x_ref[...] + jnp.dot(merged.astype(MXU_DTYPE), wo_ref[...], preferred_element_type=F32)


def _merge(x2d, packed, b_gate, ys, w_branch, w_out):
    n_tok = x2d.shape[0]
    tm = MERGE_TM
    tok = lambda w: pl.BlockSpec((tm, w), lambda i: (i, 0))
    full = lambda r, c: pl.BlockSpec((r, c), lambda i: (0, 0))
    n_gate = N_BRANCH * D_MODEL
    return pl.pallas_call(
        _merge_kernel,
        grid=(n_tok // tm,),
        in_specs=[tok(D_MODEL), tok(n_gate), full(1, n_gate)] + [tok(w) for w in BRANCH_WIDTHS]
                 + [full(BRANCH_ROWS[-1], D_MODEL), full(D_MODEL, D_MODEL)],
        out_specs=tok(D_MODEL),
        out_shape=jax.ShapeDtypeStruct((n_tok, D_MODEL), F32),
        compiler_params=pltpu.CompilerParams(
            dimension_semantics=("arbitrary",), vmem_limit_bytes=VMEM_LIMIT_BYTES),
        name="branch_merge",
    )(x2d, packed, b_gate.reshape(1, n_gate), *[y.reshape(n_tok, -1) for y in ys], w_branch, w_out)


def encoder(x, norm1_g, w_in_packed, b_gate, mla_q_norm, mla_w_qb, mla_kv_norm, mla_w_kvb, t5_table,
            ssm_conv_w, ssm_conv_b, ssm_A_log, ssm_dt_bias, ssm_D, ssm_norm_g, na_rpb,
            w_branch, w_out, norm2_g, peer_wq, peer_keys, peer_u, peer_vt, final_g):
    b, s, _ = x.shape
    x2d = x.reshape(b * s, D_MODEL)
    for l in range(DEPTH):
        packed = _inproj(x2d, norm1_g[l], w_in_packed[l])
        p3 = packed.reshape(b, s, PK_WIDTH)
        y_a = mla_mixer(p3, mla_q_norm[l], mla_w_qb[l], mla_kv_norm[l], mla_w_kvb[l],
                        col_q=PK_CQ, col_kv=PK_CKV, col_kr=PK_KR)
        y_b = dilated_mixer(p3, t5_table, col=PK_DIL)
        y_c = mamba2_mixer(p3, ssm_conv_w[l], ssm_conv_b[l], ssm_A_log[l], ssm_dt_bias[l], ssm_D[l],
                           ssm_norm_g[l], col_z=PK_Z, col_xbc=PK_XBC, col_dt=PK_DT, dt_w=PK_DT_W)
        y_d = na_mixer(p3, na_rpb[l], col=PK_NA)
        x2d = _merge(x2d, packed, b_gate[l], (y_a, y_b, y_c, y_d), w_branch[l], w_out[l])
        x2d = peer_block(x2d, norm2_g[l], peer_wq[l], peer_keys[l], peer_u[l], peer_vt[l],
                         final_g, final_norm=(l == DEPTH - 1))
    return x2d.reshape(b, s, D_MODEL)


def kernel(x_prompt, x_sample, norm1_g, w_in, b_gate, mla_q_norm, mla_w_qb, mla_kv_norm, mla_w_kvb, t5_table, ssm_conv_w, ssm_conv_b, ssm_A_log, ssm_dt_bias, ssm_D, ssm_norm_g, na_rpb, w_branch, w_out, norm2_g, peer_wq, peer_keys, peer_u, peer_v, final_g):
    peer_u16 = peer_u.astype(MXU_DTYPE)
    peer_vt16 = jnp.swapaxes(peer_v, 1, 2).astype(MXU_DTYPE)
    w_in_packed = jnp.stack([_pack_w_in(w_in[l]) for l in range(DEPTH)])
    shared = (norm1_g, w_in_packed, b_gate, mla_q_norm, mla_w_qb, mla_kv_norm, mla_w_kvb, t5_table,
              ssm_conv_w, ssm_conv_b, ssm_A_log, ssm_dt_bias, ssm_D, ssm_norm_g, na_rpb,
              w_branch.astype(MXU_DTYPE), w_out.astype(MXU_DTYPE), norm2_g, peer_wq, peer_keys,
              peer_u16, peer_vt16, final_g)
    y_prompt = encoder(x_prompt, *shared)
    y_sample = encoder(x_sample, *shared)
    return (y_prompt, y_sample)
```

```python
import functools
import math

import numpy as np
import jax
import jax.numpy as jnp
from jax import lax
from jax.experimental import pallas as pl
from jax.experimental.pallas import tpu as pltpu

F32 = jnp.float32
BF16 = jnp.bfloat16
MXU_DTYPE = BF16

D_MODEL = 1024
DEPTH = 2
GRID_W = 64
EPS = 1e-6
N_BRANCH = 4

MLA_HEADS = 4
MLA_Q_RANK = 256
MLA_KV_RANK = 128
MLA_NOPE = 64
MLA_ROPE = 32
MLA_V = 64
ROPE_THETA = 10000.0
Q_BLOCK = 128

DIL_GROUPS = ((128, 1), (512, 4), (2048, 16))
DIL_HEADS = 4
DIL_HD = 64
T5_BUCKETS = 32
T5_MAX_DIST = 1024

SSM_HEADS = 8
SSM_HD = 64
SSM_INNER = SSM_HEADS * SSM_HD
SSM_GROUPS = 2
SSM_STATE = 128
SSM_CONV = 7
SSM_CHUNK = 128
CONV_CH = SSM_INNER + 2 * SSM_GROUPS * SSM_STATE

NA_HEADS = 4
NA_HD = 64
NA_ROWS = 8
NA_COLS = 16
NA_QCB = 16
NA_KCB = NA_QCB + NA_COLS

PEER_HEADS = 8
PEER_KEYS = 128
PEER_EXPERTS = PEER_KEYS * PEER_KEYS
PEER_QDIM = 256
PEER_TOPK = 16
PEER_TOK_BLOCK = 128

BRANCH_WIDTHS = (MLA_HEADS * MLA_V, DIL_HEADS * DIL_HD, SSM_INNER, NA_HEADS * NA_HD)
BRANCH_ROWS = tuple(sum(BRANCH_WIDTHS[:i]) for i in range(N_BRANCH + 1))
IN_SIZES = (N_BRANCH * D_MODEL, MLA_Q_RANK, MLA_KV_RANK, MLA_ROPE,
            len(DIL_GROUPS) * 3 * DIL_HEADS * DIL_HD,
            SSM_INNER, CONV_CH, 2 * SSM_HEADS,
            3 * NA_HEADS * NA_HD)
IN_SPLITS = tuple(sum(IN_SIZES[:i + 1]) for i in range(len(IN_SIZES) - 1))

VMEM_LIMIT_BYTES = 56 * 1024 * 1024
LANES = 128
SUBLANES = 8


def rms_norm(x, g):
    x32 = x.astype(F32)
    y = x32 * lax.rsqrt(jnp.mean(x32 * x32, axis=-1, keepdims=True) + EPS)
    return (y * g.astype(F32)).astype(x.dtype)


PEER_ROUTE_TB = 256
PEER_TB = 512
PEER_EC = 1024
PEER_GATE_DTYPE = BF16
PEER_GATE_ROWS = 16
PEER_PIECE_ROWS = 512
PEER_HALF = PEER_QDIM // 2
PEER_CAND_ROWS = 2 * SUBLANES + 7 * SUBLANES + SUBLANES


def _gelu_exact(x):
    return 0.5 * x * (1.0 + lax.erf(x * np.float32(math.sqrt(0.5))))


def _extract_desc(vals, n_out, out_ref, row0, with_rank=False):
    rank = jnp.full(vals.shape, float(n_out), F32) if with_rank else None
    for k in range(n_out):
        m = jnp.max(vals, axis=0, keepdims=True)
        out_ref[pl.ds(row0 + k, 1), :] = m
        hit = vals == m
        if with_rank:
            rank = jnp.where(hit, float(k), rank)
        vals = jnp.where(hit, -jnp.inf, vals)
    return rank


def _peer_route_kernel(x_ref, g_ref, wqt_ref, keys_ref,
                       xnt_ref, cnt_ref, rank_ref, e1_ref, e2_ref,
                       qt_scr, top_scr, cand_scr, tops_scr):
    x = x_ref[...]
    xn = x * lax.rsqrt(jnp.mean(x * x, axis=-1, keepdims=True) + EPS) * g_ref[...]
    xnt = xn.T.astype(MXU_DTYPE)
    xnt_ref[...] = xnt
    qt_scr[...] = jnp.dot(wqt_ref[...], xnt, preferred_element_type=F32).astype(MXU_DTYPE)

    def head(h, carry):
        q1 = qt_scr[pl.ds(pl.multiple_of(h * PEER_QDIM, PEER_QDIM), PEER_HALF), :]
        q2 = qt_scr[pl.ds(pl.multiple_of(h * PEER_QDIM + PEER_HALF, PEER_HALF), PEER_HALF), :]
        s1 = jnp.dot(keys_ref[2 * h], q1, preferred_element_type=F32)
        s2 = jnp.dot(keys_ref[2 * h + 1], q2, preferred_element_type=F32)
        _extract_desc(s1, PEER_TOPK, top_scr, 0)
        rank2 = _extract_desc(s2, PEER_TOPK, top_scr, PEER_TOPK, with_rank=True)
        t1 = top_scr[0:PEER_TOPK, :]
        t2 = top_scr[PEER_TOPK:2 * PEER_TOPK, :]
        cand_scr[0:2 * SUBLANES, :] = t1[0:1, :] + t2
        for a in range(1, SUBLANES):
            cand_scr[(a + 1) * SUBLANES:(a + 2) * SUBLANES, :] = t1[a:a + 1, :] + t2[0:SUBLANES, :]
        cand_scr[9 * SUBLANES:10 * SUBLANES, :] = t1[SUBLANES:2 * SUBLANES, :] + t2[0:1, :]
        _extract_desc(cand_scr[...], PEER_TOPK, tops_scr, 0)
        top_s = tops_scr[...]
        z = jnp.sum(jnp.exp(top_s - top_s[0:1, :]), axis=0, keepdims=True)
        tau = top_s[PEER_TOPK - 1:PEER_TOPK, :]
        cnt = jnp.zeros(s1.shape, F32)
        for b in range(PEER_TOPK):
            cnt = cnt + jnp.where(s1 + t2[b:b + 1, :] >= tau, 1.0, 0.0)
        cnt_ref[h] = cnt
        rank_ref[h] = rank2.astype(rank_ref.dtype)
        e1_ref[h] = jnp.exp(s1 - t1[0:1, :]) / z
        e2_ref[h] = jnp.exp(s2 - t2[0:1, :]).astype(e2_ref.dtype)
        return carry

    lax.fori_loop(0, PEER_HEADS, head, 0)


def _peer_route(x2d, g, wqt, keys):
    n_tok = x2d.shape[0]
    tb = PEER_ROUTE_TB
    rt_shape = jax.ShapeDtypeStruct((PEER_HEADS, PEER_KEYS, n_tok), F32)
    gate_shape = jax.ShapeDtypeStruct((PEER_HEADS, PEER_KEYS, n_tok), PEER_GATE_DTYPE)
    rt_spec = pl.BlockSpec((PEER_HEADS, PEER_KEYS, tb), lambda i: (0, 0, i))
    return pl.pallas_call(
        _peer_route_kernel,
        grid=(n_tok // tb,),
        in_specs=[
            pl.BlockSpec((tb, D_MODEL), lambda i: (i, 0)),
            pl.BlockSpec((1, D_MODEL), lambda i: (0, 0)),
            pl.BlockSpec((PEER_HEADS * PEER_QDIM, D_MODEL), lambda i: (0, 0)),
            pl.BlockSpec((2 * PEER_HEADS, PEER_KEYS, PEER_HALF), lambda i: (0, 0, 0)),
        ],
        out_specs=[
            pl.BlockSpec((D_MODEL, tb), lambda i: (0, i)),
            rt_spec, rt_spec, rt_spec, rt_spec,
        ],
        out_shape=[
            jax.ShapeDtypeStruct((D_MODEL, n_tok), MXU_DTYPE),
            rt_shape, gate_shape, rt_shape, gate_shape,
        ],
        scratch_shapes=[
            pltpu.VMEM((PEER_HEADS * PEER_QDIM, tb), MXU_DTYPE),
            pltpu.VMEM((2 * PEER_TOPK, tb), F32),
            pltpu.VMEM((PEER_CAND_ROWS, tb), F32),
            pltpu.VMEM((PEER_TOPK, tb), F32),
        ],
        compiler_params=pltpu.CompilerParams(
            dimension_semantics=("arbitrary",), vmem_limit_bytes=VMEM_LIMIT_BYTES),
        name="peer_route",
    )(x2d, g, wqt, keys)


def _peer_expert_kernel(x_ref, xnt_ref, cnt_ref, rank_ref, e1_ref, e2_ref, u_ref, vt_ref, fg_ref,
                        out_ref, acc_scr, w_scr, *, final_norm):
    c = pl.program_id(1)
    n_chunks = pl.num_programs(1) - 1
    n_i1 = PEER_EC // PEER_KEYS
    tb = acc_scr.shape[1]
    gdt = PEER_GATE_DTYPE

    @pl.when(c == 0)
    def _():
        acc_scr[...] = jnp.zeros_like(acc_scr)
        w_scr[1] = jnp.zeros(w_scr.shape[1:], w_scr.dtype)

    slot = c % 2
    n_groups = PEER_KEYS // PEER_GATE_ROWS
    w_prev = w_scr.at[(c + 1) % 2]
    pr = PEER_PIECE_ROWS

    def hidden(i):
        return jnp.dot(u_ref[i * pr:(i + 1) * pr, :], xnt_ref[...], preferred_element_type=F32)

    def contract(j):
        rows = slice(j * pr, (j + 1) * pr)
        acc_scr[rows, :] += jnp.dot(vt_ref[rows, :], w_prev[...], preferred_element_type=F32)

    def gate_block(i1l, l0, hid, hoff):
        lanes = slice(l0, l0 + LANES)
        gates = [jnp.zeros((PEER_GATE_ROWS, LANES), gdt) for _ in range(n_groups)]
        for h in range(PEER_HEADS):
            cntb = jnp.broadcast_to(cnt_ref[h, i1l:i1l + 1, lanes], (PEER_GATE_ROWS, LANES)).astype(gdt)
            e1b = jnp.broadcast_to(e1_ref[h, i1l:i1l + 1, lanes], (PEER_GATE_ROWS, LANES)).astype(gdt)
            for k in range(n_groups):
                rows = slice(k * PEER_GATE_ROWS, (k + 1) * PEER_GATE_ROWS)
                w = e2_ref[h, rows, lanes] * e1b
                gates[k] = gates[k] + jnp.where(rank_ref[h, rows, lanes] < cntb, w, jnp.zeros_like(w))
        for k in range(n_groups):
            r0 = k * PEER_GATE_ROWS
            act = _gelu_exact(hid[hoff + r0:hoff + r0 + PEER_GATE_ROWS, lanes]).astype(gdt)
            e0 = i1l * PEER_KEYS + r0
            w_scr[slot, e0:e0 + PEER_GATE_ROWS, lanes] = (act * gates[k]).astype(MXU_DTYPE)

    n_lane_tiles = tb // LANES
    i1_per_piece = pr // PEER_KEYS
    n_contract = D_MODEL // pr
    hid_next = hidden(0)
    done = 0
    for i1l in range(n_i1):
        if i1l % i1_per_piece == 0:
            hid = hid_next
            if i1l + i1_per_piece < n_i1:
                hid_next = hidden(i1l // i1_per_piece + 1)
        for lt in range(n_lane_tiles):
            gate_block(i1l, lt * LANES, hid, (i1l % i1_per_piece) * PEER_KEYS)
            target = ((i1l * n_lane_tiles + lt + 1) * n_contract) // (n_i1 * n_lane_tiles)
            while done < target:
                contract(done)
                done += 1

    @pl.when(c == n_chunks)
    def _():
        y = x_ref[...] + acc_scr[...].T
        if final_norm:
            y = y * lax.rsqrt(jnp.mean(y * y, axis=-1, keepdims=True) + EPS) * fg_ref[...]
        out_ref[...] = y


def _peer_experts(x2d, xnt, cnt, rank2, e1, e2, u, vt, final_g, final_norm):
    n_tok = x2d.shape[0]
    tb, ec = PEER_TB, PEER_EC
    n_chunks = PEER_EXPERTS // ec
    rt_spec = pl.BlockSpec((PEER_HEADS, PEER_KEYS, tb), lambda j, c: (0, 0, j))
    row_spec = pl.BlockSpec((PEER_HEADS, ec // PEER_KEYS, tb),
                            lambda j, c: (0, jnp.minimum(c, n_chunks - 1), j))
    return pl.pallas_call(
        functools.partial(_peer_expert_kernel, final_norm=final_norm),
        grid=(n_tok // tb, n_chunks + 1),
        in_specs=[
            pl.BlockSpec((tb, D_MODEL), lambda j, c: (j, 0)),
            pl.BlockSpec((D_MODEL, tb), lambda j, c: (0, j)),
            row_spec, rt_spec, row_spec, rt_spec,
            pl.BlockSpec((ec, D_MODEL), lambda j, c: (jnp.minimum(c, n_chunks - 1), 0)),
            pl.BlockSpec((D_MODEL, ec), lambda j, c: (0, jnp.maximum(c - 1, 0))),
            pl.BlockSpec((1, D_MODEL), lambda j, c: (0, 0)),
        ],
        out_specs=pl.BlockSpec((tb, D_MODEL), lambda j, c: (j, 0)),
        out_shape=jax.ShapeDtypeStruct((n_tok, D_MODEL), F32),
        scratch_shapes=[
            pltpu.VMEM((D_MODEL, tb), F32),
            pltpu.VMEM((2, ec, tb), MXU_DTYPE),
        ],
        compiler_params=pltpu.CompilerParams(
            dimension_semantics=("arbitrary", "arbitrary"), vmem_limit_bytes=VMEM_LIMIT_BYTES),
        name="peer_experts",
    )(x2d, xnt, cnt, rank2, e1, e2, u, vt, final_g.reshape(1, D_MODEL))


def peer_block(x2d, norm_g, w_q, keys, u, v, final_g, final_norm=False):
    wqt = w_q.T.astype(MXU_DTYPE)
    keys2 = keys.reshape(2 * PEER_HEADS, PEER_KEYS, PEER_HALF).astype(MXU_DTYPE)
    xnt, cnt, rank2, e1, e2 = _peer_route(x2d, norm_g.reshape(1, D_MODEL), wqt, keys2)
    return _peer_experts(x2d, xnt, cnt, rank2, e1, e2, u, v, final_g, final_norm)


MLA_HG = LANES
MLA_QK_W = MLA_HEADS * MLA_HG
MLA_V_W = MLA_HEADS * MLA_V
MLA_PREP_TOK = 512
MLA_TQ = 512
MLA_TK = 2048
MLA_RHALF = MLA_ROPE // 2


def _mla_rope_tables(s):
    inv = ROPE_THETA ** (-jnp.arange(MLA_RHALF, dtype=F32) / MLA_RHALF)
    ang = jnp.arange(s).astype(F32)[:, None] * inv[None, :]
    cos, sin = jnp.cos(ang), jnp.sin(ang)
    zero_pad = jnp.zeros((s, MLA_HG - MLA_NOPE - MLA_ROPE), F32)
    cos_rot = jnp.concatenate([cos, cos, zero_pad], axis=1)
    sin_rot = jnp.concatenate([-sin, sin, zero_pad], axis=1)
    scale = np.float32((MLA_NOPE + MLA_ROPE) ** -0.5)
    q_cos = scale * jnp.concatenate([jnp.ones((s, MLA_NOPE), F32), cos_rot], axis=1)
    q_sin = scale * jnp.concatenate([jnp.zeros((s, MLA_NOPE), F32), sin_rot], axis=1)
    k_cos = jnp.concatenate([jnp.zeros((s, MLA_NOPE), F32), cos_rot], axis=1)
    k_sin = jnp.concatenate([jnp.zeros((s, MLA_NOPE), F32), sin_rot], axis=1)
    return q_cos, q_sin, k_cos, k_sin


def _mla_pack_weights(w_qb, w_kvb):
    hd_q = MLA_NOPE + MLA_ROPE
    wq = w_qb.reshape(MLA_Q_RANK, MLA_HEADS, hd_q)
    rot = wq[:, :, MLA_NOPE:]
    rot_sw = jnp.concatenate([rot[:, :, MLA_RHALF:], rot[:, :, :MLA_RHALF]], axis=2)
    pad = jnp.zeros((MLA_Q_RANK, MLA_HEADS, MLA_HG - hd_q), F32)
    wq_a = jnp.concatenate([wq, pad], axis=2).reshape(MLA_Q_RANK, MLA_QK_W)
    wq_b = jnp.concatenate([jnp.zeros_like(wq[:, :, :MLA_NOPE]), rot_sw, pad], axis=2).reshape(MLA_Q_RANK, MLA_QK_W)
    wkv = w_kvb.reshape(MLA_KV_RANK, MLA_HEADS, MLA_NOPE + MLA_V)
    wk = jnp.concatenate([wkv[:, :, :MLA_NOPE], jnp.zeros((MLA_KV_RANK, MLA_HEADS, MLA_HG - MLA_NOPE), F32)],
                         axis=2).reshape(MLA_KV_RANK, MLA_QK_W)
    wv = jnp.concatenate([wkv[:, :, MLA_NOPE:], jnp.zeros((MLA_KV_RANK, MLA_HEADS, MLA_HG - MLA_V), F32)],
                         axis=2).reshape(MLA_KV_RANK, MLA_QK_W)
    return (wq_a.astype(MXU_DTYPE), wq_b.astype(MXU_DTYPE), wk.astype(MXU_DTYPE), wv.astype(MXU_DTYPE))


def _mla_prep_kernel(cq_ref, ckv_ref, kr_ref, qn_ref, kvn_ref, wqa_ref, wqb_ref, wk_ref, wv_ref,
                     qcos_ref, qsin_ref, kcos_ref, ksin_ref, q_out, kt_out, v_out):
    cq = cq_ref[0]
    cqn = (cq * lax.rsqrt(jnp.mean(cq * cq, axis=-1, keepdims=True) + EPS) * qn_ref[...]).astype(MXU_DTYPE)
    qa = jnp.dot(cqn, wqa_ref[...], preferred_element_type=F32)
    qb = jnp.dot(cqn, wqb_ref[...], preferred_element_type=F32)
    ckv = ckv_ref[0]
    ckvn = (ckv * lax.rsqrt(jnp.mean(ckv * ckv, axis=-1, keepdims=True) + EPS) * kvn_ref[...]).astype(MXU_DTYPE)
    ka = jnp.dot(ckvn, wk_ref[...], preferred_element_type=F32)
    one_lane = (lax.broadcasted_iota(jnp.int32, (1, MLA_QK_W), 1) % MLA_HG == MLA_V).astype(F32)
    v_out[0] = (jnp.dot(ckvn, wv_ref[...], preferred_element_type=F32) + one_lane).astype(MXU_DTYPE)
    kr = kr_ref[0]
    k_rot = kr[:, 0:MLA_HG] * kcos_ref[...] + kr[:, MLA_HG:2 * MLA_HG] * ksin_ref[...]
    for h in range(MLA_HEADS):
        lanes = slice(h * MLA_HG, (h + 1) * MLA_HG)
        q_out[0, :, lanes] = (qa[:, lanes] * qcos_ref[...] + qb[:, lanes] * qsin_ref[...]).astype(MXU_DTYPE)
        kt_out[0, lanes, :] = (ka[:, lanes] + k_rot).T.astype(MXU_DTYPE)


def _mla_flash_kernel(q_ref, kt_ref, v_ref, out_ref, m_scr, acc_scr):
    ki = pl.program_id(2)

    @pl.when(ki == 0)
    def _():
        m_scr[...] = jnp.full(m_scr.shape, -jnp.inf, F32)
        acc_scr[...] = jnp.zeros(acc_scr.shape, F32)

    def logits(h):
        lanes = slice(h * MLA_HG, (h + 1) * MLA_HG)
        return jnp.dot(q_ref[0, :, lanes], kt_ref[0, lanes, :], preferred_element_type=F32)

    n_rep = kt_ref.shape[2] // MLA_HG
    s_next = logits(0)
    for h in range(MLA_HEADS):
        s = s_next
        if h + 1 < MLA_HEADS:
            s_next = logits(h + 1)
        m_old = m_scr[h]
        m_new = jnp.maximum(m_old, jnp.max(s, axis=-1, keepdims=True))
        p = jnp.exp(s - jnp.tile(m_new, (1, n_rep))).astype(MXU_DTYPE)
        acc_scr[h] = jnp.exp(m_old - m_new) * acc_scr[h] + jnp.dot(
            p, v_ref[0, :, h * MLA_HG:(h + 1) * MLA_HG], preferred_element_type=F32)
        m_scr[h] = m_new

    @pl.when(ki == pl.num_programs(2) - 1)
    def _():
        low = lax.broadcasted_iota(jnp.int32, (acc_scr.shape[1], MLA_HG), 1) < MLA_V
        outs = []
        for h in range(MLA_HEADS):
            acc = acc_scr[h]
            outs.append(acc / acc[:, MLA_V:MLA_V + 1])
        for hp in range(MLA_HEADS // 2):
            odd = pltpu.roll(outs[2 * hp + 1], MLA_V, axis=1)
            out_ref[0, :, hp * MLA_HG:(hp + 1) * MLA_HG] = jnp.where(low, outs[2 * hp], odd)


def mla_rotary_key_columns(w):
    zl = jnp.zeros(w.shape[:-1] + (MLA_NOPE,), w.dtype)
    zr = jnp.zeros(w.shape[:-1] + (MLA_HG - MLA_NOPE - MLA_ROPE,), w.dtype)
    w_sw = jnp.concatenate([w[..., MLA_RHALF:], w[..., :MLA_RHALF]], axis=-1)
    return jnp.concatenate([zl, w, zr, zl, w_sw, zr], axis=-1)


def mla_mixer(arr, q_norm, w_qb, kv_norm, w_kvb, col_q=0, col_kv=MLA_Q_RANK, col_kr=MLA_Q_RANK + MLA_KV_RANK):
    b, s, _ = arr.shape
    tt = MLA_PREP_TOK
    assert col_q % MLA_Q_RANK == 0 and col_kv % MLA_KV_RANK == 0 and col_kr % (2 * MLA_HG) == 0
    wqa, wqb, wk, wv = _mla_pack_weights(w_qb, w_kvb)
    q_cos, q_sin, k_cos, k_sin = _mla_rope_tables(s)
    tok = lambda w, c=0: pl.BlockSpec((1, tt, w), lambda i, j: (i, j, c // w))
    full = lambda r, c: pl.BlockSpec((r, c), lambda i, j: (0, 0))
    tab = pl.BlockSpec((tt, MLA_HG), lambda i, j: (j, 0))
    q, kt, v = pl.pallas_call(
        _mla_prep_kernel,
        grid=(b, s // tt),
        in_specs=[tok(MLA_Q_RANK, col_q), tok(MLA_KV_RANK, col_kv), tok(2 * MLA_HG, col_kr),
                  full(1, MLA_Q_RANK), full(1, MLA_KV_RANK),
                  full(MLA_Q_RANK, MLA_QK_W), full(MLA_Q_RANK, MLA_QK_W),
                  full(MLA_KV_RANK, MLA_QK_W), full(MLA_KV_RANK, MLA_QK_W),
                  tab, tab, tab, tab],
        out_specs=[tok(MLA_QK_W), pl.BlockSpec((1, MLA_QK_W, tt), lambda i, j: (i, 0, j)), tok(MLA_QK_W)],
        out_shape=[jax.ShapeDtypeStruct((b, s, MLA_QK_W), MXU_DTYPE),
                   jax.ShapeDtypeStruct((b, MLA_QK_W, s), MXU_DTYPE),
                   jax.ShapeDtypeStruct((b, s, MLA_QK_W), MXU_DTYPE)],
        compiler_params=pltpu.CompilerParams(
            dimension_semantics=("arbitrary", "arbitrary"), vmem_limit_bytes=VMEM_LIMIT_BYTES),
        name="mla_prep",
    )(arr, arr, arr, q_norm.reshape(1, -1), kv_norm.reshape(1, -1), wqa, wqb, wk, wv,
      q_cos, q_sin, k_cos, k_sin)
    tq, tk = min(MLA_TQ, s), min(MLA_TK, s)
    assert s % tq == 0 and s % tk == 0 and s % tt == 0
    return pl.pallas_call(
        _mla_flash_kernel,
        grid=(b, s // tq, s // tk),
        in_specs=[pl.BlockSpec((1, tq, MLA_QK_W), lambda i, j, kk: (i, j, 0)),
                  pl.BlockSpec((1, MLA_QK_W, tk), lambda i, j, kk: (i, 0, kk)),
                  pl.BlockSpec((1, tk, MLA_QK_W), lambda i, j, kk: (i, kk, 0))],
        out_specs=pl.BlockSpec((1, tq, MLA_V_W), lambda i, j, kk: (i, j, 0)),
        out_shape=jax.ShapeDtypeStruct((b, s, MLA_V_W), F32),
        scratch_shapes=[pltpu.VMEM((MLA_HEADS, tq, MLA_HG), F32),
                        pltpu.VMEM((MLA_HEADS, tq, MLA_HG), F32)],
        compiler_params=pltpu.CompilerParams(
            dimension_semantics=("arbitrary", "arbitrary", "arbitrary"), vmem_limit_bytes=VMEM_LIMIT_BYTES),
        name="mla_flash",
    )(q, kt, v)


def t5_bucket(rel):
    nb = T5_BUCKETS // 2
    ret = np.where(rel > 0, nb, 0)
    n = np.abs(rel)
    max_exact = nb // 2
    large = max_exact + (np.log(np.maximum(n, 1) / max_exact) / np.log(T5_MAX_DIST / max_exact)
                         * (nb - max_exact)).astype(np.int64)
    large = np.minimum(large, nb - 1)
    return (ret + np.where(n < max_exact, n, large)).astype(np.int32)


DIL_DIM = DIL_HEADS * DIL_HD
DIL_HALF = 64
DIL_QB = 128
DIL_KW = DIL_QB + 2 * DIL_HALF
DIL_TL = 512
DIL_N_GROUPS = len(DIL_GROUPS)


def _dil_bias_table(t5_table, gi, dil):
    rel = np.arange(DIL_KW)[None, :] - DIL_HALF - np.arange(DIL_QB)[:, None]
    bias = t5_table[:, gi * DIL_HEADS:(gi + 1) * DIL_HEADS][t5_bucket(rel * dil)].astype(F32)
    bias = jnp.where((np.abs(rel) <= DIL_HALF)[:, :, None], bias, -jnp.inf)
    return bias.transpose(2, 0, 1).reshape(DIL_HEADS * DIL_QB, DIL_KW)


def _dil_kernel(prev_ref, cur_ref, next_ref, tab_ref, o_ref, lse_ref, k_scr, v_scr, *, seq_len):
    step = pl.program_id(2)
    for i, ref in enumerate((prev_ref, cur_ref, next_ref)):
        k_scr[i * DIL_TL:(i + 1) * DIL_TL, :] = ref[0, :, DIL_DIM:2 * DIL_DIM].astype(MXU_DTYPE)
        v_scr[i * DIL_TL:(i + 1) * DIL_TL, :] = ref[0, :, 2 * DIL_DIM:3 * DIL_DIM].astype(MXU_DTYPE)
    lane_head = lax.broadcasted_iota(jnp.int32, (DIL_QB, DIL_DIM), 1) // DIL_HD
    key_off = lax.broadcasted_iota(jnp.int32, (1, DIL_KW), 1)
    for n in range(DIL_TL // DIL_QB):
        w0 = DIL_TL + n * DIL_QB - DIL_HALF
        kpos = step * DIL_TL + (n * DIL_QB - DIL_HALF) + key_off
        valid = (kpos >= 0) & (kpos < seq_len)
        q = cur_ref[0, n * DIL_QB:(n + 1) * DIL_QB, 0:DIL_DIM] * np.float32(DIL_HD ** -0.5)
        qs = jnp.concatenate([jnp.where(lane_head == h, q, 0.0) for h in range(DIL_HEADS)], axis=0)
        logits = lax.dot_general(qs.astype(MXU_DTYPE), k_scr[w0:w0 + DIL_KW, :], (((1,), (1,)), ((), ())),
                                 preferred_element_type=F32) + tab_ref[...]
        logits = jnp.where(valid, logits, -jnp.inf)
        m = jnp.max(logits, axis=-1, keepdims=True)
        p = jnp.exp(logits - m)
        denom = jnp.sum(p, axis=-1, keepdims=True)
        o_all = jnp.dot(p.astype(MXU_DTYPE), v_scr[w0:w0 + DIL_KW, :], preferred_element_type=F32) / denom
        lse_all = m + jnp.log(denom)
        o = jnp.zeros((DIL_QB, DIL_DIM), F32)
        lse = jnp.zeros((DIL_QB, DIL_DIM), F32)
        for h in range(DIL_HEADS):
            rows = slice(h * DIL_QB, (h + 1) * DIL_QB)
            o = o + jnp.where(lane_head == h, o_all[rows, :], 0.0)
            lse = lse + jnp.where(lane_head == h, lse_all[rows, :], 0.0)
        o_ref[0, n * DIL_QB:(n + 1) * DIL_QB, :] = o
        lse_ref[0, n * DIL_QB:(n + 1) * DIL_QB, :] = lse


def _dil_group(qkv, t5_table, gi, dil, col):
    b, s, width = qkv.shape
    seq_len = s // dil
    assert seq_len % DIL_TL == 0 and width % (3 * DIL_DIM) == 0 and col % (3 * DIL_DIM) == 0
    n_steps = seq_len // DIL_TL
    gi_col = col // (3 * DIL_DIM) + gi
    if dil > 1:
        qkv = qkv[:, :, gi_col * 3 * DIL_DIM:(gi_col + 1) * 3 * DIL_DIM]
        width, gi_col = 3 * DIL_DIM, 0
    n_col = width // (3 * DIL_DIM)
    view = qkv.reshape(b, seq_len, dil * width)
    blk = (1, DIL_TL, 3 * DIL_DIM)
    out_shape = jax.ShapeDtypeStruct((b, seq_len, dil * DIL_DIM), F32)
    out_spec = pl.BlockSpec((1, DIL_TL, DIL_DIM), lambda i, r, j: (i, j, r))
    o, lse = pl.pallas_call(
        functools.partial(_dil_kernel, seq_len=seq_len),
        grid=(b, dil, n_steps),
        in_specs=[
            pl.BlockSpec(blk, lambda i, r, j: (i, jnp.maximum(j - 1, 0), r * n_col + gi_col)),
            pl.BlockSpec(blk, lambda i, r, j: (i, j, r * n_col + gi_col)),
            pl.BlockSpec(blk, lambda i, r, j: (i, jnp.minimum(j + 1, n_steps - 1), r * n_col + gi_col)),
            pl.BlockSpec((DIL_HEADS * DIL_QB, DIL_KW), lambda i, r, j: (0, 0)),
        ],
        out_specs=[out_spec, out_spec],
        out_shape=[out_shape, out_shape],
        scratch_shapes=[pltpu.VMEM((3 * DIL_TL, DIL_DIM), MXU_DTYPE),
                        pltpu.VMEM((3 * DIL_TL, DIL_DIM), MXU_DTYPE)],
        compiler_params=pltpu.CompilerParams(
            dimension_semantics=("arbitrary", "arbitrary", "arbitrary"), vmem_limit_bytes=VMEM_LIMIT_BYTES),
        name=f"dilated_attention_g{gi}",
    )(view, view, view, _dil_bias_table(t5_table, gi, dil))
    return o.reshape(b, s, DIL_DIM), lse.reshape(b, s, DIL_DIM)


def _dil_combine_kernel(*refs):
    o_refs, lse_refs, out_ref = refs[:DIL_N_GROUPS], refs[DIL_N_GROUPS:2 * DIL_N_GROUPS], refs[-1]
    lses = [r[...] for r in lse_refs]
    m = functools.reduce(jnp.maximum, lses)
    ws = [jnp.exp(l - m) for l in lses]
    total = functools.reduce(jnp.add, ws)
    acc = functools.reduce(jnp.add, [w * r[...] for w, r in zip(ws, o_refs)])
    out_ref[...] = acc / total


def dilated_mixer(qkv, t5_table, col=0):
    b, s, _ = qkv.shape
    outs, lses = [], []
    for gi, (win, dil) in enumerate(DIL_GROUPS):
        assert win // (2 * dil) == DIL_HALF
        o, lse = _dil_group(qkv, t5_table, gi, dil, col)
        outs.append(o.reshape(b * s, DIL_DIM))
        lses.append(lse.reshape(b * s, DIL_DIM))
    tm = 1024
    spec = pl.BlockSpec((tm, DIL_DIM), lambda i: (i, 0))
    out = pl.pallas_call(
        _dil_combine_kernel,
        grid=(b * s // tm,),
        in_specs=[spec] * (2 * DIL_N_GROUPS),
        out_specs=spec,
        out_shape=jax.ShapeDtypeStruct((b * s, DIL_DIM), F32),
        compiler_params=pltpu.CompilerParams(
            dimension_semantics=("arbitrary",), vmem_limit_bytes=VMEM_LIMIT_BYTES),
        name="dilated_combine",
    )(*outs, *lses)
    return out.reshape(b, s, DIL_DIM)


SSM_CONV_TOK = 512
SSM_HALO = SUBLANES
SSM_HEADS_PER_GROUP = SSM_HEADS // SSM_GROUPS
SSM_GROUP_W = SSM_HEADS_PER_GROUP * SSM_HD
SSM_BC_W = SSM_GROUPS * SSM_STATE


def _softplus(x):
    return jnp.maximum(x, 0.0) + jnp.log1p(jnp.exp(-jnp.abs(x)))


def _ssm_conv_kernel(prev_ref, cur_ref, next_ref, w_ref, b_ref, out_ref, cat_scr):
    j = pl.program_id(1)
    tl = cur_ref.shape[1]
    cat_scr[0:SSM_HALO, :] = jnp.where(j > 0, prev_ref[0], 0.0)
    cat_scr[SSM_HALO:SSM_HALO + tl, :] = cur_ref[0]
    cat_scr[SSM_HALO + tl:, :] = jnp.where(j < pl.num_programs(1) - 1, next_ref[0], 0.0)
    acc = jnp.zeros((tl, CONV_CH), F32) + b_ref[...]
    for k in range(SSM_CONV):
        off = SSM_HALO + k - SSM_CONV // 2
        acc = acc + cat_scr[off:off + tl, :] * w_ref[k:k + 1, :]
    out_ref[0] = acc * jax.nn.sigmoid(acc)


def _ssm_conv(xbc, conv_w, conv_b, col):
    b, l, _ = xbc.shape
    tl = SSM_CONV_TOK
    n_steps = l // tl
    per = tl // SSM_HALO
    assert col % CONV_CH == 0
    cb = col // CONV_CH
    return pl.pallas_call(
        _ssm_conv_kernel,
        grid=(b, n_steps),
        in_specs=[
            pl.BlockSpec((1, SSM_HALO, CONV_CH), lambda i, j: (i, jnp.maximum(j * per - 1, 0), cb)),
            pl.BlockSpec((1, tl, CONV_CH), lambda i, j: (i, j, cb)),
            pl.BlockSpec((1, SSM_HALO, CONV_CH), lambda i, j: (i, jnp.minimum((j + 1) * per, n_steps * per - 1), cb)),
            pl.BlockSpec((SSM_CONV, CONV_CH), lambda i, j: (0, 0)),
            pl.BlockSpec((1, CONV_CH), lambda i, j: (0, 0)),
        ],
        out_specs=pl.BlockSpec((1, tl, CONV_CH), lambda i, j: (i, j, 0)),
        out_shape=jax.ShapeDtypeStruct((b, l, CONV_CH), F32),
        scratch_shapes=[pltpu.VMEM((tl + 2 * SSM_HALO, CONV_CH), F32)],
        compiler_params=pltpu.CompilerParams(
            dimension_semantics=("arbitrary", "arbitrary"), vmem_limit_bytes=VMEM_LIMIT_BYTES),
        name="ssm_conv",
    )(xbc, xbc, xbc, conv_w, conv_b.reshape(1, CONV_CH))


def _ssd_kernel(xbc_ref, dt_ref, dtt_ref, expand_ref, bias_e_ref, a_e_ref, bias_t_ref, a_t_ref,
                y_ref, state_scr, *, reverse):
    q = SSM_CHUNK

    @pl.when(pl.program_id(1) == 0)
    def _():
        state_scr[...] = jnp.zeros(state_scr.shape, F32)

    hi = lax.Precision.HIGHEST
    xbc = xbc_ref[0]
    xs = xbc[:, 0:SSM_INNER]
    dt_e = _softplus(jnp.dot(dt_ref[0], expand_ref[...], precision=hi, preferred_element_type=F32)
                     + bias_e_ref[...])
    a_e = dt_e * a_e_ref[...]
    ri = lax.broadcasted_iota(jnp.int32, (q, q), 0)
    ci = lax.broadcasted_iota(jnp.int32, (q, q), 1)
    seen = (ci >= ri) if reverse else (ci <= ri)
    cs_e = jnp.dot(seen.astype(F32), a_e, precision=hi, preferred_element_type=F32)
    dt_t = _softplus(dtt_ref[0] + bias_t_ref[...])
    cs_t = jnp.dot(dt_t * a_t_ref[...], seen.T.astype(F32), precision=hi, preferred_element_type=F32)
    dtx = xs * dt_e
    last = 0 if reverse else q - 1
    total = cs_e[last:last + 1, :]
    dtx_decayed = jnp.exp(total - cs_e) * dtx
    grow = jnp.exp(cs_e)
    chunk_decay = jnp.exp(total)
    lane_head = lax.broadcasted_iota(jnp.int32, (q, SSM_GROUP_W), 1) // SSM_HD
    for g in range(SSM_GROUPS):
        xl = slice(g * SSM_GROUP_W, (g + 1) * SSM_GROUP_W)
        bg = xbc[:, SSM_INNER + g * SSM_STATE:SSM_INNER + (g + 1) * SSM_STATE]
        cg = xbc[:, SSM_INNER + SSM_BC_W + g * SSM_STATE:SSM_INNER + SSM_BC_W + (g + 1) * SSM_STATE]
        cb = lax.dot_general(cg.astype(MXU_DTYPE), bg.astype(MXU_DTYPE), (((1,), (1,)), ((), ())),
                             preferred_element_type=F32)
        ms = []
        for r in range(SSM_HEADS_PER_GROUP):
            h = g * SSM_HEADS_PER_GROUP + r
            col = cs_e[:, h * SSM_HD:h * SSM_HD + 1]
            row = cs_t[h:h + 1, :]
            ms.append(cb * jnp.where(seen, jnp.exp(col - row), 0.0))
        y_all = jnp.dot(jnp.concatenate(ms, axis=0).astype(MXU_DTYPE), dtx[:, xl].astype(MXU_DTYPE),
                        preferred_element_type=F32)
        y_diag = jnp.zeros((q, SSM_GROUP_W), F32)
        for r in range(SSM_HEADS_PER_GROUP):
            y_diag = y_diag + jnp.where(lane_head == r, y_all[r * q:(r + 1) * q, :], 0.0)
        s_in = state_scr[g]
        y_off = jnp.dot(cg.astype(MXU_DTYPE), s_in.astype(MXU_DTYPE), preferred_element_type=F32) * grow[:, xl]
        y_ref[0, :, xl] = y_diag + y_off
        new = jnp.dot(bg.T.astype(MXU_DTYPE), dtx_decayed[:, xl].astype(MXU_DTYPE), preferred_element_type=F32)
        state_scr[g] = s_in * chunk_decay[:, xl] + new


def _ssd_direction(xbc_act, dt_arr, dt_t, a_log, dt_bias, direction, col_dt, dt_w):
    b, l, _ = xbc_act.shape
    assert col_dt % dt_w == 0 and dt_w >= 2 * SSM_HEADS
    q = SSM_CHUNK
    nc = l // q
    reverse = direction == 1
    a = -jnp.exp(a_log[direction].astype(F32))
    bias = dt_bias[direction].astype(F32)
    head_of_lane = np.arange(SSM_INNER) // SSM_HD
    expand = (np.arange(dt_w)[:, None] == direction * SSM_HEADS + head_of_lane[None, :]).astype(np.float32)
    chunk = (lambda j: nc - 1 - j) if reverse else (lambda j: j)
    full = lambda r, c: pl.BlockSpec((r, c), lambda i, j: (0, 0))
    return pl.pallas_call(
        functools.partial(_ssd_kernel, reverse=reverse),
        grid=(b, nc),
        in_specs=[
            pl.BlockSpec((1, q, CONV_CH), lambda i, j: (i, chunk(j), 0)),
            pl.BlockSpec((1, q, dt_w), lambda i, j: (i, chunk(j), col_dt // dt_w)),
            pl.BlockSpec((1, SSM_HEADS, q), lambda i, j: (i * 2 + direction, 0, chunk(j))),
            full(dt_w, SSM_INNER), full(1, SSM_INNER), full(1, SSM_INNER),
            full(SSM_HEADS, 1), full(SSM_HEADS, 1),
        ],
        out_specs=pl.BlockSpec((1, q, SSM_INNER), lambda i, j: (i, chunk(j), 0)),
        out_shape=jax.ShapeDtypeStruct((b, l, SSM_INNER), F32),
        scratch_shapes=[pltpu.VMEM((SSM_GROUPS, SSM_STATE, SSM_GROUP_W), F32)],
        compiler_params=pltpu.CompilerParams(
            dimension_semantics=("arbitrary", "arbitrary"), vmem_limit_bytes=VMEM_LIMIT_BYTES),
        name=f"ssd_dir{direction}",
    )(xbc_act, dt_arr, dt_t, jnp.asarray(expand), bias[head_of_lane].reshape(1, SSM_INNER),
      a[head_of_lane].reshape(1, SSM_INNER), bias.reshape(SSM_HEADS, 1), a.reshape(SSM_HEADS, 1))


def _ssm_gate_kernel(yf_ref, yb_ref, xbc_ref, z_ref, d_ref, g_ref, out_ref):
    z = z_ref[...]
    y = (yf_ref[...] + yb_ref[...] + xbc_ref[...] * d_ref[...]) * (z * jax.nn.sigmoid(z))
    out_ref[...] = y * lax.rsqrt(jnp.mean(y * y, axis=-1, keepdims=True) + EPS) * g_ref[...]


def mamba2_mixer(arr, conv_w, conv_b, A_log, dt_bias, D_skip, norm_g,
                 col_z=0, col_xbc=CONV_CH, col_dt=SSM_INNER + CONV_CH, dt_w=2 * SSM_HEADS):
    b, l, width = arr.shape
    assert col_z % SSM_INNER == 0
    xbc_act = _ssm_conv(arr, conv_w, conv_b, col_xbc)
    dt_t = jnp.swapaxes(arr[:, :, col_dt:col_dt + 2 * SSM_HEADS], 1, 2).reshape(b * 2, SSM_HEADS, l)
    y_f = _ssd_direction(xbc_act, arr, dt_t, A_log, dt_bias, 0, col_dt, dt_w)
    y_b = _ssd_direction(xbc_act, arr, dt_t, A_log, dt_bias, 1, col_dt, dt_w)
    tm = 1024
    tok = lambda cb: pl.BlockSpec((tm, SSM_INNER), lambda i: (i, cb))
    row = pl.BlockSpec((1, SSM_INNER), lambda i: (0, 0))
    d_e = D_skip.astype(F32)[np.arange(SSM_INNER) // SSM_HD].reshape(1, SSM_INNER)
    out = pl.pallas_call(
        _ssm_gate_kernel,
        grid=(b * l // tm,),
        in_specs=[tok(0), tok(0), tok(0), tok(col_z // SSM_INNER), row, row],
        out_specs=tok(0),
        out_shape=jax.ShapeDtypeStruct((b * l, SSM_INNER), F32),
        compiler_params=pltpu.CompilerParams(
            dimension_semantics=("arbitrary",), vmem_limit_bytes=VMEM_LIMIT_BYTES),
        name="ssm_gate",
    )(y_f.reshape(b * l, SSM_INNER), y_b.reshape(b * l, SSM_INNER), xbc_act.reshape(b * l, CONV_CH),
      arr.reshape(b * l, width), d_e, norm_g.reshape(1, SSM_INNER))
    return out.reshape(b, l, SSM_INNER)


NA_DIM = NA_HEADS * NA_HD
NA_ROWS_PER_STEP = 8
NA_WIN = NA_ROWS * GRID_W
NA_STEP_TOK = NA_ROWS_PER_STEP * GRID_W


def _na_bias_table(rpb):
    n_dc = 2 * NA_COLS - 1
    edge_l = jnp.repeat(rpb[:, :, :1], GRID_W, axis=2)
    edge_r = jnp.repeat(rpb[:, :, -1:], GRID_W, axis=2)
    ext = jnp.concatenate([edge_l, rpb.astype(F32), edge_r], axis=2)
    by_col = jnp.stack([ext[:, :, GRID_W + NA_COLS - 1 - qc:2 * GRID_W + NA_COLS - 1 - qc]
                        for qc in range(GRID_W)], axis=2)
    qc = np.arange(GRID_W)[:, None]
    kc = np.arange(GRID_W)[None, :]
    cs = np.clip(qc - NA_COLS // 2, 0, GRID_W - NA_COLS)
    ok = (kc >= cs) & (kc < cs + NA_COLS)
    by_col = jnp.where(ok[None, None], by_col, -jnp.inf)
    tabs = []
    for delta in range(NA_ROWS):
        rows = by_col[:, NA_ROWS - 1 - delta:2 * NA_ROWS - 1 - delta]
        tabs.append(rows.transpose(0, 2, 1, 3).reshape(NA_HEADS * GRID_W, NA_WIN))
    assert n_dc == rpb.shape[2]
    return jnp.stack(tabs, axis=0)


def _na_kernel(prev_ref, cur_ref, next_ref, tab_ref, out_ref, k_scr, v_scr, *, n_rows):
    step = pl.program_id(1)
    for i, ref in enumerate((prev_ref, cur_ref, next_ref)):
        k_scr[i * NA_STEP_TOK:(i + 1) * NA_STEP_TOK, :] = ref[0, :, NA_DIM:2 * NA_DIM].astype(MXU_DTYPE)
        v_scr[i * NA_STEP_TOK:(i + 1) * NA_STEP_TOK, :] = ref[0, :, 2 * NA_DIM:3 * NA_DIM].astype(MXU_DTYPE)
    lane_head = lax.broadcasted_iota(jnp.int32, (GRID_W, NA_DIM), 1) // NA_HD
    row0 = step * NA_ROWS_PER_STEP
    for j in range(NA_ROWS_PER_STEP):
        r = row0 + j
        r0 = jnp.clip(r - NA_ROWS // 2, 0, n_rows - NA_ROWS)
        start = pl.multiple_of((r0 - row0 + NA_ROWS_PER_STEP) * GRID_W, GRID_W)
        q = cur_ref[0, j * GRID_W:(j + 1) * GRID_W, 0:NA_DIM] * np.float32(NA_HD ** -0.5)
        qs = jnp.concatenate([jnp.where(lane_head == h, q, 0.0) for h in range(NA_HEADS)], axis=0)
        kw = k_scr[pl.ds(start, NA_WIN), :]
        vw = v_scr[pl.ds(start, NA_WIN), :]
        logits = lax.dot_general(qs.astype(MXU_DTYPE), kw, (((1,), (1,)), ((), ())),
                                 preferred_element_type=F32) + tab_ref[r - r0]
        m = jnp.max(logits, axis=-1, keepdims=True)
        p = jnp.exp(logits - m)
        denom = jnp.sum(p, axis=-1, keepdims=True)
        o_all = jnp.dot(p.astype(MXU_DTYPE), vw, preferred_element_type=F32) / denom
        o = jnp.zeros((GRID_W, NA_DIM), F32)
        for h in range(NA_HEADS):
            o = o + jnp.where(lane_head == h, o_all[h * GRID_W:(h + 1) * GRID_W, :], 0.0)
        out_ref[0, j * GRID_W:(j + 1) * GRID_W, :] = o


def na_mixer(qkv, rpb, col=0):
    b, s, _ = qkv.shape
    n_rows = s // GRID_W
    assert n_rows >= NA_ROWS and n_rows % NA_ROWS_PER_STEP == 0 and col % (3 * NA_DIM) == 0
    n_steps = n_rows // NA_ROWS_PER_STEP
    blk = (1, NA_STEP_TOK, 3 * NA_DIM)
    cb = col // (3 * NA_DIM)
    return pl.pallas_call(
        functools.partial(_na_kernel, n_rows=n_rows),
        grid=(b, n_steps),
        in_specs=[
            pl.BlockSpec(blk, lambda i, j: (i, jnp.maximum(j - 1, 0), cb)),
            pl.BlockSpec(blk, lambda i, j: (i, j, cb)),
            pl.BlockSpec(blk, lambda i, j: (i, jnp.minimum(j + 1, n_steps - 1), cb)),
            pl.BlockSpec((NA_ROWS, NA_HEADS * GRID_W, NA_WIN), lambda i, j: (0, 0, 0)),
        ],
        out_specs=pl.BlockSpec((1, NA_STEP_TOK, NA_DIM), lambda i, j: (i, j, 0)),
        out_shape=jax.ShapeDtypeStruct((b, s, NA_DIM), F32),
        scratch_shapes=[
            pltpu.VMEM((3 * NA_STEP_TOK, NA_DIM), MXU_DTYPE),
            pltpu.VMEM((3 * NA_STEP_TOK, NA_DIM), MXU_DTYPE),
        ],
        compiler_params=pltpu.CompilerParams(
            dimension_semantics=("arbitrary", "arbitrary"), vmem_limit_bytes=VMEM_LIMIT_BYTES),
        name="na_attention",
    )(qkv, qkv, qkv, _na_bias_table(rpb))


PK_GATE = 0
PK_XBC = PK_GATE + N_BRANCH * D_MODEL
PK_CQ = PK_XBC + CONV_CH
PK_DIL = PK_CQ + MLA_Q_RANK
PK_NA = PK_DIL + DIL_N_GROUPS * 3 * DIL_DIM
PK_KR = PK_NA + 3 * NA_DIM
PK_Z = PK_KR + 2 * MLA_HG
PK_CKV = PK_Z + SSM_INNER
PK_DT = PK_CKV + MLA_KV_RANK
PK_DT_W = LANES
PK_WIDTH = 13 * 3 * DIL_DIM
INPROJ_TM = 512
INPROJ_TN = PK_WIDTH // 2
MERGE_TM = 512


def _pack_w_in(w_in_l):
    gate, a_cq, a_ckv, a_kr, b_qkv, c_z, c_xbc, c_dt, d_qkv = jnp.split(w_in_l, IN_SPLITS, axis=-1)
    zeros = lambda n: jnp.zeros((D_MODEL, n), w_in_l.dtype)
    cols = [gate, c_xbc, a_cq, b_qkv, d_qkv, mla_rotary_key_columns(a_kr), c_z, a_ckv,
            c_dt, zeros(PK_DT_W - 2 * SSM_HEADS)]
    packed = jnp.concatenate(cols, axis=-1)
    assert packed.shape[1] == PK_DT + PK_DT_W
    return jnp.concatenate([packed, zeros(PK_WIDTH - packed.shape[1])], axis=-1).astype(MXU_DTYPE)


def _inproj_kernel(x_ref, g_ref, w_ref, out_ref, h_scr):
    @pl.when(pl.program_id(1) == 0)
    def _():
        x = x_ref[...]
        h_scr[...] = (x * lax.rsqrt(jnp.mean(x * x, axis=-1, keepdims=True) + EPS) * g_ref[...]).astype(MXU_DTYPE)

    out_ref[...] = jnp.dot(h_scr[...], w_ref[...], preferred_element_type=F32)


def _inproj(x2d, norm_g, w_packed):
    n_tok = x2d.shape[0]
    tm, tn = INPROJ_TM, INPROJ_TN
    return pl.pallas_call(
        _inproj_kernel,
        grid=(n_tok // tm, PK_WIDTH // tn),
        in_specs=[pl.BlockSpec((tm, D_MODEL), lambda i, j: (i, 0)),
                  pl.BlockSpec((1, D_MODEL), lambda i, j: (0, 0)),
                  pl.BlockSpec((D_MODEL, tn), lambda i, j: (0, j))],
        out_specs=pl.BlockSpec((tm, tn), lambda i, j: (i, j)),
        out_shape=jax.ShapeDtypeStruct((n_tok, PK_WIDTH), F32),
        scratch_shapes=[pltpu.VMEM((tm, D_MODEL), MXU_DTYPE)],
        compiler_params=pltpu.CompilerParams(
            dimension_semantics=("arbitrary", "arbitrary"), vmem_limit_bytes=VMEM_LIMIT_BYTES),
        name="in_projection",
    )(x2d, norm_g.reshape(1, D_MODEL), w_packed)


def _merge_kernel(x_ref, gate_ref, bg_ref, ya_ref, yb_ref, yc_ref, yd_ref, wb_ref, wo_ref, out_ref):
    merged = jnp.zeros(x_ref.shape, F32)
    for i, y_ref in enumerate((ya_ref, yb_ref, yc_ref, yd_ref)):
        proj = jnp.dot(y_ref[...].astype(MXU_DTYPE), wb_ref[BRANCH_ROWS[i]:BRANCH_ROWS[i + 1], :],
                       preferred_element_type=F32)
        lanes = slice(i * D_MODEL, (i + 1) * D_MODEL)
        merged = merged + jax.nn.sigmoid(gate_ref[:, lanes] + bg_ref[:, lanes]) * proj
    out_ref[...] = x_ref[...] + jnp.dot(merged.astype(MXU_DTYPE), wo_ref[...], preferred_element_type=F32)


def _merge(x2d, packed, b_gate, ys, w_branch, w_out):
    n_tok = x2d.shape[0]
    tm = MERGE_TM
    tok = lambda w: pl.BlockSpec((tm, w), lambda i: (i, 0))
    full = lambda r, c: pl.BlockSpec((r, c), lambda i: (0, 0))
    n_gate = N_BRANCH * D_MODEL
    return pl.pallas_call(
        _merge_kernel,
        grid=(n_tok // tm,),
        in_specs=[tok(D_MODEL), tok(n_gate), full(1, n_gate)] + [tok(w) for w in BRANCH_WIDTHS]
                 + [full(BRANCH_ROWS[-1], D_MODEL), full(D_MODEL, D_MODEL)],
        out_specs=tok(D_MODEL),
        out_shape=jax.ShapeDtypeStruct((n_tok, D_MODEL), F32),
        compiler_params=pltpu.CompilerParams(
            dimension_semantics=("arbitrary",), vmem_limit_bytes=VMEM_LIMIT_BYTES),
        name="branch_merge",
    )(x2d, packed, b_gate.reshape(1, n_gate), *[y.reshape(n_tok, -1) for y in ys], w_branch, w_out)


def encoder(x, norm1_g, w_in_packed, b_gate, mla_q_norm, mla_w_qb, mla_kv_norm, mla_w_kvb, t5_table,
            ssm_conv_w, ssm_conv_b, ssm_A_log, ssm_dt_bias, ssm_D, ssm_norm_g, na_rpb,
            w_branch, w_out, norm2_g, peer_wq, peer_keys, peer_u, peer_vt, final_g):
    b, s, _ = x.shape
    x2d = x.reshape(b * s, D_MODEL)
    for l in range(DEPTH):
        packed = _inproj(x2d, norm1_g[l], w_in_packed[l])
        p3 = packed.reshape(b, s, PK_WIDTH)
        y_a = mla_mixer(p3, mla_q_norm[l], mla_w_qb[l], mla_kv_norm[l], mla_w_kvb[l],
                        col_q=PK_CQ, col_kv=PK_CKV, col_kr=PK_KR)
        y_b = dilated_mixer(p3, t5_table, col=PK_DIL)
        y_c = mamba2_mixer(p3, ssm_conv_w[l], ssm_conv_b[l], ssm_A_log[l], ssm_dt_bias[l], ssm_D[l],
                           ssm_norm_g[l], col_z=PK_Z, col_xbc=PK_XBC, col_dt=PK_DT, dt_w=PK_DT_W)
        y_d = na_mixer(p3, na_rpb[l], col=PK_NA)
        x2d = _merge(x2d, packed, b_gate[l], (y_a, y_b, y_c, y_d), w_branch[l], w_out[l])
        x2d = peer_block(x2d, norm2_g[l], peer_wq[l], peer_keys[l], peer_u[l], peer_vt[l],
                         final_g, final_norm=(l == DEPTH - 1))
    return x2d.reshape(b, s, D_MODEL)


def kernel(x_prompt, x_sample, norm1_g, w_in, b_gate, mla_q_norm, mla_w_qb, mla_kv_norm, mla_w_kvb, t5_table, ssm_conv_w, ssm_conv_b, ssm_A_log, ssm_dt_bias, ssm_D, ssm_norm_g, na_rpb, w_branch, w_out, norm2_g, peer_wq, peer_keys, peer_u, peer_v, final_g):
    peer_u16 = peer_u.astype(MXU_DTYPE)
    peer_vt16 = jnp.swapaxes(peer_v, 1, 2).astype(MXU_DTYPE)
    w_in_packed = jnp.stack([_pack_w_in(w_in[l]) for l in range(DEPTH)])
    shared = (norm1_g, w_in_packed, b_gate, mla_q_norm, mla_w_qb, mla_kv_norm, mla_w_kvb, t5_table,
              ssm_conv_w, ssm_conv_b, ssm_A_log, ssm_dt_bias, ssm_D, ssm_norm_g, na_rpb,
              w_branch.astype(MXU_DTYPE), w_out.astype(MXU_DTYPE), norm2_g, peer_wq, peer_keys,
              peer_u16, peer_vt16, final_g)
    y_prompt = encoder(x_prompt, *shared)
    y_sample = encoder(x_sample, *shared)
    return (y_prompt, y_sample)
```

```python
import functools
import math

import numpy as np
import jax
import jax.numpy as jnp
from jax import lax
from jax.experimental import pallas as pl
from jax.experimental.pallas import tpu as pltpu

F32 = jnp.float32
BF16 = jnp.bfloat16
MXU_DTYPE = BF16

D_MODEL = 1024
DEPTH = 2
GRID_W = 64
EPS = 1e-6
N_BRANCH = 4

MLA_HEADS = 4
MLA_Q_RANK = 256
MLA_KV_RANK = 128
MLA_NOPE = 64
MLA_ROPE = 32
MLA_V = 64
ROPE_THETA = 10000.0
Q_BLOCK = 128

DIL_GROUPS = ((128, 1), (512, 4), (2048, 16))
DIL_HEADS = 4
DIL_HD = 64
T5_BUCKETS = 32
T5_MAX_DIST = 1024

SSM_HEADS = 8
SSM_HD = 64
SSM_INNER = SSM_HEADS * SSM_HD
SSM_GROUPS = 2
SSM_STATE = 128
SSM_CONV = 7
SSM_CHUNK = 128
CONV_CH = SSM_INNER + 2 * SSM_GROUPS * SSM_STATE

NA_HEADS = 4
NA_HD = 64
NA_ROWS = 8
NA_COLS = 16
NA_QCB = 16
NA_KCB = NA_QCB + NA_COLS

PEER_HEADS = 8
PEER_KEYS = 128
PEER_EXPERTS = PEER_KEYS * PEER_KEYS
PEER_QDIM = 256
PEER_TOPK = 16
PEER_TOK_BLOCK = 128

BRANCH_WIDTHS = (MLA_HEADS * MLA_V, DIL_HEADS * DIL_HD, SSM_INNER, NA_HEADS * NA_HD)
BRANCH_ROWS = tuple(sum(BRANCH_WIDTHS[:i]) for i in range(N_BRANCH + 1))
IN_SIZES = (N_BRANCH * D_MODEL, MLA_Q_RANK, MLA_KV_RANK, MLA_ROPE,
            len(DIL_GROUPS) * 3 * DIL_HEADS * DIL_HD,
            SSM_INNER, CONV_CH, 2 * SSM_HEADS,
            3 * NA_HEADS * NA_HD)
IN_SPLITS = tuple(sum(IN_SIZES[:i + 1]) for i in range(len(IN_SIZES) - 1))

VMEM_LIMIT_BYTES = 56 * 1024 * 1024
LANES = 128
SUBLANES = 8


def rms_norm(x, g):
    x32 = x.astype(F32)
    y = x32 * lax.rsqrt(jnp.mean(x32 * x32, axis=-1, keepdims=True) + EPS)
    return (y * g.astype(F32)).astype(x.dtype)


PEER_ROUTE_TB = 256
PEER_TB = 256
PEER_EC = 1024
PEER_GATE_DTYPE = BF16
PEER_GATE_ROWS = 16
PEER_HALF = PEER_QDIM // 2
PEER_CAND_ROWS = 2 * SUBLANES + 7 * SUBLANES + SUBLANES


def _gelu_exact(x):
    return 0.5 * x * (1.0 + lax.erf(x * np.float32(math.sqrt(0.5))))


def _extract_desc(vals, n_out, out_ref, row0, with_rank=False):
    rank = jnp.full(vals.shape, float(n_out), F32) if with_rank else None
    for k in range(n_out):
        m = jnp.max(vals, axis=0, keepdims=True)
        out_ref[pl.ds(row0 + k, 1), :] = m
        hit = vals == m
        if with_rank:
            rank = jnp.where(hit, float(k), rank)
        vals = jnp.where(hit, -jnp.inf, vals)
    return rank


def _peer_route_kernel(x_ref, g_ref, wqt_ref, keys_ref,
                       xnt_ref, cnt_ref, rank_ref, e1_ref, e2_ref,
                       qt_scr, top_scr, cand_scr, tops_scr):
    x = x_ref[...]
    xn = x * lax.rsqrt(jnp.mean(x * x, axis=-1, keepdims=True) + EPS) * g_ref[...]
    xnt = xn.T.astype(MXU_DTYPE)
    xnt_ref[...] = xnt
    qt_scr[...] = jnp.dot(wqt_ref[...], xnt, preferred_element_type=F32).astype(MXU_DTYPE)

    def head(h, carry):
        q1 = qt_scr[pl.ds(pl.multiple_of(h * PEER_QDIM, PEER_QDIM), PEER_HALF), :]
        q2 = qt_scr[pl.ds(pl.multiple_of(h * PEER_QDIM + PEER_HALF, PEER_HALF), PEER_HALF), :]
        s1 = jnp.dot(keys_ref[2 * h], q1, preferred_element_type=F32)
        s2 = jnp.dot(keys_ref[2 * h + 1], q2, preferred_element_type=F32)
        _extract_desc(s1, PEER_TOPK, top_scr, 0)
        rank2 = _extract_desc(s2, PEER_TOPK, top_scr, PEER_TOPK, with_rank=True)
        t1 = top_scr[0:PEER_TOPK, :]
        t2 = top_scr[PEER_TOPK:2 * PEER_TOPK, :]
        cand_scr[0:2 * SUBLANES, :] = t1[0:1, :] + t2
        for a in range(1, SUBLANES):
            cand_scr[(a + 1) * SUBLANES:(a + 2) * SUBLANES, :] = t1[a:a + 1, :] + t2[0:SUBLANES, :]
        cand_scr[9 * SUBLANES:10 * SUBLANES, :] = t1[SUBLANES:2 * SUBLANES, :] + t2[0:1, :]
        _extract_desc(cand_scr[...], PEER_TOPK, tops_scr, 0)
        top_s = tops_scr[...]
        z = jnp.sum(jnp.exp(top_s - top_s[0:1, :]), axis=0, keepdims=True)
        tau = top_s[PEER_TOPK - 1:PEER_TOPK, :]
        cnt_top = jnp.zeros(t1.shape, F32)
        for b in range(PEER_TOPK):
            cnt_top = cnt_top + jnp.where(t1 + t2[b:b + 1, :] >= tau, 1.0, 0.0)
        cnt = jnp.zeros(s1.shape, F32)
        for a in range(PEER_TOPK):
            cnt = jnp.where(s1 == t1[a:a + 1, :], cnt_top[a:a + 1, :], cnt)
        cnt_ref[h] = cnt
        rank_ref[h] = rank2.astype(rank_ref.dtype)
        e1_ref[h] = jnp.exp(s1 - t1[0:1, :]) / z
        e2_ref[h] = jnp.exp(s2 - t2[0:1, :]).astype(e2_ref.dtype)
        return carry

    lax.fori_loop(0, PEER_HEADS, head, 0)


def _peer_route(x2d, g, wqt, keys):
    n_tok = x2d.shape[0]
    tb = PEER_ROUTE_TB
    rt_shape = jax.ShapeDtypeStruct((PEER_HEADS, PEER_KEYS, n_tok), F32)
    gate_shape = jax.ShapeDtypeStruct((PEER_HEADS, PEER_KEYS, n_tok), PEER_GATE_DTYPE)
    rt_spec = pl.BlockSpec((PEER_HEADS, PEER_KEYS, tb), lambda i: (0, 0, i))
    return pl.pallas_call(
        _peer_route_kernel,
        grid=(n_tok // tb,),
        in_specs=[
            pl.BlockSpec((tb, D_MODEL), lambda i: (i, 0)),
            pl.BlockSpec((1, D_MODEL), lambda i: (0, 0)),
            pl.BlockSpec((PEER_HEADS * PEER_QDIM, D_MODEL), lambda i: (0, 0)),
            pl.BlockSpec((2 * PEER_HEADS, PEER_KEYS, PEER_HALF), lambda i: (0, 0, 0)),
        ],
        out_specs=[
            pl.BlockSpec((D_MODEL, tb), lambda i: (0, i)),
            rt_spec, rt_spec, rt_spec, rt_spec,
        ],
        out_shape=[
            jax.ShapeDtypeStruct((D_MODEL, n_tok), MXU_DTYPE),
            rt_shape, gate_shape, rt_shape, gate_shape,
        ],
        scratch_shapes=[
            pltpu.VMEM((PEER_HEADS * PEER_QDIM, tb), MXU_DTYPE),
            pltpu.VMEM((2 * PEER_TOPK, tb), F32),
            pltpu.VMEM((PEER_CAND_ROWS, tb), F32),
            pltpu.VMEM((PEER_TOPK, tb), F32),
        ],
        compiler_params=pltpu.CompilerParams(
            dimension_semantics=("arbitrary",), vmem_limit_bytes=VMEM_LIMIT_BYTES),
        name="peer_route",
    )(x2d, g, wqt, keys)


def _peer_expert_kernel(x_ref, xnt_ref, cnt_ref, rank_ref, e1_ref, e2_ref, u_ref, vt_ref, fg_ref,
                        out_ref, acc_scr, w_scr, *, final_norm):
    c = pl.program_id(1)
    n_chunks = pl.num_programs(1) - 1
    n_i1 = PEER_EC // PEER_KEYS
    tb = acc_scr.shape[1]
    gdt = PEER_GATE_DTYPE

    @pl.when(c == 0)
    def _():
        acc_scr[...] = jnp.zeros_like(acc_scr)
        w_scr[1] = jnp.zeros(w_scr.shape[1:], w_scr.dtype)

    slot = c % 2
    n_groups = PEER_KEYS // PEER_GATE_ROWS
    acc_scr[...] += jnp.dot(vt_ref[...], w_scr[(c + 1) % 2], preferred_element_type=F32)
    hid = jnp.dot(u_ref[...], xnt_ref[...], preferred_element_type=F32)

    def gate_block(i1l, l0):
        lanes = slice(l0, l0 + LANES)
        gates = [jnp.zeros((PEER_GATE_ROWS, LANES), gdt) for _ in range(n_groups)]
        for h in range(PEER_HEADS):
            cntb = jnp.broadcast_to(cnt_ref[h, i1l:i1l + 1, lanes], (PEER_GATE_ROWS, LANES)).astype(gdt)
            e1b = jnp.broadcast_to(e1_ref[h, i1l:i1l + 1, lanes], (PEER_GATE_ROWS, LANES)).astype(gdt)
            for k in range(n_groups):
                rows = slice(k * PEER_GATE_ROWS, (k + 1) * PEER_GATE_ROWS)
                w = e2_ref[h, rows, lanes] * e1b
                gates[k] = gates[k] + jnp.where(rank_ref[h, rows, lanes] < cntb, w, jnp.zeros_like(w))
        for k in range(n_groups):
            r0 = k * PEER_GATE_ROWS
            e0 = i1l * PEER_KEYS + r0
            act = _gelu_exact(hid[e0:e0 + PEER_GATE_ROWS, lanes]).astype(gdt)
            w_scr[slot, e0:e0 + PEER_GATE_ROWS, lanes] = (act * gates[k]).astype(MXU_DTYPE)

    for i1l in range(n_i1):
        for l0 in range(0, tb, LANES):
            gate_block(i1l, l0)

    @pl.when(c == n_chunks)
    def _():
        y = x_ref[...] + acc_scr[...].T
        if final_norm:
            y = y * lax.rsqrt(jnp.mean(y * y, axis=-1, keepdims=True) + EPS) * fg_ref[...]
        out_ref[...] = y


def _peer_experts(x2d, xnt, cnt, rank2, e1, e2, u, vt, final_g, final_norm):
    n_tok = x2d.shape[0]
    tb, ec = PEER_TB, PEER_EC
    n_chunks = PEER_EXPERTS // ec
    rt_spec = pl.BlockSpec((PEER_HEADS, PEER_KEYS, tb), lambda j, c: (0, 0, j))
    row_spec = pl.BlockSpec((PEER_HEADS, ec // PEER_KEYS, tb),
                            lambda j, c: (0, jnp.minimum(c, n_chunks - 1), j))
    return pl.pallas_call(
        functools.partial(_peer_expert_kernel, final_norm=final_norm),
        grid=(n_tok // tb, n_chunks + 1),
        in_specs=[
            pl.BlockSpec((tb, D_MODEL), lambda j, c: (j, 0)),
            pl.BlockSpec((D_MODEL, tb), lambda j, c: (0, j)),
            row_spec, rt_spec, row_spec, rt_spec,
            pl.BlockSpec((ec, D_MODEL), lambda j, c: (jnp.minimum(c, n_chunks - 1), 0)),
            pl.BlockSpec((D_MODEL, ec), lambda j, c: (0, jnp.maximum(c - 1, 0))),
            pl.BlockSpec((1, D_MODEL), lambda j, c: (0, 0)),
        ],
        out_specs=pl.BlockSpec((tb, D_MODEL), lambda j, c: (j, 0)),
        out_shape=jax.ShapeDtypeStruct((n_tok, D_MODEL), F32),
        scratch_shapes=[
            pltpu.VMEM((D_MODEL, tb), F32),
            pltpu.VMEM((2, ec, tb), MXU_DTYPE),
        ],
        compiler_params=pltpu.CompilerParams(
            dimension_semantics=("arbitrary", "arbitrary"), vmem_limit_bytes=VMEM_LIMIT_BYTES),
        name="peer_experts",
    )(x2d, xnt, cnt, rank2, e1, e2, u, vt, final_g.reshape(1, D_MODEL))


def peer_block(x2d, norm_g, w_q, keys, u, v, final_g, final_norm=False):
    wqt = w_q.T.astype(MXU_DTYPE)
    keys2 = keys.reshape(2 * PEER_HEADS, PEER_KEYS, PEER_HALF).astype(MXU_DTYPE)
    xnt, cnt, rank2, e1, e2 = _peer_route(x2d, norm_g.reshape(1, D_MODEL), wqt, keys2)
    return _peer_experts(x2d, xnt, cnt, rank2, e1, e2, u, v, final_g, final_norm)


MLA_HG = LANES
MLA_QK_W = MLA_HEADS * MLA_HG
MLA_V_W = MLA_HEADS * MLA_V
MLA_PREP_TOK = 512
MLA_TQ = 512
MLA_TK = 2048
MLA_RHALF = MLA_ROPE // 2


def _mla_rope_tables(s):
    inv = ROPE_THETA ** (-jnp.arange(MLA_RHALF, dtype=F32) / MLA_RHALF)
    ang = jnp.arange(s).astype(F32)[:, None] * inv[None, :]
    cos, sin = jnp.cos(ang), jnp.sin(ang)
    zero_pad = jnp.zeros((s, MLA_HG - MLA_NOPE - MLA_ROPE), F32)
    cos_rot = jnp.concatenate([cos, cos, zero_pad], axis=1)
    sin_rot = jnp.concatenate([-sin, sin, zero_pad], axis=1)
    scale = np.float32((MLA_NOPE + MLA_ROPE) ** -0.5)
    q_cos = scale * jnp.concatenate([jnp.ones((s, MLA_NOPE), F32), cos_rot], axis=1)
    q_sin = scale * jnp.concatenate([jnp.zeros((s, MLA_NOPE), F32), sin_rot], axis=1)
    k_cos = jnp.concatenate([jnp.zeros((s, MLA_NOPE), F32), cos_rot], axis=1)
    k_sin = jnp.concatenate([jnp.zeros((s, MLA_NOPE), F32), sin_rot], axis=1)
    return q_cos, q_sin, k_cos, k_sin


def _mla_pack_weights(w_qb, w_kvb):
    hd_q = MLA_NOPE + MLA_ROPE
    wq = w_qb.reshape(MLA_Q_RANK, MLA_HEADS, hd_q)
    rot = wq[:, :, MLA_NOPE:]
    rot_sw = jnp.concatenate([rot[:, :, MLA_RHALF:], rot[:, :, :MLA_RHALF]], axis=2)
    pad = jnp.zeros((MLA_Q_RANK, MLA_HEADS, MLA_HG - hd_q), F32)
    wq_a = jnp.concatenate([wq, pad], axis=2).reshape(MLA_Q_RANK, MLA_QK_W)
    wq_b = jnp.concatenate([jnp.zeros_like(wq[:, :, :MLA_NOPE]), rot_sw, pad], axis=2).reshape(MLA_Q_RANK, MLA_QK_W)
    wkv = w_kvb.reshape(MLA_KV_RANK, MLA_HEADS, MLA_NOPE + MLA_V)
    wk = jnp.concatenate([wkv[:, :, :MLA_NOPE], jnp.zeros((MLA_KV_RANK, MLA_HEADS, MLA_HG - MLA_NOPE), F32)],
                         axis=2).reshape(MLA_KV_RANK, MLA_QK_W)
    wv = jnp.concatenate([wkv[:, :, MLA_NOPE:], jnp.zeros((MLA_KV_RANK, MLA_HEADS, MLA_HG - MLA_V), F32)],
                         axis=2).reshape(MLA_KV_RANK, MLA_QK_W)
    return (wq_a.astype(MXU_DTYPE), wq_b.astype(MXU_DTYPE), wk.astype(MXU_DTYPE), wv.astype(MXU_DTYPE))


def _mla_prep_kernel(cq_ref, ckv_ref, kr_ref, qn_ref, kvn_ref, wqa_ref, wqb_ref, wk_ref, wv_ref,
                     qcos_ref, qsin_ref, kcos_ref, ksin_ref, q_out, kt_out, v_out):
    cq = cq_ref[0]
    cqn = (cq * lax.rsqrt(jnp.mean(cq * cq, axis=-1, keepdims=True) + EPS) * qn_ref[...]).astype(MXU_DTYPE)
    qa = jnp.dot(cqn, wqa_ref[...], preferred_element_type=F32)
    qb = jnp.dot(cqn, wqb_ref[...], preferred_element_type=F32)
    ckv = ckv_ref[0]
    ckvn = (ckv * lax.rsqrt(jnp.mean(ckv * ckv, axis=-1, keepdims=True) + EPS) * kvn_ref[...]).astype(MXU_DTYPE)
    ka = jnp.dot(ckvn, wk_ref[...], preferred_element_type=F32)
    one_lane = (lax.broadcasted_iota(jnp.int32, (1, MLA_QK_W), 1) % MLA_HG == MLA_V).astype(F32)
    v_out[0] = (jnp.dot(ckvn, wv_ref[...], preferred_element_type=F32) + one_lane).astype(MXU_DTYPE)
    kr = kr_ref[0]
    k_rot = kr[:, 0:MLA_HG] * kcos_ref[...] + kr[:, MLA_HG:2 * MLA_HG] * ksin_ref[...]
    for h in range(MLA_HEADS):
        lanes = slice(h * MLA_HG, (h + 1) * MLA_HG)
        q_out[0, :, lanes] = (qa[:, lanes] * qcos_ref[...] + qb[:, lanes] * qsin_ref[...]).astype(MXU_DTYPE)
        kt_out[0, lanes, :] = (ka[:, lanes] + k_rot).T.astype(MXU_DTYPE)


def _mla_flash_kernel(q_ref, kt_ref, v_ref, out_ref, m_scr, acc_scr):
    ki = pl.program_id(2)

    @pl.when(ki == 0)
    def _():
        m_scr[...] = jnp.full(m_scr.shape, -jnp.inf, F32)
        acc_scr[...] = jnp.zeros(acc_scr.shape, F32)

    def logits(h):
        lanes = slice(h * MLA_HG, (h + 1) * MLA_HG)
        return jnp.dot(q_ref[0, :, lanes], kt_ref[0, lanes, :], preferred_element_type=F32)

    n_rep = kt_ref.shape[2] // MLA_HG
    s_next = logits(0)
    for h in range(MLA_HEADS):
        s = s_next
        if h + 1 < MLA_HEADS:
            s_next = logits(h + 1)
        m_old = m_scr[h]
        m_new = jnp.maximum(m_old, jnp.max(s, axis=-1, keepdims=True))
        p = jnp.exp(s - jnp.tile(m_new, (1, n_rep))).astype(MXU_DTYPE)
        acc_scr[h] = jnp.exp(m_old - m_new) * acc_scr[h] + jnp.dot(
            p, v_ref[0, :, h * MLA_HG:(h + 1) * MLA_HG], preferred_element_type=F32)
        m_scr[h] = m_new

    @pl.when(ki == pl.num_programs(2) - 1)
    def _():
        low = lax.broadcasted_iota(jnp.int32, (acc_scr.shape[1], MLA_HG), 1) < MLA_V
        outs = []
        for h in range(MLA_HEADS):
            acc = acc_scr[h]
            outs.append(acc / acc[:, MLA_V:MLA_V + 1])
        for hp in range(MLA_HEADS // 2):
            odd = pltpu.roll(outs[2 * hp + 1], MLA_V, axis=1)
            out_ref[0, :, hp * MLA_HG:(hp + 1) * MLA_HG] = jnp.where(low, outs[2 * hp], odd)


def mla_rotary_key_columns(w):
    zl = jnp.zeros(w.shape[:-1] + (MLA_NOPE,), w.dtype)
    zr = jnp.zeros(w.shape[:-1] + (MLA_HG - MLA_NOPE - MLA_ROPE,), w.dtype)
    w_sw = jnp.concatenate([w[..., MLA_RHALF:], w[..., :MLA_RHALF]], axis=-1)
    return jnp.concatenate([zl, w, zr, zl, w_sw, zr], axis=-1)


def mla_mixer(arr, q_norm, w_qb, kv_norm, w_kvb, col_q=0, col_kv=MLA_Q_RANK, col_kr=MLA_Q_RANK + MLA_KV_RANK):
    b, s, _ = arr.shape
    tt = MLA_PREP_TOK
    assert col_q % MLA_Q_RANK == 0 and col_kv % MLA_KV_RANK == 0 and col_kr % (2 * MLA_HG) == 0
    wqa, wqb, wk, wv = _mla_pack_weights(w_qb, w_kvb)
    q_cos, q_sin, k_cos, k_sin = _mla_rope_tables(s)
    tok = lambda w, c=0: pl.BlockSpec((1, tt, w), lambda i, j: (i, j, c // w))
    full = lambda r, c: pl.BlockSpec((r, c), lambda i, j: (0, 0))
    tab = pl.BlockSpec((tt, MLA_HG), lambda i, j: (j, 0))
    q, kt, v = pl.pallas_call(
        _mla_prep_kernel,
        grid=(b, s // tt),
        in_specs=[tok(MLA_Q_RANK, col_q), tok(MLA_KV_RANK, col_kv), tok(2 * MLA_HG, col_kr),
                  full(1, MLA_Q_RANK), full(1, MLA_KV_RANK),
                  full(MLA_Q_RANK, MLA_QK_W), full(MLA_Q_RANK, MLA_QK_W),
                  full(MLA_KV_RANK, MLA_QK_W), full(MLA_KV_RANK, MLA_QK_W),
                  tab, tab, tab, tab],
        out_specs=[tok(MLA_QK_W), pl.BlockSpec((1, MLA_QK_W, tt), lambda i, j: (i, 0, j)), tok(MLA_QK_W)],
        out_shape=[jax.ShapeDtypeStruct((b, s, MLA_QK_W), MXU_DTYPE),
                   jax.ShapeDtypeStruct((b, MLA_QK_W, s), MXU_DTYPE),
                   jax.ShapeDtypeStruct((b, s, MLA_QK_W), MXU_DTYPE)],
        compiler_params=pltpu.CompilerParams(
            dimension_semantics=("arbitrary", "arbitrary"), vmem_limit_bytes=VMEM_LIMIT_BYTES),
        name="mla_prep",
    )(arr, arr, arr, q_norm.reshape(1, -1), kv_norm.reshape(1, -1), wqa, wqb, wk, wv,
      q_cos, q_sin, k_cos, k_sin)
    tq, tk = min(MLA_TQ, s), min(MLA_TK, s)
    assert s % tq == 0 and s % tk == 0 and s % tt == 0
    return pl.pallas_call(
        _mla_flash_kernel,
        grid=(b, s // tq, s // tk),
        in_specs=[pl.BlockSpec((1, tq, MLA_QK_W), lambda i, j, kk: (i, j, 0)),
                  pl.BlockSpec((1, MLA_QK_W, tk), lambda i, j, kk: (i, 0, kk)),
                  pl.BlockSpec((1, tk, MLA_QK_W), lambda i, j, kk: (i, kk, 0))],
        out_specs=pl.BlockSpec((1, tq, MLA_V_W), lambda i, j, kk: (i, j, 0)),
        out_shape=jax.ShapeDtypeStruct((b, s, MLA_V_W), F32),
        scratch_shapes=[pltpu.VMEM((MLA_HEADS, tq, MLA_HG), F32),
                        pltpu.VMEM((MLA_HEADS, tq, MLA_HG), F32)],
        compiler_params=pltpu.CompilerParams(
            dimension_semantics=("arbitrary", "arbitrary", "arbitrary"), vmem_limit_bytes=VMEM_LIMIT_BYTES),
        name="mla_flash",
    )(q, kt, v)


def t5_bucket(rel):
    nb = T5_BUCKETS // 2
    ret = np.where(rel > 0, nb, 0)
    n = np.abs(rel)
    max_exact = nb // 2
    large = max_exact + (np.log(np.maximum(n, 1) / max_exact) / np.log(T5_MAX_DIST / max_exact)
                         * (nb - max_exact)).astype(np.int64)
    large = np.minimum(large, nb - 1)
    return (ret + np.where(n < max_exact, n, large)).astype(np.int32)


DIL_DIM = DIL_HEADS * DIL_HD
DIL_HALF = 64
DIL_QB = 128
DIL_KW = DIL_QB + 2 * DIL_HALF
DIL_TL = 512
DIL_N_GROUPS = len(DIL_GROUPS)


def _dil_bias_table(t5_table, gi, dil):
    rel = np.arange(DIL_KW)[None, :] - DIL_HALF - np.arange(DIL_QB)[:, None]
    bias = t5_table[:, gi * DIL_HEADS:(gi + 1) * DIL_HEADS][t5_bucket(rel * dil)].astype(F32)
    bias = jnp.where((np.abs(rel) <= DIL_HALF)[:, :, None], bias, -jnp.inf)
    return bias.transpose(2, 0, 1).reshape(DIL_HEADS * DIL_QB, DIL_KW)


def _dil_kernel(prev_ref, cur_ref, next_ref, tab_ref, o_ref, lse_ref, k_scr, v_scr, *, seq_len):
    step = pl.program_id(2)
    for i, ref in enumerate((prev_ref, cur_ref, next_ref)):
        k_scr[i * DIL_TL:(i + 1) * DIL_TL, :] = ref[0, :, DIL_DIM:2 * DIL_DIM].astype(MXU_DTYPE)
        v_scr[i * DIL_TL:(i + 1) * DIL_TL, :] = ref[0, :, 2 * DIL_DIM:3 * DIL_DIM].astype(MXU_DTYPE)
    lane_head = lax.broadcasted_iota(jnp.int32, (DIL_QB, DIL_DIM), 1) // DIL_HD
    key_off = lax.broadcasted_iota(jnp.int32, (1, DIL_KW), 1)
    for n in range(DIL_TL // DIL_QB):
        w0 = DIL_TL + n * DIL_QB - DIL_HALF
        kpos = step * DIL_TL + (n * DIL_QB - DIL_HALF) + key_off
        valid = (kpos >= 0) & (kpos < seq_len)
        q = cur_ref[0, n * DIL_QB:(n + 1) * DIL_QB, 0:DIL_DIM] * np.float32(DIL_HD ** -0.5)
        qs = jnp.concatenate([jnp.where(lane_head == h, q, 0.0) for h in range(DIL_HEADS)], axis=0)
        logits = lax.dot_general(qs.astype(MXU_DTYPE), k_scr[w0:w0 + DIL_KW, :], (((1,), (1,)), ((), ())),
                                 preferred_element_type=F32) + tab_ref[...]
        logits = jnp.where(valid, logits, -jnp.inf)
        m = jnp.max(logits, axis=-1, keepdims=True)
        p = jnp.exp(logits - m)
        denom = jnp.sum(p, axis=-1, keepdims=True)
        o_all = jnp.dot(p.astype(MXU_DTYPE), v_scr[w0:w0 + DIL_KW, :], preferred_element_type=F32) / denom
        lse_all = m + jnp.log(denom)
        o = jnp.zeros((DIL_QB, DIL_DIM), F32)
        lse = jnp.zeros((DIL_QB, DIL_DIM), F32)
        for h in range(DIL_HEADS):
            rows = slice(h * DIL_QB, (h + 1) * DIL_QB)
            o = o + jnp.where(lane_head == h, o_all[rows, :], 0.0)
            lse = lse + jnp.where(lane_head == h, lse_all[rows, :], 0.0)
        o_ref[0, n * DIL_QB:(n + 1) * DIL_QB, :] = o
        lse_ref[0, n * DIL_QB:(n + 1) * DIL_QB, :] = lse


def _dil_group(qkv, t5_table, gi, dil, col):
    b, s, width = qkv.shape
    seq_len = s // dil
    assert seq_len % DIL_TL == 0 and width % (3 * DIL_DIM) == 0 and col % (3 * DIL_DIM) == 0
    n_steps = seq_len // DIL_TL
    gi_col = col // (3 * DIL_DIM) + gi
    if dil > 1:
        qkv = qkv[:, :, gi_col * 3 * DIL_DIM:(gi_col + 1) * 3 * DIL_DIM]
        width, gi_col = 3 * DIL_DIM, 0
    n_col = width // (3 * DIL_DIM)
    view = qkv.reshape(b, seq_len, dil * width)
    blk = (1, DIL_TL, 3 * DIL_DIM)
    out_shape = jax.ShapeDtypeStruct((b, seq_len, dil * DIL_DIM), F32)
    out_spec = pl.BlockSpec((1, DIL_TL, DIL_DIM), lambda i, r, j: (i, j, r))
    o, lse = pl.pallas_call(
        functools.partial(_dil_kernel, seq_len=seq_len),
        grid=(b, dil, n_steps),
        in_specs=[
            pl.BlockSpec(blk, lambda i, r, j: (i, jnp.maximum(j - 1, 0), r * n_col + gi_col)),
            pl.BlockSpec(blk, lambda i, r, j: (i, j, r * n_col + gi_col)),
            pl.BlockSpec(blk, lambda i, r, j: (i, jnp.minimum(j + 1, n_steps - 1), r * n_col + gi_col)),
            pl.BlockSpec((DIL_HEADS * DIL_QB, DIL_KW), lambda i, r, j: (0, 0)),
        ],
        out_specs=[out_spec, out_spec],
        out_shape=[out_shape, out_shape],
        scratch_shapes=[pltpu.VMEM((3 * DIL_TL, DIL_DIM), MXU_DTYPE),
                        pltpu.VMEM((3 * DIL_TL, DIL_DIM), MXU_DTYPE)],
        compiler_params=pltpu.CompilerParams(
            dimension_semantics=("arbitrary", "arbitrary", "arbitrary"), vmem_limit_bytes=VMEM_LIMIT_BYTES),
        name=f"dilated_attention_g{gi}",
    )(view, view, view, _dil_bias_table(t5_table, gi, dil))
    return o.reshape(b, s, DIL_DIM), lse.reshape(b, s, DIL_DIM)


def _dil_combine_kernel(*refs):
    o_refs, lse_refs, out_ref = refs[:DIL_N_GROUPS], refs[DIL_N_GROUPS:2 * DIL_N_GROUPS], refs[-1]
    lses = [r[...] for r in lse_refs]
    m = functools.reduce(jnp.maximum, lses)
    ws = [jnp.exp(l - m) for l in lses]
    total = functools.reduce(jnp.add, ws)
    acc = functools.reduce(jnp.add, [w * r[...] for w, r in zip(ws, o_refs)])
    out_ref[...] = acc / total


def dilated_mixer(qkv, t5_table, col=0):
    b, s, _ = qkv.shape
    outs, lses = [], []
    for gi, (win, dil) in enumerate(DIL_GROUPS):
        assert win // (2 * dil) == DIL_HALF
        o, lse = _dil_group(qkv, t5_table, gi, dil, col)
        outs.append(o.reshape(b * s, DIL_DIM))
        lses.append(lse.reshape(b * s, DIL_DIM))
    tm = 1024
    spec = pl.BlockSpec((tm, DIL_DIM), lambda i: (i, 0))
    out = pl.pallas_call(
        _dil_combine_kernel,
        grid=(b * s // tm,),
        in_specs=[spec] * (2 * DIL_N_GROUPS),
        out_specs=spec,
        out_shape=jax.ShapeDtypeStruct((b * s, DIL_DIM), F32),
        compiler_params=pltpu.CompilerParams(
            dimension_semantics=("arbitrary",), vmem_limit_bytes=VMEM_LIMIT_BYTES),
        name="dilated_combine",
    )(*outs, *lses)
    return out.reshape(b, s, DIL_DIM)


SSM_CONV_TOK = 512
SSM_HALO = SUBLANES
SSM_HEADS_PER_GROUP = SSM_HEADS // SSM_GROUPS
SSM_GROUP_W = SSM_HEADS_PER_GROUP * SSM_HD
SSM_BC_W = SSM_GROUPS * SSM_STATE


def _softplus(x):
    return jnp.maximum(x, 0.0) + jnp.log1p(jnp.exp(-jnp.abs(x)))


def _ssm_conv_kernel(prev_ref, cur_ref, next_ref, w_ref, b_ref, out_ref, cat_scr):
    j = pl.program_id(1)
    tl = cur_ref.shape[1]
    cat_scr[0:SSM_HALO, :] = jnp.where(j > 0, prev_ref[0], 0.0)
    cat_scr[SSM_HALO:SSM_HALO + tl, :] = cur_ref[0]
    cat_scr[SSM_HALO + tl:, :] = jnp.where(j < pl.num_programs(1) - 1, next_ref[0], 0.0)
    acc = jnp.zeros((tl, CONV_CH), F32) + b_ref[...]
    for k in range(SSM_CONV):
        off = SSM_HALO + k - SSM_CONV // 2
        acc = acc + cat_scr[off:off + tl, :] * w_ref[k:k + 1, :]
    out_ref[0] = acc * jax.nn.sigmoid(acc)


def _ssm_conv(xbc, conv_w, conv_b, col):
    b, l, _ = xbc.shape
    tl = SSM_CONV_TOK
    n_steps = l // tl
    per = tl // SSM_HALO
    assert col % CONV_CH == 0
    cb = col // CONV_CH
    return pl.pallas_call(
        _ssm_conv_kernel,
        grid=(b, n_steps),
        in_specs=[
            pl.BlockSpec((1, SSM_HALO, CONV_CH), lambda i, j: (i, jnp.maximum(j * per - 1, 0), cb)),
            pl.BlockSpec((1, tl, CONV_CH), lambda i, j: (i, j, cb)),
            pl.BlockSpec((1, SSM_HALO, CONV_CH), lambda i, j: (i, jnp.minimum((j + 1) * per, n_steps * per - 1), cb)),
            pl.BlockSpec((SSM_CONV, CONV_CH), lambda i, j: (0, 0)),
            pl.BlockSpec((1, CONV_CH), lambda i, j: (0, 0)),
        ],
        out_specs=pl.BlockSpec((1, tl, CONV_CH), lambda i, j: (i, j, 0)),
        out_shape=jax.ShapeDtypeStruct((b, l, CONV_CH), F32),
        scratch_shapes=[pltpu.VMEM((tl + 2 * SSM_HALO, CONV_CH), F32)],
        compiler_params=pltpu.CompilerParams(
            dimension_semantics=("arbitrary", "arbitrary"), vmem_limit_bytes=VMEM_LIMIT_BYTES),
        name="ssm_conv",
    )(xbc, xbc, xbc, conv_w, conv_b.reshape(1, CONV_CH))


def _ssd_kernel(xbc_ref, dt_ref, dtt_ref, expand_ref, bias_e_ref, a_e_ref, bias_t_ref, a_t_ref,
                y_ref, state_scr, *, reverse):
    q = SSM_CHUNK

    @pl.when(pl.program_id(1) == 0)
    def _():
        state_scr[...] = jnp.zeros(state_scr.shape, F32)

    hi = lax.Precision.HIGHEST
    xbc = xbc_ref[0]
    xs = xbc[:, 0:SSM_INNER]
    dt_e = _softplus(jnp.dot(dt_ref[0], expand_ref[...], precision=hi, preferred_element_type=F32)
                     + bias_e_ref[...])
    a_e = dt_e * a_e_ref[...]
    ri = lax.broadcasted_iota(jnp.int32, (q, q), 0)
    ci = lax.broadcasted_iota(jnp.int32, (q, q), 1)
    seen = (ci >= ri) if reverse else (ci <= ri)
    cs_e = jnp.dot(seen.astype(F32), a_e, precision=hi, preferred_element_type=F32)
    dt_t = _softplus(dtt_ref[0] + bias_t_ref[...])
    cs_t = jnp.dot(dt_t * a_t_ref[...], seen.T.astype(F32), precision=hi, preferred_element_type=F32)
    dtx = xs * dt_e
    last = 0 if reverse else q - 1
    total = cs_e[last:last + 1, :]
    dtx_decayed = jnp.exp(total - cs_e) * dtx
    grow = jnp.exp(cs_e)
    chunk_decay = jnp.exp(total)
    lane_head = lax.broadcasted_iota(jnp.int32, (q, SSM_GROUP_W), 1) // SSM_HD
    for g in range(SSM_GROUPS):
        xl = slice(g * SSM_GROUP_W, (g + 1) * SSM_GROUP_W)
        bg = xbc[:, SSM_INNER + g * SSM_STATE:SSM_INNER + (g + 1) * SSM_STATE]
        cg = xbc[:, SSM_INNER + SSM_BC_W + g * SSM_STATE:SSM_INNER + SSM_BC_W + (g + 1) * SSM_STATE]
        cb = lax.dot_general(cg.astype(MXU_DTYPE), bg.astype(MXU_DTYPE), (((1,), (1,)), ((), ())),
                             preferred_element_type=F32)
        ms = []
        for r in range(SSM_HEADS_PER_GROUP):
            h = g * SSM_HEADS_PER_GROUP + r
            col = cs_e[:, h * SSM_HD:h * SSM_HD + 1]
            row = cs_t[h:h + 1, :]
            ms.append(cb * jnp.where(seen, jnp.exp(col - row), 0.0))
        y_all = jnp.dot(jnp.concatenate(ms, axis=0).astype(MXU_DTYPE), dtx[:, xl].astype(MXU_DTYPE),
                        preferred_element_type=F32)
        y_diag = jnp.zeros((q, SSM_GROUP_W), F32)
        for r in range(SSM_HEADS_PER_GROUP):
            y_diag = y_diag + jnp.where(lane_head == r, y_all[r * q:(r + 1) * q, :], 0.0)
        s_in = state_scr[g]
        y_off = jnp.dot(cg.astype(MXU_DTYPE), s_in.astype(MXU_DTYPE), preferred_element_type=F32) * grow[:, xl]
        y_ref[0, :, xl] = y_diag + y_off
        new = jnp.dot(bg.T.astype(MXU_DTYPE), dtx_decayed[:, xl].astype(MXU_DTYPE), preferred_element_type=F32)
        state_scr[g] = s_in * chunk_decay[:, xl] + new


def _ssd_direction(xbc_act, dt_arr, dt_t, a_log, dt_bias, direction, col_dt, dt_w):
    b, l, _ = xbc_act.shape
    assert col_dt % dt_w == 0 and dt_w >= 2 * SSM_HEADS
    q = SSM_CHUNK
    nc = l // q
    reverse = direction == 1
    a = -jnp.exp(a_log[direction].astype(F32))
    bias = dt_bias[direction].astype(F32)
    head_of_lane = np.arange(SSM_INNER) // SSM_HD
    expand = (np.arange(dt_w)[:, None] == direction * SSM_HEADS + head_of_lane[None, :]).astype(np.float32)
    chunk = (lambda j: nc - 1 - j) if reverse else (lambda j: j)
    full = lambda r, c: pl.BlockSpec((r, c), lambda i, j: (0, 0))
    return pl.pallas_call(
        functools.partial(_ssd_kernel, reverse=reverse),
        grid=(b, nc),
        in_specs=[
            pl.BlockSpec((1, q, CONV_CH), lambda i, j: (i, chunk(j), 0)),
            pl.BlockSpec((1, q, dt_w), lambda i, j: (i, chunk(j), col_dt // dt_w)),
            pl.BlockSpec((1, SSM_HEADS, q), lambda i, j: (i * 2 + direction, 0, chunk(j))),
            full(dt_w, SSM_INNER), full(1, SSM_INNER), full(1, SSM_INNER),
            full(SSM_HEADS, 1), full(SSM_HEADS, 1),
        ],
        out_specs=pl.BlockSpec((1, q, SSM_INNER), lambda i, j: (i, chunk(j), 0)),
        out_shape=jax.ShapeDtypeStruct((b, l, SSM_INNER), F32),
        scratch_shapes=[pltpu.VMEM((SSM_GROUPS, SSM_STATE, SSM_GROUP_W), F32)],
        compiler_params=pltpu.CompilerParams(
            dimension_semantics=("arbitrary", "arbitrary"), vmem_limit_bytes=VMEM_LIMIT_BYTES),
        name=f"ssd_dir{direction}",
    )(xbc_act, dt_arr, dt_t, jnp.asarray(expand), bias[head_of_lane].reshape(1, SSM_INNER),
      a[head_of_lane].reshape(1, SSM_INNER), bias.reshape(SSM_HEADS, 1), a.reshape(SSM_HEADS, 1))


def _ssm_gate_kernel(yf_ref, yb_ref, xbc_ref, z_ref, d_ref, g_ref, out_ref):
    z = z_ref[...]
    y = (yf_ref[...] + yb_ref[...] + xbc_ref[...] * d_ref[...]) * (z * jax.nn.sigmoid(z))
    out_ref[...] = y * lax.rsqrt(jnp.mean(y * y, axis=-1, keepdims=True) + EPS) * g_ref[...]


def mamba2_mixer(arr, conv_w, conv_b, A_log, dt_bias, D_skip, norm_g,
                 col_z=0, col_xbc=CONV_CH, col_dt=SSM_INNER + CONV_CH, dt_w=2 * SSM_HEADS):
    b, l, width = arr.shape
    assert col_z % SSM_INNER == 0
    xbc_act = _ssm_conv(arr, conv_w, conv_b, col_xbc)
    dt_t = jnp.swapaxes(arr[:, :, col_dt:col_dt + 2 * SSM_HEADS], 1, 2).reshape(b * 2, SSM_HEADS, l)
    y_f = _ssd_direction(xbc_act, arr, dt_t, A_log, dt_bias, 0, col_dt, dt_w)
    y_b = _ssd_direction(xbc_act, arr, dt_t, A_log, dt_bias, 1, col_dt, dt_w)
    tm = 1024
    tok = lambda cb: pl.BlockSpec((tm, SSM_INNER), lambda i: (i, cb))
    row = pl.BlockSpec((1, SSM_INNER), lambda i: (0, 0))
    d_e = D_skip.astype(F32)[np.arange(SSM_INNER) // SSM_HD].reshape(1, SSM_INNER)
    out = pl.pallas_call(
        _ssm_gate_kernel,
        grid=(b * l // tm,),
        in_specs=[tok(0), tok(0), tok(0), tok(col_z // SSM_INNER), row, row],
        out_specs=tok(0),
        out_shape=jax.ShapeDtypeStruct((b * l, SSM_INNER), F32),
        compiler_params=pltpu.CompilerParams(
            dimension_semantics=("arbitrary",), vmem_limit_bytes=VMEM_LIMIT_BYTES),
        name="ssm_gate",
    )(y_f.reshape(b * l, SSM_INNER), y_b.reshape(b * l, SSM_INNER), xbc_act.reshape(b * l, CONV_CH),
      arr.reshape(b * l, width), d_e, norm_g.reshape(1, SSM_INNER))
    return out.reshape(b, l, SSM_INNER)


NA_DIM = NA_HEADS * NA_HD
NA_ROWS_PER_STEP = 8
NA_WIN = NA_ROWS * GRID_W
NA_STEP_TOK = NA_ROWS_PER_STEP * GRID_W


def _na_bias_table(rpb):
    n_dc = 2 * NA_COLS - 1
    edge_l = jnp.repeat(rpb[:, :, :1], GRID_W, axis=2)
    edge_r = jnp.repeat(rpb[:, :, -1:], GRID_W, axis=2)
    ext = jnp.concatenate([edge_l, rpb.astype(F32), edge_r], axis=2)
    by_col = jnp.stack([ext[:, :, GRID_W + NA_COLS - 1 - qc:2 * GRID_W + NA_COLS - 1 - qc]
                        for qc in range(GRID_W)], axis=2)
    qc = np.arange(GRID_W)[:, None]
    kc = np.arange(GRID_W)[None, :]
    cs = np.clip(qc - NA_COLS // 2, 0, GRID_W - NA_COLS)
    ok = (kc >= cs) & (kc < cs + NA_COLS)
    by_col = jnp.where(ok[None, None], by_col, -jnp.inf)
    tabs = []
    for delta in range(NA_ROWS):
        rows = by_col[:, NA_ROWS - 1 - delta:2 * NA_ROWS - 1 - delta]
        tabs.append(rows.transpose(0, 2, 1, 3).reshape(NA_HEADS * GRID_W, NA_WIN))
    assert n_dc == rpb.shape[2]
    return jnp.stack(tabs, axis=0)


def _na_kernel(prev_ref, cur_ref, next_ref, tab_ref, out_ref, k_scr, v_scr, *, n_rows):
    step = pl.program_id(1)
    for i, ref in enumerate((prev_ref, cur_ref, next_ref)):
        k_scr[i * NA_STEP_TOK:(i + 1) * NA_STEP_TOK, :] = ref[0, :, NA_DIM:2 * NA_DIM].astype(MXU_DTYPE)
        v_scr[i * NA_STEP_TOK:(i + 1) * NA_STEP_TOK, :] = ref[0, :, 2 * NA_DIM:3 * NA_DIM].astype(MXU_DTYPE)
    lane_head = lax.broadcasted_iota(jnp.int32, (GRID_W, NA_DIM), 1) // NA_HD
    row0 = step * NA_ROWS_PER_STEP
    for j in range(NA_ROWS_PER_STEP):
        r = row0 + j
        r0 = jnp.clip(r - NA_ROWS // 2, 0, n_rows - NA_ROWS)
        start = pl.multiple_of((r0 - row0 + NA_ROWS_PER_STEP) * GRID_W, GRID_W)
        q = cur_ref[0, j * GRID_W:(j + 1) * GRID_W, 0:NA_DIM] * np.float32(NA_HD ** -0.5)
        qs = jnp.concatenate([jnp.where(lane_head == h, q, 0.0) for h in range(NA_HEADS)], axis=0)
        kw = k_scr[pl.ds(start, NA_WIN), :]
        vw = v_scr[pl.ds(start, NA_WIN), :]
        logits = lax.dot_general(qs.astype(MXU_DTYPE), kw, (((1,), (1,)), ((), ())),
                                 preferred_element_type=F32) + tab_ref[r - r0]
        m = jnp.max(logits, axis=-1, keepdims=True)
        p = jnp.exp(logits - m)
        denom = jnp.sum(p, axis=-1, keepdims=True)
        o_all = jnp.dot(p.astype(MXU_DTYPE), vw, preferred_element_type=F32) / denom
        o = jnp.zeros((GRID_W, NA_DIM), F32)
        for h in range(NA_HEADS):
            o = o + jnp.where(lane_head == h, o_all[h * GRID_W:(h + 1) * GRID_W, :], 0.0)
        out_ref[0, j * GRID_W:(j + 1) * GRID_W, :] = o


def na_mixer(qkv, rpb, col=0):
    b, s, _ = qkv.shape
    n_rows = s // GRID_W
    assert n_rows >= NA_ROWS and n_rows % NA_ROWS_PER_STEP == 0 and col % (3 * NA_DIM) == 0
    n_steps = n_rows // NA_ROWS_PER_STEP
    blk = (1, NA_STEP_TOK, 3 * NA_DIM)
    cb = col // (3 * NA_DIM)
    return pl.pallas_call(
        functools.partial(_na_kernel, n_rows=n_rows),
        grid=(b, n_steps),
        in_specs=[
            pl.BlockSpec(blk, lambda i, j: (i, jnp.maximum(j - 1, 0), cb)),
            pl.BlockSpec(blk, lambda i, j: (i, j, cb)),
            pl.BlockSpec(blk, lambda i, j: (i, jnp.minimum(j + 1, n_steps - 1), cb)),
            pl.BlockSpec((NA_ROWS, NA_HEADS * GRID_W, NA_WIN), lambda i, j: (0, 0, 0)),
        ],
        out_specs=pl.BlockSpec((1, NA_STEP_TOK, NA_DIM), lambda i, j: (i, j, 0)),
        out_shape=jax.ShapeDtypeStruct((b, s, NA_DIM), F32),
        scratch_shapes=[
            pltpu.VMEM((3 * NA_STEP_TOK, NA_DIM), MXU_DTYPE),
            pltpu.VMEM((3 * NA_STEP_TOK, NA_DIM), MXU_DTYPE),
        ],
        compiler_params=pltpu.CompilerParams(
            dimension_semantics=("arbitrary", "arbitrary"), vmem_limit_bytes=VMEM_LIMIT_BYTES),
        name="na_attention",
    )(qkv, qkv, qkv, _na_bias_table(rpb))


PK_GATE = 0
PK_XBC = PK_GATE + N_BRANCH * D_MODEL
PK_CQ = PK_XBC + CONV_CH
PK_DIL = PK_CQ + MLA_Q_RANK
PK_NA = PK_DIL + DIL_N_GROUPS * 3 * DIL_DIM
PK_KR = PK_NA + 3 * NA_DIM
PK_Z = PK_KR + 2 * MLA_HG
PK_CKV = PK_Z + SSM_INNER
PK_DT = PK_CKV + MLA_KV_RANK
PK_DT_W = LANES
PK_WIDTH = 13 * 3 * DIL_DIM
INPROJ_TM = 512
INPROJ_TN = PK_WIDTH // 2
MERGE_TM = 512


def _pack_w_in(w_in_l):
    gate, a_cq, a_ckv, a_kr, b_qkv, c_z, c_xbc, c_dt, d_qkv = jnp.split(w_in_l, IN_SPLITS, axis=-1)
    zeros = lambda n: jnp.zeros((D_MODEL, n), w_in_l.dtype)
    cols = [gate, c_xbc, a_cq, b_qkv, d_qkv, mla_rotary_key_columns(a_kr), c_z, a_ckv,
            c_dt, zeros(PK_DT_W - 2 * SSM_HEADS)]
    packed = jnp.concatenate(cols, axis=-1)
    assert packed.shape[1] == PK_DT + PK_DT_W
    return jnp.concatenate([packed, zeros(PK_WIDTH - packed.shape[1])], axis=-1).astype(MXU_DTYPE)


def _inproj_kernel(x_ref, g_ref, w_ref, out_ref, h_scr):
    @pl.when(pl.program_id(1) == 0)
    def _():
        x = x_ref[...]
        h_scr[...] = (x * lax.rsqrt(jnp.mean(x * x, axis=-1, keepdims=True) + EPS) * g_ref[...]).astype(MXU_DTYPE)

    out_ref[...] = jnp.dot(h_scr[...], w_ref[...], preferred_element_type=F32)


def _inproj(x2d, norm_g, w_packed):
    n_tok = x2d.shape[0]
    tm, tn = INPROJ_TM, INPROJ_TN
    return pl.pallas_call(
        _inproj_kernel,
        grid=(n_tok // tm, PK_WIDTH // tn),
        in_specs=[pl.BlockSpec((tm, D_MODEL), lambda i, j: (i, 0)),
                  pl.BlockSpec((1, D_MODEL), lambda i, j: (0, 0)),
                  pl.BlockSpec((D_MODEL, tn), lambda i, j: (0, j))],
        out_specs=pl.BlockSpec((tm, tn), lambda i, j: (i, j)),
        out_shape=jax.ShapeDtypeStruct((n_tok, PK_WIDTH), F32),
        scratch_shapes=[pltpu.VMEM((tm, D_MODEL), MXU_DTYPE)],
        compiler_params=pltpu.CompilerParams(
            dimension_semantics=("arbitrary", "arbitrary"), vmem_limit_bytes=VMEM_LIMIT_BYTES),
        name="in_projection",
    )(x2d, norm_g.reshape(1, D_MODEL), w_packed)


def _merge_kernel(x_ref, gate_ref, bg_ref, ya_ref, yb_ref, yc_ref, yd_ref, wb_ref, wo_ref, out_ref):
    merged = jnp.zeros(x_ref.shape, F32)
    for i, y_ref in enumerate((ya_ref, yb_ref, yc_ref, yd_ref)):
        proj = jnp.dot(y_ref[...].astype(MXU_DTYPE), wb_ref[BRANCH_ROWS[i]:BRANCH_ROWS[i + 1], :],
                       preferred_element_type=F32)
        lanes = slice(i * D_MODEL, (i + 1) * D_MODEL)
        merged = merged + jax.nn.sigmoid(gate_ref[:, lanes] + bg_ref[:, lanes]) * proj
    out_ref[...] = x_ref[...] + jnp.dot(merged.astype(MXU_DTYPE), wo_ref[...], preferred_element_type=F32)


def _merge(x2d, packed, b_gate, ys, w_branch, w_out):
    n_tok = x2d.shape[0]
    tm = MERGE_TM
    tok = lambda w: pl.BlockSpec((tm, w), lambda i: (i, 0))
    full = lambda r, c: pl.BlockSpec((r, c), lambda i: (0, 0))
    n_gate = N_BRANCH * D_MODEL
    return pl.pallas_call(
        _merge_kernel,
        grid=(n_tok // tm,),
        in_specs=[tok(D_MODEL), tok(n_gate), full(1, n_gate)] + [tok(w) for w in BRANCH_WIDTHS]
                 + [full(BRANCH_ROWS[-1], D_MODEL), full(D_MODEL, D_MODEL)],
        out_specs=tok(D_MODEL),
        out_shape=jax.ShapeDtypeStruct((n_tok, D_MODEL), F32),
        compiler_params=pltpu.CompilerParams(
            dimension_semantics=("arbitrary",), vmem_limit_bytes=VMEM_LIMIT_BYTES),
        name="branch_merge",
    )(x2d, packed, b_gate.reshape(1, n_gate), *[y.reshape(n_tok, -1) for y in ys], w_branch, w_out)


def encoder(x, norm1_g, w_in_packed, b_gate, mla_q_norm, mla_w_qb, mla_kv_norm, mla_w_kvb, t5_table,
            ssm_conv_w, ssm_conv_b, ssm_A_log, ssm_dt_bias, ssm_D, ssm_norm_g, na_rpb,
            w_branch, w_out, norm2_g, peer_wq, peer_keys, peer_u, peer_vt, final_g):
    b, s, _ = x.shape
    x2d = x.reshape(b * s, D_MODEL)
    for l in range(DEPTH):
        packed = _inproj(x2d, norm1_g[l], w_in_packed[l])
        p3 = packed.reshape(b, s, PK_WIDTH)
        y_a = mla_mixer(p3, mla_q_norm[l], mla_w_qb[l], mla_kv_norm[l], mla_w_kvb[l],
                        col_q=PK_CQ, col_kv=PK_CKV, col_kr=PK_KR)
        y_b = dilated_mixer(p3, t5_table, col=PK_DIL)
        y_c = mamba2_mixer(p3, ssm_conv_w[l], ssm_conv_b[l], ssm_A_log[l], ssm_dt_bias[l], ssm_D[l],
                           ssm_norm_g[l], col_z=PK_Z, col_xbc=PK_XBC, col_dt=PK_DT, dt_w=PK_DT_W)
        y_d = na_mixer(p3, na_rpb[l], col=PK_NA)
        x2d = _merge(x2d, packed, b_gate[l], (y_a, y_b, y_c, y_d), w_branch[l], w_out[l])
        x2d = peer_block(x2d, norm2_g[l], peer_wq[l], peer_keys[l], peer_u[l], peer_vt[l],
                         final_g, final_norm=(l == DEPTH - 1))
    return x2d.reshape(b, s, D_MODEL)


def kernel(x_prompt, x_sample, norm1_g, w_in, b_gate, mla_q_norm, mla_w_qb, mla_kv_norm, mla_w_kvb, t5_table, ssm_conv_w, ssm_conv_b, ssm_A_log, ssm_dt_bias, ssm_D, ssm_norm_g, na_rpb, w_branch, w_out, norm2_g, peer_wq, peer_keys, peer_u, peer_v, final_g):
    peer_u16 = peer_u.astype(MXU_DTYPE)
    peer_vt16 = jnp.swapaxes(peer_v, 1, 2).astype(MXU_DTYPE)
    w_in_packed = jnp.stack([_pack_w_in(w_in[l]) for l in range(DEPTH)])
    shared = (norm1_g, w_in_packed, b_gate, mla_q_norm, mla_w_qb, mla_kv_norm, mla_w_kvb, t5_table,
              ssm_conv_w, ssm_conv_b, ssm_A_log, ssm_dt_bias, ssm_D, ssm_norm_g, na_rpb,
              w_branch.astype(MXU_DTYPE), w_out.astype(MXU_DTYPE), norm2_g, peer_wq, peer_keys,
              peer_u16, peer_vt16, final_g)
    y_prompt = encoder(x_prompt, *shared)
    y_sample = encoder(x_sample, *shared)
    return (y_prompt, y_sample)
```

```python
import functools
import math

import numpy as np
import jax
import jax.numpy as jnp
from jax import lax
from jax.experimental import pallas as pl
from jax.experimental.pallas import tpu as pltpu

F32 = jnp.float32
BF16 = jnp.bfloat16
MXU_DTYPE = BF16

D_MODEL = 1024
DEPTH = 2
GRID_W = 64
EPS = 1e-6
N_BRANCH = 4

MLA_HEADS = 4
MLA_Q_RANK = 256
MLA_KV_RANK = 128
MLA_NOPE = 64
MLA_ROPE = 32
MLA_V = 64
ROPE_THETA = 10000.0
Q_BLOCK = 128

DIL_GROUPS = ((128, 1), (512, 4), (2048, 16))
DIL_HEADS = 4
DIL_HD = 64
T5_BUCKETS = 32
T5_MAX_DIST = 1024

SSM_HEADS = 8
SSM_HD = 64
SSM_INNER = SSM_HEADS * SSM_HD
SSM_GROUPS = 2
SSM_STATE = 128
SSM_CONV = 7
SSM_CHUNK = 128
CONV_CH = SSM_INNER + 2 * SSM_GROUPS * SSM_STATE

NA_HEADS = 4
NA_HD = 64
NA_ROWS = 8
NA_COLS = 16
NA_QCB = 16
NA_KCB = NA_QCB + NA_COLS

PEER_HEADS = 8
PEER_KEYS = 128
PEER_EXPERTS = PEER_KEYS * PEER_KEYS
PEER_QDIM = 256
PEER_TOPK = 16
PEER_TOK_BLOCK = 128

BRANCH_WIDTHS = (MLA_HEADS * MLA_V, DIL_HEADS * DIL_HD, SSM_INNER, NA_HEADS * NA_HD)
BRANCH_ROWS = tuple(sum(BRANCH_WIDTHS[:i]) for i in range(N_BRANCH + 1))
IN_SIZES = (N_BRANCH * D_MODEL, MLA_Q_RANK, MLA_KV_RANK, MLA_ROPE,
            len(DIL_GROUPS) * 3 * DIL_HEADS * DIL_HD,
            SSM_INNER, CONV_CH, 2 * SSM_HEADS,
            3 * NA_HEADS * NA_HD)
IN_SPLITS = tuple(sum(IN_SIZES[:i + 1]) for i in range(len(IN_SIZES) - 1))

VMEM_LIMIT_BYTES = 56 * 1024 * 1024
LANES = 128
SUBLANES = 8


def rms_norm(x, g):
    x32 = x.astype(F32)
    y = x32 * lax.rsqrt(jnp.mean(x32 * x32, axis=-1, keepdims=True) + EPS)
    return (y * g.astype(F32)).astype(x.dtype)


PEER_ROUTE_TB = 256
PEER_TB = 512
PEER_EC = 1024
PEER_GATE_DTYPE = BF16
PEER_GATE_ROWS = 16
PEER_HALF = PEER_QDIM // 2
PEER_CAND_ROWS = 2 * SUBLANES + 7 * SUBLANES + SUBLANES


def _gelu_exact(x):
    return 0.5 * x * (1.0 + lax.erf(x * np.float32(math.sqrt(0.5))))


def _extract_desc(vals, n_out, out_ref, row0, with_rank=False):
    rank = jnp.full(vals.shape, float(n_out), F32) if with_rank else None
    for k in range(n_out):
        m = jnp.max(vals, axis=0, keepdims=True)
        out_ref[pl.ds(row0 + k, 1), :] = m
        hit = vals == m
        if with_rank:
            rank = jnp.where(hit, float(k), rank)
        vals = jnp.where(hit, -jnp.inf, vals)
    return rank


def _peer_route_kernel(x_ref, g_ref, wqt_ref, keys_ref,
                       xnt_ref, cnt_ref, rank_ref, e1_ref, e2_ref,
                       qt_scr, top_scr, cand_scr, tops_scr):
    x = x_ref[...]
    xn = x * lax.rsqrt(jnp.mean(x * x, axis=-1, keepdims=True) + EPS) * g_ref[...]
    xnt = xn.T.astype(MXU_DTYPE)
    xnt_ref[...] = xnt
    qt_scr[...] = jnp.dot(wqt_ref[...], xnt, preferred_element_type=F32).astype(MXU_DTYPE)

    def head(h, carry):
        q1 = qt_scr[pl.ds(pl.multiple_of(h * PEER_QDIM, PEER_QDIM), PEER_HALF), :]
        q2 = qt_scr[pl.ds(pl.multiple_of(h * PEER_QDIM + PEER_HALF, PEER_HALF), PEER_HALF), :]
        s1 = jnp.dot(keys_ref[2 * h], q1, preferred_element_type=F32)
        s2 = jnp.dot(keys_ref[2 * h + 1], q2, preferred_element_type=F32)
        _extract_desc(s1, PEER_TOPK, top_scr, 0)
        rank2 = _extract_desc(s2, PEER_TOPK, top_scr, PEER_TOPK, with_rank=True)
        t1 = top_scr[0:PEER_TOPK, :]
        t2 = top_scr[PEER_TOPK:2 * PEER_TOPK, :]
        cand_scr[0:2 * SUBLANES, :] = t1[0:1, :] + t2
        for a in range(1, SUBLANES):
            cand_scr[(a + 1) * SUBLANES:(a + 2) * SUBLANES, :] = t1[a:a + 1, :] + t2[0:SUBLANES, :]
        cand_scr[9 * SUBLANES:10 * SUBLANES, :] = t1[SUBLANES:2 * SUBLANES, :] + t2[0:1, :]
        _extract_desc(cand_scr[...], PEER_TOPK, tops_scr, 0)
        top_s = tops_scr[...]
        z = jnp.sum(jnp.exp(top_s - top_s[0:1, :]), axis=0, keepdims=True)
        tau = top_s[PEER_TOPK - 1:PEER_TOPK, :]
        cnt_top = jnp.zeros(t1.shape, F32)
        for b in range(PEER_TOPK):
            cnt_top = cnt_top + jnp.where(t1 + t2[b:b + 1, :] >= tau, 1.0, 0.0)
        cnt = jnp.zeros(s1.shape, F32)
        for a in range(PEER_TOPK):
            cnt = jnp.where(s1 == t1[a:a + 1, :], cnt_top[a:a + 1, :], cnt)
        cnt_ref[h] = cnt
        rank_ref[h] = rank2.astype(rank_ref.dtype)
        e1_ref[h] = jnp.exp(s1 - t1[0:1, :]) / z
        e2_ref[h] = jnp.exp(s2 - t2[0:1, :]).astype(e2_ref.dtype)
        return carry

    lax.fori_loop(0, PEER_HEADS, head, 0)


def _peer_route(x2d, g, wqt, keys):
    n_tok = x2d.shape[0]
    tb = PEER_ROUTE_TB
    rt_shape = jax.ShapeDtypeStruct((PEER_HEADS, PEER_KEYS, n_tok), F32)
    gate_shape = jax.ShapeDtypeStruct((PEER_HEADS, PEER_KEYS, n_tok), PEER_GATE_DTYPE)
    rt_spec = pl.BlockSpec((PEER_HEADS, PEER_KEYS, tb), lambda i: (0, 0, i))
    return pl.pallas_call(
        _peer_route_kernel,
        grid=(n_tok // tb,),
        in_specs=[
            pl.BlockSpec((tb, D_MODEL), lambda i: (i, 0)),
            pl.BlockSpec((1, D_MODEL), lambda i: (0, 0)),
            pl.BlockSpec((PEER_HEADS * PEER_QDIM, D_MODEL), lambda i: (0, 0)),
            pl.BlockSpec((2 * PEER_HEADS, PEER_KEYS, PEER_HALF), lambda i: (0, 0, 0)),
        ],
        out_specs=[
            pl.BlockSpec((D_MODEL, tb), lambda i: (0, i)),
            rt_spec, rt_spec, rt_spec, rt_spec,
        ],
        out_shape=[
            jax.ShapeDtypeStruct((D_MODEL, n_tok), MXU_DTYPE),
            rt_shape, gate_shape, rt_shape, gate_shape,
        ],
        scratch_shapes=[
            pltpu.VMEM((PEER_HEADS * PEER_QDIM, tb), MXU_DTYPE),
            pltpu.VMEM((2 * PEER_TOPK, tb), F32),
            pltpu.VMEM((PEER_CAND_ROWS, tb), F32),
            pltpu.VMEM((PEER_TOPK, tb), F32),
        ],
        compiler_params=pltpu.CompilerParams(
            dimension_semantics=("arbitrary",), vmem_limit_bytes=VMEM_LIMIT_BYTES),
        name="peer_route",
    )(x2d, g, wqt, keys)


def _peer_expert_kernel(x_ref, xnt_ref, cnt_ref, rank_ref, e1_ref, e2_ref, u_ref, vt_ref, fg_ref,
                        out_ref, acc_scr, w_scr, *, final_norm):
    c = pl.program_id(1)
    n_chunks = pl.num_programs(1) - 1
    n_i1 = PEER_EC // PEER_KEYS
    tb = acc_scr.shape[1]
    gdt = PEER_GATE_DTYPE

    @pl.when(c == 0)
    def _():
        acc_scr[...] = jnp.zeros_like(acc_scr)
        w_scr[1] = jnp.zeros(w_scr.shape[1:], w_scr.dtype)

    slot = c % 2
    n_groups = PEER_KEYS // PEER_GATE_ROWS
    acc_scr[...] += jnp.dot(vt_ref[...], w_scr[(c + 1) % 2], preferred_element_type=F32)
    hid = jnp.dot(u_ref[...], xnt_ref[...], preferred_element_type=F32)

    def gate_block(i1l, l0):
        lanes = slice(l0, l0 + LANES)
        gates = [jnp.zeros((PEER_GATE_ROWS, LANES), gdt) for _ in range(n_groups)]
        for h in range(PEER_HEADS):
            cntb = jnp.broadcast_to(cnt_ref[h, i1l:i1l + 1, lanes], (PEER_GATE_ROWS, LANES)).astype(gdt)
            e1b = jnp.broadcast_to(e1_ref[h, i1l:i1l + 1, lanes], (PEER_GATE_ROWS, LANES)).astype(gdt)
            for k in range(n_groups):
                rows = slice(k * PEER_GATE_ROWS, (k + 1) * PEER_GATE_ROWS)
                w = e2_ref[h, rows, lanes] * e1b
                gates[k] = gates[k] + jnp.where(rank_ref[h, rows, lanes] < cntb, w, jnp.zeros_like(w))
        for k in range(n_groups):
            r0 = k * PEER_GATE_ROWS
            e0 = i1l * PEER_KEYS + r0
            act = _gelu_exact(hid[e0:e0 + PEER_GATE_ROWS, lanes]).astype(gdt)
            w_scr[slot, e0:e0 + PEER_GATE_ROWS, lanes] = (act * gates[k]).astype(MXU_DTYPE)

    for i1l in range(n_i1):
        for l0 in range(0, tb, LANES):
            gate_block(i1l, l0)

    @pl.when(c == n_chunks)
    def _():
        y = x_ref[...] + acc_scr[...].T
        if final_norm:
            y = y * lax.rsqrt(jnp.mean(y * y, axis=-1, keepdims=True) + EPS) * fg_ref[...]
        out_ref[...] = y


def _peer_experts(x2d, xnt, cnt, rank2, e1, e2, u, vt, final_g, final_norm):
    n_tok = x2d.shape[0]
    tb, ec = PEER_TB, PEER_EC
    n_chunks = PEER_EXPERTS // ec
    rt_spec = pl.BlockSpec((PEER_HEADS, PEER_KEYS, tb), lambda j, c: (0, 0, j))
    row_spec = pl.BlockSpec((PEER_HEADS, ec // PEER_KEYS, tb),
                            lambda j, c: (0, jnp.minimum(c, n_chunks - 1), j))
    return pl.pallas_call(
        functools.partial(_peer_expert_kernel, final_norm=final_norm),
        grid=(n_tok // tb, n_chunks + 1),
        in_specs=[
            pl.BlockSpec((tb, D_MODEL), lambda j, c: (j, 0)),
            pl.BlockSpec((D_MODEL, tb), lambda j, c: (0, j)),
            row_spec, rt_spec, row_spec, rt_spec,
            pl.BlockSpec((ec, D_MODEL), lambda j, c: (jnp.minimum(c, n_chunks - 1), 0)),
            pl.BlockSpec((D_MODEL, ec), lambda j, c: (0, jnp.maximum(c - 1, 0))),
            pl.BlockSpec((1, D_MODEL), lambda j, c: (0, 0)),
        ],
        out_specs=pl.BlockSpec((tb, D_MODEL), lambda j, c: (j, 0)),
        out_shape=jax.ShapeDtypeStruct((n_tok, D_MODEL), F32),
        scratch_shapes=[
            pltpu.VMEM((D_MODEL, tb), F32),
            pltpu.VMEM((2, ec, tb), MXU_DTYPE),
        ],
        compiler_params=pltpu.CompilerParams(
            dimension_semantics=("arbitrary", "arbitrary"), vmem_limit_bytes=VMEM_LIMIT_BYTES),
        name="peer_experts",
    )(x2d, xnt, cnt, rank2, e1, e2, u, vt, final_g.reshape(1, D_MODEL))


def peer_block(x2d, norm_g, w_q, keys, u, v, final_g, final_norm=False):
    wqt = w_q.T.astype(MXU_DTYPE)
    keys2 = keys.reshape(2 * PEER_HEADS, PEER_KEYS, PEER_HALF).astype(MXU_DTYPE)
    xnt, cnt, rank2, e1, e2 = _peer_route(x2d, norm_g.reshape(1, D_MODEL), wqt, keys2)
    return _peer_experts(x2d, xnt, cnt, rank2, e1, e2, u, v, final_g, final_norm)


MLA_HG = LANES
MLA_QK_W = MLA_HEADS * MLA_HG
MLA_V_W = MLA_HEADS * MLA_V
MLA_PREP_TOK = 512
MLA_TQ = 512
MLA_TK = 2048
MLA_RHALF = MLA_ROPE // 2


def _mla_rope_tables(s):
    inv = ROPE_THETA ** (-jnp.arange(MLA_RHALF, dtype=F32) / MLA_RHALF)
    ang = jnp.arange(s).astype(F32)[:, None] * inv[None, :]
    cos, sin = jnp.cos(ang), jnp.sin(ang)
    zero_pad = jnp.zeros((s, MLA_HG - MLA_NOPE - MLA_ROPE), F32)
    cos_rot = jnp.concatenate([cos, cos, zero_pad], axis=1)
    sin_rot = jnp.concatenate([-sin, sin, zero_pad], axis=1)
    scale = np.float32((MLA_NOPE + MLA_ROPE) ** -0.5)
    q_cos = scale * jnp.concatenate([jnp.ones((s, MLA_NOPE), F32), cos_rot], axis=1)
    q_sin = scale * jnp.concatenate([jnp.zeros((s, MLA_NOPE), F32), sin_rot], axis=1)
    k_cos = jnp.concatenate([jnp.zeros((s, MLA_NOPE), F32), cos_rot], axis=1)
    k_sin = jnp.concatenate([jnp.zeros((s, MLA_NOPE), F32), sin_rot], axis=1)
    return q_cos, q_sin, k_cos, k_sin


def _mla_pack_weights(w_qb, w_kvb):
    hd_q = MLA_NOPE + MLA_ROPE
    wq = w_qb.reshape(MLA_Q_RANK, MLA_HEADS, hd_q)
    rot = wq[:, :, MLA_NOPE:]
    rot_sw = jnp.concatenate([rot[:, :, MLA_RHALF:], rot[:, :, :MLA_RHALF]], axis=2)
    pad = jnp.zeros((MLA_Q_RANK, MLA_HEADS, MLA_HG - hd_q), F32)
    wq_a = jnp.concatenate([wq, pad], axis=2).reshape(MLA_Q_RANK, MLA_QK_W)
    wq_b = jnp.concatenate([jnp.zeros_like(wq[:, :, :MLA_NOPE]), rot_sw, pad], axis=2).reshape(MLA_Q_RANK, MLA_QK_W)
    wkv = w_kvb.reshape(MLA_KV_RANK, MLA_HEADS, MLA_NOPE + MLA_V)
    wk = jnp.concatenate([wkv[:, :, :MLA_NOPE], jnp.zeros((MLA_KV_RANK, MLA_HEADS, MLA_HG - MLA_NOPE), F32)],
                         axis=2).reshape(MLA_KV_RANK, MLA_QK_W)
    wv = jnp.concatenate([wkv[:, :, MLA_NOPE:], jnp.zeros((MLA_KV_RANK, MLA_HEADS, MLA_HG - MLA_V), F32)],
                         axis=2).reshape(MLA_KV_RANK, MLA_QK_W)
    return (wq_a.astype(MXU_DTYPE), wq_b.astype(MXU_DTYPE), wk.astype(MXU_DTYPE), wv.astype(MXU_DTYPE))


def _mla_prep_kernel(cq_ref, ckv_ref, kr_ref, qn_ref, kvn_ref, wqa_ref, wqb_ref, wk_ref, wv_ref,
                     qcos_ref, qsin_ref, kcos_ref, ksin_ref, q_out, kt_out, v_out):
    cq = cq_ref[0]
    cqn = (cq * lax.rsqrt(jnp.mean(cq * cq, axis=-1, keepdims=True) + EPS) * qn_ref[...]).astype(MXU_DTYPE)
    qa = jnp.dot(cqn, wqa_ref[...], preferred_element_type=F32)
    qb = jnp.dot(cqn, wqb_ref[...], preferred_element_type=F32)
    ckv = ckv_ref[0]
    ckvn = (ckv * lax.rsqrt(jnp.mean(ckv * ckv, axis=-1, keepdims=True) + EPS) * kvn_ref[...]).astype(MXU_DTYPE)
    ka = jnp.dot(ckvn, wk_ref[...], preferred_element_type=F32)
    one_lane = (lax.broadcasted_iota(jnp.int32, (1, MLA_QK_W), 1) % MLA_HG == MLA_V).astype(F32)
    v_out[0] = (jnp.dot(ckvn, wv_ref[...], preferred_element_type=F32) + one_lane).astype(MXU_DTYPE)
    kr = kr_ref[0]
    k_rot = kr[:, 0:MLA_HG] * kcos_ref[...] + kr[:, MLA_HG:2 * MLA_HG] * ksin_ref[...]
    for h in range(MLA_HEADS):
        lanes = slice(h * MLA_HG, (h + 1) * MLA_HG)
        q_out[0, :, lanes] = (qa[:, lanes] * qcos_ref[...] + qb[:, lanes] * qsin_ref[...]).astype(MXU_DTYPE)
        kt_out[0, lanes, :] = (ka[:, lanes] + k_rot).T.astype(MXU_DTYPE)


def _mla_flash_kernel(q_ref, kt_ref, v_ref, out_ref, m_scr, acc_scr):
    ki = pl.program_id(2)

    @pl.when(ki == 0)
    def _():
        m_scr[...] = jnp.full(m_scr.shape, -jnp.inf, F32)
        acc_scr[...] = jnp.zeros(acc_scr.shape, F32)

    def logits(h):
        lanes = slice(h * MLA_HG, (h + 1) * MLA_HG)
        return jnp.dot(q_ref[0, :, lanes], kt_ref[0, lanes, :], preferred_element_type=F32)

    n_rep = kt_ref.shape[2] // MLA_HG
    s_next = logits(0)
    for h in range(MLA_HEADS):
        s = s_next
        if h + 1 < MLA_HEADS:
            s_next = logits(h + 1)
        m_old = m_scr[h]
        m_new = jnp.maximum(m_old, jnp.max(s, axis=-1, keepdims=True))
        p = jnp.exp(s - jnp.tile(m_new, (1, n_rep))).astype(MXU_DTYPE)
        acc_scr[h] = jnp.exp(m_old - m_new) * acc_scr[h] + jnp.dot(
            p, v_ref[0, :, h * MLA_HG:(h + 1) * MLA_HG], preferred_element_type=F32)
        m_scr[h] = m_new

    @pl.when(ki == pl.num_programs(2) - 1)
    def _():
        low = lax.broadcasted_iota(jnp.int32, (acc_scr.shape[1], MLA_HG), 1) < MLA_V
        outs = []
        for h in range(MLA_HEADS):
            acc = acc_scr[h]
            outs.append(acc / acc[:, MLA_V:MLA_V + 1])
        for hp in range(MLA_HEADS // 2):
            odd = pltpu.roll(outs[2 * hp + 1], MLA_V, axis=1)
            out_ref[0, :, hp * MLA_HG:(hp + 1) * MLA_HG] = jnp.where(low, outs[2 * hp], odd)


def mla_rotary_key_columns(w):
    zl = jnp.zeros(w.shape[:-1] + (MLA_NOPE,), w.dtype)
    zr = jnp.zeros(w.shape[:-1] + (MLA_HG - MLA_NOPE - MLA_ROPE,), w.dtype)
    w_sw = jnp.concatenate([w[..., MLA_RHALF:], w[..., :MLA_RHALF]], axis=-1)
    return jnp.concatenate([zl, w, zr, zl, w_sw, zr], axis=-1)


def mla_mixer(arr, q_norm, w_qb, kv_norm, w_kvb, col_q=0, col_kv=MLA_Q_RANK, col_kr=MLA_Q_RANK + MLA_KV_RANK):
    b, s, _ = arr.shape
    tt = MLA_PREP_TOK
    assert col_q % MLA_Q_RANK == 0 and col_kv % MLA_KV_RANK == 0 and col_kr % (2 * MLA_HG) == 0
    wqa, wqb, wk, wv = _mla_pack_weights(w_qb, w_kvb)
    q_cos, q_sin, k_cos, k_sin = _mla_rope_tables(s)
    tok = lambda w, c=0: pl.BlockSpec((1, tt, w), lambda i, j: (i, j, c // w))
    full = lambda r, c: pl.BlockSpec((r, c), lambda i, j: (0, 0))
    tab = pl.BlockSpec((tt, MLA_HG), lambda i, j: (j, 0))
    q, kt, v = pl.pallas_call(
        _mla_prep_kernel,
        grid=(b, s // tt),
        in_specs=[tok(MLA_Q_RANK, col_q), tok(MLA_KV_RANK, col_kv), tok(2 * MLA_HG, col_kr),
                  full(1, MLA_Q_RANK), full(1, MLA_KV_RANK),
                  full(MLA_Q_RANK, MLA_QK_W), full(MLA_Q_RANK, MLA_QK_W),
                  full(MLA_KV_RANK, MLA_QK_W), full(MLA_KV_RANK, MLA_QK_W),
                  tab, tab, tab, tab],
        out_specs=[tok(MLA_QK_W), pl.BlockSpec((1, MLA_QK_W, tt), lambda i, j: (i, 0, j)), tok(MLA_QK_W)],
        out_shape=[jax.ShapeDtypeStruct((b, s, MLA_QK_W), MXU_DTYPE),
                   jax.ShapeDtypeStruct((b, MLA_QK_W, s), MXU_DTYPE),
                   jax.ShapeDtypeStruct((b, s, MLA_QK_W), MXU_DTYPE)],
        compiler_params=pltpu.CompilerParams(
            dimension_semantics=("arbitrary", "arbitrary"), vmem_limit_bytes=VMEM_LIMIT_BYTES),
        name="mla_prep",
    )(arr, arr, arr, q_norm.reshape(1, -1), kv_norm.reshape(1, -1), wqa, wqb, wk, wv,
      q_cos, q_sin, k_cos, k_sin)
    tq, tk = min(MLA_TQ, s), min(MLA_TK, s)
    assert s % tq == 0 and s % tk == 0 and s % tt == 0
    return pl.pallas_call(
        _mla_flash_kernel,
        grid=(b, s // tq, s // tk),
        in_specs=[pl.BlockSpec((1, tq, MLA_QK_W), lambda i, j, kk: (i, j, 0)),
                  pl.BlockSpec((1, MLA_QK_W, tk), lambda i, j, kk: (i, 0, kk)),
                  pl.BlockSpec((1, tk, MLA_QK_W), lambda i, j, kk: (i, kk, 0))],
        out_specs=pl.BlockSpec((1, tq, MLA_V_W), lambda i, j, kk: (i, j, 0)),
        out_shape=jax.ShapeDtypeStruct((b, s, MLA_V_W), F32),
        scratch_shapes=[pltpu.VMEM((MLA_HEADS, tq, MLA_HG), F32),
                        pltpu.VMEM((MLA_HEADS, tq, MLA_HG), F32)],
        compiler_params=pltpu.CompilerParams(
            dimension_semantics=("arbitrary", "arbitrary", "arbitrary"), vmem_limit_bytes=VMEM_LIMIT_BYTES),
        name="mla_flash",
    )(q, kt, v)


def t5_bucket(rel):
    nb = T5_BUCKETS // 2
    ret = np.where(rel > 0, nb, 0)
    n = np.abs(rel)
    max_exact = nb // 2
    large = max_exact + (np.log(np.maximum(n, 1) / max_exact) / np.log(T5_MAX_DIST / max_exact)
                         * (nb - max_exact)).astype(np.int64)
    large = np.minimum(large, nb - 1)
    return (ret + np.where(n < max_exact, n, large)).astype(np.int32)


DIL_DIM = DIL_HEADS * DIL_HD
DIL_HALF = 64
DIL_QB = 128
DIL_KW = DIL_QB + 2 * DIL_HALF
DIL_TL = 512
DIL_N_GROUPS = len(DIL_GROUPS)


def _dil_bias_table(t5_table, gi, dil):
    rel = np.arange(DIL_KW)[None, :] - DIL_HALF - np.arange(DIL_QB)[:, None]
    bias = t5_table[:, gi * DIL_HEADS:(gi + 1) * DIL_HEADS][t5_bucket(rel * dil)].astype(F32)
    bias = jnp.where((np.abs(rel) <= DIL_HALF)[:, :, None], bias, -jnp.inf)
    return bias.transpose(2, 0, 1).reshape(DIL_HEADS * DIL_QB, DIL_KW)


def _dil_kernel(prev_ref, cur_ref, next_ref, tab_ref, o_ref, lse_ref, k_scr, v_scr, *, seq_len):
    step = pl.program_id(2)
    for i, ref in enumerate((prev_ref, cur_ref, next_ref)):
        k_scr[i * DIL_TL:(i + 1) * DIL_TL, :] = ref[0, :, DIL_DIM:2 * DIL_DIM].astype(MXU_DTYPE)
        v_scr[i * DIL_TL:(i + 1) * DIL_TL, :] = ref[0, :, 2 * DIL_DIM:3 * DIL_DIM].astype(MXU_DTYPE)
    lane_head = lax.broadcasted_iota(jnp.int32, (DIL_QB, DIL_DIM), 1) // DIL_HD
    key_off = lax.broadcasted_iota(jnp.int32, (1, DIL_KW), 1)
    for n in range(DIL_TL // DIL_QB):
        w0 = DIL_TL + n * DIL_QB - DIL_HALF
        kpos = step * DIL_TL + (n * DIL_QB - DIL_HALF) + key_off
        valid = (kpos >= 0) & (kpos < seq_len)
        q = cur_ref[0, n * DIL_QB:(n + 1) * DIL_QB, 0:DIL_DIM] * np.float32(DIL_HD ** -0.5)
        qs = jnp.concatenate([jnp.where(lane_head == h, q, 0.0) for h in range(DIL_HEADS)], axis=0)
        logits = lax.dot_general(qs.astype(MXU_DTYPE), k_scr[w0:w0 + DIL_KW, :], (((1,), (1,)), ((), ())),
                                 preferred_element_type=F32) + tab_ref[...]
        logits = jnp.where(valid, logits, -jnp.inf)
        m = jnp.max(logits, axis=-1, keepdims=True)
        p = jnp.exp(logits - m)
        denom = jnp.sum(p, axis=-1, keepdims=True)
        o_all = jnp.dot(p.astype(MXU_DTYPE), v_scr[w0:w0 + DIL_KW, :], preferred_element_type=F32) / denom
        lse_all = m + jnp.log(denom)
        o = jnp.zeros((DIL_QB, DIL_DIM), F32)
        lse = jnp.zeros((DIL_QB, DIL_DIM), F32)
        for h in range(DIL_HEADS):
            rows = slice(h * DIL_QB, (h + 1) * DIL_QB)
            o = o + jnp.where(lane_head == h, o_all[rows, :], 0.0)
            lse = lse + jnp.where(lane_head == h, lse_all[rows, :], 0.0)
        o_ref[0, n * DIL_QB:(n + 1) * DIL_QB, :] = o
        lse_ref[0, n * DIL_QB:(n + 1) * DIL_QB, :] = lse


def _dil_group(qkv, t5_table, gi, dil, col):
    b, s, width = qkv.shape
    seq_len = s // dil
    assert seq_len % DIL_TL == 0 and width % (3 * DIL_DIM) == 0 and col % (3 * DIL_DIM) == 0
    n_steps = seq_len // DIL_TL
    gi_col = col // (3 * DIL_DIM) + gi
    if dil > 1:
        qkv = qkv[:, :, gi_col * 3 * DIL_DIM:(gi_col + 1) * 3 * DIL_DIM]
        width, gi_col = 3 * DIL_DIM, 0
    n_col = width // (3 * DIL_DIM)
    view = qkv.reshape(b, seq_len, dil * width)
    blk = (1, DIL_TL, 3 * DIL_DIM)
    out_shape = jax.ShapeDtypeStruct((b, seq_len, dil * DIL_DIM), F32)
    out_spec = pl.BlockSpec((1, DIL_TL, DIL_DIM), lambda i, r, j: (i, j, r))
    o, lse = pl.pallas_call(
        functools.partial(_dil_kernel, seq_len=seq_len),
        grid=(b, dil, n_steps),
        in_specs=[
            pl.BlockSpec(blk, lambda i, r, j: (i, jnp.maximum(j - 1, 0), r * n_col + gi_col)),
            pl.BlockSpec(blk, lambda i, r, j: (i, j, r * n_col + gi_col)),
            pl.BlockSpec(blk, lambda i, r, j: (i, jnp.minimum(j + 1, n_steps - 1), r * n_col + gi_col)),
            pl.BlockSpec((DIL_HEADS * DIL_QB, DIL_KW), lambda i, r, j: (0, 0)),
        ],
        out_specs=[out_spec, out_spec],
        out_shape=[out_shape, out_shape],
        scratch_shapes=[pltpu.VMEM((3 * DIL_TL, DIL_DIM), MXU_DTYPE),
                        pltpu.VMEM((3 * DIL_TL, DIL_DIM), MXU_DTYPE)],
        compiler_params=pltpu.CompilerParams(
            dimension_semantics=("arbitrary", "arbitrary", "arbitrary"), vmem_limit_bytes=VMEM_LIMIT_BYTES),
        name=f"dilated_attention_g{gi}",
    )(view, view, view, _dil_bias_table(t5_table, gi, dil))
    return o.reshape(b, s, DIL_DIM), lse.reshape(b, s, DIL_DIM)


def _dil_combine_kernel(*refs):
    o_refs, lse_refs, out_ref = refs[:DIL_N_GROUPS], refs[DIL_N_GROUPS:2 * DIL_N_GROUPS], refs[-1]
    lses = [r[...] for r in lse_refs]
    m = functools.reduce(jnp.maximum, lses)
    ws = [jnp.exp(l - m) for l in lses]
    total = functools.reduce(jnp.add, ws)
    acc = functools.reduce(jnp.add, [w * r[...] for w, r in zip(ws, o_refs)])
    out_ref[...] = acc / total


def dilated_mixer(qkv, t5_table, col=0):
    b, s, _ = qkv.shape
    outs, lses = [], []
    for gi, (win, dil) in enumerate(DIL_GROUPS):
        assert win // (2 * dil) == DIL_HALF
        o, lse = _dil_group(qkv, t5_table, gi, dil, col)
        outs.append(o.reshape(b * s, DIL_DIM))
        lses.append(lse.reshape(b * s, DIL_DIM))
    tm = 1024
    spec = pl.BlockSpec((tm, DIL_DIM), lambda i: (i, 0))
    out = pl.pallas_call(
        _dil_combine_kernel,
        grid=(b * s // tm,),
        in_specs=[spec] * (2 * DIL_N_GROUPS),
        out_specs=spec,
        out_shape=jax.ShapeDtypeStruct((b * s, DIL_DIM), F32),
        compiler_params=pltpu.CompilerParams(
            dimension_semantics=("arbitrary",), vmem_limit_bytes=VMEM_LIMIT_BYTES),
        name="dilated_combine",
    )(*outs, *lses)
    return out.reshape(b, s, DIL_DIM)


SSM_CONV_TOK = 512
SSM_HALO = SUBLANES
SSM_HEADS_PER_GROUP = SSM_HEADS // SSM_GROUPS
SSM_GROUP_W = SSM_HEADS_PER_GROUP * SSM_HD
SSM_BC_W = SSM_GROUPS * SSM_STATE


def _softplus(x):
    return jnp.maximum(x, 0.0) + jnp.log1p(jnp.exp(-jnp.abs(x)))


def _ssm_conv_kernel(prev_ref, cur_ref, next_ref, w_ref, b_ref, out_ref, cat_scr):
    j = pl.program_id(1)
    tl = cur_ref.shape[1]
    cat_scr[0:SSM_HALO, :] = jnp.where(j > 0, prev_ref[0], 0.0)
    cat_scr[SSM_HALO:SSM_HALO + tl, :] = cur_ref[0]
    cat_scr[SSM_HALO + tl:, :] = jnp.where(j < pl.num_programs(1) - 1, next_ref[0], 0.0)
    acc = jnp.zeros((tl, CONV_CH), F32) + b_ref[...]
    for k in range(SSM_CONV):
        off = SSM_HALO + k - SSM_CONV // 2
        acc = acc + cat_scr[off:off + tl, :] * w_ref[k:k + 1, :]
    out_ref[0] = acc * jax.nn.sigmoid(acc)


def _ssm_conv(xbc, conv_w, conv_b, col):
    b, l, _ = xbc.shape
    tl = SSM_CONV_TOK
    n_steps = l // tl
    per = tl // SSM_HALO
    assert col % CONV_CH == 0
    cb = col // CONV_CH
    return pl.pallas_call(
        _ssm_conv_kernel,
        grid=(b, n_steps),
        in_specs=[
            pl.BlockSpec((1, SSM_HALO, CONV_CH), lambda i, j: (i, jnp.maximum(j * per - 1, 0), cb)),
            pl.BlockSpec((1, tl, CONV_CH), lambda i, j: (i, j, cb)),
            pl.BlockSpec((1, SSM_HALO, CONV_CH), lambda i, j: (i, jnp.minimum((j + 1) * per, n_steps * per - 1), cb)),
            pl.BlockSpec((SSM_CONV, CONV_CH), lambda i, j: (0, 0)),
            pl.BlockSpec((1, CONV_CH), lambda i, j: (0, 0)),
        ],
        out_specs=pl.BlockSpec((1, tl, CONV_CH), lambda i, j: (i, j, 0)),
        out_shape=jax.ShapeDtypeStruct((b, l, CONV_CH), F32),
        scratch_shapes=[pltpu.VMEM((tl + 2 * SSM_HALO, CONV_CH), F32)],
        compiler_params=pltpu.CompilerParams(
            dimension_semantics=("arbitrary", "arbitrary"), vmem_limit_bytes=VMEM_LIMIT_BYTES),
        name="ssm_conv",
    )(xbc, xbc, xbc, conv_w, conv_b.reshape(1, CONV_CH))


def _ssd_kernel(xbc_ref, dt_ref, dtt_ref, expand_ref, bias_e_ref, a_e_ref, bias_t_ref, a_t_ref,
                y_ref, state_scr, *, reverse):
    q = SSM_CHUNK

    @pl.when(pl.program_id(1) == 0)
    def _():
        state_scr[...] = jnp.zeros(state_scr.shape, F32)

    hi = lax.Precision.HIGHEST
    xbc = xbc_ref[0]
    xs = xbc[:, 0:SSM_INNER]
    dt_e = _softplus(jnp.dot(dt_ref[0], expand_ref[...], precision=hi, preferred_element_type=F32)
                     + bias_e_ref[...])
    a_e = dt_e * a_e_ref[...]
    ri = lax.broadcasted_iota(jnp.int32, (q, q), 0)
    ci = lax.broadcasted_iota(jnp.int32, (q, q), 1)
    seen = (ci >= ri) if reverse else (ci <= ri)
    cs_e = jnp.dot(seen.astype(F32), a_e, precision=hi, preferred_element_type=F32)
    dt_t = _softplus(dtt_ref[0] + bias_t_ref[...])
    cs_t = jnp.dot(dt_t * a_t_ref[...], seen.T.astype(F32), precision=hi, preferred_element_type=F32)
    dtx = xs * dt_e
    last = 0 if reverse else q - 1
    total = cs_e[last:last + 1, :]
    dtx_decayed = jnp.exp(total - cs_e) * dtx
    grow = jnp.exp(cs_e)
    chunk_decay = jnp.exp(total)
    lane_head = lax.broadcasted_iota(jnp.int32, (q, SSM_GROUP_W), 1) // SSM_HD
    for g in range(SSM_GROUPS):
        xl = slice(g * SSM_GROUP_W, (g + 1) * SSM_GROUP_W)
        bg = xbc[:, SSM_INNER + g * SSM_STATE:SSM_INNER + (g + 1) * SSM_STATE]
        cg = xbc[:, SSM_INNER + SSM_BC_W + g * SSM_STATE:SSM_INNER + SSM_BC_W + (g + 1) * SSM_STATE]
        cb = lax.dot_general(cg.astype(MXU_DTYPE), bg.astype(MXU_DTYPE), (((1,), (1,)), ((), ())),
                             preferred_element_type=F32)
        ms = []
        for r in range(SSM_HEADS_PER_GROUP):
            h = g * SSM_HEADS_PER_GROUP + r
            col = cs_e[:, h * SSM_HD:h * SSM_HD + 1]
            row = cs_t[h:h + 1, :]
            ms.append(cb * jnp.where(seen, jnp.exp(col - row), 0.0))
        y_all = jnp.dot(jnp.concatenate(ms, axis=0).astype(MXU_DTYPE), dtx[:, xl].astype(MXU_DTYPE),
                        preferred_element_type=F32)
        y_diag = jnp.zeros((q, SSM_GROUP_W), F32)
        for r in range(SSM_HEADS_PER_GROUP):
            y_diag = y_diag + jnp.where(lane_head == r, y_all[r * q:(r + 1) * q, :], 0.0)
        s_in = state_scr[g]
        y_off = jnp.dot(cg.astype(MXU_DTYPE), s_in.astype(MXU_DTYPE), preferred_element_type=F32) * grow[:, xl]
        y_ref[0, :, xl] = y_diag + y_off
        new = jnp.dot(bg.T.astype(MXU_DTYPE), dtx_decayed[:, xl].astype(MXU_DTYPE), preferred_element_type=F32)
        state_scr[g] = s_in * chunk_decay[:, xl] + new


def _ssd_direction(xbc_act, dt_arr, dt_t, a_log, dt_bias, direction, col_dt, dt_w):
    b, l, _ = xbc_act.shape
    assert col_dt % dt_w == 0 and dt_w >= 2 * SSM_HEADS
    q = SSM_CHUNK
    nc = l // q
    reverse = direction == 1
    a = -jnp.exp(a_log[direction].astype(F32))
    bias = dt_bias[direction].astype(F32)
    head_of_lane = np.arange(SSM_INNER) // SSM_HD
    expand = (np.arange(dt_w)[:, None] == direction * SSM_HEADS + head_of_lane[None, :]).astype(np.float32)
    chunk = (lambda j: nc - 1 - j) if reverse else (lambda j: j)
    full = lambda r, c: pl.BlockSpec((r, c), lambda i, j: (0, 0))
    return pl.pallas_call(
        functools.partial(_ssd_kernel, reverse=reverse),
        grid=(b, nc),
        in_specs=[
            pl.BlockSpec((1, q, CONV_CH), lambda i, j: (i, chunk(j), 0)),
            pl.BlockSpec((1, q, dt_w), lambda i, j: (i, chunk(j), col_dt // dt_w)),
            pl.BlockSpec((1, SSM_HEADS, q), lambda i, j: (i * 2 + direction, 0, chunk(j))),
            full(dt_w, SSM_INNER), full(1, SSM_INNER), full(1, SSM_INNER),
            full(SSM_HEADS, 1), full(SSM_HEADS, 1),
        ],
        out_specs=pl.BlockSpec((1, q, SSM_INNER), lambda i, j: (i, chunk(j), 0)),
        out_shape=jax.ShapeDtypeStruct((b, l, SSM_INNER), F32),
        scratch_shapes=[pltpu.VMEM((SSM_GROUPS, SSM_STATE, SSM_GROUP_W), F32)],
        compiler_params=pltpu.CompilerParams(
            dimension_semantics=("arbitrary", "arbitrary"), vmem_limit_bytes=VMEM_LIMIT_BYTES),
        name=f"ssd_dir{direction}",
    )(xbc_act, dt_arr, dt_t, jnp.asarray(expand), bias[head_of_lane].reshape(1, SSM_INNER),
      a[head_of_lane].reshape(1, SSM_INNER), bias.reshape(SSM_HEADS, 1), a.reshape(SSM_HEADS, 1))


def _ssm_gate_kernel(yf_ref, yb_ref, xbc_ref, z_ref, d_ref, g_ref, out_ref):
    z = z_ref[...]
    y = (yf_ref[...] + yb_ref[...] + xbc_ref[...] * d_ref[...]) * (z * jax.nn.sigmoid(z))
    out_ref[...] = y * lax.rsqrt(jnp.mean(y * y, axis=-1, keepdims=True) + EPS) * g_ref[...]


def mamba2_mixer(arr, conv_w, conv_b, A_log, dt_bias, D_skip, norm_g,
                 col_z=0, col_xbc=CONV_CH, col_dt=SSM_INNER + CONV_CH, dt_w=2 * SSM_HEADS):
    b, l, width = arr.shape
    assert col_z % SSM_INNER == 0
    xbc_act = _ssm_conv(arr, conv_w, conv_b, col_xbc)
    dt_t = jnp.swapaxes(arr[:, :, col_dt:col_dt + 2 * SSM_HEADS], 1, 2).reshape(b * 2, SSM_HEADS, l)
    y_f = _ssd_direction(xbc_act, arr, dt_t, A_log, dt_bias, 0, col_dt, dt_w)
    y_b = _ssd_direction(xbc_act, arr, dt_t, A_log, dt_bias, 1, col_dt, dt_w)
    tm = 1024
    tok = lambda cb: pl.BlockSpec((tm, SSM_INNER), lambda i: (i, cb))
    row = pl.BlockSpec((1, SSM_INNER), lambda i: (0, 0))
    d_e = D_skip.astype(F32)[np.arange(SSM_INNER) // SSM_HD].reshape(1, SSM_INNER)
    out = pl.pallas_call(
        _ssm_gate_kernel,
        grid=(b * l // tm,),
        in_specs=[tok(0), tok(0), tok(0), tok(col_z // SSM_INNER), row, row],
        out_specs=tok(0),
        out_shape=jax.ShapeDtypeStruct((b * l, SSM_INNER), F32),
        compiler_params=pltpu.CompilerParams(
            dimension_semantics=("arbitrary",), vmem_limit_bytes=VMEM_LIMIT_BYTES),
        name="ssm_gate",
    )(y_f.reshape(b * l, SSM_INNER), y_b.reshape(b * l, SSM_INNER), xbc_act.reshape(b * l, CONV_CH),
      arr.reshape(b * l, width), d_e, norm_g.reshape(1, SSM_INNER))
    return out.reshape(b, l, SSM_INNER)


NA_DIM = NA_HEADS * NA_HD
NA_ROWS_PER_STEP = 8
NA_WIN = NA_ROWS * GRID_W
NA_STEP_TOK = NA_ROWS_PER_STEP * GRID_W


def _na_bias_table(rpb):
    n_dc = 2 * NA_COLS - 1
    edge_l = jnp.repeat(rpb[:, :, :1], GRID_W, axis=2)
    edge_r = jnp.repeat(rpb[:, :, -1:], GRID_W, axis=2)
    ext = jnp.concatenate([edge_l, rpb.astype(F32), edge_r], axis=2)
    by_col = jnp.stack([ext[:, :, GRID_W + NA_COLS - 1 - qc:2 * GRID_W + NA_COLS - 1 - qc]
                        for qc in range(GRID_W)], axis=2)
    qc = np.arange(GRID_W)[:, None]
    kc = np.arange(GRID_W)[None, :]
    cs = np.clip(qc - NA_COLS // 2, 0, GRID_W - NA_COLS)
    ok = (kc >= cs) & (kc < cs + NA_COLS)
    by_col = jnp.where(ok[None, None], by_col, -jnp.inf)
    tabs = []
    for delta in range(NA_ROWS):
        rows = by_col[:, NA_ROWS - 1 - delta:2 * NA_ROWS - 1 - delta]
        tabs.append(rows.transpose(0, 2, 1, 3).reshape(NA_HEADS * GRID_W, NA_WIN))
    assert n_dc == rpb.shape[2]
    return jnp.stack(tabs, axis=0)


def _na_kernel(prev_ref, cur_ref, next_ref, tab_ref, out_ref, k_scr, v_scr, *, n_rows):
    step = pl.program_id(1)
    for i, ref in enumerate((prev_ref, cur_ref, next_ref)):
        k_scr[i * NA_STEP_TOK:(i + 1) * NA_STEP_TOK, :] = ref[0, :, NA_DIM:2 * NA_DIM].astype(MXU_DTYPE)
        v_scr[i * NA_STEP_TOK:(i + 1) * NA_STEP_TOK, :] = ref[0, :, 2 * NA_DIM:3 * NA_DIM].astype(MXU_DTYPE)
    lane_head = lax.broadcasted_iota(jnp.int32, (GRID_W, NA_DIM), 1) // NA_HD
    row0 = step * NA_ROWS_PER_STEP
    for j in range(NA_ROWS_PER_STEP):
        r = row0 + j
        r0 = jnp.clip(r - NA_ROWS // 2, 0, n_rows - NA_ROWS)
        start = pl.multiple_of((r0 - row0 + NA_ROWS_PER_STEP) * GRID_W, GRID_W)
        q = cur_ref[0, j * GRID_W:(j + 1) * GRID_W, 0:NA_DIM] * np.float32(NA_HD ** -0.5)
        qs = jnp.concatenate([jnp.where(lane_head == h, q, 0.0) for h in range(NA_HEADS)], axis=0)
        kw = k_scr[pl.ds(start, NA_WIN), :]
        vw = v_scr[pl.ds(start, NA_WIN), :]
        logits = lax.dot_general(qs.astype(MXU_DTYPE), kw, (((1,), (1,)), ((), ())),
                                 preferred_element_type=F32) + tab_ref[r - r0]
        m = jnp.max(logits, axis=-1, keepdims=True)
        p = jnp.exp(logits - m)
        denom = jnp.sum(p, axis=-1, keepdims=True)
        o_all = jnp.dot(p.astype(MXU_DTYPE), vw, preferred_element_type=F32) / denom
        o = jnp.zeros((GRID_W, NA_DIM), F32)
        for h in range(NA_HEADS):
            o = o + jnp.where(lane_head == h, o_all[h * GRID_W:(h + 1) * GRID_W, :], 0.0)
        out_ref[0, j * GRID_W:(j + 1) * GRID_W, :] = o


def na_mixer(qkv, rpb, col=0):
    b, s, _ = qkv.shape
    n_rows = s // GRID_W
    assert n_rows >= NA_ROWS and n_rows % NA_ROWS_PER_STEP == 0 and col % (3 * NA_DIM) == 0
    n_steps = n_rows // NA_ROWS_PER_STEP
    blk = (1, NA_STEP_TOK, 3 * NA_DIM)
    cb = col // (3 * NA_DIM)
    return pl.pallas_call(
        functools.partial(_na_kernel, n_rows=n_rows),
        grid=(b, n_steps),
        in_specs=[
            pl.BlockSpec(blk, lambda i, j: (i, jnp.maximum(j - 1, 0), cb)),
            pl.BlockSpec(blk, lambda i, j: (i, j, cb)),
            pl.BlockSpec(blk, lambda i, j: (i, jnp.minimum(j + 1, n_steps - 1), cb)),
            pl.BlockSpec((NA_ROWS, NA_HEADS * GRID_W, NA_WIN), lambda i, j: (0, 0, 0)),
        ],
        out_specs=pl.BlockSpec((1, NA_STEP_TOK, NA_DIM), lambda i, j: (i, j, 0)),
        out_shape=jax.ShapeDtypeStruct((b, s, NA_DIM), F32),
        scratch_shapes=[
            pltpu.VMEM((3 * NA_STEP_TOK, NA_DIM), MXU_DTYPE),
            pltpu.VMEM((3 * NA_STEP_TOK, NA_DIM), MXU_DTYPE),
        ],
        compiler_params=pltpu.CompilerParams(
            dimension_semantics=("arbitrary", "arbitrary"), vmem_limit_bytes=VMEM_LIMIT_BYTES),
        name="na_attention",
    )(qkv, qkv, qkv, _na_bias_table(rpb))


PK_GATE = 0
PK_XBC = PK_GATE + N_BRANCH * D_MODEL
PK_CQ = PK_XBC + CONV_CH
PK_DIL = PK_CQ + MLA_Q_RANK
PK_NA = PK_DIL + DIL_N_GROUPS * 3 * DIL_DIM
PK_KR = PK_NA + 3 * NA_DIM
PK_Z = PK_KR + 2 * MLA_HG
PK_CKV = PK_Z + SSM_INNER
PK_DT = PK_CKV + MLA_KV_RANK
PK_DT_W = LANES
PK_WIDTH = 13 * 3 * DIL_DIM
INPROJ_TM = 512
INPROJ_TN = PK_WIDTH // 2
MERGE_TM = 512


def _pack_w_in(w_in_l):
    gate, a_cq, a_ckv, a_kr, b_qkv, c_z, c_xbc, c_dt, d_qkv = jnp.split(w_in_l, IN_SPLITS, axis=-1)
    zeros = lambda n: jnp.zeros((D_MODEL, n), w_in_l.dtype)
    cols = [gate, c_xbc, a_cq, b_qkv, d_qkv, mla_rotary_key_columns(a_kr), c_z, a_ckv,
            c_dt, zeros(PK_DT_W - 2 * SSM_HEADS)]
    packed = jnp.concatenate(cols, axis=-1)
    assert packed.shape[1] == PK_DT + PK_DT_W
    return jnp.concatenate([packed, zeros(PK_WIDTH - packed.shape[1])], axis=-1).astype(MXU_DTYPE)


def _inproj_kernel(x_ref, g_ref, w_ref, out_ref, h_scr):
    @pl.when(pl.program_id(1) == 0)
    def _():
        x = x_ref[...]
        h_scr[...] = (x * lax.rsqrt(jnp.mean(x * x, axis=-1, keepdims=True) + EPS) * g_ref[...]).astype(MXU_DTYPE)

    out_ref[...] = jnp.dot(h_scr[...], w_ref[...], preferred_element_type=F32)


def _inproj(x2d, norm_g, w_packed):
    n_tok = x2d.shape[0]
    tm, tn = INPROJ_TM, INPROJ_TN
    return pl.pallas_call(
        _inproj_kernel,
        grid=(n_tok // tm, PK_WIDTH // tn),
        in_specs=[pl.BlockSpec((tm, D_MODEL), lambda i, j: (i, 0)),
                  pl.BlockSpec((1, D_MODEL), lambda i, j: (0, 0)),
                  pl.BlockSpec((D_MODEL, tn), lambda i, j: (0, j))],
        out_specs=pl.BlockSpec((tm, tn), lambda i, j: (i, j)),
        out_shape=jax.ShapeDtypeStruct((n_tok, PK_WIDTH), F32),
        scratch_shapes=[pltpu.VMEM((tm, D_MODEL), MXU_DTYPE)],
        compiler_params=pltpu.CompilerParams(
            dimension_semantics=("arbitrary", "arbitrary"), vmem_limit_bytes=VMEM_LIMIT_BYTES),
        name="in_projection",
    )(x2d, norm_g.reshape(1, D_MODEL), w_packed)


def _merge_kernel(x_ref, gate_ref, bg_ref, ya_ref, yb_ref, yc_ref, yd_ref, wb_ref, wo_ref, out_ref):
    merged = jnp.zeros(x_ref.shape, F32)
    for i, y_ref in enumerate((ya_ref, yb_ref, yc_ref, yd_ref)):
        proj = jnp.dot(y_ref[...].astype(MXU_DTYPE), wb_ref[BRANCH_ROWS[i]:BRANCH_ROWS[i + 1], :],
                       preferred_element_type=F32)
        lanes = slice(i * D_MODEL, (i + 1) * D_MODEL)
        merged = merged + jax.nn.sigmoid(gate_ref[:, lanes] + bg_ref[:, lanes]) * proj
    out_ref[...] = x_ref[...] + jnp.dot(merged.astype(MXU_DTYPE), wo_ref[...], preferred_element_type=F32)


def _merge(x2d, packed, b_gate, ys, w_branch, w_out):
    n_tok = x2d.shape[0]
    tm = MERGE_TM
    tok = lambda w: pl.BlockSpec((tm, w), lambda i: (i, 0))
    full = lambda r, c: pl.BlockSpec((r, c), lambda i: (0, 0))
    n_gate = N_BRANCH * D_MODEL
    return pl.pallas_call(
        _merge_kernel,
        grid=(n_tok // tm,),
        in_specs=[tok(D_MODEL), tok(n_gate), full(1, n_gate)] + [tok(w) for w in BRANCH_WIDTHS]
                 + [full(BRANCH_ROWS[-1], D_MODEL), full(D_MODEL, D_MODEL)],
        out_specs=tok(D_MODEL),
        out_shape=jax.ShapeDtypeStruct((n_tok, D_MODEL), F32),
        compiler_params=pltpu.CompilerParams(
            dimension_semantics=("arbitrary",), vmem_limit_bytes=VMEM_LIMIT_BYTES),
        name="branch_merge",
    )(x2d, packed, b_gate.reshape(1, n_gate), *[y.reshape(n_tok, -1) for y in ys], w_branch, w_out)


def encoder(x, norm1_g, w_in_packed, b_gate, mla_q_norm, mla_w_qb, mla_kv_norm, mla_w_kvb, t5_table,
            ssm_conv_w, ssm_conv_b, ssm_A_log, ssm_dt_bias, ssm_D, ssm_norm_g, na_rpb,
            w_branch, w_out, norm2_g, peer_wq, peer_keys, peer_u, peer_vt, final_g):
    b, s, _ = x.shape
    x2d = x.reshape(b * s, D_MODEL)
    for l in range(DEPTH):
        packed = _inproj(x2d, norm1_g[l], w_in_packed[l])
        p3 = packed.reshape(b, s, PK_WIDTH)
        y_a = mla_mixer(p3, mla_q_norm[l], mla_w_qb[l], mla_kv_norm[l], mla_w_kvb[l],
                        col_q=PK_CQ, col_kv=PK_CKV, col_kr=PK_KR)
        y_b = dilated_mixer(p3, t5_table, col=PK_DIL)
        y_c = mamba2_mixer(p3, ssm_conv_w[l], ssm_conv_b[l], ssm_A_log[l], ssm_dt_bias[l], ssm_D[l],
                           ssm_norm_g[l], col_z=PK_Z, col_xbc=PK_XBC, col_dt=PK_DT, dt_w=PK_DT_W)
        y_d = na_mixer(p3, na_rpb[l], col=PK_NA)
        x2d = _merge(x2d, packed, b_gate[l], (y_a, y_b, y_c, y_d), w_branch[l], w_out[l])
        x2d = peer_block(x2d, norm2_g[l], peer_wq[l], peer_keys[l], peer_u[l], peer_vt[l],
                         final_g, final_norm=(l == DEPTH - 1))
    return x2d.reshape(b, s, D_MODEL)


def kernel(x_prompt, x_sample, norm1_g, w_in, b_gate, mla_q_norm, mla_w_qb, mla_kv_norm, mla_w_kvb, t5_table, ssm_conv_w, ssm_conv_b, ssm_A_log, ssm_dt_bias, ssm_D, ssm_norm_g, na_rpb, w_branch, w_out, norm2_g, peer_wq, peer_keys, peer_u, peer_v, final_g):
    peer_u16 = peer_u.astype(MXU_DTYPE)
    peer_vt16 = jnp.swapaxes(peer_v, 1, 2).astype(MXU_DTYPE)
    w_in_packed = jnp.stack([_pack_w_in(w_in[l]) for l in range(DEPTH)])
    shared = (norm1_g, w_in_packed, b_gate, mla_q_norm, mla_w_qb, mla_kv_norm, mla_w_kvb, t5_table,
              ssm_conv_w, ssm_conv_b, ssm_A_log, ssm_dt_bias, ssm_D, ssm_norm_g, na_rpb,
              w_branch.astype(MXU_DTYPE), w_out.astype(MXU_DTYPE), norm2_g, peer_wq, peer_keys,
              peer_u16, peer_vt16, final_g)
    y_prompt = encoder(x_prompt, *shared)
    y_sample = encoder(x_sample, *shared)
    return (y_prompt, y_sample)
```

```python
import functools
import math

import numpy as np
import jax
import jax.numpy as jnp
from jax import lax
from jax.experimental import pallas as pl
from jax.experimental.pallas import tpu as pltpu

F32 = jnp.float32
BF16 = jnp.bfloat16
MXU_DTYPE = BF16

D_MODEL = 1024
DEPTH = 2
GRID_W = 64
EPS = 1e-6
N_BRANCH = 4

MLA_HEADS = 4
MLA_Q_RANK = 256
MLA_KV_RANK = 128
MLA_NOPE = 64
MLA_ROPE = 32
MLA_V = 64
ROPE_THETA = 10000.0
Q_BLOCK = 128

DIL_GROUPS = ((128, 1), (512, 4), (2048, 16))
DIL_HEADS = 4
DIL_HD = 64
T5_BUCKETS = 32
T5_MAX_DIST = 1024

SSM_HEADS = 8
SSM_HD = 64
SSM_INNER = SSM_HEADS * SSM_HD
SSM_GROUPS = 2
SSM_STATE = 128
SSM_CONV = 7
SSM_CHUNK = 128
CONV_CH = SSM_INNER + 2 * SSM_GROUPS * SSM_STATE

NA_HEADS = 4
NA_HD = 64
NA_ROWS = 8
NA_COLS = 16
NA_QCB = 16
NA_KCB = NA_QCB + NA_COLS

PEER_HEADS = 8
PEER_KEYS = 128
PEER_EXPERTS = PEER_KEYS * PEER_KEYS
PEER_QDIM = 256
PEER_TOPK = 16
PEER_TOK_BLOCK = 128

BRANCH_WIDTHS = (MLA_HEADS * MLA_V, DIL_HEADS * DIL_HD, SSM_INNER, NA_HEADS * NA_HD)
BRANCH_ROWS = tuple(sum(BRANCH_WIDTHS[:i]) for i in range(N_BRANCH + 1))
IN_SIZES = (N_BRANCH * D_MODEL, MLA_Q_RANK, MLA_KV_RANK, MLA_ROPE,
            len(DIL_GROUPS) * 3 * DIL_HEADS * DIL_HD,
            SSM_INNER, CONV_CH, 2 * SSM_HEADS,
            3 * NA_HEADS * NA_HD)
IN_SPLITS = tuple(sum(IN_SIZES[:i + 1]) for i in range(len(IN_SIZES) - 1))

VMEM_LIMIT_BYTES = 56 * 1024 * 1024
LANES = 128
SUBLANES = 8


def rms_norm(x, g):
    x32 = x.astype(F32)
    y = x32 * lax.rsqrt(jnp.mean(x32 * x32, axis=-1, keepdims=True) + EPS)
    return (y * g.astype(F32)).astype(x.dtype)


PEER_ROUTE_TB = 512
PEER_TB = 512
PEER_EC = 1024
PEER_GATE_DTYPE = BF16
PEER_GATE_ROWS = 16
PEER_HALF = PEER_QDIM // 2
PEER_CAND_ROWS = 2 * SUBLANES + 7 * SUBLANES + SUBLANES


def _gelu_exact(x):
    return 0.5 * x * (1.0 + lax.erf(x * np.float32(math.sqrt(0.5))))


def _extract_desc(vals, n_out, out_ref, row0, with_rank=False):
    rank = jnp.full(vals.shape, float(n_out), F32) if with_rank else None
    for k in range(n_out):
        m = jnp.max(vals, axis=0, keepdims=True)
        out_ref[pl.ds(row0 + k, 1), :] = m
        hit = vals == m
        if with_rank:
            rank = jnp.where(hit, float(k), rank)
        vals = jnp.where(hit, -jnp.inf, vals)
    return rank


def _peer_route_kernel(x_ref, g_ref, wqt_ref, keys_ref,
                       xnt_ref, cnt_ref, rank_ref, e1_ref, e2_ref,
                       qt_scr, top_scr, cand_scr, tops_scr):
    x = x_ref[...]
    xn = x * lax.rsqrt(jnp.mean(x * x, axis=-1, keepdims=True) + EPS) * g_ref[...]
    xnt = xn.T.astype(MXU_DTYPE)
    xnt_ref[...] = xnt
    qt_scr[...] = jnp.dot(wqt_ref[...], xnt, preferred_element_type=F32).astype(MXU_DTYPE)

    def head(h, carry):
        q1 = qt_scr[pl.ds(pl.multiple_of(h * PEER_QDIM, PEER_QDIM), PEER_HALF), :]
        q2 = qt_scr[pl.ds(pl.multiple_of(h * PEER_QDIM + PEER_HALF, PEER_HALF), PEER_HALF), :]
        s1 = jnp.dot(keys_ref[2 * h], q1, preferred_element_type=F32)
        s2 = jnp.dot(keys_ref[2 * h + 1], q2, preferred_element_type=F32)
        _extract_desc(s1, PEER_TOPK, top_scr, 0)
        rank2 = _extract_desc(s2, PEER_TOPK, top_scr, PEER_TOPK, with_rank=True)
        t1 = top_scr[0:PEER_TOPK, :]
        t2 = top_scr[PEER_TOPK:2 * PEER_TOPK, :]
        cand_scr[0:2 * SUBLANES, :] = t1[0:1, :] + t2
        for a in range(1, SUBLANES):
            cand_scr[(a + 1) * SUBLANES:(a + 2) * SUBLANES, :] = t1[a:a + 1, :] + t2[0:SUBLANES, :]
        cand_scr[9 * SUBLANES:10 * SUBLANES, :] = t1[SUBLANES:2 * SUBLANES, :] + t2[0:1, :]
        _extract_desc(cand_scr[...], PEER_TOPK, tops_scr, 0)
        top_s = tops_scr[...]
        z = jnp.sum(jnp.exp(top_s - top_s[0:1, :]), axis=0, keepdims=True)
        tau = top_s[PEER_TOPK - 1:PEER_TOPK, :]
        cnt_top = jnp.zeros(t1.shape, F32)
        for b in range(PEER_TOPK):
            cnt_top = cnt_top + jnp.where(t1 + t2[b:b + 1, :] >= tau, 1.0, 0.0)
        cnt = jnp.zeros(s1.shape, F32)
        for a in range(PEER_TOPK):
            cnt = jnp.where(s1 == t1[a:a + 1, :], cnt_top[a:a + 1, :], cnt)
        cnt_ref[h] = cnt
        rank_ref[h] = rank2.astype(rank_ref.dtype)
        e1_ref[h] = jnp.exp(s1 - t1[0:1, :]) / z
        e2_ref[h] = jnp.exp(s2 - t2[0:1, :]).astype(e2_ref.dtype)
        return carry

    lax.fori_loop(0, PEER_HEADS, head, 0)


def _peer_route(x2d, g, wqt, keys):
    n_tok = x2d.shape[0]
    tb = PEER_ROUTE_TB
    rt_shape = jax.ShapeDtypeStruct((PEER_HEADS, PEER_KEYS, n_tok), F32)
    gate_shape = jax.ShapeDtypeStruct((PEER_HEADS, PEER_KEYS, n_tok), PEER_GATE_DTYPE)
    rt_spec = pl.BlockSpec((PEER_HEADS, PEER_KEYS, tb), lambda i: (0, 0, i))
    return pl.pallas_call(
        _peer_route_kernel,
        grid=(n_tok // tb,),
        in_specs=[
            pl.BlockSpec((tb, D_MODEL), lambda i: (i, 0)),
            pl.BlockSpec((1, D_MODEL), lambda i: (0, 0)),
            pl.BlockSpec((PEER_HEADS * PEER_QDIM, D_MODEL), lambda i: (0, 0)),
            pl.BlockSpec((2 * PEER_HEADS, PEER_KEYS, PEER_HALF), lambda i: (0, 0, 0)),
        ],
        out_specs=[
            pl.BlockSpec((D_MODEL, tb), lambda i: (0, i)),
            rt_spec, rt_spec, rt_spec, rt_spec,
        ],
        out_shape=[
            jax.ShapeDtypeStruct((D_MODEL, n_tok), MXU_DTYPE),
            rt_shape, gate_shape, rt_shape, gate_shape,
        ],
        scratch_shapes=[
            pltpu.VMEM((PEER_HEADS * PEER_QDIM, tb), MXU_DTYPE),
            pltpu.VMEM((2 * PEER_TOPK, tb), F32),
            pltpu.VMEM((PEER_CAND_ROWS, tb), F32),
            pltpu.VMEM((PEER_TOPK, tb), F32),
        ],
        compiler_params=pltpu.CompilerParams(
            dimension_semantics=("arbitrary",), vmem_limit_bytes=VMEM_LIMIT_BYTES),
        name="peer_route",
    )(x2d, g, wqt, keys)


def _peer_expert_kernel(x_ref, xnt_ref, cnt_ref, rank_ref, e1_ref, e2_ref, u_ref, vt_ref, fg_ref,
                        out_ref, acc_scr, w_scr, *, final_norm):
    c = pl.program_id(1)
    n_chunks = pl.num_programs(1) - 1
    n_i1 = PEER_EC // PEER_KEYS
    tb = acc_scr.shape[1]
    gdt = PEER_GATE_DTYPE

    @pl.when(c == 0)
    def _():
        acc_scr[...] = jnp.zeros_like(acc_scr)
        w_scr[1] = jnp.zeros(w_scr.shape[1:], w_scr.dtype)

    slot = c % 2
    n_groups = PEER_KEYS // PEER_GATE_ROWS
    acc_scr[...] += jnp.dot(vt_ref[...], w_scr[(c + 1) % 2], preferred_element_type=F32)
    hid = jnp.dot(u_ref[...], xnt_ref[...], preferred_element_type=F32)

    def gate_block(i1l, l0):
        lanes = slice(l0, l0 + LANES)
        gates = [jnp.zeros((PEER_GATE_ROWS, LANES), gdt) for _ in range(n_groups)]
        for h in range(PEER_HEADS):
            cntb = jnp.broadcast_to(cnt_ref[h, i1l:i1l + 1, lanes], (PEER_GATE_ROWS, LANES)).astype(gdt)
            e1b = jnp.broadcast_to(e1_ref[h, i1l:i1l + 1, lanes], (PEER_GATE_ROWS, LANES)).astype(gdt)
            for k in range(n_groups):
                rows = slice(k * PEER_GATE_ROWS, (k + 1) * PEER_GATE_ROWS)
                w = e2_ref[h, rows, lanes] * e1b
                gates[k] = gates[k] + jnp.where(rank_ref[h, rows, lanes] < cntb, w, jnp.zeros_like(w))
        for k in range(n_groups):
            r0 = k * PEER_GATE_ROWS
            e0 = i1l * PEER_KEYS + r0
            act = _gelu_exact(hid[e0:e0 + PEER_GATE_ROWS, lanes]).astype(gdt)
            w_scr[slot, e0:e0 + PEER_GATE_ROWS, lanes] = (act * gates[k]).astype(MXU_DTYPE)

    for i1l in range(n_i1):
        for l0 in range(0, tb, LANES):
            gate_block(i1l, l0)

    @pl.when(c == n_chunks)
    def _():
        y = x_ref[...] + acc_scr[...].T
        if final_norm:
            y = y * lax.rsqrt(jnp.mean(y * y, axis=-1, keepdims=True) + EPS) * fg_ref[...]
        out_ref[...] = y


def _peer_experts(x2d, xnt, cnt, rank2, e1, e2, u, vt, final_g, final_norm):
    n_tok = x2d.shape[0]
    tb, ec = PEER_TB, PEER_EC
    n_chunks = PEER_EXPERTS // ec
    rt_spec = pl.BlockSpec((PEER_HEADS, PEER_KEYS, tb), lambda j, c: (0, 0, j))
    row_spec = pl.BlockSpec((PEER_HEADS, ec // PEER_KEYS, tb),
                            lambda j, c: (0, jnp.minimum(c, n_chunks - 1), j))
    return pl.pallas_call(
        functools.partial(_peer_expert_kernel, final_norm=final_norm),
        grid=(n_tok // tb, n_chunks + 1),
        in_specs=[
            pl.BlockSpec((tb, D_MODEL), lambda j, c: (j, 0)),
            pl.BlockSpec((D_MODEL, tb), lambda j, c: (0, j)),
            row_spec, rt_spec, row_spec, rt_spec,
            pl.BlockSpec((ec, D_MODEL), lambda j, c: (jnp.minimum(c, n_chunks - 1), 0)),
            pl.BlockSpec((D_MODEL, ec), lambda j, c: (0, jnp.maximum(c - 1, 0))),
            pl.BlockSpec((1, D_MODEL), lambda j, c: (0, 0)),
        ],
        out_specs=pl.BlockSpec((tb, D_MODEL), lambda j, c: (j, 0)),
        out_shape=jax.ShapeDtypeStruct((n_tok, D_MODEL), F32),
        scratch_shapes=[
            pltpu.VMEM((D_MODEL, tb), F32),
            pltpu.VMEM((2, ec, tb), MXU_DTYPE),
        ],
        compiler_params=pltpu.CompilerParams(
            dimension_semantics=("arbitrary", "arbitrary"), vmem_limit_bytes=VMEM_LIMIT_BYTES),
        name="peer_experts",
    )(x2d, xnt, cnt, rank2, e1, e2, u, vt, final_g.reshape(1, D_MODEL))


def peer_block(x2d, norm_g, w_q, keys, u, v, final_g, final_norm=False):
    wqt = w_q.T.astype(MXU_DTYPE)
    keys2 = keys.reshape(2 * PEER_HEADS, PEER_KEYS, PEER_HALF).astype(MXU_DTYPE)
    xnt, cnt, rank2, e1, e2 = _peer_route(x2d, norm_g.reshape(1, D_MODEL), wqt, keys2)
    return _peer_experts(x2d, xnt, cnt, rank2, e1, e2, u, v, final_g, final_norm)


MLA_HG = LANES
MLA_QK_W = MLA_HEADS * MLA_HG
MLA_V_W = MLA_HEADS * MLA_V
MLA_PREP_TOK = 512
MLA_TQ = 512
MLA_TK = 2048
MLA_RHALF = MLA_ROPE // 2


def _mla_rope_tables(s):
    inv = ROPE_THETA ** (-jnp.arange(MLA_RHALF, dtype=F32) / MLA_RHALF)
    ang = jnp.arange(s).astype(F32)[:, None] * inv[None, :]
    cos, sin = jnp.cos(ang), jnp.sin(ang)
    zero_pad = jnp.zeros((s, MLA_HG - MLA_NOPE - MLA_ROPE), F32)
    cos_rot = jnp.concatenate([cos, cos, zero_pad], axis=1)
    sin_rot = jnp.concatenate([-sin, sin, zero_pad], axis=1)
    scale = np.float32((MLA_NOPE + MLA_ROPE) ** -0.5)
    q_cos = scale * jnp.concatenate([jnp.ones((s, MLA_NOPE), F32), cos_rot], axis=1)
    q_sin = scale * jnp.concatenate([jnp.zeros((s, MLA_NOPE), F32), sin_rot], axis=1)
    k_cos = jnp.concatenate([jnp.zeros((s, MLA_NOPE), F32), cos_rot], axis=1)
    k_sin = jnp.concatenate([jnp.zeros((s, MLA_NOPE), F32), sin_rot], axis=1)
    return q_cos, q_sin, k_cos, k_sin


def _mla_pack_weights(w_qb, w_kvb):
    hd_q = MLA_NOPE + MLA_ROPE
    wq = w_qb.reshape(MLA_Q_RANK, MLA_HEADS, hd_q)
    rot = wq[:, :, MLA_NOPE:]
    rot_sw = jnp.concatenate([rot[:, :, MLA_RHALF:], rot[:, :, :MLA_RHALF]], axis=2)
    pad = jnp.zeros((MLA_Q_RANK, MLA_HEADS, MLA_HG - hd_q), F32)
    wq_a = jnp.concatenate([wq, pad], axis=2).reshape(MLA_Q_RANK, MLA_QK_W)
    wq_b = jnp.concatenate([jnp.zeros_like(wq[:, :, :MLA_NOPE]), rot_sw, pad], axis=2).reshape(MLA_Q_RANK, MLA_QK_W)
    wkv = w_kvb.reshape(MLA_KV_RANK, MLA_HEADS, MLA_NOPE + MLA_V)
    wk = jnp.concatenate([wkv[:, :, :MLA_NOPE], jnp.zeros((MLA_KV_RANK, MLA_HEADS, MLA_HG - MLA_NOPE), F32)],
                         axis=2).reshape(MLA_KV_RANK, MLA_QK_W)
    wv = jnp.concatenate([wkv[:, :, MLA_NOPE:], jnp.zeros((MLA_KV_RANK, MLA_HEADS, MLA_HG - MLA_V), F32)],
                         axis=2).reshape(MLA_KV_RANK, MLA_QK_W)
    return (wq_a.astype(MXU_DTYPE), wq_b.astype(MXU_DTYPE), wk.astype(MXU_DTYPE), wv.astype(MXU_DTYPE))


def _mla_prep_kernel(cq_ref, ckv_ref, kr_ref, qn_ref, kvn_ref, wqa_ref, wqb_ref, wk_ref, wv_ref,
                     qcos_ref, qsin_ref, kcos_ref, ksin_ref, q_out, kt_out, v_out):
    cq = cq_ref[0]
    cqn = (cq * lax.rsqrt(jnp.mean(cq * cq, axis=-1, keepdims=True) + EPS) * qn_ref[...]).astype(MXU_DTYPE)
    qa = jnp.dot(cqn, wqa_ref[...], preferred_element_type=F32)
    qb = jnp.dot(cqn, wqb_ref[...], preferred_element_type=F32)
    ckv = ckv_ref[0]
    ckvn = (ckv * lax.rsqrt(jnp.mean(ckv * ckv, axis=-1, keepdims=True) + EPS) * kvn_ref[...]).astype(MXU_DTYPE)
    ka = jnp.dot(ckvn, wk_ref[...], preferred_element_type=F32)
    one_lane = (lax.broadcasted_iota(jnp.int32, (1, MLA_QK_W), 1) % MLA_HG == MLA_V).astype(F32)
    v_out[0] = (jnp.dot(ckvn, wv_ref[...], preferred_element_type=F32) + one_lane).astype(MXU_DTYPE)
    kr = kr_ref[0]
    k_rot = kr[:, 0:MLA_HG] * kcos_ref[...] + kr[:, MLA_HG:2 * MLA_HG] * ksin_ref[...]
    for h in range(MLA_HEADS):
        lanes = slice(h * MLA_HG, (h + 1) * MLA_HG)
        q_out[0, :, lanes] = (qa[:, lanes] * qcos_ref[...] + qb[:, lanes] * qsin_ref[...]).astype(MXU_DTYPE)
        kt_out[0, lanes, :] = (ka[:, lanes] + k_rot).T.astype(MXU_DTYPE)


def _mla_flash_kernel(q_ref, kt_ref, v_ref, out_ref, m_scr, acc_scr):
    ki = pl.program_id(2)

    @pl.when(ki == 0)
    def _():
        m_scr[...] = jnp.full(m_scr.shape, -jnp.inf, F32)
        acc_scr[...] = jnp.zeros(acc_scr.shape, F32)

    def logits(h):
        lanes = slice(h * MLA_HG, (h + 1) * MLA_HG)
        return jnp.dot(q_ref[0, :, lanes], kt_ref[0, lanes, :], preferred_element_type=F32)

    n_rep = kt_ref.shape[2] // MLA_HG
    s_next = logits(0)
    for h in range(MLA_HEADS):
        s = s_next
        if h + 1 < MLA_HEADS:
            s_next = logits(h + 1)
        m_old = m_scr[h]
        m_new = jnp.maximum(m_old, jnp.max(s, axis=-1, keepdims=True))
        p = jnp.exp(s - jnp.tile(m_new, (1, n_rep))).astype(MXU_DTYPE)
        acc_scr[h] = jnp.exp(m_old - m_new) * acc_scr[h] + jnp.dot(
            p, v_ref[0, :, h * MLA_HG:(h + 1) * MLA_HG], preferred_element_type=F32)
        m_scr[h] = m_new

    @pl.when(ki == pl.num_programs(2) - 1)
    def _():
        low = lax.broadcasted_iota(jnp.int32, (acc_scr.shape[1], MLA_HG), 1) < MLA_V
        outs = []
        for h in range(MLA_HEADS):
            acc = acc_scr[h]
            outs.append(acc / acc[:, MLA_V:MLA_V + 1])
        for hp in range(MLA_HEADS // 2):
            odd = pltpu.roll(outs[2 * hp + 1], MLA_V, axis=1)
            out_ref[0, :, hp * MLA_HG:(hp + 1) * MLA_HG] = jnp.where(low, outs[2 * hp], odd)


def mla_rotary_key_columns(w):
    zl = jnp.zeros(w.shape[:-1] + (MLA_NOPE,), w.dtype)
    zr = jnp.zeros(w.shape[:-1] + (MLA_HG - MLA_NOPE - MLA_ROPE,), w.dtype)
    w_sw = jnp.concatenate([w[..., MLA_RHALF:], w[..., :MLA_RHALF]], axis=-1)
    return jnp.concatenate([zl, w, zr, zl, w_sw, zr], axis=-1)


def mla_mixer(arr, q_norm, w_qb, kv_norm, w_kvb, col_q=0, col_kv=MLA_Q_RANK, col_kr=MLA_Q_RANK + MLA_KV_RANK):
    b, s, _ = arr.shape
    tt = MLA_PREP_TOK
    assert col_q % MLA_Q_RANK == 0 and col_kv % MLA_KV_RANK == 0 and col_kr % (2 * MLA_HG) == 0
    wqa, wqb, wk, wv = _mla_pack_weights(w_qb, w_kvb)
    q_cos, q_sin, k_cos, k_sin = _mla_rope_tables(s)
    tok = lambda w, c=0: pl.BlockSpec((1, tt, w), lambda i, j: (i, j, c // w))
    full = lambda r, c: pl.BlockSpec((r, c), lambda i, j: (0, 0))
    tab = pl.BlockSpec((tt, MLA_HG), lambda i, j: (j, 0))
    q, kt, v = pl.pallas_call(
        _mla_prep_kernel,
        grid=(b, s // tt),
        in_specs=[tok(MLA_Q_RANK, col_q), tok(MLA_KV_RANK, col_kv), tok(2 * MLA_HG, col_kr),
                  full(1, MLA_Q_RANK), full(1, MLA_KV_RANK),
                  full(MLA_Q_RANK, MLA_QK_W), full(MLA_Q_RANK, MLA_QK_W),
                  full(MLA_KV_RANK, MLA_QK_W), full(MLA_KV_RANK, MLA_QK_W),
                  tab, tab, tab, tab],
        out_specs=[tok(MLA_QK_W), pl.BlockSpec((1, MLA_QK_W, tt), lambda i, j: (i, 0, j)), tok(MLA_QK_W)],
        out_shape=[jax.ShapeDtypeStruct((b, s, MLA_QK_W), MXU_DTYPE),
                   jax.ShapeDtypeStruct((b, MLA_QK_W, s), MXU_DTYPE),
                   jax.ShapeDtypeStruct((b, s, MLA_QK_W), MXU_DTYPE)],
        compiler_params=pltpu.CompilerParams(
            dimension_semantics=("arbitrary", "arbitrary"), vmem_limit_bytes=VMEM_LIMIT_BYTES),
        name="mla_prep",
    )(arr, arr, arr, q_norm.reshape(1, -1), kv_norm.reshape(1, -1), wqa, wqb, wk, wv,
      q_cos, q_sin, k_cos, k_sin)
    tq, tk = min(MLA_TQ, s), min(MLA_TK, s)
    assert s % tq == 0 and s % tk == 0 and s % tt == 0
    return pl.pallas_call(
        _mla_flash_kernel,
        grid=(b, s // tq, s // tk),
        in_specs=[pl.BlockSpec((1, tq, MLA_QK_W), lambda i, j, kk: (i, j, 0)),
                  pl.BlockSpec((1, MLA_QK_W, tk), lambda i, j, kk: (i, 0, kk)),
                  pl.BlockSpec((1, tk, MLA_QK_W), lambda i, j, kk: (i, kk, 0))],
        out_specs=pl.BlockSpec((1, tq, MLA_V_W), lambda i, j, kk: (i, j, 0)),
        out_shape=jax.ShapeDtypeStruct((b, s, MLA_V_W), F32),
        scratch_shapes=[pltpu.VMEM((MLA_HEADS, tq, MLA_HG), F32),
                        pltpu.VMEM((MLA_HEADS, tq, MLA_HG), F32)],
        compiler_params=pltpu.CompilerParams(
            dimension_semantics=("arbitrary", "arbitrary", "arbitrary"), vmem_limit_bytes=VMEM_LIMIT_BYTES),
        name="mla_flash",
    )(q, kt, v)


def t5_bucket(rel):
    nb = T5_BUCKETS // 2
    ret = np.where(rel > 0, nb, 0)
    n = np.abs(rel)
    max_exact = nb // 2
    large = max_exact + (np.log(np.maximum(n, 1) / max_exact) / np.log(T5_MAX_DIST / max_exact)
                         * (nb - max_exact)).astype(np.int64)
    large = np.minimum(large, nb - 1)
    return (ret + np.where(n < max_exact, n, large)).astype(np.int32)


DIL_DIM = DIL_HEADS * DIL_HD
DIL_HALF = 64
DIL_QB = 128
DIL_KW = DIL_QB + 2 * DIL_HALF
DIL_TL = 512
DIL_N_GROUPS = len(DIL_GROUPS)


def _dil_bias_table(t5_table, gi, dil):
    rel = np.arange(DIL_KW)[None, :] - DIL_HALF - np.arange(DIL_QB)[:, None]
    bias = t5_table[:, gi * DIL_HEADS:(gi + 1) * DIL_HEADS][t5_bucket(rel * dil)].astype(F32)
    bias = jnp.where((np.abs(rel) <= DIL_HALF)[:, :, None], bias, -jnp.inf)
    return bias.transpose(2, 0, 1).reshape(DIL_HEADS * DIL_QB, DIL_KW)


def _dil_kernel(prev_ref, cur_ref, next_ref, tab_ref, o_ref, lse_ref, k_scr, v_scr, *, seq_len):
    step = pl.program_id(2)
    for i, ref in enumerate((prev_ref, cur_ref, next_ref)):
        k_scr[i * DIL_TL:(i + 1) * DIL_TL, :] = ref[0, :, DIL_DIM:2 * DIL_DIM].astype(MXU_DTYPE)
        v_scr[i * DIL_TL:(i + 1) * DIL_TL, :] = ref[0, :, 2 * DIL_DIM:3 * DIL_DIM].astype(MXU_DTYPE)
    lane_head = lax.broadcasted_iota(jnp.int32, (DIL_QB, DIL_DIM), 1) // DIL_HD
    key_off = lax.broadcasted_iota(jnp.int32, (1, DIL_KW), 1)
    for n in range(DIL_TL // DIL_QB):
        w0 = DIL_TL + n * DIL_QB - DIL_HALF
        kpos = step * DIL_TL + (n * DIL_QB - DIL_HALF) + key_off
        valid = (kpos >= 0) & (kpos < seq_len)
        q = cur_ref[0, n * DIL_QB:(n + 1) * DIL_QB, 0:DIL_DIM] * np.float32(DIL_HD ** -0.5)
        qs = jnp.concatenate([jnp.where(lane_head == h, q, 0.0) for h in range(DIL_HEADS)], axis=0)
        logits = lax.dot_general(qs.astype(MXU_DTYPE), k_scr[w0:w0 + DIL_KW, :], (((1,), (1,)), ((), ())),
                                 preferred_element_type=F32) + tab_ref[...]
        logits = jnp.where(valid, logits, -jnp.inf)
        m = jnp.max(logits, axis=-1, keepdims=True)
        p = jnp.exp(logits - m)
        denom = jnp.sum(p, axis=-1, keepdims=True)
        o_all = jnp.dot(p.astype(MXU_DTYPE), v_scr[w0:w0 + DIL_KW, :], preferred_element_type=F32) / denom
        lse_all = m + jnp.log(denom)
        o = jnp.zeros((DIL_QB, DIL_DIM), F32)
        lse = jnp.zeros((DIL_QB, DIL_DIM), F32)
        for h in range(DIL_HEADS):
            rows = slice(h * DIL_QB, (h + 1) * DIL_QB)
            o = o + jnp.where(lane_head == h, o_all[rows, :], 0.0)
            lse = lse + jnp.where(lane_head == h, lse_all[rows, :], 0.0)
        o_ref[0, n * DIL_QB:(n + 1) * DIL_QB, :] = o
        lse_ref[0, n * DIL_QB:(n + 1) * DIL_QB, :] = lse


def _dil_group(qkv, t5_table, gi, dil, col):
    b, s, width = qkv.shape
    seq_len = s // dil
    assert seq_len % DIL_TL == 0 and width % (3 * DIL_DIM) == 0 and col % (3 * DIL_DIM) == 0
    n_steps = seq_len // DIL_TL
    gi_col = col // (3 * DIL_DIM) + gi
    if dil > 1:
        qkv = qkv[:, :, gi_col * 3 * DIL_DIM:(gi_col + 1) * 3 * DIL_DIM]
        width, gi_col = 3 * DIL_DIM, 0
    n_col = width // (3 * DIL_DIM)
    view = qkv.reshape(b, seq_len, dil * width)
    blk = (1, DIL_TL, 3 * DIL_DIM)
    out_shape = jax.ShapeDtypeStruct((b, seq_len, dil * DIL_DIM), F32)
    out_spec = pl.BlockSpec((1, DIL_TL, DIL_DIM), lambda i, r, j: (i, j, r))
    o, lse = pl.pallas_call(
        functools.partial(_dil_kernel, seq_len=seq_len),
        grid=(b, dil, n_steps),
        in_specs=[
            pl.BlockSpec(blk, lambda i, r, j: (i, jnp.maximum(j - 1, 0), r * n_col + gi_col)),
            pl.BlockSpec(blk, lambda i, r, j: (i, j, r * n_col + gi_col)),
            pl.BlockSpec(blk, lambda i, r, j: (i, jnp.minimum(j + 1, n_steps - 1), r * n_col + gi_col)),
            pl.BlockSpec((DIL_HEADS * DIL_QB, DIL_KW), lambda i, r, j: (0, 0)),
        ],
        out_specs=[out_spec, out_spec],
        out_shape=[out_shape, out_shape],
        scratch_shapes=[pltpu.VMEM((3 * DIL_TL, DIL_DIM), MXU_DTYPE),
                        pltpu.VMEM((3 * DIL_TL, DIL_DIM), MXU_DTYPE)],
        compiler_params=pltpu.CompilerParams(
            dimension_semantics=("arbitrary", "arbitrary", "arbitrary"), vmem_limit_bytes=VMEM_LIMIT_BYTES),
        name=f"dilated_attention_g{gi}",
    )(view, view, view, _dil_bias_table(t5_table, gi, dil))
    return o.reshape(b, s, DIL_DIM), lse.reshape(b, s, DIL_DIM)


def _dil_combine_kernel(*refs):
    o_refs, lse_refs, out_ref = refs[:DIL_N_GROUPS], refs[DIL_N_GROUPS:2 * DIL_N_GROUPS], refs[-1]
    lses = [r[...] for r in lse_refs]
    m = functools.reduce(jnp.maximum, lses)
    ws = [jnp.exp(l - m) for l in lses]
    total = functools.reduce(jnp.add, ws)
    acc = functools.reduce(jnp.add, [w * r[...] for w, r in zip(ws, o_refs)])
    out_ref[...] = acc / total


def dilated_mixer(qkv, t5_table, col=0):
    b, s, _ = qkv.shape
    outs, lses = [], []
    for gi, (win, dil) in enumerate(DIL_GROUPS):
        assert win // (2 * dil) == DIL_HALF
        o, lse = _dil_group(qkv, t5_table, gi, dil, col)
        outs.append(o.reshape(b * s, DIL_DIM))
        lses.append(lse.reshape(b * s, DIL_DIM))
    tm = 1024
    spec = pl.BlockSpec((tm, DIL_DIM), lambda i: (i, 0))
    out = pl.pallas_call(
        _dil_combine_kernel,
        grid=(b * s // tm,),
        in_specs=[spec] * (2 * DIL_N_GROUPS),
        out_specs=spec,
        out_shape=jax.ShapeDtypeStruct((b * s, DIL_DIM), F32),
        compiler_params=pltpu.CompilerParams(
            dimension_semantics=("arbitrary",), vmem_limit_bytes=VMEM_LIMIT_BYTES),
        name="dilated_combine",
    )(*outs, *lses)
    return out.reshape(b, s, DIL_DIM)


SSM_CONV_TOK = 512
SSM_HALO = SUBLANES
SSM_HEADS_PER_GROUP = SSM_HEADS // SSM_GROUPS
SSM_GROUP_W = SSM_HEADS_PER_GROUP * SSM_HD
SSM_BC_W = SSM_GROUPS * SSM_STATE


def _softplus(x):
    return jnp.maximum(x, 0.0) + jnp.log1p(jnp.exp(-jnp.abs(x)))


def _ssm_conv_kernel(prev_ref, cur_ref, next_ref, w_ref, b_ref, out_ref, cat_scr):
    j = pl.program_id(1)
    tl = cur_ref.shape[1]
    cat_scr[0:SSM_HALO, :] = jnp.where(j > 0, prev_ref[0], 0.0)
    cat_scr[SSM_HALO:SSM_HALO + tl, :] = cur_ref[0]
    cat_scr[SSM_HALO + tl:, :] = jnp.where(j < pl.num_programs(1) - 1, next_ref[0], 0.0)
    acc = jnp.zeros((tl, CONV_CH), F32) + b_ref[...]
    for k in range(SSM_CONV):
        off = SSM_HALO + k - SSM_CONV // 2
        acc = acc + cat_scr[off:off + tl, :] * w_ref[k:k + 1, :]
    out_ref[0] = acc * jax.nn.sigmoid(acc)


def _ssm_conv(xbc, conv_w, conv_b, col):
    b, l, _ = xbc.shape
    tl = SSM_CONV_TOK
    n_steps = l // tl
    per = tl // SSM_HALO
    assert col % CONV_CH == 0
    cb = col // CONV_CH
    return pl.pallas_call(
        _ssm_conv_kernel,
        grid=(b, n_steps),
        in_specs=[
            pl.BlockSpec((1, SSM_HALO, CONV_CH), lambda i, j: (i, jnp.maximum(j * per - 1, 0), cb)),
            pl.BlockSpec((1, tl, CONV_CH), lambda i, j: (i, j, cb)),
            pl.BlockSpec((1, SSM_HALO, CONV_CH), lambda i, j: (i, jnp.minimum((j + 1) * per, n_steps * per - 1), cb)),
            pl.BlockSpec((SSM_CONV, CONV_CH), lambda i, j: (0, 0)),
            pl.BlockSpec((1, CONV_CH), lambda i, j: (0, 0)),
        ],
        out_specs=pl.BlockSpec((1, tl, CONV_CH), lambda i, j: (i, j, 0)),
        out_shape=jax.ShapeDtypeStruct((b, l, CONV_CH), F32),
        scratch_shapes=[pltpu.VMEM((tl + 2 * SSM_HALO, CONV_CH), F32)],
        compiler_params=pltpu.CompilerParams(
            dimension_semantics=("arbitrary", "arbitrary"), vmem_limit_bytes=VMEM_LIMIT_BYTES),
        name="ssm_conv",
    )(xbc, xbc, xbc, conv_w, conv_b.reshape(1, CONV_CH))


def _ssd_kernel(xf_ref, dtf_ref, dttf_ref, xb_ref, dtb_ref, dttb_ref,
                expf_ref, bef_ref, aef_ref, btf_ref, atf_ref, expb_ref, beb_ref, aeb_ref, btb_ref, atb_ref,
                yf_ref, yb_ref, state_scr):
    @pl.when(pl.program_id(1) == 0)
    def _():
        state_scr[...] = jnp.zeros(state_scr.shape, F32)

    fwd = _ssd_chunk(xf_ref, dtf_ref, dttf_ref, expf_ref, bef_ref, aef_ref, btf_ref, atf_ref,
                     yf_ref, state_scr.at[0], reverse=False)
    bwd = _ssd_chunk(xb_ref, dtb_ref, dttb_ref, expb_ref, beb_ref, aeb_ref, btb_ref, atb_ref,
                     yb_ref, state_scr.at[1], reverse=True)
    for _ in zip(fwd, bwd):
        pass
    for _ in fwd:
        pass
    for _ in bwd:
        pass


def _ssd_chunk(xbc_ref, dt_ref, dtt_ref, expand_ref, bias_e_ref, a_e_ref, bias_t_ref, a_t_ref,
               y_ref, state_scr, *, reverse):
    q = SSM_CHUNK
    hi = lax.Precision.HIGHEST
    xbc = xbc_ref[0]
    xs = xbc[:, 0:SSM_INNER]
    dt_e = _softplus(jnp.dot(dt_ref[0], expand_ref[...], precision=hi, preferred_element_type=F32)
                     + bias_e_ref[...])
    a_e = dt_e * a_e_ref[...]
    ri = lax.broadcasted_iota(jnp.int32, (q, q), 0)
    ci = lax.broadcasted_iota(jnp.int32, (q, q), 1)
    seen = (ci >= ri) if reverse else (ci <= ri)
    cs_e = jnp.dot(seen.astype(F32), a_e, precision=hi, preferred_element_type=F32)
    dt_t = _softplus(dtt_ref[0] + bias_t_ref[...])
    cs_t = jnp.dot(dt_t * a_t_ref[...], seen.T.astype(F32), precision=hi, preferred_element_type=F32)
    yield
    dtx = xs * dt_e
    last = 0 if reverse else q - 1
    total = cs_e[last:last + 1, :]
    dtx_decayed = jnp.exp(total - cs_e) * dtx
    grow = jnp.exp(cs_e)
    chunk_decay = jnp.exp(total)
    lane_head = lax.broadcasted_iota(jnp.int32, (q, SSM_GROUP_W), 1) // SSM_HD
    for g in range(SSM_GROUPS):
        xl = slice(g * SSM_GROUP_W, (g + 1) * SSM_GROUP_W)
        bg = xbc[:, SSM_INNER + g * SSM_STATE:SSM_INNER + (g + 1) * SSM_STATE]
        cg = xbc[:, SSM_INNER + SSM_BC_W + g * SSM_STATE:SSM_INNER + SSM_BC_W + (g + 1) * SSM_STATE]
        cb = lax.dot_general(cg.astype(MXU_DTYPE), bg.astype(MXU_DTYPE), (((1,), (1,)), ((), ())),
                             preferred_element_type=F32)
        yield
        ms = []
        for r in range(SSM_HEADS_PER_GROUP):
            h = g * SSM_HEADS_PER_GROUP + r
            col = cs_e[:, h * SSM_HD:h * SSM_HD + 1]
            row = cs_t[h:h + 1, :]
            ms.append(cb * jnp.where(seen, jnp.exp(col - row), 0.0))
        y_all = jnp.dot(jnp.concatenate(ms, axis=0).astype(MXU_DTYPE), dtx[:, xl].astype(MXU_DTYPE),
                        preferred_element_type=F32)
        yield
        y_diag = jnp.zeros((q, SSM_GROUP_W), F32)
        for r in range(SSM_HEADS_PER_GROUP):
            y_diag = y_diag + jnp.where(lane_head == r, y_all[r * q:(r + 1) * q, :], 0.0)
        s_in = state_scr[g]
        y_off = jnp.dot(cg.astype(MXU_DTYPE), s_in.astype(MXU_DTYPE), preferred_element_type=F32) * grow[:, xl]
        y_ref[0, :, xl] = y_diag + y_off
        yield
        new = jnp.dot(bg.T.astype(MXU_DTYPE), dtx_decayed[:, xl].astype(MXU_DTYPE), preferred_element_type=F32)
        state_scr[g] = s_in * chunk_decay[:, xl] + new


def _ssd_scans(xbc_act, dt_arr, dt_t, a_log, dt_bias, col_dt, dt_w):
    b, l, _ = xbc_act.shape
    assert col_dt % dt_w == 0 and dt_w >= 2 * SSM_HEADS
    q = SSM_CHUNK
    nc = l // q
    head_of_lane = np.arange(SSM_INNER) // SSM_HD
    full = lambda r, c: pl.BlockSpec((r, c), lambda i, j: (0, 0))
    chunk = (lambda j: j, lambda j: nc - 1 - j)
    data_specs, param_specs, params = [], [], []
    for d in range(2):
        a = -jnp.exp(a_log[d].astype(F32))
        bias = dt_bias[d].astype(F32)
        expand = (np.arange(dt_w)[:, None] == d * SSM_HEADS + head_of_lane[None, :]).astype(np.float32)
        data_specs += [
            pl.BlockSpec((1, q, CONV_CH), lambda i, j, d=d: (i, chunk[d](j), 0)),
            pl.BlockSpec((1, q, dt_w), lambda i, j, d=d: (i, chunk[d](j), col_dt // dt_w)),
            pl.BlockSpec((1, SSM_HEADS, q), lambda i, j, d=d: (i * 2 + d, 0, chunk[d](j))),
        ]
        param_specs += [full(dt_w, SSM_INNER), full(1, SSM_INNER), full(1, SSM_INNER),
                        full(SSM_HEADS, 1), full(SSM_HEADS, 1)]
        params += [jnp.asarray(expand), bias[head_of_lane].reshape(1, SSM_INNER),
                   a[head_of_lane].reshape(1, SSM_INNER), bias.reshape(SSM_HEADS, 1), a.reshape(SSM_HEADS, 1)]
    y_shape = jax.ShapeDtypeStruct((b, l, SSM_INNER), F32)
    return pl.pallas_call(
        _ssd_kernel,
        grid=(b, nc),
        in_specs=data_specs + param_specs,
        out_specs=[pl.BlockSpec((1, q, SSM_INNER), lambda i, j, d=d: (i, chunk[d](j), 0)) for d in range(2)],
        out_shape=[y_shape, y_shape],
        scratch_shapes=[pltpu.VMEM((2, SSM_GROUPS, SSM_STATE, SSM_GROUP_W), F32)],
        compiler_params=pltpu.CompilerParams(
            dimension_semantics=("arbitrary", "arbitrary"), vmem_limit_bytes=VMEM_LIMIT_BYTES),
        name="ssd_scans",
    )(xbc_act, dt_arr, dt_t, xbc_act, dt_arr, dt_t, *params)


def _ssm_gate_kernel(yf_ref, yb_ref, xbc_ref, z_ref, d_ref, g_ref, out_ref):
    z = z_ref[...]
    y = (yf_ref[...] + yb_ref[...] + xbc_ref[...] * d_ref[...]) * (z * jax.nn.sigmoid(z))
    out_ref[...] = y * lax.rsqrt(jnp.mean(y * y, axis=-1, keepdims=True) + EPS) * g_ref[...]


def mamba2_mixer(arr, conv_w, conv_b, A_log, dt_bias, D_skip, norm_g,
                 col_z=0, col_xbc=CONV_CH, col_dt=SSM_INNER + CONV_CH, dt_w=2 * SSM_HEADS):
    b, l, width = arr.shape
    assert col_z % SSM_INNER == 0
    xbc_act = _ssm_conv(arr, conv_w, conv_b, col_xbc)
    dt_t = jnp.swapaxes(arr[:, :, col_dt:col_dt + 2 * SSM_HEADS], 1, 2).reshape(b * 2, SSM_HEADS, l)
    y_f, y_b = _ssd_scans(xbc_act, arr, dt_t, A_log, dt_bias, col_dt, dt_w)
    tm = 1024
    tok = lambda cb: pl.BlockSpec((tm, SSM_INNER), lambda i: (i, cb))
    row = pl.BlockSpec((1, SSM_INNER), lambda i: (0, 0))
    d_e = D_skip.astype(F32)[np.arange(SSM_INNER) // SSM_HD].reshape(1, SSM_INNER)
    out = pl.pallas_call(
        _ssm_gate_kernel,
        grid=(b * l // tm,),
        in_specs=[tok(0), tok(0), tok(0), tok(col_z // SSM_INNER), row, row],
        out_specs=tok(0),
        out_shape=jax.ShapeDtypeStruct((b * l, SSM_INNER), F32),
        compiler_params=pltpu.CompilerParams(
            dimension_semantics=("arbitrary",), vmem_limit_bytes=VMEM_LIMIT_BYTES),
        name="ssm_gate",
    )(y_f.reshape(b * l, SSM_INNER), y_b.reshape(b * l, SSM_INNER), xbc_act.reshape(b * l, CONV_CH),
      arr.reshape(b * l, width), d_e, norm_g.reshape(1, SSM_INNER))
    return out.reshape(b, l, SSM_INNER)


NA_DIM = NA_HEADS * NA_HD
NA_ROWS_PER_STEP = 8
NA_WIN = NA_ROWS * GRID_W
NA_STEP_TOK = NA_ROWS_PER_STEP * GRID_W


def _na_bias_table(rpb):
    n_dc = 2 * NA_COLS - 1
    edge_l = jnp.repeat(rpb[:, :, :1], GRID_W, axis=2)
    edge_r = jnp.repeat(rpb[:, :, -1:], GRID_W, axis=2)
    ext = jnp.concatenate([edge_l, rpb.astype(F32), edge_r], axis=2)
    by_col = jnp.stack([ext[:, :, GRID_W + NA_COLS - 1 - qc:2 * GRID_W + NA_COLS - 1 - qc]
                        for qc in range(GRID_W)], axis=2)
    qc = np.arange(GRID_W)[:, None]
    kc = np.arange(GRID_W)[None, :]
    cs = np.clip(qc - NA_COLS // 2, 0, GRID_W - NA_COLS)
    ok = (kc >= cs) & (kc < cs + NA_COLS)
    by_col = jnp.where(ok[None, None], by_col, -jnp.inf)
    tabs = []
    for delta in range(NA_ROWS):
        rows = by_col[:, NA_ROWS - 1 - delta:2 * NA_ROWS - 1 - delta]
        tabs.append(rows.transpose(0, 2, 1, 3).reshape(NA_HEADS * GRID_W, NA_WIN))
    assert n_dc == rpb.shape[2]
    return jnp.stack(tabs, axis=0)


def _na_kernel(prev_ref, cur_ref, next_ref, tab_ref, out_ref, k_scr, v_scr, *, n_rows):
    step = pl.program_id(1)
    for i, ref in enumerate((prev_ref, cur_ref, next_ref)):
        k_scr[i * NA_STEP_TOK:(i + 1) * NA_STEP_TOK, :] = ref[0, :, NA_DIM:2 * NA_DIM].astype(MXU_DTYPE)
        v_scr[i * NA_STEP_TOK:(i + 1) * NA_STEP_TOK, :] = ref[0, :, 2 * NA_DIM:3 * NA_DIM].astype(MXU_DTYPE)
    lane_head = lax.broadcasted_iota(jnp.int32, (GRID_W, NA_DIM), 1) // NA_HD
    row0 = step * NA_ROWS_PER_STEP
    for j in range(NA_ROWS_PER_STEP):
        r = row0 + j
        r0 = jnp.clip(r - NA_ROWS // 2, 0, n_rows - NA_ROWS)
        start = pl.multiple_of((r0 - row0 + NA_ROWS_PER_STEP) * GRID_W, GRID_W)
        q = cur_ref[0, j * GRID_W:(j + 1) * GRID_W, 0:NA_DIM] * np.float32(NA_HD ** -0.5)
        qs = jnp.concatenate([jnp.where(lane_head == h, q, 0.0) for h in range(NA_HEADS)], axis=0)
        kw = k_scr[pl.ds(start, NA_WIN), :]
        vw = v_scr[pl.ds(start, NA_WIN), :]
        logits = lax.dot_general(qs.astype(MXU_DTYPE), kw, (((1,), (1,)), ((), ())),
                                 preferred_element_type=F32) + tab_ref[r - r0]
        m = jnp.max(logits, axis=-1, keepdims=True)
        p = jnp.exp(logits - m)
        denom = jnp.sum(p, axis=-1, keepdims=True)
        o_all = jnp.dot(p.astype(MXU_DTYPE), vw, preferred_element_type=F32) / denom
        o = jnp.zeros((GRID_W, NA_DIM), F32)
        for h in range(NA_HEADS):
            o = o + jnp.where(lane_head == h, o_all[h * GRID_W:(h + 1) * GRID_W, :], 0.0)
        out_ref[0, j * GRID_W:(j + 1) * GRID_W, :] = o


def na_mixer(qkv, rpb, col=0):
    b, s, _ = qkv.shape
    n_rows = s // GRID_W
    assert n_rows >= NA_ROWS and n_rows % NA_ROWS_PER_STEP == 0 and col % (3 * NA_DIM) == 0
    n_steps = n_rows // NA_ROWS_PER_STEP
    blk = (1, NA_STEP_TOK, 3 * NA_DIM)
    cb = col // (3 * NA_DIM)
    return pl.pallas_call(
        functools.partial(_na_kernel, n_rows=n_rows),
        grid=(b, n_steps),
        in_specs=[
            pl.BlockSpec(blk, lambda i, j: (i, jnp.maximum(j - 1, 0), cb)),
            pl.BlockSpec(blk, lambda i, j: (i, j, cb)),
            pl.BlockSpec(blk, lambda i, j: (i, jnp.minimum(j + 1, n_steps - 1), cb)),
            pl.BlockSpec((NA_ROWS, NA_HEADS * GRID_W, NA_WIN), lambda i, j: (0, 0, 0)),
        ],
        out_specs=pl.BlockSpec((1, NA_STEP_TOK, NA_DIM), lambda i, j: (i, j, 0)),
        out_shape=jax.ShapeDtypeStruct((b, s, NA_DIM), F32),
        scratch_shapes=[
            pltpu.VMEM((3 * NA_STEP_TOK, NA_DIM), MXU_DTYPE),
            pltpu.VMEM((3 * NA_STEP_TOK, NA_DIM), MXU_DTYPE),
        ],
        compiler_params=pltpu.CompilerParams(
            dimension_semantics=("arbitrary", "arbitrary"), vmem_limit_bytes=VMEM_LIMIT_BYTES),
        name="na_attention",
    )(qkv, qkv, qkv, _na_bias_table(rpb))


PK_GATE = 0
PK_XBC = PK_GATE + N_BRANCH * D_MODEL
PK_CQ = PK_XBC + CONV_CH
PK_DIL = PK_CQ + MLA_Q_RANK
PK_NA = PK_DIL + DIL_N_GROUPS * 3 * DIL_DIM
PK_KR = PK_NA + 3 * NA_DIM
PK_Z = PK_KR + 2 * MLA_HG
PK_CKV = PK_Z + SSM_INNER
PK_DT = PK_CKV + MLA_KV_RANK
PK_DT_W = LANES
PK_WIDTH = 13 * 3 * DIL_DIM
INPROJ_TM = 512
INPROJ_TN = PK_WIDTH // 2
MERGE_TM = 512


def _pack_w_in(w_in_l):
    gate, a_cq, a_ckv, a_kr, b_qkv, c_z, c_xbc, c_dt, d_qkv = jnp.split(w_in_l, IN_SPLITS, axis=-1)
    zeros = lambda n: jnp.zeros((D_MODEL, n), w_in_l.dtype)
    cols = [gate, c_xbc, a_cq, b_qkv, d_qkv, mla_rotary_key_columns(a_kr), c_z, a_ckv,
            c_dt, zeros(PK_DT_W - 2 * SSM_HEADS)]
    packed = jnp.concatenate(cols, axis=-1)
    assert packed.shape[1] == PK_DT + PK_DT_W
    return jnp.concatenate([packed, zeros(PK_WIDTH - packed.shape[1])], axis=-1).astype(MXU_DTYPE)


def _inproj_kernel(x_ref, g_ref, w_ref, out_ref, h_scr):
    @pl.when(pl.program_id(1) == 0)
    def _():
        x = x_ref[...]
        h_scr[...] = (x * lax.rsqrt(jnp.mean(x * x, axis=-1, keepdims=True) + EPS) * g_ref[...]).astype(MXU_DTYPE)

    out_ref[...] = jnp.dot(h_scr[...], w_ref[...], preferred_element_type=F32)


def _inproj(x2d, norm_g, w_packed):
    n_tok = x2d.shape[0]
    tm, tn = INPROJ_TM, INPROJ_TN
    return pl.pallas_call(
        _inproj_kernel,
        grid=(n_tok // tm, PK_WIDTH // tn),
        in_specs=[pl.BlockSpec((tm, D_MODEL), lambda i, j: (i, 0)),
                  pl.BlockSpec((1, D_MODEL), lambda i, j: (0, 0)),
                  pl.BlockSpec((D_MODEL, tn), lambda i, j: (0, j))],
        out_specs=pl.BlockSpec((tm, tn), lambda i, j: (i, j)),
        out_shape=jax.ShapeDtypeStruct((n_tok, PK_WIDTH), F32),
        scratch_shapes=[pltpu.VMEM((tm, D_MODEL), MXU_DTYPE)],
        compiler_params=pltpu.CompilerParams(
            dimension_semantics=("arbitrary", "arbitrary"), vmem_limit_bytes=VMEM_LIMIT_BYTES),
        name="in_projection",
    )(x2d, norm_g.reshape(1, D_MODEL), w_packed)


def _merge_kernel(x_ref, gate_ref, bg_ref, ya_ref, yb_ref, yc_ref, yd_ref, wb_ref, wo_ref, out_ref):
    merged = jnp.zeros(x_ref.shape, F32)
    for i, y_ref in enumerate((ya_ref, yb_ref, yc_ref, yd_ref)):
        proj = jnp.dot(y_ref[...].astype(MXU_DTYPE), wb_ref[BRANCH_ROWS[i]:BRANCH_ROWS[i + 1], :],
                       preferred_element_type=F32)
        lanes = slice(i * D_MODEL, (i + 1) * D_MODEL)
        merged = merged + jax.nn.sigmoid(gate_ref[:, lanes] + bg_ref[:, lanes]) * proj
    out_ref[...] = x_ref[...] + jnp.dot(merged.astype(MXU_DTYPE), wo_ref[...], preferred_element_type=F32)


def _merge(x2d, packed, b_gate, ys, w_branch, w_out):
    n_tok = x2d.shape[0]
    tm = MERGE_TM
    tok = lambda w: pl.BlockSpec((tm, w), lambda i: (i, 0))
    full = lambda r, c: pl.BlockSpec((r, c), lambda i: (0, 0))
    n_gate = N_BRANCH * D_MODEL
    return pl.pallas_call(
        _merge_kernel,
        grid=(n_tok // tm,),
        in_specs=[tok(D_MODEL), tok(n_gate), full(1, n_gate)] + [tok(w) for w in BRANCH_WIDTHS]
                 + [full(BRANCH_ROWS[-1], D_MODEL), full(D_MODEL, D_MODEL)],
        out_specs=tok(D_MODEL),
        out_shape=jax.ShapeDtypeStruct((n_tok, D_MODEL), F32),
        compiler_params=pltpu.CompilerParams(
            dimension_semantics=("arbitrary",), vmem_limit_bytes=VMEM_LIMIT_BYTES),
        name="branch_merge",
    )(x2d, packed, b_gate.reshape(1, n_gate), *[y.reshape(n_tok, -1) for y in ys], w_branch, w_out)


def encoder(x, norm1_g, w_in_packed, b_gate, mla_q_norm, mla_w_qb, mla_kv_norm, mla_w_kvb, t5_table,
            ssm_conv_w, ssm_conv_b, ssm_A_log, ssm_dt_bias, ssm_D, ssm_norm_g, na_rpb,
            w_branch, w_out, norm2_g, peer_wq, peer_keys, peer_u, peer_vt, final_g):
    b, s, _ = x.shape
    x2d = x.reshape(b * s, D_MODEL)
    for l in range(DEPTH):
        packed = _inproj(x2d, norm1_g[l], w_in_packed[l])
        p3 = packed.reshape(b, s, PK_WIDTH)
        y_a = mla_mixer(p3, mla_q_norm[l], mla_w_qb[l], mla_kv_norm[l], mla_w_kvb[l],
                        col_q=PK_CQ, col_kv=PK_CKV, col_kr=PK_KR)
        y_b = dilated_mixer(p3, t5_table, col=PK_DIL)
        y_c = mamba2_mixer(p3, ssm_conv_w[l], ssm_conv_b[l], ssm_A_log[l], ssm_dt_bias[l], ssm_D[l],
                           ssm_norm_g[l], col_z=PK_Z, col_xbc=PK_XBC, col_dt=PK_DT, dt_w=PK_DT_W)
        y_d = na_mixer(p3, na_rpb[l], col=PK_NA)
        x2d = _merge(x2d, packed, b_gate[l], (y_a, y_b, y_c, y_d), w_branch[l], w_out[l])
        x2d = peer_block(x2d, norm2_g[l], peer_wq[l], peer_keys[l], peer_u[l], peer_vt[l],
                         final_g, final_norm=(l == DEPTH - 1))
    return x2d.reshape(b, s, D_MODEL)


def kernel(x_prompt, x_sample, norm1_g, w_in, b_gate, mla_q_norm, mla_w_qb, mla_kv_norm, mla_w_kvb, t5_table, ssm_conv_w, ssm_conv_b, ssm_A_log, ssm_dt_bias, ssm_D, ssm_norm_g, na_rpb, w_branch, w_out, norm2_g, peer_wq, peer_keys, peer_u, peer_v, final_g):
    peer_u16 = peer_u.astype(MXU_DTYPE)
    peer_vt16 = jnp.swapaxes(peer_v, 1, 2).astype(MXU_DTYPE)
    w_in_packed = jnp.stack([_pack_w_in(w_in[l]) for l in range(DEPTH)])
    shared = (norm1_g, w_in_packed, b_gate, mla_q_norm, mla_w_qb, mla_kv_norm, mla_w_kvb, t5_table,
              ssm_conv_w, ssm_conv_b, ssm_A_log, ssm_dt_bias, ssm_D, ssm_norm_g, na_rpb,
              w_branch.astype(MXU_DTYPE), w_out.astype(MXU_DTYPE), norm2_g, peer_wq, peer_keys,
              peer_u16, peer_vt16, final_g)
    y_prompt = encoder(x_prompt, *shared)
    y_sample = encoder(x_sample, *shared)
    return (y_prompt, y_sample)
```

```python
import functools
import math

import numpy as np
import jax
import jax.numpy as jnp
from jax import lax
from jax.experimental import pallas as pl
from jax.experimental.pallas import tpu as pltpu

F32 = jnp.float32
BF16 = jnp.bfloat16
MXU_DTYPE = BF16

D_MODEL = 1024
DEPTH = 2
GRID_W = 64
EPS = 1e-6
N_BRANCH = 4

MLA_HEADS = 4
MLA_Q_RANK = 256
MLA_KV_RANK = 128
MLA_NOPE = 64
MLA_ROPE = 32
MLA_V = 64
ROPE_THETA = 10000.0
Q_BLOCK = 128

DIL_GROUPS = ((128, 1), (512, 4), (2048, 16))
DIL_HEADS = 4
DIL_HD = 64
T5_BUCKETS = 32
T5_MAX_DIST = 1024

SSM_HEADS = 8
SSM_HD = 64
SSM_INNER = SSM_HEADS * SSM_HD
SSM_GROUPS = 2
SSM_STATE = 128
SSM_CONV = 7
SSM_CHUNK = 128
CONV_CH = SSM_INNER + 2 * SSM_GROUPS * SSM_STATE

NA_HEADS = 4
NA_HD = 64
NA_ROWS = 8
NA_COLS = 16
NA_QCB = 16
NA_KCB = NA_QCB + NA_COLS

PEER_HEADS = 8
PEER_KEYS = 128
PEER_EXPERTS = PEER_KEYS * PEER_KEYS
PEER_QDIM = 256
PEER_TOPK = 16
PEER_TOK_BLOCK = 128

BRANCH_WIDTHS = (MLA_HEADS * MLA_V, DIL_HEADS * DIL_HD, SSM_INNER, NA_HEADS * NA_HD)
BRANCH_ROWS = tuple(sum(BRANCH_WIDTHS[:i]) for i in range(N_BRANCH + 1))
IN_SIZES = (N_BRANCH * D_MODEL, MLA_Q_RANK, MLA_KV_RANK, MLA_ROPE,
            len(DIL_GROUPS) * 3 * DIL_HEADS * DIL_HD,
            SSM_INNER, CONV_CH, 2 * SSM_HEADS,
            3 * NA_HEADS * NA_HD)
IN_SPLITS = tuple(sum(IN_SIZES[:i + 1]) for i in range(len(IN_SIZES) - 1))

VMEM_LIMIT_BYTES = 56 * 1024 * 1024
LANES = 128
SUBLANES = 8


def rms_norm(x, g):
    x32 = x.astype(F32)
    y = x32 * lax.rsqrt(jnp.mean(x32 * x32, axis=-1, keepdims=True) + EPS)
    return (y * g.astype(F32)).astype(x.dtype)


PEER_ROUTE_TB = 512
PEER_TB = 512
PEER_EC = 1024
PEER_GATE_DTYPE = BF16
PEER_GATE_ROWS = 16
PEER_HALF = PEER_QDIM // 2
PEER_CAND_ROWS = 2 * SUBLANES + 7 * SUBLANES + SUBLANES


def _gelu_exact_x2(x):
    return x * (1.0 + lax.erf(x * np.float32(math.sqrt(0.5))))


def _extract_desc(vals, n_out, out_ref, row0, with_rank=False):
    rank = jnp.full(vals.shape, float(n_out), F32) if with_rank else None
    for k in range(n_out):
        m = jnp.max(vals, axis=0, keepdims=True)
        out_ref[pl.ds(row0 + k, 1), :] = m
        hit = vals == m
        if with_rank:
            rank = jnp.where(hit, float(k), rank)
        vals = jnp.where(hit, -jnp.inf, vals)
    return rank


def _peer_route_kernel(x_ref, g_ref, wqt_ref, keys_ref,
                       xnt_ref, cnt_ref, rank_ref, e1_ref, e2_ref,
                       qt_scr, top_scr, cand_scr, tops_scr):
    x = x_ref[...]
    xn = x * lax.rsqrt(jnp.mean(x * x, axis=-1, keepdims=True) + EPS) * g_ref[...]
    xnt = xn.T.astype(MXU_DTYPE)
    xnt_ref[...] = xnt
    qt_scr[...] = jnp.dot(wqt_ref[...], xnt, preferred_element_type=F32).astype(MXU_DTYPE)

    def head(h, carry):
        q1 = qt_scr[pl.ds(pl.multiple_of(h * PEER_QDIM, PEER_QDIM), PEER_HALF), :]
        q2 = qt_scr[pl.ds(pl.multiple_of(h * PEER_QDIM + PEER_HALF, PEER_HALF), PEER_HALF), :]
        s1 = jnp.dot(keys_ref[2 * h], q1, preferred_element_type=F32)
        s2 = jnp.dot(keys_ref[2 * h + 1], q2, preferred_element_type=F32)
        _extract_desc(s1, PEER_TOPK, top_scr, 0)
        rank2 = _extract_desc(s2, PEER_TOPK, top_scr, PEER_TOPK, with_rank=True)
        t1 = top_scr[0:PEER_TOPK, :]
        t2 = top_scr[PEER_TOPK:2 * PEER_TOPK, :]
        cand_scr[0:2 * SUBLANES, :] = t1[0:1, :] + t2
        for a in range(1, SUBLANES):
            cand_scr[(a + 1) * SUBLANES:(a + 2) * SUBLANES, :] = t1[a:a + 1, :] + t2[0:SUBLANES, :]
        cand_scr[9 * SUBLANES:10 * SUBLANES, :] = t1[SUBLANES:2 * SUBLANES, :] + t2[0:1, :]
        _extract_desc(cand_scr[...], PEER_TOPK, tops_scr, 0)
        top_s = tops_scr[...]
        z = jnp.sum(jnp.exp(top_s - top_s[0:1, :]), axis=0, keepdims=True)
        tau = top_s[PEER_TOPK - 1:PEER_TOPK, :]
        cnt_top = jnp.zeros(t1.shape, F32)
        for b in range(PEER_TOPK):
            cnt_top = cnt_top + jnp.where(t1 + t2[b:b + 1, :] >= tau, 1.0, 0.0)
        cnt = jnp.zeros(s1.shape, F32)
        for a in range(PEER_TOPK):
            cnt = jnp.where(s1 == t1[a:a + 1, :], cnt_top[a:a + 1, :], cnt)
        cnt_ref[h] = cnt
        rank_ref[h] = rank2.astype(rank_ref.dtype)
        e1_ref[h] = jnp.exp(s1 - t1[0:1, :]) * (0.5 / z)
        e2_ref[h] = jnp.exp(s2 - t2[0:1, :]).astype(e2_ref.dtype)
        return carry

    lax.fori_loop(0, PEER_HEADS, head, 0)


def _peer_route(x2d, g, wqt, keys):
    n_tok = x2d.shape[0]
    tb = PEER_ROUTE_TB
    rt_shape = jax.ShapeDtypeStruct((PEER_HEADS, PEER_KEYS, n_tok), F32)
    gate_shape = jax.ShapeDtypeStruct((PEER_HEADS, PEER_KEYS, n_tok), PEER_GATE_DTYPE)
    rt_spec = pl.BlockSpec((PEER_HEADS, PEER_KEYS, tb), lambda i: (0, 0, i))
    return pl.pallas_call(
        _peer_route_kernel,
        grid=(n_tok // tb,),
        in_specs=[
            pl.BlockSpec((tb, D_MODEL), lambda i: (i, 0)),
            pl.BlockSpec((1, D_MODEL), lambda i: (0, 0)),
            pl.BlockSpec((PEER_HEADS * PEER_QDIM, D_MODEL), lambda i: (0, 0)),
            pl.BlockSpec((2 * PEER_HEADS, PEER_KEYS, PEER_HALF), lambda i: (0, 0, 0)),
        ],
        out_specs=[
            pl.BlockSpec((D_MODEL, tb), lambda i: (0, i)),
            rt_spec, rt_spec, rt_spec, rt_spec,
        ],
        out_shape=[
            jax.ShapeDtypeStruct((D_MODEL, n_tok), MXU_DTYPE),
            rt_shape, gate_shape, rt_shape, gate_shape,
        ],
        scratch_shapes=[
            pltpu.VMEM((PEER_HEADS * PEER_QDIM, tb), MXU_DTYPE),
            pltpu.VMEM((2 * PEER_TOPK, tb), F32),
            pltpu.VMEM((PEER_CAND_ROWS, tb), F32),
            pltpu.VMEM((PEER_TOPK, tb), F32),
        ],
        compiler_params=pltpu.CompilerParams(
            dimension_semantics=("arbitrary",), vmem_limit_bytes=VMEM_LIMIT_BYTES),
        name="peer_route",
    )(x2d, g, wqt, keys)


def _peer_expert_kernel(x_ref, xnt_ref, cnt_ref, rank_ref, e1_ref, e2_ref, u_ref, vt_ref, fg_ref,
                        out_ref, acc_scr, w_scr, *, final_norm):
    c = pl.program_id(1)
    n_chunks = pl.num_programs(1) - 1
    n_i1 = PEER_EC // PEER_KEYS
    tb = acc_scr.shape[1]
    gdt = PEER_GATE_DTYPE

    @pl.when(c == 0)
    def _():
        acc_scr[...] = jnp.zeros_like(acc_scr)
        w_scr[1] = jnp.zeros(w_scr.shape[1:], w_scr.dtype)

    slot = c % 2
    n_groups = PEER_KEYS // PEER_GATE_ROWS
    acc_scr[...] += jnp.dot(vt_ref[...], w_scr[(c + 1) % 2], preferred_element_type=F32)
    hid = jnp.dot(u_ref[...], xnt_ref[...], preferred_element_type=F32)

    def gate_block(i1l, l0):
        lanes = slice(l0, l0 + LANES)
        gates = [jnp.zeros((PEER_GATE_ROWS, LANES), gdt) for _ in range(n_groups)]
        for h in range(PEER_HEADS):
            cntb = jnp.broadcast_to(cnt_ref[h, i1l:i1l + 1, lanes], (PEER_GATE_ROWS, LANES)).astype(gdt)
            e1b = jnp.broadcast_to(e1_ref[h, i1l:i1l + 1, lanes], (PEER_GATE_ROWS, LANES)).astype(gdt)
            for k in range(n_groups):
                rows = slice(k * PEER_GATE_ROWS, (k + 1) * PEER_GATE_ROWS)
                w = e2_ref[h, rows, lanes] * e1b
                gates[k] = gates[k] + jnp.where(rank_ref[h, rows, lanes] < cntb, w, jnp.zeros_like(w))
        for k in range(n_groups):
            r0 = k * PEER_GATE_ROWS
            e0 = i1l * PEER_KEYS + r0
            act = _gelu_exact_x2(hid[e0:e0 + PEER_GATE_ROWS, lanes]).astype(gdt)
            w_scr[slot, e0:e0 + PEER_GATE_ROWS, lanes] = (act * gates[k]).astype(MXU_DTYPE)

    for i1l in range(n_i1):
        for l0 in range(0, tb, LANES):
            gate_block(i1l, l0)

    @pl.when(c == n_chunks)
    def _():
        y = x_ref[...] + acc_scr[...].T
        if final_norm:
            y = y * lax.rsqrt(jnp.mean(y * y, axis=-1, keepdims=True) + EPS) * fg_ref[...]
        out_ref[...] = y


def _peer_experts(x2d, xnt, cnt, rank2, e1, e2, u, vt, final_g, final_norm):
    n_tok = x2d.shape[0]
    tb, ec = PEER_TB, PEER_EC
    n_chunks = PEER_EXPERTS // ec
    rt_spec = pl.BlockSpec((PEER_HEADS, PEER_KEYS, tb), lambda j, c: (0, 0, j))
    row_spec = pl.BlockSpec((PEER_HEADS, ec // PEER_KEYS, tb),
                            lambda j, c: (0, jnp.minimum(c, n_chunks - 1), j))
    return pl.pallas_call(
        functools.partial(_peer_expert_kernel, final_norm=final_norm),
        grid=(n_tok // tb, n_chunks + 1),
        in_specs=[
            pl.BlockSpec((tb, D_MODEL), lambda j, c: (j, 0)),
            pl.BlockSpec((D_MODEL, tb), lambda j, c: (0, j)),
            row_spec, rt_spec, row_spec, rt_spec,
            pl.BlockSpec((ec, D_MODEL), lambda j, c: (jnp.minimum(c, n_chunks - 1), 0)),
            pl.BlockSpec((D_MODEL, ec), lambda j, c: (0, jnp.maximum(c - 1, 0))),
            pl.BlockSpec((1, D_MODEL), lambda j, c: (0, 0)),
        ],
        out_specs=pl.BlockSpec((tb, D_MODEL), lambda j, c: (j, 0)),
        out_shape=jax.ShapeDtypeStruct((n_tok, D_MODEL), F32),
        scratch_shapes=[
            pltpu.VMEM((D_MODEL, tb), F32),
            pltpu.VMEM((2, ec, tb), MXU_DTYPE),
        ],
        compiler_params=pltpu.CompilerParams(
            dimension_semantics=("arbitrary", "arbitrary"), vmem_limit_bytes=VMEM_LIMIT_BYTES),
        name="peer_experts",
    )(x2d, xnt, cnt, rank2, e1, e2, u, vt, final_g.reshape(1, D_MODEL))


def peer_block(x2d, norm_g, w_q, keys, u, v, final_g, final_norm=False):
    wqt = w_q.T.astype(MXU_DTYPE)
    keys2 = keys.reshape(2 * PEER_HEADS, PEER_KEYS, PEER_HALF).astype(MXU_DTYPE)
    xnt, cnt, rank2, e1, e2 = _peer_route(x2d, norm_g.reshape(1, D_MODEL), wqt, keys2)
    return _peer_experts(x2d, xnt, cnt, rank2, e1, e2, u, v, final_g, final_norm)


MLA_HG = LANES
MLA_QK_W = MLA_HEADS * MLA_HG
MLA_V_W = MLA_HEADS * MLA_V
MLA_PREP_TOK = 512
MLA_TQ = 512
MLA_TK = 2048
MLA_RHALF = MLA_ROPE // 2


def _mla_rope_tables(s):
    inv = ROPE_THETA ** (-jnp.arange(MLA_RHALF, dtype=F32) / MLA_RHALF)
    ang = jnp.arange(s).astype(F32)[:, None] * inv[None, :]
    cos, sin = jnp.cos(ang), jnp.sin(ang)
    zero_pad = jnp.zeros((s, MLA_HG - MLA_NOPE - MLA_ROPE), F32)
    cos_rot = jnp.concatenate([cos, cos, zero_pad], axis=1)
    sin_rot = jnp.concatenate([-sin, sin, zero_pad], axis=1)
    scale = np.float32((MLA_NOPE + MLA_ROPE) ** -0.5)
    q_cos = scale * jnp.concatenate([jnp.ones((s, MLA_NOPE), F32), cos_rot], axis=1)
    q_sin = scale * jnp.concatenate([jnp.zeros((s, MLA_NOPE), F32), sin_rot], axis=1)
    k_cos = jnp.concatenate([jnp.zeros((s, MLA_NOPE), F32), cos_rot], axis=1)
    k_sin = jnp.concatenate([jnp.zeros((s, MLA_NOPE), F32), sin_rot], axis=1)
    return q_cos, q_sin, k_cos, k_sin


def _mla_pack_weights(w_qb, w_kvb):
    hd_q = MLA_NOPE + MLA_ROPE
    wq = w_qb.reshape(MLA_Q_RANK, MLA_HEADS, hd_q)
    rot = wq[:, :, MLA_NOPE:]
    rot_sw = jnp.concatenate([rot[:, :, MLA_RHALF:], rot[:, :, :MLA_RHALF]], axis=2)
    pad = jnp.zeros((MLA_Q_RANK, MLA_HEADS, MLA_HG - hd_q), F32)
    wq_a = jnp.concatenate([wq, pad], axis=2).reshape(MLA_Q_RANK, MLA_QK_W)
    wq_b = jnp.concatenate([jnp.zeros_like(wq[:, :, :MLA_NOPE]), rot_sw, pad], axis=2).reshape(MLA_Q_RANK, MLA_QK_W)
    wkv = w_kvb.reshape(MLA_KV_RANK, MLA_HEADS, MLA_NOPE + MLA_V)
    wk = jnp.concatenate([wkv[:, :, :MLA_NOPE], jnp.zeros((MLA_KV_RANK, MLA_HEADS, MLA_HG - MLA_NOPE), F32)],
                         axis=2).reshape(MLA_KV_RANK, MLA_QK_W)
    wv = jnp.concatenate([wkv[:, :, MLA_NOPE:], jnp.zeros((MLA_KV_RANK, MLA_HEADS, MLA_HG - MLA_V), F32)],
                         axis=2).reshape(MLA_KV_RANK, MLA_QK_W)
    return (wq_a.astype(MXU_DTYPE), wq_b.astype(MXU_DTYPE), wk.astype(MXU_DTYPE), wv.astype(MXU_DTYPE))


def _mla_prep_kernel(cq_ref, ckv_ref, kr_ref, qn_ref, kvn_ref, wqa_ref, wqb_ref, wk_ref, wv_ref,
                     qcos_ref, qsin_ref, kcos_ref, ksin_ref, q_out, kt_out, v_out):
    cq = cq_ref[0]
    cqn = (cq * lax.rsqrt(jnp.mean(cq * cq, axis=-1, keepdims=True) + EPS) * qn_ref[...]).astype(MXU_DTYPE)
    qa = jnp.dot(cqn, wqa_ref[...], preferred_element_type=F32)
    qb = jnp.dot(cqn, wqb_ref[...], preferred_element_type=F32)
    ckv = ckv_ref[0]
    ckvn = (ckv * lax.rsqrt(jnp.mean(ckv * ckv, axis=-1, keepdims=True) + EPS) * kvn_ref[...]).astype(MXU_DTYPE)
    ka = jnp.dot(ckvn, wk_ref[...], preferred_element_type=F32)
    one_lane = (lax.broadcasted_iota(jnp.int32, (1, MLA_QK_W), 1) % MLA_HG == MLA_V).astype(F32)
    v_out[0] = (jnp.dot(ckvn, wv_ref[...], preferred_element_type=F32) + one_lane).astype(MXU_DTYPE)
    kr = kr_ref[0]
    k_rot = kr[:, 0:MLA_HG] * kcos_ref[...] + kr[:, MLA_HG:2 * MLA_HG] * ksin_ref[...]
    for h in range(MLA_HEADS):
        lanes = slice(h * MLA_HG, (h + 1) * MLA_HG)
        q_out[0, :, lanes] = (qa[:, lanes] * qcos_ref[...] + qb[:, lanes] * qsin_ref[...]).astype(MXU_DTYPE)
        kt_out[0, lanes, :] = (ka[:, lanes] + k_rot).T.astype(MXU_DTYPE)


def _mla_flash_kernel(q_ref, kt_ref, v_ref, out_ref, m_scr, acc_scr):
    ki = pl.program_id(2)

    @pl.when(ki == 0)
    def _():
        m_scr[...] = jnp.full(m_scr.shape, -jnp.inf, F32)
        acc_scr[...] = jnp.zeros(acc_scr.shape, F32)

    def logits(h):
        lanes = slice(h * MLA_HG, (h + 1) * MLA_HG)
        return jnp.dot(q_ref[0, :, lanes], kt_ref[0, lanes, :], preferred_element_type=F32)

    n_rep = kt_ref.shape[2] // MLA_HG
    s_next = logits(0)
    for h in range(MLA_HEADS):
        s = s_next
        if h + 1 < MLA_HEADS:
            s_next = logits(h + 1)
        m_old = m_scr[h]
        m_new = jnp.maximum(m_old, jnp.max(s, axis=-1, keepdims=True))
        p = jnp.exp(s - jnp.tile(m_new, (1, n_rep))).astype(MXU_DTYPE)
        acc_scr[h] = jnp.exp(m_old - m_new) * acc_scr[h] + jnp.dot(
            p, v_ref[0, :, h * MLA_HG:(h + 1) * MLA_HG], preferred_element_type=F32)
        m_scr[h] = m_new

    @pl.when(ki == pl.num_programs(2) - 1)
    def _():
        low = lax.broadcasted_iota(jnp.int32, (acc_scr.shape[1], MLA_HG), 1) < MLA_V
        outs = []
        for h in range(MLA_HEADS):
            acc = acc_scr[h]
            outs.append(acc / acc[:, MLA_V:MLA_V + 1])
        for hp in range(MLA_HEADS // 2):
            odd = pltpu.roll(outs[2 * hp + 1], MLA_V, axis=1)
            out_ref[0, :, hp * MLA_HG:(hp + 1) * MLA_HG] = jnp.where(low, outs[2 * hp], odd)


def mla_rotary_key_columns(w):
    zl = jnp.zeros(w.shape[:-1] + (MLA_NOPE,), w.dtype)
    zr = jnp.zeros(w.shape[:-1] + (MLA_HG - MLA_NOPE - MLA_ROPE,), w.dtype)
    w_sw = jnp.concatenate([w[..., MLA_RHALF:], w[..., :MLA_RHALF]], axis=-1)
    return jnp.concatenate([zl, w, zr, zl, w_sw, zr], axis=-1)


def mla_mixer(arr, q_norm, w_qb, kv_norm, w_kvb, col_q=0, col_kv=MLA_Q_RANK, col_kr=MLA_Q_RANK + MLA_KV_RANK):
    b, s, _ = arr.shape
    tt = MLA_PREP_TOK
    assert col_q % MLA_Q_RANK == 0 and col_kv % MLA_KV_RANK == 0 and col_kr % (2 * MLA_HG) == 0
    wqa, wqb, wk, wv = _mla_pack_weights(w_qb, w_kvb)
    q_cos, q_sin, k_cos, k_sin = _mla_rope_tables(s)
    tok = lambda w, c=0: pl.BlockSpec((1, tt, w), lambda i, j: (i, j, c // w))
    full = lambda r, c: pl.BlockSpec((r, c), lambda i, j: (0, 0))
    tab = pl.BlockSpec((tt, MLA_HG), lambda i, j: (j, 0))
    q, kt, v = pl.pallas_call(
        _mla_prep_kernel,
        grid=(b, s // tt),
        in_specs=[tok(MLA_Q_RANK, col_q), tok(MLA_KV_RANK, col_kv), tok(2 * MLA_HG, col_kr),
                  full(1, MLA_Q_RANK), full(1, MLA_KV_RANK),
                  full(MLA_Q_RANK, MLA_QK_W), full(MLA_Q_RANK, MLA_QK_W),
                  full(MLA_KV_RANK, MLA_QK_W), full(MLA_KV_RANK, MLA_QK_W),
                  tab, tab, tab, tab],
        out_specs=[tok(MLA_QK_W), pl.BlockSpec((1, MLA_QK_W, tt), lambda i, j: (i, 0, j)), tok(MLA_QK_W)],
        out_shape=[jax.ShapeDtypeStruct((b, s, MLA_QK_W), MXU_DTYPE),
                   jax.ShapeDtypeStruct((b, MLA_QK_W, s), MXU_DTYPE),
                   jax.ShapeDtypeStruct((b, s, MLA_QK_W), MXU_DTYPE)],
        compiler_params=pltpu.CompilerParams(
            dimension_semantics=("arbitrary", "arbitrary"), vmem_limit_bytes=VMEM_LIMIT_BYTES),
        name="mla_prep",
    )(arr, arr, arr, q_norm.reshape(1, -1), kv_norm.reshape(1, -1), wqa, wqb, wk, wv,
      q_cos, q_sin, k_cos, k_sin)
    tq, tk = min(MLA_TQ, s), min(MLA_TK, s)
    assert s % tq == 0 and s % tk == 0 and s % tt == 0
    return pl.pallas_call(
        _mla_flash_kernel,
        grid=(b, s // tq, s // tk),
        in_specs=[pl.BlockSpec((1, tq, MLA_QK_W), lambda i, j, kk: (i, j, 0)),
                  pl.BlockSpec((1, MLA_QK_W, tk), lambda i, j, kk: (i, 0, kk)),
                  pl.BlockSpec((1, tk, MLA_QK_W), lambda i, j, kk: (i, kk, 0))],
        out_specs=pl.BlockSpec((1, tq, MLA_V_W), lambda i, j, kk: (i, j, 0)),
        out_shape=jax.ShapeDtypeStruct((b, s, MLA_V_W), F32),
        scratch_shapes=[pltpu.VMEM((MLA_HEADS, tq, MLA_HG), F32),
                        pltpu.VMEM((MLA_HEADS, tq, MLA_HG), F32)],
        compiler_params=pltpu.CompilerParams(
            dimension_semantics=("arbitrary", "arbitrary", "arbitrary"), vmem_limit_bytes=VMEM_LIMIT_BYTES),
        name="mla_flash",
    )(q, kt, v)


def t5_bucket(rel):
    nb = T5_BUCKETS // 2
    ret = np.where(rel > 0, nb, 0)
    n = np.abs(rel)
    max_exact = nb // 2
    large = max_exact + (np.log(np.maximum(n, 1) / max_exact) / np.log(T5_MAX_DIST / max_exact)
                         * (nb - max_exact)).astype(np.int64)
    large = np.minimum(large, nb - 1)
    return (ret + np.where(n < max_exact, n, large)).astype(np.int32)


DIL_DIM = DIL_HEADS * DIL_HD
DIL_HALF = 64
DIL_QB = 128
DIL_KW = DIL_QB + 2 * DIL_HALF
DIL_TL = 512
DIL_N_GROUPS = len(DIL_GROUPS)


def _dil_bias_table(t5_table, gi, dil):
    rel = np.arange(DIL_KW)[None, :] - DIL_HALF - np.arange(DIL_QB)[:, None]
    bias = t5_table[:, gi * DIL_HEADS:(gi + 1) * DIL_HEADS][t5_bucket(rel * dil)].astype(F32)
    bias = jnp.where((np.abs(rel) <= DIL_HALF)[:, :, None], bias, -jnp.inf)
    return bias.transpose(2, 0, 1).reshape(DIL_HEADS * DIL_QB, DIL_KW)


def _dil_kernel(prev_ref, cur_ref, next_ref, tab_ref, o_ref, lse_ref, k_scr, v_scr, *, seq_len):
    step = pl.program_id(2)
    for i, ref in enumerate((prev_ref, cur_ref, next_ref)):
        k_scr[i * DIL_TL:(i + 1) * DIL_TL, :] = ref[0, :, DIL_DIM:2 * DIL_DIM].astype(MXU_DTYPE)
        v_scr[i * DIL_TL:(i + 1) * DIL_TL, :] = ref[0, :, 2 * DIL_DIM:3 * DIL_DIM].astype(MXU_DTYPE)
    lane_head = lax.broadcasted_iota(jnp.int32, (DIL_QB, DIL_DIM), 1) // DIL_HD
    key_off = lax.broadcasted_iota(jnp.int32, (1, DIL_KW), 1)
    for n in range(DIL_TL // DIL_QB):
        w0 = DIL_TL + n * DIL_QB - DIL_HALF
        kpos = step * DIL_TL + (n * DIL_QB - DIL_HALF) + key_off
        valid = (kpos >= 0) & (kpos < seq_len)
        q = cur_ref[0, n * DIL_QB:(n + 1) * DIL_QB, 0:DIL_DIM] * np.float32(DIL_HD ** -0.5)
        qs = jnp.concatenate([jnp.where(lane_head == h, q, 0.0) for h in range(DIL_HEADS)], axis=0)
        logits = lax.dot_general(qs.astype(MXU_DTYPE), k_scr[w0:w0 + DIL_KW, :], (((1,), (1,)), ((), ())),
                                 preferred_element_type=F32) + tab_ref[...]
        logits = jnp.where(valid, logits, -jnp.inf)
        m = jnp.max(logits, axis=-1, keepdims=True)
        p = jnp.exp(logits - m)
        denom = jnp.sum(p, axis=-1, keepdims=True)
        o_all = jnp.dot(p.astype(MXU_DTYPE), v_scr[w0:w0 + DIL_KW, :], preferred_element_type=F32) / denom
        lse_all = m + jnp.log(denom)
        o = jnp.zeros((DIL_QB, DIL_DIM), F32)
        lse = jnp.zeros((DIL_QB, DIL_DIM), F32)
        for h in range(DIL_HEADS):
            rows = slice(h * DIL_QB, (h + 1) * DIL_QB)
            o = o + jnp.where(lane_head == h, o_all[rows, :], 0.0)
            lse = lse + jnp.where(lane_head == h, lse_all[rows, :], 0.0)
        o_ref[0, n * DIL_QB:(n + 1) * DIL_QB, :] = o
        lse_ref[0, n * DIL_QB:(n + 1) * DIL_QB, :] = lse


def _dil_group(qkv, t5_table, gi, dil, col):
    b, s, width = qkv.shape
    seq_len = s // dil
    assert seq_len % DIL_TL == 0 and col % (3 * DIL_DIM) == 0
    n_steps = seq_len // DIL_TL
    gi_col = col // (3 * DIL_DIM) + gi
    if dil > 1:
        qkv = qkv[:, :, gi_col * 3 * DIL_DIM:(gi_col + 1) * 3 * DIL_DIM]
        width, gi_col = 3 * DIL_DIM, 0
    n_col = width // (3 * DIL_DIM)
    view = qkv.reshape(b, seq_len, dil * width)
    blk = (1, DIL_TL, 3 * DIL_DIM)
    out_shape = jax.ShapeDtypeStruct((b, seq_len, dil * DIL_DIM), F32)
    out_spec = pl.BlockSpec((1, DIL_TL, DIL_DIM), lambda i, r, j: (i, j, r))
    o, lse = pl.pallas_call(
        functools.partial(_dil_kernel, seq_len=seq_len),
        grid=(b, dil, n_steps),
        in_specs=[
            pl.BlockSpec(blk, lambda i, r, j: (i, jnp.maximum(j - 1, 0), r * n_col + gi_col)),
            pl.BlockSpec(blk, lambda i, r, j: (i, j, r * n_col + gi_col)),
            pl.BlockSpec(blk, lambda i, r, j: (i, jnp.minimum(j + 1, n_steps - 1), r * n_col + gi_col)),
            pl.BlockSpec((DIL_HEADS * DIL_QB, DIL_KW), lambda i, r, j: (0, 0)),
        ],
        out_specs=[out_spec, out_spec],
        out_shape=[out_shape, out_shape],
        scratch_shapes=[pltpu.VMEM((3 * DIL_TL, DIL_DIM), MXU_DTYPE),
                        pltpu.VMEM((3 * DIL_TL, DIL_DIM), MXU_DTYPE)],
        compiler_params=pltpu.CompilerParams(
            dimension_semantics=("arbitrary", "arbitrary", "arbitrary"), vmem_limit_bytes=VMEM_LIMIT_BYTES),
        name=f"dilated_attention_g{gi}",
    )(view, view, view, _dil_bias_table(t5_table, gi, dil))
    return o.reshape(b, s, DIL_DIM), lse.reshape(b, s, DIL_DIM)


def _dil_combine_kernel(*refs):
    o_refs, lse_refs, out_ref = refs[:DIL_N_GROUPS], refs[DIL_N_GROUPS:2 * DIL_N_GROUPS], refs[-1]
    lses = [r[...] for r in lse_refs]
    m = functools.reduce(jnp.maximum, lses)
    ws = [jnp.exp(l - m) for l in lses]
    total = functools.reduce(jnp.add, ws)
    acc = functools.reduce(jnp.add, [w * r[...] for w, r in zip(ws, o_refs)])
    out_ref[...] = acc / total


def dilated_mixer(qkv, t5_table, col=0):
    b, s, _ = qkv.shape
    outs, lses = [], []
    for gi, (win, dil) in enumerate(DIL_GROUPS):
        assert win // (2 * dil) == DIL_HALF
        o, lse = _dil_group(qkv, t5_table, gi, dil, col)
        outs.append(o.reshape(b * s, DIL_DIM))
        lses.append(lse.reshape(b * s, DIL_DIM))
    tm = 1024
    spec = pl.BlockSpec((tm, DIL_DIM), lambda i: (i, 0))
    out = pl.pallas_call(
        _dil_combine_kernel,
        grid=(b * s // tm,),
        in_specs=[spec] * (2 * DIL_N_GROUPS),
        out_specs=spec,
        out_shape=jax.ShapeDtypeStruct((b * s, DIL_DIM), F32),
        compiler_params=pltpu.CompilerParams(
            dimension_semantics=("arbitrary",), vmem_limit_bytes=VMEM_LIMIT_BYTES),
        name="dilated_combine",
    )(*outs, *lses)
    return out.reshape(b, s, DIL_DIM)


SSM_CONV_TOK = 512
SSM_HALO = SUBLANES
SSM_HEADS_PER_GROUP = SSM_HEADS // SSM_GROUPS
SSM_GROUP_W = SSM_HEADS_PER_GROUP * SSM_HD
SSM_BC_W = SSM_GROUPS * SSM_STATE


def _softplus(x):
    return jnp.maximum(x, 0.0) + jnp.log1p(jnp.exp(-jnp.abs(x)))


def _ssm_conv_kernel(prev_ref, cur_ref, next_ref, w_ref, b_ref, out_ref, cat_scr):
    j = pl.program_id(1)
    tl = cur_ref.shape[1]
    cat_scr[0:SSM_HALO, :] = jnp.where(j > 0, prev_ref[0], 0.0)
    cat_scr[SSM_HALO:SSM_HALO + tl, :] = cur_ref[0]
    cat_scr[SSM_HALO + tl:, :] = jnp.where(j < pl.num_programs(1) - 1, next_ref[0], 0.0)
    acc = jnp.zeros((tl, CONV_CH), F32) + b_ref[...]
    for k in range(SSM_CONV):
        off = SSM_HALO + k - SSM_CONV // 2
        acc = acc + cat_scr[off:off + tl, :] * w_ref[k:k + 1, :]
    out_ref[0] = acc * jax.nn.sigmoid(acc)


def _ssm_conv(xbc, conv_w, conv_b, col):
    b, l, _ = xbc.shape
    tl = SSM_CONV_TOK
    n_steps = l // tl
    per = tl // SSM_HALO
    assert col % CONV_CH == 0
    cb = col // CONV_CH
    return pl.pallas_call(
        _ssm_conv_kernel,
        grid=(b, n_steps),
        in_specs=[
            pl.BlockSpec((1, SSM_HALO, CONV_CH), lambda i, j: (i, jnp.maximum(j * per - 1, 0), cb)),
            pl.BlockSpec((1, tl, CONV_CH), lambda i, j: (i, j, cb)),
            pl.BlockSpec((1, SSM_HALO, CONV_CH), lambda i, j: (i, jnp.minimum((j + 1) * per, n_steps * per - 1), cb)),
            pl.BlockSpec((SSM_CONV, CONV_CH), lambda i, j: (0, 0)),
            pl.BlockSpec((1, CONV_CH), lambda i, j: (0, 0)),
        ],
        out_specs=pl.BlockSpec((1, tl, CONV_CH), lambda i, j: (i, j, 0)),
        out_shape=jax.ShapeDtypeStruct((b, l, CONV_CH), F32),
        scratch_shapes=[pltpu.VMEM((tl + 2 * SSM_HALO, CONV_CH), F32)],
        compiler_params=pltpu.CompilerParams(
            dimension_semantics=("arbitrary", "arbitrary"), vmem_limit_bytes=VMEM_LIMIT_BYTES),
        name="ssm_conv",
    )(xbc, xbc, xbc, conv_w, conv_b.reshape(1, CONV_CH))


def _ssd_kernel(xf_ref, dtf_ref, dttf_ref, xb_ref, dtb_ref, dttb_ref,
                expf_ref, bef_ref, aef_ref, btf_ref, atf_ref, expb_ref, beb_ref, aeb_ref, btb_ref, atb_ref,
                yf_ref, yb_ref, state_scr):
    @pl.when(pl.program_id(1) == 0)
    def _():
        state_scr[...] = jnp.zeros(state_scr.shape, F32)

    fwd = _ssd_chunk(xf_ref, dtf_ref, dttf_ref, expf_ref, bef_ref, aef_ref, btf_ref, atf_ref,
                     yf_ref, state_scr.at[0], reverse=False)
    bwd = _ssd_chunk(xb_ref, dtb_ref, dttb_ref, expb_ref, beb_ref, aeb_ref, btb_ref, atb_ref,
                     yb_ref, state_scr.at[1], reverse=True)
    for _ in zip(fwd, bwd):
        pass
    for _ in fwd:
        pass
    for _ in bwd:
        pass


def _ssd_chunk(xbc_ref, dt_ref, dtt_ref, expand_ref, bias_e_ref, a_e_ref, bias_t_ref, a_t_ref,
               y_ref, state_scr, *, reverse):
    q = SSM_CHUNK
    hi = lax.Precision.HIGHEST
    xbc = xbc_ref[0]
    xs = xbc[:, 0:SSM_INNER]
    dt_e = _softplus(jnp.dot(dt_ref[0], expand_ref[...], precision=hi, preferred_element_type=F32)
                     + bias_e_ref[...])
    a_e = dt_e * a_e_ref[...]
    ri = lax.broadcasted_iota(jnp.int32, (q, q), 0)
    ci = lax.broadcasted_iota(jnp.int32, (q, q), 1)
    seen = (ci >= ri) if reverse else (ci <= ri)
    cs_e = jnp.dot(seen.astype(F32), a_e, precision=hi, preferred_element_type=F32)
    dt_t = _softplus(dtt_ref[0] + bias_t_ref[...])
    cs_t = jnp.dot(dt_t * a_t_ref[...], seen.T.astype(F32), precision=hi, preferred_element_type=F32)
    yield
    dtx = xs * dt_e
    last = 0 if reverse else q - 1
    total = cs_e[last:last + 1, :]
    dtx_decayed = jnp.exp(total - cs_e) * dtx
    grow = jnp.exp(cs_e)
    chunk_decay = jnp.exp(total)
    lane_head = lax.broadcasted_iota(jnp.int32, (q, SSM_GROUP_W), 1) // SSM_HD
    for g in range(SSM_GROUPS):
        xl = slice(g * SSM_GROUP_W, (g + 1) * SSM_GROUP_W)
        bg = xbc[:, SSM_INNER + g * SSM_STATE:SSM_INNER + (g + 1) * SSM_STATE]
        cg = xbc[:, SSM_INNER + SSM_BC_W + g * SSM_STATE:SSM_INNER + SSM_BC_W + (g + 1) * SSM_STATE]
        cb = lax.dot_general(cg.astype(MXU_DTYPE), bg.astype(MXU_DTYPE), (((1,), (1,)), ((), ())),
                             preferred_element_type=F32)
        yield
        ms = []
        for r in range(SSM_HEADS_PER_GROUP):
            h = g * SSM_HEADS_PER_GROUP + r
            col = cs_e[:, h * SSM_HD:h * SSM_HD + 1]
            row = cs_t[h:h + 1, :]
            ms.append(cb * jnp.where(seen, jnp.exp(col - row), 0.0))
        y_all = jnp.dot(jnp.concatenate(ms, axis=0).astype(MXU_DTYPE), dtx[:, xl].astype(MXU_DTYPE),
                        preferred_element_type=F32)
        yield
        y_diag = jnp.zeros((q, SSM_GROUP_W), F32)
        for r in range(SSM_HEADS_PER_GROUP):
            y_diag = y_diag + jnp.where(lane_head == r, y_all[r * q:(r + 1) * q, :], 0.0)
        s_in = state_scr[g]
        y_off = jnp.dot(cg.astype(MXU_DTYPE), s_in.astype(MXU_DTYPE), preferred_element_type=F32) * grow[:, xl]
        y_ref[0, :, xl] = y_diag + y_off
        yield
        new = jnp.dot(bg.T.astype(MXU_DTYPE), dtx_decayed[:, xl].astype(MXU_DTYPE), preferred_element_type=F32)
        state_scr[g] = s_in * chunk_decay[:, xl] + new


def _ssd_scans(xbc_act, dt_arr, dt_t, a_log, dt_bias, col_dt, dt_w):
    b, l, _ = xbc_act.shape
    assert col_dt % dt_w == 0 and dt_w >= 2 * SSM_HEADS
    q = SSM_CHUNK
    nc = l // q
    head_of_lane = np.arange(SSM_INNER) // SSM_HD
    full = lambda r, c: pl.BlockSpec((r, c), lambda i, j: (0, 0))
    chunk = (lambda j: j, lambda j: nc - 1 - j)
    data_specs, param_specs, params = [], [], []
    for d in range(2):
        a = -jnp.exp(a_log[d].astype(F32))
        bias = dt_bias[d].astype(F32)
        expand = (np.arange(dt_w)[:, None] == d * SSM_HEADS + head_of_lane[None, :]).astype(np.float32)
        data_specs += [
            pl.BlockSpec((1, q, CONV_CH), lambda i, j, d=d: (i, chunk[d](j), 0)),
            pl.BlockSpec((1, q, dt_w), lambda i, j, d=d: (i, chunk[d](j), col_dt // dt_w)),
            pl.BlockSpec((1, SSM_HEADS, q), lambda i, j, d=d: (i * 2 + d, 0, chunk[d](j))),
        ]
        param_specs += [full(dt_w, SSM_INNER), full(1, SSM_INNER), full(1, SSM_INNER),
                        full(SSM_HEADS, 1), full(SSM_HEADS, 1)]
        params += [jnp.asarray(expand), bias[head_of_lane].reshape(1, SSM_INNER),
                   a[head_of_lane].reshape(1, SSM_INNER), bias.reshape(SSM_HEADS, 1), a.reshape(SSM_HEADS, 1)]
    y_shape = jax.ShapeDtypeStruct((b, l, SSM_INNER), F32)
    return pl.pallas_call(
        _ssd_kernel,
        grid=(b, nc),
        in_specs=data_specs + param_specs,
        out_specs=[pl.BlockSpec((1, q, SSM_INNER), lambda i, j, d=d: (i, chunk[d](j), 0)) for d in range(2)],
        out_shape=[y_shape, y_shape],
        scratch_shapes=[pltpu.VMEM((2, SSM_GROUPS, SSM_STATE, SSM_GROUP_W), F32)],
        compiler_params=pltpu.CompilerParams(
            dimension_semantics=("arbitrary", "arbitrary"), vmem_limit_bytes=VMEM_LIMIT_BYTES),
        name="ssd_scans",
    )(xbc_act, dt_arr, dt_t, xbc_act, dt_arr, dt_t, *params)


def _ssm_gate_kernel(yf_ref, yb_ref, xbc_ref, z_ref, d_ref, g_ref, out_ref):
    z = z_ref[...]
    y = (yf_ref[...] + yb_ref[...] + xbc_ref[...] * d_ref[...]) * (z * jax.nn.sigmoid(z))
    out_ref[...] = y * lax.rsqrt(jnp.mean(y * y, axis=-1, keepdims=True) + EPS) * g_ref[...]


def mamba2_mixer(arr, conv_w, conv_b, A_log, dt_bias, D_skip, norm_g,
                 col_z=0, col_xbc=CONV_CH, col_dt=SSM_INNER + CONV_CH, dt_w=2 * SSM_HEADS):
    b, l, width = arr.shape
    assert col_z % SSM_INNER == 0
    xbc_act = _ssm_conv(arr, conv_w, conv_b, col_xbc)
    dt_t = jnp.swapaxes(arr[:, :, col_dt:col_dt + 2 * SSM_HEADS], 1, 2).reshape(b * 2, SSM_HEADS, l)
    y_f, y_b = _ssd_scans(xbc_act, arr, dt_t, A_log, dt_bias, col_dt, dt_w)
    tm = 1024
    tok = lambda cb: pl.BlockSpec((tm, SSM_INNER), lambda i: (i, cb))
    row = pl.BlockSpec((1, SSM_INNER), lambda i: (0, 0))
    d_e = D_skip.astype(F32)[np.arange(SSM_INNER) // SSM_HD].reshape(1, SSM_INNER)
    out = pl.pallas_call(
        _ssm_gate_kernel,
        grid=(b * l // tm,),
        in_specs=[tok(0), tok(0), tok(0), tok(col_z // SSM_INNER), row, row],
        out_specs=tok(0),
        out_shape=jax.ShapeDtypeStruct((b * l, SSM_INNER), F32),
        compiler_params=pltpu.CompilerParams(
            dimension_semantics=("arbitrary",), vmem_limit_bytes=VMEM_LIMIT_BYTES),
        name="ssm_gate",
    )(y_f.reshape(b * l, SSM_INNER), y_b.reshape(b * l, SSM_INNER), xbc_act.reshape(b * l, CONV_CH),
      arr.reshape(b * l, width), d_e, norm_g.reshape(1, SSM_INNER))
    return out.reshape(b, l, SSM_INNER)


NA_DIM = NA_HEADS * NA_HD
NA_ROWS_PER_STEP = 8
NA_WIN = NA_ROWS * GRID_W
NA_STEP_TOK = NA_ROWS_PER_STEP * GRID_W


def _na_bias_table(rpb):
    n_dc = 2 * NA_COLS - 1
    edge_l = jnp.repeat(rpb[:, :, :1], GRID_W, axis=2)
    edge_r = jnp.repeat(rpb[:, :, -1:], GRID_W, axis=2)
    ext = jnp.concatenate([edge_l, rpb.astype(F32), edge_r], axis=2)
    by_col = jnp.stack([ext[:, :, GRID_W + NA_COLS - 1 - qc:2 * GRID_W + NA_COLS - 1 - qc]
                        for qc in range(GRID_W)], axis=2)
    qc = np.arange(GRID_W)[:, None]
    kc = np.arange(GRID_W)[None, :]
    cs = np.clip(qc - NA_COLS // 2, 0, GRID_W - NA_COLS)
    ok = (kc >= cs) & (kc < cs + NA_COLS)
    by_col = jnp.where(ok[None, None], by_col, -jnp.inf)
    tabs = []
    for delta in range(NA_ROWS):
        rows = by_col[:, NA_ROWS - 1 - delta:2 * NA_ROWS - 1 - delta]
        tabs.append(rows.transpose(0, 2, 1, 3).reshape(NA_HEADS * GRID_W, NA_WIN))
    assert n_dc == rpb.shape[2]
    return jnp.stack(tabs, axis=0)


def _na_kernel(prev_ref, cur_ref, next_ref, tab_ref, out_ref, k_scr, v_scr, *, n_rows):
    step = pl.program_id(1)
    for i, ref in enumerate((prev_ref, cur_ref, next_ref)):
        k_scr[i * NA_STEP_TOK:(i + 1) * NA_STEP_TOK, :] = ref[0, :, NA_DIM:2 * NA_DIM].astype(MXU_DTYPE)
        v_scr[i * NA_STEP_TOK:(i + 1) * NA_STEP_TOK, :] = ref[0, :, 2 * NA_DIM:3 * NA_DIM].astype(MXU_DTYPE)
    lane_head = lax.broadcasted_iota(jnp.int32, (GRID_W, NA_DIM), 1) // NA_HD
    row0 = step * NA_ROWS_PER_STEP
    for j in range(NA_ROWS_PER_STEP):
        r = row0 + j
        r0 = jnp.clip(r - NA_ROWS // 2, 0, n_rows - NA_ROWS)
        start = pl.multiple_of((r0 - row0 + NA_ROWS_PER_STEP) * GRID_W, GRID_W)
        q = cur_ref[0, j * GRID_W:(j + 1) * GRID_W, 0:NA_DIM] * np.float32(NA_HD ** -0.5)
        qs = jnp.concatenate([jnp.where(lane_head == h, q, 0.0) for h in range(NA_HEADS)], axis=0)
        kw = k_scr[pl.ds(start, NA_WIN), :]
        vw = v_scr[pl.ds(start, NA_WIN), :]
        logits = lax.dot_general(qs.astype(MXU_DTYPE), kw, (((1,), (1,)), ((), ())),
                                 preferred_element_type=F32) + tab_ref[r - r0]
        m = jnp.max(logits, axis=-1, keepdims=True)
        p = jnp.exp(logits - m)
        denom = jnp.sum(p, axis=-1, keepdims=True)
        o_all = jnp.dot(p.astype(MXU_DTYPE), vw, preferred_element_type=F32) / denom
        o = jnp.zeros((GRID_W, NA_DIM), F32)
        for h in range(NA_HEADS):
            o = o + jnp.where(lane_head == h, o_all[h * GRID_W:(h + 1) * GRID_W, :], 0.0)
        out_ref[0, j * GRID_W:(j + 1) * GRID_W, :] = o


def na_mixer(qkv, rpb, col=0):
    b, s, _ = qkv.shape
    n_rows = s // GRID_W
    assert n_rows >= NA_ROWS and n_rows % NA_ROWS_PER_STEP == 0 and col % (3 * NA_DIM) == 0
    n_steps = n_rows // NA_ROWS_PER_STEP
    blk = (1, NA_STEP_TOK, 3 * NA_DIM)
    cb = col // (3 * NA_DIM)
    return pl.pallas_call(
        functools.partial(_na_kernel, n_rows=n_rows),
        grid=(b, n_steps),
        in_specs=[
            pl.BlockSpec(blk, lambda i, j: (i, jnp.maximum(j - 1, 0), cb)),
            pl.BlockSpec(blk, lambda i, j: (i, j, cb)),
            pl.BlockSpec(blk, lambda i, j: (i, jnp.minimum(j + 1, n_steps - 1), cb)),
            pl.BlockSpec((NA_ROWS, NA_HEADS * GRID_W, NA_WIN), lambda i, j: (0, 0, 0)),
        ],
        out_specs=pl.BlockSpec((1, NA_STEP_TOK, NA_DIM), lambda i, j: (i, j, 0)),
        out_shape=jax.ShapeDtypeStruct((b, s, NA_DIM), F32),
        scratch_shapes=[
            pltpu.VMEM((3 * NA_STEP_TOK, NA_DIM), MXU_DTYPE),
            pltpu.VMEM((3 * NA_STEP_TOK, NA_DIM), MXU_DTYPE),
        ],
        compiler_params=pltpu.CompilerParams(
            dimension_semantics=("arbitrary", "arbitrary"), vmem_limit_bytes=VMEM_LIMIT_BYTES),
        name="na_attention",
    )(qkv, qkv, qkv, _na_bias_table(rpb))


PK_GATE = 0
PK_XBC = PK_GATE + N_BRANCH * D_MODEL
PK_CQ = PK_XBC + CONV_CH
PK_DIL = PK_CQ + MLA_Q_RANK
PK_NA = PK_DIL + DIL_N_GROUPS * 3 * DIL_DIM
PK_KR = PK_NA + 3 * NA_DIM
PK_Z = PK_KR + 2 * MLA_HG
PK_CKV = PK_Z + SSM_INNER
PK_DT = PK_CKV + MLA_KV_RANK
PK_DT_W = LANES
PK_WIDTH = PK_DT + PK_DT_W
INPROJ_TM = 512
INPROJ_TN = PK_WIDTH // 2
MERGE_TM = 512


def _pack_w_in(w_in_l):
    gate, a_cq, a_ckv, a_kr, b_qkv, c_z, c_xbc, c_dt, d_qkv = jnp.split(w_in_l, IN_SPLITS, axis=-1)
    zeros = lambda n: jnp.zeros((D_MODEL, n), w_in_l.dtype)
    cols = [gate, c_xbc, a_cq, b_qkv, d_qkv, mla_rotary_key_columns(a_kr), c_z, a_ckv,
            c_dt, zeros(PK_DT_W - 2 * SSM_HEADS)]
    packed = jnp.concatenate(cols, axis=-1)
    assert packed.shape[1] == PK_DT + PK_DT_W
    return jnp.concatenate([packed, zeros(PK_WIDTH - packed.shape[1])], axis=-1).astype(MXU_DTYPE)


def _inproj_kernel(x_ref, g_ref, w_ref, out_ref, h_scr):
    @pl.when(pl.program_id(1) == 0)
    def _():
        x = x_ref[...]
        h_scr[...] = (x * lax.rsqrt(jnp.mean(x * x, axis=-1, keepdims=True) + EPS) * g_ref[...]).astype(MXU_DTYPE)

    out_ref[...] = jnp.dot(h_scr[...], w_ref[...], preferred_element_type=F32)


def _inproj(x2d, norm_g, w_packed):
    n_tok = x2d.shape[0]
    tm, tn = INPROJ_TM, INPROJ_TN
    return pl.pallas_call(
        _inproj_kernel,
        grid=(n_tok // tm, PK_WIDTH // tn),
        in_specs=[pl.BlockSpec((tm, D_MODEL), lambda i, j: (i, 0)),
                  pl.BlockSpec((1, D_MODEL), lambda i, j: (0, 0)),
                  pl.BlockSpec((D_MODEL, tn), lambda i, j: (0, j))],
        out_specs=pl.BlockSpec((tm, tn), lambda i, j: (i, j)),
        out_shape=jax.ShapeDtypeStruct((n_tok, PK_WIDTH), F32),
        scratch_shapes=[pltpu.VMEM((tm, D_MODEL), MXU_DTYPE)],
        compiler_params=pltpu.CompilerParams(
            dimension_semantics=("arbitrary", "arbitrary"), vmem_limit_bytes=VMEM_LIMIT_BYTES),
        name="in_projection",
    )(x2d, norm_g.reshape(1, D_MODEL), w_packed)


def _merge_kernel(x_ref, gate_ref, bg_ref, ya_ref, yb_ref, yc_ref, yd_ref, wb_ref, wo_ref, out_ref):
    merged = jnp.zeros(x_ref.shape, F32)
    for i, y_ref in enumerate((ya_ref, yb_ref, yc_ref, yd_ref)):
        proj = jnp.dot(y_ref[...].astype(MXU_DTYPE), wb_ref[BRANCH_ROWS[i]:BRANCH_ROWS[i + 1], :],
                       preferred_element_type=F32)
        lanes = slice(i * D_MODEL, (i + 1) * D_MODEL)
        merged = merged + jax.nn.sigmoid(gate_ref[:, lanes] + bg_ref[:, lanes]) * proj
    out_ref[...] = x_ref[...] + jnp.dot(merged.astype(MXU_DTYPE), wo_ref[...], preferred_element_type=F32)


def _merge(x2d, packed, b_gate, ys, w_branch, w_out):
    n_tok = x2d.shape[0]
    tm = MERGE_TM
    tok = lambda w: pl.BlockSpec((tm, w), lambda i: (i, 0))
    full = lambda r, c: pl.BlockSpec((r, c), lambda i: (0, 0))
    n_gate = N_BRANCH * D_MODEL
    return pl.pallas_call(
        _merge_kernel,
        grid=(n_tok // tm,),
        in_specs=[tok(D_MODEL), tok(n_gate), full(1, n_gate)] + [tok(w) for w in BRANCH_WIDTHS]
                 + [full(BRANCH_ROWS[-1], D_MODEL), full(D_MODEL, D_MODEL)],
        out_specs=tok(D_MODEL),
        out_shape=jax.ShapeDtypeStruct((n_tok, D_MODEL), F32),
        compiler_params=pltpu.CompilerParams(
            dimension_semantics=("arbitrary",), vmem_limit_bytes=VMEM_LIMIT_BYTES),
        name="branch_merge",
    )(x2d, packed, b_gate.reshape(1, n_gate), *[y.reshape(n_tok, -1) for y in ys], w_branch, w_out)


def encoder(x, norm1_g, w_in_packed, b_gate, mla_q_norm, mla_w_qb, mla_kv_norm, mla_w_kvb, t5_table,
            ssm_conv_w, ssm_conv_b, ssm_A_log, ssm_dt_bias, ssm_D, ssm_norm_g, na_rpb,
            w_branch, w_out, norm2_g, peer_wq, peer_keys, peer_u, peer_vt, final_g):
    b, s, _ = x.shape
    x2d = x.reshape(b * s, D_MODEL)
    for l in range(DEPTH):
        packed = _inproj(x2d, norm1_g[l], w_in_packed[l])
        p3 = packed.reshape(b, s, PK_WIDTH)
        y_a = mla_mixer(p3, mla_q_norm[l], mla_w_qb[l], mla_kv_norm[l], mla_w_kvb[l],
                        col_q=PK_CQ, col_kv=PK_CKV, col_kr=PK_KR)
        y_b = dilated_mixer(p3, t5_table, col=PK_DIL)
        y_c = mamba2_mixer(p3, ssm_conv_w[l], ssm_conv_b[l], ssm_A_log[l], ssm_dt_bias[l], ssm_D[l],
                           ssm_norm_g[l], col_z=PK_Z, col_xbc=PK_XBC, col_dt=PK_DT, dt_w=PK_DT_W)
        y_d = na_mixer(p3, na_rpb[l], col=PK_NA)
        x2d = _merge(x2d, packed, b_gate[l], (y_a, y_b, y_c, y_d), w_branch[l], w_out[l])
        x2d = peer_block(x2d, norm2_g[l], peer_wq[l], peer_keys[l], peer_u[l], peer_vt[l],
                         final_g, final_norm=(l == DEPTH - 1))
    return x2d.reshape(b, s, D_MODEL)


def kernel(x_prompt, x_sample, norm1_g, w_in, b_gate, mla_q_norm, mla_w_qb, mla_kv_norm, mla_w_kvb, t5_table, ssm_conv_w, ssm_conv_b, ssm_A_log, ssm_dt_bias, ssm_D, ssm_norm_g, na_rpb, w_branch, w_out, norm2_g, peer_wq, peer_keys, peer_u, peer_v, final_g):
    peer_u16 = peer_u.astype(MXU_DTYPE)
    peer_vt16 = jnp.swapaxes(peer_v, 1, 2).astype(MXU_DTYPE)
    w_in_packed = jnp.stack([_pack_w_in(w_in[l]) for l in range(DEPTH)])
    shared = (norm1_g, w_in_packed, b_gate, mla_q_norm, mla_w_qb, mla_kv_norm, mla_w_kvb, t5_table,
              ssm_conv_w, ssm_conv_b, ssm_A_log, ssm_dt_bias, ssm_D, ssm_norm_g, na_rpb,
              w_branch.astype(MXU_DTYPE), w_out.astype(MXU_DTYPE), norm2_g, peer_wq, peer_keys,
              peer_u16, peer_vt16, final_g)
    y_prompt = encoder(x_prompt, *shared)
    y_sample = encoder(x_sample, *shared)
    return (y_prompt, y_sample)
```

```python
import functools
import math

import numpy as np
import jax
import jax.numpy as jnp
from jax import lax
from jax.experimental import pallas as pl
from jax.experimental.pallas import tpu as pltpu

F32 = jnp.float32
BF16 = jnp.bfloat16
MXU_DTYPE = BF16

D_MODEL = 1024
DEPTH = 2
GRID_W = 64
EPS = 1e-6
N_BRANCH = 4

MLA_HEADS = 4
MLA_Q_RANK = 256
MLA_KV_RANK = 128
MLA_NOPE = 64
MLA_ROPE = 32
MLA_V = 64
ROPE_THETA = 10000.0

DIL_GROUPS = ((128, 1), (512, 4), (2048, 16))
DIL_HEADS = 4
DIL_HD = 64
T5_BUCKETS = 32
T5_MAX_DIST = 1024

SSM_HEADS = 8
SSM_HD = 64
SSM_INNER = SSM_HEADS * SSM_HD
SSM_GROUPS = 2
SSM_STATE = 128
SSM_CONV = 7
SSM_CHUNK = 128
CONV_CH = SSM_INNER + 2 * SSM_GROUPS * SSM_STATE

NA_HEADS = 4
NA_HD = 64
NA_ROWS = 8
NA_COLS = 16

PEER_HEADS = 8
PEER_KEYS = 128
PEER_EXPERTS = PEER_KEYS * PEER_KEYS
PEER_QDIM = 256
PEER_TOPK = 16

BRANCH_WIDTHS = (MLA_HEADS * MLA_V, DIL_HEADS * DIL_HD, SSM_INNER, NA_HEADS * NA_HD)
BRANCH_ROWS = tuple(sum(BRANCH_WIDTHS[:i]) for i in range(N_BRANCH + 1))
IN_SIZES = (N_BRANCH * D_MODEL, MLA_Q_RANK, MLA_KV_RANK, MLA_ROPE,
            len(DIL_GROUPS) * 3 * DIL_HEADS * DIL_HD,
            SSM_INNER, CONV_CH, 2 * SSM_HEADS,
            3 * NA_HEADS * NA_HD)
IN_SPLITS = tuple(sum(IN_SIZES[:i + 1]) for i in range(len(IN_SIZES) - 1))

VMEM_LIMIT_BYTES = 56 * 1024 * 1024
LANES = 128
SUBLANES = 8


PEER_ROUTE_TB = 512
PEER_TB = 512
PEER_EC = 1024
PEER_GATE_DTYPE = BF16
PEER_GATE_ROWS = 16
PEER_HALF = PEER_QDIM // 2
PEER_CAND_ROWS = 2 * SUBLANES + 7 * SUBLANES + SUBLANES


def _gelu_exact_x2(x):
    return x * (1.0 + lax.erf(x * np.float32(math.sqrt(0.5))))


def _extract_desc(vals, n_out, out_ref, row0, with_rank=False):
    rank = jnp.full(vals.shape, float(n_out), F32) if with_rank else None
    for k in range(n_out):
        m = jnp.max(vals, axis=0, keepdims=True)
        out_ref[pl.ds(row0 + k, 1), :] = m
        hit = vals == m
        if with_rank:
            rank = jnp.where(hit, float(k), rank)
        vals = jnp.where(hit, -jnp.inf, vals)
    return rank


def _peer_route_kernel(x_ref, g_ref, wqt_ref, keys_ref,
                       xnt_ref, cnt_ref, rank_ref, e1_ref, e2_ref,
                       qt_scr, top_scr, cand_scr, tops_scr):
    x = x_ref[...]
    xn = x * lax.rsqrt(jnp.mean(x * x, axis=-1, keepdims=True) + EPS) * g_ref[...]
    xnt = xn.T.astype(MXU_DTYPE)
    xnt_ref[...] = xnt
    qt_scr[...] = jnp.dot(wqt_ref[...], xnt, preferred_element_type=F32).astype(MXU_DTYPE)

    def head(h, carry):
        q1 = qt_scr[pl.ds(pl.multiple_of(h * PEER_QDIM, PEER_QDIM), PEER_HALF), :]
        q2 = qt_scr[pl.ds(pl.multiple_of(h * PEER_QDIM + PEER_HALF, PEER_HALF), PEER_HALF), :]
        s1 = jnp.dot(keys_ref[2 * h], q1, preferred_element_type=F32)
        s2 = jnp.dot(keys_ref[2 * h + 1], q2, preferred_element_type=F32)
        _extract_desc(s1, PEER_TOPK, top_scr, 0)
        rank2 = _extract_desc(s2, PEER_TOPK, top_scr, PEER_TOPK, with_rank=True)
        t1 = top_scr[0:PEER_TOPK, :]
        t2 = top_scr[PEER_TOPK:2 * PEER_TOPK, :]
        cand_scr[0:2 * SUBLANES, :] = t1[0:1, :] + t2
        for a in range(1, SUBLANES):
            cand_scr[(a + 1) * SUBLANES:(a + 2) * SUBLANES, :] = t1[a:a + 1, :] + t2[0:SUBLANES, :]
        cand_scr[9 * SUBLANES:10 * SUBLANES, :] = t1[SUBLANES:2 * SUBLANES, :] + t2[0:1, :]
        _extract_desc(cand_scr[...], PEER_TOPK, tops_scr, 0)
        top_s = tops_scr[...]
        z = jnp.sum(jnp.exp(top_s - top_s[0:1, :]), axis=0, keepdims=True)
        tau = top_s[PEER_TOPK - 1:PEER_TOPK, :]
        cnt_top = jnp.zeros(t1.shape, F32)
        for b in range(PEER_TOPK):
            cnt_top = cnt_top + jnp.where(t1 + t2[b:b + 1, :] >= tau, 1.0, 0.0)
        cnt = jnp.zeros(s1.shape, F32)
        for a in range(PEER_TOPK):
            cnt = jnp.where(s1 == t1[a:a + 1, :], cnt_top[a:a + 1, :], cnt)
        cnt_ref[h] = cnt
        rank_ref[h] = rank2.astype(rank_ref.dtype)
        e1_ref[h] = jnp.exp(s1 - t1[0:1, :]) * (0.5 / z)
        e2_ref[h] = jnp.exp(s2 - t2[0:1, :]).astype(e2_ref.dtype)
        return carry

    lax.fori_loop(0, PEER_HEADS, head, 0)


def _peer_route(x2d, g, wqt, keys):
    n_tok = x2d.shape[0]
    tb = PEER_ROUTE_TB
    rt_shape = jax.ShapeDtypeStruct((PEER_HEADS, PEER_KEYS, n_tok), F32)
    gate_shape = jax.ShapeDtypeStruct((PEER_HEADS, PEER_KEYS, n_tok), PEER_GATE_DTYPE)
    rt_spec = pl.BlockSpec((PEER_HEADS, PEER_KEYS, tb), lambda i: (0, 0, i))
    return pl.pallas_call(
        _peer_route_kernel,
        grid=(n_tok // tb,),
        in_specs=[
            pl.BlockSpec((tb, D_MODEL), lambda i: (i, 0)),
            pl.BlockSpec((1, D_MODEL), lambda i: (0, 0)),
            pl.BlockSpec((PEER_HEADS * PEER_QDIM, D_MODEL), lambda i: (0, 0)),
            pl.BlockSpec((2 * PEER_HEADS, PEER_KEYS, PEER_HALF), lambda i: (0, 0, 0)),
        ],
        out_specs=[
            pl.BlockSpec((D_MODEL, tb), lambda i: (0, i)),
            rt_spec, rt_spec, rt_spec, rt_spec,
        ],
        out_shape=[
            jax.ShapeDtypeStruct((D_MODEL, n_tok), MXU_DTYPE),
            rt_shape, gate_shape, rt_shape, gate_shape,
        ],
        scratch_shapes=[
            pltpu.VMEM((PEER_HEADS * PEER_QDIM, tb), MXU_DTYPE),
            pltpu.VMEM((2 * PEER_TOPK, tb), F32),
            pltpu.VMEM((PEER_CAND_ROWS, tb), F32),
            pltpu.VMEM((PEER_TOPK, tb), F32),
        ],
        compiler_params=pltpu.CompilerParams(
            dimension_semantics=("arbitrary",), vmem_limit_bytes=VMEM_LIMIT_BYTES),
        name="peer_route",
    )(x2d, g, wqt, keys)


def _peer_expert_kernel(x_ref, xnt_ref, cnt_ref, rank_ref, e1_ref, e2_ref, u_ref, vt_ref, fg_ref,
                        out_ref, acc_scr, w_scr, *, final_norm):
    c = pl.program_id(1)
    n_chunks = pl.num_programs(1) - 1
    n_i1 = PEER_EC // PEER_KEYS
    tb = acc_scr.shape[1]
    gdt = PEER_GATE_DTYPE

    @pl.when(c == 0)
    def _():
        acc_scr[...] = jnp.zeros_like(acc_scr)
        w_scr[1] = jnp.zeros(w_scr.shape[1:], w_scr.dtype)

    slot = c % 2
    n_groups = PEER_KEYS // PEER_GATE_ROWS
    acc_scr[...] += jnp.dot(vt_ref[...], w_scr[(c + 1) % 2], preferred_element_type=F32)
    hid = jnp.dot(u_ref[...], xnt_ref[...], preferred_element_type=F32)

    def gate_block(i1l, l0):
        lanes = slice(l0, l0 + LANES)
        gates = [jnp.zeros((PEER_GATE_ROWS, LANES), gdt) for _ in range(n_groups)]
        for h in range(PEER_HEADS):
            cntb = jnp.broadcast_to(cnt_ref[h, i1l:i1l + 1, lanes], (PEER_GATE_ROWS, LANES)).astype(gdt)
            e1b = jnp.broadcast_to(e1_ref[h, i1l:i1l + 1, lanes], (PEER_GATE_ROWS, LANES)).astype(gdt)
            for k in range(n_groups):
                rows = slice(k * PEER_GATE_ROWS, (k + 1) * PEER_GATE_ROWS)
                w = e2_ref[h, rows, lanes] * e1b
                gates[k] = gates[k] + jnp.where(rank_ref[h, rows, lanes] < cntb, w, jnp.zeros_like(w))
        for k in range(n_groups):
            r0 = k * PEER_GATE_ROWS
            e0 = i1l * PEER_KEYS + r0
            act = _gelu_exact_x2(hid[e0:e0 + PEER_GATE_ROWS, lanes]).astype(gdt)
            w_scr[slot, e0:e0 + PEER_GATE_ROWS, lanes] = (act * gates[k]).astype(MXU_DTYPE)

    for i1l in range(n_i1):
        for l0 in range(0, tb, LANES):
            gate_block(i1l, l0)

    @pl.when(c == n_chunks)
    def _():
        y = x_ref[...] + acc_scr[...].T
        if final_norm:
            y = y * lax.rsqrt(jnp.mean(y * y, axis=-1, keepdims=True) + EPS) * fg_ref[...]
        out_ref[...] = y


def _peer_experts(x2d, xnt, cnt, rank2, e1, e2, u, vt, final_g, final_norm):
    n_tok = x2d.shape[0]
    tb, ec = PEER_TB, PEER_EC
    n_chunks = PEER_EXPERTS // ec
    rt_spec = pl.BlockSpec((PEER_HEADS, PEER_KEYS, tb), lambda j, c: (0, 0, j))
    row_spec = pl.BlockSpec((PEER_HEADS, ec // PEER_KEYS, tb),
                            lambda j, c: (0, jnp.minimum(c, n_chunks - 1), j))
    return pl.pallas_call(
        functools.partial(_peer_expert_kernel, final_norm=final_norm),
        grid=(n_tok // tb, n_chunks + 1),
        in_specs=[
            pl.BlockSpec((tb, D_MODEL), lambda j, c: (j, 0)),
            pl.BlockSpec((D_MODEL, tb), lambda j, c: (0, j)),
            row_spec, rt_spec, row_spec, rt_spec,
            pl.BlockSpec((ec, D_MODEL), lambda j, c: (jnp.minimum(c, n_chunks - 1), 0)),
            pl.BlockSpec((D_MODEL, ec), lambda j, c: (0, jnp.maximum(c - 1, 0))),
            pl.BlockSpec((1, D_MODEL), lambda j, c: (0, 0)),
        ],
        out_specs=pl.BlockSpec((tb, D_MODEL), lambda j, c: (j, 0)),
        out_shape=jax.ShapeDtypeStruct((n_tok, D_MODEL), F32),
        scratch_shapes=[
            pltpu.VMEM((D_MODEL, tb), F32),
            pltpu.VMEM((2, ec, tb), MXU_DTYPE),
        ],
        compiler_params=pltpu.CompilerParams(
            dimension_semantics=("arbitrary", "arbitrary"), vmem_limit_bytes=VMEM_LIMIT_BYTES),
        name="peer_experts",
    )(x2d, xnt, cnt, rank2, e1, e2, u, vt, final_g.reshape(1, D_MODEL))


def peer_block(x2d, norm_g, w_q, keys, u, v, final_g, final_norm=False):
    wqt = w_q.T.astype(MXU_DTYPE)
    keys2 = keys.reshape(2 * PEER_HEADS, PEER_KEYS, PEER_HALF).astype(MXU_DTYPE)
    xnt, cnt, rank2, e1, e2 = _peer_route(x2d, norm_g.reshape(1, D_MODEL), wqt, keys2)
    return _peer_experts(x2d, xnt, cnt, rank2, e1, e2, u, v, final_g, final_norm)


MLA_HG = LANES
MLA_QK_W = MLA_HEADS * MLA_HG
MLA_V_W = MLA_HEADS * MLA_V
MLA_PREP_TOK = 512
MLA_TQ = 512
MLA_TK = 2048
MLA_RHALF = MLA_ROPE // 2


def _mla_rope_tables(s):
    inv = ROPE_THETA ** (-jnp.arange(MLA_RHALF, dtype=F32) / MLA_RHALF)
    ang = jnp.arange(s).astype(F32)[:, None] * inv[None, :]
    cos, sin = jnp.cos(ang), jnp.sin(ang)
    zero_pad = jnp.zeros((s, MLA_HG - MLA_NOPE - MLA_ROPE), F32)
    cos_rot = jnp.concatenate([cos, cos, zero_pad], axis=1)
    sin_rot = jnp.concatenate([-sin, sin, zero_pad], axis=1)
    scale = np.float32((MLA_NOPE + MLA_ROPE) ** -0.5)
    q_cos = scale * jnp.concatenate([jnp.ones((s, MLA_NOPE), F32), cos_rot], axis=1)
    q_sin = scale * jnp.concatenate([jnp.zeros((s, MLA_NOPE), F32), sin_rot], axis=1)
    k_cos = jnp.concatenate([jnp.zeros((s, MLA_NOPE), F32), cos_rot], axis=1)
    k_sin = jnp.concatenate([jnp.zeros((s, MLA_NOPE), F32), sin_rot], axis=1)
    return q_cos, q_sin, k_cos, k_sin


def _mla_pack_weights(w_qb, w_kvb):
    hd_q = MLA_NOPE + MLA_ROPE
    wq = w_qb.reshape(MLA_Q_RANK, MLA_HEADS, hd_q)
    rot = wq[:, :, MLA_NOPE:]
    rot_sw = jnp.concatenate([rot[:, :, MLA_RHALF:], rot[:, :, :MLA_RHALF]], axis=2)
    pad = jnp.zeros((MLA_Q_RANK, MLA_HEADS, MLA_HG - hd_q), F32)
    wq_a = jnp.concatenate([wq, pad], axis=2).reshape(MLA_Q_RANK, MLA_QK_W)
    wq_b = jnp.concatenate([jnp.zeros_like(wq[:, :, :MLA_NOPE]), rot_sw, pad], axis=2).reshape(MLA_Q_RANK, MLA_QK_W)
    wkv = w_kvb.reshape(MLA_KV_RANK, MLA_HEADS, MLA_NOPE + MLA_V)
    wk = jnp.concatenate([wkv[:, :, :MLA_NOPE], jnp.zeros((MLA_KV_RANK, MLA_HEADS, MLA_HG - MLA_NOPE), F32)],
                         axis=2).reshape(MLA_KV_RANK, MLA_QK_W)
    wv = jnp.concatenate([wkv[:, :, MLA_NOPE:], jnp.zeros((MLA_KV_RANK, MLA_HEADS, MLA_HG - MLA_V), F32)],
                         axis=2).reshape(MLA_KV_RANK, MLA_QK_W)
    return (wq_a.astype(MXU_DTYPE), wq_b.astype(MXU_DTYPE), wk.astype(MXU_DTYPE), wv.astype(MXU_DTYPE))


def _mla_prep_kernel(cq_ref, ckv_ref, kr_ref, qn_ref, kvn_ref, wqa_ref, wqb_ref, wk_ref, wv_ref,
                     qcos_ref, qsin_ref, kcos_ref, ksin_ref, q_out, kt_out, v_out):
    cq = cq_ref[0]
    cqn = (cq * lax.rsqrt(jnp.mean(cq * cq, axis=-1, keepdims=True) + EPS) * qn_ref[...]).astype(MXU_DTYPE)
    qa = jnp.dot(cqn, wqa_ref[...], preferred_element_type=F32)
    qb = jnp.dot(cqn, wqb_ref[...], preferred_element_type=F32)
    ckv = ckv_ref[0]
    ckvn = (ckv * lax.rsqrt(jnp.mean(ckv * ckv, axis=-1, keepdims=True) + EPS) * kvn_ref[...]).astype(MXU_DTYPE)
    ka = jnp.dot(ckvn, wk_ref[...], preferred_element_type=F32)
    one_lane = (lax.broadcasted_iota(jnp.int32, (1, MLA_QK_W), 1) % MLA_HG == MLA_V).astype(F32)
    v_out[0] = (jnp.dot(ckvn, wv_ref[...], preferred_element_type=F32) + one_lane).astype(MXU_DTYPE)
    kr = kr_ref[0]
    k_rot = kr[:, 0:MLA_HG] * kcos_ref[...] + kr[:, MLA_HG:2 * MLA_HG] * ksin_ref[...]
    for h in range(MLA_HEADS):
        lanes = slice(h * MLA_HG, (h + 1) * MLA_HG)
        q_out[0, :, lanes] = (qa[:, lanes] * qcos_ref[...] + qb[:, lanes] * qsin_ref[...]).astype(MXU_DTYPE)
        kt_out[0, lanes, :] = (ka[:, lanes] + k_rot).T.astype(MXU_DTYPE)


def _mla_flash_kernel(q_ref, kt_ref, v_ref, out_ref, m_scr, acc_scr):
    ki = pl.program_id(2)

    @pl.when(ki == 0)
    def _():
        m_scr[...] = jnp.full(m_scr.shape, -jnp.inf, F32)
        acc_scr[...] = jnp.zeros(acc_scr.shape, F32)

    def logits(h):
        lanes = slice(h * MLA_HG, (h + 1) * MLA_HG)
        return jnp.dot(q_ref[0, :, lanes], kt_ref[0, lanes, :], preferred_element_type=F32)

    n_rep = kt_ref.shape[2] // MLA_HG
    s_next = logits(0)
    for h in range(MLA_HEADS):
        s = s_next
        if h + 1 < MLA_HEADS:
            s_next = logits(h + 1)
        m_old = m_scr[h]
        m_new = jnp.maximum(m_old, jnp.max(s, axis=-1, keepdims=True))
        p = jnp.exp(s - jnp.tile(m_new, (1, n_rep))).astype(MXU_DTYPE)
        acc_scr[h] = jnp.exp(m_old - m_new) * acc_scr[h] + jnp.dot(
            p, v_ref[0, :, h * MLA_HG:(h + 1) * MLA_HG], preferred_element_type=F32)
        m_scr[h] = m_new

    @pl.when(ki == pl.num_programs(2) - 1)
    def _():
        low = lax.broadcasted_iota(jnp.int32, (acc_scr.shape[1], MLA_HG), 1) < MLA_V
        outs = []
        for h in range(MLA_HEADS):
            acc = acc_scr[h]
            outs.append(acc / acc[:, MLA_V:MLA_V + 1])
        for hp in range(MLA_HEADS // 2):
            odd = pltpu.roll(outs[2 * hp + 1], MLA_V, axis=1)
            out_ref[0, :, hp * MLA_HG:(hp + 1) * MLA_HG] = jnp.where(low, outs[2 * hp], odd)


def mla_rotary_key_columns(w):
    zl = jnp.zeros(w.shape[:-1] + (MLA_NOPE,), w.dtype)
    zr = jnp.zeros(w.shape[:-1] + (MLA_HG - MLA_NOPE - MLA_ROPE,), w.dtype)
    w_sw = jnp.concatenate([w[..., MLA_RHALF:], w[..., :MLA_RHALF]], axis=-1)
    return jnp.concatenate([zl, w, zr, zl, w_sw, zr], axis=-1)


def mla_mixer(arr, q_norm, w_qb, kv_norm, w_kvb, col_q=0, col_kv=MLA_Q_RANK, col_kr=MLA_Q_RANK + MLA_KV_RANK):
    b, s, _ = arr.shape
    tt = MLA_PREP_TOK
    assert col_q % MLA_Q_RANK == 0 and col_kv % MLA_KV_RANK == 0 and col_kr % (2 * MLA_HG) == 0
    wqa, wqb, wk, wv = _mla_pack_weights(w_qb, w_kvb)
    q_cos, q_sin, k_cos, k_sin = _mla_rope_tables(s)
    tok = lambda w, c=0: pl.BlockSpec((1, tt, w), lambda i, j: (i, j, c // w))
    full = lambda r, c: pl.BlockSpec((r, c), lambda i, j: (0, 0))
    tab = pl.BlockSpec((tt, MLA_HG), lambda i, j: (j, 0))
    q, kt, v = pl.pallas_call(
        _mla_prep_kernel,
        grid=(b, s // tt),
        in_specs=[tok(MLA_Q_RANK, col_q), tok(MLA_KV_RANK, col_kv), tok(2 * MLA_HG, col_kr),
                  full(1, MLA_Q_RANK), full(1, MLA_KV_RANK),
                  full(MLA_Q_RANK, MLA_QK_W), full(MLA_Q_RANK, MLA_QK_W),
                  full(MLA_KV_RANK, MLA_QK_W), full(MLA_KV_RANK, MLA_QK_W),
                  tab, tab, tab, tab],
        out_specs=[tok(MLA_QK_W), pl.BlockSpec((1, MLA_QK_W, tt), lambda i, j: (i, 0, j)), tok(MLA_QK_W)],
        out_shape=[jax.ShapeDtypeStruct((b, s, MLA_QK_W), MXU_DTYPE),
                   jax.ShapeDtypeStruct((b, MLA_QK_W, s), MXU_DTYPE),
                   jax.ShapeDtypeStruct((b, s, MLA_QK_W), MXU_DTYPE)],
        compiler_params=pltpu.CompilerParams(
            dimension_semantics=("arbitrary", "arbitrary"), vmem_limit_bytes=VMEM_LIMIT_BYTES),
        name="mla_prep",
    )(arr, arr, arr, q_norm.reshape(1, -1), kv_norm.reshape(1, -1), wqa, wqb, wk, wv,
      q_cos, q_sin, k_cos, k_sin)
    tq, tk = min(MLA_TQ, s), min(MLA_TK, s)
    assert s % tq == 0 and s % tk == 0 and s % tt == 0
    return pl.pallas_call(
        _mla_flash_kernel,
        grid=(b, s // tq, s // tk),
        in_specs=[pl.BlockSpec((1, tq, MLA_QK_W), lambda i, j, kk: (i, j, 0)),
                  pl.BlockSpec((1, MLA_QK_W, tk), lambda i, j, kk: (i, 0, kk)),
                  pl.BlockSpec((1, tk, MLA_QK_W), lambda i, j, kk: (i, kk, 0))],
        out_specs=pl.BlockSpec((1, tq, MLA_V_W), lambda i, j, kk: (i, j, 0)),
        out_shape=jax.ShapeDtypeStruct((b, s, MLA_V_W), F32),
        scratch_shapes=[pltpu.VMEM((MLA_HEADS, tq, MLA_HG), F32),
                        pltpu.VMEM((MLA_HEADS, tq, MLA_HG), F32)],
        compiler_params=pltpu.CompilerParams(
            dimension_semantics=("arbitrary", "arbitrary", "arbitrary"), vmem_limit_bytes=VMEM_LIMIT_BYTES),
        name="mla_flash",
    )(q, kt, v)


def t5_bucket(rel):
    nb = T5_BUCKETS // 2
    ret = np.where(rel > 0, nb, 0)
    n = np.abs(rel)
    max_exact = nb // 2
    large = max_exact + (np.log(np.maximum(n, 1) / max_exact) / np.log(T5_MAX_DIST / max_exact)
                         * (nb - max_exact)).astype(np.int64)
    large = np.minimum(large, nb - 1)
    return (ret + np.where(n < max_exact, n, large)).astype(np.int32)


DIL_DIM = DIL_HEADS * DIL_HD
DIL_HALF = 64
DIL_QB = 128
DIL_KW = DIL_QB + 2 * DIL_HALF
DIL_TL = 512
DIL_N_GROUPS = len(DIL_GROUPS)


def _dil_bias_table(t5_table, gi, dil):
    rel = np.arange(DIL_KW)[None, :] - DIL_HALF - np.arange(DIL_QB)[:, None]
    bias = t5_table[:, gi * DIL_HEADS:(gi + 1) * DIL_HEADS][t5_bucket(rel * dil)].astype(F32)
    bias = jnp.where((np.abs(rel) <= DIL_HALF)[:, :, None], bias, -jnp.inf)
    return bias.transpose(2, 0, 1).reshape(DIL_HEADS * DIL_QB, DIL_KW)


def _dil_kernel(prev_ref, cur_ref, next_ref, tab_ref, o_ref, lse_ref, k_scr, v_scr, *, seq_len):
    step = pl.program_id(2)
    for i, ref in enumerate((prev_ref, cur_ref, next_ref)):
        k_scr[i * DIL_TL:(i + 1) * DIL_TL, :] = ref[0, :, DIL_DIM:2 * DIL_DIM].astype(MXU_DTYPE)
        v_scr[i * DIL_TL:(i + 1) * DIL_TL, :] = ref[0, :, 2 * DIL_DIM:3 * DIL_DIM].astype(MXU_DTYPE)
    lane_head = lax.broadcasted_iota(jnp.int32, (DIL_QB, DIL_DIM), 1) // DIL_HD
    key_off = lax.broadcasted_iota(jnp.int32, (1, DIL_KW), 1)
    for n in range(DIL_TL // DIL_QB):
        w0 = DIL_TL + n * DIL_QB - DIL_HALF
        kpos = step * DIL_TL + (n * DIL_QB - DIL_HALF) + key_off
        valid = (kpos >= 0) & (kpos < seq_len)
        q = cur_ref[0, n * DIL_QB:(n + 1) * DIL_QB, 0:DIL_DIM] * np.float32(DIL_HD ** -0.5)
        qs = jnp.concatenate([jnp.where(lane_head == h, q, 0.0) for h in range(DIL_HEADS)], axis=0)
        logits = lax.dot_general(qs.astype(MXU_DTYPE), k_scr[w0:w0 + DIL_KW, :], (((1,), (1,)), ((), ())),
                                 preferred_element_type=F32) + tab_ref[...]
        logits = jnp.where(valid, logits, -jnp.inf)
        m = jnp.max(logits, axis=-1, keepdims=True)
        p = jnp.exp(logits - m)
        denom = jnp.sum(p, axis=-1, keepdims=True)
        o_all = jnp.dot(p.astype(MXU_DTYPE), v_scr[w0:w0 + DIL_KW, :], preferred_element_type=F32) / denom
        lse_all = m + jnp.log(denom)
        o = jnp.zeros((DIL_QB, DIL_DIM), F32)
        lse = jnp.zeros((DIL_QB, DIL_DIM), F32)
        for h in range(DIL_HEADS):
            rows = slice(h * DIL_QB, (h + 1) * DIL_QB)
            o = o + jnp.where(lane_head == h, o_all[rows, :], 0.0)
            lse = lse + jnp.where(lane_head == h, lse_all[rows, :], 0.0)
        o_ref[0, n * DIL_QB:(n + 1) * DIL_QB, :] = o
        lse_ref[0, n * DIL_QB:(n + 1) * DIL_QB, :] = lse


def _dil_group(qkv, t5_table, gi, dil, col):
    b, s, width = qkv.shape
    seq_len = s // dil
    assert seq_len % DIL_TL == 0 and col % (3 * DIL_DIM) == 0
    n_steps = seq_len // DIL_TL
    gi_col = col // (3 * DIL_DIM) + gi
    if dil > 1:
        qkv = qkv[:, :, gi_col * 3 * DIL_DIM:(gi_col + 1) * 3 * DIL_DIM]
        width, gi_col = 3 * DIL_DIM, 0
    n_col = width // (3 * DIL_DIM)
    view = qkv.reshape(b, seq_len, dil * width)
    blk = (1, DIL_TL, 3 * DIL_DIM)
    out_shape = jax.ShapeDtypeStruct((b, seq_len, dil * DIL_DIM), F32)
    out_spec = pl.BlockSpec((1, DIL_TL, DIL_DIM), lambda i, r, j: (i, j, r))
    o, lse = pl.pallas_call(
        functools.partial(_dil_kernel, seq_len=seq_len),
        grid=(b, dil, n_steps),
        in_specs=[
            pl.BlockSpec(blk, lambda i, r, j: (i, jnp.maximum(j - 1, 0), r * n_col + gi_col)),
            pl.BlockSpec(blk, lambda i, r, j: (i, j, r * n_col + gi_col)),
            pl.BlockSpec(blk, lambda i, r, j: (i, jnp.minimum(j + 1, n_steps - 1), r * n_col + gi_col)),
            pl.BlockSpec((DIL_HEADS * DIL_QB, DIL_KW), lambda i, r, j: (0, 0)),
        ],
        out_specs=[out_spec, out_spec],
        out_shape=[out_shape, out_shape],
        scratch_shapes=[pltpu.VMEM((3 * DIL_TL, DIL_DIM), MXU_DTYPE),
                        pltpu.VMEM((3 * DIL_TL, DIL_DIM), MXU_DTYPE)],
        compiler_params=pltpu.CompilerParams(
            dimension_semantics=("arbitrary", "arbitrary", "arbitrary"), vmem_limit_bytes=VMEM_LIMIT_BYTES),
        name=f"dilated_attention_g{gi}",
    )(view, view, view, _dil_bias_table(t5_table, gi, dil))
    return o.reshape(b, s, DIL_DIM), lse.reshape(b, s, DIL_DIM)


def _dil_combine_kernel(*refs):
    o_refs, lse_refs, out_ref = refs[:DIL_N_GROUPS], refs[DIL_N_GROUPS:2 * DIL_N_GROUPS], refs[-1]
    lses = [r[...] for r in lse_refs]
    m = functools.reduce(jnp.maximum, lses)
    ws = [jnp.exp(l - m) for l in lses]
    total = functools.reduce(jnp.add, ws)
    acc = functools.reduce(jnp.add, [w * r[...] for w, r in zip(ws, o_refs)])
    out_ref[...] = acc / total


def dilated_mixer(qkv, t5_table, col=0):
    b, s, _ = qkv.shape
    outs, lses = [], []
    for gi, (win, dil) in enumerate(DIL_GROUPS):
        assert win // (2 * dil) == DIL_HALF
        o, lse = _dil_group(qkv, t5_table, gi, dil, col)
        outs.append(o.reshape(b * s, DIL_DIM))
        lses.append(lse.reshape(b * s, DIL_DIM))
    tm = 1024
    spec = pl.BlockSpec((tm, DIL_DIM), lambda i: (i, 0))
    out = pl.pallas_call(
        _dil_combine_kernel,
        grid=(b * s // tm,),
        in_specs=[spec] * (2 * DIL_N_GROUPS),
        out_specs=spec,
        out_shape=jax.ShapeDtypeStruct((b * s, DIL_DIM), F32),
        compiler_params=pltpu.CompilerParams(
            dimension_semantics=("arbitrary",), vmem_limit_bytes=VMEM_LIMIT_BYTES),
        name="dilated_combine",
    )(*outs, *lses)
    return out.reshape(b, s, DIL_DIM)


SSM_CONV_TOK = 512
SSM_HALO = SUBLANES
SSM_HEADS_PER_GROUP = SSM_HEADS // SSM_GROUPS
SSM_GROUP_W = SSM_HEADS_PER_GROUP * SSM_HD
SSM_BC_W = SSM_GROUPS * SSM_STATE


def _softplus(x):
    return jnp.maximum(x, 0.0) + jnp.log1p(jnp.exp(-jnp.abs(x)))


def _ssm_conv_kernel(prev_ref, cur_ref, next_ref, w_ref, b_ref, out_ref, cat_scr):
    j = pl.program_id(1)
    tl = cur_ref.shape[1]
    cat_scr[0:SSM_HALO, :] = jnp.where(j > 0, prev_ref[0], 0.0)
    cat_scr[SSM_HALO:SSM_HALO + tl, :] = cur_ref[0]
    cat_scr[SSM_HALO + tl:, :] = jnp.where(j < pl.num_programs(1) - 1, next_ref[0], 0.0)
    acc = jnp.zeros((tl, CONV_CH), F32) + b_ref[...]
    for k in range(SSM_CONV):
        off = SSM_HALO + k - SSM_CONV // 2
        acc = acc + cat_scr[off:off + tl, :] * w_ref[k:k + 1, :]
    out_ref[0] = acc * jax.nn.sigmoid(acc)


def _ssm_conv(xbc, conv_w, conv_b, col):
    b, l, _ = xbc.shape
    tl = SSM_CONV_TOK
    n_steps = l // tl
    per = tl // SSM_HALO
    assert col % CONV_CH == 0
    cb = col // CONV_CH
    return pl.pallas_call(
        _ssm_conv_kernel,
        grid=(b, n_steps),
        in_specs=[
            pl.BlockSpec((1, SSM_HALO, CONV_CH), lambda i, j: (i, jnp.maximum(j * per - 1, 0), cb)),
            pl.BlockSpec((1, tl, CONV_CH), lambda i, j: (i, j, cb)),
            pl.BlockSpec((1, SSM_HALO, CONV_CH), lambda i, j: (i, jnp.minimum((j + 1) * per, n_steps * per - 1), cb)),
            pl.BlockSpec((SSM_CONV, CONV_CH), lambda i, j: (0, 0)),
            pl.BlockSpec((1, CONV_CH), lambda i, j: (0, 0)),
        ],
        out_specs=pl.BlockSpec((1, tl, CONV_CH), lambda i, j: (i, j, 0)),
        out_shape=jax.ShapeDtypeStruct((b, l, CONV_CH), F32),
        scratch_shapes=[pltpu.VMEM((tl + 2 * SSM_HALO, CONV_CH), F32)],
        compiler_params=pltpu.CompilerParams(
            dimension_semantics=("arbitrary", "arbitrary"), vmem_limit_bytes=VMEM_LIMIT_BYTES),
        name="ssm_conv",
    )(xbc, xbc, xbc, conv_w, conv_b.reshape(1, CONV_CH))


def _ssd_kernel(xf_ref, dtf_ref, dttf_ref, xb_ref, dtb_ref, dttb_ref,
                expf_ref, bef_ref, aef_ref, btf_ref, atf_ref, expb_ref, beb_ref, aeb_ref, btb_ref, atb_ref,
                yf_ref, yb_ref, state_scr):
    @pl.when(pl.program_id(1) == 0)
    def _():
        state_scr[...] = jnp.zeros(state_scr.shape, F32)

    fwd = _ssd_chunk(xf_ref, dtf_ref, dttf_ref, expf_ref, bef_ref, aef_ref, btf_ref, atf_ref,
                     yf_ref, state_scr.at[0], reverse=False)
    bwd = _ssd_chunk(xb_ref, dtb_ref, dttb_ref, expb_ref, beb_ref, aeb_ref, btb_ref, atb_ref,
                     yb_ref, state_scr.at[1], reverse=True)
    for _ in zip(fwd, bwd):
        pass
    for _ in fwd:
        pass
    for _ in bwd:
        pass


def _ssd_chunk(xbc_ref, dt_ref, dtt_ref, expand_ref, bias_e_ref, a_e_ref, bias_t_ref, a_t_ref,
               y_ref, state_scr, *, reverse):
    q = SSM_CHUNK
    hi = lax.Precision.HIGHEST
    xbc = xbc_ref[0]
    xs = xbc[:, 0:SSM_INNER]
    dt_e = _softplus(jnp.dot(dt_ref[0], expand_ref[...], precision=hi, preferred_element_type=F32)
                     + bias_e_ref[...])
    a_e = dt_e * a_e_ref[...]
    ri = lax.broadcasted_iota(jnp.int32, (q, q), 0)
    ci = lax.broadcasted_iota(jnp.int32, (q, q), 1)
    seen = (ci >= ri) if reverse else (ci <= ri)
    cs_e = jnp.dot(seen.astype(F32), a_e, precision=hi, preferred_element_type=F32)
    dt_t = _softplus(dtt_ref[0] + bias_t_ref[...])
    cs_t = jnp.dot(dt_t * a_t_ref[...], seen.T.astype(F32), precision=hi, preferred_element_type=F32)
    yield
    dtx = xs * dt_e
    last = 0 if reverse else q - 1
    total = cs_e[last:last + 1, :]
    dtx_decayed = jnp.exp(total - cs_e) * dtx
    grow = jnp.exp(cs_e)
    chunk_decay = jnp.exp(total)
    lane_head = lax.broadcasted_iota(jnp.int32, (q, SSM_GROUP_W), 1) // SSM_HD
    for g in range(SSM_GROUPS):
        xl = slice(g * SSM_GROUP_W, (g + 1) * SSM_GROUP_W)
        bg = xbc[:, SSM_INNER + g * SSM_STATE:SSM_INNER + (g + 1) * SSM_STATE]
        cg = xbc[:, SSM_INNER + SSM_BC_W + g * SSM_STATE:SSM_INNER + SSM_BC_W + (g + 1) * SSM_STATE]
        cb = lax.dot_general(cg.astype(MXU_DTYPE), bg.astype(MXU_DTYPE), (((1,), (1,)), ((), ())),
                             preferred_element_type=F32)
        yield
        ms = []
        for r in range(SSM_HEADS_PER_GROUP):
            h = g * SSM_HEADS_PER_GROUP + r
            col = cs_e[:, h * SSM_HD:h * SSM_HD + 1]
            row = cs_t[h:h + 1, :]
            ms.append(cb * jnp.where(seen, jnp.exp(col - row), 0.0))
        y_all = jnp.dot(jnp.concatenate(ms, axis=0).astype(MXU_DTYPE), dtx[:, xl].astype(MXU_DTYPE),
                        preferred_element_type=F32)
        yield
        y_diag = jnp.zeros((q, SSM_GROUP_W), F32)
        for r in range(SSM_HEADS_PER_GROUP):
            y_diag = y_diag + jnp.where(lane_head == r, y_all[r * q:(r + 1) * q, :], 0.0)
        s_in = state_scr[g]
        y_off = jnp.dot(cg.astype(MXU_DTYPE), s_in.astype(MXU_DTYPE), preferred_element_type=F32) * grow[:, xl]
        y_ref[0, :, xl] = y_diag + y_off
        yield
        new = jnp.dot(bg.T.astype(MXU_DTYPE), dtx_decayed[:, xl].astype(MXU_DTYPE), preferred_element_type=F32)
        state_scr[g] = s_in * chunk_decay[:, xl] + new


def _ssd_scans(xbc_act, dt_arr, dt_t, a_log, dt_bias, col_dt, dt_w):
    b, l, _ = xbc_act.shape
    assert col_dt % dt_w == 0 and dt_w >= 2 * SSM_HEADS
    q = SSM_CHUNK
    nc = l // q
    head_of_lane = np.arange(SSM_INNER) // SSM_HD
    full = lambda r, c: pl.BlockSpec((r, c), lambda i, j: (0, 0))
    chunk = (lambda j: j, lambda j: nc - 1 - j)
    data_specs, param_specs, params = [], [], []
    for d in range(2):
        a = -jnp.exp(a_log[d].astype(F32))
        bias = dt_bias[d].astype(F32)
        expand = (np.arange(dt_w)[:, None] == d * SSM_HEADS + head_of_lane[None, :]).astype(np.float32)
        data_specs += [
            pl.BlockSpec((1, q, CONV_CH), lambda i, j, d=d: (i, chunk[d](j), 0)),
            pl.BlockSpec((1, q, dt_w), lambda i, j, d=d: (i, chunk[d](j), col_dt // dt_w)),
            pl.BlockSpec((1, SSM_HEADS, q), lambda i, j, d=d: (i * 2 + d, 0, chunk[d](j))),
        ]
        param_specs += [full(dt_w, SSM_INNER), full(1, SSM_INNER), full(1, SSM_INNER),
                        full(SSM_HEADS, 1), full(SSM_HEADS, 1)]
        params += [jnp.asarray(expand), bias[head_of_lane].reshape(1, SSM_INNER),
                   a[head_of_lane].reshape(1, SSM_INNER), bias.reshape(SSM_HEADS, 1), a.reshape(SSM_HEADS, 1)]
    y_shape = jax.ShapeDtypeStruct((b, l, SSM_INNER), F32)
    return pl.pallas_call(
        _ssd_kernel,
        grid=(b, nc),
        in_specs=data_specs + param_specs,
        out_specs=[pl.BlockSpec((1, q, SSM_INNER), lambda i, j, d=d: (i, chunk[d](j), 0)) for d in range(2)],
        out_shape=[y_shape, y_shape],
        scratch_shapes=[pltpu.VMEM((2, SSM_GROUPS, SSM_STATE, SSM_GROUP_W), F32)],
        compiler_params=pltpu.CompilerParams(
            dimension_semantics=("arbitrary", "arbitrary"), vmem_limit_bytes=VMEM_LIMIT_BYTES),
        name="ssd_scans",
    )(xbc_act, dt_arr, dt_t, xbc_act, dt_arr, dt_t, *params)


def _ssm_gate_kernel(yf_ref, yb_ref, xbc_ref, z_ref, d_ref, g_ref, out_ref):
    z = z_ref[...]
    y = (yf_ref[...] + yb_ref[...] + xbc_ref[...] * d_ref[...]) * (z * jax.nn.sigmoid(z))
    out_ref[...] = y * lax.rsqrt(jnp.mean(y * y, axis=-1, keepdims=True) + EPS) * g_ref[...]


def mamba2_mixer(arr, conv_w, conv_b, A_log, dt_bias, D_skip, norm_g,
                 col_z=0, col_xbc=CONV_CH, col_dt=SSM_INNER + CONV_CH, dt_w=2 * SSM_HEADS):
    b, l, width = arr.shape
    assert col_z % SSM_INNER == 0
    xbc_act = _ssm_conv(arr, conv_w, conv_b, col_xbc)
    dt_t = jnp.swapaxes(arr[:, :, col_dt:col_dt + 2 * SSM_HEADS], 1, 2).reshape(b * 2, SSM_HEADS, l)
    y_f, y_b = _ssd_scans(xbc_act, arr, dt_t, A_log, dt_bias, col_dt, dt_w)
    tm = 1024
    tok = lambda cb: pl.BlockSpec((tm, SSM_INNER), lambda i: (i, cb))
    row = pl.BlockSpec((1, SSM_INNER), lambda i: (0, 0))
    d_e = D_skip.astype(F32)[np.arange(SSM_INNER) // SSM_HD].reshape(1, SSM_INNER)
    out = pl.pallas_call(
        _ssm_gate_kernel,
        grid=(b * l // tm,),
        in_specs=[tok(0), tok(0), tok(0), tok(col_z // SSM_INNER), row, row],
        out_specs=tok(0),
        out_shape=jax.ShapeDtypeStruct((b * l, SSM_INNER), F32),
        compiler_params=pltpu.CompilerParams(
            dimension_semantics=("arbitrary",), vmem_limit_bytes=VMEM_LIMIT_BYTES),
        name="ssm_gate",
    )(y_f.reshape(b * l, SSM_INNER), y_b.reshape(b * l, SSM_INNER), xbc_act.reshape(b * l, CONV_CH),
      arr.reshape(b * l, width), d_e, norm_g.reshape(1, SSM_INNER))
    return out.reshape(b, l, SSM_INNER)


NA_DIM = NA_HEADS * NA_HD
NA_ROWS_PER_STEP = 8
NA_WIN = NA_ROWS * GRID_W
NA_STEP_TOK = NA_ROWS_PER_STEP * GRID_W


def _na_bias_table(rpb):
    n_dc = 2 * NA_COLS - 1
    edge_l = jnp.repeat(rpb[:, :, :1], GRID_W, axis=2)
    edge_r = jnp.repeat(rpb[:, :, -1:], GRID_W, axis=2)
    ext = jnp.concatenate([edge_l, rpb.astype(F32), edge_r], axis=2)
    by_col = jnp.stack([ext[:, :, GRID_W + NA_COLS - 1 - qc:2 * GRID_W + NA_COLS - 1 - qc]
                        for qc in range(GRID_W)], axis=2)
    qc = np.arange(GRID_W)[:, None]
    kc = np.arange(GRID_W)[None, :]
    cs = np.clip(qc - NA_COLS // 2, 0, GRID_W - NA_COLS)
    ok = (kc >= cs) & (kc < cs + NA_COLS)
    by_col = jnp.where(ok[None, None], by_col, -jnp.inf)
    tabs = []
    for delta in range(NA_ROWS):
        rows = by_col[:, NA_ROWS - 1 - delta:2 * NA_ROWS - 1 - delta]
        tabs.append(rows.transpose(0, 2, 1, 3).reshape(NA_HEADS * GRID_W, NA_WIN))
    assert n_dc == rpb.shape[2]
    return jnp.stack(tabs, axis=0)


def _na_kernel(prev_ref, cur_ref, next_ref, tab_ref, out_ref, k_scr, v_scr, *, n_rows):
    step = pl.program_id(1)
    for i, ref in enumerate((prev_ref, cur_ref, next_ref)):
        k_scr[i * NA_STEP_TOK:(i + 1) * NA_STEP_TOK, :] = ref[0, :, NA_DIM:2 * NA_DIM].astype(MXU_DTYPE)
        v_scr[i * NA_STEP_TOK:(i + 1) * NA_STEP_TOK, :] = ref[0, :, 2 * NA_DIM:3 * NA_DIM].astype(MXU_DTYPE)
    lane_head = lax.broadcasted_iota(jnp.int32, (GRID_W, NA_DIM), 1) // NA_HD
    row0 = step * NA_ROWS_PER_STEP
    for j in range(NA_ROWS_PER_STEP):
        r = row0 + j
        r0 = jnp.clip(r - NA_ROWS // 2, 0, n_rows - NA_ROWS)
        start = pl.multiple_of((r0 - row0 + NA_ROWS_PER_STEP) * GRID_W, GRID_W)
        q = cur_ref[0, j * GRID_W:(j + 1) * GRID_W, 0:NA_DIM] * np.float32(NA_HD ** -0.5)
        qs = jnp.concatenate([jnp.where(lane_head == h, q, 0.0) for h in range(NA_HEADS)], axis=0)
        kw = k_scr[pl.ds(start, NA_WIN), :]
        vw = v_scr[pl.ds(start, NA_WIN), :]
        logits = lax.dot_general(qs.astype(MXU_DTYPE), kw, (((1,), (1,)), ((), ())),
                                 preferred_element_type=F32) + tab_ref[r - r0]
        m = jnp.max(logits, axis=-1, keepdims=True)
        p = jnp.exp(logits - m)
        denom = jnp.sum(p, axis=-1, keepdims=True)
        o_all = jnp.dot(p.astype(MXU_DTYPE), vw, preferred_element_type=F32) / denom
        o = jnp.zeros((GRID_W, NA_DIM), F32)
        for h in range(NA_HEADS):
            o = o + jnp.where(lane_head == h, o_all[h * GRID_W:(h + 1) * GRID_W, :], 0.0)
        out_ref[0, j * GRID_W:(j + 1) * GRID_W, :] = o


def na_mixer(qkv, rpb, col=0):
    b, s, _ = qkv.shape
    n_rows = s // GRID_W
    assert n_rows >= NA_ROWS and n_rows % NA_ROWS_PER_STEP == 0 and col % (3 * NA_DIM) == 0
    n_steps = n_rows // NA_ROWS_PER_STEP
    blk = (1, NA_STEP_TOK, 3 * NA_DIM)
    cb = col // (3 * NA_DIM)
    return pl.pallas_call(
        functools.partial(_na_kernel, n_rows=n_rows),
        grid=(b, n_steps),
        in_specs=[
            pl.BlockSpec(blk, lambda i, j: (i, jnp.maximum(j - 1, 0), cb)),
            pl.BlockSpec(blk, lambda i, j: (i, j, cb)),
            pl.BlockSpec(blk, lambda i, j: (i, jnp.minimum(j + 1, n_steps - 1), cb)),
            pl.BlockSpec((NA_ROWS, NA_HEADS * GRID_W, NA_WIN), lambda i, j: (0, 0, 0)),
        ],
        out_specs=pl.BlockSpec((1, NA_STEP_TOK, NA_DIM), lambda i, j: (i, j, 0)),
        out_shape=jax.ShapeDtypeStruct((b, s, NA_DIM), F32),
        scratch_shapes=[
            pltpu.VMEM((3 * NA_STEP_TOK, NA_DIM), MXU_DTYPE),
            pltpu.VMEM((3 * NA_STEP_TOK, NA_DIM), MXU_DTYPE),
        ],
        compiler_params=pltpu.CompilerParams(
            dimension_semantics=("arbitrary", "arbitrary"), vmem_limit_bytes=VMEM_LIMIT_BYTES),
        name="na_attention",
    )(qkv, qkv, qkv, _na_bias_table(rpb))


PK_GATE = 0
PK_XBC = PK_GATE + N_BRANCH * D_MODEL
PK_CQ = PK_XBC + CONV_CH
PK_DIL = PK_CQ + MLA_Q_RANK
PK_NA = PK_DIL + DIL_N_GROUPS * 3 * DIL_DIM
PK_KR = PK_NA + 3 * NA_DIM
PK_Z = PK_KR + 2 * MLA_HG
PK_CKV = PK_Z + SSM_INNER
PK_DT = PK_CKV + MLA_KV_RANK
PK_DT_W = LANES
PK_WIDTH = PK_DT + PK_DT_W
INPROJ_TM = 512
INPROJ_TN = PK_WIDTH // 2
MERGE_TM = 512


def _pack_w_in(w_in_l):
    gate, a_cq, a_ckv, a_kr, b_qkv, c_z, c_xbc, c_dt, d_qkv = jnp.split(w_in_l, IN_SPLITS, axis=-1)
    zeros = lambda n: jnp.zeros((D_MODEL, n), w_in_l.dtype)
    cols = [gate, c_xbc, a_cq, b_qkv, d_qkv, mla_rotary_key_columns(a_kr), c_z, a_ckv,
            c_dt, zeros(PK_DT_W - 2 * SSM_HEADS)]
    packed = jnp.concatenate(cols, axis=-1)
    assert packed.shape[1] == PK_DT + PK_DT_W
    return jnp.concatenate([packed, zeros(PK_WIDTH - packed.shape[1])], axis=-1).astype(MXU_DTYPE)


def _inproj_kernel(x_ref, g_ref, w_ref, out_ref, h_scr):
    @pl.when(pl.program_id(1) == 0)
    def _():
        x = x_ref[...]
        h_scr[...] = (x * lax.rsqrt(jnp.mean(x * x, axis=-1, keepdims=True) + EPS) * g_ref[...]).astype(MXU_DTYPE)

    out_ref[...] = jnp.dot(h_scr[...], w_ref[...], preferred_element_type=F32)


def _inproj(x2d, norm_g, w_packed):
    n_tok = x2d.shape[0]
    tm, tn = INPROJ_TM, INPROJ_TN
    return pl.pallas_call(
        _inproj_kernel,
        grid=(n_tok // tm, PK_WIDTH // tn),
        in_specs=[pl.BlockSpec((tm, D_MODEL), lambda i, j: (i, 0)),
                  pl.BlockSpec((1, D_MODEL), lambda i, j: (0, 0)),
                  pl.BlockSpec((D_MODEL, tn), lambda i, j: (0, j))],
        out_specs=pl.BlockSpec((tm, tn), lambda i, j: (i, j)),
        out_shape=jax.ShapeDtypeStruct((n_tok, PK_WIDTH), F32),
        scratch_shapes=[pltpu.VMEM((tm, D_MODEL), MXU_DTYPE)],
        compiler_params=pltpu.CompilerParams(
            dimension_semantics=("arbitrary", "arbitrary"), vmem_limit_bytes=VMEM_LIMIT_BYTES),
        name="in_projection",
    )(x2d, norm_g.reshape(1, D_MODEL), w_packed)


def _merge_kernel(x_ref, gate_ref, bg_ref, ya_ref, yb_ref, yc_ref, yd_ref, wb_ref, wo_ref, out_ref):
    merged = jnp.zeros(x_ref.shape, F32)
    for i, y_ref in enumerate((ya_ref, yb_ref, yc_ref, yd_ref)):
        proj = jnp.dot(y_ref[...].astype(MXU_DTYPE), wb_ref[BRANCH_ROWS[i]:BRANCH_ROWS[i + 1], :],
                       preferred_element_type=F32)
        lanes = slice(i * D_MODEL, (i + 1) * D_MODEL)
        merged = merged + jax.nn.sigmoid(gate_ref[:, lanes] + bg_ref[:, lanes]) * proj
    out_ref[...] = x_ref[...] + jnp.dot(merged.astype(MXU_DTYPE), wo_ref[...], preferred_element_type=F32)


def _merge(x2d, packed, b_gate, ys, w_branch, w_out):
    n_tok = x2d.shape[0]
    tm = MERGE_TM
    tok = lambda w: pl.BlockSpec((tm, w), lambda i: (i, 0))
    full = lambda r, c: pl.BlockSpec((r, c), lambda i: (0, 0))
    n_gate = N_BRANCH * D_MODEL
    return pl.pallas_call(
        _merge_kernel,
        grid=(n_tok // tm,),
        in_specs=[tok(D_MODEL), tok(n_gate), full(1, n_gate)] + [tok(w) for w in BRANCH_WIDTHS]
                 + [full(BRANCH_ROWS[-1], D_MODEL), full(D_MODEL, D_MODEL)],
        out_specs=tok(D_MODEL),
        out_shape=jax.ShapeDtypeStruct((n_tok, D_MODEL), F32),
        compiler_params=pltpu.CompilerParams(
            dimension_semantics=("arbitrary",), vmem_limit_bytes=VMEM_LIMIT_BYTES),
        name="branch_merge",
    )(x2d, packed, b_gate.reshape(1, n_gate), *[y.reshape(n_tok, -1) for y in ys], w_branch, w_out)


def encoder(x, norm1_g, w_in_packed, b_gate, mla_q_norm, mla_w_qb, mla_kv_norm, mla_w_kvb, t5_table,
            ssm_conv_w, ssm_conv_b, ssm_A_log, ssm_dt_bias, ssm_D, ssm_norm_g, na_rpb,
            w_branch, w_out, norm2_g, peer_wq, peer_keys, peer_u, peer_vt, final_g):
    b, s, _ = x.shape
    x2d = x.reshape(b * s, D_MODEL)
    for l in range(DEPTH):
        packed = _inproj(x2d, norm1_g[l], w_in_packed[l])
        p3 = packed.reshape(b, s, PK_WIDTH)
        y_a = mla_mixer(p3, mla_q_norm[l], mla_w_qb[l], mla_kv_norm[l], mla_w_kvb[l],
                        col_q=PK_CQ, col_kv=PK_CKV, col_kr=PK_KR)
        y_b = dilated_mixer(p3, t5_table, col=PK_DIL)
        y_c = mamba2_mixer(p3, ssm_conv_w[l], ssm_conv_b[l], ssm_A_log[l], ssm_dt_bias[l], ssm_D[l],
                           ssm_norm_g[l], col_z=PK_Z, col_xbc=PK_XBC, col_dt=PK_DT, dt_w=PK_DT_W)
        y_d = na_mixer(p3, na_rpb[l], col=PK_NA)
        x2d = _merge(x2d, packed, b_gate[l], (y_a, y_b, y_c, y_d), w_branch[l], w_out[l])
        x2d = peer_block(x2d, norm2_g[l], peer_wq[l], peer_keys[l], peer_u[l], peer_vt[l],
                         final_g, final_norm=(l == DEPTH - 1))
    return x2d.reshape(b, s, D_MODEL)


def kernel(x_prompt, x_sample, norm1_g, w_in, b_gate, mla_q_norm, mla_w_qb, mla_kv_norm, mla_w_kvb, t5_table, ssm_conv_w, ssm_conv_b, ssm_A_log, ssm_dt_bias, ssm_D, ssm_norm_g, na_rpb, w_branch, w_out, norm2_g, peer_wq, peer_keys, peer_u, peer_v, final_g):
    peer_u16 = peer_u.astype(MXU_DTYPE)
    peer_vt16 = jnp.swapaxes(peer_v, 1, 2).astype(MXU_DTYPE)
    w_in_packed = jnp.stack([_pack_w_in(w_in[l]) for l in range(DEPTH)])
    shared = (norm1_g, w_in_packed, b_gate, mla_q_norm, mla_w_qb, mla_kv_norm, mla_w_kvb, t5_table,
              ssm_conv_w, ssm_conv_b, ssm_A_log, ssm_dt_bias, ssm_D, ssm_norm_g, na_rpb,
              w_branch.astype(MXU_DTYPE), w_out.astype(MXU_DTYPE), norm2_g, peer_wq, peer_keys,
              peer_u16, peer_vt16, final_g)
    y_prompt = encoder(x_prompt, *shared)
    y_sample = encoder(x_sample, *shared)
    return (y_prompt, y_sample)
```

```python
import functools
import math

import numpy as np
import jax
import jax.numpy as jnp
from jax import lax
from jax.experimental import pallas as pl
from jax.experimental.pallas import tpu as pltpu

F32 = jnp.float32
BF16 = jnp.bfloat16
MXU_DTYPE = BF16

D_MODEL = 1024
DEPTH = 2
GRID_W = 64
EPS = 1e-6
N_BRANCH = 4

MLA_HEADS = 4
MLA_Q_RANK = 256
MLA_KV_RANK = 128
MLA_NOPE = 64
MLA_ROPE = 32
MLA_V = 64
ROPE_THETA = 10000.0

DIL_GROUPS = ((128, 1), (512, 4), (2048, 16))
DIL_HEADS = 4
DIL_HD = 64
T5_BUCKETS = 32
T5_MAX_DIST = 1024

SSM_HEADS = 8
SSM_HD = 64
SSM_INNER = SSM_HEADS * SSM_HD
SSM_GROUPS = 2
SSM_STATE = 128
SSM_CONV = 7
SSM_CHUNK = 128
CONV_CH = SSM_INNER + 2 * SSM_GROUPS * SSM_STATE

NA_HEADS = 4
NA_HD = 64
NA_ROWS = 8
NA_COLS = 16

PEER_HEADS = 8
PEER_KEYS = 128
PEER_EXPERTS = PEER_KEYS * PEER_KEYS
PEER_QDIM = 256
PEER_TOPK = 16

BRANCH_WIDTHS = (MLA_HEADS * MLA_V, DIL_HEADS * DIL_HD, SSM_INNER, NA_HEADS * NA_HD)
BRANCH_ROWS = tuple(sum(BRANCH_WIDTHS[:i]) for i in range(N_BRANCH + 1))
IN_SIZES = (N_BRANCH * D_MODEL, MLA_Q_RANK, MLA_KV_RANK, MLA_ROPE,
            len(DIL_GROUPS) * 3 * DIL_HEADS * DIL_HD,
            SSM_INNER, CONV_CH, 2 * SSM_HEADS,
            3 * NA_HEADS * NA_HD)
IN_SPLITS = tuple(sum(IN_SIZES[:i + 1]) for i in range(len(IN_SIZES) - 1))

VMEM_LIMIT_BYTES = 56 * 1024 * 1024
LANES = 128
SUBLANES = 8


PEER_ROUTE_TB = 512
PEER_TB = 512
PEER_EC = 1024
PEER_GATE_DTYPE = BF16
PEER_GATE_ROWS = 16
PEER_HALF = PEER_QDIM // 2
PEER_CAND_ROWS = 2 * SUBLANES + 7 * SUBLANES + SUBLANES


def _gelu_exact_x2(x):
    return x * (1.0 + lax.erf(x * np.float32(math.sqrt(0.5))))


def _extract_desc(vals, n_out, out_ref, row0):
    for k in range(n_out):
        m = jnp.max(vals, axis=0, keepdims=True)
        out_ref[pl.ds(row0 + k, 1), :] = m
        vals = jnp.where(vals == m, -jnp.inf, vals)


def _oddeven_sort_network(lo, hi):
    def merge(lo, hi, r):
        step = r * 2
        if step < hi - lo:
            yield from merge(lo, hi, step)
            yield from merge(lo + r, hi, step)
            yield from ((i, i + r) for i in range(lo + r, hi - r, step))
        else:
            yield (lo, lo + r)

    if hi - lo >= 1:
        mid = lo + (hi - lo) // 2
        yield from _oddeven_sort_network(lo, mid)
        yield from _oddeven_sort_network(mid + 1, hi)
        yield from merge(lo, hi, 1)


def _bitonic_merge_network(n):
    d = n // 2
    while d >= 1:
        yield from ((i, i + d) for i in range(n) if (i // d) % 2 == 0)
        d //= 2


PEER_SORT_NET = tuple(_oddeven_sort_network(0, PEER_TOPK - 1))
PEER_MERGE_NET = tuple(_bitonic_merge_network(PEER_TOPK))


def _compare_exchange(tiles, network):
    for i, j in network:
        tiles[i], tiles[j] = jnp.maximum(tiles[i], tiles[j]), jnp.minimum(tiles[i], tiles[j])


def _sorted_top(s):
    assert s.shape[0] == PEER_TOPK * SUBLANES
    tiles = [s[i * SUBLANES:(i + 1) * SUBLANES, :] for i in range(PEER_TOPK)]
    _compare_exchange(tiles, PEER_SORT_NET)
    shift = SUBLANES // 2
    while shift >= 1:
        tiles = [jnp.maximum(tiles[i], pltpu.roll(tiles[PEER_TOPK - 1 - i], shift, axis=0))
                 for i in range(PEER_TOPK)]
        _compare_exchange(tiles, PEER_MERGE_NET)
        shift //= 2
    return tiles


def _peer_route_kernel(x_ref, g_ref, wqt_ref, keys_ref,
                       xnt_ref, cnt_ref, rank_ref, e1_ref, e2_ref,
                       qt_scr, top_scr, cand_scr, tops_scr):
    x = x_ref[...]
    xn = x * lax.rsqrt(jnp.mean(x * x, axis=-1, keepdims=True) + EPS) * g_ref[...]
    xnt = xn.T.astype(MXU_DTYPE)
    xnt_ref[...] = xnt
    qt_scr[...] = jnp.dot(wqt_ref[...], xnt, preferred_element_type=F32).astype(MXU_DTYPE)

    def head(h, carry):
        q1 = qt_scr[pl.ds(pl.multiple_of(h * PEER_QDIM, PEER_QDIM), PEER_HALF), :]
        q2 = qt_scr[pl.ds(pl.multiple_of(h * PEER_QDIM + PEER_HALF, PEER_HALF), PEER_HALF), :]
        s1_all = jnp.dot(keys_ref[2 * h], q1, preferred_element_type=F32)
        s2_all = jnp.dot(keys_ref[2 * h + 1], q2, preferred_element_type=F32)
        for l0 in range(0, s1_all.shape[1], LANES):
            lanes = slice(l0, l0 + LANES)
            s1, s2 = s1_all[:, lanes], s2_all[:, lanes]
            top1, top2 = _sorted_top(s1), _sorted_top(s2)
            for k in range(PEER_TOPK):
                top_scr[k:k + 1, lanes] = top1[k][0:1, :]
                top_scr[PEER_TOPK + k:PEER_TOPK + k + 1, lanes] = top2[k][0:1, :]
            t1 = top_scr[0:PEER_TOPK, lanes]
            t2 = top_scr[PEER_TOPK:2 * PEER_TOPK, lanes]
            cand_scr[0:2 * SUBLANES, lanes] = t1[0:1, :] + t2
            for a in range(1, SUBLANES):
                cand_scr[(a + 1) * SUBLANES:(a + 2) * SUBLANES, lanes] = t1[a:a + 1, :] + t2[0:SUBLANES, :]
            cand_scr[9 * SUBLANES:10 * SUBLANES, lanes] = t1[SUBLANES:2 * SUBLANES, :] + t2[0:1, :]
            _extract_desc(cand_scr[:, lanes], PEER_TOPK, tops_scr.at[:, lanes], 0)
            top_s = tops_scr[:, lanes]
            z = jnp.sum(jnp.exp(top_s - top_s[0:1, :]), axis=0, keepdims=True)
            tau = top_s[PEER_TOPK - 1:PEER_TOPK, :]
            rank_tiles = []
            for i in range(PEER_KEYS // SUBLANES):
                s2_tile = s2[i * SUBLANES:(i + 1) * SUBLANES, :]
                rank = jnp.zeros(s2_tile.shape, F32)
                for k in range(PEER_TOPK):
                    rank = rank + jnp.where(top2[k] > s2_tile, 1.0, 0.0)
                rank_tiles.append(rank)
            cnt_top = jnp.zeros(t1.shape, F32)
            for b in range(PEER_TOPK):
                cnt_top = cnt_top + jnp.where(t1 + t2[b:b + 1, :] >= tau, 1.0, 0.0)
            cnt = jnp.zeros(s1.shape, F32)
            for a in range(PEER_TOPK):
                cnt = jnp.where(s1 == t1[a:a + 1, :], cnt_top[a:a + 1, :], cnt)
            cnt_ref[h, :, lanes] = cnt
            rank_ref[h, :, lanes] = jnp.concatenate(rank_tiles, axis=0).astype(rank_ref.dtype)
            e1_ref[h, :, lanes] = jnp.exp(s1 - t1[0:1, :]) * (0.5 / z)
            e2_ref[h, :, lanes] = jnp.exp(s2 - t2[0:1, :]).astype(e2_ref.dtype)
        return carry

    lax.fori_loop(0, PEER_HEADS, head, 0)


def _peer_route(x2d, g, wqt, keys):
    n_tok = x2d.shape[0]
    tb = PEER_ROUTE_TB
    rt_shape = jax.ShapeDtypeStruct((PEER_HEADS, PEER_KEYS, n_tok), F32)
    gate_shape = jax.ShapeDtypeStruct((PEER_HEADS, PEER_KEYS, n_tok), PEER_GATE_DTYPE)
    rt_spec = pl.BlockSpec((PEER_HEADS, PEER_KEYS, tb), lambda i: (0, 0, i))
    return pl.pallas_call(
        _peer_route_kernel,
        grid=(n_tok // tb,),
        in_specs=[
            pl.BlockSpec((tb, D_MODEL), lambda i: (i, 0)),
            pl.BlockSpec((1, D_MODEL), lambda i: (0, 0)),
            pl.BlockSpec((PEER_HEADS * PEER_QDIM, D_MODEL), lambda i: (0, 0)),
            pl.BlockSpec((2 * PEER_HEADS, PEER_KEYS, PEER_HALF), lambda i: (0, 0, 0)),
        ],
        out_specs=[
            pl.BlockSpec((D_MODEL, tb), lambda i: (0, i)),
            rt_spec, rt_spec, rt_spec, rt_spec,
        ],
        out_shape=[
            jax.ShapeDtypeStruct((D_MODEL, n_tok), MXU_DTYPE),
            rt_shape, gate_shape, rt_shape, gate_shape,
        ],
        scratch_shapes=[
            pltpu.VMEM((PEER_HEADS * PEER_QDIM, tb), MXU_DTYPE),
            pltpu.VMEM((2 * PEER_TOPK, tb), F32),
            pltpu.VMEM((PEER_CAND_ROWS, tb), F32),
            pltpu.VMEM((PEER_TOPK, tb), F32),
        ],
        compiler_params=pltpu.CompilerParams(
            dimension_semantics=("arbitrary",), vmem_limit_bytes=VMEM_LIMIT_BYTES),
        name="peer_route",
    )(x2d, g, wqt, keys)


def _peer_expert_kernel(x_ref, xnt_ref, cnt_ref, rank_ref, e1_ref, e2_ref, u_ref, vt_ref, fg_ref,
                        out_ref, acc_scr, w_scr, *, final_norm):
    c = pl.program_id(1)
    n_chunks = pl.num_programs(1) - 1
    n_i1 = PEER_EC // PEER_KEYS
    tb = acc_scr.shape[1]
    gdt = PEER_GATE_DTYPE

    @pl.when(c == 0)
    def _():
        acc_scr[...] = jnp.zeros_like(acc_scr)
        w_scr[1] = jnp.zeros(w_scr.shape[1:], w_scr.dtype)

    slot = c % 2
    n_groups = PEER_KEYS // PEER_GATE_ROWS
    acc_scr[...] += jnp.dot(vt_ref[...], w_scr[(c + 1) % 2], preferred_element_type=F32)
    hid = jnp.dot(u_ref[...], xnt_ref[...], preferred_element_type=F32)

    def gate_block(i1l, l0):
        lanes = slice(l0, l0 + LANES)
        gates = [jnp.zeros((PEER_GATE_ROWS, LANES), gdt) for _ in range(n_groups)]
        for h in range(PEER_HEADS):
            cntb = jnp.broadcast_to(cnt_ref[h, i1l:i1l + 1, lanes], (PEER_GATE_ROWS, LANES)).astype(gdt)
            e1b = jnp.broadcast_to(e1_ref[h, i1l:i1l + 1, lanes], (PEER_GATE_ROWS, LANES)).astype(gdt)
            for k in range(n_groups):
                rows = slice(k * PEER_GATE_ROWS, (k + 1) * PEER_GATE_ROWS)
                w = e2_ref[h, rows, lanes] * e1b
                gates[k] = gates[k] + jnp.where(rank_ref[h, rows, lanes] < cntb, w, jnp.zeros_like(w))
        for k in range(n_groups):
            r0 = k * PEER_GATE_ROWS
            e0 = i1l * PEER_KEYS + r0
            act = _gelu_exact_x2(hid[e0:e0 + PEER_GATE_ROWS, lanes]).astype(gdt)
            w_scr[slot, e0:e0 + PEER_GATE_ROWS, lanes] = (act * gates[k]).astype(MXU_DTYPE)

    for i1l in range(n_i1):
        for l0 in range(0, tb, LANES):
            gate_block(i1l, l0)

    @pl.when(c == n_chunks)
    def _():
        y = x_ref[...] + acc_scr[...].T
        if final_norm:
            y = y * lax.rsqrt(jnp.mean(y * y, axis=-1, keepdims=True) + EPS) * fg_ref[...]
        out_ref[...] = y


def _peer_experts(x2d, xnt, cnt, rank2, e1, e2, u, vt, final_g, final_norm):
    n_tok = x2d.shape[0]
    tb, ec = PEER_TB, PEER_EC
    n_chunks = PEER_EXPERTS // ec
    rt_spec = pl.BlockSpec((PEER_HEADS, PEER_KEYS, tb), lambda j, c: (0, 0, j))
    row_spec = pl.BlockSpec((PEER_HEADS, ec // PEER_KEYS, tb),
                            lambda j, c: (0, jnp.minimum(c, n_chunks - 1), j))
    return pl.pallas_call(
        functools.partial(_peer_expert_kernel, final_norm=final_norm),
        grid=(n_tok // tb, n_chunks + 1),
        in_specs=[
            pl.BlockSpec((tb, D_MODEL), lambda j, c: (j, 0)),
            pl.BlockSpec((D_MODEL, tb), lambda j, c: (0, j)),
            row_spec, rt_spec, row_spec, rt_spec,
            pl.BlockSpec((ec, D_MODEL), lambda j, c: (jnp.minimum(c, n_chunks - 1), 0)),
            pl.BlockSpec((D_MODEL, ec), lambda j, c: (0, jnp.maximum(c - 1, 0))),
            pl.BlockSpec((1, D_MODEL), lambda j, c: (0, 0)),
        ],
        out_specs=pl.BlockSpec((tb, D_MODEL), lambda j, c: (j, 0)),
        out_shape=jax.ShapeDtypeStruct((n_tok, D_MODEL), F32),
        scratch_shapes=[
            pltpu.VMEM((D_MODEL, tb), F32),
            pltpu.VMEM((2, ec, tb), MXU_DTYPE),
        ],
        compiler_params=pltpu.CompilerParams(
            dimension_semantics=("arbitrary", "arbitrary"), vmem_limit_bytes=VMEM_LIMIT_BYTES),
        name="peer_experts",
    )(x2d, xnt, cnt, rank2, e1, e2, u, vt, final_g.reshape(1, D_MODEL))


def peer_block(x2d, norm_g, w_q, keys, u, v, final_g, final_norm=False):
    wqt = w_q.T.astype(MXU_DTYPE)
    keys2 = keys.reshape(2 * PEER_HEADS, PEER_KEYS, PEER_HALF).astype(MXU_DTYPE)
    xnt, cnt, rank2, e1, e2 = _peer_route(x2d, norm_g.reshape(1, D_MODEL), wqt, keys2)
    return _peer_experts(x2d, xnt, cnt, rank2, e1, e2, u, v, final_g, final_norm)


MLA_HG = LANES
MLA_QK_W = MLA_HEADS * MLA_HG
MLA_V_W = MLA_HEADS * MLA_V
MLA_PREP_TOK = 512
MLA_TQ = 512
MLA_TK = 2048
MLA_RHALF = MLA_ROPE // 2


def _mla_rope_tables(s):
    inv = ROPE_THETA ** (-jnp.arange(MLA_RHALF, dtype=F32) / MLA_RHALF)
    ang = jnp.arange(s).astype(F32)[:, None] * inv[None, :]
    cos, sin = jnp.cos(ang), jnp.sin(ang)
    zero_pad = jnp.zeros((s, MLA_HG - MLA_NOPE - MLA_ROPE), F32)
    cos_rot = jnp.concatenate([cos, cos, zero_pad], axis=1)
    sin_rot = jnp.concatenate([-sin, sin, zero_pad], axis=1)
    scale = np.float32((MLA_NOPE + MLA_ROPE) ** -0.5)
    q_cos = scale * jnp.concatenate([jnp.ones((s, MLA_NOPE), F32), cos_rot], axis=1)
    q_sin = scale * jnp.concatenate([jnp.zeros((s, MLA_NOPE), F32), sin_rot], axis=1)
    k_cos = jnp.concatenate([jnp.zeros((s, MLA_NOPE), F32), cos_rot], axis=1)
    k_sin = jnp.concatenate([jnp.zeros((s, MLA_NOPE), F32), sin_rot], axis=1)
    return q_cos, q_sin, k_cos, k_sin


def _mla_pack_weights(w_qb, w_kvb):
    hd_q = MLA_NOPE + MLA_ROPE
    wq = w_qb.reshape(MLA_Q_RANK, MLA_HEADS, hd_q)
    rot = wq[:, :, MLA_NOPE:]
    rot_sw = jnp.concatenate([rot[:, :, MLA_RHALF:], rot[:, :, :MLA_RHALF]], axis=2)
    pad = jnp.zeros((MLA_Q_RANK, MLA_HEADS, MLA_HG - hd_q), F32)
    wq_a = jnp.concatenate([wq, pad], axis=2).reshape(MLA_Q_RANK, MLA_QK_W)
    wq_b = jnp.concatenate([jnp.zeros_like(wq[:, :, :MLA_NOPE]), rot_sw, pad], axis=2).reshape(MLA_Q_RANK, MLA_QK_W)
    wkv = w_kvb.reshape(MLA_KV_RANK, MLA_HEADS, MLA_NOPE + MLA_V)
    wk = jnp.concatenate([wkv[:, :, :MLA_NOPE], jnp.zeros((MLA_KV_RANK, MLA_HEADS, MLA_HG - MLA_NOPE), F32)],
                         axis=2).reshape(MLA_KV_RANK, MLA_QK_W)
    wv = jnp.concatenate([wkv[:, :, MLA_NOPE:], jnp.zeros((MLA_KV_RANK, MLA_HEADS, MLA_HG - MLA_V), F32)],
                         axis=2).reshape(MLA_KV_RANK, MLA_QK_W)
    return (wq_a.astype(MXU_DTYPE), wq_b.astype(MXU_DTYPE), wk.astype(MXU_DTYPE), wv.astype(MXU_DTYPE))


def _mla_prep_kernel(cq_ref, ckv_ref, kr_ref, qn_ref, kvn_ref, wqa_ref, wqb_ref, wk_ref, wv_ref,
                     qcos_ref, qsin_ref, kcos_ref, ksin_ref, q_out, kt_out, v_out):
    cq = cq_ref[0]
    cqn = (cq * lax.rsqrt(jnp.mean(cq * cq, axis=-1, keepdims=True) + EPS) * qn_ref[...]).astype(MXU_DTYPE)
    qa = jnp.dot(cqn, wqa_ref[...], preferred_element_type=F32)
    qb = jnp.dot(cqn, wqb_ref[...], preferred_element_type=F32)
    ckv = ckv_ref[0]
    ckvn = (ckv * lax.rsqrt(jnp.mean(ckv * ckv, axis=-1, keepdims=True) + EPS) * kvn_ref[...]).astype(MXU_DTYPE)
    ka = jnp.dot(ckvn, wk_ref[...], preferred_element_type=F32)
    one_lane = (lax.broadcasted_iota(jnp.int32, (1, MLA_QK_W), 1) % MLA_HG == MLA_V).astype(F32)
    v_out[0] = (jnp.dot(ckvn, wv_ref[...], preferred_element_type=F32) + one_lane).astype(MXU_DTYPE)
    kr = kr_ref[0]
    k_rot = kr[:, 0:MLA_HG] * kcos_ref[...] + kr[:, MLA_HG:2 * MLA_HG] * ksin_ref[...]
    for h in range(MLA_HEADS):
        lanes = slice(h * MLA_HG, (h + 1) * MLA_HG)
        q_out[0, :, lanes] = (qa[:, lanes] * qcos_ref[...] + qb[:, lanes] * qsin_ref[...]).astype(MXU_DTYPE)
        kt_out[0, lanes, :] = (ka[:, lanes] + k_rot).T.astype(MXU_DTYPE)


def _mla_flash_kernel(q_ref, kt_ref, v_ref, out_ref, m_scr, acc_scr):
    ki = pl.program_id(2)

    @pl.when(ki == 0)
    def _():
        m_scr[...] = jnp.full(m_scr.shape, -jnp.inf, F32)
        acc_scr[...] = jnp.zeros(acc_scr.shape, F32)

    def logits(h):
        lanes = slice(h * MLA_HG, (h + 1) * MLA_HG)
        return jnp.dot(q_ref[0, :, lanes], kt_ref[0, lanes, :], preferred_element_type=F32)

    n_rep = kt_ref.shape[2] // MLA_HG
    s_next = logits(0)
    for h in range(MLA_HEADS):
        s = s_next
        if h + 1 < MLA_HEADS:
            s_next = logits(h + 1)
        m_old = m_scr[h]
        m_new = jnp.maximum(m_old, jnp.max(s, axis=-1, keepdims=True))
        p = jnp.exp(s - jnp.tile(m_new, (1, n_rep))).astype(MXU_DTYPE)
        acc_scr[h] = jnp.exp(m_old - m_new) * acc_scr[h] + jnp.dot(
            p, v_ref[0, :, h * MLA_HG:(h + 1) * MLA_HG], preferred_element_type=F32)
        m_scr[h] = m_new

    @pl.when(ki == pl.num_programs(2) - 1)
    def _():
        low = lax.broadcasted_iota(jnp.int32, (acc_scr.shape[1], MLA_HG), 1) < MLA_V
        outs = []
        for h in range(MLA_HEADS):
            acc = acc_scr[h]
            outs.append(acc / acc[:, MLA_V:MLA_V + 1])
        for hp in range(MLA_HEADS // 2):
            odd = pltpu.roll(outs[2 * hp + 1], MLA_V, axis=1)
            out_ref[0, :, hp * MLA_HG:(hp + 1) * MLA_HG] = jnp.where(low, outs[2 * hp], odd)


def mla_rotary_key_columns(w):
    zl = jnp.zeros(w.shape[:-1] + (MLA_NOPE,), w.dtype)
    zr = jnp.zeros(w.shape[:-1] + (MLA_HG - MLA_NOPE - MLA_ROPE,), w.dtype)
    w_sw = jnp.concatenate([w[..., MLA_RHALF:], w[..., :MLA_RHALF]], axis=-1)
    return jnp.concatenate([zl, w, zr, zl, w_sw, zr], axis=-1)


def mla_mixer(arr, q_norm, w_qb, kv_norm, w_kvb, col_q=0, col_kv=MLA_Q_RANK, col_kr=MLA_Q_RANK + MLA_KV_RANK):
    b, s, _ = arr.shape
    tt = MLA_PREP_TOK
    assert col_q % MLA_Q_RANK == 0 and col_kv % MLA_KV_RANK == 0 and col_kr % (2 * MLA_HG) == 0
    wqa, wqb, wk, wv = _mla_pack_weights(w_qb, w_kvb)
    q_cos, q_sin, k_cos, k_sin = _mla_rope_tables(s)
    tok = lambda w, c=0: pl.BlockSpec((1, tt, w), lambda i, j: (i, j, c // w))
    full = lambda r, c: pl.BlockSpec((r, c), lambda i, j: (0, 0))
    tab = pl.BlockSpec((tt, MLA_HG), lambda i, j: (j, 0))
    q, kt, v = pl.pallas_call(
        _mla_prep_kernel,
        grid=(b, s // tt),
        in_specs=[tok(MLA_Q_RANK, col_q), tok(MLA_KV_RANK, col_kv), tok(2 * MLA_HG, col_kr),
                  full(1, MLA_Q_RANK), full(1, MLA_KV_RANK),
                  full(MLA_Q_RANK, MLA_QK_W), full(MLA_Q_RANK, MLA_QK_W),
                  full(MLA_KV_RANK, MLA_QK_W), full(MLA_KV_RANK, MLA_QK_W),
                  tab, tab, tab, tab],
        out_specs=[tok(MLA_QK_W), pl.BlockSpec((1, MLA_QK_W, tt), lambda i, j: (i, 0, j)), tok(MLA_QK_W)],
        out_shape=[jax.ShapeDtypeStruct((b, s, MLA_QK_W), MXU_DTYPE),
                   jax.ShapeDtypeStruct((b, MLA_QK_W, s), MXU_DTYPE),
                   jax.ShapeDtypeStruct((b, s, MLA_QK_W), MXU_DTYPE)],
        compiler_params=pltpu.CompilerParams(
            dimension_semantics=("arbitrary", "arbitrary"), vmem_limit_bytes=VMEM_LIMIT_BYTES),
        name="mla_prep",
    )(arr, arr, arr, q_norm.reshape(1, -1), kv_norm.reshape(1, -1), wqa, wqb, wk, wv,
      q_cos, q_sin, k_cos, k_sin)
    tq, tk = min(MLA_TQ, s), min(MLA_TK, s)
    assert s % tq == 0 and s % tk == 0 and s % tt == 0
    return pl.pallas_call(
        _mla_flash_kernel,
        grid=(b, s // tq, s // tk),
        in_specs=[pl.BlockSpec((1, tq, MLA_QK_W), lambda i, j, kk: (i, j, 0)),
                  pl.BlockSpec((1, MLA_QK_W, tk), lambda i, j, kk: (i, 0, kk)),
                  pl.BlockSpec((1, tk, MLA_QK_W), lambda i, j, kk: (i, kk, 0))],
        out_specs=pl.BlockSpec((1, tq, MLA_V_W), lambda i, j, kk: (i, j, 0)),
        out_shape=jax.ShapeDtypeStruct((b, s, MLA_V_W), F32),
        scratch_shapes=[pltpu.VMEM((MLA_HEADS, tq, MLA_HG), F32),
                        pltpu.VMEM((MLA_HEADS, tq, MLA_HG), F32)],
        compiler_params=pltpu.CompilerParams(
            dimension_semantics=("arbitrary", "arbitrary", "arbitrary"), vmem_limit_bytes=VMEM_LIMIT_BYTES),
        name="mla_flash",
    )(q, kt, v)


def t5_bucket(rel):
    nb = T5_BUCKETS // 2
    ret = np.where(rel > 0, nb, 0)
    n = np.abs(rel)
    max_exact = nb // 2
    large = max_exact + (np.log(np.maximum(n, 1) / max_exact) / np.log(T5_MAX_DIST / max_exact)
                         * (nb - max_exact)).astype(np.int64)
    large = np.minimum(large, nb - 1)
    return (ret + np.where(n < max_exact, n, large)).astype(np.int32)


DIL_DIM = DIL_HEADS * DIL_HD
DIL_HALF = 64
DIL_QB = 128
DIL_KW = DIL_QB + 2 * DIL_HALF
DIL_TL = 512
DIL_N_GROUPS = len(DIL_GROUPS)


def _dil_bias_table(t5_table, gi, dil):
    rel = np.arange(DIL_KW)[None, :] - DIL_HALF - np.arange(DIL_QB)[:, None]
    bias = t5_table[:, gi * DIL_HEADS:(gi + 1) * DIL_HEADS][t5_bucket(rel * dil)].astype(F32)
    bias = jnp.where((np.abs(rel) <= DIL_HALF)[:, :, None], bias, -jnp.inf)
    return bias.transpose(2, 0, 1).reshape(DIL_HEADS * DIL_QB, DIL_KW)


def _dil_kernel(prev_ref, cur_ref, next_ref, tab_ref, o_ref, lse_ref, k_scr, v_scr, *, seq_len):
    step = pl.program_id(2)
    for i, ref in enumerate((prev_ref, cur_ref, next_ref)):
        k_scr[i * DIL_TL:(i + 1) * DIL_TL, :] = ref[0, :, DIL_DIM:2 * DIL_DIM].astype(MXU_DTYPE)
        v_scr[i * DIL_TL:(i + 1) * DIL_TL, :] = ref[0, :, 2 * DIL_DIM:3 * DIL_DIM].astype(MXU_DTYPE)
    lane_head = lax.broadcasted_iota(jnp.int32, (DIL_QB, DIL_DIM), 1) // DIL_HD
    key_off = lax.broadcasted_iota(jnp.int32, (1, DIL_KW), 1)
    for n in range(DIL_TL // DIL_QB):
        w0 = DIL_TL + n * DIL_QB - DIL_HALF
        kpos = step * DIL_TL + (n * DIL_QB - DIL_HALF) + key_off
        valid = (kpos >= 0) & (kpos < seq_len)
        q = cur_ref[0, n * DIL_QB:(n + 1) * DIL_QB, 0:DIL_DIM] * np.float32(DIL_HD ** -0.5)
        qs = jnp.concatenate([jnp.where(lane_head == h, q, 0.0) for h in range(DIL_HEADS)], axis=0)
        logits = lax.dot_general(qs.astype(MXU_DTYPE), k_scr[w0:w0 + DIL_KW, :], (((1,), (1,)), ((), ())),
                                 preferred_element_type=F32) + tab_ref[...]
        logits = jnp.where(valid, logits, -jnp.inf)
        m = jnp.max(logits, axis=-1, keepdims=True)
        p = jnp.exp(logits - m)
        denom = jnp.sum(p, axis=-1, keepdims=True)
        o_all = jnp.dot(p.astype(MXU_DTYPE), v_scr[w0:w0 + DIL_KW, :], preferred_element_type=F32) / denom
        lse_all = m + jnp.log(denom)
        o = jnp.zeros((DIL_QB, DIL_DIM), F32)
        lse = jnp.zeros((DIL_QB, DIL_DIM), F32)
        for h in range(DIL_HEADS):
            rows = slice(h * DIL_QB, (h + 1) * DIL_QB)
            o = o + jnp.where(lane_head == h, o_all[rows, :], 0.0)
            lse = lse + jnp.where(lane_head == h, lse_all[rows, :], 0.0)
        o_ref[0, n * DIL_QB:(n + 1) * DIL_QB, :] = o
        lse_ref[0, n * DIL_QB:(n + 1) * DIL_QB, :] = lse


def _dil_group(qkv, t5_table, gi, dil, col):
    b, s, width = qkv.shape
    seq_len = s // dil
    assert seq_len % DIL_TL == 0 and col % (3 * DIL_DIM) == 0
    n_steps = seq_len // DIL_TL
    gi_col = col // (3 * DIL_DIM) + gi
    if dil > 1:
        qkv = qkv[:, :, gi_col * 3 * DIL_DIM:(gi_col + 1) * 3 * DIL_DIM]
        width, gi_col = 3 * DIL_DIM, 0
    n_col = width // (3 * DIL_DIM)
    view = qkv.reshape(b, seq_len, dil * width)
    blk = (1, DIL_TL, 3 * DIL_DIM)
    out_shape = jax.ShapeDtypeStruct((b, seq_len, dil * DIL_DIM), F32)
    out_spec = pl.BlockSpec((1, DIL_TL, DIL_DIM), lambda i, r, j: (i, j, r))
    o, lse = pl.pallas_call(
        functools.partial(_dil_kernel, seq_len=seq_len),
        grid=(b, dil, n_steps),
        in_specs=[
            pl.BlockSpec(blk, lambda i, r, j: (i, jnp.maximum(j - 1, 0), r * n_col + gi_col)),
            pl.BlockSpec(blk, lambda i, r, j: (i, j, r * n_col + gi_col)),
            pl.BlockSpec(blk, lambda i, r, j: (i, jnp.minimum(j + 1, n_steps - 1), r * n_col + gi_col)),
            pl.BlockSpec((DIL_HEADS * DIL_QB, DIL_KW), lambda i, r, j: (0, 0)),
        ],
        out_specs=[out_spec, out_spec],
        out_shape=[out_shape, out_shape],
        scratch_shapes=[pltpu.VMEM((3 * DIL_TL, DIL_DIM), MXU_DTYPE),
                        pltpu.VMEM((3 * DIL_TL, DIL_DIM), MXU_DTYPE)],
        compiler_params=pltpu.CompilerParams(
            dimension_semantics=("arbitrary", "arbitrary", "arbitrary"), vmem_limit_bytes=VMEM_LIMIT_BYTES),
        name=f"dilated_attention_g{gi}",
    )(view, view, view, _dil_bias_table(t5_table, gi, dil))
    return o.reshape(b, s, DIL_DIM), lse.reshape(b, s, DIL_DIM)


def _dil_combine_kernel(*refs):
    o_refs, lse_refs, out_ref = refs[:DIL_N_GROUPS], refs[DIL_N_GROUPS:2 * DIL_N_GROUPS], refs[-1]
    lses = [r[...] for r in lse_refs]
    m = functools.reduce(jnp.maximum, lses)
    ws = [jnp.exp(l - m) for l in lses]
    total = functools.reduce(jnp.add, ws)
    acc = functools.reduce(jnp.add, [w * r[...] for w, r in zip(ws, o_refs)])
    out_ref[...] = acc / total


def dilated_mixer(qkv, t5_table, col=0):
    b, s, _ = qkv.shape
    outs, lses = [], []
    for gi, (win, dil) in enumerate(DIL_GROUPS):
        assert win // (2 * dil) == DIL_HALF
        o, lse = _dil_group(qkv, t5_table, gi, dil, col)
        outs.append(o.reshape(b * s, DIL_DIM))
        lses.append(lse.reshape(b * s, DIL_DIM))
    tm = 1024
    spec = pl.BlockSpec((tm, DIL_DIM), lambda i: (i, 0))
    out = pl.pallas_call(
        _dil_combine_kernel,
        grid=(b * s // tm,),
        in_specs=[spec] * (2 * DIL_N_GROUPS),
        out_specs=spec,
        out_shape=jax.ShapeDtypeStruct((b * s, DIL_DIM), F32),
        compiler_params=pltpu.CompilerParams(
            dimension_semantics=("arbitrary",), vmem_limit_bytes=VMEM_LIMIT_BYTES),
        name="dilated_combine",
    )(*outs, *lses)
    return out.reshape(b, s, DIL_DIM)


SSM_CONV_TOK = 512
SSM_HALO = SUBLANES
SSM_HEADS_PER_GROUP = SSM_HEADS // SSM_GROUPS
SSM_GROUP_W = SSM_HEADS_PER_GROUP * SSM_HD
SSM_BC_W = SSM_GROUPS * SSM_STATE


def _softplus(x):
    return jnp.maximum(x, 0.0) + jnp.log1p(jnp.exp(-jnp.abs(x)))


def _ssm_conv_kernel(prev_ref, cur_ref, next_ref, w_ref, b_ref, out_ref, cat_scr):
    j = pl.program_id(1)
    tl = cur_ref.shape[1]
    cat_scr[0:SSM_HALO, :] = jnp.where(j > 0, prev_ref[0], 0.0)
    cat_scr[SSM_HALO:SSM_HALO + tl, :] = cur_ref[0]
    cat_scr[SSM_HALO + tl:, :] = jnp.where(j < pl.num_programs(1) - 1, next_ref[0], 0.0)
    acc = jnp.zeros((tl, CONV_CH), F32) + b_ref[...]
    for k in range(SSM_CONV):
        off = SSM_HALO + k - SSM_CONV // 2
        acc = acc + cat_scr[off:off + tl, :] * w_ref[k:k + 1, :]
    out_ref[0] = acc * jax.nn.sigmoid(acc)


def _ssm_conv(xbc, conv_w, conv_b, col):
    b, l, _ = xbc.shape
    tl = SSM_CONV_TOK
    n_steps = l // tl
    per = tl // SSM_HALO
    assert col % CONV_CH == 0
    cb = col // CONV_CH
    return pl.pallas_call(
        _ssm_conv_kernel,
        grid=(b, n_steps),
        in_specs=[
            pl.BlockSpec((1, SSM_HALO, CONV_CH), lambda i, j: (i, jnp.maximum(j * per - 1, 0), cb)),
            pl.BlockSpec((1, tl, CONV_CH), lambda i, j: (i, j, cb)),
            pl.BlockSpec((1, SSM_HALO, CONV_CH), lambda i, j: (i, jnp.minimum((j + 1) * per, n_steps * per - 1), cb)),
            pl.BlockSpec((SSM_CONV, CONV_CH), lambda i, j: (0, 0)),
            pl.BlockSpec((1, CONV_CH), lambda i, j: (0, 0)),
        ],
        out_specs=pl.BlockSpec((1, tl, CONV_CH), lambda i, j: (i, j, 0)),
        out_shape=jax.ShapeDtypeStruct((b, l, CONV_CH), F32),
        scratch_shapes=[pltpu.VMEM((tl + 2 * SSM_HALO, CONV_CH), F32)],
        compiler_params=pltpu.CompilerParams(
            dimension_semantics=("arbitrary", "arbitrary"), vmem_limit_bytes=VMEM_LIMIT_BYTES),
        name="ssm_conv",
    )(xbc, xbc, xbc, conv_w, conv_b.reshape(1, CONV_CH))


def _ssd_kernel(xf_ref, dtf_ref, dttf_ref, xb_ref, dtb_ref, dttb_ref,
                expf_ref, bef_ref, aef_ref, btf_ref, atf_ref, expb_ref, beb_ref, aeb_ref, btb_ref, atb_ref,
                yf_ref, yb_ref, state_scr):
    @pl.when(pl.program_id(1) == 0)
    def _():
        state_scr[...] = jnp.zeros(state_scr.shape, F32)

    fwd = _ssd_chunk(xf_ref, dtf_ref, dttf_ref, expf_ref, bef_ref, aef_ref, btf_ref, atf_ref,
                     yf_ref, state_scr.at[0], reverse=False)
    bwd = _ssd_chunk(xb_ref, dtb_ref, dttb_ref, expb_ref, beb_ref, aeb_ref, btb_ref, atb_ref,
                     yb_ref, state_scr.at[1], reverse=True)
    for _ in zip(fwd, bwd):
        pass
    for _ in fwd:
        pass
    for _ in bwd:
        pass


def _ssd_chunk(xbc_ref, dt_ref, dtt_ref, expand_ref, bias_e_ref, a_e_ref, bias_t_ref, a_t_ref,
               y_ref, state_scr, *, reverse):
    q = SSM_CHUNK
    hi = lax.Precision.HIGHEST
    xbc = xbc_ref[0]
    xs = xbc[:, 0:SSM_INNER]
    dt_e = _softplus(jnp.dot(dt_ref[0], expand_ref[...], precision=hi, preferred_element_type=F32)
                     + bias_e_ref[...])
    a_e = dt_e * a_e_ref[...]
    ri = lax.broadcasted_iota(jnp.int32, (q, q), 0)
    ci = lax.broadcasted_iota(jnp.int32, (q, q), 1)
    seen = (ci >= ri) if reverse else (ci <= ri)
    cs_e = jnp.dot(seen.astype(F32), a_e, precision=hi, preferred_element_type=F32)
    dt_t = _softplus(dtt_ref[0] + bias_t_ref[...])
    cs_t = jnp.dot(dt_t * a_t_ref[...], seen.T.astype(F32), precision=hi, preferred_element_type=F32)
    yield
    dtx = xs * dt_e
    last = 0 if reverse else q - 1
    total = cs_e[last:last + 1, :]
    dtx_decayed = jnp.exp(total - cs_e) * dtx
    grow = jnp.exp(cs_e)
    chunk_decay = jnp.exp(total)
    lane_head = lax.broadcasted_iota(jnp.int32, (q, SSM_GROUP_W), 1) // SSM_HD
    for g in range(SSM_GROUPS):
        xl = slice(g * SSM_GROUP_W, (g + 1) * SSM_GROUP_W)
        bg = xbc[:, SSM_INNER + g * SSM_STATE:SSM_INNER + (g + 1) * SSM_STATE]
        cg = xbc[:, SSM_INNER + SSM_BC_W + g * SSM_STATE:SSM_INNER + SSM_BC_W + (g + 1) * SSM_STATE]
        cb = lax.dot_general(cg.astype(MXU_DTYPE), bg.astype(MXU_DTYPE), (((1,), (1,)), ((), ())),
                             preferred_element_type=F32)
        yield
        ms = []
        for r in range(SSM_HEADS_PER_GROUP):
            h = g * SSM_HEADS_PER_GROUP + r
            col = cs_e[:, h * SSM_HD:h * SSM_HD + 1]
            row = cs_t[h:h + 1, :]
            ms.append(cb * jnp.where(seen, jnp.exp(col - row), 0.0))
        y_all = jnp.dot(jnp.concatenate(ms, axis=0).astype(MXU_DTYPE), dtx[:, xl].astype(MXU_DTYPE),
                        preferred_element_type=F32)
        yield
        y_diag = jnp.zeros((q, SSM_GROUP_W), F32)
        for r in range(SSM_HEADS_PER_GROUP):
            y_diag = y_diag + jnp.where(lane_head == r, y_all[r * q:(r + 1) * q, :], 0.0)
        s_in = state_scr[g]
        y_off = jnp.dot(cg.astype(MXU_DTYPE), s_in.astype(MXU_DTYPE), preferred_element_type=F32) * grow[:, xl]
        y_ref[0, :, xl] = y_diag + y_off
        yield
        new = jnp.dot(bg.T.astype(MXU_DTYPE), dtx_decayed[:, xl].astype(MXU_DTYPE), preferred_element_type=F32)
        state_scr[g] = s_in * chunk_decay[:, xl] + new


def _ssd_scans(xbc_act, dt_arr, dt_t, a_log, dt_bias, col_dt, dt_w):
    b, l, _ = xbc_act.shape
    assert col_dt % dt_w == 0 and dt_w >= 2 * SSM_HEADS
    q = SSM_CHUNK
    nc = l // q
    head_of_lane = np.arange(SSM_INNER) // SSM_HD
    full = lambda r, c: pl.BlockSpec((r, c), lambda i, j: (0, 0))
    chunk = (lambda j: j, lambda j: nc - 1 - j)
    data_specs, param_specs, params = [], [], []
    for d in range(2):
        a = -jnp.exp(a_log[d].astype(F32))
        bias = dt_bias[d].astype(F32)
        expand = (np.arange(dt_w)[:, None] == d * SSM_HEADS + head_of_lane[None, :]).astype(np.float32)
        data_specs += [
            pl.BlockSpec((1, q, CONV_CH), lambda i, j, d=d: (i, chunk[d](j), 0)),
            pl.BlockSpec((1, q, dt_w), lambda i, j, d=d: (i, chunk[d](j), col_dt // dt_w)),
            pl.BlockSpec((1, SSM_HEADS, q), lambda i, j, d=d: (i * 2 + d, 0, chunk[d](j))),
        ]
        param_specs += [full(dt_w, SSM_INNER), full(1, SSM_INNER), full(1, SSM_INNER),
                        full(SSM_HEADS, 1), full(SSM_HEADS, 1)]
        params += [jnp.asarray(expand), bias[head_of_lane].reshape(1, SSM_INNER),
                   a[head_of_lane].reshape(1, SSM_INNER), bias.reshape(SSM_HEADS, 1), a.reshape(SSM_HEADS, 1)]
    y_shape = jax.ShapeDtypeStruct((b, l, SSM_INNER), F32)
    return pl.pallas_call(
        _ssd_kernel,
        grid=(b, nc),
        in_specs=data_specs + param_specs,
        out_specs=[pl.BlockSpec((1, q, SSM_INNER), lambda i, j, d=d: (i, chunk[d](j), 0)) for d in range(2)],
        out_shape=[y_shape, y_shape],
        scratch_shapes=[pltpu.VMEM((2, SSM_GROUPS, SSM_STATE, SSM_GROUP_W), F32)],
        compiler_params=pltpu.CompilerParams(
            dimension_semantics=("arbitrary", "arbitrary"), vmem_limit_bytes=VMEM_LIMIT_BYTES),
        name="ssd_scans",
    )(xbc_act, dt_arr, dt_t, xbc_act, dt_arr, dt_t, *params)


def _ssm_gate_kernel(yf_ref, yb_ref, xbc_ref, z_ref, d_ref, g_ref, out_ref):
    z = z_ref[...]
    y = (yf_ref[...] + yb_ref[...] + xbc_ref[...] * d_ref[...]) * (z * jax.nn.sigmoid(z))
    out_ref[...] = y * lax.rsqrt(jnp.mean(y * y, axis=-1, keepdims=True) + EPS) * g_ref[...]


def mamba2_mixer(arr, conv_w, conv_b, A_log, dt_bias, D_skip, norm_g,
                 col_z=0, col_xbc=CONV_CH, col_dt=SSM_INNER + CONV_CH, dt_w=2 * SSM_HEADS):
    b, l, width = arr.shape
    assert col_z % SSM_INNER == 0
    xbc_act = _ssm_conv(arr, conv_w, conv_b, col_xbc)
    dt_t = jnp.swapaxes(arr[:, :, col_dt:col_dt + 2 * SSM_HEADS], 1, 2).reshape(b * 2, SSM_HEADS, l)
    y_f, y_b = _ssd_scans(xbc_act, arr, dt_t, A_log, dt_bias, col_dt, dt_w)
    tm = 1024
    tok = lambda cb: pl.BlockSpec((tm, SSM_INNER), lambda i: (i, cb))
    row = pl.BlockSpec((1, SSM_INNER), lambda i: (0, 0))
    d_e = D_skip.astype(F32)[np.arange(SSM_INNER) // SSM_HD].reshape(1, SSM_INNER)
    out = pl.pallas_call(
        _ssm_gate_kernel,
        grid=(b * l // tm,),
        in_specs=[tok(0), tok(0), tok(0), tok(col_z // SSM_INNER), row, row],
        out_specs=tok(0),
        out_shape=jax.ShapeDtypeStruct((b * l, SSM_INNER), F32),
        compiler_params=pltpu.CompilerParams(
            dimension_semantics=("arbitrary",), vmem_limit_bytes=VMEM_LIMIT_BYTES),
        name="ssm_gate",
    )(y_f.reshape(b * l, SSM_INNER), y_b.reshape(b * l, SSM_INNER), xbc_act.reshape(b * l, CONV_CH),
      arr.reshape(b * l, width), d_e, norm_g.reshape(1, SSM_INNER))
    return out.reshape(b, l, SSM_INNER)


NA_DIM = NA_HEADS * NA_HD
NA_ROWS_PER_STEP = 8
NA_WIN = NA_ROWS * GRID_W
NA_STEP_TOK = NA_ROWS_PER_STEP * GRID_W


def _na_bias_table(rpb):
    n_dc = 2 * NA_COLS - 1
    edge_l = jnp.repeat(rpb[:, :, :1], GRID_W, axis=2)
    edge_r = jnp.repeat(rpb[:, :, -1:], GRID_W, axis=2)
    ext = jnp.concatenate([edge_l, rpb.astype(F32), edge_r], axis=2)
    by_col = jnp.stack([ext[:, :, GRID_W + NA_COLS - 1 - qc:2 * GRID_W + NA_COLS - 1 - qc]
                        for qc in range(GRID_W)], axis=2)
    qc = np.arange(GRID_W)[:, None]
    kc = np.arange(GRID_W)[None, :]
    cs = np.clip(qc - NA_COLS // 2, 0, GRID_W - NA_COLS)
    ok = (kc >= cs) & (kc < cs + NA_COLS)
    by_col = jnp.where(ok[None, None], by_col, -jnp.inf)
    tabs = []
    for delta in range(NA_ROWS):
        rows = by_col[:, NA_ROWS - 1 - delta:2 * NA_ROWS - 1 - delta]
        tabs.append(rows.transpose(0, 2, 1, 3).reshape(NA_HEADS * GRID_W, NA_WIN))
    assert n_dc == rpb.shape[2]
    return jnp.stack(tabs, axis=0)


def _na_kernel(prev_ref, cur_ref, next_ref, tab_ref, out_ref, k_scr, v_scr, *, n_rows):
    step = pl.program_id(1)
    for i, ref in enumerate((prev_ref, cur_ref, next_ref)):
        k_scr[i * NA_STEP_TOK:(i + 1) * NA_STEP_TOK, :] = ref[0, :, NA_DIM:2 * NA_DIM].astype(MXU_DTYPE)
        v_scr[i * NA_STEP_TOK:(i + 1) * NA_STEP_TOK, :] = ref[0, :, 2 * NA_DIM:3 * NA_DIM].astype(MXU_DTYPE)
    lane_head = lax.broadcasted_iota(jnp.int32, (GRID_W, NA_DIM), 1) // NA_HD
    row0 = step * NA_ROWS_PER_STEP
    for j in range(NA_ROWS_PER_STEP):
        r = row0 + j
        r0 = jnp.clip(r - NA_ROWS // 2, 0, n_rows - NA_ROWS)
        start = pl.multiple_of((r0 - row0 + NA_ROWS_PER_STEP) * GRID_W, GRID_W)
        q = cur_ref[0, j * GRID_W:(j + 1) * GRID_W, 0:NA_DIM] * np.float32(NA_HD ** -0.5)
        qs = jnp.concatenate([jnp.where(lane_head == h, q, 0.0) for h in range(NA_HEADS)], axis=0)
        kw = k_scr[pl.ds(start, NA_WIN), :]
        vw = v_scr[pl.ds(start, NA_WIN), :]
        logits = lax.dot_general(qs.astype(MXU_DTYPE), kw, (((1,), (1,)), ((), ())),
                                 preferred_element_type=F32) + tab_ref[r - r0]
        m = jnp.max(logits, axis=-1, keepdims=True)
        p = jnp.exp(logits - m)
        denom = jnp.sum(p, axis=-1, keepdims=True)
        o_all = jnp.dot(p.astype(MXU_DTYPE), vw, preferred_element_type=F32) / denom
        o = jnp.zeros((GRID_W, NA_DIM), F32)
        for h in range(NA_HEADS):
            o = o + jnp.where(lane_head == h, o_all[h * GRID_W:(h + 1) * GRID_W, :], 0.0)
        out_ref[0, j * GRID_W:(j + 1) * GRID_W, :] = o


def na_mixer(qkv, rpb, col=0):
    b, s, _ = qkv.shape
    n_rows = s // GRID_W
    assert n_rows >= NA_ROWS and n_rows % NA_ROWS_PER_STEP == 0 and col % (3 * NA_DIM) == 0
    n_steps = n_rows // NA_ROWS_PER_STEP
    blk = (1, NA_STEP_TOK, 3 * NA_DIM)
    cb = col // (3 * NA_DIM)
    return pl.pallas_call(
        functools.partial(_na_kernel, n_rows=n_rows),
        grid=(b, n_steps),
        in_specs=[
            pl.BlockSpec(blk, lambda i, j: (i, jnp.maximum(j - 1, 0), cb)),
            pl.BlockSpec(blk, lambda i, j: (i, j, cb)),
            pl.BlockSpec(blk, lambda i, j: (i, jnp.minimum(j + 1, n_steps - 1), cb)),
            pl.BlockSpec((NA_ROWS, NA_HEADS * GRID_W, NA_WIN), lambda i, j: (0, 0, 0)),
        ],
        out_specs=pl.BlockSpec((1, NA_STEP_TOK, NA_DIM), lambda i, j: (i, j, 0)),
        out_shape=jax.ShapeDtypeStruct((b, s, NA_DIM), F32),
        scratch_shapes=[
            pltpu.VMEM((3 * NA_STEP_TOK, NA_DIM), MXU_DTYPE),
            pltpu.VMEM((3 * NA_STEP_TOK, NA_DIM), MXU_DTYPE),
        ],
        compiler_params=pltpu.CompilerParams(
            dimension_semantics=("arbitrary", "arbitrary"), vmem_limit_bytes=VMEM_LIMIT_BYTES),
        name="na_attention",
    )(qkv, qkv, qkv, _na_bias_table(rpb))


PK_GATE = 0
PK_XBC = PK_GATE + N_BRANCH * D_MODEL
PK_CQ = PK_XBC + CONV_CH
PK_DIL = PK_CQ + MLA_Q_RANK
PK_NA = PK_DIL + DIL_N_GROUPS * 3 * DIL_DIM
PK_KR = PK_NA + 3 * NA_DIM
PK_Z = PK_KR + 2 * MLA_HG
PK_CKV = PK_Z + SSM_INNER
PK_DT = PK_CKV + MLA_KV_RANK
PK_DT_W = LANES
PK_WIDTH = PK_DT + PK_DT_W
INPROJ_TM = 512
INPROJ_TN = PK_WIDTH // 2
MERGE_TM = 512


def _pack_w_in(w_in_l):
    gate, a_cq, a_ckv, a_kr, b_qkv, c_z, c_xbc, c_dt, d_qkv = jnp.split(w_in_l, IN_SPLITS, axis=-1)
    zeros = lambda n: jnp.zeros((D_MODEL, n), w_in_l.dtype)
    cols = [gate, c_xbc, a_cq, b_qkv, d_qkv, mla_rotary_key_columns(a_kr), c_z, a_ckv,
            c_dt, zeros(PK_DT_W - 2 * SSM_HEADS)]
    packed = jnp.concatenate(cols, axis=-1)
    assert packed.shape[1] == PK_DT + PK_DT_W
    return jnp.concatenate([packed, zeros(PK_WIDTH - packed.shape[1])], axis=-1).astype(MXU_DTYPE)


def _inproj_kernel(x_ref, g_ref, w_ref, out_ref, h_scr):
    @pl.when(pl.program_id(1) == 0)
    def _():
        x = x_ref[...]
        h_scr[...] = (x * lax.rsqrt(jnp.mean(x * x, axis=-1, keepdims=True) + EPS) * g_ref[...]).astype(MXU_DTYPE)

    out_ref[...] = jnp.dot(h_scr[...], w_ref[...], preferred_element_type=F32)


def _inproj(x2d, norm_g, w_packed):
    n_tok = x2d.shape[0]
    tm, tn = INPROJ_TM, INPROJ_TN
    return pl.pallas_call(
        _inproj_kernel,
        grid=(n_tok // tm, PK_WIDTH // tn),
        in_specs=[pl.BlockSpec((tm, D_MODEL), lambda i, j: (i, 0)),
                  pl.BlockSpec((1, D_MODEL), lambda i, j: (0, 0)),
                  pl.BlockSpec((D_MODEL, tn), lambda i, j: (0, j))],
        out_specs=pl.BlockSpec((tm, tn), lambda i, j: (i, j)),
        out_shape=jax.ShapeDtypeStruct((n_tok, PK_WIDTH), F32),
        scratch_shapes=[pltpu.VMEM((tm, D_MODEL), MXU_DTYPE)],
        compiler_params=pltpu.CompilerParams(
            dimension_semantics=("arbitrary", "arbitrary"), vmem_limit_bytes=VMEM_LIMIT_BYTES),
        name="in_projection",
    )(x2d, norm_g.reshape(1, D_MODEL), w_packed)


def _merge_kernel(x_ref, gate_ref, bg_ref, ya_ref, yb_ref, yc_ref, yd_ref, wb_ref, wo_ref, out_ref):
    merged = jnp.zeros(x_ref.shape, F32)
    for i, y_ref in enumerate((ya_ref, yb_ref, yc_ref, yd_ref)):
        proj = jnp.dot(y_ref[...].astype(MXU_DTYPE), wb_ref[BRANCH_ROWS[i]:BRANCH_ROWS[i + 1], :],
                       preferred_element_type=F32)
        lanes = slice(i * D_MODEL, (i + 1) * D_MODEL)
        merged = merged + jax.nn.sigmoid(gate_ref[:, lanes] + bg_ref[:, lanes]) * proj
    out_ref[...] = x_ref[...] + jnp.dot(merged.astype(MXU_DTYPE), wo_ref[...], preferred_element_type=F32)


def _merge(x2d, packed, b_gate, ys, w_branch, w_out):
    n_tok = x2d.shape[0]
    tm = MERGE_TM
    tok = lambda w: pl.BlockSpec((tm, w), lambda i: (i, 0))
    full = lambda r, c: pl.BlockSpec((r, c), lambda i: (0, 0))
    n_gate = N_BRANCH * D_MODEL
    return pl.pallas_call(
        _merge_kernel,
        grid=(n_tok // tm,),
        in_specs=[tok(D_MODEL), tok(n_gate), full(1, n_gate)] + [tok(w) for w in BRANCH_WIDTHS]
                 + [full(BRANCH_ROWS[-1], D_MODEL), full(D_MODEL, D_MODEL)],
        out_specs=tok(D_MODEL),
        out_shape=jax.ShapeDtypeStruct((n_tok, D_MODEL), F32),
        compiler_params=pltpu.CompilerParams(
            dimension_semantics=("arbitrary",), vmem_limit_bytes=VMEM_LIMIT_BYTES),
        name="branch_merge",
    )(x2d, packed, b_gate.reshape(1, n_gate), *[y.reshape(n_tok, -1) for y in ys], w_branch, w_out)


def encoder(x, norm1_g, w_in_packed, b_gate, mla_q_norm, mla_w_qb, mla_kv_norm, mla_w_kvb, t5_table,
            ssm_conv_w, ssm_conv_b, ssm_A_log, ssm_dt_bias, ssm_D, ssm_norm_g, na_rpb,
            w_branch, w_out, norm2_g, peer_wq, peer_keys, peer_u, peer_vt, final_g):
    b, s, _ = x.shape
    x2d = x.reshape(b * s, D_MODEL)
    for l in range(DEPTH):
        packed = _inproj(x2d, norm1_g[l], w_in_packed[l])
        p3 = packed.reshape(b, s, PK_WIDTH)
        y_a = mla_mixer(p3, mla_q_norm[l], mla_w_qb[l], mla_kv_norm[l], mla_w_kvb[l],
                        col_q=PK_CQ, col_kv=PK_CKV, col_kr=PK_KR)
        y_b = dilated_mixer(p3, t5_table, col=PK_DIL)
        y_c = mamba2_mixer(p3, ssm_conv_w[l], ssm_conv_b[l], ssm_A_log[l], ssm_dt_bias[l], ssm_D[l],
                           ssm_norm_g[l], col_z=PK_Z, col_xbc=PK_XBC, col_dt=PK_DT, dt_w=PK_DT_W)
        y_d = na_mixer(p3, na_rpb[l], col=PK_NA)
        x2d = _merge(x2d, packed, b_gate[l], (y_a, y_b, y_c, y_d), w_branch[l], w_out[l])
        x2d = peer_block(x2d, norm2_g[l], peer_wq[l], peer_keys[l], peer_u[l], peer_vt[l],
                         final_g, final_norm=(l == DEPTH - 1))
    return x2d.reshape(b, s, D_MODEL)


def kernel(x_prompt, x_sample, norm1_g, w_in, b_gate, mla_q_norm, mla_w_qb, mla_kv_norm, mla_w_kvb, t5_table, ssm_conv_w, ssm_conv_b, ssm_A_log, ssm_dt_bias, ssm_D, ssm_norm_g, na_rpb, w_branch, w_out, norm2_g, peer_wq, peer_keys, peer_u, peer_v, final_g):
    peer_u16 = peer_u.astype(MXU_DTYPE)
    peer_vt16 = jnp.swapaxes(peer_v, 1, 2).astype(MXU_DTYPE)
    w_in_packed = jnp.stack([_pack_w_in(w_in[l]) for l in range(DEPTH)])
    shared = (norm1_g, w_in_packed, b_gate, mla_q_norm, mla_w_qb, mla_kv_norm, mla_w_kvb, t5_table,
              ssm_conv_w, ssm_conv_b, ssm_A_log, ssm_dt_bias, ssm_D, ssm_norm_g, na_rpb,
              w_branch.astype(MXU_DTYPE), w_out.astype(MXU_DTYPE), norm2_g, peer_wq, peer_keys,
              peer_u16, peer_vt16, final_g)
    y_prompt = encoder(x_prompt, *shared)
    y_sample = encoder(x_sample, *shared)
    return (y_prompt, y_sample)
```

```python
import functools
import math

import numpy as np
import jax
import jax.numpy as jnp
from jax import lax
from jax.experimental import pallas as pl
from jax.experimental.pallas import tpu as pltpu

F32 = jnp.float32
BF16 = jnp.bfloat16
MXU_DTYPE = BF16

D_MODEL = 1024
DEPTH = 2
GRID_W = 64
EPS = 1e-6
N_BRANCH = 4

MLA_HEADS = 4
MLA_Q_RANK = 256
MLA_KV_RANK = 128
MLA_NOPE = 64
MLA_ROPE = 32
MLA_V = 64
ROPE_THETA = 10000.0

DIL_GROUPS = ((128, 1), (512, 4), (2048, 16))
DIL_HEADS = 4
DIL_HD = 64
T5_BUCKETS = 32
T5_MAX_DIST = 1024

SSM_HEADS = 8
SSM_HD = 64
SSM_INNER = SSM_HEADS * SSM_HD
SSM_GROUPS = 2
SSM_STATE = 128
SSM_CONV = 7
SSM_CHUNK = 128
CONV_CH = SSM_INNER + 2 * SSM_GROUPS * SSM_STATE

NA_HEADS = 4
NA_HD = 64
NA_ROWS = 8
NA_COLS = 16

PEER_HEADS = 8
PEER_KEYS = 128
PEER_EXPERTS = PEER_KEYS * PEER_KEYS
PEER_QDIM = 256
PEER_TOPK = 16

BRANCH_WIDTHS = (MLA_HEADS * MLA_V, DIL_HEADS * DIL_HD, SSM_INNER, NA_HEADS * NA_HD)
BRANCH_ROWS = tuple(sum(BRANCH_WIDTHS[:i]) for i in range(N_BRANCH + 1))
IN_SIZES = (N_BRANCH * D_MODEL, MLA_Q_RANK, MLA_KV_RANK, MLA_ROPE,
            len(DIL_GROUPS) * 3 * DIL_HEADS * DIL_HD,
            SSM_INNER, CONV_CH, 2 * SSM_HEADS,
            3 * NA_HEADS * NA_HD)
IN_SPLITS = tuple(sum(IN_SIZES[:i + 1]) for i in range(len(IN_SIZES) - 1))

VMEM_LIMIT_BYTES = 56 * 1024 * 1024
LANES = 128
SUBLANES = 8


PEER_ROUTE_TB = 512
PEER_TB = 512
PEER_EC = 1024
PEER_GATE_DTYPE = BF16
PEER_GATE_ROWS = 16


def _gate_pack():
    return 4 // jnp.dtype(PEER_GATE_DTYPE).itemsize


def _gate_to_storage(x):
    x = x.astype(PEER_GATE_DTYPE)
    return x if _gate_pack() == 1 else pltpu.bitcast(x, jnp.uint32)


def _gate_from_storage(x):
    return x if _gate_pack() == 1 else pltpu.bitcast(x, PEER_GATE_DTYPE)


def _gate_splat(row):
    return jnp.broadcast_to(row.astype(PEER_GATE_DTYPE), (PEER_GATE_ROWS, row.shape[1]))
PEER_HALF = PEER_QDIM // 2
PEER_CAND_ROWS = 2 * SUBLANES + 7 * SUBLANES + SUBLANES


def _gelu_exact_x2(x):
    return x * (1.0 + lax.erf(x * np.float32(math.sqrt(0.5))))


def _extract_desc(vals, n_out, out_ref, row0):
    for k in range(n_out):
        m = jnp.max(vals, axis=0, keepdims=True)
        out_ref[pl.ds(row0 + k, 1), :] = m
        vals = jnp.where(vals == m, -jnp.inf, vals)


def _oddeven_sort_network(lo, hi):
    def merge(lo, hi, r):
        step = r * 2
        if step < hi - lo:
            yield from merge(lo, hi, step)
            yield from merge(lo + r, hi, step)
            yield from ((i, i + r) for i in range(lo + r, hi - r, step))
        else:
            yield (lo, lo + r)

    if hi - lo >= 1:
        mid = lo + (hi - lo) // 2
        yield from _oddeven_sort_network(lo, mid)
        yield from _oddeven_sort_network(mid + 1, hi)
        yield from merge(lo, hi, 1)


def _bitonic_merge_network(n):
    d = n // 2
    while d >= 1:
        yield from ((i, i + d) for i in range(n) if (i // d) % 2 == 0)
        d //= 2


PEER_SORT_NET = tuple(_oddeven_sort_network(0, PEER_TOPK - 1))
PEER_MERGE_NET = tuple(_bitonic_merge_network(PEER_TOPK))


def _compare_exchange(tiles, network):
    for i, j in network:
        tiles[i], tiles[j] = jnp.maximum(tiles[i], tiles[j]), jnp.minimum(tiles[i], tiles[j])


def _sorted_top(s):
    assert s.shape[0] == PEER_TOPK * SUBLANES
    tiles = [s[i * SUBLANES:(i + 1) * SUBLANES, :] for i in range(PEER_TOPK)]
    _compare_exchange(tiles, PEER_SORT_NET)
    shift = SUBLANES // 2
    while shift >= 1:
        tiles = [jnp.maximum(tiles[i], pltpu.roll(tiles[PEER_TOPK - 1 - i], shift, axis=0))
                 for i in range(PEER_TOPK)]
        _compare_exchange(tiles, PEER_MERGE_NET)
        shift //= 2
    return tiles


def _peer_route_kernel(x_ref, g_ref, wqt_ref, keys_ref,
                       xnt_ref, cnt_ref, rank_ref, e1_ref, e2_ref,
                       qt_scr, top_scr, cand_scr, tops_scr):
    x = x_ref[...]
    xn = x * lax.rsqrt(jnp.mean(x * x, axis=-1, keepdims=True) + EPS) * g_ref[...]
    xnt = xn.T.astype(MXU_DTYPE)
    xnt_ref[...] = xnt
    qt_scr[...] = jnp.dot(wqt_ref[...], xnt, preferred_element_type=F32).astype(MXU_DTYPE)

    def head(h, carry):
        q1 = qt_scr[pl.ds(pl.multiple_of(h * PEER_QDIM, PEER_QDIM), PEER_HALF), :]
        q2 = qt_scr[pl.ds(pl.multiple_of(h * PEER_QDIM + PEER_HALF, PEER_HALF), PEER_HALF), :]
        s1_all = jnp.dot(keys_ref[2 * h], q1, preferred_element_type=F32)
        s2_all = jnp.dot(keys_ref[2 * h + 1], q2, preferred_element_type=F32)
        for l0 in range(0, s1_all.shape[1], LANES):
            lanes = slice(l0, l0 + LANES)
            s1, s2 = s1_all[:, lanes], s2_all[:, lanes]
            top1, top2 = _sorted_top(s1), _sorted_top(s2)
            for k in range(PEER_TOPK):
                top_scr[k:k + 1, lanes] = top1[k][0:1, :]
                top_scr[PEER_TOPK + k:PEER_TOPK + k + 1, lanes] = top2[k][0:1, :]
            t1 = top_scr[0:PEER_TOPK, lanes]
            t2 = top_scr[PEER_TOPK:2 * PEER_TOPK, lanes]
            cand_scr[0:2 * SUBLANES, lanes] = t1[0:1, :] + t2
            for a in range(1, SUBLANES):
                cand_scr[(a + 1) * SUBLANES:(a + 2) * SUBLANES, lanes] = t1[a:a + 1, :] + t2[0:SUBLANES, :]
            cand_scr[9 * SUBLANES:10 * SUBLANES, lanes] = t1[SUBLANES:2 * SUBLANES, :] + t2[0:1, :]
            _extract_desc(cand_scr[:, lanes], PEER_TOPK, tops_scr.at[:, lanes], 0)
            top_s = tops_scr[:, lanes]
            z = jnp.sum(jnp.exp(top_s - top_s[0:1, :]), axis=0, keepdims=True)
            tau = top_s[PEER_TOPK - 1:PEER_TOPK, :]
            rank_tiles = []
            for i in range(PEER_KEYS // SUBLANES):
                s2_tile = s2[i * SUBLANES:(i + 1) * SUBLANES, :]
                rank = jnp.zeros(s2_tile.shape, F32)
                for k in range(PEER_TOPK):
                    rank = rank + jnp.where(top2[k] > s2_tile, 1.0, 0.0)
                rank_tiles.append(rank)
            cnt_top = jnp.zeros(t1.shape, F32)
            for b in range(PEER_TOPK):
                cnt_top = cnt_top + jnp.where(t1 + t2[b:b + 1, :] >= tau, 1.0, 0.0)
            cnt = jnp.zeros(s1.shape, F32)
            for a in range(PEER_TOPK):
                cnt = jnp.where(s1 == t1[a:a + 1, :], cnt_top[a:a + 1, :], cnt)
            cnt_ref[h, :, lanes] = cnt
            rank_ref[h, :, lanes] = _gate_to_storage(jnp.concatenate(rank_tiles, axis=0))
            e1_ref[h, :, lanes] = jnp.exp(s1 - t1[0:1, :]) * (0.5 / z)
            e2_ref[h, :, lanes] = _gate_to_storage(jnp.exp(s2 - t2[0:1, :]))
        return carry

    lax.fori_loop(0, PEER_HEADS, head, 0)


def _peer_route(x2d, g, wqt, keys):
    n_tok = x2d.shape[0]
    tb = PEER_ROUTE_TB
    rt_shape = jax.ShapeDtypeStruct((PEER_HEADS, PEER_KEYS, n_tok), F32)
    gate_rows = PEER_KEYS // _gate_pack()
    gate_store = PEER_GATE_DTYPE if _gate_pack() == 1 else jnp.uint32
    gate_shape = jax.ShapeDtypeStruct((PEER_HEADS, gate_rows, n_tok), gate_store)
    rt_spec = pl.BlockSpec((PEER_HEADS, PEER_KEYS, tb), lambda i: (0, 0, i))
    gate_spec = pl.BlockSpec((PEER_HEADS, gate_rows, tb), lambda i: (0, 0, i))
    return pl.pallas_call(
        _peer_route_kernel,
        grid=(n_tok // tb,),
        in_specs=[
            pl.BlockSpec((tb, D_MODEL), lambda i: (i, 0)),
            pl.BlockSpec((1, D_MODEL), lambda i: (0, 0)),
            pl.BlockSpec((PEER_HEADS * PEER_QDIM, D_MODEL), lambda i: (0, 0)),
            pl.BlockSpec((2 * PEER_HEADS, PEER_KEYS, PEER_HALF), lambda i: (0, 0, 0)),
        ],
        out_specs=[
            pl.BlockSpec((D_MODEL, tb), lambda i: (0, i)),
            rt_spec, gate_spec, rt_spec, gate_spec,
        ],
        out_shape=[
            jax.ShapeDtypeStruct((D_MODEL, n_tok), MXU_DTYPE),
            rt_shape, gate_shape, rt_shape, gate_shape,
        ],
        scratch_shapes=[
            pltpu.VMEM((PEER_HEADS * PEER_QDIM, tb), MXU_DTYPE),
            pltpu.VMEM((2 * PEER_TOPK, tb), F32),
            pltpu.VMEM((PEER_CAND_ROWS, tb), F32),
            pltpu.VMEM((PEER_TOPK, tb), F32),
        ],
        compiler_params=pltpu.CompilerParams(
            dimension_semantics=("arbitrary",), vmem_limit_bytes=VMEM_LIMIT_BYTES),
        name="peer_route",
    )(x2d, g, wqt, keys)


def _peer_expert_kernel(x_ref, xnt_ref, cnt_ref, rank_ref, e1_ref, e2_ref, u_ref, vt_ref, fg_ref,
                        out_ref, acc_scr, w_scr, *, final_norm):
    c = pl.program_id(1)
    n_chunks = pl.num_programs(1) - 1
    n_i1 = PEER_EC // PEER_KEYS
    tb = acc_scr.shape[1]
    gdt = PEER_GATE_DTYPE

    @pl.when(c == 0)
    def _():
        acc_scr[...] = jnp.zeros_like(acc_scr)
        w_scr[1] = jnp.zeros(w_scr.shape[1:], w_scr.dtype)

    slot = c % 2
    n_groups = PEER_KEYS // PEER_GATE_ROWS
    acc_scr[...] += jnp.dot(vt_ref[...], w_scr[(c + 1) % 2], preferred_element_type=F32)
    hid = jnp.dot(u_ref[...], xnt_ref[...], preferred_element_type=F32)

    def gate_block(i1l, l0):
        lanes = slice(l0, l0 + LANES)
        gates = [jnp.zeros((PEER_GATE_ROWS, LANES), gdt) for _ in range(n_groups)]
        for h in range(PEER_HEADS):
            cntb = _gate_splat(cnt_ref[h, i1l:i1l + 1, lanes])
            e1b = _gate_splat(e1_ref[h, i1l:i1l + 1, lanes])
            for k in range(n_groups):
                rows = slice(k * SUBLANES, (k + 1) * SUBLANES)
                w = _gate_from_storage(e2_ref[h, rows, lanes]) * e1b
                rank = _gate_from_storage(rank_ref[h, rows, lanes])
                gates[k] = gates[k] + jnp.where(rank < cntb, w, jnp.zeros_like(w))
        for k in range(n_groups):
            r0 = k * PEER_GATE_ROWS
            e0 = i1l * PEER_KEYS + r0
            act = _gelu_exact_x2(hid[e0:e0 + PEER_GATE_ROWS, lanes]).astype(gdt)
            w_scr[slot, e0:e0 + PEER_GATE_ROWS, lanes] = (act * gates[k]).astype(MXU_DTYPE)

    for i1l in range(n_i1):
        for l0 in range(0, tb, LANES):
            gate_block(i1l, l0)

    @pl.when(c == n_chunks)
    def _():
        y = x_ref[...] + acc_scr[...].T
        if final_norm:
            y = y * lax.rsqrt(jnp.mean(y * y, axis=-1, keepdims=True) + EPS) * fg_ref[...]
        out_ref[...] = y


def _peer_experts(x2d, xnt, cnt, rank2, e1, e2, u, vt, final_g, final_norm):
    n_tok = x2d.shape[0]
    tb, ec = PEER_TB, PEER_EC
    n_chunks = PEER_EXPERTS // ec
    assert PEER_GATE_ROWS == SUBLANES * _gate_pack()
    rt_spec = pl.BlockSpec((PEER_HEADS, PEER_KEYS // _gate_pack(), tb), lambda j, c: (0, 0, j))
    row_spec = pl.BlockSpec((PEER_HEADS, ec // PEER_KEYS, tb),
                            lambda j, c: (0, jnp.minimum(c, n_chunks - 1), j))
    return pl.pallas_call(
        functools.partial(_peer_expert_kernel, final_norm=final_norm),
        grid=(n_tok // tb, n_chunks + 1),
        in_specs=[
            pl.BlockSpec((tb, D_MODEL), lambda j, c: (j, 0)),
            pl.BlockSpec((D_MODEL, tb), lambda j, c: (0, j)),
            row_spec, rt_spec, row_spec, rt_spec,
            pl.BlockSpec((ec, D_MODEL), lambda j, c: (jnp.minimum(c, n_chunks - 1), 0)),
            pl.BlockSpec((D_MODEL, ec), lambda j, c: (0, jnp.maximum(c - 1, 0))),
            pl.BlockSpec((1, D_MODEL), lambda j, c: (0, 0)),
        ],
        out_specs=pl.BlockSpec((tb, D_MODEL), lambda j, c: (j, 0)),
        out_shape=jax.ShapeDtypeStruct((n_tok, D_MODEL), F32),
        scratch_shapes=[
            pltpu.VMEM((D_MODEL, tb), F32),
            pltpu.VMEM((2, ec, tb), MXU_DTYPE),
        ],
        compiler_params=pltpu.CompilerParams(
            dimension_semantics=("arbitrary", "arbitrary"), vmem_limit_bytes=VMEM_LIMIT_BYTES),
        name="peer_experts",
    )(x2d, xnt, cnt, rank2, e1, e2, u, vt, final_g.reshape(1, D_MODEL))


def peer_block(x2d, norm_g, w_q, keys, u, v, final_g, final_norm=False):
    wqt = w_q.T.astype(MXU_DTYPE)
    keys2 = keys.reshape(2 * PEER_HEADS, PEER_KEYS, PEER_HALF).astype(MXU_DTYPE)
    xnt, cnt, rank2, e1, e2 = _peer_route(x2d, norm_g.reshape(1, D_MODEL), wqt, keys2)
    return _peer_experts(x2d, xnt, cnt, rank2, e1, e2, u, v, final_g, final_norm)


MLA_HG = LANES
MLA_QK_W = MLA_HEADS * MLA_HG
MLA_V_W = MLA_HEADS * MLA_V
MLA_PREP_TOK = 512
MLA_TQ = 512
MLA_TK = 2048
MLA_RHALF = MLA_ROPE // 2


def _mla_rope_tables(s):
    inv = ROPE_THETA ** (-jnp.arange(MLA_RHALF, dtype=F32) / MLA_RHALF)
    ang = jnp.arange(s).astype(F32)[:, None] * inv[None, :]
    cos, sin = jnp.cos(ang), jnp.sin(ang)
    zero_pad = jnp.zeros((s, MLA_HG - MLA_NOPE - MLA_ROPE), F32)
    cos_rot = jnp.concatenate([cos, cos, zero_pad], axis=1)
    sin_rot = jnp.concatenate([-sin, sin, zero_pad], axis=1)
    scale = np.float32((MLA_NOPE + MLA_ROPE) ** -0.5)
    q_cos = scale * jnp.concatenate([jnp.ones((s, MLA_NOPE), F32), cos_rot], axis=1)
    q_sin = scale * jnp.concatenate([jnp.zeros((s, MLA_NOPE), F32), sin_rot], axis=1)
    k_cos = jnp.concatenate([jnp.zeros((s, MLA_NOPE), F32), cos_rot], axis=1)
    k_sin = jnp.concatenate([jnp.zeros((s, MLA_NOPE), F32), sin_rot], axis=1)
    return q_cos, q_sin, k_cos, k_sin


def _mla_pack_weights(w_qb, w_kvb):
    hd_q = MLA_NOPE + MLA_ROPE
    wq = w_qb.reshape(MLA_Q_RANK, MLA_HEADS, hd_q)
    rot = wq[:, :, MLA_NOPE:]
    rot_sw = jnp.concatenate([rot[:, :, MLA_RHALF:], rot[:, :, :MLA_RHALF]], axis=2)
    pad = jnp.zeros((MLA_Q_RANK, MLA_HEADS, MLA_HG - hd_q), F32)
    wq_a = jnp.concatenate([wq, pad], axis=2).reshape(MLA_Q_RANK, MLA_QK_W)
    wq_b = jnp.concatenate([jnp.zeros_like(wq[:, :, :MLA_NOPE]), rot_sw, pad], axis=2).reshape(MLA_Q_RANK, MLA_QK_W)
    wkv = w_kvb.reshape(MLA_KV_RANK, MLA_HEADS, MLA_NOPE + MLA_V)
    wk = jnp.concatenate([wkv[:, :, :MLA_NOPE], jnp.zeros((MLA_KV_RANK, MLA_HEADS, MLA_HG - MLA_NOPE), F32)],
                         axis=2).reshape(MLA_KV_RANK, MLA_QK_W)
    wv = jnp.concatenate([wkv[:, :, MLA_NOPE:], jnp.zeros((MLA_KV_RANK, MLA_HEADS, MLA_HG - MLA_V), F32)],
                         axis=2).reshape(MLA_KV_RANK, MLA_QK_W)
    return (wq_a.astype(MXU_DTYPE), wq_b.astype(MXU_DTYPE), wk.astype(MXU_DTYPE), wv.astype(MXU_DTYPE))


def _mla_prep_kernel(cq_ref, ckv_ref, kr_ref, qn_ref, kvn_ref, wqa_ref, wqb_ref, wk_ref, wv_ref,
                     qcos_ref, qsin_ref, kcos_ref, ksin_ref, q_out, kt_out, v_out):
    cq = cq_ref[0]
    cqn = (cq * lax.rsqrt(jnp.mean(cq * cq, axis=-1, keepdims=True) + EPS) * qn_ref[...]).astype(MXU_DTYPE)
    qa = jnp.dot(cqn, wqa_ref[...], preferred_element_type=F32)
    qb = jnp.dot(cqn, wqb_ref[...], preferred_element_type=F32)
    ckv = ckv_ref[0]
    ckvn = (ckv * lax.rsqrt(jnp.mean(ckv * ckv, axis=-1, keepdims=True) + EPS) * kvn_ref[...]).astype(MXU_DTYPE)
    ka = jnp.dot(ckvn, wk_ref[...], preferred_element_type=F32)
    one_lane = (lax.broadcasted_iota(jnp.int32, (1, MLA_QK_W), 1) % MLA_HG == MLA_V).astype(F32)
    v_out[0] = (jnp.dot(ckvn, wv_ref[...], preferred_element_type=F32) + one_lane).astype(MXU_DTYPE)
    kr = kr_ref[0]
    k_rot = kr[:, 0:MLA_HG] * kcos_ref[...] + kr[:, MLA_HG:2 * MLA_HG] * ksin_ref[...]
    for h in range(MLA_HEADS):
        lanes = slice(h * MLA_HG, (h + 1) * MLA_HG)
        q_out[0, :, lanes] = (qa[:, lanes] * qcos_ref[...] + qb[:, lanes] * qsin_ref[...]).astype(MXU_DTYPE)
        kt_out[0, lanes, :] = (ka[:, lanes] + k_rot).T.astype(MXU_DTYPE)


def _mla_flash_kernel(q_ref, kt_ref, v_ref, out_ref, m_scr, acc_scr):
    ki = pl.program_id(2)

    @pl.when(ki == 0)
    def _():
        m_scr[...] = jnp.full(m_scr.shape, -jnp.inf, F32)
        acc_scr[...] = jnp.zeros(acc_scr.shape, F32)

    def logits(h):
        lanes = slice(h * MLA_HG, (h + 1) * MLA_HG)
        return jnp.dot(q_ref[0, :, lanes], kt_ref[0, lanes, :], preferred_element_type=F32)

    n_rep = kt_ref.shape[2] // MLA_HG
    s_next = logits(0)
    for h in range(MLA_HEADS):
        s = s_next
        if h + 1 < MLA_HEADS:
            s_next = logits(h + 1)
        m_old = m_scr[h]
        m_new = jnp.maximum(m_old, jnp.max(s, axis=-1, keepdims=True))
        p = jnp.exp(s - jnp.tile(m_new, (1, n_rep))).astype(MXU_DTYPE)
        acc_scr[h] = jnp.exp(m_old - m_new) * acc_scr[h] + jnp.dot(
            p, v_ref[0, :, h * MLA_HG:(h + 1) * MLA_HG], preferred_element_type=F32)
        m_scr[h] = m_new

    @pl.when(ki == pl.num_programs(2) - 1)
    def _():
        low = lax.broadcasted_iota(jnp.int32, (acc_scr.shape[1], MLA_HG), 1) < MLA_V
        outs = []
        for h in range(MLA_HEADS):
            acc = acc_scr[h]
            outs.append(acc / acc[:, MLA_V:MLA_V + 1])
        for hp in range(MLA_HEADS // 2):
            odd = pltpu.roll(outs[2 * hp + 1], MLA_V, axis=1)
            out_ref[0, :, hp * MLA_HG:(hp + 1) * MLA_HG] = jnp.where(low, outs[2 * hp], odd)


def mla_rotary_key_columns(w):
    zl = jnp.zeros(w.shape[:-1] + (MLA_NOPE,), w.dtype)
    zr = jnp.zeros(w.shape[:-1] + (MLA_HG - MLA_NOPE - MLA_ROPE,), w.dtype)
    w_sw = jnp.concatenate([w[..., MLA_RHALF:], w[..., :MLA_RHALF]], axis=-1)
    return jnp.concatenate([zl, w, zr, zl, w_sw, zr], axis=-1)


def mla_mixer(arr, q_norm, w_qb, kv_norm, w_kvb, col_q=0, col_kv=MLA_Q_RANK, col_kr=MLA_Q_RANK + MLA_KV_RANK):
    b, s, _ = arr.shape
    tt = MLA_PREP_TOK
    assert col_q % MLA_Q_RANK == 0 and col_kv % MLA_KV_RANK == 0 and col_kr % (2 * MLA_HG) == 0
    wqa, wqb, wk, wv = _mla_pack_weights(w_qb, w_kvb)
    q_cos, q_sin, k_cos, k_sin = _mla_rope_tables(s)
    tok = lambda w, c=0: pl.BlockSpec((1, tt, w), lambda i, j: (i, j, c // w))
    full = lambda r, c: pl.BlockSpec((r, c), lambda i, j: (0, 0))
    tab = pl.BlockSpec((tt, MLA_HG), lambda i, j: (j, 0))
    q, kt, v = pl.pallas_call(
        _mla_prep_kernel,
        grid=(b, s // tt),
        in_specs=[tok(MLA_Q_RANK, col_q), tok(MLA_KV_RANK, col_kv), tok(2 * MLA_HG, col_kr),
                  full(1, MLA_Q_RANK), full(1, MLA_KV_RANK),
                  full(MLA_Q_RANK, MLA_QK_W), full(MLA_Q_RANK, MLA_QK_W),
                  full(MLA_KV_RANK, MLA_QK_W), full(MLA_KV_RANK, MLA_QK_W),
                  tab, tab, tab, tab],
        out_specs=[tok(MLA_QK_W), pl.BlockSpec((1, MLA_QK_W, tt), lambda i, j: (i, 0, j)), tok(MLA_QK_W)],
        out_shape=[jax.ShapeDtypeStruct((b, s, MLA_QK_W), MXU_DTYPE),
                   jax.ShapeDtypeStruct((b, MLA_QK_W, s), MXU_DTYPE),
                   jax.ShapeDtypeStruct((b, s, MLA_QK_W), MXU_DTYPE)],
        compiler_params=pltpu.CompilerParams(
            dimension_semantics=("arbitrary", "arbitrary"), vmem_limit_bytes=VMEM_LIMIT_BYTES),
        name="mla_prep",
    )(arr, arr, arr, q_norm.reshape(1, -1), kv_norm.reshape(1, -1), wqa, wqb, wk, wv,
      q_cos, q_sin, k_cos, k_sin)
    tq, tk = min(MLA_TQ, s), min(MLA_TK, s)
    assert s % tq == 0 and s % tk == 0 and s % tt == 0
    return pl.pallas_call(
        _mla_flash_kernel,
        grid=(b, s // tq, s // tk),
        in_specs=[pl.BlockSpec((1, tq, MLA_QK_W), lambda i, j, kk: (i, j, 0)),
                  pl.BlockSpec((1, MLA_QK_W, tk), lambda i, j, kk: (i, 0, kk)),
                  pl.BlockSpec((1, tk, MLA_QK_W), lambda i, j, kk: (i, kk, 0))],
        out_specs=pl.BlockSpec((1, tq, MLA_V_W), lambda i, j, kk: (i, j, 0)),
        out_shape=jax.ShapeDtypeStruct((b, s, MLA_V_W), F32),
        scratch_shapes=[pltpu.VMEM((MLA_HEADS, tq, MLA_HG), F32),
                        pltpu.VMEM((MLA_HEADS, tq, MLA_HG), F32)],
        compiler_params=pltpu.CompilerParams(
            dimension_semantics=("arbitrary", "arbitrary", "arbitrary"), vmem_limit_bytes=VMEM_LIMIT_BYTES),
        name="mla_flash",
    )(q, kt, v)


def t5_bucket(rel):
    nb = T5_BUCKETS // 2
    ret = np.where(rel > 0, nb, 0)
    n = np.abs(rel)
    max_exact = nb // 2
    large = max_exact + (np.log(np.maximum(n, 1) / max_exact) / np.log(T5_MAX_DIST / max_exact)
                         * (nb - max_exact)).astype(np.int64)
    large = np.minimum(large, nb - 1)
    return (ret + np.where(n < max_exact, n, large)).astype(np.int32)


DIL_DIM = DIL_HEADS * DIL_HD
DIL_HALF = 64
DIL_QB = 128
DIL_KW = DIL_QB + 2 * DIL_HALF
DIL_TL = 512
DIL_N_GROUPS = len(DIL_GROUPS)


def _dil_bias_table(t5_table, gi, dil):
    rel = np.arange(DIL_KW)[None, :] - DIL_HALF - np.arange(DIL_QB)[:, None]
    bias = t5_table[:, gi * DIL_HEADS:(gi + 1) * DIL_HEADS][t5_bucket(rel * dil)].astype(F32)
    bias = jnp.where((np.abs(rel) <= DIL_HALF)[:, :, None], bias, -jnp.inf)
    return bias.transpose(2, 0, 1).reshape(DIL_HEADS * DIL_QB, DIL_KW)


def _dil_kernel(prev_ref, cur_ref, next_ref, tab_ref, o_ref, lse_ref, k_scr, v_scr, *, seq_len):
    step = pl.program_id(2)
    for i, ref in enumerate((prev_ref, cur_ref, next_ref)):
        k_scr[i * DIL_TL:(i + 1) * DIL_TL, :] = ref[0, :, DIL_DIM:2 * DIL_DIM].astype(MXU_DTYPE)
        v_scr[i * DIL_TL:(i + 1) * DIL_TL, :] = ref[0, :, 2 * DIL_DIM:3 * DIL_DIM].astype(MXU_DTYPE)
    lane_head = lax.broadcasted_iota(jnp.int32, (DIL_QB, DIL_DIM), 1) // DIL_HD
    key_off = lax.broadcasted_iota(jnp.int32, (1, DIL_KW), 1)
    for n in range(DIL_TL // DIL_QB):
        w0 = DIL_TL + n * DIL_QB - DIL_HALF
        kpos = step * DIL_TL + (n * DIL_QB - DIL_HALF) + key_off
        valid = (kpos >= 0) & (kpos < seq_len)
        q = cur_ref[0, n * DIL_QB:(n + 1) * DIL_QB, 0:DIL_DIM] * np.float32(DIL_HD ** -0.5)
        qs = jnp.concatenate([jnp.where(lane_head == h, q, 0.0) for h in range(DIL_HEADS)], axis=0)
        logits = lax.dot_general(qs.astype(MXU_DTYPE), k_scr[w0:w0 + DIL_KW, :], (((1,), (1,)), ((), ())),
                                 preferred_element_type=F32) + tab_ref[...]
        logits = jnp.where(valid, logits, -jnp.inf)
        m = jnp.max(logits, axis=-1, keepdims=True)
        p = jnp.exp(logits - m)
        denom = jnp.sum(p, axis=-1, keepdims=True)
        o_all = jnp.dot(p.astype(MXU_DTYPE), v_scr[w0:w0 + DIL_KW, :], preferred_element_type=F32) / denom
        lse_all = m + jnp.log(denom)
        o = jnp.zeros((DIL_QB, DIL_DIM), F32)
        lse = jnp.zeros((DIL_QB, DIL_DIM), F32)
        for h in range(DIL_HEADS):
            rows = slice(h * DIL_QB, (h + 1) * DIL_QB)
            o = o + jnp.where(lane_head == h, o_all[rows, :], 0.0)
            lse = lse + jnp.where(lane_head == h, lse_all[rows, :], 0.0)
        o_ref[0, n * DIL_QB:(n + 1) * DIL_QB, :] = o
        lse_ref[0, n * DIL_QB:(n + 1) * DIL_QB, :] = lse


def _dil_group(qkv, t5_table, gi, dil, col):
    b, s, width = qkv.shape
    seq_len = s // dil
    assert seq_len % DIL_TL == 0 and col % (3 * DIL_DIM) == 0
    n_steps = seq_len // DIL_TL
    gi_col = col // (3 * DIL_DIM) + gi
    if dil > 1:
        qkv = qkv[:, :, gi_col * 3 * DIL_DIM:(gi_col + 1) * 3 * DIL_DIM]
        width, gi_col = 3 * DIL_DIM, 0
    n_col = width // (3 * DIL_DIM)
    view = qkv.reshape(b, seq_len, dil * width)
    blk = (1, DIL_TL, 3 * DIL_DIM)
    out_shape = jax.ShapeDtypeStruct((b, seq_len, dil * DIL_DIM), F32)
    out_spec = pl.BlockSpec((1, DIL_TL, DIL_DIM), lambda i, r, j: (i, j, r))
    o, lse = pl.pallas_call(
        functools.partial(_dil_kernel, seq_len=seq_len),
        grid=(b, dil, n_steps),
        in_specs=[
            pl.BlockSpec(blk, lambda i, r, j: (i, jnp.maximum(j - 1, 0), r * n_col + gi_col)),
            pl.BlockSpec(blk, lambda i, r, j: (i, j, r * n_col + gi_col)),
            pl.BlockSpec(blk, lambda i, r, j: (i, jnp.minimum(j + 1, n_steps - 1), r * n_col + gi_col)),
            pl.BlockSpec((DIL_HEADS * DIL_QB, DIL_KW), lambda i, r, j: (0, 0)),
        ],
        out_specs=[out_spec, out_spec],
        out_shape=[out_shape, out_shape],
        scratch_shapes=[pltpu.VMEM((3 * DIL_TL, DIL_DIM), MXU_DTYPE),
                        pltpu.VMEM((3 * DIL_TL, DIL_DIM), MXU_DTYPE)],
        compiler_params=pltpu.CompilerParams(
            dimension_semantics=("arbitrary", "arbitrary", "arbitrary"), vmem_limit_bytes=VMEM_LIMIT_BYTES),
        name=f"dilated_attention_g{gi}",
    )(view, view, view, _dil_bias_table(t5_table, gi, dil))
    return o.reshape(b, s, DIL_DIM), lse.reshape(b, s, DIL_DIM)


def _dil_combine_kernel(*refs):
    o_refs, lse_refs, out_ref = refs[:DIL_N_GROUPS], refs[DIL_N_GROUPS:2 * DIL_N_GROUPS], refs[-1]
    lses = [r[...] for r in lse_refs]
    m = functools.reduce(jnp.maximum, lses)
    ws = [jnp.exp(l - m) for l in lses]
    total = functools.reduce(jnp.add, ws)
    acc = functools.reduce(jnp.add, [w * r[...] for w, r in zip(ws, o_refs)])
    out_ref[...] = acc / total


def dilated_mixer(qkv, t5_table, col=0):
    b, s, _ = qkv.shape
    outs, lses = [], []
    for gi, (win, dil) in enumerate(DIL_GROUPS):
        assert win // (2 * dil) == DIL_HALF
        o, lse = _dil_group(qkv, t5_table, gi, dil, col)
        outs.append(o.reshape(b * s, DIL_DIM))
        lses.append(lse.reshape(b * s, DIL_DIM))
    tm = 1024
    spec = pl.BlockSpec((tm, DIL_DIM), lambda i: (i, 0))
    out = pl.pallas_call(
        _dil_combine_kernel,
        grid=(b * s // tm,),
        in_specs=[spec] * (2 * DIL_N_GROUPS),
        out_specs=spec,
        out_shape=jax.ShapeDtypeStruct((b * s, DIL_DIM), F32),
        compiler_params=pltpu.CompilerParams(
            dimension_semantics=("arbitrary",), vmem_limit_bytes=VMEM_LIMIT_BYTES),
        name="dilated_combine",
    )(*outs, *lses)
    return out.reshape(b, s, DIL_DIM)


SSM_CONV_TOK = 512
SSM_HALO = SUBLANES
SSM_HEADS_PER_GROUP = SSM_HEADS // SSM_GROUPS
SSM_GROUP_W = SSM_HEADS_PER_GROUP * SSM_HD
SSM_BC_W = SSM_GROUPS * SSM_STATE


def _softplus(x):
    return jnp.maximum(x, 0.0) + jnp.log1p(jnp.exp(-jnp.abs(x)))


def _ssm_conv_kernel(prev_ref, cur_ref, next_ref, w_ref, b_ref, out_ref, cat_scr):
    j = pl.program_id(1)
    tl = cur_ref.shape[1]
    cat_scr[0:SSM_HALO, :] = jnp.where(j > 0, prev_ref[0], 0.0)
    cat_scr[SSM_HALO:SSM_HALO + tl, :] = cur_ref[0]
    cat_scr[SSM_HALO + tl:, :] = jnp.where(j < pl.num_programs(1) - 1, next_ref[0], 0.0)
    acc = jnp.zeros((tl, CONV_CH), F32) + b_ref[...]
    for k in range(SSM_CONV):
        off = SSM_HALO + k - SSM_CONV // 2
        acc = acc + cat_scr[off:off + tl, :] * w_ref[k:k + 1, :]
    out_ref[0] = acc * jax.nn.sigmoid(acc)


def _ssm_conv(xbc, conv_w, conv_b, col):
    b, l, _ = xbc.shape
    tl = SSM_CONV_TOK
    n_steps = l // tl
    per = tl // SSM_HALO
    assert col % CONV_CH == 0
    cb = col // CONV_CH
    return pl.pallas_call(
        _ssm_conv_kernel,
        grid=(b, n_steps),
        in_specs=[
            pl.BlockSpec((1, SSM_HALO, CONV_CH), lambda i, j: (i, jnp.maximum(j * per - 1, 0), cb)),
            pl.BlockSpec((1, tl, CONV_CH), lambda i, j: (i, j, cb)),
            pl.BlockSpec((1, SSM_HALO, CONV_CH), lambda i, j: (i, jnp.minimum((j + 1) * per, n_steps * per - 1), cb)),
            pl.BlockSpec((SSM_CONV, CONV_CH), lambda i, j: (0, 0)),
            pl.BlockSpec((1, CONV_CH), lambda i, j: (0, 0)),
        ],
        out_specs=pl.BlockSpec((1, tl, CONV_CH), lambda i, j: (i, j, 0)),
        out_shape=jax.ShapeDtypeStruct((b, l, CONV_CH), F32),
        scratch_shapes=[pltpu.VMEM((tl + 2 * SSM_HALO, CONV_CH), F32)],
        compiler_params=pltpu.CompilerParams(
            dimension_semantics=("arbitrary", "arbitrary"), vmem_limit_bytes=VMEM_LIMIT_BYTES),
        name="ssm_conv",
    )(xbc, xbc, xbc, conv_w, conv_b.reshape(1, CONV_CH))


def _ssd_kernel(xf_ref, dtf_ref, dttf_ref, xb_ref, dtb_ref, dttb_ref,
                expf_ref, bef_ref, aef_ref, btf_ref, atf_ref, expb_ref, beb_ref, aeb_ref, btb_ref, atb_ref,
                yf_ref, yb_ref, state_scr):
    @pl.when(pl.program_id(1) == 0)
    def _():
        state_scr[...] = jnp.zeros(state_scr.shape, F32)

    fwd = _ssd_chunk(xf_ref, dtf_ref, dttf_ref, expf_ref, bef_ref, aef_ref, btf_ref, atf_ref,
                     yf_ref, state_scr.at[0], reverse=False)
    bwd = _ssd_chunk(xb_ref, dtb_ref, dttb_ref, expb_ref, beb_ref, aeb_ref, btb_ref, atb_ref,
                     yb_ref, state_scr.at[1], reverse=True)
    for _ in zip(fwd, bwd):
        pass
    for _ in fwd:
        pass
    for _ in bwd:
        pass


def _ssd_chunk(xbc_ref, dt_ref, dtt_ref, expand_ref, bias_e_ref, a_e_ref, bias_t_ref, a_t_ref,
               y_ref, state_scr, *, reverse):
    q = SSM_CHUNK
    hi = lax.Precision.HIGHEST
    xbc = xbc_ref[0]
    xs = xbc[:, 0:SSM_INNER]
    dt_e = _softplus(jnp.dot(dt_ref[0], expand_ref[...], precision=hi, preferred_element_type=F32)
                     + bias_e_ref[...])
    a_e = dt_e * a_e_ref[...]
    ri = lax.broadcasted_iota(jnp.int32, (q, q), 0)
    ci = lax.broadcasted_iota(jnp.int32, (q, q), 1)
    seen = (ci >= ri) if reverse else (ci <= ri)
    cs_e = jnp.dot(seen.astype(F32), a_e, precision=hi, preferred_element_type=F32)
    dt_t = _softplus(dtt_ref[0] + bias_t_ref[...])
    cs_t = jnp.dot(dt_t * a_t_ref[...], seen.T.astype(F32), precision=hi, preferred_element_type=F32)
    yield
    dtx = xs * dt_e
    last = 0 if reverse else q - 1
    total = cs_e[last:last + 1, :]
    dtx_decayed = jnp.exp(total - cs_e) * dtx
    grow = jnp.exp(cs_e)
    chunk_decay = jnp.exp(total)
    lane_head = lax.broadcasted_iota(jnp.int32, (q, SSM_GROUP_W), 1) // SSM_HD
    for g in range(SSM_GROUPS):
        xl = slice(g * SSM_GROUP_W, (g + 1) * SSM_GROUP_W)
        bg = xbc[:, SSM_INNER + g * SSM_STATE:SSM_INNER + (g + 1) * SSM_STATE]
        cg = xbc[:, SSM_INNER + SSM_BC_W + g * SSM_STATE:SSM_INNER + SSM_BC_W + (g + 1) * SSM_STATE]
        cb = lax.dot_general(cg.astype(MXU_DTYPE), bg.astype(MXU_DTYPE), (((1,), (1,)), ((), ())),
                             preferred_element_type=F32)
        yield
        ms = []
        for r in range(SSM_HEADS_PER_GROUP):
            h = g * SSM_HEADS_PER_GROUP + r
            col = cs_e[:, h * SSM_HD:h * SSM_HD + 1]
            row = cs_t[h:h + 1, :]
            ms.append(cb * jnp.where(seen, jnp.exp(col - row), 0.0))
        y_all = jnp.dot(jnp.concatenate(ms, axis=0).astype(MXU_DTYPE), dtx[:, xl].astype(MXU_DTYPE),
                        preferred_element_type=F32)
        yield
        y_diag = jnp.zeros((q, SSM_GROUP_W), F32)
        for r in range(SSM_HEADS_PER_GROUP):
            y_diag = y_diag + jnp.where(lane_head == r, y_all[r * q:(r + 1) * q, :], 0.0)
        s_in = state_scr[g]
        y_off = jnp.dot(cg.astype(MXU_DTYPE), s_in.astype(MXU_DTYPE), preferred_element_type=F32) * grow[:, xl]
        y_ref[0, :, xl] = y_diag + y_off
        yield
        new = jnp.dot(bg.T.astype(MXU_DTYPE), dtx_decayed[:, xl].astype(MXU_DTYPE), preferred_element_type=F32)
        state_scr[g] = s_in * chunk_decay[:, xl] + new


def _ssd_scans(xbc_act, dt_arr, dt_t, a_log, dt_bias, col_dt, dt_w):
    b, l, _ = xbc_act.shape
    assert col_dt % dt_w == 0 and dt_w >= 2 * SSM_HEADS
    q = SSM_CHUNK
    nc = l // q
    head_of_lane = np.arange(SSM_INNER) // SSM_HD
    full = lambda r, c: pl.BlockSpec((r, c), lambda i, j: (0, 0))
    chunk = (lambda j: j, lambda j: nc - 1 - j)
    data_specs, param_specs, params = [], [], []
    for d in range(2):
        a = -jnp.exp(a_log[d].astype(F32))
        bias = dt_bias[d].astype(F32)
        expand = (np.arange(dt_w)[:, None] == d * SSM_HEADS + head_of_lane[None, :]).astype(np.float32)
        data_specs += [
            pl.BlockSpec((1, q, CONV_CH), lambda i, j, d=d: (i, chunk[d](j), 0)),
            pl.BlockSpec((1, q, dt_w), lambda i, j, d=d: (i, chunk[d](j), col_dt // dt_w)),
            pl.BlockSpec((1, SSM_HEADS, q), lambda i, j, d=d: (i * 2 + d, 0, chunk[d](j))),
        ]
        param_specs += [full(dt_w, SSM_INNER), full(1, SSM_INNER), full(1, SSM_INNER),
                        full(SSM_HEADS, 1), full(SSM_HEADS, 1)]
        params += [jnp.asarray(expand), bias[head_of_lane].reshape(1, SSM_INNER),
                   a[head_of_lane].reshape(1, SSM_INNER), bias.reshape(SSM_HEADS, 1), a.reshape(SSM_HEADS, 1)]
    y_shape = jax.ShapeDtypeStruct((b, l, SSM_INNER), F32)
    return pl.pallas_call(
        _ssd_kernel,
        grid=(b, nc),
        in_specs=data_specs + param_specs,
        out_specs=[pl.BlockSpec((1, q, SSM_INNER), lambda i, j, d=d: (i, chunk[d](j), 0)) for d in range(2)],
        out_shape=[y_shape, y_shape],
        scratch_shapes=[pltpu.VMEM((2, SSM_GROUPS, SSM_STATE, SSM_GROUP_W), F32)],
        compiler_params=pltpu.CompilerParams(
            dimension_semantics=("arbitrary", "arbitrary"), vmem_limit_bytes=VMEM_LIMIT_BYTES),
        name="ssd_scans",
    )(xbc_act, dt_arr, dt_t, xbc_act, dt_arr, dt_t, *params)


def _ssm_gate_kernel(yf_ref, yb_ref, xbc_ref, z_ref, d_ref, g_ref, out_ref):
    z = z_ref[...]
    y = (yf_ref[...] + yb_ref[...] + xbc_ref[...] * d_ref[...]) * (z * jax.nn.sigmoid(z))
    out_ref[...] = y * lax.rsqrt(jnp.mean(y * y, axis=-1, keepdims=True) + EPS) * g_ref[...]


def mamba2_mixer(arr, conv_w, conv_b, A_log, dt_bias, D_skip, norm_g,
                 col_z=0, col_xbc=CONV_CH, col_dt=SSM_INNER + CONV_CH, dt_w=2 * SSM_HEADS):
    b, l, width = arr.shape
    assert col_z % SSM_INNER == 0
    xbc_act = _ssm_conv(arr, conv_w, conv_b, col_xbc)
    dt_t = jnp.swapaxes(arr[:, :, col_dt:col_dt + 2 * SSM_HEADS], 1, 2).reshape(b * 2, SSM_HEADS, l)
    y_f, y_b = _ssd_scans(xbc_act, arr, dt_t, A_log, dt_bias, col_dt, dt_w)
    tm = 1024
    tok = lambda cb: pl.BlockSpec((tm, SSM_INNER), lambda i: (i, cb))
    row = pl.BlockSpec((1, SSM_INNER), lambda i: (0, 0))
    d_e = D_skip.astype(F32)[np.arange(SSM_INNER) // SSM_HD].reshape(1, SSM_INNER)
    out = pl.pallas_call(
        _ssm_gate_kernel,
        grid=(b * l // tm,),
        in_specs=[tok(0), tok(0), tok(0), tok(col_z // SSM_INNER), row, row],
        out_specs=tok(0),
        out_shape=jax.ShapeDtypeStruct((b * l, SSM_INNER), F32),
        compiler_params=pltpu.CompilerParams(
            dimension_semantics=("arbitrary",), vmem_limit_bytes=VMEM_LIMIT_BYTES),
        name="ssm_gate",
    )(y_f.reshape(b * l, SSM_INNER), y_b.reshape(b * l, SSM_INNER), xbc_act.reshape(b * l, CONV_CH),
      arr.reshape(b * l, width), d_e, norm_g.reshape(1, SSM_INNER))
    return out.reshape(b, l, SSM_INNER)


NA_DIM = NA_HEADS * NA_HD
NA_ROWS_PER_STEP = 8
NA_WIN = NA_ROWS * GRID_W
NA_STEP_TOK = NA_ROWS_PER_STEP * GRID_W


def _na_bias_table(rpb):
    n_dc = 2 * NA_COLS - 1
    edge_l = jnp.repeat(rpb[:, :, :1], GRID_W, axis=2)
    edge_r = jnp.repeat(rpb[:, :, -1:], GRID_W, axis=2)
    ext = jnp.concatenate([edge_l, rpb.astype(F32), edge_r], axis=2)
    by_col = jnp.stack([ext[:, :, GRID_W + NA_COLS - 1 - qc:2 * GRID_W + NA_COLS - 1 - qc]
                        for qc in range(GRID_W)], axis=2)
    qc = np.arange(GRID_W)[:, None]
    kc = np.arange(GRID_W)[None, :]
    cs = np.clip(qc - NA_COLS // 2, 0, GRID_W - NA_COLS)
    ok = (kc >= cs) & (kc < cs + NA_COLS)
    by_col = jnp.where(ok[None, None], by_col, -jnp.inf)
    tabs = []
    for delta in range(NA_ROWS):
        rows = by_col[:, NA_ROWS - 1 - delta:2 * NA_ROWS - 1 - delta]
        tabs.append(rows.transpose(0, 2, 1, 3).reshape(NA_HEADS * GRID_W, NA_WIN))
    assert n_dc == rpb.shape[2]
    return jnp.stack(tabs, axis=0)


def _na_kernel(prev_ref, cur_ref, next_ref, tab_ref, out_ref, k_scr, v_scr, *, n_rows):
    step = pl.program_id(1)
    for i, ref in enumerate((prev_ref, cur_ref, next_ref)):
        k_scr[i * NA_STEP_TOK:(i + 1) * NA_STEP_TOK, :] = ref[0, :, NA_DIM:2 * NA_DIM].astype(MXU_DTYPE)
        v_scr[i * NA_STEP_TOK:(i + 1) * NA_STEP_TOK, :] = ref[0, :, 2 * NA_DIM:3 * NA_DIM].astype(MXU_DTYPE)
    lane_head = lax.broadcasted_iota(jnp.int32, (GRID_W, NA_DIM), 1) // NA_HD
    row0 = step * NA_ROWS_PER_STEP
    for j in range(NA_ROWS_PER_STEP):
        r = row0 + j
        r0 = jnp.clip(r - NA_ROWS // 2, 0, n_rows - NA_ROWS)
        start = pl.multiple_of((r0 - row0 + NA_ROWS_PER_STEP) * GRID_W, GRID_W)
        q = cur_ref[0, j * GRID_W:(j + 1) * GRID_W, 0:NA_DIM] * np.float32(NA_HD ** -0.5)
        qs = jnp.concatenate([jnp.where(lane_head == h, q, 0.0) for h in range(NA_HEADS)], axis=0)
        kw = k_scr[pl.ds(start, NA_WIN), :]
        vw = v_scr[pl.ds(start, NA_WIN), :]
        logits = lax.dot_general(qs.astype(MXU_DTYPE), kw, (((1,), (1,)), ((), ())),
                                 preferred_element_type=F32) + tab_ref[r - r0]
        m = jnp.max(logits, axis=-1, keepdims=True)
        p = jnp.exp(logits - m)
        denom = jnp.sum(p, axis=-1, keepdims=True)
        o_all = jnp.dot(p.astype(MXU_DTYPE), vw, preferred_element_type=F32) / denom
        o = jnp.zeros((GRID_W, NA_DIM), F32)
        for h in range(NA_HEADS):
            o = o + jnp.where(lane_head == h, o_all[h * GRID_W:(h + 1) * GRID_W, :], 0.0)
        out_ref[0, j * GRID_W:(j + 1) * GRID_W, :] = o


def na_mixer(qkv, rpb, col=0):
    b, s, _ = qkv.shape
    n_rows = s // GRID_W
    assert n_rows >= NA_ROWS and n_rows % NA_ROWS_PER_STEP == 0 and col % (3 * NA_DIM) == 0
    n_steps = n_rows // NA_ROWS_PER_STEP
    blk = (1, NA_STEP_TOK, 3 * NA_DIM)
    cb = col // (3 * NA_DIM)
    return pl.pallas_call(
        functools.partial(_na_kernel, n_rows=n_rows),
        grid=(b, n_steps),
        in_specs=[
            pl.BlockSpec(blk, lambda i, j: (i, jnp.maximum(j - 1, 0), cb)),
            pl.BlockSpec(blk, lambda i, j: (i, j, cb)),
            pl.BlockSpec(blk, lambda i, j: (i, jnp.minimum(j + 1, n_steps - 1), cb)),
            pl.BlockSpec((NA_ROWS, NA_HEADS * GRID_W, NA_WIN), lambda i, j: (0, 0, 0)),
        ],
        out_specs=pl.BlockSpec((1, NA_STEP_TOK, NA_DIM), lambda i, j: (i, j, 0)),
        out_shape=jax.ShapeDtypeStruct((b, s, NA_DIM), F32),
        scratch_shapes=[
            pltpu.VMEM((3 * NA_STEP_TOK, NA_DIM), MXU_DTYPE),
            pltpu.VMEM((3 * NA_STEP_TOK, NA_DIM), MXU_DTYPE),
        ],
        compiler_params=pltpu.CompilerParams(
            dimension_semantics=("arbitrary", "arbitrary"), vmem_limit_bytes=VMEM_LIMIT_BYTES),
        name="na_attention",
    )(qkv, qkv, qkv, _na_bias_table(rpb))


PK_GATE = 0
PK_XBC = PK_GATE + N_BRANCH * D_MODEL
PK_CQ = PK_XBC + CONV_CH
PK_DIL = PK_CQ + MLA_Q_RANK
PK_NA = PK_DIL + DIL_N_GROUPS * 3 * DIL_DIM
PK_KR = PK_NA + 3 * NA_DIM
PK_Z = PK_KR + 2 * MLA_HG
PK_CKV = PK_Z + SSM_INNER
PK_DT = PK_CKV + MLA_KV_RANK
PK_DT_W = LANES
PK_WIDTH = PK_DT + PK_DT_W
INPROJ_TM = 512
INPROJ_TN = PK_WIDTH // 2
MERGE_TM = 512


def _pack_w_in(w_in_l):
    gate, a_cq, a_ckv, a_kr, b_qkv, c_z, c_xbc, c_dt, d_qkv = jnp.split(w_in_l, IN_SPLITS, axis=-1)
    zeros = lambda n: jnp.zeros((D_MODEL, n), w_in_l.dtype)
    cols = [gate, c_xbc, a_cq, b_qkv, d_qkv, mla_rotary_key_columns(a_kr), c_z, a_ckv,
            c_dt, zeros(PK_DT_W - 2 * SSM_HEADS)]
    packed = jnp.concatenate(cols, axis=-1)
    assert packed.shape[1] == PK_DT + PK_DT_W
    return jnp.concatenate([packed, zeros(PK_WIDTH - packed.shape[1])], axis=-1).astype(MXU_DTYPE)


def _inproj_kernel(x_ref, g_ref, w_ref, out_ref, h_scr):
    @pl.when(pl.program_id(1) == 0)
    def _():
        x = x_ref[...]
        h_scr[...] = (x * lax.rsqrt(jnp.mean(x * x, axis=-1, keepdims=True) + EPS) * g_ref[...]).astype(MXU_DTYPE)

    out_ref[...] = jnp.dot(h_scr[...], w_ref[...], preferred_element_type=F32)


def _inproj(x2d, norm_g, w_packed):
    n_tok = x2d.shape[0]
    tm, tn = INPROJ_TM, INPROJ_TN
    return pl.pallas_call(
        _inproj_kernel,
        grid=(n_tok // tm, PK_WIDTH // tn),
        in_specs=[pl.BlockSpec((tm, D_MODEL), lambda i, j: (i, 0)),
                  pl.BlockSpec((1, D_MODEL), lambda i, j: (0, 0)),
                  pl.BlockSpec((D_MODEL, tn), lambda i, j: (0, j))],
        out_specs=pl.BlockSpec((tm, tn), lambda i, j: (i, j)),
        out_shape=jax.ShapeDtypeStruct((n_tok, PK_WIDTH), F32),
        scratch_shapes=[pltpu.VMEM((tm, D_MODEL), MXU_DTYPE)],
        compiler_params=pltpu.CompilerParams(
            dimension_semantics=("arbitrary", "arbitrary"), vmem_limit_bytes=VMEM_LIMIT_BYTES),
        name="in_projection",
    )(x2d, norm_g.reshape(1, D_MODEL), w_packed)


def _merge_kernel(x_ref, gate_ref, bg_ref, ya_ref, yb_ref, yc_ref, yd_ref, wb_ref, wo_ref, out_ref):
    merged = jnp.zeros(x_ref.shape, F32)
    for i, y_ref in enumerate((ya_ref, yb_ref, yc_ref, yd_ref)):
        proj = jnp.dot(y_ref[...].astype(MXU_DTYPE), wb_ref[BRANCH_ROWS[i]:BRANCH_ROWS[i + 1], :],
                       preferred_element_type=F32)
        lanes = slice(i * D_MODEL, (i + 1) * D_MODEL)
        merged = merged + jax.nn.sigmoid(gate_ref[:, lanes] + bg_ref[:, lanes]) * proj
    out_ref[...] = x_ref[...] + jnp.dot(merged.astype(MXU_DTYPE), wo_ref[...], preferred_element_type=F32)


def _merge(x2d, packed, b_gate, ys, w_branch, w_out):
    n_tok = x2d.shape[0]
    tm = MERGE_TM
    tok = lambda w: pl.BlockSpec((tm, w), lambda i: (i, 0))
    full = lambda r, c: pl.BlockSpec((r, c), lambda i: (0, 0))
    n_gate = N_BRANCH * D_MODEL
    return pl.pallas_call(
        _merge_kernel,
        grid=(n_tok // tm,),
        in_specs=[tok(D_MODEL), tok(n_gate), full(1, n_gate)] + [tok(w) for w in BRANCH_WIDTHS]
                 + [full(BRANCH_ROWS[-1], D_MODEL), full(D_MODEL, D_MODEL)],
        out_specs=tok(D_MODEL),
        out_shape=jax.ShapeDtypeStruct((n_tok, D_MODEL), F32),
        compiler_params=pltpu.CompilerParams(
            dimension_semantics=("arbitrary",), vmem_limit_bytes=VMEM_LIMIT_BYTES),
        name="branch_merge",
    )(x2d, packed, b_gate.reshape(1, n_gate), *[y.reshape(n_tok, -1) for y in ys], w_branch, w_out)


def encoder(x, norm1_g, w_in_packed, b_gate, mla_q_norm, mla_w_qb, mla_kv_norm, mla_w_kvb, t5_table,
            ssm_conv_w, ssm_conv_b, ssm_A_log, ssm_dt_bias, ssm_D, ssm_norm_g, na_rpb,
            w_branch, w_out, norm2_g, peer_wq, peer_keys, peer_u, peer_vt, final_g):
    b, s, _ = x.shape
    x2d = x.reshape(b * s, D_MODEL)
    for l in range(DEPTH):
        packed = _inproj(x2d, norm1_g[l], w_in_packed[l])
        p3 = packed.reshape(b, s, PK_WIDTH)
        y_a = mla_mixer(p3, mla_q_norm[l], mla_w_qb[l], mla_kv_norm[l], mla_w_kvb[l],
                        col_q=PK_CQ, col_kv=PK_CKV, col_kr=PK_KR)
        y_b = dilated_mixer(p3, t5_table, col=PK_DIL)
        y_c = mamba2_mixer(p3, ssm_conv_w[l], ssm_conv_b[l], ssm_A_log[l], ssm_dt_bias[l], ssm_D[l],
                           ssm_norm_g[l], col_z=PK_Z, col_xbc=PK_XBC, col_dt=PK_DT, dt_w=PK_DT_W)
        y_d = na_mixer(p3, na_rpb[l], col=PK_NA)
        x2d = _merge(x2d, packed, b_gate[l], (y_a, y_b, y_c, y_d), w_branch[l], w_out[l])
        x2d = peer_block(x2d, norm2_g[l], peer_wq[l], peer_keys[l], peer_u[l], peer_vt[l],
                         final_g, final_norm=(l == DEPTH - 1))
    return x2d.reshape(b, s, D_MODEL)


def kernel(x_prompt, x_sample, norm1_g, w_in, b_gate, mla_q_norm, mla_w_qb, mla_kv_norm, mla_w_kvb, t5_table, ssm_conv_w, ssm_conv_b, ssm_A_log, ssm_dt_bias, ssm_D, ssm_norm_g, na_rpb, w_branch, w_out, norm2_g, peer_wq, peer_keys, peer_u, peer_v, final_g):
    peer_u16 = peer_u.astype(MXU_DTYPE)
    peer_vt16 = jnp.swapaxes(peer_v, 1, 2).astype(MXU_DTYPE)
    w_in_packed = jnp.stack([_pack_w_in(w_in[l]) for l in range(DEPTH)])
    shared = (norm1_g, w_in_packed, b_gate, mla_q_norm, mla_w_qb, mla_kv_norm, mla_w_kvb, t5_table,
              ssm_conv_w, ssm_conv_b, ssm_A_log, ssm_dt_bias, ssm_D, ssm_norm_g, na_rpb,
              w_branch.astype(MXU_DTYPE), w_out.astype(MXU_DTYPE), norm2_g, peer_wq, peer_keys,
              peer_u16, peer_vt16, final_g)
    y_prompt = encoder(x_prompt, *shared)
    y_sample = encoder(x_sample, *shared)
    return (y_prompt, y_sample)
```

```python
import functools
import math

import numpy as np
import jax
import jax.numpy as jnp
from jax import lax
from jax.experimental import pallas as pl
from jax.experimental.pallas import tpu as pltpu

F32 = jnp.float32
BF16 = jnp.bfloat16
MXU_DTYPE = BF16

D_MODEL = 1024
DEPTH = 2
GRID_W = 64
EPS = 1e-6
N_BRANCH = 4

MLA_HEADS = 4
MLA_Q_RANK = 256
MLA_KV_RANK = 128
MLA_NOPE = 64
MLA_ROPE = 32
MLA_V = 64
ROPE_THETA = 10000.0

DIL_GROUPS = ((128, 1), (512, 4), (2048, 16))
DIL_HEADS = 4
DIL_HD = 64
T5_BUCKETS = 32
T5_MAX_DIST = 1024

SSM_HEADS = 8
SSM_HD = 64
SSM_INNER = SSM_HEADS * SSM_HD
SSM_GROUPS = 2
SSM_STATE = 128
SSM_CONV = 7
SSM_CHUNK = 128
CONV_CH = SSM_INNER + 2 * SSM_GROUPS * SSM_STATE

NA_HEADS = 4
NA_HD = 64
NA_ROWS = 8
NA_COLS = 16

PEER_HEADS = 8
PEER_KEYS = 128
PEER_EXPERTS = PEER_KEYS * PEER_KEYS
PEER_QDIM = 256
PEER_TOPK = 16

BRANCH_WIDTHS = (MLA_HEADS * MLA_V, DIL_HEADS * DIL_HD, SSM_INNER, NA_HEADS * NA_HD)
BRANCH_ROWS = tuple(sum(BRANCH_WIDTHS[:i]) for i in range(N_BRANCH + 1))
IN_SIZES = (N_BRANCH * D_MODEL, MLA_Q_RANK, MLA_KV_RANK, MLA_ROPE,
            len(DIL_GROUPS) * 3 * DIL_HEADS * DIL_HD,
            SSM_INNER, CONV_CH, 2 * SSM_HEADS,
            3 * NA_HEADS * NA_HD)
IN_SPLITS = tuple(sum(IN_SIZES[:i + 1]) for i in range(len(IN_SIZES) - 1))

VMEM_LIMIT_BYTES = 56 * 1024 * 1024
LANES = 128
SUBLANES = 8


PEER_ROUTE_TB = 512
PEER_TB = 256
PEER_EC = 1024
PEER_GATE_DTYPE = BF16
PEER_GATE_ROWS = 16


def _gate_pack():
    return 4 // jnp.dtype(PEER_GATE_DTYPE).itemsize


def _gate_to_storage(x):
    x = x.astype(PEER_GATE_DTYPE)
    return x if _gate_pack() == 1 else pltpu.bitcast(x, jnp.uint32)


def _gate_from_storage(x):
    return x if _gate_pack() == 1 else pltpu.bitcast(x, PEER_GATE_DTYPE)


def _gate_splat(row):
    return jnp.broadcast_to(row.astype(PEER_GATE_DTYPE), (PEER_GATE_ROWS, row.shape[1]))
PEER_HALF = PEER_QDIM // 2
PEER_CAND_ROWS = 2 * SUBLANES + 7 * SUBLANES + SUBLANES


def _gelu_exact_x2(x):
    return x * (1.0 + lax.erf(x * np.float32(math.sqrt(0.5))))


def _extract_desc(vals, n_out, out_ref, row0):
    for k in range(n_out):
        m = jnp.max(vals, axis=0, keepdims=True)
        out_ref[pl.ds(row0 + k, 1), :] = m
        vals = jnp.where(vals == m, -jnp.inf, vals)


def _oddeven_sort_network(lo, hi):
    def merge(lo, hi, r):
        step = r * 2
        if step < hi - lo:
            yield from merge(lo, hi, step)
            yield from merge(lo + r, hi, step)
            yield from ((i, i + r) for i in range(lo + r, hi - r, step))
        else:
            yield (lo, lo + r)

    if hi - lo >= 1:
        mid = lo + (hi - lo) // 2
        yield from _oddeven_sort_network(lo, mid)
        yield from _oddeven_sort_network(mid + 1, hi)
        yield from merge(lo, hi, 1)


def _bitonic_merge_network(n):
    d = n // 2
    while d >= 1:
        yield from ((i, i + d) for i in range(n) if (i // d) % 2 == 0)
        d //= 2


PEER_SORT_NET = tuple(_oddeven_sort_network(0, PEER_TOPK - 1))
PEER_MERGE_NET = tuple(_bitonic_merge_network(PEER_TOPK))


def _compare_exchange(tiles, network):
    for i, j in network:
        tiles[i], tiles[j] = jnp.maximum(tiles[i], tiles[j]), jnp.minimum(tiles[i], tiles[j])


def _sorted_top(s):
    assert s.shape[0] == PEER_TOPK * SUBLANES
    tiles = [s[i * SUBLANES:(i + 1) * SUBLANES, :] for i in range(PEER_TOPK)]
    _compare_exchange(tiles, PEER_SORT_NET)
    shift = SUBLANES // 2
    while shift >= 1:
        tiles = [jnp.maximum(tiles[i], pltpu.roll(tiles[PEER_TOPK - 1 - i], shift, axis=0))
                 for i in range(PEER_TOPK)]
        _compare_exchange(tiles, PEER_MERGE_NET)
        shift //= 2
    return tiles


def _peer_route_kernel(x_ref, g_ref, wqt_ref, keys_ref,
                       xnt_ref, cnt_ref, rank_ref, e1_ref, e2_ref,
                       qt_scr, top_scr, cand_scr, tops_scr):
    x = x_ref[...]
    xn = x * lax.rsqrt(jnp.mean(x * x, axis=-1, keepdims=True) + EPS) * g_ref[...]
    xnt = xn.T.astype(MXU_DTYPE)
    xnt_ref[...] = xnt
    qt_scr[...] = jnp.dot(wqt_ref[...], xnt, preferred_element_type=F32).astype(MXU_DTYPE)

    def head(h, carry):
        q1 = qt_scr[pl.ds(pl.multiple_of(h * PEER_QDIM, PEER_QDIM), PEER_HALF), :]
        q2 = qt_scr[pl.ds(pl.multiple_of(h * PEER_QDIM + PEER_HALF, PEER_HALF), PEER_HALF), :]
        s1_all = jnp.dot(keys_ref[2 * h], q1, preferred_element_type=F32)
        s2_all = jnp.dot(keys_ref[2 * h + 1], q2, preferred_element_type=F32)
        for l0 in range(0, s1_all.shape[1], LANES):
            lanes = slice(l0, l0 + LANES)
            s1, s2 = s1_all[:, lanes], s2_all[:, lanes]
            top1, top2 = _sorted_top(s1), _sorted_top(s2)
            for k in range(PEER_TOPK):
                top_scr[k:k + 1, lanes] = top1[k][0:1, :]
                top_scr[PEER_TOPK + k:PEER_TOPK + k + 1, lanes] = top2[k][0:1, :]
            t1 = top_scr[0:PEER_TOPK, lanes]
            t2 = top_scr[PEER_TOPK:2 * PEER_TOPK, lanes]
            cand_scr[0:2 * SUBLANES, lanes] = t1[0:1, :] + t2
            for a in range(1, SUBLANES):
                cand_scr[(a + 1) * SUBLANES:(a + 2) * SUBLANES, lanes] = t1[a:a + 1, :] + t2[0:SUBLANES, :]
            cand_scr[9 * SUBLANES:10 * SUBLANES, lanes] = t1[SUBLANES:2 * SUBLANES, :] + t2[0:1, :]
            _extract_desc(cand_scr[:, lanes], PEER_TOPK, tops_scr.at[:, lanes], 0)
            top_s = tops_scr[:, lanes]
            z = jnp.sum(jnp.exp(top_s - top_s[0:1, :]), axis=0, keepdims=True)
            tau = top_s[PEER_TOPK - 1:PEER_TOPK, :]
            rank_tiles = []
            for i in range(PEER_KEYS // SUBLANES):
                s2_tile = s2[i * SUBLANES:(i + 1) * SUBLANES, :]
                rank = jnp.zeros(s2_tile.shape, F32)
                for k in range(PEER_TOPK):
                    rank = rank + jnp.where(top2[k] > s2_tile, 1.0, 0.0)
                rank_tiles.append(rank)
            cnt_top = jnp.zeros(t1.shape, F32)
            for b in range(PEER_TOPK):
                cnt_top = cnt_top + jnp.where(t1 + t2[b:b + 1, :] >= tau, 1.0, 0.0)
            cnt = jnp.zeros(s1.shape, F32)
            for a in range(PEER_TOPK):
                cnt = jnp.where(s1 == t1[a:a + 1, :], cnt_top[a:a + 1, :], cnt)
            cnt_ref[h, :, lanes] = cnt
            rank_ref[h, :, lanes] = _gate_to_storage(jnp.concatenate(rank_tiles, axis=0))
            e1_ref[h, :, lanes] = jnp.exp(s1 - t1[0:1, :]) * (0.5 / z)
            e2_ref[h, :, lanes] = _gate_to_storage(jnp.exp(s2 - t2[0:1, :]))
        return carry

    lax.fori_loop(0, PEER_HEADS, head, 0)


def _peer_route(x2d, g, wqt, keys):
    n_tok = x2d.shape[0]
    tb = PEER_ROUTE_TB
    rt_shape = jax.ShapeDtypeStruct((PEER_HEADS, PEER_KEYS, n_tok), F32)
    gate_rows = PEER_KEYS // _gate_pack()
    gate_store = PEER_GATE_DTYPE if _gate_pack() == 1 else jnp.uint32
    gate_shape = jax.ShapeDtypeStruct((PEER_HEADS, gate_rows, n_tok), gate_store)
    rt_spec = pl.BlockSpec((PEER_HEADS, PEER_KEYS, tb), lambda i: (0, 0, i))
    gate_spec = pl.BlockSpec((PEER_HEADS, gate_rows, tb), lambda i: (0, 0, i))
    return pl.pallas_call(
        _peer_route_kernel,
        grid=(n_tok // tb,),
        in_specs=[
            pl.BlockSpec((tb, D_MODEL), lambda i: (i, 0)),
            pl.BlockSpec((1, D_MODEL), lambda i: (0, 0)),
            pl.BlockSpec((PEER_HEADS * PEER_QDIM, D_MODEL), lambda i: (0, 0)),
            pl.BlockSpec((2 * PEER_HEADS, PEER_KEYS, PEER_HALF), lambda i: (0, 0, 0)),
        ],
        out_specs=[
            pl.BlockSpec((D_MODEL, tb), lambda i: (0, i)),
            rt_spec, gate_spec, rt_spec, gate_spec,
        ],
        out_shape=[
            jax.ShapeDtypeStruct((D_MODEL, n_tok), MXU_DTYPE),
            rt_shape, gate_shape, rt_shape, gate_shape,
        ],
        scratch_shapes=[
            pltpu.VMEM((PEER_HEADS * PEER_QDIM, tb), MXU_DTYPE),
            pltpu.VMEM((2 * PEER_TOPK, tb), F32),
            pltpu.VMEM((PEER_CAND_ROWS, tb), F32),
            pltpu.VMEM((PEER_TOPK, tb), F32),
        ],
        compiler_params=pltpu.CompilerParams(
            dimension_semantics=("arbitrary",), vmem_limit_bytes=VMEM_LIMIT_BYTES),
        name="peer_route",
    )(x2d, g, wqt, keys)


def _peer_expert_kernel(x_ref, xnt_ref, cnt_ref, rank_ref, e1_ref, e2_ref, u_ref, vt_ref, fg_ref,
                        out_ref, acc_scr, w_scr, *, final_norm):
    c = pl.program_id(1)
    n_chunks = pl.num_programs(1) - 1
    n_i1 = PEER_EC // PEER_KEYS
    tb = acc_scr.shape[1]
    gdt = PEER_GATE_DTYPE

    @pl.when(c == 0)
    def _():
        acc_scr[...] = jnp.zeros_like(acc_scr)
        w_scr[1] = jnp.zeros(w_scr.shape[1:], w_scr.dtype)

    slot = c % 2
    n_groups = PEER_KEYS // PEER_GATE_ROWS
    acc_scr[...] += jnp.dot(vt_ref[...], w_scr[(c + 1) % 2], preferred_element_type=F32)
    hid = jnp.dot(u_ref[...], xnt_ref[...], preferred_element_type=F32)

    def gate_block(i1l, l0):
        lanes = slice(l0, l0 + LANES)
        gates = [jnp.zeros((PEER_GATE_ROWS, LANES), gdt) for _ in range(n_groups)]
        for h in range(PEER_HEADS):
            cntb = _gate_splat(cnt_ref[h, i1l:i1l + 1, lanes])
            e1b = _gate_splat(e1_ref[h, i1l:i1l + 1, lanes])
            for k in range(n_groups):
                rows = slice(k * SUBLANES, (k + 1) * SUBLANES)
                w = _gate_from_storage(e2_ref[h, rows, lanes]) * e1b
                rank = _gate_from_storage(rank_ref[h, rows, lanes])
                gates[k] = gates[k] + jnp.where(rank < cntb, w, jnp.zeros_like(w))
        for k in range(n_groups):
            r0 = k * PEER_GATE_ROWS
            e0 = i1l * PEER_KEYS + r0
            act = _gelu_exact_x2(hid[e0:e0 + PEER_GATE_ROWS, lanes]).astype(gdt)
            w_scr[slot, e0:e0 + PEER_GATE_ROWS, lanes] = (act * gates[k]).astype(MXU_DTYPE)

    for i1l in range(n_i1):
        for l0 in range(0, tb, LANES):
            gate_block(i1l, l0)

    @pl.when(c == n_chunks)
    def _():
        y = x_ref[...] + acc_scr[...].T
        if final_norm:
            y = y * lax.rsqrt(jnp.mean(y * y, axis=-1, keepdims=True) + EPS) * fg_ref[...]
        out_ref[...] = y


def _peer_experts(x2d, xnt, cnt, rank2, e1, e2, u, vt, final_g, final_norm):
    n_tok = x2d.shape[0]
    tb, ec = PEER_TB, PEER_EC
    n_chunks = PEER_EXPERTS // ec
    assert PEER_GATE_ROWS == SUBLANES * _gate_pack()
    rt_spec = pl.BlockSpec((PEER_HEADS, PEER_KEYS // _gate_pack(), tb), lambda j, c: (0, 0, j))
    row_spec = pl.BlockSpec((PEER_HEADS, ec // PEER_KEYS, tb),
                            lambda j, c: (0, jnp.minimum(c, n_chunks - 1), j))
    return pl.pallas_call(
        functools.partial(_peer_expert_kernel, final_norm=final_norm),
        grid=(n_tok // tb, n_chunks + 1),
        in_specs=[
            pl.BlockSpec((tb, D_MODEL), lambda j, c: (j, 0)),
            pl.BlockSpec((D_MODEL, tb), lambda j, c: (0, j)),
            row_spec, rt_spec, row_spec, rt_spec,
            pl.BlockSpec((ec, D_MODEL), lambda j, c: (jnp.minimum(c, n_chunks - 1), 0)),
            pl.BlockSpec((D_MODEL, ec), lambda j, c: (0, jnp.maximum(c - 1, 0))),
            pl.BlockSpec((1, D_MODEL), lambda j, c: (0, 0)),
        ],
        out_specs=pl.BlockSpec((tb, D_MODEL), lambda j, c: (j, 0)),
        out_shape=jax.ShapeDtypeStruct((n_tok, D_MODEL), F32),
        scratch_shapes=[
            pltpu.VMEM((D_MODEL, tb), F32),
            pltpu.VMEM((2, ec, tb), MXU_DTYPE),
        ],
        compiler_params=pltpu.CompilerParams(
            dimension_semantics=("arbitrary", "arbitrary"), vmem_limit_bytes=VMEM_LIMIT_BYTES),
        name="peer_experts",
    )(x2d, xnt, cnt, rank2, e1, e2, u, vt, final_g.reshape(1, D_MODEL))


def peer_block(x2d, norm_g, w_q, keys, u, v, final_g, final_norm=False):
    wqt = w_q.T.astype(MXU_DTYPE)
    keys2 = keys.reshape(2 * PEER_HEADS, PEER_KEYS, PEER_HALF).astype(MXU_DTYPE)
    xnt, cnt, rank2, e1, e2 = _peer_route(x2d, norm_g.reshape(1, D_MODEL), wqt, keys2)
    return _peer_experts(x2d, xnt, cnt, rank2, e1, e2, u, v, final_g, final_norm)


MLA_HG = LANES
MLA_QK_W = MLA_HEADS * MLA_HG
MLA_V_W = MLA_HEADS * MLA_V
MLA_PREP_TOK = 512
MLA_TQ = 512
MLA_TK = 2048
MLA_RHALF = MLA_ROPE // 2


def _mla_rope_tables(s):
    inv = ROPE_THETA ** (-jnp.arange(MLA_RHALF, dtype=F32) / MLA_RHALF)
    ang = jnp.arange(s).astype(F32)[:, None] * inv[None, :]
    cos, sin = jnp.cos(ang), jnp.sin(ang)
    zero_pad = jnp.zeros((s, MLA_HG - MLA_NOPE - MLA_ROPE), F32)
    cos_rot = jnp.concatenate([cos, cos, zero_pad], axis=1)
    sin_rot = jnp.concatenate([-sin, sin, zero_pad], axis=1)
    scale = np.float32((MLA_NOPE + MLA_ROPE) ** -0.5)
    q_cos = scale * jnp.concatenate([jnp.ones((s, MLA_NOPE), F32), cos_rot], axis=1)
    q_sin = scale * jnp.concatenate([jnp.zeros((s, MLA_NOPE), F32), sin_rot], axis=1)
    k_cos = jnp.concatenate([jnp.zeros((s, MLA_NOPE), F32), cos_rot], axis=1)
    k_sin = jnp.concatenate([jnp.zeros((s, MLA_NOPE), F32), sin_rot], axis=1)
    return q_cos, q_sin, k_cos, k_sin


def _mla_pack_weights(w_qb, w_kvb):
    hd_q = MLA_NOPE + MLA_ROPE
    wq = w_qb.reshape(MLA_Q_RANK, MLA_HEADS, hd_q)
    rot = wq[:, :, MLA_NOPE:]
    rot_sw = jnp.concatenate([rot[:, :, MLA_RHALF:], rot[:, :, :MLA_RHALF]], axis=2)
    pad = jnp.zeros((MLA_Q_RANK, MLA_HEADS, MLA_HG - hd_q), F32)
    wq_a = jnp.concatenate([wq, pad], axis=2).reshape(MLA_Q_RANK, MLA_QK_W)
    wq_b = jnp.concatenate([jnp.zeros_like(wq[:, :, :MLA_NOPE]), rot_sw, pad], axis=2).reshape(MLA_Q_RANK, MLA_QK_W)
    wkv = w_kvb.reshape(MLA_KV_RANK, MLA_HEADS, MLA_NOPE + MLA_V)
    wk = jnp.concatenate([wkv[:, :, :MLA_NOPE], jnp.zeros((MLA_KV_RANK, MLA_HEADS, MLA_HG - MLA_NOPE), F32)],
                         axis=2).reshape(MLA_KV_RANK, MLA_QK_W)
    wv = jnp.concatenate([wkv[:, :, MLA_NOPE:], jnp.zeros((MLA_KV_RANK, MLA_HEADS, MLA_HG - MLA_V), F32)],
                         axis=2).reshape(MLA_KV_RANK, MLA_QK_W)
    return (wq_a.astype(MXU_DTYPE), wq_b.astype(MXU_DTYPE), wk.astype(MXU_DTYPE), wv.astype(MXU_DTYPE))


def _mla_prep_kernel(cq_ref, ckv_ref, kr_ref, qn_ref, kvn_ref, wqa_ref, wqb_ref, wk_ref, wv_ref,
                     qcos_ref, qsin_ref, kcos_ref, ksin_ref, q_out, kt_out, v_out):
    cq = cq_ref[0]
    cqn = (cq * lax.rsqrt(jnp.mean(cq * cq, axis=-1, keepdims=True) + EPS) * qn_ref[...]).astype(MXU_DTYPE)
    qa = jnp.dot(cqn, wqa_ref[...], preferred_element_type=F32)
    qb = jnp.dot(cqn, wqb_ref[...], preferred_element_type=F32)
    ckv = ckv_ref[0]
    ckvn = (ckv * lax.rsqrt(jnp.mean(ckv * ckv, axis=-1, keepdims=True) + EPS) * kvn_ref[...]).astype(MXU_DTYPE)
    ka = jnp.dot(ckvn, wk_ref[...], preferred_element_type=F32)
    one_lane = (lax.broadcasted_iota(jnp.int32, (1, MLA_QK_W), 1) % MLA_HG == MLA_V).astype(F32)
    v_out[0] = (jnp.dot(ckvn, wv_ref[...], preferred_element_type=F32) + one_lane).astype(MXU_DTYPE)
    kr = kr_ref[0]
    k_rot = kr[:, 0:MLA_HG] * kcos_ref[...] + kr[:, MLA_HG:2 * MLA_HG] * ksin_ref[...]
    for h in range(MLA_HEADS):
        lanes = slice(h * MLA_HG, (h + 1) * MLA_HG)
        q_out[0, :, lanes] = (qa[:, lanes] * qcos_ref[...] + qb[:, lanes] * qsin_ref[...]).astype(MXU_DTYPE)
        kt_out[0, lanes, :] = (ka[:, lanes] + k_rot).T.astype(MXU_DTYPE)


def _mla_flash_kernel(q_ref, kt_ref, v_ref, out_ref, m_scr, acc_scr):
    ki = pl.program_id(2)

    @pl.when(ki == 0)
    def _():
        m_scr[...] = jnp.full(m_scr.shape, -jnp.inf, F32)
        acc_scr[...] = jnp.zeros(acc_scr.shape, F32)

    def logits(h):
        lanes = slice(h * MLA_HG, (h + 1) * MLA_HG)
        return jnp.dot(q_ref[0, :, lanes], kt_ref[0, lanes, :], preferred_element_type=F32)

    n_rep = kt_ref.shape[2] // MLA_HG
    s_next = logits(0)
    for h in range(MLA_HEADS):
        s = s_next
        if h + 1 < MLA_HEADS:
            s_next = logits(h + 1)
        m_old = m_scr[h]
        m_new = jnp.maximum(m_old, jnp.max(s, axis=-1, keepdims=True))
        p = jnp.exp(s - jnp.tile(m_new, (1, n_rep))).astype(MXU_DTYPE)
        acc_scr[h] = jnp.exp(m_old - m_new) * acc_scr[h] + jnp.dot(
            p, v_ref[0, :, h * MLA_HG:(h + 1) * MLA_HG], preferred_element_type=F32)
        m_scr[h] = m_new

    @pl.when(ki == pl.num_programs(2) - 1)
    def _():
        low = lax.broadcasted_iota(jnp.int32, (acc_scr.shape[1], MLA_HG), 1) < MLA_V
        outs = []
        for h in range(MLA_HEADS):
            acc = acc_scr[h]
            outs.append(acc / acc[:, MLA_V:MLA_V + 1])
        for hp in range(MLA_HEADS // 2):
            odd = pltpu.roll(outs[2 * hp + 1], MLA_V, axis=1)
            out_ref[0, :, hp * MLA_HG:(hp + 1) * MLA_HG] = jnp.where(low, outs[2 * hp], odd)


def mla_rotary_key_columns(w):
    zl = jnp.zeros(w.shape[:-1] + (MLA_NOPE,), w.dtype)
    zr = jnp.zeros(w.shape[:-1] + (MLA_HG - MLA_NOPE - MLA_ROPE,), w.dtype)
    w_sw = jnp.concatenate([w[..., MLA_RHALF:], w[..., :MLA_RHALF]], axis=-1)
    return jnp.concatenate([zl, w, zr, zl, w_sw, zr], axis=-1)


def mla_mixer(arr, q_norm, w_qb, kv_norm, w_kvb, col_q=0, col_kv=MLA_Q_RANK, col_kr=MLA_Q_RANK + MLA_KV_RANK):
    b, s, _ = arr.shape
    tt = MLA_PREP_TOK
    assert col_q % MLA_Q_RANK == 0 and col_kv % MLA_KV_RANK == 0 and col_kr % (2 * MLA_HG) == 0
    wqa, wqb, wk, wv = _mla_pack_weights(w_qb, w_kvb)
    q_cos, q_sin, k_cos, k_sin = _mla_rope_tables(s)
    tok = lambda w, c=0: pl.BlockSpec((1, tt, w), lambda i, j: (i, j, c // w))
    full = lambda r, c: pl.BlockSpec((r, c), lambda i, j: (0, 0))
    tab = pl.BlockSpec((tt, MLA_HG), lambda i, j: (j, 0))
    q, kt, v = pl.pallas_call(
        _mla_prep_kernel,
        grid=(b, s // tt),
        in_specs=[tok(MLA_Q_RANK, col_q), tok(MLA_KV_RANK, col_kv), tok(2 * MLA_HG, col_kr),
                  full(1, MLA_Q_RANK), full(1, MLA_KV_RANK),
                  full(MLA_Q_RANK, MLA_QK_W), full(MLA_Q_RANK, MLA_QK_W),
                  full(MLA_KV_RANK, MLA_QK_W), full(MLA_KV_RANK, MLA_QK_W),
                  tab, tab, tab, tab],
        out_specs=[tok(MLA_QK_W), pl.BlockSpec((1, MLA_QK_W, tt), lambda i, j: (i, 0, j)), tok(MLA_QK_W)],
        out_shape=[jax.ShapeDtypeStruct((b, s, MLA_QK_W), MXU_DTYPE),
                   jax.ShapeDtypeStruct((b, MLA_QK_W, s), MXU_DTYPE),
                   jax.ShapeDtypeStruct((b, s, MLA_QK_W), MXU_DTYPE)],
        compiler_params=pltpu.CompilerParams(
            dimension_semantics=("arbitrary", "arbitrary"), vmem_limit_bytes=VMEM_LIMIT_BYTES),
        name="mla_prep",
    )(arr, arr, arr, q_norm.reshape(1, -1), kv_norm.reshape(1, -1), wqa, wqb, wk, wv,
      q_cos, q_sin, k_cos, k_sin)
    tq, tk = min(MLA_TQ, s), min(MLA_TK, s)
    assert s % tq == 0 and s % tk == 0 and s % tt == 0
    return pl.pallas_call(
        _mla_flash_kernel,
        grid=(b, s // tq, s // tk),
        in_specs=[pl.BlockSpec((1, tq, MLA_QK_W), lambda i, j, kk: (i, j, 0)),
                  pl.BlockSpec((1, MLA_QK_W, tk), lambda i, j, kk: (i, 0, kk)),
                  pl.BlockSpec((1, tk, MLA_QK_W), lambda i, j, kk: (i, kk, 0))],
        out_specs=pl.BlockSpec((1, tq, MLA_V_W), lambda i, j, kk: (i, j, 0)),
        out_shape=jax.ShapeDtypeStruct((b, s, MLA_V_W), F32),
        scratch_shapes=[pltpu.VMEM((MLA_HEADS, tq, MLA_HG), F32),
                        pltpu.VMEM((MLA_HEADS, tq, MLA_HG), F32)],
        compiler_params=pltpu.CompilerParams(
            dimension_semantics=("arbitrary", "arbitrary", "arbitrary"), vmem_limit_bytes=VMEM_LIMIT_BYTES),
        name="mla_flash",
    )(q, kt, v)


def t5_bucket(rel):
    nb = T5_BUCKETS // 2
    ret = np.where(rel > 0, nb, 0)
    n = np.abs(rel)
    max_exact = nb // 2
    large = max_exact + (np.log(np.maximum(n, 1) / max_exact) / np.log(T5_MAX_DIST / max_exact)
                         * (nb - max_exact)).astype(np.int64)
    large = np.minimum(large, nb - 1)
    return (ret + np.where(n < max_exact, n, large)).astype(np.int32)


DIL_DIM = DIL_HEADS * DIL_HD
DIL_HALF = 64
DIL_QB = 128
DIL_KW = DIL_QB + 2 * DIL_HALF
DIL_TL = 512
DIL_N_GROUPS = len(DIL_GROUPS)


def _dil_bias_table(t5_table, gi, dil):
    rel = np.arange(DIL_KW)[None, :] - DIL_HALF - np.arange(DIL_QB)[:, None]
    bias = t5_table[:, gi * DIL_HEADS:(gi + 1) * DIL_HEADS][t5_bucket(rel * dil)].astype(F32)
    bias = jnp.where((np.abs(rel) <= DIL_HALF)[:, :, None], bias, -jnp.inf)
    return bias.transpose(2, 0, 1).reshape(DIL_HEADS * DIL_QB, DIL_KW)


def _dil_kernel(prev_ref, cur_ref, next_ref, tab_ref, o_ref, lse_ref, k_scr, v_scr, *, seq_len):
    step = pl.program_id(2)
    for i, ref in enumerate((prev_ref, cur_ref, next_ref)):
        k_scr[i * DIL_TL:(i + 1) * DIL_TL, :] = ref[0, :, DIL_DIM:2 * DIL_DIM].astype(MXU_DTYPE)
        v_scr[i * DIL_TL:(i + 1) * DIL_TL, :] = ref[0, :, 2 * DIL_DIM:3 * DIL_DIM].astype(MXU_DTYPE)
    lane_head = lax.broadcasted_iota(jnp.int32, (DIL_QB, DIL_DIM), 1) // DIL_HD
    key_off = lax.broadcasted_iota(jnp.int32, (1, DIL_KW), 1)
    for n in range(DIL_TL // DIL_QB):
        w0 = DIL_TL + n * DIL_QB - DIL_HALF
        kpos = step * DIL_TL + (n * DIL_QB - DIL_HALF) + key_off
        valid = (kpos >= 0) & (kpos < seq_len)
        q = cur_ref[0, n * DIL_QB:(n + 1) * DIL_QB, 0:DIL_DIM] * np.float32(DIL_HD ** -0.5)
        qs = jnp.concatenate([jnp.where(lane_head == h, q, 0.0) for h in range(DIL_HEADS)], axis=0)
        logits = lax.dot_general(qs.astype(MXU_DTYPE), k_scr[w0:w0 + DIL_KW, :], (((1,), (1,)), ((), ())),
                                 preferred_element_type=F32) + tab_ref[...]
        logits = jnp.where(valid, logits, -jnp.inf)
        m = jnp.max(logits, axis=-1, keepdims=True)
        p = jnp.exp(logits - m)
        denom = jnp.sum(p, axis=-1, keepdims=True)
        o_all = jnp.dot(p.astype(MXU_DTYPE), v_scr[w0:w0 + DIL_KW, :], preferred_element_type=F32) / denom
        lse_all = m + jnp.log(denom)
        o = jnp.zeros((DIL_QB, DIL_DIM), F32)
        lse = jnp.zeros((DIL_QB, DIL_DIM), F32)
        for h in range(DIL_HEADS):
            rows = slice(h * DIL_QB, (h + 1) * DIL_QB)
            o = o + jnp.where(lane_head == h, o_all[rows, :], 0.0)
            lse = lse + jnp.where(lane_head == h, lse_all[rows, :], 0.0)
        o_ref[0, n * DIL_QB:(n + 1) * DIL_QB, :] = o
        lse_ref[0, n * DIL_QB:(n + 1) * DIL_QB, :] = lse


def _dil_group(qkv, t5_table, gi, dil, col):
    b, s, width = qkv.shape
    seq_len = s // dil
    assert seq_len % DIL_TL == 0 and col % (3 * DIL_DIM) == 0
    n_steps = seq_len // DIL_TL
    gi_col = col // (3 * DIL_DIM) + gi
    if dil > 1:
        qkv = qkv[:, :, gi_col * 3 * DIL_DIM:(gi_col + 1) * 3 * DIL_DIM]
        width, gi_col = 3 * DIL_DIM, 0
    n_col = width // (3 * DIL_DIM)
    view = qkv.reshape(b, seq_len, dil * width)
    blk = (1, DIL_TL, 3 * DIL_DIM)
    out_shape = jax.ShapeDtypeStruct((b, seq_len, dil * DIL_DIM), F32)
    out_spec = pl.BlockSpec((1, DIL_TL, DIL_DIM), lambda i, r, j: (i, j, r))
    o, lse = pl.pallas_call(
        functools.partial(_dil_kernel, seq_len=seq_len),
        grid=(b, dil, n_steps),
        in_specs=[
            pl.BlockSpec(blk, lambda i, r, j: (i, jnp.maximum(j - 1, 0), r * n_col + gi_col)),
            pl.BlockSpec(blk, lambda i, r, j: (i, j, r * n_col + gi_col)),
            pl.BlockSpec(blk, lambda i, r, j: (i, jnp.minimum(j + 1, n_steps - 1), r * n_col + gi_col)),
            pl.BlockSpec((DIL_HEADS * DIL_QB, DIL_KW), lambda i, r, j: (0, 0)),
        ],
        out_specs=[out_spec, out_spec],
        out_shape=[out_shape, out_shape],
        scratch_shapes=[pltpu.VMEM((3 * DIL_TL, DIL_DIM), MXU_DTYPE),
                        pltpu.VMEM((3 * DIL_TL, DIL_DIM), MXU_DTYPE)],
        compiler_params=pltpu.CompilerParams(
            dimension_semantics=("arbitrary", "arbitrary", "arbitrary"), vmem_limit_bytes=VMEM_LIMIT_BYTES),
        name=f"dilated_attention_g{gi}",
    )(view, view, view, _dil_bias_table(t5_table, gi, dil))
    return o.reshape(b, s, DIL_DIM), lse.reshape(b, s, DIL_DIM)


def _dil_combine_kernel(*refs):
    o_refs, lse_refs, out_ref = refs[:DIL_N_GROUPS], refs[DIL_N_GROUPS:2 * DIL_N_GROUPS], refs[-1]
    lses = [r[...] for r in lse_refs]
    m = functools.reduce(jnp.maximum, lses)
    ws = [jnp.exp(l - m) for l in lses]
    total = functools.reduce(jnp.add, ws)
    acc = functools.reduce(jnp.add, [w * r[...] for w, r in zip(ws, o_refs)])
    out_ref[...] = acc / total


def dilated_mixer(qkv, t5_table, col=0):
    b, s, _ = qkv.shape
    outs, lses = [], []
    for gi, (win, dil) in enumerate(DIL_GROUPS):
        assert win // (2 * dil) == DIL_HALF
        o, lse = _dil_group(qkv, t5_table, gi, dil, col)
        outs.append(o.reshape(b * s, DIL_DIM))
        lses.append(lse.reshape(b * s, DIL_DIM))
    tm = 1024
    spec = pl.BlockSpec((tm, DIL_DIM), lambda i: (i, 0))
    out = pl.pallas_call(
        _dil_combine_kernel,
        grid=(b * s // tm,),
        in_specs=[spec] * (2 * DIL_N_GROUPS),
        out_specs=spec,
        out_shape=jax.ShapeDtypeStruct((b * s, DIL_DIM), F32),
        compiler_params=pltpu.CompilerParams(
            dimension_semantics=("arbitrary",), vmem_limit_bytes=VMEM_LIMIT_BYTES),
        name="dilated_combine",
    )(*outs, *lses)
    return out.reshape(b, s, DIL_DIM)


SSM_CONV_TOK = 512
SSM_HALO = SUBLANES
SSM_HEADS_PER_GROUP = SSM_HEADS // SSM_GROUPS
SSM_GROUP_W = SSM_HEADS_PER_GROUP * SSM_HD
SSM_BC_W = SSM_GROUPS * SSM_STATE


def _softplus(x):
    return jnp.maximum(x, 0.0) + jnp.log1p(jnp.exp(-jnp.abs(x)))


def _ssm_conv_kernel(prev_ref, cur_ref, next_ref, w_ref, b_ref, out_ref, cat_scr):
    j = pl.program_id(1)
    tl = cur_ref.shape[1]
    cat_scr[0:SSM_HALO, :] = jnp.where(j > 0, prev_ref[0], 0.0)
    cat_scr[SSM_HALO:SSM_HALO + tl, :] = cur_ref[0]
    cat_scr[SSM_HALO + tl:, :] = jnp.where(j < pl.num_programs(1) - 1, next_ref[0], 0.0)
    acc = jnp.zeros((tl, CONV_CH), F32) + b_ref[...]
    for k in range(SSM_CONV):
        off = SSM_HALO + k - SSM_CONV // 2
        acc = acc + cat_scr[off:off + tl, :] * w_ref[k:k + 1, :]
    out_ref[0] = acc * jax.nn.sigmoid(acc)


def _ssm_conv(xbc, conv_w, conv_b, col):
    b, l, _ = xbc.shape
    tl = SSM_CONV_TOK
    n_steps = l // tl
    per = tl // SSM_HALO
    assert col % CONV_CH == 0
    cb = col // CONV_CH
    return pl.pallas_call(
        _ssm_conv_kernel,
        grid=(b, n_steps),
        in_specs=[
            pl.BlockSpec((1, SSM_HALO, CONV_CH), lambda i, j: (i, jnp.maximum(j * per - 1, 0), cb)),
            pl.BlockSpec((1, tl, CONV_CH), lambda i, j: (i, j, cb)),
            pl.BlockSpec((1, SSM_HALO, CONV_CH), lambda i, j: (i, jnp.minimum((j + 1) * per, n_steps * per - 1), cb)),
            pl.BlockSpec((SSM_CONV, CONV_CH), lambda i, j: (0, 0)),
            pl.BlockSpec((1, CONV_CH), lambda i, j: (0, 0)),
        ],
        out_specs=pl.BlockSpec((1, tl, CONV_CH), lambda i, j: (i, j, 0)),
        out_shape=jax.ShapeDtypeStruct((b, l, CONV_CH), F32),
        scratch_shapes=[pltpu.VMEM((tl + 2 * SSM_HALO, CONV_CH), F32)],
        compiler_params=pltpu.CompilerParams(
            dimension_semantics=("arbitrary", "arbitrary"), vmem_limit_bytes=VMEM_LIMIT_BYTES),
        name="ssm_conv",
    )(xbc, xbc, xbc, conv_w, conv_b.reshape(1, CONV_CH))


def _ssd_kernel(xf_ref, dtf_ref, dttf_ref, xb_ref, dtb_ref, dttb_ref,
                expf_ref, bef_ref, aef_ref, btf_ref, atf_ref, expb_ref, beb_ref, aeb_ref, btb_ref, atb_ref,
                yf_ref, yb_ref, state_scr):
    @pl.when(pl.program_id(1) == 0)
    def _():
        state_scr[...] = jnp.zeros(state_scr.shape, F32)

    fwd = _ssd_chunk(xf_ref, dtf_ref, dttf_ref, expf_ref, bef_ref, aef_ref, btf_ref, atf_ref,
                     yf_ref, state_scr.at[0], reverse=False)
    bwd = _ssd_chunk(xb_ref, dtb_ref, dttb_ref, expb_ref, beb_ref, aeb_ref, btb_ref, atb_ref,
                     yb_ref, state_scr.at[1], reverse=True)
    for _ in zip(fwd, bwd):
        pass
    for _ in fwd:
        pass
    for _ in bwd:
        pass


def _ssd_chunk(xbc_ref, dt_ref, dtt_ref, expand_ref, bias_e_ref, a_e_ref, bias_t_ref, a_t_ref,
               y_ref, state_scr, *, reverse):
    q = SSM_CHUNK
    hi = lax.Precision.HIGHEST
    xbc = xbc_ref[0]
    xs = xbc[:, 0:SSM_INNER]
    dt_e = _softplus(jnp.dot(dt_ref[0], expand_ref[...], precision=hi, preferred_element_type=F32)
                     + bias_e_ref[...])
    a_e = dt_e * a_e_ref[...]
    ri = lax.broadcasted_iota(jnp.int32, (q, q), 0)
    ci = lax.broadcasted_iota(jnp.int32, (q, q), 1)
    seen = (ci >= ri) if reverse else (ci <= ri)
    cs_e = jnp.dot(seen.astype(F32), a_e, precision=hi, preferred_element_type=F32)
    dt_t = _softplus(dtt_ref[0] + bias_t_ref[...])
    cs_t = jnp.dot(dt_t * a_t_ref[...], seen.T.astype(F32), precision=hi, preferred_element_type=F32)
    yield
    dtx = xs * dt_e
    last = 0 if reverse else q - 1
    total = cs_e[last:last + 1, :]
    dtx_decayed = jnp.exp(total - cs_e) * dtx
    grow = jnp.exp(cs_e)
    chunk_decay = jnp.exp(total)
    lane_head = lax.broadcasted_iota(jnp.int32, (q, SSM_GROUP_W), 1) // SSM_HD
    for g in range(SSM_GROUPS):
        xl = slice(g * SSM_GROUP_W, (g + 1) * SSM_GROUP_W)
        bg = xbc[:, SSM_INNER + g * SSM_STATE:SSM_INNER + (g + 1) * SSM_STATE]
        cg = xbc[:, SSM_INNER + SSM_BC_W + g * SSM_STATE:SSM_INNER + SSM_BC_W + (g + 1) * SSM_STATE]
        cb = lax.dot_general(cg.astype(MXU_DTYPE), bg.astype(MXU_DTYPE), (((1,), (1,)), ((), ())),
                             preferred_element_type=F32)
        yield
        ms = []
        for r in range(SSM_HEADS_PER_GROUP):
            h = g * SSM_HEADS_PER_GROUP + r
            col = cs_e[:, h * SSM_HD:h * SSM_HD + 1]
            row = cs_t[h:h + 1, :]
            ms.append(cb * jnp.where(seen, jnp.exp(col - row), 0.0))
        y_all = jnp.dot(jnp.concatenate(ms, axis=0).astype(MXU_DTYPE), dtx[:, xl].astype(MXU_DTYPE),
                        preferred_element_type=F32)
        yield
        y_diag = jnp.zeros((q, SSM_GROUP_W), F32)
        for r in range(SSM_HEADS_PER_GROUP):
            y_diag = y_diag + jnp.where(lane_head == r, y_all[r * q:(r + 1) * q, :], 0.0)
        s_in = state_scr[g]
        y_off = jnp.dot(cg.astype(MXU_DTYPE), s_in.astype(MXU_DTYPE), preferred_element_type=F32) * grow[:, xl]
        y_ref[0, :, xl] = y_diag + y_off
        yield
        new = jnp.dot(bg.T.astype(MXU_DTYPE), dtx_decayed[:, xl].astype(MXU_DTYPE), preferred_element_type=F32)
        state_scr[g] = s_in * chunk_decay[:, xl] + new


def _ssd_scans(xbc_act, dt_arr, dt_t, a_log, dt_bias, col_dt, dt_w):
    b, l, _ = xbc_act.shape
    assert col_dt % dt_w == 0 and dt_w >= 2 * SSM_HEADS
    q = SSM_CHUNK
    nc = l // q
    head_of_lane = np.arange(SSM_INNER) // SSM_HD
    full = lambda r, c: pl.BlockSpec((r, c), lambda i, j: (0, 0))
    chunk = (lambda j: j, lambda j: nc - 1 - j)
    data_specs, param_specs, params = [], [], []
    for d in range(2):
        a = -jnp.exp(a_log[d].astype(F32))
        bias = dt_bias[d].astype(F32)
        expand = (np.arange(dt_w)[:, None] == d * SSM_HEADS + head_of_lane[None, :]).astype(np.float32)
        data_specs += [
            pl.BlockSpec((1, q, CONV_CH), lambda i, j, d=d: (i, chunk[d](j), 0)),
            pl.BlockSpec((1, q, dt_w), lambda i, j, d=d: (i, chunk[d](j), col_dt // dt_w)),
            pl.BlockSpec((1, SSM_HEADS, q), lambda i, j, d=d: (i * 2 + d, 0, chunk[d](j))),
        ]
        param_specs += [full(dt_w, SSM_INNER), full(1, SSM_INNER), full(1, SSM_INNER),
                        full(SSM_HEADS, 1), full(SSM_HEADS, 1)]
        params += [jnp.asarray(expand), bias[head_of_lane].reshape(1, SSM_INNER),
                   a[head_of_lane].reshape(1, SSM_INNER), bias.reshape(SSM_HEADS, 1), a.reshape(SSM_HEADS, 1)]
    y_shape = jax.ShapeDtypeStruct((b, l, SSM_INNER), F32)
    return pl.pallas_call(
        _ssd_kernel,
        grid=(b, nc),
        in_specs=data_specs + param_specs,
        out_specs=[pl.BlockSpec((1, q, SSM_INNER), lambda i, j, d=d: (i, chunk[d](j), 0)) for d in range(2)],
        out_shape=[y_shape, y_shape],
        scratch_shapes=[pltpu.VMEM((2, SSM_GROUPS, SSM_STATE, SSM_GROUP_W), F32)],
        compiler_params=pltpu.CompilerParams(
            dimension_semantics=("arbitrary", "arbitrary"), vmem_limit_bytes=VMEM_LIMIT_BYTES),
        name="ssd_scans",
    )(xbc_act, dt_arr, dt_t, xbc_act, dt_arr, dt_t, *params)


def _ssm_gate_kernel(yf_ref, yb_ref, xbc_ref, z_ref, d_ref, g_ref, out_ref):
    z = z_ref[...]
    y = (yf_ref[...] + yb_ref[...] + xbc_ref[...] * d_ref[...]) * (z * jax.nn.sigmoid(z))
    out_ref[...] = y * lax.rsqrt(jnp.mean(y * y, axis=-1, keepdims=True) + EPS) * g_ref[...]


def mamba2_mixer(arr, conv_w, conv_b, A_log, dt_bias, D_skip, norm_g,
                 col_z=0, col_xbc=CONV_CH, col_dt=SSM_INNER + CONV_CH, dt_w=2 * SSM_HEADS):
    b, l, width = arr.shape
    assert col_z % SSM_INNER == 0
    xbc_act = _ssm_conv(arr, conv_w, conv_b, col_xbc)
    dt_t = jnp.swapaxes(arr[:, :, col_dt:col_dt + 2 * SSM_HEADS], 1, 2).reshape(b * 2, SSM_HEADS, l)
    y_f, y_b = _ssd_scans(xbc_act, arr, dt_t, A_log, dt_bias, col_dt, dt_w)
    tm = 1024
    tok = lambda cb: pl.BlockSpec((tm, SSM_INNER), lambda i: (i, cb))
    row = pl.BlockSpec((1, SSM_INNER), lambda i: (0, 0))
    d_e = D_skip.astype(F32)[np.arange(SSM_INNER) // SSM_HD].reshape(1, SSM_INNER)
    out = pl.pallas_call(
        _ssm_gate_kernel,
        grid=(b * l // tm,),
        in_specs=[tok(0), tok(0), tok(0), tok(col_z // SSM_INNER), row, row],
        out_specs=tok(0),
        out_shape=jax.ShapeDtypeStruct((b * l, SSM_INNER), F32),
        compiler_params=pltpu.CompilerParams(
            dimension_semantics=("arbitrary",), vmem_limit_bytes=VMEM_LIMIT_BYTES),
        name="ssm_gate",
    )(y_f.reshape(b * l, SSM_INNER), y_b.reshape(b * l, SSM_INNER), xbc_act.reshape(b * l, CONV_CH),
      arr.reshape(b * l, width), d_e, norm_g.reshape(1, SSM_INNER))
    return out.reshape(b, l, SSM_INNER)


NA_DIM = NA_HEADS * NA_HD
NA_ROWS_PER_STEP = 8
NA_WIN = NA_ROWS * GRID_W
NA_STEP_TOK = NA_ROWS_PER_STEP * GRID_W


def _na_bias_table(rpb):
    n_dc = 2 * NA_COLS - 1
    edge_l = jnp.repeat(rpb[:, :, :1], GRID_W, axis=2)
    edge_r = jnp.repeat(rpb[:, :, -1:], GRID_W, axis=2)
    ext = jnp.concatenate([edge_l, rpb.astype(F32), edge_r], axis=2)
    by_col = jnp.stack([ext[:, :, GRID_W + NA_COLS - 1 - qc:2 * GRID_W + NA_COLS - 1 - qc]
                        for qc in range(GRID_W)], axis=2)
    qc = np.arange(GRID_W)[:, None]
    kc = np.arange(GRID_W)[None, :]
    cs = np.clip(qc - NA_COLS // 2, 0, GRID_W - NA_COLS)
    ok = (kc >= cs) & (kc < cs + NA_COLS)
    by_col = jnp.where(ok[None, None], by_col, -jnp.inf)
    tabs = []
    for delta in range(NA_ROWS):
        rows = by_col[:, NA_ROWS - 1 - delta:2 * NA_ROWS - 1 - delta]
        tabs.append(rows.transpose(0, 2, 1, 3).reshape(NA_HEADS * GRID_W, NA_WIN))
    assert n_dc == rpb.shape[2]
    return jnp.stack(tabs, axis=0)


def _na_kernel(prev_ref, cur_ref, next_ref, tab_ref, out_ref, k_scr, v_scr, *, n_rows):
    step = pl.program_id(1)
    for i, ref in enumerate((prev_ref, cur_ref, next_ref)):
        k_scr[i * NA_STEP_TOK:(i + 1) * NA_STEP_TOK, :] = ref[0, :, NA_DIM:2 * NA_DIM].astype(MXU_DTYPE)
        v_scr[i * NA_STEP_TOK:(i + 1) * NA_STEP_TOK, :] = ref[0, :, 2 * NA_DIM:3 * NA_DIM].astype(MXU_DTYPE)
    lane_head = lax.broadcasted_iota(jnp.int32, (GRID_W, NA_DIM), 1) // NA_HD
    row0 = step * NA_ROWS_PER_STEP
    for j in range(NA_ROWS_PER_STEP):
        r = row0 + j
        r0 = jnp.clip(r - NA_ROWS // 2, 0, n_rows - NA_ROWS)
        start = pl.multiple_of((r0 - row0 + NA_ROWS_PER_STEP) * GRID_W, GRID_W)
        q = cur_ref[0, j * GRID_W:(j + 1) * GRID_W, 0:NA_DIM] * np.float32(NA_HD ** -0.5)
        qs = jnp.concatenate([jnp.where(lane_head == h, q, 0.0) for h in range(NA_HEADS)], axis=0)
        kw = k_scr[pl.ds(start, NA_WIN), :]
        vw = v_scr[pl.ds(start, NA_WIN), :]
        logits = lax.dot_general(qs.astype(MXU_DTYPE), kw, (((1,), (1,)), ((), ())),
                                 preferred_element_type=F32) + tab_ref[r - r0]
        m = jnp.max(logits, axis=-1, keepdims=True)
        p = jnp.exp(logits - m)
        denom = jnp.sum(p, axis=-1, keepdims=True)
        o_all = jnp.dot(p.astype(MXU_DTYPE), vw, preferred_element_type=F32) / denom
        o = jnp.zeros((GRID_W, NA_DIM), F32)
        for h in range(NA_HEADS):
            o = o + jnp.where(lane_head == h, o_all[h * GRID_W:(h + 1) * GRID_W, :], 0.0)
        out_ref[0, j * GRID_W:(j + 1) * GRID_W, :] = o


def na_mixer(qkv, rpb, col=0):
    b, s, _ = qkv.shape
    n_rows = s // GRID_W
    assert n_rows >= NA_ROWS and n_rows % NA_ROWS_PER_STEP == 0 and col % (3 * NA_DIM) == 0
    n_steps = n_rows // NA_ROWS_PER_STEP
    blk = (1, NA_STEP_TOK, 3 * NA_DIM)
    cb = col // (3 * NA_DIM)
    return pl.pallas_call(
        functools.partial(_na_kernel, n_rows=n_rows),
        grid=(b, n_steps),
        in_specs=[
            pl.BlockSpec(blk, lambda i, j: (i, jnp.maximum(j - 1, 0), cb)),
            pl.BlockSpec(blk, lambda i, j: (i, j, cb)),
            pl.BlockSpec(blk, lambda i, j: (i, jnp.minimum(j + 1, n_steps - 1), cb)),
            pl.BlockSpec((NA_ROWS, NA_HEADS * GRID_W, NA_WIN), lambda i, j: (0, 0, 0)),
        ],
        out_specs=pl.BlockSpec((1, NA_STEP_TOK, NA_DIM), lambda i, j: (i, j, 0)),
        out_shape=jax.ShapeDtypeStruct((b, s, NA_DIM), F32),
        scratch_shapes=[
            pltpu.VMEM((3 * NA_STEP_TOK, NA_DIM), MXU_DTYPE),
            pltpu.VMEM((3 * NA_STEP_TOK, NA_DIM), MXU_DTYPE),
        ],
        compiler_params=pltpu.CompilerParams(
            dimension_semantics=("arbitrary", "arbitrary"), vmem_limit_bytes=VMEM_LIMIT_BYTES),
        name="na_attention",
    )(qkv, qkv, qkv, _na_bias_table(rpb))


PK_GATE = 0
PK_XBC = PK_GATE + N_BRANCH * D_MODEL
PK_CQ = PK_XBC + CONV_CH
PK_DIL = PK_CQ + MLA_Q_RANK
PK_NA = PK_DIL + DIL_N_GROUPS * 3 * DIL_DIM
PK_KR = PK_NA + 3 * NA_DIM
PK_Z = PK_KR + 2 * MLA_HG
PK_CKV = PK_Z + SSM_INNER
PK_DT = PK_CKV + MLA_KV_RANK
PK_DT_W = LANES
PK_WIDTH = PK_DT + PK_DT_W
INPROJ_TM = 512
INPROJ_TN = PK_WIDTH // 2
MERGE_TM = 512


def _pack_w_in(w_in_l):
    gate, a_cq, a_ckv, a_kr, b_qkv, c_z, c_xbc, c_dt, d_qkv = jnp.split(w_in_l, IN_SPLITS, axis=-1)
    zeros = lambda n: jnp.zeros((D_MODEL, n), w_in_l.dtype)
    cols = [gate, c_xbc, a_cq, b_qkv, d_qkv, mla_rotary_key_columns(a_kr), c_z, a_ckv,
            c_dt, zeros(PK_DT_W - 2 * SSM_HEADS)]
    packed = jnp.concatenate(cols, axis=-1)
    assert packed.shape[1] == PK_DT + PK_DT_W
    return jnp.concatenate([packed, zeros(PK_WIDTH - packed.shape[1])], axis=-1).astype(MXU_DTYPE)


def _inproj_kernel(x_ref, g_ref, w_ref, out_ref, h_scr):
    @pl.when(pl.program_id(1) == 0)
    def _():
        x = x_ref[...]
        h_scr[...] = (x * lax.rsqrt(jnp.mean(x * x, axis=-1, keepdims=True) + EPS) * g_ref[...]).astype(MXU_DTYPE)

    out_ref[...] = jnp.dot(h_scr[...], w_ref[...], preferred_element_type=F32)


def _inproj(x2d, norm_g, w_packed):
    n_tok = x2d.shape[0]
    tm, tn = INPROJ_TM, INPROJ_TN
    return pl.pallas_call(
        _inproj_kernel,
        grid=(n_tok // tm, PK_WIDTH // tn),
        in_specs=[pl.BlockSpec((tm, D_MODEL), lambda i, j: (i, 0)),
                  pl.BlockSpec((1, D_MODEL), lambda i, j: (0, 0)),
                  pl.BlockSpec((D_MODEL, tn), lambda i, j: (0, j))],
        out_specs=pl.BlockSpec((tm, tn), lambda i, j: (i, j)),
        out_shape=jax.ShapeDtypeStruct((n_tok, PK_WIDTH), F32),
        scratch_shapes=[pltpu.VMEM((tm, D_MODEL), MXU_DTYPE)],
        compiler_params=pltpu.CompilerParams(
            dimension_semantics=("arbitrary", "arbitrary"), vmem_limit_bytes=VMEM_LIMIT_BYTES),
        name="in_projection",
    )(x2d, norm_g.reshape(1, D_MODEL), w_packed)


def _merge_kernel(x_ref, gate_ref, bg_ref, ya_ref, yb_ref, yc_ref, yd_ref, wb_ref, wo_ref, out_ref):
    merged = jnp.zeros(x_ref.shape, F32)
    for i, y_ref in enumerate((ya_ref, yb_ref, yc_ref, yd_ref)):
        proj = jnp.dot(y_ref[...].astype(MXU_DTYPE), wb_ref[BRANCH_ROWS[i]:BRANCH_ROWS[i + 1], :],
                       preferred_element_type=F32)
        lanes = slice(i * D_MODEL, (i + 1) * D_MODEL)
        merged = merged + jax.nn.sigmoid(gate_ref[:, lanes] + bg_ref[:, lanes]) * proj
    out_ref[...] = x_ref[...] + jnp.dot(merged.astype(MXU_DTYPE), wo_ref[...], preferred_element_type=F32)


def _merge(x2d, packed, b_gate, ys, w_branch, w_out):
    n_tok = x2d.shape[0]
    tm = MERGE_TM
    tok = lambda w: pl.BlockSpec((tm, w), lambda i: (i, 0))
    full = lambda r, c: pl.BlockSpec((r, c), lambda i: (0, 0))
    n_gate = N_BRANCH * D_MODEL
    return pl.pallas_call(
        _merge_kernel,
        grid=(n_tok // tm,),
        in_specs=[tok(D_MODEL), tok(n_gate), full(1, n_gate)] + [tok(w) for w in BRANCH_WIDTHS]
                 + [full(BRANCH_ROWS[-1], D_MODEL), full(D_MODEL, D_MODEL)],
        out_specs=tok(D_MODEL),
        out_shape=jax.ShapeDtypeStruct((n_tok, D_MODEL), F32),
        compiler_params=pltpu.CompilerParams(
            dimension_semantics=("arbitrary",), vmem_limit_bytes=VMEM_LIMIT_BYTES),
        name="branch_merge",
    )(x2d, packed, b_gate.reshape(1, n_gate), *[y.reshape(n_tok, -1) for y in ys], w_branch, w_out)


def encoder(x, norm1_g, w_in_packed, b_gate, mla_q_norm, mla_w_qb, mla_kv_norm, mla_w_kvb, t5_table,
            ssm_conv_w, ssm_conv_b, ssm_A_log, ssm_dt_bias, ssm_D, ssm_norm_g, na_rpb,
            w_branch, w_out, norm2_g, peer_wq, peer_keys, peer_u, peer_vt, final_g):
    b, s, _ = x.shape
    x2d = x.reshape(b * s, D_MODEL)
    for l in range(DEPTH):
        packed = _inproj(x2d, norm1_g[l], w_in_packed[l])
        p3 = packed.reshape(b, s, PK_WIDTH)
        y_a = mla_mixer(p3, mla_q_norm[l], mla_w_qb[l], mla_kv_norm[l], mla_w_kvb[l],
                        col_q=PK_CQ, col_kv=PK_CKV, col_kr=PK_KR)
        y_b = dilated_mixer(p3, t5_table, col=PK_DIL)
        y_c = mamba2_mixer(p3, ssm_conv_w[l], ssm_conv_b[l], ssm_A_log[l], ssm_dt_bias[l], ssm_D[l],
                           ssm_norm_g[l], col_z=PK_Z, col_xbc=PK_XBC, col_dt=PK_DT, dt_w=PK_DT_W)
        y_d = na_mixer(p3, na_rpb[l], col=PK_NA)
        x2d = _merge(x2d, packed, b_gate[l], (y_a, y_b, y_c, y_d), w_branch[l], w_out[l])
        x2d = peer_block(x2d, norm2_g[l], peer_wq[l], peer_keys[l], peer_u[l], peer_vt[l],
                         final_g, final_norm=(l == DEPTH - 1))
    return x2d.reshape(b, s, D_MODEL)


def kernel(x_prompt, x_sample, norm1_g, w_in, b_gate, mla_q_norm, mla_w_qb, mla_kv_norm, mla_w_kvb, t5_table, ssm_conv_w, ssm_conv_b, ssm_A_log, ssm_dt_bias, ssm_D, ssm_norm_g, na_rpb, w_branch, w_out, norm2_g, peer_wq, peer_keys, peer_u, peer_v, final_g):
    peer_u16 = peer_u.astype(MXU_DTYPE)
    peer_vt16 = jnp.swapaxes(peer_v, 1, 2).astype(MXU_DTYPE)
    w_in_packed = jnp.stack([_pack_w_in(w_in[l]) for l in range(DEPTH)])
    shared = (norm1_g, w_in_packed, b_gate, mla_q_norm, mla_w_qb, mla_kv_norm, mla_w_kvb, t5_table,
              ssm_conv_w, ssm_conv_b, ssm_A_log, ssm_dt_bias, ssm_D, ssm_norm_g, na_rpb,
              w_branch.astype(MXU_DTYPE), w_out.astype(MXU_DTYPE), norm2_g, peer_wq, peer_keys,
              peer_u16, peer_vt16, final_g)
    y_prompt = encoder(x_prompt, *shared)
    y_sample = encoder(x_sample, *shared)
    return (y_prompt, y_sample)
```

```python
import functools
import math

import numpy as np
import jax
import jax.numpy as jnp
from jax import lax
from jax.experimental import pallas as pl
from jax.experimental.pallas import tpu as pltpu

F32 = jnp.float32
BF16 = jnp.bfloat16
MXU_DTYPE = BF16

D_MODEL = 1024
DEPTH = 2
GRID_W = 64
EPS = 1e-6
N_BRANCH = 4

MLA_HEADS = 4
MLA_Q_RANK = 256
MLA_KV_RANK = 128
MLA_NOPE = 64
MLA_ROPE = 32
MLA_V = 64
ROPE_THETA = 10000.0

DIL_GROUPS = ((128, 1), (512, 4), (2048, 16))
DIL_HEADS = 4
DIL_HD = 64
T5_BUCKETS = 32
T5_MAX_DIST = 1024

SSM_HEADS = 8
SSM_HD = 64
SSM_INNER = SSM_HEADS * SSM_HD
SSM_GROUPS = 2
SSM_STATE = 128
SSM_CONV = 7
SSM_CHUNK = 128
CONV_CH = SSM_INNER + 2 * SSM_GROUPS * SSM_STATE

NA_HEADS = 4
NA_HD = 64
NA_ROWS = 8
NA_COLS = 16

PEER_HEADS = 8
PEER_KEYS = 128
PEER_EXPERTS = PEER_KEYS * PEER_KEYS
PEER_QDIM = 256
PEER_TOPK = 16

BRANCH_WIDTHS = (MLA_HEADS * MLA_V, DIL_HEADS * DIL_HD, SSM_INNER, NA_HEADS * NA_HD)
BRANCH_ROWS = tuple(sum(BRANCH_WIDTHS[:i]) for i in range(N_BRANCH + 1))
IN_SIZES = (N_BRANCH * D_MODEL, MLA_Q_RANK, MLA_KV_RANK, MLA_ROPE,
            len(DIL_GROUPS) * 3 * DIL_HEADS * DIL_HD,
            SSM_INNER, CONV_CH, 2 * SSM_HEADS,
            3 * NA_HEADS * NA_HD)
IN_SPLITS = tuple(sum(IN_SIZES[:i + 1]) for i in range(len(IN_SIZES) - 1))

VMEM_LIMIT_BYTES = 56 * 1024 * 1024
LANES = 128
SUBLANES = 8


PEER_ROUTE_TB = 512
PEER_TB = 512
PEER_EC = 1024
PEER_GATE_DTYPE = BF16
PEER_GATE_ROWS = 16


def _gate_pack():
    return 4 // jnp.dtype(PEER_GATE_DTYPE).itemsize


def _gate_to_storage(x):
    x = x.astype(PEER_GATE_DTYPE)
    return x if _gate_pack() == 1 else pltpu.bitcast(x, jnp.uint32)


def _gate_from_storage(x):
    return x if _gate_pack() == 1 else pltpu.bitcast(x, PEER_GATE_DTYPE)


def _gate_splat(row):
    return jnp.broadcast_to(row.astype(PEER_GATE_DTYPE), (PEER_GATE_ROWS, row.shape[1]))
PEER_HALF = PEER_QDIM // 2
PEER_CAND_ROWS = 2 * SUBLANES + 7 * SUBLANES + SUBLANES


def _gelu_exact_x2(x):
    return x * (1.0 + lax.erf(x * np.float32(math.sqrt(0.5))))


def _extract_desc(vals, n_out, out_ref, row0):
    for k in range(n_out):
        m = jnp.max(vals, axis=0, keepdims=True)
        out_ref[pl.ds(row0 + k, 1), :] = m
        vals = jnp.where(vals == m, -jnp.inf, vals)


def _oddeven_sort_network(lo, hi):
    def merge(lo, hi, r):
        step = r * 2
        if step < hi - lo:
            yield from merge(lo, hi, step)
            yield from merge(lo + r, hi, step)
            yield from ((i, i + r) for i in range(lo + r, hi - r, step))
        else:
            yield (lo, lo + r)

    if hi - lo >= 1:
        mid = lo + (hi - lo) // 2
        yield from _oddeven_sort_network(lo, mid)
        yield from _oddeven_sort_network(mid + 1, hi)
        yield from merge(lo, hi, 1)


def _bitonic_merge_network(n):
    d = n // 2
    while d >= 1:
        yield from ((i, i + d) for i in range(n) if (i // d) % 2 == 0)
        d //= 2


PEER_SORT_NET = tuple(_oddeven_sort_network(0, PEER_TOPK - 1))
PEER_MERGE_NET = tuple(_bitonic_merge_network(PEER_TOPK))


def _compare_exchange(tiles, network):
    for i, j in network:
        tiles[i], tiles[j] = jnp.maximum(tiles[i], tiles[j]), jnp.minimum(tiles[i], tiles[j])


def _sorted_top(s):
    assert s.shape[0] == PEER_TOPK * SUBLANES
    tiles = [s[i * SUBLANES:(i + 1) * SUBLANES, :] for i in range(PEER_TOPK)]
    _compare_exchange(tiles, PEER_SORT_NET)
    shift = SUBLANES // 2
    while shift >= 1:
        tiles = [jnp.maximum(tiles[i], pltpu.roll(tiles[PEER_TOPK - 1 - i], shift, axis=0))
                 for i in range(PEER_TOPK)]
        _compare_exchange(tiles, PEER_MERGE_NET)
        shift //= 2
    return tiles


def _peer_route_kernel(x_ref, g_ref, wqt_ref, keys_ref,
                       xnt_ref, cnt_ref, rank_ref, e1_ref, e2_ref,
                       qt_scr, top_scr, cand_scr, tops_scr):
    x = x_ref[...]
    xn = x * lax.rsqrt(jnp.mean(x * x, axis=-1, keepdims=True) + EPS) * g_ref[...]
    xnt = xn.T.astype(MXU_DTYPE)
    xnt_ref[...] = xnt
    qt_scr[...] = jnp.dot(wqt_ref[...], xnt, preferred_element_type=F32).astype(MXU_DTYPE)

    def head(h, carry):
        q1 = qt_scr[pl.ds(pl.multiple_of(h * PEER_QDIM, PEER_QDIM), PEER_HALF), :]
        q2 = qt_scr[pl.ds(pl.multiple_of(h * PEER_QDIM + PEER_HALF, PEER_HALF), PEER_HALF), :]
        s1_all = jnp.dot(keys_ref[2 * h], q1, preferred_element_type=F32)
        s2_all = jnp.dot(keys_ref[2 * h + 1], q2, preferred_element_type=F32)
        for l0 in range(0, s1_all.shape[1], LANES):
            lanes = slice(l0, l0 + LANES)
            s1, s2 = s1_all[:, lanes], s2_all[:, lanes]
            top1, top2 = _sorted_top(s1), _sorted_top(s2)
            for k in range(PEER_TOPK):
                top_scr[k:k + 1, lanes] = top1[k][0:1, :]
                top_scr[PEER_TOPK + k:PEER_TOPK + k + 1, lanes] = top2[k][0:1, :]
            t1 = top_scr[0:PEER_TOPK, lanes]
            t2 = top_scr[PEER_TOPK:2 * PEER_TOPK, lanes]
            cand_scr[0:2 * SUBLANES, lanes] = t1[0:1, :] + t2
            for a in range(1, SUBLANES):
                cand_scr[(a + 1) * SUBLANES:(a + 2) * SUBLANES, lanes] = t1[a:a + 1, :] + t2[0:SUBLANES, :]
            cand_scr[9 * SUBLANES:10 * SUBLANES, lanes] = t1[SUBLANES:2 * SUBLANES, :] + t2[0:1, :]
            _extract_desc(cand_scr[:, lanes], PEER_TOPK, tops_scr.at[:, lanes], 0)
            top_s = tops_scr[:, lanes]
            z = jnp.sum(jnp.exp(top_s - top_s[0:1, :]), axis=0, keepdims=True)
            tau = top_s[PEER_TOPK - 1:PEER_TOPK, :]
            rank_tiles = []
            for i in range(PEER_KEYS // SUBLANES):
                s2_tile = s2[i * SUBLANES:(i + 1) * SUBLANES, :]
                rank = jnp.zeros(s2_tile.shape, F32)
                for k in range(PEER_TOPK):
                    rank = rank + jnp.where(top2[k] > s2_tile, 1.0, 0.0)
                rank_tiles.append(rank)
            cnt_top = jnp.zeros(t1.shape, F32)
            for b in range(PEER_TOPK):
                cnt_top = cnt_top + jnp.where(t1 + t2[b:b + 1, :] >= tau, 1.0, 0.0)
            cnt = jnp.zeros(s1.shape, F32)
            for a in range(PEER_TOPK):
                cnt = jnp.where(s1 == t1[a:a + 1, :], cnt_top[a:a + 1, :], cnt)
            cnt_ref[h, :, lanes] = cnt
            rank_ref[h, :, lanes] = _gate_to_storage(jnp.concatenate(rank_tiles, axis=0))
            e1_ref[h, :, lanes] = jnp.exp(s1 - t1[0:1, :]) * (0.5 / z)
            e2_ref[h, :, lanes] = _gate_to_storage(jnp.exp(s2 - t2[0:1, :]))
        return carry

    lax.fori_loop(0, PEER_HEADS, head, 0)


def _peer_route(x2d, g, wqt, keys):
    n_tok = x2d.shape[0]
    tb = PEER_ROUTE_TB
    rt_shape = jax.ShapeDtypeStruct((PEER_HEADS, PEER_KEYS, n_tok), F32)
    gate_rows = PEER_KEYS // _gate_pack()
    gate_store = PEER_GATE_DTYPE if _gate_pack() == 1 else jnp.uint32
    gate_shape = jax.ShapeDtypeStruct((PEER_HEADS, gate_rows, n_tok), gate_store)
    rt_spec = pl.BlockSpec((PEER_HEADS, PEER_KEYS, tb), lambda i: (0, 0, i))
    gate_spec = pl.BlockSpec((PEER_HEADS, gate_rows, tb), lambda i: (0, 0, i))
    return pl.pallas_call(
        _peer_route_kernel,
        grid=(n_tok // tb,),
        in_specs=[
            pl.BlockSpec((tb, D_MODEL), lambda i: (i, 0)),
            pl.BlockSpec((1, D_MODEL), lambda i: (0, 0)),
            pl.BlockSpec((PEER_HEADS * PEER_QDIM, D_MODEL), lambda i: (0, 0)),
            pl.BlockSpec((2 * PEER_HEADS, PEER_KEYS, PEER_HALF), lambda i: (0, 0, 0)),
        ],
        out_specs=[
            pl.BlockSpec((D_MODEL, tb), lambda i: (0, i)),
            rt_spec, gate_spec, rt_spec, gate_spec,
        ],
        out_shape=[
            jax.ShapeDtypeStruct((D_MODEL, n_tok), MXU_DTYPE),
            rt_shape, gate_shape, rt_shape, gate_shape,
        ],
        scratch_shapes=[
            pltpu.VMEM((PEER_HEADS * PEER_QDIM, tb), MXU_DTYPE),
            pltpu.VMEM((2 * PEER_TOPK, tb), F32),
            pltpu.VMEM((PEER_CAND_ROWS, tb), F32),
            pltpu.VMEM((PEER_TOPK, tb), F32),
        ],
        compiler_params=pltpu.CompilerParams(
            dimension_semantics=("arbitrary",), vmem_limit_bytes=VMEM_LIMIT_BYTES),
        name="peer_route",
    )(x2d, g, wqt, keys)


def _peer_expert_kernel(x_ref, xnt_ref, cnt_ref, rank_ref, e1_ref, e2_ref, u_ref, vt_ref, fg_ref,
                        out_ref, acc_scr, w_scr, *, final_norm):
    c = pl.program_id(1)
    n_chunks = pl.num_programs(1) - 1
    n_i1 = PEER_EC // PEER_KEYS
    tb = acc_scr.shape[1]
    gdt = PEER_GATE_DTYPE

    @pl.when(c == 0)
    def _():
        acc_scr[...] = jnp.zeros_like(acc_scr)
        w_scr[1] = jnp.zeros(w_scr.shape[1:], w_scr.dtype)

    slot = c % 2
    n_groups = PEER_KEYS // PEER_GATE_ROWS
    acc_scr[...] += jnp.dot(vt_ref[...], w_scr[(c + 1) % 2], preferred_element_type=F32)
    hid = jnp.dot(u_ref[...], xnt_ref[...], preferred_element_type=F32)

    def gate_block(i1l, l0):
        lanes = slice(l0, l0 + LANES)
        gates = [jnp.zeros((PEER_GATE_ROWS, LANES), gdt) for _ in range(n_groups)]
        for h in range(PEER_HEADS):
            cntb = _gate_splat(cnt_ref[h, i1l:i1l + 1, lanes])
            e1b = _gate_splat(e1_ref[h, i1l:i1l + 1, lanes])
            for k in range(n_groups):
                rows = slice(k * SUBLANES, (k + 1) * SUBLANES)
                w = _gate_from_storage(e2_ref[h, rows, lanes]) * e1b
                rank = _gate_from_storage(rank_ref[h, rows, lanes])
                gates[k] = gates[k] + jnp.where(rank < cntb, w, jnp.zeros_like(w))
        for k in range(n_groups):
            r0 = k * PEER_GATE_ROWS
            e0 = i1l * PEER_KEYS + r0
            act = _gelu_exact_x2(hid[e0:e0 + PEER_GATE_ROWS, lanes]).astype(gdt)
            w_scr[slot, e0:e0 + PEER_GATE_ROWS, lanes] = (act * gates[k]).astype(MXU_DTYPE)

    for i1l in range(n_i1):
        for l0 in range(0, tb, LANES):
            gate_block(i1l, l0)

    @pl.when(c == n_chunks)
    def _():
        y = x_ref[...] + acc_scr[...].T
        if final_norm:
            y = y * lax.rsqrt(jnp.mean(y * y, axis=-1, keepdims=True) + EPS) * fg_ref[...]
        out_ref[...] = y


def _peer_experts(x2d, xnt, cnt, rank2, e1, e2, u, vt, final_g, final_norm):
    n_tok = x2d.shape[0]
    tb, ec = PEER_TB, PEER_EC
    n_chunks = PEER_EXPERTS // ec
    assert PEER_GATE_ROWS == SUBLANES * _gate_pack()
    rt_spec = pl.BlockSpec((PEER_HEADS, PEER_KEYS // _gate_pack(), tb), lambda j, c: (0, 0, j))
    row_spec = pl.BlockSpec((PEER_HEADS, ec // PEER_KEYS, tb),
                            lambda j, c: (0, jnp.minimum(c, n_chunks - 1), j))
    return pl.pallas_call(
        functools.partial(_peer_expert_kernel, final_norm=final_norm),
        grid=(n_tok // tb, n_chunks + 1),
        in_specs=[
            pl.BlockSpec((tb, D_MODEL), lambda j, c: (j, 0)),
            pl.BlockSpec((D_MODEL, tb), lambda j, c: (0, j)),
            row_spec, rt_spec, row_spec, rt_spec,
            pl.BlockSpec((ec, D_MODEL), lambda j, c: (jnp.minimum(c, n_chunks - 1), 0)),
            pl.BlockSpec((D_MODEL, ec), lambda j, c: (0, jnp.maximum(c - 1, 0))),
            pl.BlockSpec((1, D_MODEL), lambda j, c: (0, 0)),
        ],
        out_specs=pl.BlockSpec((tb, D_MODEL), lambda j, c: (j, 0)),
        out_shape=jax.ShapeDtypeStruct((n_tok, D_MODEL), F32),
        scratch_shapes=[
            pltpu.VMEM((D_MODEL, tb), F32),
            pltpu.VMEM((2, ec, tb), MXU_DTYPE),
        ],
        compiler_params=pltpu.CompilerParams(
            dimension_semantics=("arbitrary", "arbitrary"), vmem_limit_bytes=VMEM_LIMIT_BYTES),
        name="peer_experts",
    )(x2d, xnt, cnt, rank2, e1, e2, u, vt, final_g.reshape(1, D_MODEL))


def peer_block(x2d, norm_g, w_q, keys, u, v, final_g, final_norm=False):
    wqt = w_q.T.astype(MXU_DTYPE)
    keys2 = keys.reshape(2 * PEER_HEADS, PEER_KEYS, PEER_HALF).astype(MXU_DTYPE)
    xnt, cnt, rank2, e1, e2 = _peer_route(x2d, norm_g.reshape(1, D_MODEL), wqt, keys2)
    return _peer_experts(x2d, xnt, cnt, rank2, e1, e2, u, v, final_g, final_norm)


MLA_HG = LANES
MLA_QK_W = MLA_HEADS * MLA_HG
MLA_V_W = MLA_HEADS * MLA_V
MLA_PREP_TOK = 512
MLA_TQ = 512
MLA_TK = 2048
MLA_RHALF = MLA_ROPE // 2


def _mla_rope_tables(s):
    inv = ROPE_THETA ** (-jnp.arange(MLA_RHALF, dtype=F32) / MLA_RHALF)
    ang = jnp.arange(s).astype(F32)[:, None] * inv[None, :]
    cos, sin = jnp.cos(ang), jnp.sin(ang)
    zero_pad = jnp.zeros((s, MLA_HG - MLA_NOPE - MLA_ROPE), F32)
    cos_rot = jnp.concatenate([cos, cos, zero_pad], axis=1)
    sin_rot = jnp.concatenate([-sin, sin, zero_pad], axis=1)
    scale = np.float32((MLA_NOPE + MLA_ROPE) ** -0.5)
    q_cos = scale * jnp.concatenate([jnp.ones((s, MLA_NOPE), F32), cos_rot], axis=1)
    q_sin = scale * jnp.concatenate([jnp.zeros((s, MLA_NOPE), F32), sin_rot], axis=1)
    k_cos = jnp.concatenate([jnp.zeros((s, MLA_NOPE), F32), cos_rot], axis=1)
    k_sin = jnp.concatenate([jnp.zeros((s, MLA_NOPE), F32), sin_rot], axis=1)
    return q_cos, q_sin, k_cos, k_sin


def _mla_pack_weights(w_qb, w_kvb):
    hd_q = MLA_NOPE + MLA_ROPE
    wq = w_qb.reshape(MLA_Q_RANK, MLA_HEADS, hd_q)
    rot = wq[:, :, MLA_NOPE:]
    rot_sw = jnp.concatenate([rot[:, :, MLA_RHALF:], rot[:, :, :MLA_RHALF]], axis=2)
    pad = jnp.zeros((MLA_Q_RANK, MLA_HEADS, MLA_HG - hd_q), F32)
    wq_a = jnp.concatenate([wq, pad], axis=2).reshape(MLA_Q_RANK, MLA_QK_W)
    wq_b = jnp.concatenate([jnp.zeros_like(wq[:, :, :MLA_NOPE]), rot_sw, pad], axis=2).reshape(MLA_Q_RANK, MLA_QK_W)
    wkv = w_kvb.reshape(MLA_KV_RANK, MLA_HEADS, MLA_NOPE + MLA_V)
    wk = jnp.concatenate([wkv[:, :, :MLA_NOPE], jnp.zeros((MLA_KV_RANK, MLA_HEADS, MLA_HG - MLA_NOPE), F32)],
                         axis=2).reshape(MLA_KV_RANK, MLA_QK_W)
    wv = jnp.concatenate([wkv[:, :, MLA_NOPE:], jnp.zeros((MLA_KV_RANK, MLA_HEADS, MLA_HG - MLA_V), F32)],
                         axis=2).reshape(MLA_KV_RANK, MLA_QK_W)
    return (wq_a.astype(MXU_DTYPE), wq_b.astype(MXU_DTYPE), wk.astype(MXU_DTYPE), wv.astype(MXU_DTYPE))


def _mla_prep_kernel(cq_ref, ckv_ref, kr_ref, qn_ref, kvn_ref, wqa_ref, wqb_ref, wk_ref, wv_ref,
                     qcos_ref, qsin_ref, kcos_ref, ksin_ref, q_out, kt_out, v_out):
    cq = cq_ref[0]
    cqn = (cq * lax.rsqrt(jnp.mean(cq * cq, axis=-1, keepdims=True) + EPS) * qn_ref[...]).astype(MXU_DTYPE)
    qa = jnp.dot(cqn, wqa_ref[...], preferred_element_type=F32)
    qb = jnp.dot(cqn, wqb_ref[...], preferred_element_type=F32)
    ckv = ckv_ref[0]
    ckvn = (ckv * lax.rsqrt(jnp.mean(ckv * ckv, axis=-1, keepdims=True) + EPS) * kvn_ref[...]).astype(MXU_DTYPE)
    ka = jnp.dot(ckvn, wk_ref[...], preferred_element_type=F32)
    one_lane = (lax.broadcasted_iota(jnp.int32, (1, MLA_QK_W), 1) % MLA_HG == MLA_V).astype(F32)
    v_out[0] = (jnp.dot(ckvn, wv_ref[...], preferred_element_type=F32) + one_lane).astype(MXU_DTYPE)
    kr = kr_ref[0]
    k_rot = kr[:, 0:MLA_HG] * kcos_ref[...] + kr[:, MLA_HG:2 * MLA_HG] * ksin_ref[...]
    for h in range(MLA_HEADS):
        lanes = slice(h * MLA_HG, (h + 1) * MLA_HG)
        q_out[0, :, lanes] = (qa[:, lanes] * qcos_ref[...] + qb[:, lanes] * qsin_ref[...]).astype(MXU_DTYPE)
        kt_out[0, lanes, :] = (ka[:, lanes] + k_rot).T.astype(MXU_DTYPE)


def _mla_flash_kernel(q_ref, kt_ref, v_ref, out_ref, m_scr, acc_scr):
    ki = pl.program_id(2)

    @pl.when(ki == 0)
    def _():
        m_scr[...] = jnp.full(m_scr.shape, -jnp.inf, F32)
        acc_scr[...] = jnp.zeros(acc_scr.shape, F32)

    def logits(h):
        lanes = slice(h * MLA_HG, (h + 1) * MLA_HG)
        return jnp.dot(q_ref[0, :, lanes], kt_ref[0, lanes, :], preferred_element_type=F32)

    n_rep = kt_ref.shape[2] // MLA_HG
    s_next = logits(0)
    for h in range(MLA_HEADS):
        s = s_next
        if h + 1 < MLA_HEADS:
            s_next = logits(h + 1)
        m_old = m_scr[h]
        m_new = jnp.maximum(m_old, jnp.max(s, axis=-1, keepdims=True))
        p = jnp.exp(s - jnp.tile(m_new, (1, n_rep))).astype(MXU_DTYPE)
        acc_scr[h] = jnp.exp(m_old - m_new) * acc_scr[h] + jnp.dot(
            p, v_ref[0, :, h * MLA_HG:(h + 1) * MLA_HG], preferred_element_type=F32)
        m_scr[h] = m_new

    @pl.when(ki == pl.num_programs(2) - 1)
    def _():
        low = lax.broadcasted_iota(jnp.int32, (acc_scr.shape[1], MLA_HG), 1) < MLA_V
        outs = []
        for h in range(MLA_HEADS):
            acc = acc_scr[h]
            outs.append(acc / acc[:, MLA_V:MLA_V + 1])
        for hp in range(MLA_HEADS // 2):
            odd = pltpu.roll(outs[2 * hp + 1], MLA_V, axis=1)
            out_ref[0, :, hp * MLA_HG:(hp + 1) * MLA_HG] = jnp.where(low, outs[2 * hp], odd)


def mla_rotary_key_columns(w):
    zl = jnp.zeros(w.shape[:-1] + (MLA_NOPE,), w.dtype)
    zr = jnp.zeros(w.shape[:-1] + (MLA_HG - MLA_NOPE - MLA_ROPE,), w.dtype)
    w_sw = jnp.concatenate([w[..., MLA_RHALF:], w[..., :MLA_RHALF]], axis=-1)
    return jnp.concatenate([zl, w, zr, zl, w_sw, zr], axis=-1)


def mla_mixer(arr, q_norm, w_qb, kv_norm, w_kvb, col_q=0, col_kv=MLA_Q_RANK, col_kr=MLA_Q_RANK + MLA_KV_RANK):
    b, s, _ = arr.shape
    tt = MLA_PREP_TOK
    assert col_q % MLA_Q_RANK == 0 and col_kv % MLA_KV_RANK == 0 and col_kr % (2 * MLA_HG) == 0
    wqa, wqb, wk, wv = _mla_pack_weights(w_qb, w_kvb)
    q_cos, q_sin, k_cos, k_sin = _mla_rope_tables(s)
    tok = lambda w, c=0: pl.BlockSpec((1, tt, w), lambda i, j: (i, j, c // w))
    full = lambda r, c: pl.BlockSpec((r, c), lambda i, j: (0, 0))
    tab = pl.BlockSpec((tt, MLA_HG), lambda i, j: (j, 0))
    q, kt, v = pl.pallas_call(
        _mla_prep_kernel,
        grid=(b, s // tt),
        in_specs=[tok(MLA_Q_RANK, col_q), tok(MLA_KV_RANK, col_kv), tok(2 * MLA_HG, col_kr),
                  full(1, MLA_Q_RANK), full(1, MLA_KV_RANK),
                  full(MLA_Q_RANK, MLA_QK_W), full(MLA_Q_RANK, MLA_QK_W),
                  full(MLA_KV_RANK, MLA_QK_W), full(MLA_KV_RANK, MLA_QK_W),
                  tab, tab, tab, tab],
        out_specs=[tok(MLA_QK_W), pl.BlockSpec((1, MLA_QK_W, tt), lambda i, j: (i, 0, j)), tok(MLA_QK_W)],
        out_shape=[jax.ShapeDtypeStruct((b, s, MLA_QK_W), MXU_DTYPE),
                   jax.ShapeDtypeStruct((b, MLA_QK_W, s), MXU_DTYPE),
                   jax.ShapeDtypeStruct((b, s, MLA_QK_W), MXU_DTYPE)],
        compiler_params=pltpu.CompilerParams(
            dimension_semantics=("arbitrary", "arbitrary"), vmem_limit_bytes=VMEM_LIMIT_BYTES),
        name="mla_prep",
    )(arr, arr, arr, q_norm.reshape(1, -1), kv_norm.reshape(1, -1), wqa, wqb, wk, wv,
      q_cos, q_sin, k_cos, k_sin)
    tq, tk = min(MLA_TQ, s), min(MLA_TK, s)
    assert s % tq == 0 and s % tk == 0 and s % tt == 0
    return pl.pallas_call(
        _mla_flash_kernel,
        grid=(b, s // tq, s // tk),
        in_specs=[pl.BlockSpec((1, tq, MLA_QK_W), lambda i, j, kk: (i, j, 0)),
                  pl.BlockSpec((1, MLA_QK_W, tk), lambda i, j, kk: (i, 0, kk)),
                  pl.BlockSpec((1, tk, MLA_QK_W), lambda i, j, kk: (i, kk, 0))],
        out_specs=pl.BlockSpec((1, tq, MLA_V_W), lambda i, j, kk: (i, j, 0)),
        out_shape=jax.ShapeDtypeStruct((b, s, MLA_V_W), F32),
        scratch_shapes=[pltpu.VMEM((MLA_HEADS, tq, MLA_HG), F32),
                        pltpu.VMEM((MLA_HEADS, tq, MLA_HG), F32)],
        compiler_params=pltpu.CompilerParams(
            dimension_semantics=("arbitrary", "arbitrary", "arbitrary"), vmem_limit_bytes=VMEM_LIMIT_BYTES),
        name="mla_flash",
    )(q, kt, v)


def t5_bucket(rel):
    nb = T5_BUCKETS // 2
    ret = np.where(rel > 0, nb, 0)
    n = np.abs(rel)
    max_exact = nb // 2
    large = max_exact + (np.log(np.maximum(n, 1) / max_exact) / np.log(T5_MAX_DIST / max_exact)
                         * (nb - max_exact)).astype(np.int64)
    large = np.minimum(large, nb - 1)
    return (ret + np.where(n < max_exact, n, large)).astype(np.int32)


DIL_DIM = DIL_HEADS * DIL_HD
DIL_HALF = 64
DIL_QB = 128
DIL_KW = DIL_QB + 2 * DIL_HALF
DIL_TL = 512
DIL_N_GROUPS = len(DIL_GROUPS)


def _dil_bias_table(t5_table, gi, dil):
    rel = np.arange(DIL_KW)[None, :] - DIL_HALF - np.arange(DIL_QB)[:, None]
    bias = t5_table[:, gi * DIL_HEADS:(gi + 1) * DIL_HEADS][t5_bucket(rel * dil)].astype(F32)
    bias = jnp.where((np.abs(rel) <= DIL_HALF)[:, :, None], bias, -jnp.inf)
    return bias.transpose(2, 0, 1).reshape(DIL_HEADS * DIL_QB, DIL_KW)


def _dil_kernel(prev_ref, cur_ref, next_ref, tab_ref, o_ref, lse_ref, k_scr, v_scr, *, seq_len):
    step = pl.program_id(2)
    for i, ref in enumerate((prev_ref, cur_ref, next_ref)):
        k_scr[i * DIL_TL:(i + 1) * DIL_TL, :] = ref[0, :, DIL_DIM:2 * DIL_DIM].astype(MXU_DTYPE)
        v_scr[i * DIL_TL:(i + 1) * DIL_TL, :] = ref[0, :, 2 * DIL_DIM:3 * DIL_DIM].astype(MXU_DTYPE)
    lane_head = lax.broadcasted_iota(jnp.int32, (DIL_QB, DIL_DIM), 1) // DIL_HD
    key_off = lax.broadcasted_iota(jnp.int32, (1, DIL_KW), 1)
    for n in range(DIL_TL // DIL_QB):
        w0 = DIL_TL + n * DIL_QB - DIL_HALF
        kpos = step * DIL_TL + (n * DIL_QB - DIL_HALF) + key_off
        valid = (kpos >= 0) & (kpos < seq_len)
        q = cur_ref[0, n * DIL_QB:(n + 1) * DIL_QB, 0:DIL_DIM] * np.float32(DIL_HD ** -0.5)
        qs = jnp.concatenate([jnp.where(lane_head == h, q, 0.0) for h in range(DIL_HEADS)], axis=0)
        logits = lax.dot_general(qs.astype(MXU_DTYPE), k_scr[w0:w0 + DIL_KW, :], (((1,), (1,)), ((), ())),
                                 preferred_element_type=F32) + tab_ref[...]
        logits = jnp.where(valid, logits, -jnp.inf)
        m = jnp.max(logits, axis=-1, keepdims=True)
        p = jnp.exp(logits - m)
        denom = jnp.sum(p, axis=-1, keepdims=True)
        o_all = jnp.dot(p.astype(MXU_DTYPE), v_scr[w0:w0 + DIL_KW, :], preferred_element_type=F32) / denom
        lse_all = m + jnp.log(denom)
        o = jnp.zeros((DIL_QB, DIL_DIM), F32)
        lse = jnp.zeros((DIL_QB, DIL_DIM), F32)
        for h in range(DIL_HEADS):
            rows = slice(h * DIL_QB, (h + 1) * DIL_QB)
            o = o + jnp.where(lane_head == h, o_all[rows, :], 0.0)
            lse = lse + jnp.where(lane_head == h, lse_all[rows, :], 0.0)
        o_ref[0, n * DIL_QB:(n + 1) * DIL_QB, :] = o.astype(o_ref.dtype)
        lse_ref[0, n * DIL_QB:(n + 1) * DIL_QB, :] = lse


def _dil_group(qkv, t5_table, gi, dil, col):
    b, s, width = qkv.shape
    seq_len = s // dil
    assert seq_len % DIL_TL == 0 and col % (3 * DIL_DIM) == 0
    n_steps = seq_len // DIL_TL
    gi_col = col // (3 * DIL_DIM) + gi
    if dil > 1:
        qkv = qkv[:, :, gi_col * 3 * DIL_DIM:(gi_col + 1) * 3 * DIL_DIM]
        width, gi_col = 3 * DIL_DIM, 0
    n_col = width // (3 * DIL_DIM)
    view = qkv.reshape(b, seq_len, dil * width)
    blk = (1, DIL_TL, 3 * DIL_DIM)
    out_shape = jax.ShapeDtypeStruct((b, seq_len, dil * DIL_DIM), F32)
    out_spec = pl.BlockSpec((1, DIL_TL, DIL_DIM), lambda i, r, j: (i, j, r))
    o, lse = pl.pallas_call(
        functools.partial(_dil_kernel, seq_len=seq_len),
        grid=(b, dil, n_steps),
        in_specs=[
            pl.BlockSpec(blk, lambda i, r, j: (i, jnp.maximum(j - 1, 0), r * n_col + gi_col)),
            pl.BlockSpec(blk, lambda i, r, j: (i, j, r * n_col + gi_col)),
            pl.BlockSpec(blk, lambda i, r, j: (i, jnp.minimum(j + 1, n_steps - 1), r * n_col + gi_col)),
            pl.BlockSpec((DIL_HEADS * DIL_QB, DIL_KW), lambda i, r, j: (0, 0)),
        ],
        out_specs=[out_spec, out_spec],
        out_shape=[jax.ShapeDtypeStruct(out_shape.shape, MXU_DTYPE), out_shape],
        scratch_shapes=[pltpu.VMEM((3 * DIL_TL, DIL_DIM), MXU_DTYPE),
                        pltpu.VMEM((3 * DIL_TL, DIL_DIM), MXU_DTYPE)],
        compiler_params=pltpu.CompilerParams(
            dimension_semantics=("arbitrary", "arbitrary", "arbitrary"), vmem_limit_bytes=VMEM_LIMIT_BYTES),
        name=f"dilated_attention_g{gi}",
    )(view, view, view, _dil_bias_table(t5_table, gi, dil))
    return o.reshape(b, s, DIL_DIM), lse.reshape(b, s, DIL_DIM)


def _dil_combine_kernel(*refs):
    o_refs, lse_refs, out_ref = refs[:DIL_N_GROUPS], refs[DIL_N_GROUPS:2 * DIL_N_GROUPS], refs[-1]
    lses = [r[...] for r in lse_refs]
    m = functools.reduce(jnp.maximum, lses)
    ws = [jnp.exp(l - m) for l in lses]
    total = functools.reduce(jnp.add, ws)
    acc = functools.reduce(jnp.add, [w * r[...].astype(F32) for w, r in zip(ws, o_refs)])
    out_ref[...] = acc / total


def dilated_mixer(qkv, t5_table, col=0):
    b, s, _ = qkv.shape
    outs, lses = [], []
    for gi, (win, dil) in enumerate(DIL_GROUPS):
        assert win // (2 * dil) == DIL_HALF
        o, lse = _dil_group(qkv, t5_table, gi, dil, col)
        outs.append(o.reshape(b * s, DIL_DIM))
        lses.append(lse.reshape(b * s, DIL_DIM))
    tm = 1024
    spec = pl.BlockSpec((tm, DIL_DIM), lambda i: (i, 0))
    out = pl.pallas_call(
        _dil_combine_kernel,
        grid=(b * s // tm,),
        in_specs=[spec] * (2 * DIL_N_GROUPS),
        out_specs=spec,
        out_shape=jax.ShapeDtypeStruct((b * s, DIL_DIM), F32),
        compiler_params=pltpu.CompilerParams(
            dimension_semantics=("arbitrary",), vmem_limit_bytes=VMEM_LIMIT_BYTES),
        name="dilated_combine",
    )(*outs, *lses)
    return out.reshape(b, s, DIL_DIM)


SSM_CONV_TOK = 512
SSM_HALO = SUBLANES
SSM_HEADS_PER_GROUP = SSM_HEADS // SSM_GROUPS
SSM_GROUP_W = SSM_HEADS_PER_GROUP * SSM_HD
SSM_BC_W = SSM_GROUPS * SSM_STATE


def _softplus(x):
    return jnp.maximum(x, 0.0) + jnp.log1p(jnp.exp(-jnp.abs(x)))


def _ssm_conv_kernel(prev_ref, cur_ref, next_ref, w_ref, b_ref, out_ref, cat_scr):
    j = pl.program_id(1)
    tl = cur_ref.shape[1]
    cat_scr[0:SSM_HALO, :] = jnp.where(j > 0, prev_ref[0], 0.0)
    cat_scr[SSM_HALO:SSM_HALO + tl, :] = cur_ref[0]
    cat_scr[SSM_HALO + tl:, :] = jnp.where(j < pl.num_programs(1) - 1, next_ref[0], 0.0)
    acc = jnp.zeros((tl, CONV_CH), F32) + b_ref[...]
    for k in range(SSM_CONV):
        off = SSM_HALO + k - SSM_CONV // 2
        acc = acc + cat_scr[off:off + tl, :] * w_ref[k:k + 1, :]
    out_ref[0] = acc * jax.nn.sigmoid(acc)


def _ssm_conv(xbc, conv_w, conv_b, col):
    b, l, _ = xbc.shape
    tl = SSM_CONV_TOK
    n_steps = l // tl
    per = tl // SSM_HALO
    assert col % CONV_CH == 0
    cb = col // CONV_CH
    return pl.pallas_call(
        _ssm_conv_kernel,
        grid=(b, n_steps),
        in_specs=[
            pl.BlockSpec((1, SSM_HALO, CONV_CH), lambda i, j: (i, jnp.maximum(j * per - 1, 0), cb)),
            pl.BlockSpec((1, tl, CONV_CH), lambda i, j: (i, j, cb)),
            pl.BlockSpec((1, SSM_HALO, CONV_CH), lambda i, j: (i, jnp.minimum((j + 1) * per, n_steps * per - 1), cb)),
            pl.BlockSpec((SSM_CONV, CONV_CH), lambda i, j: (0, 0)),
            pl.BlockSpec((1, CONV_CH), lambda i, j: (0, 0)),
        ],
        out_specs=pl.BlockSpec((1, tl, CONV_CH), lambda i, j: (i, j, 0)),
        out_shape=jax.ShapeDtypeStruct((b, l, CONV_CH), F32),
        scratch_shapes=[pltpu.VMEM((tl + 2 * SSM_HALO, CONV_CH), F32)],
        compiler_params=pltpu.CompilerParams(
            dimension_semantics=("arbitrary", "arbitrary"), vmem_limit_bytes=VMEM_LIMIT_BYTES),
        name="ssm_conv",
    )(xbc, xbc, xbc, conv_w, conv_b.reshape(1, CONV_CH))


def _ssd_kernel(xf_ref, dtf_ref, dttf_ref, xb_ref, dtb_ref, dttb_ref,
                expf_ref, bef_ref, aef_ref, btf_ref, atf_ref, expb_ref, beb_ref, aeb_ref, btb_ref, atb_ref,
                yf_ref, yb_ref, state_scr):
    @pl.when(pl.program_id(1) == 0)
    def _():
        state_scr[...] = jnp.zeros(state_scr.shape, F32)

    fwd = _ssd_chunk(xf_ref, dtf_ref, dttf_ref, expf_ref, bef_ref, aef_ref, btf_ref, atf_ref,
                     yf_ref, state_scr.at[0], reverse=False)
    bwd = _ssd_chunk(xb_ref, dtb_ref, dttb_ref, expb_ref, beb_ref, aeb_ref, btb_ref, atb_ref,
                     yb_ref, state_scr.at[1], reverse=True)
    for _ in zip(fwd, bwd):
        pass
    for _ in fwd:
        pass
    for _ in bwd:
        pass


def _ssd_chunk(xbc_ref, dt_ref, dtt_ref, expand_ref, bias_e_ref, a_e_ref, bias_t_ref, a_t_ref,
               y_ref, state_scr, *, reverse):
    q = SSM_CHUNK
    hi = lax.Precision.HIGHEST
    xbc = xbc_ref[0]
    xs = xbc[:, 0:SSM_INNER]
    dt_e = _softplus(jnp.dot(dt_ref[0], expand_ref[...], precision=hi, preferred_element_type=F32)
                     + bias_e_ref[...])
    a_e = dt_e * a_e_ref[...]
    ri = lax.broadcasted_iota(jnp.int32, (q, q), 0)
    ci = lax.broadcasted_iota(jnp.int32, (q, q), 1)
    seen = (ci >= ri) if reverse else (ci <= ri)
    cs_e = jnp.dot(seen.astype(F32), a_e, precision=hi, preferred_element_type=F32)
    dt_t = _softplus(dtt_ref[0] + bias_t_ref[...])
    cs_t = jnp.dot(dt_t * a_t_ref[...], seen.T.astype(F32), precision=hi, preferred_element_type=F32)
    yield
    dtx = xs * dt_e
    last = 0 if reverse else q - 1
    total = cs_e[last:last + 1, :]
    dtx_decayed = jnp.exp(total - cs_e) * dtx
    grow = jnp.exp(cs_e)
    chunk_decay = jnp.exp(total)
    lane_head = lax.broadcasted_iota(jnp.int32, (q, SSM_GROUP_W), 1) // SSM_HD
    for g in range(SSM_GROUPS):
        xl = slice(g * SSM_GROUP_W, (g + 1) * SSM_GROUP_W)
        bg = xbc[:, SSM_INNER + g * SSM_STATE:SSM_INNER + (g + 1) * SSM_STATE]
        cg = xbc[:, SSM_INNER + SSM_BC_W + g * SSM_STATE:SSM_INNER + SSM_BC_W + (g + 1) * SSM_STATE]
        cb = lax.dot_general(cg.astype(MXU_DTYPE), bg.astype(MXU_DTYPE), (((1,), (1,)), ((), ())),
                             preferred_element_type=F32)
        yield
        ms = []
        for r in range(SSM_HEADS_PER_GROUP):
            h = g * SSM_HEADS_PER_GROUP + r
            col = cs_e[:, h * SSM_HD:h * SSM_HD + 1]
            row = cs_t[h:h + 1, :]
            ms.append(cb * jnp.where(seen, jnp.exp(col - row), 0.0))
        y_all = jnp.dot(jnp.concatenate(ms, axis=0).astype(MXU_DTYPE), dtx[:, xl].astype(MXU_DTYPE),
                        preferred_element_type=F32)
        yield
        y_diag = jnp.zeros((q, SSM_GROUP_W), F32)
        for r in range(SSM_HEADS_PER_GROUP):
            y_diag = y_diag + jnp.where(lane_head == r, y_all[r * q:(r + 1) * q, :], 0.0)
        s_in = state_scr[g]
        y_off = jnp.dot(cg.astype(MXU_DTYPE), s_in.astype(MXU_DTYPE), preferred_element_type=F32) * grow[:, xl]
        y_ref[0, :, xl] = y_diag + y_off
        yield
        new = jnp.dot(bg.T.astype(MXU_DTYPE), dtx_decayed[:, xl].astype(MXU_DTYPE), preferred_element_type=F32)
        state_scr[g] = s_in * chunk_decay[:, xl] + new


def _ssd_scans(xbc_act, dt_arr, dt_t, a_log, dt_bias, col_dt, dt_w):
    b, l, _ = xbc_act.shape
    assert col_dt % dt_w == 0 and dt_w >= 2 * SSM_HEADS
    q = SSM_CHUNK
    nc = l // q
    head_of_lane = np.arange(SSM_INNER) // SSM_HD
    full = lambda r, c: pl.BlockSpec((r, c), lambda i, j: (0, 0))
    chunk = (lambda j: j, lambda j: nc - 1 - j)
    data_specs, param_specs, params = [], [], []
    for d in range(2):
        a = -jnp.exp(a_log[d].astype(F32))
        bias = dt_bias[d].astype(F32)
        expand = (np.arange(dt_w)[:, None] == d * SSM_HEADS + head_of_lane[None, :]).astype(np.float32)
        data_specs += [
            pl.BlockSpec((1, q, CONV_CH), lambda i, j, d=d: (i, chunk[d](j), 0)),
            pl.BlockSpec((1, q, dt_w), lambda i, j, d=d: (i, chunk[d](j), col_dt // dt_w)),
            pl.BlockSpec((1, SSM_HEADS, q), lambda i, j, d=d: (i * 2 + d, 0, chunk[d](j))),
        ]
        param_specs += [full(dt_w, SSM_INNER), full(1, SSM_INNER), full(1, SSM_INNER),
                        full(SSM_HEADS, 1), full(SSM_HEADS, 1)]
        params += [jnp.asarray(expand), bias[head_of_lane].reshape(1, SSM_INNER),
                   a[head_of_lane].reshape(1, SSM_INNER), bias.reshape(SSM_HEADS, 1), a.reshape(SSM_HEADS, 1)]
    y_shape = jax.ShapeDtypeStruct((b, l, SSM_INNER), F32)
    return pl.pallas_call(
        _ssd_kernel,
        grid=(b, nc),
        in_specs=data_specs + param_specs,
        out_specs=[pl.BlockSpec((1, q, SSM_INNER), lambda i, j, d=d: (i, chunk[d](j), 0)) for d in range(2)],
        out_shape=[y_shape, y_shape],
        scratch_shapes=[pltpu.VMEM((2, SSM_GROUPS, SSM_STATE, SSM_GROUP_W), F32)],
        compiler_params=pltpu.CompilerParams(
            dimension_semantics=("arbitrary", "arbitrary"), vmem_limit_bytes=VMEM_LIMIT_BYTES),
        name="ssd_scans",
    )(xbc_act, dt_arr, dt_t, xbc_act, dt_arr, dt_t, *params)


def _ssm_gate_kernel(yf_ref, yb_ref, xbc_ref, z_ref, d_ref, g_ref, out_ref):
    z = z_ref[...]
    y = (yf_ref[...] + yb_ref[...] + xbc_ref[...] * d_ref[...]) * (z * jax.nn.sigmoid(z))
    out_ref[...] = y * lax.rsqrt(jnp.mean(y * y, axis=-1, keepdims=True) + EPS) * g_ref[...]


def mamba2_mixer(arr, conv_w, conv_b, A_log, dt_bias, D_skip, norm_g,
                 col_z=0, col_xbc=CONV_CH, col_dt=SSM_INNER + CONV_CH, dt_w=2 * SSM_HEADS):
    b, l, width = arr.shape
    assert col_z % SSM_INNER == 0
    xbc_act = _ssm_conv(arr, conv_w, conv_b, col_xbc)
    dt_t = jnp.swapaxes(arr[:, :, col_dt:col_dt + 2 * SSM_HEADS], 1, 2).reshape(b * 2, SSM_HEADS, l)
    y_f, y_b = _ssd_scans(xbc_act, arr, dt_t, A_log, dt_bias, col_dt, dt_w)
    tm = 1024
    tok = lambda cb: pl.BlockSpec((tm, SSM_INNER), lambda i: (i, cb))
    row = pl.BlockSpec((1, SSM_INNER), lambda i: (0, 0))
    d_e = D_skip.astype(F32)[np.arange(SSM_INNER) // SSM_HD].reshape(1, SSM_INNER)
    out = pl.pallas_call(
        _ssm_gate_kernel,
        grid=(b * l // tm,),
        in_specs=[tok(0), tok(0), tok(0), tok(col_z // SSM_INNER), row, row],
        out_specs=tok(0),
        out_shape=jax.ShapeDtypeStruct((b * l, SSM_INNER), F32),
        compiler_params=pltpu.CompilerParams(
            dimension_semantics=("arbitrary",), vmem_limit_bytes=VMEM_LIMIT_BYTES),
        name="ssm_gate",
    )(y_f.reshape(b * l, SSM_INNER), y_b.reshape(b * l, SSM_INNER), xbc_act.reshape(b * l, CONV_CH),
      arr.reshape(b * l, width), d_e, norm_g.reshape(1, SSM_INNER))
    return out.reshape(b, l, SSM_INNER)


NA_DIM = NA_HEADS * NA_HD
NA_ROWS_PER_STEP = 8
NA_WIN = NA_ROWS * GRID_W
NA_STEP_TOK = NA_ROWS_PER_STEP * GRID_W


def _na_bias_table(rpb):
    n_dc = 2 * NA_COLS - 1
    edge_l = jnp.repeat(rpb[:, :, :1], GRID_W, axis=2)
    edge_r = jnp.repeat(rpb[:, :, -1:], GRID_W, axis=2)
    ext = jnp.concatenate([edge_l, rpb.astype(F32), edge_r], axis=2)
    by_col = jnp.stack([ext[:, :, GRID_W + NA_COLS - 1 - qc:2 * GRID_W + NA_COLS - 1 - qc]
                        for qc in range(GRID_W)], axis=2)
    qc = np.arange(GRID_W)[:, None]
    kc = np.arange(GRID_W)[None, :]
    cs = np.clip(qc - NA_COLS // 2, 0, GRID_W - NA_COLS)
    ok = (kc >= cs) & (kc < cs + NA_COLS)
    by_col = jnp.where(ok[None, None], by_col, -jnp.inf)
    tabs = []
    for delta in range(NA_ROWS):
        rows = by_col[:, NA_ROWS - 1 - delta:2 * NA_ROWS - 1 - delta]
        tabs.append(rows.transpose(0, 2, 1, 3).reshape(NA_HEADS * GRID_W, NA_WIN))
    assert n_dc == rpb.shape[2]
    return jnp.stack(tabs, axis=0)


def _na_kernel(prev_ref, cur_ref, next_ref, tab_ref, out_ref, k_scr, v_scr, *, n_rows):
    step = pl.program_id(1)
    for i, ref in enumerate((prev_ref, cur_ref, next_ref)):
        k_scr[i * NA_STEP_TOK:(i + 1) * NA_STEP_TOK, :] = ref[0, :, NA_DIM:2 * NA_DIM].astype(MXU_DTYPE)
        v_scr[i * NA_STEP_TOK:(i + 1) * NA_STEP_TOK, :] = ref[0, :, 2 * NA_DIM:3 * NA_DIM].astype(MXU_DTYPE)
    lane_head = lax.broadcasted_iota(jnp.int32, (GRID_W, NA_DIM), 1) // NA_HD
    row0 = step * NA_ROWS_PER_STEP
    for j in range(NA_ROWS_PER_STEP):
        r = row0 + j
        r0 = jnp.clip(r - NA_ROWS // 2, 0, n_rows - NA_ROWS)
        start = pl.multiple_of((r0 - row0 + NA_ROWS_PER_STEP) * GRID_W, GRID_W)
        q = cur_ref[0, j * GRID_W:(j + 1) * GRID_W, 0:NA_DIM] * np.float32(NA_HD ** -0.5)
        qs = jnp.concatenate([jnp.where(lane_head == h, q, 0.0) for h in range(NA_HEADS)], axis=0)
        kw = k_scr[pl.ds(start, NA_WIN), :]
        vw = v_scr[pl.ds(start, NA_WIN), :]
        logits = lax.dot_general(qs.astype(MXU_DTYPE), kw, (((1,), (1,)), ((), ())),
                                 preferred_element_type=F32) + tab_ref[r - r0]
        m = jnp.max(logits, axis=-1, keepdims=True)
        p = jnp.exp(logits - m)
        denom = jnp.sum(p, axis=-1, keepdims=True)
        o_all = jnp.dot(p.astype(MXU_DTYPE), vw, preferred_element_type=F32) / denom
        o = jnp.zeros((GRID_W, NA_DIM), F32)
        for h in range(NA_HEADS):
            o = o + jnp.where(lane_head == h, o_all[h * GRID_W:(h + 1) * GRID_W, :], 0.0)
        out_ref[0, j * GRID_W:(j + 1) * GRID_W, :] = o


def na_mixer(qkv, rpb, col=0):
    b, s, _ = qkv.shape
    n_rows = s // GRID_W
    assert n_rows >= NA_ROWS and n_rows % NA_ROWS_PER_STEP == 0 and col % (3 * NA_DIM) == 0
    n_steps = n_rows // NA_ROWS_PER_STEP
    blk = (1, NA_STEP_TOK, 3 * NA_DIM)
    cb = col // (3 * NA_DIM)
    return pl.pallas_call(
        functools.partial(_na_kernel, n_rows=n_rows),
        grid=(b, n_steps),
        in_specs=[
            pl.BlockSpec(blk, lambda i, j: (i, jnp.maximum(j - 1, 0), cb)),
            pl.BlockSpec(blk, lambda i, j: (i, j, cb)),
            pl.BlockSpec(blk, lambda i, j: (i, jnp.minimum(j + 1, n_steps - 1), cb)),
            pl.BlockSpec((NA_ROWS, NA_HEADS * GRID_W, NA_WIN), lambda i, j: (0, 0, 0)),
        ],
        out_specs=pl.BlockSpec((1, NA_STEP_TOK, NA_DIM), lambda i, j: (i, j, 0)),
        out_shape=jax.ShapeDtypeStruct((b, s, NA_DIM), F32),
        scratch_shapes=[
            pltpu.VMEM((3 * NA_STEP_TOK, NA_DIM), MXU_DTYPE),
            pltpu.VMEM((3 * NA_STEP_TOK, NA_DIM), MXU_DTYPE),
        ],
        compiler_params=pltpu.CompilerParams(
            dimension_semantics=("arbitrary", "arbitrary"), vmem_limit_bytes=VMEM_LIMIT_BYTES),
        name="na_attention",
    )(qkv, qkv, qkv, _na_bias_table(rpb))


PK_GATE = 0
PK_XBC = PK_GATE + N_BRANCH * D_MODEL
PK_CQ = PK_XBC + CONV_CH
PK_DIL = PK_CQ + MLA_Q_RANK
PK_NA = PK_DIL + DIL_N_GROUPS * 3 * DIL_DIM
PK_KR = PK_NA + 3 * NA_DIM
PK_Z = PK_KR + 2 * MLA_HG
PK_CKV = PK_Z + SSM_INNER
PK_DT = PK_CKV + MLA_KV_RANK
PK_DT_W = LANES
PK_WIDTH = PK_DT + PK_DT_W
INPROJ_TM = 512
INPROJ_TN = PK_WIDTH // 2
MERGE_TM = 512


def _pack_w_in(w_in_l):
    gate, a_cq, a_ckv, a_kr, b_qkv, c_z, c_xbc, c_dt, d_qkv = jnp.split(w_in_l, IN_SPLITS, axis=-1)
    zeros = lambda n: jnp.zeros((D_MODEL, n), w_in_l.dtype)
    cols = [gate, c_xbc, a_cq, b_qkv, d_qkv, mla_rotary_key_columns(a_kr), c_z, a_ckv,
            c_dt, zeros(PK_DT_W - 2 * SSM_HEADS)]
    packed = jnp.concatenate(cols, axis=-1)
    assert packed.shape[1] == PK_DT + PK_DT_W
    return jnp.concatenate([packed, zeros(PK_WIDTH - packed.shape[1])], axis=-1).astype(MXU_DTYPE)


def _inproj_kernel(x_ref, g_ref, w_ref, out_ref, h_scr):
    @pl.when(pl.program_id(1) == 0)
    def _():
        x = x_ref[...]
        h_scr[...] = (x * lax.rsqrt(jnp.mean(x * x, axis=-1, keepdims=True) + EPS) * g_ref[...]).astype(MXU_DTYPE)

    out_ref[...] = jnp.dot(h_scr[...], w_ref[...], preferred_element_type=F32)


def _inproj(x2d, norm_g, w_packed):
    n_tok = x2d.shape[0]
    tm, tn = INPROJ_TM, INPROJ_TN
    return pl.pallas_call(
        _inproj_kernel,
        grid=(n_tok // tm, PK_WIDTH // tn),
        in_specs=[pl.BlockSpec((tm, D_MODEL), lambda i, j: (i, 0)),
                  pl.BlockSpec((1, D_MODEL), lambda i, j: (0, 0)),
                  pl.BlockSpec((D_MODEL, tn), lambda i, j: (0, j))],
        out_specs=pl.BlockSpec((tm, tn), lambda i, j: (i, j)),
        out_shape=jax.ShapeDtypeStruct((n_tok, PK_WIDTH), F32),
        scratch_shapes=[pltpu.VMEM((tm, D_MODEL), MXU_DTYPE)],
        compiler_params=pltpu.CompilerParams(
            dimension_semantics=("arbitrary", "arbitrary"), vmem_limit_bytes=VMEM_LIMIT_BYTES),
        name="in_projection",
    )(x2d, norm_g.reshape(1, D_MODEL), w_packed)


def _merge_kernel(x_ref, gate_ref, bg_ref, ya_ref, yb_ref, yc_ref, yd_ref, wb_ref, wo_ref, out_ref):
    merged = jnp.zeros(x_ref.shape, F32)
    for i, y_ref in enumerate((ya_ref, yb_ref, yc_ref, yd_ref)):
        proj = jnp.dot(y_ref[...].astype(MXU_DTYPE), wb_ref[BRANCH_ROWS[i]:BRANCH_ROWS[i + 1], :],
                       preferred_element_type=F32)
        lanes = slice(i * D_MODEL, (i + 1) * D_MODEL)
        merged = merged + jax.nn.sigmoid(gate_ref[:, lanes] + bg_ref[:, lanes]) * proj
    out_ref[...] = x_ref[...] + jnp.dot(merged.astype(MXU_DTYPE), wo_ref[...], preferred_element_type=F32)


def _merge(x2d, packed, b_gate, ys, w_branch, w_out):
    n_tok = x2d.shape[0]
    tm = MERGE_TM
    tok = lambda w: pl.BlockSpec((tm, w), lambda i: (i, 0))
    full = lambda r, c: pl.BlockSpec((r, c), lambda i: (0, 0))
    n_gate = N_BRANCH * D_MODEL
    return pl.pallas_call(
        _merge_kernel,
        grid=(n_tok // tm,),
        in_specs=[tok(D_MODEL), tok(n_gate), full(1, n_gate)] + [tok(w) for w in BRANCH_WIDTHS]
                 + [full(BRANCH_ROWS[-1], D_MODEL), full(D_MODEL, D_MODEL)],
        out_specs=tok(D_MODEL),
        out_shape=jax.ShapeDtypeStruct((n_tok, D_MODEL), F32),
        compiler_params=pltpu.CompilerParams(
            dimension_semantics=("arbitrary",), vmem_limit_bytes=VMEM_LIMIT_BYTES),
        name="branch_merge",
    )(x2d, packed, b_gate.reshape(1, n_gate), *[y.reshape(n_tok, -1) for y in ys], w_branch, w_out)


def encoder(x, norm1_g, w_in_packed, b_gate, mla_q_norm, mla_w_qb, mla_kv_norm, mla_w_kvb, t5_table,
            ssm_conv_w, ssm_conv_b, ssm_A_log, ssm_dt_bias, ssm_D, ssm_norm_g, na_rpb,
            w_branch, w_out, norm2_g, peer_wq, peer_keys, peer_u, peer_vt, final_g):
    b, s, _ = x.shape
    x2d = x.reshape(b * s, D_MODEL)
    for l in range(DEPTH):
        packed = _inproj(x2d, norm1_g[l], w_in_packed[l])
        p3 = packed.reshape(b, s, PK_WIDTH)
        y_a = mla_mixer(p3, mla_q_norm[l], mla_w_qb[l], mla_kv_norm[l], mla_w_kvb[l],
                        col_q=PK_CQ, col_kv=PK_CKV, col_kr=PK_KR)
        y_b = dilated_mixer(p3, t5_table, col=PK_DIL)
        y_c = mamba2_mixer(p3, ssm_conv_w[l], ssm_conv_b[l], ssm_A_log[l], ssm_dt_bias[l], ssm_D[l],
                           ssm_norm_g[l], col_z=PK_Z, col_xbc=PK_XBC, col_dt=PK_DT, dt_w=PK_DT_W)
        y_d = na_mixer(p3, na_rpb[l], col=PK_NA)
        x2d = _merge(x2d, packed, b_gate[l], (y_a, y_b, y_c, y_d), w_branch[l], w_out[l])
        x2d = peer_block(x2d, norm2_g[l], peer_wq[l], peer_keys[l], peer_u[l], peer_vt[l],
                         final_g, final_norm=(l == DEPTH - 1))
    return x2d.reshape(b, s, D_MODEL)


def kernel(x_prompt, x_sample, norm1_g, w_in, b_gate, mla_q_norm, mla_w_qb, mla_kv_norm, mla_w_kvb, t5_table, ssm_conv_w, ssm_conv_b, ssm_A_log, ssm_dt_bias, ssm_D, ssm_norm_g, na_rpb, w_branch, w_out, norm2_g, peer_wq, peer_keys, peer_u, peer_v, final_g):
    peer_u16 = peer_u.astype(MXU_DTYPE)
    peer_vt16 = jnp.swapaxes(peer_v, 1, 2).astype(MXU_DTYPE)
    w_in_packed = jnp.stack([_pack_w_in(w_in[l]) for l in range(DEPTH)])
    shared = (norm1_g, w_in_packed, b_gate, mla_q_norm, mla_w_qb, mla_kv_norm, mla_w_kvb, t5_table,
              ssm_conv_w, ssm_conv_b, ssm_A_log, ssm_dt_bias, ssm_D, ssm_norm_g, na_rpb,
              w_branch.astype(MXU_DTYPE), w_out.astype(MXU_DTYPE), norm2_g, peer_wq, peer_keys,
              peer_u16, peer_vt16, final_g)
    y_prompt = encoder(x_prompt, *shared)
    y_sample = encoder(x_sample, *shared)
    return (y_prompt, y_sample)
```

```python
import functools
import math

import numpy as np
import jax
import jax.numpy as jnp
from jax import lax
from jax.experimental import pallas as pl
from jax.experimental.pallas import tpu as pltpu

F32 = jnp.float32
BF16 = jnp.bfloat16
MXU_DTYPE = BF16

D_MODEL = 1024
DEPTH = 2
GRID_W = 64
EPS = 1e-6
N_BRANCH = 4

MLA_HEADS = 4
MLA_Q_RANK = 256
MLA_KV_RANK = 128
MLA_NOPE = 64
MLA_ROPE = 32
MLA_V = 64
ROPE_THETA = 10000.0

DIL_GROUPS = ((128, 1), (512, 4), (2048, 16))
DIL_HEADS = 4
DIL_HD = 64
T5_BUCKETS = 32
T5_MAX_DIST = 1024

SSM_HEADS = 8
SSM_HD = 64
SSM_INNER = SSM_HEADS * SSM_HD
SSM_GROUPS = 2
SSM_STATE = 128
SSM_CONV = 7
SSM_CHUNK = 128
CONV_CH = SSM_INNER + 2 * SSM_GROUPS * SSM_STATE

NA_HEADS = 4
NA_HD = 64
NA_ROWS = 8
NA_COLS = 16

PEER_HEADS = 8
PEER_KEYS = 128
PEER_EXPERTS = PEER_KEYS * PEER_KEYS
PEER_QDIM = 256
PEER_TOPK = 16

BRANCH_WIDTHS = (MLA_HEADS * MLA_V, DIL_HEADS * DIL_HD, SSM_INNER, NA_HEADS * NA_HD)
BRANCH_ROWS = tuple(sum(BRANCH_WIDTHS[:i]) for i in range(N_BRANCH + 1))
IN_SIZES = (N_BRANCH * D_MODEL, MLA_Q_RANK, MLA_KV_RANK, MLA_ROPE,
            len(DIL_GROUPS) * 3 * DIL_HEADS * DIL_HD,
            SSM_INNER, CONV_CH, 2 * SSM_HEADS,
            3 * NA_HEADS * NA_HD)
IN_SPLITS = tuple(sum(IN_SIZES[:i + 1]) for i in range(len(IN_SIZES) - 1))

VMEM_LIMIT_BYTES = 56 * 1024 * 1024
LANES = 128
SUBLANES = 8


PEER_ROUTE_TB = 512
PEER_TB = 512
PEER_EC = 1024
PEER_GATE_DTYPE = BF16
PEER_GATE_ROWS = 16


def _gate_pack():
    return 4 // jnp.dtype(PEER_GATE_DTYPE).itemsize


def _gate_to_storage(x):
    x = x.astype(PEER_GATE_DTYPE)
    return x if _gate_pack() == 1 else pltpu.bitcast(x, jnp.uint32)


def _gate_from_storage(x):
    return x if _gate_pack() == 1 else pltpu.bitcast(x, PEER_GATE_DTYPE)


def _gate_splat(row):
    return jnp.broadcast_to(row.astype(PEER_GATE_DTYPE), (PEER_GATE_ROWS, row.shape[1]))
PEER_HALF = PEER_QDIM // 2
PEER_CAND_ROWS = 2 * SUBLANES + 7 * SUBLANES + SUBLANES


def _gelu_exact_x2(x):
    return x * (1.0 + lax.erf(x * np.float32(math.sqrt(0.5))))


def _extract_desc(vals, n_out, out_ref, row0):
    for k in range(n_out):
        m = jnp.max(vals, axis=0, keepdims=True)
        out_ref[pl.ds(row0 + k, 1), :] = m
        vals = jnp.where(vals == m, -jnp.inf, vals)


def _oddeven_sort_network(lo, hi):
    def merge(lo, hi, r):
        step = r * 2
        if step < hi - lo:
            yield from merge(lo, hi, step)
            yield from merge(lo + r, hi, step)
            yield from ((i, i + r) for i in range(lo + r, hi - r, step))
        else:
            yield (lo, lo + r)

    if hi - lo >= 1:
        mid = lo + (hi - lo) // 2
        yield from _oddeven_sort_network(lo, mid)
        yield from _oddeven_sort_network(mid + 1, hi)
        yield from merge(lo, hi, 1)


def _bitonic_merge_network(n):
    d = n // 2
    while d >= 1:
        yield from ((i, i + d) for i in range(n) if (i // d) % 2 == 0)
        d //= 2


PEER_SORT_NET = tuple(_oddeven_sort_network(0, PEER_TOPK - 1))
PEER_MERGE_NET = tuple(_bitonic_merge_network(PEER_TOPK))


def _compare_exchange(tiles, network):
    for i, j in network:
        tiles[i], tiles[j] = jnp.maximum(tiles[i], tiles[j]), jnp.minimum(tiles[i], tiles[j])


def _sorted_top(s):
    assert s.shape[0] == PEER_TOPK * SUBLANES
    tiles = [s[i * SUBLANES:(i + 1) * SUBLANES, :] for i in range(PEER_TOPK)]
    _compare_exchange(tiles, PEER_SORT_NET)
    shift = SUBLANES // 2
    while shift >= 1:
        tiles = [jnp.maximum(tiles[i], pltpu.roll(tiles[PEER_TOPK - 1 - i], shift, axis=0))
                 for i in range(PEER_TOPK)]
        _compare_exchange(tiles, PEER_MERGE_NET)
        shift //= 2
    return tiles


def _peer_route_kernel(x_ref, g_ref, wqt_ref, keys_ref,
                       xnt_ref, cnt_ref, rank_ref, e1_ref, e2_ref,
                       qt_scr, top_scr, cand_scr, tops_scr):
    x = x_ref[...]
    xn = x * lax.rsqrt(jnp.mean(x * x, axis=-1, keepdims=True) + EPS) * g_ref[...]
    xnt = xn.T.astype(MXU_DTYPE)
    xnt_ref[...] = xnt
    qt_scr[...] = jnp.dot(wqt_ref[...], xnt, preferred_element_type=F32).astype(MXU_DTYPE)

    def head(h, carry):
        q1 = qt_scr[pl.ds(pl.multiple_of(h * PEER_QDIM, PEER_QDIM), PEER_HALF), :]
        q2 = qt_scr[pl.ds(pl.multiple_of(h * PEER_QDIM + PEER_HALF, PEER_HALF), PEER_HALF), :]
        s1_all = jnp.dot(keys_ref[2 * h], q1, preferred_element_type=F32)
        s2_all = jnp.dot(keys_ref[2 * h + 1], q2, preferred_element_type=F32)
        for l0 in range(0, s1_all.shape[1], LANES):
            lanes = slice(l0, l0 + LANES)
            s1, s2 = s1_all[:, lanes], s2_all[:, lanes]
            top1, top2 = _sorted_top(s1), _sorted_top(s2)
            for k in range(PEER_TOPK):
                top_scr[k:k + 1, lanes] = top1[k][0:1, :]
                top_scr[PEER_TOPK + k:PEER_TOPK + k + 1, lanes] = top2[k][0:1, :]
            t1 = top_scr[0:PEER_TOPK, lanes]
            t2 = top_scr[PEER_TOPK:2 * PEER_TOPK, lanes]
            cand_scr[0:2 * SUBLANES, lanes] = t1[0:1, :] + t2
            for a in range(1, SUBLANES):
                cand_scr[(a + 1) * SUBLANES:(a + 2) * SUBLANES, lanes] = t1[a:a + 1, :] + t2[0:SUBLANES, :]
            cand_scr[9 * SUBLANES:10 * SUBLANES, lanes] = t1[SUBLANES:2 * SUBLANES, :] + t2[0:1, :]
            _extract_desc(cand_scr[:, lanes], PEER_TOPK, tops_scr.at[:, lanes], 0)
            top_s = tops_scr[:, lanes]
            z = jnp.sum(jnp.exp(top_s - top_s[0:1, :]), axis=0, keepdims=True)
            tau = top_s[PEER_TOPK - 1:PEER_TOPK, :]
            rank_tiles = []
            for i in range(PEER_KEYS // SUBLANES):
                s2_tile = s2[i * SUBLANES:(i + 1) * SUBLANES, :]
                rank = jnp.zeros(s2_tile.shape, F32)
                for k in range(PEER_TOPK):
                    rank = rank + jnp.where(top2[k] > s2_tile, 1.0, 0.0)
                rank_tiles.append(rank)
            cnt_top = jnp.zeros(t1.shape, F32)
            for b in range(PEER_TOPK):
                cnt_top = cnt_top + jnp.where(t1 + t2[b:b + 1, :] >= tau, 1.0, 0.0)
            cnt = jnp.zeros(s1.shape, F32)
            for a in range(PEER_TOPK):
                cnt = jnp.where(s1 == t1[a:a + 1, :], cnt_top[a:a + 1, :], cnt)
            cnt_ref[h, :, lanes] = cnt
            rank_ref[h, :, lanes] = _gate_to_storage(jnp.concatenate(rank_tiles, axis=0))
            e1_ref[h, :, lanes] = jnp.exp(s1 - t1[0:1, :]) * (0.5 / z)
            e2_ref[h, :, lanes] = _gate_to_storage(jnp.exp(s2 - t2[0:1, :]))
        return carry

    lax.fori_loop(0, PEER_HEADS, head, 0)


def _peer_route(x2d, g, wqt, keys):
    n_tok = x2d.shape[0]
    tb = PEER_ROUTE_TB
    rt_shape = jax.ShapeDtypeStruct((PEER_HEADS, PEER_KEYS, n_tok), F32)
    gate_rows = PEER_KEYS // _gate_pack()
    gate_store = PEER_GATE_DTYPE if _gate_pack() == 1 else jnp.uint32
    gate_shape = jax.ShapeDtypeStruct((PEER_HEADS, gate_rows, n_tok), gate_store)
    rt_spec = pl.BlockSpec((PEER_HEADS, PEER_KEYS, tb), lambda i: (0, 0, i))
    gate_spec = pl.BlockSpec((PEER_HEADS, gate_rows, tb), lambda i: (0, 0, i))
    return pl.pallas_call(
        _peer_route_kernel,
        grid=(n_tok // tb,),
        in_specs=[
            pl.BlockSpec((tb, D_MODEL), lambda i: (i, 0)),
            pl.BlockSpec((1, D_MODEL), lambda i: (0, 0)),
            pl.BlockSpec((PEER_HEADS * PEER_QDIM, D_MODEL), lambda i: (0, 0)),
            pl.BlockSpec((2 * PEER_HEADS, PEER_KEYS, PEER_HALF), lambda i: (0, 0, 0)),
        ],
        out_specs=[
            pl.BlockSpec((D_MODEL, tb), lambda i: (0, i)),
            rt_spec, gate_spec, rt_spec, gate_spec,
        ],
        out_shape=[
            jax.ShapeDtypeStruct((D_MODEL, n_tok), MXU_DTYPE),
            rt_shape, gate_shape, rt_shape, gate_shape,
        ],
        scratch_shapes=[
            pltpu.VMEM((PEER_HEADS * PEER_QDIM, tb), MXU_DTYPE),
            pltpu.VMEM((2 * PEER_TOPK, tb), F32),
            pltpu.VMEM((PEER_CAND_ROWS, tb), F32),
            pltpu.VMEM((PEER_TOPK, tb), F32),
        ],
        compiler_params=pltpu.CompilerParams(
            dimension_semantics=("arbitrary",), vmem_limit_bytes=VMEM_LIMIT_BYTES),
        name="peer_route",
    )(x2d, g, wqt, keys)


def _peer_expert_kernel(x_ref, xnt_ref, cnt_ref, rank_ref, e1_ref, e2_ref, u_ref, vt_ref, fg_ref,
                        out_ref, acc_scr, w_scr, *, final_norm):
    c = pl.program_id(1)
    n_chunks = pl.num_programs(1) - 1
    n_i1 = PEER_EC // PEER_KEYS
    tb = acc_scr.shape[1]
    gdt = PEER_GATE_DTYPE

    @pl.when(c == 0)
    def _():
        acc_scr[...] = jnp.zeros_like(acc_scr)
        w_scr[1] = jnp.zeros(w_scr.shape[1:], w_scr.dtype)

    slot = c % 2
    n_groups = PEER_KEYS // PEER_GATE_ROWS
    acc_scr[...] += jnp.dot(vt_ref[...], w_scr[(c + 1) % 2], preferred_element_type=F32)
    hid = jnp.dot(u_ref[...], xnt_ref[...], preferred_element_type=F32)

    def gate_block(i1l, l0):
        lanes = slice(l0, l0 + LANES)
        gates = [jnp.zeros((PEER_GATE_ROWS, LANES), gdt) for _ in range(n_groups)]
        for h in range(PEER_HEADS):
            cntb = _gate_splat(cnt_ref[h, i1l:i1l + 1, lanes])
            e1b = _gate_splat(e1_ref[h, i1l:i1l + 1, lanes])
            for k in range(n_groups):
                rows = slice(k * SUBLANES, (k + 1) * SUBLANES)
                w = _gate_from_storage(e2_ref[h, rows, lanes]) * e1b
                rank = _gate_from_storage(rank_ref[h, rows, lanes])
                gates[k] = gates[k] + jnp.where(rank < cntb, w, jnp.zeros_like(w))
        for k in range(n_groups):
            r0 = k * PEER_GATE_ROWS
            e0 = i1l * PEER_KEYS + r0
            act = _gelu_exact_x2(hid[e0:e0 + PEER_GATE_ROWS, lanes]).astype(gdt)
            w_scr[slot, e0:e0 + PEER_GATE_ROWS, lanes] = (act * gates[k]).astype(MXU_DTYPE)

    for i1l in range(n_i1):
        for l0 in range(0, tb, LANES):
            gate_block(i1l, l0)

    @pl.when(c == n_chunks)
    def _():
        y = x_ref[...] + acc_scr[...].T
        if final_norm:
            y = y * lax.rsqrt(jnp.mean(y * y, axis=-1, keepdims=True) + EPS) * fg_ref[...]
        out_ref[...] = y


def _peer_experts(x2d, xnt, cnt, rank2, e1, e2, u, vt, final_g, final_norm):
    n_tok = x2d.shape[0]
    tb, ec = PEER_TB, PEER_EC
    n_chunks = PEER_EXPERTS // ec
    assert PEER_GATE_ROWS == SUBLANES * _gate_pack()
    rt_spec = pl.BlockSpec((PEER_HEADS, PEER_KEYS // _gate_pack(), tb), lambda j, c: (0, 0, j))
    row_spec = pl.BlockSpec((PEER_HEADS, ec // PEER_KEYS, tb),
                            lambda j, c: (0, jnp.minimum(c, n_chunks - 1), j))
    return pl.pallas_call(
        functools.partial(_peer_expert_kernel, final_norm=final_norm),
        grid=(n_tok // tb, n_chunks + 1),
        in_specs=[
            pl.BlockSpec((tb, D_MODEL), lambda j, c: (j, 0)),
            pl.BlockSpec((D_MODEL, tb), lambda j, c: (0, j)),
            row_spec, rt_spec, row_spec, rt_spec,
            pl.BlockSpec((ec, D_MODEL), lambda j, c: (jnp.minimum(c, n_chunks - 1), 0)),
            pl.BlockSpec((D_MODEL, ec), lambda j, c: (0, jnp.maximum(c - 1, 0))),
            pl.BlockSpec((1, D_MODEL), lambda j, c: (0, 0)),
        ],
        out_specs=pl.BlockSpec((tb, D_MODEL), lambda j, c: (j, 0)),
        out_shape=jax.ShapeDtypeStruct((n_tok, D_MODEL), F32),
        scratch_shapes=[
            pltpu.VMEM((D_MODEL, tb), F32),
            pltpu.VMEM((2, ec, tb), MXU_DTYPE),
        ],
        compiler_params=pltpu.CompilerParams(
            dimension_semantics=("arbitrary", "arbitrary"), vmem_limit_bytes=VMEM_LIMIT_BYTES),
        name="peer_experts",
    )(x2d, xnt, cnt, rank2, e1, e2, u, vt, final_g.reshape(1, D_MODEL))


def peer_block(x2d, norm_g, w_q, keys, u, v, final_g, final_norm=False):
    wqt = w_q.T.astype(MXU_DTYPE)
    keys2 = keys.reshape(2 * PEER_HEADS, PEER_KEYS, PEER_HALF).astype(MXU_DTYPE)
    xnt, cnt, rank2, e1, e2 = _peer_route(x2d, norm_g.reshape(1, D_MODEL), wqt, keys2)
    return _peer_experts(x2d, xnt, cnt, rank2, e1, e2, u, v, final_g, final_norm)


MLA_HG = LANES
MLA_QK_W = MLA_HEADS * MLA_HG
MLA_V_W = MLA_HEADS * MLA_V
MLA_PREP_TOK = 512
MLA_TQ = 512
MLA_TK = 2048
MLA_RHALF = MLA_ROPE // 2


def _mla_rope_tables(s):
    inv = ROPE_THETA ** (-jnp.arange(MLA_RHALF, dtype=F32) / MLA_RHALF)
    ang = jnp.arange(s).astype(F32)[:, None] * inv[None, :]
    cos, sin = jnp.cos(ang), jnp.sin(ang)
    zero_pad = jnp.zeros((s, MLA_HG - MLA_NOPE - MLA_ROPE), F32)
    cos_rot = jnp.concatenate([cos, cos, zero_pad], axis=1)
    sin_rot = jnp.concatenate([-sin, sin, zero_pad], axis=1)
    scale = np.float32((MLA_NOPE + MLA_ROPE) ** -0.5)
    q_cos = scale * jnp.concatenate([jnp.ones((s, MLA_NOPE), F32), cos_rot], axis=1)
    q_sin = scale * jnp.concatenate([jnp.zeros((s, MLA_NOPE), F32), sin_rot], axis=1)
    k_cos = jnp.concatenate([jnp.zeros((s, MLA_NOPE), F32), cos_rot], axis=1)
    k_sin = jnp.concatenate([jnp.zeros((s, MLA_NOPE), F32), sin_rot], axis=1)
    return q_cos, q_sin, k_cos, k_sin


def _mla_pack_weights(w_qb, w_kvb):
    hd_q = MLA_NOPE + MLA_ROPE
    wq = w_qb.reshape(MLA_Q_RANK, MLA_HEADS, hd_q)
    rot = wq[:, :, MLA_NOPE:]
    rot_sw = jnp.concatenate([rot[:, :, MLA_RHALF:], rot[:, :, :MLA_RHALF]], axis=2)
    pad = jnp.zeros((MLA_Q_RANK, MLA_HEADS, MLA_HG - hd_q), F32)
    wq_a = jnp.concatenate([wq, pad], axis=2).reshape(MLA_Q_RANK, MLA_QK_W)
    wq_b = jnp.concatenate([jnp.zeros_like(wq[:, :, :MLA_NOPE]), rot_sw, pad], axis=2).reshape(MLA_Q_RANK, MLA_QK_W)
    wkv = w_kvb.reshape(MLA_KV_RANK, MLA_HEADS, MLA_NOPE + MLA_V)
    wk = jnp.concatenate([wkv[:, :, :MLA_NOPE], jnp.zeros((MLA_KV_RANK, MLA_HEADS, MLA_HG - MLA_NOPE), F32)],
                         axis=2).reshape(MLA_KV_RANK, MLA_QK_W)
    wv = jnp.concatenate([wkv[:, :, MLA_NOPE:], jnp.zeros((MLA_KV_RANK, MLA_HEADS, MLA_HG - MLA_V), F32)],
                         axis=2).reshape(MLA_KV_RANK, MLA_QK_W)
    return (wq_a.astype(MXU_DTYPE), wq_b.astype(MXU_DTYPE), wk.astype(MXU_DTYPE), wv.astype(MXU_DTYPE))


def _mla_prep_kernel(cq_ref, ckv_ref, kr_ref, qn_ref, kvn_ref, wqa_ref, wqb_ref, wk_ref, wv_ref,
                     qcos_ref, qsin_ref, kcos_ref, ksin_ref, q_out, kt_out, v_out):
    cq = cq_ref[0]
    cqn = (cq * lax.rsqrt(jnp.mean(cq * cq, axis=-1, keepdims=True) + EPS) * qn_ref[...]).astype(MXU_DTYPE)
    qa = jnp.dot(cqn, wqa_ref[...], preferred_element_type=F32)
    qb = jnp.dot(cqn, wqb_ref[...], preferred_element_type=F32)
    ckv = ckv_ref[0]
    ckvn = (ckv * lax.rsqrt(jnp.mean(ckv * ckv, axis=-1, keepdims=True) + EPS) * kvn_ref[...]).astype(MXU_DTYPE)
    ka = jnp.dot(ckvn, wk_ref[...], preferred_element_type=F32)
    one_lane = (lax.broadcasted_iota(jnp.int32, (1, MLA_QK_W), 1) % MLA_HG == MLA_V).astype(F32)
    v_out[0] = (jnp.dot(ckvn, wv_ref[...], preferred_element_type=F32) + one_lane).astype(MXU_DTYPE)
    kr = kr_ref[0]
    k_rot = kr[:, 0:MLA_HG] * kcos_ref[...] + kr[:, MLA_HG:2 * MLA_HG] * ksin_ref[...]
    for h in range(MLA_HEADS):
        lanes = slice(h * MLA_HG, (h + 1) * MLA_HG)
        q_out[0, :, lanes] = (qa[:, lanes] * qcos_ref[...] + qb[:, lanes] * qsin_ref[...]).astype(MXU_DTYPE)
        kt_out[0, lanes, :] = (ka[:, lanes] + k_rot).T.astype(MXU_DTYPE)


def _mla_flash_kernel(q_ref, kt_ref, v_ref, out_ref, m_scr, acc_scr):
    ki = pl.program_id(2)

    @pl.when(ki == 0)
    def _():
        m_scr[...] = jnp.full(m_scr.shape, -jnp.inf, F32)
        acc_scr[...] = jnp.zeros(acc_scr.shape, F32)

    def logits(h):
        lanes = slice(h * MLA_HG, (h + 1) * MLA_HG)
        return jnp.dot(q_ref[0, :, lanes], kt_ref[0, lanes, :], preferred_element_type=F32)

    n_rep = kt_ref.shape[2] // MLA_HG
    s_next = logits(0)
    for h in range(MLA_HEADS):
        s = s_next
        if h + 1 < MLA_HEADS:
            s_next = logits(h + 1)
        m_old = m_scr[h]
        m_new = jnp.maximum(m_old, jnp.max(s, axis=-1, keepdims=True))
        p = jnp.exp(s - jnp.tile(m_new, (1, n_rep))).astype(MXU_DTYPE)
        acc_scr[h] = jnp.exp(m_old - m_new) * acc_scr[h] + jnp.dot(
            p, v_ref[0, :, h * MLA_HG:(h + 1) * MLA_HG], preferred_element_type=F32)
        m_scr[h] = m_new

    @pl.when(ki == pl.num_programs(2) - 1)
    def _():
        low = lax.broadcasted_iota(jnp.int32, (acc_scr.shape[1], MLA_HG), 1) < MLA_V
        outs = []
        for h in range(MLA_HEADS):
            acc = acc_scr[h]
            outs.append(acc / acc[:, MLA_V:MLA_V + 1])
        for hp in range(MLA_HEADS // 2):
            odd = pltpu.roll(outs[2 * hp + 1], MLA_V, axis=1)
            out_ref[0, :, hp * MLA_HG:(hp + 1) * MLA_HG] = jnp.where(low, outs[2 * hp], odd)


def mla_rotary_key_columns(w):
    zl = jnp.zeros(w.shape[:-1] + (MLA_NOPE,), w.dtype)
    zr = jnp.zeros(w.shape[:-1] + (MLA_HG - MLA_NOPE - MLA_ROPE,), w.dtype)
    w_sw = jnp.concatenate([w[..., MLA_RHALF:], w[..., :MLA_RHALF]], axis=-1)
    return jnp.concatenate([zl, w, zr, zl, w_sw, zr], axis=-1)


def mla_mixer(arr, q_norm, w_qb, kv_norm, w_kvb, col_q=0, col_kv=MLA_Q_RANK, col_kr=MLA_Q_RANK + MLA_KV_RANK):
    b, s, _ = arr.shape
    tt = MLA_PREP_TOK
    assert col_q % MLA_Q_RANK == 0 and col_kv % MLA_KV_RANK == 0 and col_kr % (2 * MLA_HG) == 0
    wqa, wqb, wk, wv = _mla_pack_weights(w_qb, w_kvb)
    q_cos, q_sin, k_cos, k_sin = _mla_rope_tables(s)
    tok = lambda w, c=0: pl.BlockSpec((1, tt, w), lambda i, j: (i, j, c // w))
    full = lambda r, c: pl.BlockSpec((r, c), lambda i, j: (0, 0))
    tab = pl.BlockSpec((tt, MLA_HG), lambda i, j: (j, 0))
    q, kt, v = pl.pallas_call(
        _mla_prep_kernel,
        grid=(b, s // tt),
        in_specs=[tok(MLA_Q_RANK, col_q), tok(MLA_KV_RANK, col_kv), tok(2 * MLA_HG, col_kr),
                  full(1, MLA_Q_RANK), full(1, MLA_KV_RANK),
                  full(MLA_Q_RANK, MLA_QK_W), full(MLA_Q_RANK, MLA_QK_W),
                  full(MLA_KV_RANK, MLA_QK_W), full(MLA_KV_RANK, MLA_QK_W),
                  tab, tab, tab, tab],
        out_specs=[tok(MLA_QK_W), pl.BlockSpec((1, MLA_QK_W, tt), lambda i, j: (i, 0, j)), tok(MLA_QK_W)],
        out_shape=[jax.ShapeDtypeStruct((b, s, MLA_QK_W), MXU_DTYPE),
                   jax.ShapeDtypeStruct((b, MLA_QK_W, s), MXU_DTYPE),
                   jax.ShapeDtypeStruct((b, s, MLA_QK_W), MXU_DTYPE)],
        compiler_params=pltpu.CompilerParams(
            dimension_semantics=("arbitrary", "arbitrary"), vmem_limit_bytes=VMEM_LIMIT_BYTES),
        name="mla_prep",
    )(arr, arr, arr, q_norm.reshape(1, -1), kv_norm.reshape(1, -1), wqa, wqb, wk, wv,
      q_cos, q_sin, k_cos, k_sin)
    tq, tk = min(MLA_TQ, s), min(MLA_TK, s)
    assert s % tq == 0 and s % tk == 0 and s % tt == 0
    return pl.pallas_call(
        _mla_flash_kernel,
        grid=(b, s // tq, s // tk),
        in_specs=[pl.BlockSpec((1, tq, MLA_QK_W), lambda i, j, kk: (i, j, 0)),
                  pl.BlockSpec((1, MLA_QK_W, tk), lambda i, j, kk: (i, 0, kk)),
                  pl.BlockSpec((1, tk, MLA_QK_W), lambda i, j, kk: (i, kk, 0))],
        out_specs=pl.BlockSpec((1, tq, MLA_V_W), lambda i, j, kk: (i, j, 0)),
        out_shape=jax.ShapeDtypeStruct((b, s, MLA_V_W), F32),
        scratch_shapes=[pltpu.VMEM((MLA_HEADS, tq, MLA_HG), F32),
                        pltpu.VMEM((MLA_HEADS, tq, MLA_HG), F32)],
        compiler_params=pltpu.CompilerParams(
            dimension_semantics=("arbitrary", "arbitrary", "arbitrary"), vmem_limit_bytes=VMEM_LIMIT_BYTES),
        name="mla_flash",
    )(q, kt, v)


def t5_bucket(rel):
    nb = T5_BUCKETS // 2
    ret = np.where(rel > 0, nb, 0)
    n = np.abs(rel)
    max_exact = nb // 2
    large = max_exact + (np.log(np.maximum(n, 1) / max_exact) / np.log(T5_MAX_DIST / max_exact)
                         * (nb - max_exact)).astype(np.int64)
    large = np.minimum(large, nb - 1)
    return (ret + np.where(n < max_exact, n, large)).astype(np.int32)


DIL_DIM = DIL_HEADS * DIL_HD
DIL_HALF = 64
DIL_QB = 128
DIL_KW = DIL_QB + 2 * DIL_HALF
DIL_TL = 512
DIL_N_GROUPS = len(DIL_GROUPS)


def _dil_bias_table(t5_table, gi, dil):
    rel = np.arange(DIL_KW)[None, :] - DIL_HALF - np.arange(DIL_QB)[:, None]
    bias = t5_table[:, gi * DIL_HEADS:(gi + 1) * DIL_HEADS][t5_bucket(rel * dil)].astype(F32)
    bias = jnp.where((np.abs(rel) <= DIL_HALF)[:, :, None], bias, -jnp.inf)
    return bias.transpose(2, 0, 1).reshape(DIL_HEADS * DIL_QB, DIL_KW)


def _dil_kernel(prev_ref, cur_ref, next_ref, tab_ref, o_ref, lse_ref, k_scr, v_scr, *, seq_len):
    step = pl.program_id(2)
    for i, ref in enumerate((prev_ref, cur_ref, next_ref)):
        k_scr[i * DIL_TL:(i + 1) * DIL_TL, :] = ref[0, :, DIL_DIM:2 * DIL_DIM].astype(MXU_DTYPE)
        v_scr[i * DIL_TL:(i + 1) * DIL_TL, :] = ref[0, :, 2 * DIL_DIM:3 * DIL_DIM].astype(MXU_DTYPE)
    lane_head = lax.broadcasted_iota(jnp.int32, (DIL_QB, DIL_DIM), 1) // DIL_HD
    key_off = lax.broadcasted_iota(jnp.int32, (1, DIL_KW), 1)
    for n in range(DIL_TL // DIL_QB):
        w0 = DIL_TL + n * DIL_QB - DIL_HALF
        kpos = step * DIL_TL + (n * DIL_QB - DIL_HALF) + key_off
        valid = (kpos >= 0) & (kpos < seq_len)
        q = cur_ref[0, n * DIL_QB:(n + 1) * DIL_QB, 0:DIL_DIM] * np.float32(DIL_HD ** -0.5)
        qs = jnp.concatenate([jnp.where(lane_head == h, q, 0.0) for h in range(DIL_HEADS)], axis=0)
        logits = lax.dot_general(qs.astype(MXU_DTYPE), k_scr[w0:w0 + DIL_KW, :], (((1,), (1,)), ((), ())),
                                 preferred_element_type=F32) + tab_ref[...]
        logits = jnp.where(valid, logits, -jnp.inf)
        m = jnp.max(logits, axis=-1, keepdims=True)
        p = jnp.exp(logits - m)
        denom = jnp.sum(p, axis=-1, keepdims=True)
        o_all = jnp.dot(p.astype(MXU_DTYPE), v_scr[w0:w0 + DIL_KW, :], preferred_element_type=F32) / denom
        lse_all = m + jnp.log(denom)
        o = jnp.zeros((DIL_QB, DIL_DIM), F32)
        lse = jnp.zeros((DIL_QB, DIL_DIM), F32)
        for h in range(DIL_HEADS):
            rows = slice(h * DIL_QB, (h + 1) * DIL_QB)
            o = o + jnp.where(lane_head == h, o_all[rows, :], 0.0)
            lse = lse + jnp.where(lane_head == h, lse_all[rows, :], 0.0)
        o_ref[0, n * DIL_QB:(n + 1) * DIL_QB, :] = o.astype(o_ref.dtype)
        lse_ref[0, n * DIL_QB:(n + 1) * DIL_QB, :] = lse


def _dil_group(qkv, t5_table, gi, dil, col):
    b, s, width = qkv.shape
    seq_len = s // dil
    assert seq_len % DIL_TL == 0 and col % (3 * DIL_DIM) == 0
    n_steps = seq_len // DIL_TL
    gi_col = col // (3 * DIL_DIM) + gi
    if dil > 1:
        qkv = qkv[:, :, gi_col * 3 * DIL_DIM:(gi_col + 1) * 3 * DIL_DIM]
        width, gi_col = 3 * DIL_DIM, 0
    n_col = width // (3 * DIL_DIM)
    view = qkv.reshape(b, seq_len, dil * width)
    blk = (1, DIL_TL, 3 * DIL_DIM)
    out_shape = jax.ShapeDtypeStruct((b, seq_len, dil * DIL_DIM), F32)
    out_spec = pl.BlockSpec((1, DIL_TL, DIL_DIM), lambda i, r, j: (i, j, r))
    o, lse = pl.pallas_call(
        functools.partial(_dil_kernel, seq_len=seq_len),
        grid=(b, dil, n_steps),
        in_specs=[
            pl.BlockSpec(blk, lambda i, r, j: (i, jnp.maximum(j - 1, 0), r * n_col + gi_col)),
            pl.BlockSpec(blk, lambda i, r, j: (i, j, r * n_col + gi_col)),
            pl.BlockSpec(blk, lambda i, r, j: (i, jnp.minimum(j + 1, n_steps - 1), r * n_col + gi_col)),
            pl.BlockSpec((DIL_HEADS * DIL_QB, DIL_KW), lambda i, r, j: (0, 0)),
        ],
        out_specs=[out_spec, out_spec],
        out_shape=[jax.ShapeDtypeStruct(out_shape.shape, MXU_DTYPE), out_shape],
        scratch_shapes=[pltpu.VMEM((3 * DIL_TL, DIL_DIM), MXU_DTYPE),
                        pltpu.VMEM((3 * DIL_TL, DIL_DIM), MXU_DTYPE)],
        compiler_params=pltpu.CompilerParams(
            dimension_semantics=("arbitrary", "arbitrary", "arbitrary"), vmem_limit_bytes=VMEM_LIMIT_BYTES),
        name=f"dilated_attention_g{gi}",
    )(view, view, view, _dil_bias_table(t5_table, gi, dil))
    return o.reshape(b, s, DIL_DIM), lse.reshape(b, s, DIL_DIM)


def _dil_combine_kernel(*refs):
    o_refs, lse_refs, out_ref = refs[:DIL_N_GROUPS], refs[DIL_N_GROUPS:2 * DIL_N_GROUPS], refs[-1]
    lses = [r[...] for r in lse_refs]
    m = functools.reduce(jnp.maximum, lses)
    ws = [jnp.exp(l - m) for l in lses]
    total = functools.reduce(jnp.add, ws)
    acc = functools.reduce(jnp.add, [w * r[...].astype(F32) for w, r in zip(ws, o_refs)])
    out_ref[...] = acc / total


def dilated_mixer(qkv, t5_table, col=0):
    b, s, _ = qkv.shape
    outs, lses = [], []
    for gi, (win, dil) in enumerate(DIL_GROUPS):
        assert win // (2 * dil) == DIL_HALF
        o, lse = _dil_group(qkv, t5_table, gi, dil, col)
        outs.append(o.reshape(b * s, DIL_DIM))
        lses.append(lse.reshape(b * s, DIL_DIM))
    tm = 1024
    spec = pl.BlockSpec((tm, DIL_DIM), lambda i: (i, 0))
    out = pl.pallas_call(
        _dil_combine_kernel,
        grid=(b * s // tm,),
        in_specs=[spec] * (2 * DIL_N_GROUPS),
        out_specs=spec,
        out_shape=jax.ShapeDtypeStruct((b * s, DIL_DIM), F32),
        compiler_params=pltpu.CompilerParams(
            dimension_semantics=("arbitrary",), vmem_limit_bytes=VMEM_LIMIT_BYTES),
        name="dilated_combine",
    )(*outs, *lses)
    return out.reshape(b, s, DIL_DIM)


SSM_CONV_TOK = 512
SSM_HALO = SUBLANES
SSM_HEADS_PER_GROUP = SSM_HEADS // SSM_GROUPS
SSM_GROUP_W = SSM_HEADS_PER_GROUP * SSM_HD
SSM_BC_W = SSM_GROUPS * SSM_STATE


def _softplus(x):
    return jnp.maximum(x, 0.0) + jnp.log1p(jnp.exp(-jnp.abs(x)))


def _ssm_conv_kernel(prev_ref, cur_ref, next_ref, w_ref, b_ref, out_ref, cat_scr):
    j = pl.program_id(1)
    tl = cur_ref.shape[1]
    cat_scr[0:SSM_HALO, :] = jnp.where(j > 0, prev_ref[0], 0.0)
    cat_scr[SSM_HALO:SSM_HALO + tl, :] = cur_ref[0]
    cat_scr[SSM_HALO + tl:, :] = jnp.where(j < pl.num_programs(1) - 1, next_ref[0], 0.0)
    acc = jnp.zeros((tl, CONV_CH), F32) + b_ref[...]
    for k in range(SSM_CONV):
        off = SSM_HALO + k - SSM_CONV // 2
        acc = acc + cat_scr[off:off + tl, :] * w_ref[k:k + 1, :]
    out_ref[0] = acc * jax.nn.sigmoid(acc)


def _ssm_conv(xbc, conv_w, conv_b, col):
    b, l, _ = xbc.shape
    tl = SSM_CONV_TOK
    n_steps = l // tl
    per = tl // SSM_HALO
    assert col % CONV_CH == 0
    cb = col // CONV_CH
    return pl.pallas_call(
        _ssm_conv_kernel,
        grid=(b, n_steps),
        in_specs=[
            pl.BlockSpec((1, SSM_HALO, CONV_CH), lambda i, j: (i, jnp.maximum(j * per - 1, 0), cb)),
            pl.BlockSpec((1, tl, CONV_CH), lambda i, j: (i, j, cb)),
            pl.BlockSpec((1, SSM_HALO, CONV_CH), lambda i, j: (i, jnp.minimum((j + 1) * per, n_steps * per - 1), cb)),
            pl.BlockSpec((SSM_CONV, CONV_CH), lambda i, j: (0, 0)),
            pl.BlockSpec((1, CONV_CH), lambda i, j: (0, 0)),
        ],
        out_specs=pl.BlockSpec((1, tl, CONV_CH), lambda i, j: (i, j, 0)),
        out_shape=jax.ShapeDtypeStruct((b, l, CONV_CH), F32),
        scratch_shapes=[pltpu.VMEM((tl + 2 * SSM_HALO, CONV_CH), F32)],
        compiler_params=pltpu.CompilerParams(
            dimension_semantics=("arbitrary", "arbitrary"), vmem_limit_bytes=VMEM_LIMIT_BYTES),
        name="ssm_conv",
    )(xbc, xbc, xbc, conv_w, conv_b.reshape(1, CONV_CH))


def _ssd_kernel(xf_ref, dtf_ref, dttf_ref, xb_ref, dtb_ref, dttb_ref,
                expf_ref, bef_ref, aef_ref, btf_ref, atf_ref, expb_ref, beb_ref, aeb_ref, btb_ref, atb_ref,
                yf_ref, yb_ref, state_scr):
    @pl.when(pl.program_id(1) == 0)
    def _():
        state_scr[...] = jnp.zeros(state_scr.shape, F32)

    fwd = _ssd_chunk(xf_ref, dtf_ref, dttf_ref, expf_ref, bef_ref, aef_ref, btf_ref, atf_ref,
                     yf_ref, state_scr.at[0], reverse=False)
    bwd = _ssd_chunk(xb_ref, dtb_ref, dttb_ref, expb_ref, beb_ref, aeb_ref, btb_ref, atb_ref,
                     yb_ref, state_scr.at[1], reverse=True)
    for _ in zip(fwd, bwd):
        pass
    for _ in fwd:
        pass
    for _ in bwd:
        pass


def _ssd_chunk(xbc_ref, dt_ref, dtt_ref, expand_ref, bias_e_ref, a_e_ref, bias_t_ref, a_t_ref,
               y_ref, state_scr, *, reverse):
    q = SSM_CHUNK
    hi = lax.Precision.HIGHEST
    xbc = xbc_ref[0]
    xs = xbc[:, 0:SSM_INNER]
    dt_e = _softplus(jnp.dot(dt_ref[0], expand_ref[...], precision=hi, preferred_element_type=F32)
                     + bias_e_ref[...])
    a_e = dt_e * a_e_ref[...]
    ri = lax.broadcasted_iota(jnp.int32, (q, q), 0)
    ci = lax.broadcasted_iota(jnp.int32, (q, q), 1)
    seen = (ci >= ri) if reverse else (ci <= ri)
    cs_e = jnp.dot(seen.astype(F32), a_e, precision=hi, preferred_element_type=F32)
    dt_t = _softplus(dtt_ref[0] + bias_t_ref[...])
    cs_t = jnp.dot(dt_t * a_t_ref[...], seen.T.astype(F32), precision=hi, preferred_element_type=F32)
    yield
    dtx = xs * dt_e
    last = 0 if reverse else q - 1
    total = cs_e[last:last + 1, :]
    dtx_decayed = jnp.exp(total - cs_e) * dtx
    grow = jnp.exp(cs_e)
    chunk_decay = jnp.exp(total)
    lane_head = lax.broadcasted_iota(jnp.int32, (q, SSM_GROUP_W), 1) // SSM_HD
    for g in range(SSM_GROUPS):
        xl = slice(g * SSM_GROUP_W, (g + 1) * SSM_GROUP_W)
        bg = xbc[:, SSM_INNER + g * SSM_STATE:SSM_INNER + (g + 1) * SSM_STATE]
        cg = xbc[:, SSM_INNER + SSM_BC_W + g * SSM_STATE:SSM_INNER + SSM_BC_W + (g + 1) * SSM_STATE]
        cb = lax.dot_general(cg.astype(MXU_DTYPE), bg.astype(MXU_DTYPE), (((1,), (1,)), ((), ())),
                             preferred_element_type=F32)
        yield
        ms = []
        for r in range(SSM_HEADS_PER_GROUP):
            h = g * SSM_HEADS_PER_GROUP + r
            col = cs_e[:, h * SSM_HD:h * SSM_HD + 1]
            row = cs_t[h:h + 1, :]
            ms.append(cb * jnp.where(seen, jnp.exp(col - row), 0.0))
        y_all = jnp.dot(jnp.concatenate(ms, axis=0).astype(MXU_DTYPE), dtx[:, xl].astype(MXU_DTYPE),
                        preferred_element_type=F32)
        yield
        y_diag = jnp.zeros((q, SSM_GROUP_W), F32)
        for r in range(SSM_HEADS_PER_GROUP):
            y_diag = y_diag + jnp.where(lane_head == r, y_all[r * q:(r + 1) * q, :], 0.0)
        s_in = state_scr[g]
        y_off = jnp.dot(cg.astype(MXU_DTYPE), s_in.astype(MXU_DTYPE), preferred_element_type=F32) * grow[:, xl]
        y_ref[0, :, xl] = y_diag + y_off
        yield
        new = jnp.dot(bg.T.astype(MXU_DTYPE), dtx_decayed[:, xl].astype(MXU_DTYPE), preferred_element_type=F32)
        state_scr[g] = s_in * chunk_decay[:, xl] + new


def _ssd_scans(xbc_act, dt_arr, dt_t, a_log, dt_bias, col_dt, dt_w):
    b, l, _ = xbc_act.shape
    assert col_dt % dt_w == 0 and dt_w >= 2 * SSM_HEADS
    q = SSM_CHUNK
    nc = l // q
    head_of_lane = np.arange(SSM_INNER) // SSM_HD
    full = lambda r, c: pl.BlockSpec((r, c), lambda i, j: (0, 0))
    chunk = (lambda j: j, lambda j: nc - 1 - j)
    data_specs, param_specs, params = [], [], []
    for d in range(2):
        a = -jnp.exp(a_log[d].astype(F32))
        bias = dt_bias[d].astype(F32)
        expand = (np.arange(dt_w)[:, None] == d * SSM_HEADS + head_of_lane[None, :]).astype(np.float32)
        data_specs += [
            pl.BlockSpec((1, q, CONV_CH), lambda i, j, d=d: (i, chunk[d](j), 0)),
            pl.BlockSpec((1, q, dt_w), lambda i, j, d=d: (i, chunk[d](j), col_dt // dt_w)),
            pl.BlockSpec((1, SSM_HEADS, q), lambda i, j, d=d: (i * 2 + d, 0, chunk[d](j))),
        ]
        param_specs += [full(dt_w, SSM_INNER), full(1, SSM_INNER), full(1, SSM_INNER),
                        full(SSM_HEADS, 1), full(SSM_HEADS, 1)]
        params += [jnp.asarray(expand), bias[head_of_lane].reshape(1, SSM_INNER),
                   a[head_of_lane].reshape(1, SSM_INNER), bias.reshape(SSM_HEADS, 1), a.reshape(SSM_HEADS, 1)]
    y_shape = jax.ShapeDtypeStruct((b, l, SSM_INNER), F32)
    return pl.pallas_call(
        _ssd_kernel,
        grid=(b, nc),
        in_specs=data_specs + param_specs,
        out_specs=[pl.BlockSpec((1, q, SSM_INNER), lambda i, j, d=d: (i, chunk[d](j), 0)) for d in range(2)],
        out_shape=[y_shape, y_shape],
        scratch_shapes=[pltpu.VMEM((2, SSM_GROUPS, SSM_STATE, SSM_GROUP_W), F32)],
        compiler_params=pltpu.CompilerParams(
            dimension_semantics=("arbitrary", "arbitrary"), vmem_limit_bytes=VMEM_LIMIT_BYTES),
        name="ssd_scans",
    )(xbc_act, dt_arr, dt_t, xbc_act, dt_arr, dt_t, *params)


def _ssm_gate_kernel(yf_ref, yb_ref, xbc_ref, z_ref, d_ref, g_ref, out_ref):
    z = z_ref[...]
    y = (yf_ref[...] + yb_ref[...] + xbc_ref[...] * d_ref[...]) * (z * jax.nn.sigmoid(z))
    out_ref[...] = y * lax.rsqrt(jnp.mean(y * y, axis=-1, keepdims=True) + EPS) * g_ref[...]


def mamba2_mixer(arr, conv_w, conv_b, A_log, dt_bias, D_skip, norm_g,
                 col_z=0, col_xbc=CONV_CH, col_dt=SSM_INNER + CONV_CH, dt_w=2 * SSM_HEADS):
    b, l, width = arr.shape
    assert col_z % SSM_INNER == 0
    xbc_act = _ssm_conv(arr, conv_w, conv_b, col_xbc)
    dt_t = jnp.swapaxes(arr[:, :, col_dt:col_dt + 2 * SSM_HEADS], 1, 2).reshape(b * 2, SSM_HEADS, l)
    y_f, y_b = _ssd_scans(xbc_act, arr, dt_t, A_log, dt_bias, col_dt, dt_w)
    tm = 1024
    tok = lambda cb: pl.BlockSpec((tm, SSM_INNER), lambda i: (i, cb))
    row = pl.BlockSpec((1, SSM_INNER), lambda i: (0, 0))
    d_e = D_skip.astype(F32)[np.arange(SSM_INNER) // SSM_HD].reshape(1, SSM_INNER)
    out = pl.pallas_call(
        _ssm_gate_kernel,
        grid=(b * l // tm,),
        in_specs=[tok(0), tok(0), tok(0), tok(col_z // SSM_INNER), row, row],
        out_specs=tok(0),
        out_shape=jax.ShapeDtypeStruct((b * l, SSM_INNER), F32),
        compiler_params=pltpu.CompilerParams(
            dimension_semantics=("arbitrary",), vmem_limit_bytes=VMEM_LIMIT_BYTES),
        name="ssm_gate",
    )(y_f.reshape(b * l, SSM_INNER), y_b.reshape(b * l, SSM_INNER), xbc_act.reshape(b * l, CONV_CH),
      arr.reshape(b * l, width), d_e, norm_g.reshape(1, SSM_INNER))
    return out.reshape(b, l, SSM_INNER)


NA_DIM = NA_HEADS * NA_HD
NA_ROWS_PER_STEP = 8
NA_WIN = NA_ROWS * GRID_W
NA_STEP_TOK = NA_ROWS_PER_STEP * GRID_W


def _na_bias_table(rpb):
    n_dc = 2 * NA_COLS - 1
    edge_l = jnp.repeat(rpb[:, :, :1], GRID_W, axis=2)
    edge_r = jnp.repeat(rpb[:, :, -1:], GRID_W, axis=2)
    ext = jnp.concatenate([edge_l, rpb.astype(F32), edge_r], axis=2)
    by_col = jnp.stack([ext[:, :, GRID_W + NA_COLS - 1 - qc:2 * GRID_W + NA_COLS - 1 - qc]
                        for qc in range(GRID_W)], axis=2)
    qc = np.arange(GRID_W)[:, None]
    kc = np.arange(GRID_W)[None, :]
    cs = np.clip(qc - NA_COLS // 2, 0, GRID_W - NA_COLS)
    ok = (kc >= cs) & (kc < cs + NA_COLS)
    by_col = jnp.where(ok[None, None], by_col, -jnp.inf)
    tabs = []
    for delta in range(NA_ROWS):
        rows = by_col[:, NA_ROWS - 1 - delta:2 * NA_ROWS - 1 - delta]
        tabs.append(rows.transpose(0, 2, 1, 3).reshape(NA_HEADS * GRID_W, NA_WIN))
    assert n_dc == rpb.shape[2]
    return jnp.stack(tabs, axis=0)


def _na_kernel(prev_ref, cur_ref, next_ref, tab_ref, out_ref, k_scr, v_scr, *, n_rows):
    step = pl.program_id(1)
    for i, ref in enumerate((prev_ref, cur_ref, next_ref)):
        k_scr[i * NA_STEP_TOK:(i + 1) * NA_STEP_TOK, :] = ref[0, :, NA_DIM:2 * NA_DIM].astype(MXU_DTYPE)
        v_scr[i * NA_STEP_TOK:(i + 1) * NA_STEP_TOK, :] = ref[0, :, 2 * NA_DIM:3 * NA_DIM].astype(MXU_DTYPE)
    lane_head = lax.broadcasted_iota(jnp.int32, (GRID_W, NA_DIM), 1) // NA_HD
    row0 = step * NA_ROWS_PER_STEP
    for j in range(NA_ROWS_PER_STEP):
        r = row0 + j
        r0 = jnp.clip(r - NA_ROWS // 2, 0, n_rows - NA_ROWS)
        start = pl.multiple_of((r0 - row0 + NA_ROWS_PER_STEP) * GRID_W, GRID_W)
        q = cur_ref[0, j * GRID_W:(j + 1) * GRID_W, 0:NA_DIM] * np.float32(NA_HD ** -0.5)
        qs = jnp.concatenate([jnp.where(lane_head == h, q, 0.0) for h in range(NA_HEADS)], axis=0)
        kw = k_scr[pl.ds(start, NA_WIN), :]
        vw = v_scr[pl.ds(start, NA_WIN), :]
        logits = lax.dot_general(qs.astype(MXU_DTYPE), kw, (((1,), (1,)), ((), ())),
                                 preferred_element_type=F32) + tab_ref[r - r0]
        m = jnp.max(logits, axis=-1, keepdims=True)
        p = jnp.exp(logits - m)
        denom = jnp.sum(p, axis=-1, keepdims=True)
        o_all = jnp.dot(p.astype(MXU_DTYPE), vw, preferred_element_type=F32) / denom
        o = jnp.zeros((GRID_W, NA_DIM), F32)
        for h in range(NA_HEADS):
            o = o + jnp.where(lane_head == h, o_all[h * GRID_W:(h + 1) * GRID_W, :], 0.0)
        out_ref[0, j * GRID_W:(j + 1) * GRID_W, :] = o


def na_mixer(qkv, rpb, col=0):
    b, s, _ = qkv.shape
    n_rows = s // GRID_W
    assert n_rows >= NA_ROWS and n_rows % NA_ROWS_PER_STEP == 0 and col % (3 * NA_DIM) == 0
    n_steps = n_rows // NA_ROWS_PER_STEP
    blk = (1, NA_STEP_TOK, 3 * NA_DIM)
    cb = col // (3 * NA_DIM)
    return pl.pallas_call(
        functools.partial(_na_kernel, n_rows=n_rows),
        grid=(b, n_steps),
        in_specs=[
            pl.BlockSpec(blk, lambda i, j: (i, jnp.maximum(j - 1, 0), cb)),
            pl.BlockSpec(blk, lambda i, j: (i, j, cb)),
            pl.BlockSpec(blk, lambda i, j: (i, jnp.minimum(j + 1, n_steps - 1), cb)),
            pl.BlockSpec((NA_ROWS, NA_HEADS * GRID_W, NA_WIN), lambda i, j: (0, 0, 0)),
        ],
        out_specs=pl.BlockSpec((1, NA_STEP_TOK, NA_DIM), lambda i, j: (i, j, 0)),
        out_shape=jax.ShapeDtypeStruct((b, s, NA_DIM), F32),
        scratch_shapes=[
            pltpu.VMEM((3 * NA_STEP_TOK, NA_DIM), MXU_DTYPE),
            pltpu.VMEM((3 * NA_STEP_TOK, NA_DIM), MXU_DTYPE),
        ],
        compiler_params=pltpu.CompilerParams(
            dimension_semantics=("arbitrary", "arbitrary"), vmem_limit_bytes=VMEM_LIMIT_BYTES),
        name="na_attention",
    )(qkv, qkv, qkv, _na_bias_table(rpb))


PK_GATE = 0
PK_XBC = PK_GATE + N_BRANCH * D_MODEL
PK_CQ = PK_XBC + CONV_CH
PK_DIL = PK_CQ + MLA_Q_RANK
PK_NA = PK_DIL + DIL_N_GROUPS * 3 * DIL_DIM
PK_KR = PK_NA + 3 * NA_DIM
PK_Z = PK_KR + 2 * MLA_HG
PK_CKV = PK_Z + SSM_INNER
PK_DT = PK_CKV + MLA_KV_RANK
PK_DT_W = LANES
PK_WIDTH = PK_DT + PK_DT_W
INPROJ_TM = 512
INPROJ_TN = PK_WIDTH // 2
MERGE_TM = 512


def _pack_w_in(w_in_l):
    gate, a_cq, a_ckv, a_kr, b_qkv, c_z, c_xbc, c_dt, d_qkv = jnp.split(w_in_l, IN_SPLITS, axis=-1)
    zeros = lambda n: jnp.zeros((D_MODEL, n), w_in_l.dtype)
    cols = [gate, c_xbc, a_cq, b_qkv, d_qkv, mla_rotary_key_columns(a_kr), c_z, a_ckv,
            c_dt, zeros(PK_DT_W - 2 * SSM_HEADS)]
    packed = jnp.concatenate(cols, axis=-1)
    assert packed.shape[1] == PK_DT + PK_DT_W
    return jnp.concatenate([packed, zeros(PK_WIDTH - packed.shape[1])], axis=-1).astype(MXU_DTYPE)


def _inproj_kernel(x_ref, g_ref, w_ref, out_ref):
    x = x_ref[...]
    h = (x * lax.rsqrt(jnp.mean(x * x, axis=-1, keepdims=True) + EPS) * g_ref[...]).astype(MXU_DTYPE)
    out_ref[...] = jnp.dot(h, w_ref[...], preferred_element_type=F32)


def _inproj(x2d, norm_g, w_packed):
    n_tok = x2d.shape[0]
    tm, tn = INPROJ_TM, INPROJ_TN
    return pl.pallas_call(
        _inproj_kernel,
        grid=(PK_WIDTH // tn, n_tok // tm),
        in_specs=[pl.BlockSpec((tm, D_MODEL), lambda j, i: (i, 0)),
                  pl.BlockSpec((1, D_MODEL), lambda j, i: (0, 0)),
                  pl.BlockSpec((D_MODEL, tn), lambda j, i: (0, j))],
        out_specs=pl.BlockSpec((tm, tn), lambda j, i: (i, j)),
        out_shape=jax.ShapeDtypeStruct((n_tok, PK_WIDTH), F32),
        compiler_params=pltpu.CompilerParams(
            dimension_semantics=("arbitrary", "arbitrary"), vmem_limit_bytes=VMEM_LIMIT_BYTES),
        name="in_projection",
    )(x2d, norm_g.reshape(1, D_MODEL), w_packed)


def _merge_kernel(x_ref, gate_ref, bg_ref, ya_ref, yb_ref, yc_ref, yd_ref, wb_ref, wo_ref, out_ref):
    merged = jnp.zeros(x_ref.shape, F32)
    for i, y_ref in enumerate((ya_ref, yb_ref, yc_ref, yd_ref)):
        proj = jnp.dot(y_ref[...].astype(MXU_DTYPE), wb_ref[BRANCH_ROWS[i]:BRANCH_ROWS[i + 1], :],
                       preferred_element_type=F32)
        lanes = slice(i * D_MODEL, (i + 1) * D_MODEL)
        merged = merged + jax.nn.sigmoid(gate_ref[:, lanes] + bg_ref[:, lanes]) * proj
    out_ref[...] = x_ref[...] + jnp.dot(merged.astype(MXU_DTYPE), wo_ref[...], preferred_element_type=F32)


def _merge(x2d, packed, b_gate, ys, w_branch, w_out):
    n_tok = x2d.shape[0]
    tm = MERGE_TM
    tok = lambda w: pl.BlockSpec((tm, w), lambda i: (i, 0))
    full = lambda r, c: pl.BlockSpec((r, c), lambda i: (0, 0))
    n_gate = N_BRANCH * D_MODEL
    return pl.pallas_call(
        _merge_kernel,
        grid=(n_tok // tm,),
        in_specs=[tok(D_MODEL), tok(n_gate), full(1, n_gate)] + [tok(w) for w in BRANCH_WIDTHS]
                 + [full(BRANCH_ROWS[-1], D_MODEL), full(D_MODEL, D_MODEL)],
        out_specs=tok(D_MODEL),
        out_shape=jax.ShapeDtypeStruct((n_tok, D_MODEL), F32),
        compiler_params=pltpu.CompilerParams(
            dimension_semantics=("arbitrary",), vmem_limit_bytes=VMEM_LIMIT_BYTES),
        name="branch_merge",
    )(x2d, packed, b_gate.reshape(1, n_gate), *[y.reshape(n_tok, -1) for y in ys], w_branch, w_out)


def encoder(x, norm1_g, w_in_packed, b_gate, mla_q_norm, mla_w_qb, mla_kv_norm, mla_w_kvb, t5_table,
            ssm_conv_w, ssm_conv_b, ssm_A_log, ssm_dt_bias, ssm_D, ssm_norm_g, na_rpb,
            w_branch, w_out, norm2_g, peer_wq, peer_keys, peer_u, peer_vt, final_g):
    b, s, _ = x.shape
    x2d = x.reshape(b * s, D_MODEL)
    for l in range(DEPTH):
        packed = _inproj(x2d, norm1_g[l], w_in_packed[l])
        p3 = packed.reshape(b, s, PK_WIDTH)
        y_a = mla_mixer(p3, mla_q_norm[l], mla_w_qb[l], mla_kv_norm[l], mla_w_kvb[l],
                        col_q=PK_CQ, col_kv=PK_CKV, col_kr=PK_KR)
        y_b = dilated_mixer(p3, t5_table, col=PK_DIL)
        y_c = mamba2_mixer(p3, ssm_conv_w[l], ssm_conv_b[l], ssm_A_log[l], ssm_dt_bias[l], ssm_D[l],
                           ssm_norm_g[l], col_z=PK_Z, col_xbc=PK_XBC, col_dt=PK_DT, dt_w=PK_DT_W)
        y_d = na_mixer(p3, na_rpb[l], col=PK_NA)
        x2d = _merge(x2d, packed, b_gate[l], (y_a, y_b, y_c, y_d), w_branch[l], w_out[l])
        x2d = peer_block(x2d, norm2_g[l], peer_wq[l], peer_keys[l], peer_u[l], peer_vt[l],
                         final_g, final_norm=(l == DEPTH - 1))
    return x2d.reshape(b, s, D_MODEL)


def kernel(x_prompt, x_sample, norm1_g, w_in, b_gate, mla_q_norm, mla_w_qb, mla_kv_norm, mla_w_kvb, t5_table, ssm_conv_w, ssm_conv_b, ssm_A_log, ssm_dt_bias, ssm_D, ssm_norm_g, na_rpb, w_branch, w_out, norm2_g, peer_wq, peer_keys, peer_u, peer_v, final_g):
    peer_u16 = peer_u.astype(MXU_DTYPE)
    peer_vt16 = jnp.swapaxes(peer_v, 1, 2).astype(MXU_DTYPE)
    w_in_packed = jnp.stack([_pack_w_in(w_in[l]) for l in range(DEPTH)])
    shared = (norm1_g, w_in_packed, b_gate, mla_q_norm, mla_w_qb, mla_kv_norm, mla_w_kvb, t5_table,
              ssm_conv_w, ssm_conv_b, ssm_A_log, ssm_dt_bias, ssm_D, ssm_norm_g, na_rpb,
              w_branch.astype(MXU_DTYPE), w_out.astype(MXU_DTYPE), norm2_g, peer_wq, peer_keys,
              peer_u16, peer_vt16, final_g)
    y_prompt = encoder(x_prompt, *shared)
    y_sample = encoder(x_sample, *shared)
    return (y_prompt, y_sample)
```

```python
import functools
import math

import numpy as np
import jax
import jax.numpy as jnp
from jax import lax
from jax.experimental import pallas as pl
from jax.experimental.pallas import tpu as pltpu

F32 = jnp.float32
BF16 = jnp.bfloat16
MXU_DTYPE = BF16

D_MODEL = 1024
DEPTH = 2
GRID_W = 64
EPS = 1e-6
N_BRANCH = 4

MLA_HEADS = 4
MLA_Q_RANK = 256
MLA_KV_RANK = 128
MLA_NOPE = 64
MLA_ROPE = 32
MLA_V = 64
ROPE_THETA = 10000.0

DIL_GROUPS = ((128, 1), (512, 4), (2048, 16))
DIL_HEADS = 4
DIL_HD = 64
T5_BUCKETS = 32
T5_MAX_DIST = 1024

SSM_HEADS = 8
SSM_HD = 64
SSM_INNER = SSM_HEADS * SSM_HD
SSM_GROUPS = 2
SSM_STATE = 128
SSM_CONV = 7
SSM_CHUNK = 128
CONV_CH = SSM_INNER + 2 * SSM_GROUPS * SSM_STATE

NA_HEADS = 4
NA_HD = 64
NA_ROWS = 8
NA_COLS = 16

PEER_HEADS = 8
PEER_KEYS = 128
PEER_EXPERTS = PEER_KEYS * PEER_KEYS
PEER_QDIM = 256
PEER_TOPK = 16

BRANCH_WIDTHS = (MLA_HEADS * MLA_V, DIL_HEADS * DIL_HD, SSM_INNER, NA_HEADS * NA_HD)
BRANCH_ROWS = tuple(sum(BRANCH_WIDTHS[:i]) for i in range(N_BRANCH + 1))
IN_SIZES = (N_BRANCH * D_MODEL, MLA_Q_RANK, MLA_KV_RANK, MLA_ROPE,
            len(DIL_GROUPS) * 3 * DIL_HEADS * DIL_HD,
            SSM_INNER, CONV_CH, 2 * SSM_HEADS,
            3 * NA_HEADS * NA_HD)
IN_SPLITS = tuple(sum(IN_SIZES[:i + 1]) for i in range(len(IN_SIZES) - 1))

VMEM_LIMIT_BYTES = 56 * 1024 * 1024
LANES = 128
SUBLANES = 8


PEER_ROUTE_TB = 512
PEER_TB = 512
PEER_EC = 1024
PEER_GATE_DTYPE = BF16
PEER_GATE_ROWS = 16


def _gate_pack():
    return 4 // jnp.dtype(PEER_GATE_DTYPE).itemsize


def _gate_to_storage(x):
    x = x.astype(PEER_GATE_DTYPE)
    return x if _gate_pack() == 1 else pltpu.bitcast(x, jnp.uint32)


def _gate_from_storage(x):
    return x if _gate_pack() == 1 else pltpu.bitcast(x, PEER_GATE_DTYPE)


def _gate_splat(row):
    return jnp.broadcast_to(row.astype(PEER_GATE_DTYPE), (PEER_GATE_ROWS, row.shape[1]))
PEER_HALF = PEER_QDIM // 2
PEER_CAND_ROWS = 2 * SUBLANES + 7 * SUBLANES + SUBLANES


def _gelu_exact_x2(x):
    return x * (1.0 + lax.erf(x * np.float32(math.sqrt(0.5))))


def _extract_desc(vals, n_out, out_ref, row0):
    for k in range(n_out):
        m = jnp.max(vals, axis=0, keepdims=True)
        out_ref[pl.ds(row0 + k, 1), :] = m
        vals = jnp.where(vals == m, -jnp.inf, vals)


def _oddeven_sort_network(lo, hi):
    def merge(lo, hi, r):
        step = r * 2
        if step < hi - lo:
            yield from merge(lo, hi, step)
            yield from merge(lo + r, hi, step)
            yield from ((i, i + r) for i in range(lo + r, hi - r, step))
        else:
            yield (lo, lo + r)

    if hi - lo >= 1:
        mid = lo + (hi - lo) // 2
        yield from _oddeven_sort_network(lo, mid)
        yield from _oddeven_sort_network(mid + 1, hi)
        yield from merge(lo, hi, 1)


def _bitonic_merge_network(n):
    d = n // 2
    while d >= 1:
        yield from ((i, i + d) for i in range(n) if (i // d) % 2 == 0)
        d //= 2


PEER_SORT_NET = tuple(_oddeven_sort_network(0, PEER_TOPK - 1))
PEER_MERGE_NET = tuple(_bitonic_merge_network(PEER_TOPK))


def _compare_exchange(tiles, network):
    for i, j in network:
        tiles[i], tiles[j] = jnp.maximum(tiles[i], tiles[j]), jnp.minimum(tiles[i], tiles[j])


def _sorted_top(s):
    assert s.shape[0] == PEER_TOPK * SUBLANES
    tiles = [s[i * SUBLANES:(i + 1) * SUBLANES, :] for i in range(PEER_TOPK)]
    _compare_exchange(tiles, PEER_SORT_NET)
    shift = SUBLANES // 2
    while shift >= 1:
        tiles = [jnp.maximum(tiles[i], pltpu.roll(tiles[PEER_TOPK - 1 - i], shift, axis=0))
                 for i in range(PEER_TOPK)]
        _compare_exchange(tiles, PEER_MERGE_NET)
        shift //= 2
    return tiles


def _peer_route_kernel(x_ref, g_ref, wqt_ref, keys_ref,
                       xnt_ref, cnt_ref, rank_ref, e1_ref, e2_ref,
                       qt_scr, top_scr, cand_scr, tops_scr):
    x = x_ref[...]
    xn = x * lax.rsqrt(jnp.mean(x * x, axis=-1, keepdims=True) + EPS) * g_ref[...]
    xnt = xn.T.astype(MXU_DTYPE)
    xnt_ref[...] = xnt
    qt_scr[...] = jnp.dot(wqt_ref[...], xnt, preferred_element_type=F32).astype(MXU_DTYPE)

    def head(h, carry):
        q1 = qt_scr[pl.ds(pl.multiple_of(h * PEER_QDIM, PEER_QDIM), PEER_HALF), :]
        q2 = qt_scr[pl.ds(pl.multiple_of(h * PEER_QDIM + PEER_HALF, PEER_HALF), PEER_HALF), :]
        s1_all = jnp.dot(keys_ref[2 * h], q1, preferred_element_type=F32)
        s2_all = jnp.dot(keys_ref[2 * h + 1], q2, preferred_element_type=F32)
        for l0 in range(0, s1_all.shape[1], LANES):
            lanes = slice(l0, l0 + LANES)
            s1, s2 = s1_all[:, lanes], s2_all[:, lanes]
            top1, top2 = _sorted_top(s1), _sorted_top(s2)
            for k in range(PEER_TOPK):
                top_scr[k:k + 1, lanes] = top1[k][0:1, :]
                top_scr[PEER_TOPK + k:PEER_TOPK + k + 1, lanes] = top2[k][0:1, :]
            t1 = top_scr[0:PEER_TOPK, lanes]
            t2 = top_scr[PEER_TOPK:2 * PEER_TOPK, lanes]
            cand_scr[0:2 * SUBLANES, lanes] = t1[0:1, :] + t2
            for a in range(1, SUBLANES):
                cand_scr[(a + 1) * SUBLANES:(a + 2) * SUBLANES, lanes] = t1[a:a + 1, :] + t2[0:SUBLANES, :]
            cand_scr[9 * SUBLANES:10 * SUBLANES, lanes] = t1[SUBLANES:2 * SUBLANES, :] + t2[0:1, :]
            _extract_desc(cand_scr[:, lanes], PEER_TOPK, tops_scr.at[:, lanes], 0)
            top_s = tops_scr[:, lanes]
            z = jnp.sum(jnp.exp(top_s - top_s[0:1, :]), axis=0, keepdims=True)
            tau = top_s[PEER_TOPK - 1:PEER_TOPK, :]
            rank_tiles = []
            for i in range(PEER_KEYS // SUBLANES):
                s2_tile = s2[i * SUBLANES:(i + 1) * SUBLANES, :]
                rank = jnp.zeros(s2_tile.shape, F32)
                for k in range(PEER_TOPK):
                    rank = rank + jnp.where(top2[k] > s2_tile, 1.0, 0.0)
                rank_tiles.append(rank)
            cnt_top = jnp.zeros(t1.shape, F32)
            for b in range(PEER_TOPK):
                cnt_top = cnt_top + jnp.where(t1 + t2[b:b + 1, :] >= tau, 1.0, 0.0)
            cnt = jnp.zeros(s1.shape, F32)
            for a in range(PEER_TOPK):
                cnt = jnp.where(s1 == t1[a:a + 1, :], cnt_top[a:a + 1, :], cnt)
            cnt_ref[h, :, lanes] = cnt
            rank_ref[h, :, lanes] = _gate_to_storage(jnp.concatenate(rank_tiles, axis=0))
            e1_ref[h, :, lanes] = jnp.exp(s1 - t1[0:1, :]) * (0.5 / z)
            e2_ref[h, :, lanes] = _gate_to_storage(jnp.exp(s2 - t2[0:1, :]))
        return carry

    lax.fori_loop(0, PEER_HEADS, head, 0)


def _peer_route(x2d, g, wqt, keys):
    n_tok = x2d.shape[0]
    tb = PEER_ROUTE_TB
    rt_shape = jax.ShapeDtypeStruct((PEER_HEADS, PEER_KEYS, n_tok), F32)
    gate_rows = PEER_KEYS // _gate_pack()
    gate_store = PEER_GATE_DTYPE if _gate_pack() == 1 else jnp.uint32
    gate_shape = jax.ShapeDtypeStruct((PEER_HEADS, gate_rows, n_tok), gate_store)
    rt_spec = pl.BlockSpec((PEER_HEADS, PEER_KEYS, tb), lambda i: (0, 0, i))
    gate_spec = pl.BlockSpec((PEER_HEADS, gate_rows, tb), lambda i: (0, 0, i))
    return pl.pallas_call(
        _peer_route_kernel,
        grid=(n_tok // tb,),
        in_specs=[
            pl.BlockSpec((tb, D_MODEL), lambda i: (i, 0)),
            pl.BlockSpec((1, D_MODEL), lambda i: (0, 0)),
            pl.BlockSpec((PEER_HEADS * PEER_QDIM, D_MODEL), lambda i: (0, 0)),
            pl.BlockSpec((2 * PEER_HEADS, PEER_KEYS, PEER_HALF), lambda i: (0, 0, 0)),
        ],
        out_specs=[
            pl.BlockSpec((D_MODEL, tb), lambda i: (0, i)),
            rt_spec, gate_spec, rt_spec, gate_spec,
        ],
        out_shape=[
            jax.ShapeDtypeStruct((D_MODEL, n_tok), MXU_DTYPE),
            rt_shape, gate_shape, rt_shape, gate_shape,
        ],
        scratch_shapes=[
            pltpu.VMEM((PEER_HEADS * PEER_QDIM, tb), MXU_DTYPE),
            pltpu.VMEM((2 * PEER_TOPK, tb), F32),
            pltpu.VMEM((PEER_CAND_ROWS, tb), F32),
            pltpu.VMEM((PEER_TOPK, tb), F32),
        ],
        compiler_params=pltpu.CompilerParams(
            dimension_semantics=("arbitrary",), vmem_limit_bytes=VMEM_LIMIT_BYTES),
        name="peer_route",
    )(x2d, g, wqt, keys)


def _peer_expert_kernel(x_ref, xnt_ref, cnt_ref, rank_ref, e1_ref, e2_ref, u_ref, vt_ref, fg_ref,
                        out_ref, acc_scr, w_scr, *, final_norm):
    c = pl.program_id(1)
    n_chunks = pl.num_programs(1) - 1
    n_i1 = PEER_EC // PEER_KEYS
    tb = acc_scr.shape[1]
    gdt = PEER_GATE_DTYPE

    @pl.when(c == 0)
    def _():
        acc_scr[...] = jnp.zeros_like(acc_scr)
        w_scr[1] = jnp.zeros(w_scr.shape[1:], w_scr.dtype)

    slot = c % 2
    n_groups = PEER_KEYS // PEER_GATE_ROWS
    acc_scr[...] += jnp.dot(vt_ref[...], w_scr[(c + 1) % 2], preferred_element_type=F32)
    hid = jnp.dot(u_ref[...], xnt_ref[...], preferred_element_type=F32)

    def gate_block(i1l, l0):
        lanes = slice(l0, l0 + LANES)
        gates = [jnp.zeros((PEER_GATE_ROWS, LANES), gdt) for _ in range(n_groups)]
        for h in range(PEER_HEADS):
            cntb = _gate_splat(cnt_ref[h, i1l:i1l + 1, lanes])
            e1b = _gate_splat(e1_ref[h, i1l:i1l + 1, lanes])
            for k in range(n_groups):
                rows = slice(k * SUBLANES, (k + 1) * SUBLANES)
                w = _gate_from_storage(e2_ref[h, rows, lanes]) * e1b
                rank = _gate_from_storage(rank_ref[h, rows, lanes])
                gates[k] = gates[k] + jnp.where(rank < cntb, w, jnp.zeros_like(w))
        for k in range(n_groups):
            r0 = k * PEER_GATE_ROWS
            e0 = i1l * PEER_KEYS + r0
            act = _gelu_exact_x2(hid[e0:e0 + PEER_GATE_ROWS, lanes]).astype(gdt)
            w_scr[slot, e0:e0 + PEER_GATE_ROWS, lanes] = (act * gates[k]).astype(MXU_DTYPE)

    for i1l in range(n_i1):
        for l0 in range(0, tb, LANES):
            gate_block(i1l, l0)

    @pl.when(c == n_chunks)
    def _():
        y = x_ref[...] + acc_scr[...].T
        if final_norm:
            y = y * lax.rsqrt(jnp.mean(y * y, axis=-1, keepdims=True) + EPS) * fg_ref[...]
        out_ref[...] = y


def _peer_experts(x2d, xnt, cnt, rank2, e1, e2, u, vt, final_g, final_norm):
    n_tok = x2d.shape[0]
    tb, ec = PEER_TB, PEER_EC
    n_chunks = PEER_EXPERTS // ec
    assert PEER_GATE_ROWS == SUBLANES * _gate_pack()
    rt_spec = pl.BlockSpec((PEER_HEADS, PEER_KEYS // _gate_pack(), tb), lambda j, c: (0, 0, j))
    row_spec = pl.BlockSpec((PEER_HEADS, ec // PEER_KEYS, tb),
                            lambda j, c: (0, jnp.minimum(c, n_chunks - 1), j))
    return pl.pallas_call(
        functools.partial(_peer_expert_kernel, final_norm=final_norm),
        grid=(n_tok // tb, n_chunks + 1),
        in_specs=[
            pl.BlockSpec((tb, D_MODEL), lambda j, c: (j, 0)),
            pl.BlockSpec((D_MODEL, tb), lambda j, c: (0, j)),
            row_spec, rt_spec, row_spec, rt_spec,
            pl.BlockSpec((ec, D_MODEL), lambda j, c: (jnp.minimum(c, n_chunks - 1), 0)),
            pl.BlockSpec((D_MODEL, ec), lambda j, c: (0, jnp.maximum(c - 1, 0))),
            pl.BlockSpec((1, D_MODEL), lambda j, c: (0, 0)),
        ],
        out_specs=pl.BlockSpec((tb, D_MODEL), lambda j, c: (j, 0)),
        out_shape=jax.ShapeDtypeStruct((n_tok, D_MODEL), F32),
        scratch_shapes=[
            pltpu.VMEM((D_MODEL, tb), F32),
            pltpu.VMEM((2, ec, tb), MXU_DTYPE),
        ],
        compiler_params=pltpu.CompilerParams(
            dimension_semantics=("arbitrary", "arbitrary"), vmem_limit_bytes=VMEM_LIMIT_BYTES),
        name="peer_experts",
    )(x2d, xnt, cnt, rank2, e1, e2, u, vt, final_g.reshape(1, D_MODEL))


def peer_block(x2d, norm_g, w_q, keys, u, v, final_g, final_norm=False):
    wqt = w_q.T.astype(MXU_DTYPE)
    keys2 = keys.reshape(2 * PEER_HEADS, PEER_KEYS, PEER_HALF).astype(MXU_DTYPE)
    xnt, cnt, rank2, e1, e2 = _peer_route(x2d, norm_g.reshape(1, D_MODEL), wqt, keys2)
    return _peer_experts(x2d, xnt, cnt, rank2, e1, e2, u, v, final_g, final_norm)


MLA_HG = LANES
MLA_QK_W = MLA_HEADS * MLA_HG
MLA_V_W = MLA_HEADS * MLA_V
MLA_PREP_TOK = 512
MLA_TQ = 1024
MLA_TK = 2048
MLA_RHALF = MLA_ROPE // 2


def _mla_rope_tables(s):
    inv = ROPE_THETA ** (-jnp.arange(MLA_RHALF, dtype=F32) / MLA_RHALF)
    ang = jnp.arange(s).astype(F32)[:, None] * inv[None, :]
    cos, sin = jnp.cos(ang), jnp.sin(ang)
    zero_pad = jnp.zeros((s, MLA_HG - MLA_NOPE - MLA_ROPE), F32)
    cos_rot = jnp.concatenate([cos, cos, zero_pad], axis=1)
    sin_rot = jnp.concatenate([-sin, sin, zero_pad], axis=1)
    scale = np.float32((MLA_NOPE + MLA_ROPE) ** -0.5)
    q_cos = scale * jnp.concatenate([jnp.ones((s, MLA_NOPE), F32), cos_rot], axis=1)
    q_sin = scale * jnp.concatenate([jnp.zeros((s, MLA_NOPE), F32), sin_rot], axis=1)
    k_cos = jnp.concatenate([jnp.zeros((s, MLA_NOPE), F32), cos_rot], axis=1)
    k_sin = jnp.concatenate([jnp.zeros((s, MLA_NOPE), F32), sin_rot], axis=1)
    return q_cos, q_sin, k_cos, k_sin


def _mla_pack_weights(w_qb, w_kvb):
    hd_q = MLA_NOPE + MLA_ROPE
    wq = w_qb.reshape(MLA_Q_RANK, MLA_HEADS, hd_q)
    rot = wq[:, :, MLA_NOPE:]
    rot_sw = jnp.concatenate([rot[:, :, MLA_RHALF:], rot[:, :, :MLA_RHALF]], axis=2)
    pad = jnp.zeros((MLA_Q_RANK, MLA_HEADS, MLA_HG - hd_q), F32)
    wq_a = jnp.concatenate([wq, pad], axis=2).reshape(MLA_Q_RANK, MLA_QK_W)
    wq_b = jnp.concatenate([jnp.zeros_like(wq[:, :, :MLA_NOPE]), rot_sw, pad], axis=2).reshape(MLA_Q_RANK, MLA_QK_W)
    wkv = w_kvb.reshape(MLA_KV_RANK, MLA_HEADS, MLA_NOPE + MLA_V)
    wk = jnp.concatenate([wkv[:, :, :MLA_NOPE], jnp.zeros((MLA_KV_RANK, MLA_HEADS, MLA_HG - MLA_NOPE), F32)],
                         axis=2).reshape(MLA_KV_RANK, MLA_QK_W)
    wv = jnp.concatenate([wkv[:, :, MLA_NOPE:], jnp.zeros((MLA_KV_RANK, MLA_HEADS, MLA_HG - MLA_V), F32)],
                         axis=2).reshape(MLA_KV_RANK, MLA_QK_W)
    return (wq_a.astype(MXU_DTYPE), wq_b.astype(MXU_DTYPE), wk.astype(MXU_DTYPE), wv.astype(MXU_DTYPE))


def _mla_prep_kernel(cq_ref, ckv_ref, kr_ref, qn_ref, kvn_ref, wqa_ref, wqb_ref, wk_ref, wv_ref,
                     qcos_ref, qsin_ref, kcos_ref, ksin_ref, q_out, kt_out, v_out):
    cq = cq_ref[0]
    cqn = (cq * lax.rsqrt(jnp.mean(cq * cq, axis=-1, keepdims=True) + EPS) * qn_ref[...]).astype(MXU_DTYPE)
    qa = jnp.dot(cqn, wqa_ref[...], preferred_element_type=F32)
    qb = jnp.dot(cqn, wqb_ref[...], preferred_element_type=F32)
    ckv = ckv_ref[0]
    ckvn = (ckv * lax.rsqrt(jnp.mean(ckv * ckv, axis=-1, keepdims=True) + EPS) * kvn_ref[...]).astype(MXU_DTYPE)
    ka = jnp.dot(ckvn, wk_ref[...], preferred_element_type=F32)
    one_lane = (lax.broadcasted_iota(jnp.int32, (1, MLA_QK_W), 1) % MLA_HG == MLA_V).astype(F32)
    v_out[0] = (jnp.dot(ckvn, wv_ref[...], preferred_element_type=F32) + one_lane).astype(MXU_DTYPE)
    kr = kr_ref[0]
    k_rot = kr[:, 0:MLA_HG] * kcos_ref[...] + kr[:, MLA_HG:2 * MLA_HG] * ksin_ref[...]
    for h in range(MLA_HEADS):
        lanes = slice(h * MLA_HG, (h + 1) * MLA_HG)
        q_out[0, :, lanes] = (qa[:, lanes] * qcos_ref[...] + qb[:, lanes] * qsin_ref[...]).astype(MXU_DTYPE)
        kt_out[0, lanes, :] = (ka[:, lanes] + k_rot).T.astype(MXU_DTYPE)


def _mla_flash_kernel(q_ref, kt_ref, v_ref, out_ref, m_scr, acc_scr):
    ki = pl.program_id(2)

    @pl.when(ki == 0)
    def _():
        m_scr[...] = jnp.full(m_scr.shape, -jnp.inf, F32)
        acc_scr[...] = jnp.zeros(acc_scr.shape, F32)

    def logits(h):
        lanes = slice(h * MLA_HG, (h + 1) * MLA_HG)
        return jnp.dot(q_ref[0, :, lanes], kt_ref[0, lanes, :], preferred_element_type=F32)

    n_rep = kt_ref.shape[2] // MLA_HG
    s_next = logits(0)
    for h in range(MLA_HEADS):
        s = s_next
        if h + 1 < MLA_HEADS:
            s_next = logits(h + 1)
        m_old = m_scr[h]
        m_new = jnp.maximum(m_old, jnp.max(s, axis=-1, keepdims=True))
        p = jnp.exp(s - jnp.tile(m_new, (1, n_rep))).astype(MXU_DTYPE)
        acc_scr[h] = jnp.exp(m_old - m_new) * acc_scr[h] + jnp.dot(
            p, v_ref[0, :, h * MLA_HG:(h + 1) * MLA_HG], preferred_element_type=F32)
        m_scr[h] = m_new

    @pl.when(ki == pl.num_programs(2) - 1)
    def _():
        low = lax.broadcasted_iota(jnp.int32, (acc_scr.shape[1], MLA_HG), 1) < MLA_V
        outs = []
        for h in range(MLA_HEADS):
            acc = acc_scr[h]
            outs.append(acc / acc[:, MLA_V:MLA_V + 1])
        for hp in range(MLA_HEADS // 2):
            odd = pltpu.roll(outs[2 * hp + 1], MLA_V, axis=1)
            out_ref[0, :, hp * MLA_HG:(hp + 1) * MLA_HG] = jnp.where(low, outs[2 * hp], odd)


def mla_rotary_key_columns(w):
    zl = jnp.zeros(w.shape[:-1] + (MLA_NOPE,), w.dtype)
    zr = jnp.zeros(w.shape[:-1] + (MLA_HG - MLA_NOPE - MLA_ROPE,), w.dtype)
    w_sw = jnp.concatenate([w[..., MLA_RHALF:], w[..., :MLA_RHALF]], axis=-1)
    return jnp.concatenate([zl, w, zr, zl, w_sw, zr], axis=-1)


def mla_mixer(arr, q_norm, w_qb, kv_norm, w_kvb, col_q=0, col_kv=MLA_Q_RANK, col_kr=MLA_Q_RANK + MLA_KV_RANK):
    b, s, _ = arr.shape
    tt = MLA_PREP_TOK
    assert col_q % MLA_Q_RANK == 0 and col_kv % MLA_KV_RANK == 0 and col_kr % (2 * MLA_HG) == 0
    wqa, wqb, wk, wv = _mla_pack_weights(w_qb, w_kvb)
    q_cos, q_sin, k_cos, k_sin = _mla_rope_tables(s)
    tok = lambda w, c=0: pl.BlockSpec((1, tt, w), lambda i, j: (i, j, c // w))
    full = lambda r, c: pl.BlockSpec((r, c), lambda i, j: (0, 0))
    tab = pl.BlockSpec((tt, MLA_HG), lambda i, j: (j, 0))
    q, kt, v = pl.pallas_call(
        _mla_prep_kernel,
        grid=(b, s // tt),
        in_specs=[tok(MLA_Q_RANK, col_q), tok(MLA_KV_RANK, col_kv), tok(2 * MLA_HG, col_kr),
                  full(1, MLA_Q_RANK), full(1, MLA_KV_RANK),
                  full(MLA_Q_RANK, MLA_QK_W), full(MLA_Q_RANK, MLA_QK_W),
                  full(MLA_KV_RANK, MLA_QK_W), full(MLA_KV_RANK, MLA_QK_W),
                  tab, tab, tab, tab],
        out_specs=[tok(MLA_QK_W), pl.BlockSpec((1, MLA_QK_W, tt), lambda i, j: (i, 0, j)), tok(MLA_QK_W)],
        out_shape=[jax.ShapeDtypeStruct((b, s, MLA_QK_W), MXU_DTYPE),
                   jax.ShapeDtypeStruct((b, MLA_QK_W, s), MXU_DTYPE),
                   jax.ShapeDtypeStruct((b, s, MLA_QK_W), MXU_DTYPE)],
        compiler_params=pltpu.CompilerParams(
            dimension_semantics=("arbitrary", "arbitrary"), vmem_limit_bytes=VMEM_LIMIT_BYTES),
        name="mla_prep",
    )(arr, arr, arr, q_norm.reshape(1, -1), kv_norm.reshape(1, -1), wqa, wqb, wk, wv,
      q_cos, q_sin, k_cos, k_sin)
    tq, tk = min(MLA_TQ, s), min(MLA_TK, s)
    assert s % tq == 0 and s % tk == 0 and s % tt == 0
    return pl.pallas_call(
        _mla_flash_kernel,
        grid=(b, s // tq, s // tk),
        in_specs=[pl.BlockSpec((1, tq, MLA_QK_W), lambda i, j, kk: (i, j, 0)),
                  pl.BlockSpec((1, MLA_QK_W, tk), lambda i, j, kk: (i, 0, kk)),
                  pl.BlockSpec((1, tk, MLA_QK_W), lambda i, j, kk: (i, kk, 0))],
        out_specs=pl.BlockSpec((1, tq, MLA_V_W), lambda i, j, kk: (i, j, 0)),
        out_shape=jax.ShapeDtypeStruct((b, s, MLA_V_W), F32),
        scratch_shapes=[pltpu.VMEM((MLA_HEADS, tq, MLA_HG), F32),
                        pltpu.VMEM((MLA_HEADS, tq, MLA_HG), F32)],
        compiler_params=pltpu.CompilerParams(
            dimension_semantics=("arbitrary", "arbitrary", "arbitrary"), vmem_limit_bytes=VMEM_LIMIT_BYTES),
        name="mla_flash",
    )(q, kt, v)


def t5_bucket(rel):
    nb = T5_BUCKETS // 2
    ret = np.where(rel > 0, nb, 0)
    n = np.abs(rel)
    max_exact = nb // 2
    large = max_exact + (np.log(np.maximum(n, 1) / max_exact) / np.log(T5_MAX_DIST / max_exact)
                         * (nb - max_exact)).astype(np.int64)
    large = np.minimum(large, nb - 1)
    return (ret + np.where(n < max_exact, n, large)).astype(np.int32)


DIL_DIM = DIL_HEADS * DIL_HD
DIL_HALF = 64
DIL_QB = 128
DIL_KW = DIL_QB + 2 * DIL_HALF
DIL_TL = 512
DIL_N_GROUPS = len(DIL_GROUPS)


def _dil_bias_table(t5_table, gi, dil):
    rel = np.arange(DIL_KW)[None, :] - DIL_HALF - np.arange(DIL_QB)[:, None]
    bias = t5_table[:, gi * DIL_HEADS:(gi + 1) * DIL_HEADS][t5_bucket(rel * dil)].astype(F32)
    bias = jnp.where((np.abs(rel) <= DIL_HALF)[:, :, None], bias, -jnp.inf)
    return bias.transpose(2, 0, 1).reshape(DIL_HEADS * DIL_QB, DIL_KW)


def _dil_kernel(prev_ref, cur_ref, next_ref, tab_ref, o_ref, lse_ref, k_scr, v_scr, *, seq_len):
    step = pl.program_id(2)
    for i, ref in enumerate((prev_ref, cur_ref, next_ref)):
        k_scr[i * DIL_TL:(i + 1) * DIL_TL, :] = ref[0, :, DIL_DIM:2 * DIL_DIM].astype(MXU_DTYPE)
        v_scr[i * DIL_TL:(i + 1) * DIL_TL, :] = ref[0, :, 2 * DIL_DIM:3 * DIL_DIM].astype(MXU_DTYPE)
    lane_head = lax.broadcasted_iota(jnp.int32, (DIL_QB, DIL_DIM), 1) // DIL_HD
    key_off = lax.broadcasted_iota(jnp.int32, (1, DIL_KW), 1)
    for n in range(DIL_TL // DIL_QB):
        w0 = DIL_TL + n * DIL_QB - DIL_HALF
        kpos = step * DIL_TL + (n * DIL_QB - DIL_HALF) + key_off
        valid = (kpos >= 0) & (kpos < seq_len)
        q = cur_ref[0, n * DIL_QB:(n + 1) * DIL_QB, 0:DIL_DIM] * np.float32(DIL_HD ** -0.5)
        qs = jnp.concatenate([jnp.where(lane_head == h, q, 0.0) for h in range(DIL_HEADS)], axis=0)
        logits = lax.dot_general(qs.astype(MXU_DTYPE), k_scr[w0:w0 + DIL_KW, :], (((1,), (1,)), ((), ())),
                                 preferred_element_type=F32) + tab_ref[...]
        logits = jnp.where(valid, logits, -jnp.inf)
        m = jnp.max(logits, axis=-1, keepdims=True)
        p = jnp.exp(logits - m)
        denom = jnp.sum(p, axis=-1, keepdims=True)
        o_all = jnp.dot(p.astype(MXU_DTYPE), v_scr[w0:w0 + DIL_KW, :], preferred_element_type=F32) / denom
        lse_all = m + jnp.log(denom)
        o = jnp.zeros((DIL_QB, DIL_DIM), F32)
        lse = jnp.zeros((DIL_QB, DIL_DIM), F32)
        for h in range(DIL_HEADS):
            rows = slice(h * DIL_QB, (h + 1) * DIL_QB)
            o = o + jnp.where(lane_head == h, o_all[rows, :], 0.0)
            lse = lse + jnp.where(lane_head == h, lse_all[rows, :], 0.0)
        o_ref[0, n * DIL_QB:(n + 1) * DIL_QB, :] = o.astype(o_ref.dtype)
        lse_ref[0, n * DIL_QB:(n + 1) * DIL_QB, :] = lse


def _dil_group(qkv, t5_table, gi, dil, col):
    b, s, width = qkv.shape
    seq_len = s // dil
    assert seq_len % DIL_TL == 0 and col % (3 * DIL_DIM) == 0
    n_steps = seq_len // DIL_TL
    gi_col = col // (3 * DIL_DIM) + gi
    if dil > 1:
        qkv = qkv[:, :, gi_col * 3 * DIL_DIM:(gi_col + 1) * 3 * DIL_DIM]
        width, gi_col = 3 * DIL_DIM, 0
    n_col = width // (3 * DIL_DIM)
    view = qkv.reshape(b, seq_len, dil * width)
    blk = (1, DIL_TL, 3 * DIL_DIM)
    out_shape = jax.ShapeDtypeStruct((b, seq_len, dil * DIL_DIM), F32)
    out_spec = pl.BlockSpec((1, DIL_TL, DIL_DIM), lambda i, r, j: (i, j, r))
    o, lse = pl.pallas_call(
        functools.partial(_dil_kernel, seq_len=seq_len),
        grid=(b, dil, n_steps),
        in_specs=[
            pl.BlockSpec(blk, lambda i, r, j: (i, jnp.maximum(j - 1, 0), r * n_col + gi_col)),
            pl.BlockSpec(blk, lambda i, r, j: (i, j, r * n_col + gi_col)),
            pl.BlockSpec(blk, lambda i, r, j: (i, jnp.minimum(j + 1, n_steps - 1), r * n_col + gi_col)),
            pl.BlockSpec((DIL_HEADS * DIL_QB, DIL_KW), lambda i, r, j: (0, 0)),
        ],
        out_specs=[out_spec, out_spec],
        out_shape=[jax.ShapeDtypeStruct(out_shape.shape, MXU_DTYPE), out_shape],
        scratch_shapes=[pltpu.VMEM((3 * DIL_TL, DIL_DIM), MXU_DTYPE),
                        pltpu.VMEM((3 * DIL_TL, DIL_DIM), MXU_DTYPE)],
        compiler_params=pltpu.CompilerParams(
            dimension_semantics=("arbitrary", "arbitrary", "arbitrary"), vmem_limit_bytes=VMEM_LIMIT_BYTES),
        name=f"dilated_attention_g{gi}",
    )(view, view, view, _dil_bias_table(t5_table, gi, dil))
    return o.reshape(b, s, DIL_DIM), lse.reshape(b, s, DIL_DIM)


def _dil_combine_kernel(*refs):
    o_refs, lse_refs, out_ref = refs[:DIL_N_GROUPS], refs[DIL_N_GROUPS:2 * DIL_N_GROUPS], refs[-1]
    lses = [r[...] for r in lse_refs]
    m = functools.reduce(jnp.maximum, lses)
    ws = [jnp.exp(l - m) for l in lses]
    total = functools.reduce(jnp.add, ws)
    acc = functools.reduce(jnp.add, [w * r[...].astype(F32) for w, r in zip(ws, o_refs)])
    out_ref[...] = acc / total


def dilated_mixer(qkv, t5_table, col=0):
    b, s, _ = qkv.shape
    outs, lses = [], []
    for gi, (win, dil) in enumerate(DIL_GROUPS):
        assert win // (2 * dil) == DIL_HALF
        o, lse = _dil_group(qkv, t5_table, gi, dil, col)
        outs.append(o.reshape(b * s, DIL_DIM))
        lses.append(lse.reshape(b * s, DIL_DIM))
    tm = 1024
    spec = pl.BlockSpec((tm, DIL_DIM), lambda i: (i, 0))
    out = pl.pallas_call(
        _dil_combine_kernel,
        grid=(b * s // tm,),
        in_specs=[spec] * (2 * DIL_N_GROUPS),
        out_specs=spec,
        out_shape=jax.ShapeDtypeStruct((b * s, DIL_DIM), F32),
        compiler_params=pltpu.CompilerParams(
            dimension_semantics=("arbitrary",), vmem_limit_bytes=VMEM_LIMIT_BYTES),
        name="dilated_combine",
    )(*outs, *lses)
    return out.reshape(b, s, DIL_DIM)


SSM_CONV_TOK = 512
SSM_HALO = SUBLANES
SSM_HEADS_PER_GROUP = SSM_HEADS // SSM_GROUPS
SSM_GROUP_W = SSM_HEADS_PER_GROUP * SSM_HD
SSM_BC_W = SSM_GROUPS * SSM_STATE


def _softplus(x):
    return jnp.maximum(x, 0.0) + jnp.log1p(jnp.exp(-jnp.abs(x)))


def _split3(x):
    x1 = x.astype(BF16)
    r1 = x - x1.astype(F32)
    x2 = r1.astype(BF16)
    x3 = (r1 - x2.astype(F32)).astype(BF16)
    return x1, x2, x3


def _dot_with_01(a, b, exact):
    if exact == "lhs":
        a01 = a.astype(BF16)
        return sum(jnp.dot(a01, t, preferred_element_type=F32) for t in _split3(b))
    b01 = b.astype(BF16)
    return sum(jnp.dot(t, b01, preferred_element_type=F32) for t in _split3(a))


def _ssm_conv_kernel(prev_ref, cur_ref, next_ref, w_ref, b_ref, out_ref, cat_scr):
    j = pl.program_id(1)
    tl = cur_ref.shape[1]
    cat_scr[0:SSM_HALO, :] = jnp.where(j > 0, prev_ref[0], 0.0)
    cat_scr[SSM_HALO:SSM_HALO + tl, :] = cur_ref[0]
    cat_scr[SSM_HALO + tl:, :] = jnp.where(j < pl.num_programs(1) - 1, next_ref[0], 0.0)
    acc = jnp.zeros((tl, CONV_CH), F32) + b_ref[...]
    for k in range(SSM_CONV):
        off = SSM_HALO + k - SSM_CONV // 2
        acc = acc + cat_scr[off:off + tl, :] * w_ref[k:k + 1, :]
    out_ref[0] = acc * jax.nn.sigmoid(acc)


def _ssm_conv(xbc, conv_w, conv_b, col):
    b, l, _ = xbc.shape
    tl = SSM_CONV_TOK
    n_steps = l // tl
    per = tl // SSM_HALO
    assert col % CONV_CH == 0
    cb = col // CONV_CH
    return pl.pallas_call(
        _ssm_conv_kernel,
        grid=(b, n_steps),
        in_specs=[
            pl.BlockSpec((1, SSM_HALO, CONV_CH), lambda i, j: (i, jnp.maximum(j * per - 1, 0), cb)),
            pl.BlockSpec((1, tl, CONV_CH), lambda i, j: (i, j, cb)),
            pl.BlockSpec((1, SSM_HALO, CONV_CH), lambda i, j: (i, jnp.minimum((j + 1) * per, n_steps * per - 1), cb)),
            pl.BlockSpec((SSM_CONV, CONV_CH), lambda i, j: (0, 0)),
            pl.BlockSpec((1, CONV_CH), lambda i, j: (0, 0)),
        ],
        out_specs=pl.BlockSpec((1, tl, CONV_CH), lambda i, j: (i, j, 0)),
        out_shape=jax.ShapeDtypeStruct((b, l, CONV_CH), F32),
        scratch_shapes=[pltpu.VMEM((tl + 2 * SSM_HALO, CONV_CH), F32)],
        compiler_params=pltpu.CompilerParams(
            dimension_semantics=("arbitrary", "arbitrary"), vmem_limit_bytes=VMEM_LIMIT_BYTES),
        name="ssm_conv",
    )(xbc, xbc, xbc, conv_w, conv_b.reshape(1, CONV_CH))


def _ssd_kernel(xf_ref, dtf_ref, dttf_ref, xb_ref, dtb_ref, dttb_ref,
                expf_ref, bef_ref, aef_ref, btf_ref, atf_ref, expb_ref, beb_ref, aeb_ref, btb_ref, atb_ref,
                yf_ref, yb_ref, state_scr):
    @pl.when(pl.program_id(1) == 0)
    def _():
        state_scr[...] = jnp.zeros(state_scr.shape, F32)

    fwd = _ssd_chunk(xf_ref, dtf_ref, dttf_ref, expf_ref, bef_ref, aef_ref, btf_ref, atf_ref,
                     yf_ref, state_scr.at[0], reverse=False)
    bwd = _ssd_chunk(xb_ref, dtb_ref, dttb_ref, expb_ref, beb_ref, aeb_ref, btb_ref, atb_ref,
                     yb_ref, state_scr.at[1], reverse=True)
    for _ in zip(fwd, bwd):
        pass
    for _ in fwd:
        pass
    for _ in bwd:
        pass


def _ssd_chunk(xbc_ref, dt_ref, dtt_ref, expand_ref, bias_e_ref, a_e_ref, bias_t_ref, a_t_ref,
               y_ref, state_scr, *, reverse):
    q = SSM_CHUNK
    xbc = xbc_ref[0]
    xs = xbc[:, 0:SSM_INNER]
    dt_e = _softplus(_dot_with_01(dt_ref[0], expand_ref[...], "rhs") + bias_e_ref[...])
    a_e = dt_e * a_e_ref[...]
    ri = lax.broadcasted_iota(jnp.int32, (q, q), 0)
    ci = lax.broadcasted_iota(jnp.int32, (q, q), 1)
    seen = (ci >= ri) if reverse else (ci <= ri)
    cs_e = _dot_with_01(seen.astype(F32), a_e, "lhs")
    dt_t = _softplus(dtt_ref[0] + bias_t_ref[...])
    cs_t = _dot_with_01(dt_t * a_t_ref[...], seen.T.astype(F32), "rhs")
    yield
    dtx = xs * dt_e
    last = 0 if reverse else q - 1
    total = cs_e[last:last + 1, :]
    dtx_decayed = jnp.exp(total - cs_e) * dtx
    grow = jnp.exp(cs_e)
    chunk_decay = jnp.exp(total)
    lane_head = lax.broadcasted_iota(jnp.int32, (q, SSM_GROUP_W), 1) // SSM_HD
    for g in range(SSM_GROUPS):
        xl = slice(g * SSM_GROUP_W, (g + 1) * SSM_GROUP_W)
        bg = xbc[:, SSM_INNER + g * SSM_STATE:SSM_INNER + (g + 1) * SSM_STATE]
        cg = xbc[:, SSM_INNER + SSM_BC_W + g * SSM_STATE:SSM_INNER + SSM_BC_W + (g + 1) * SSM_STATE]
        cb = lax.dot_general(cg.astype(MXU_DTYPE), bg.astype(MXU_DTYPE), (((1,), (1,)), ((), ())),
                             preferred_element_type=F32)
        yield
        ms = []
        for r in range(SSM_HEADS_PER_GROUP):
            h = g * SSM_HEADS_PER_GROUP + r
            col = cs_e[:, h * SSM_HD:h * SSM_HD + 1]
            row = cs_t[h:h + 1, :]
            ms.append(cb * jnp.where(seen, jnp.exp(col - row), 0.0))
        y_all = jnp.dot(jnp.concatenate(ms, axis=0).astype(MXU_DTYPE), dtx[:, xl].astype(MXU_DTYPE),
                        preferred_element_type=F32)
        yield
        y_diag = jnp.zeros((q, SSM_GROUP_W), F32)
        for r in range(SSM_HEADS_PER_GROUP):
            y_diag = y_diag + jnp.where(lane_head == r, y_all[r * q:(r + 1) * q, :], 0.0)
        s_in = state_scr[g]
        y_off = jnp.dot(cg.astype(MXU_DTYPE), s_in.astype(MXU_DTYPE), preferred_element_type=F32) * grow[:, xl]
        y_ref[0, :, xl] = y_diag + y_off
        yield
        new = jnp.dot(bg.T.astype(MXU_DTYPE), dtx_decayed[:, xl].astype(MXU_DTYPE), preferred_element_type=F32)
        state_scr[g] = s_in * chunk_decay[:, xl] + new


def _ssd_scans(xbc_act, dt_arr, dt_t, a_log, dt_bias, col_dt, dt_w):
    b, l, _ = xbc_act.shape
    assert col_dt % dt_w == 0 and dt_w >= 2 * SSM_HEADS
    q = SSM_CHUNK
    nc = l // q
    head_of_lane = np.arange(SSM_INNER) // SSM_HD
    full = lambda r, c: pl.BlockSpec((r, c), lambda i, j: (0, 0))
    chunk = (lambda j: j, lambda j: nc - 1 - j)
    data_specs, param_specs, params = [], [], []
    for d in range(2):
        a = -jnp.exp(a_log[d].astype(F32))
        bias = dt_bias[d].astype(F32)
        expand = (np.arange(dt_w)[:, None] == d * SSM_HEADS + head_of_lane[None, :]).astype(np.float32)
        data_specs += [
            pl.BlockSpec((1, q, CONV_CH), lambda i, j, d=d: (i, chunk[d](j), 0)),
            pl.BlockSpec((1, q, dt_w), lambda i, j, d=d: (i, chunk[d](j), col_dt // dt_w)),
            pl.BlockSpec((1, SSM_HEADS, q), lambda i, j, d=d: (i * 2 + d, 0, chunk[d](j))),
        ]
        param_specs += [full(dt_w, SSM_INNER), full(1, SSM_INNER), full(1, SSM_INNER),
                        full(SSM_HEADS, 1), full(SSM_HEADS, 1)]
        params += [jnp.asarray(expand), bias[head_of_lane].reshape(1, SSM_INNER),
                   a[head_of_lane].reshape(1, SSM_INNER), bias.reshape(SSM_HEADS, 1), a.reshape(SSM_HEADS, 1)]
    y_shape = jax.ShapeDtypeStruct((b, l, SSM_INNER), F32)
    return pl.pallas_call(
        _ssd_kernel,
        grid=(b, nc),
        in_specs=data_specs + param_specs,
        out_specs=[pl.BlockSpec((1, q, SSM_INNER), lambda i, j, d=d: (i, chunk[d](j), 0)) for d in range(2)],
        out_shape=[y_shape, y_shape],
        scratch_shapes=[pltpu.VMEM((2, SSM_GROUPS, SSM_STATE, SSM_GROUP_W), F32)],
        compiler_params=pltpu.CompilerParams(
            dimension_semantics=("arbitrary", "arbitrary"), vmem_limit_bytes=VMEM_LIMIT_BYTES),
        name="ssd_scans",
    )(xbc_act, dt_arr, dt_t, xbc_act, dt_arr, dt_t, *params)


def _ssm_gate_kernel(yf_ref, yb_ref, xbc_ref, z_ref, d_ref, g_ref, out_ref):
    z = z_ref[...]
    y = (yf_ref[...] + yb_ref[...] + xbc_ref[...] * d_ref[...]) * (z * jax.nn.sigmoid(z))
    out_ref[...] = y * lax.rsqrt(jnp.mean(y * y, axis=-1, keepdims=True) + EPS) * g_ref[...]


def mamba2_mixer(arr, conv_w, conv_b, A_log, dt_bias, D_skip, norm_g,
                 col_z=0, col_xbc=CONV_CH, col_dt=SSM_INNER + CONV_CH, dt_w=2 * SSM_HEADS):
    b, l, width = arr.shape
    assert col_z % SSM_INNER == 0
    xbc_act = _ssm_conv(arr, conv_w, conv_b, col_xbc)
    dt_t = jnp.swapaxes(arr[:, :, col_dt:col_dt + 2 * SSM_HEADS], 1, 2).reshape(b * 2, SSM_HEADS, l)
    y_f, y_b = _ssd_scans(xbc_act, arr, dt_t, A_log, dt_bias, col_dt, dt_w)
    tm = 1024
    tok = lambda cb: pl.BlockSpec((tm, SSM_INNER), lambda i: (i, cb))
    row = pl.BlockSpec((1, SSM_INNER), lambda i: (0, 0))
    d_e = D_skip.astype(F32)[np.arange(SSM_INNER) // SSM_HD].reshape(1, SSM_INNER)
    out = pl.pallas_call(
        _ssm_gate_kernel,
        grid=(b * l // tm,),
        in_specs=[tok(0), tok(0), tok(0), tok(col_z // SSM_INNER), row, row],
        out_specs=tok(0),
        out_shape=jax.ShapeDtypeStruct((b * l, SSM_INNER), F32),
        compiler_params=pltpu.CompilerParams(
            dimension_semantics=("arbitrary",), vmem_limit_bytes=VMEM_LIMIT_BYTES),
        name="ssm_gate",
    )(y_f.reshape(b * l, SSM_INNER), y_b.reshape(b * l, SSM_INNER), xbc_act.reshape(b * l, CONV_CH),
      arr.reshape(b * l, width), d_e, norm_g.reshape(1, SSM_INNER))
    return out.reshape(b, l, SSM_INNER)


NA_DIM = NA_HEADS * NA_HD
NA_ROWS_PER_STEP = 8
NA_WIN = NA_ROWS * GRID_W
NA_STEP_TOK = NA_ROWS_PER_STEP * GRID_W


def _na_bias_table(rpb):
    n_dc = 2 * NA_COLS - 1
    edge_l = jnp.repeat(rpb[:, :, :1], GRID_W, axis=2)
    edge_r = jnp.repeat(rpb[:, :, -1:], GRID_W, axis=2)
    ext = jnp.concatenate([edge_l, rpb.astype(F32), edge_r], axis=2)
    by_col = jnp.stack([ext[:, :, GRID_W + NA_COLS - 1 - qc:2 * GRID_W + NA_COLS - 1 - qc]
                        for qc in range(GRID_W)], axis=2)
    qc = np.arange(GRID_W)[:, None]
    kc = np.arange(GRID_W)[None, :]
    cs = np.clip(qc - NA_COLS // 2, 0, GRID_W - NA_COLS)
    ok = (kc >= cs) & (kc < cs + NA_COLS)
    by_col = jnp.where(ok[None, None], by_col, -jnp.inf)
    tabs = []
    for delta in range(NA_ROWS):
        rows = by_col[:, NA_ROWS - 1 - delta:2 * NA_ROWS - 1 - delta]
        tabs.append(rows.transpose(0, 2, 1, 3).reshape(NA_HEADS * GRID_W, NA_WIN))
    assert n_dc == rpb.shape[2]
    return jnp.stack(tabs, axis=0)


def _na_kernel(prev_ref, cur_ref, next_ref, tab_ref, out_ref, k_scr, v_scr, *, n_rows):
    step = pl.program_id(1)
    for i, ref in enumerate((prev_ref, cur_ref, next_ref)):
        k_scr[i * NA_STEP_TOK:(i + 1) * NA_STEP_TOK, :] = ref[0, :, NA_DIM:2 * NA_DIM].astype(MXU_DTYPE)
        v_scr[i * NA_STEP_TOK:(i + 1) * NA_STEP_TOK, :] = ref[0, :, 2 * NA_DIM:3 * NA_DIM].astype(MXU_DTYPE)
    lane_head = lax.broadcasted_iota(jnp.int32, (GRID_W, NA_DIM), 1) // NA_HD
    row0 = step * NA_ROWS_PER_STEP
    for j in range(NA_ROWS_PER_STEP):
        r = row0 + j
        r0 = jnp.clip(r - NA_ROWS // 2, 0, n_rows - NA_ROWS)
        start = pl.multiple_of((r0 - row0 + NA_ROWS_PER_STEP) * GRID_W, GRID_W)
        q = cur_ref[0, j * GRID_W:(j + 1) * GRID_W, 0:NA_DIM] * np.float32(NA_HD ** -0.5)
        qs = jnp.concatenate([jnp.where(lane_head == h, q, 0.0) for h in range(NA_HEADS)], axis=0)
        kw = k_scr[pl.ds(start, NA_WIN), :]
        vw = v_scr[pl.ds(start, NA_WIN), :]
        logits = lax.dot_general(qs.astype(MXU_DTYPE), kw, (((1,), (1,)), ((), ())),
                                 preferred_element_type=F32) + tab_ref[r - r0]
        m = jnp.max(logits, axis=-1, keepdims=True)
        p = jnp.exp(logits - m)
        denom = jnp.sum(p, axis=-1, keepdims=True)
        o_all = jnp.dot(p.astype(MXU_DTYPE), vw, preferred_element_type=F32) / denom
        o = jnp.zeros((GRID_W, NA_DIM), F32)
        for h in range(NA_HEADS):
            o = o + jnp.where(lane_head == h, o_all[h * GRID_W:(h + 1) * GRID_W, :], 0.0)
        out_ref[0, j * GRID_W:(j + 1) * GRID_W, :] = o


def na_mixer(qkv, rpb, col=0):
    b, s, _ = qkv.shape
    n_rows = s // GRID_W
    assert n_rows >= NA_ROWS and n_rows % NA_ROWS_PER_STEP == 0 and col % (3 * NA_DIM) == 0
    n_steps = n_rows // NA_ROWS_PER_STEP
    blk = (1, NA_STEP_TOK, 3 * NA_DIM)
    cb = col // (3 * NA_DIM)
    return pl.pallas_call(
        functools.partial(_na_kernel, n_rows=n_rows),
        grid=(b, n_steps),
        in_specs=[
            pl.BlockSpec(blk, lambda i, j: (i, jnp.maximum(j - 1, 0), cb)),
            pl.BlockSpec(blk, lambda i, j: (i, j, cb)),
            pl.BlockSpec(blk, lambda i, j: (i, jnp.minimum(j + 1, n_steps - 1), cb)),
            pl.BlockSpec((NA_ROWS, NA_HEADS * GRID_W, NA_WIN), lambda i, j: (0, 0, 0)),
        ],
        out_specs=pl.BlockSpec((1, NA_STEP_TOK, NA_DIM), lambda i, j: (i, j, 0)),
        out_shape=jax.ShapeDtypeStruct((b, s, NA_DIM), F32),
        scratch_shapes=[
            pltpu.VMEM((3 * NA_STEP_TOK, NA_DIM), MXU_DTYPE),
            pltpu.VMEM((3 * NA_STEP_TOK, NA_DIM), MXU_DTYPE),
        ],
        compiler_params=pltpu.CompilerParams(
            dimension_semantics=("arbitrary", "arbitrary"), vmem_limit_bytes=VMEM_LIMIT_BYTES),
        name="na_attention",
    )(qkv, qkv, qkv, _na_bias_table(rpb))


PK_GATE = 0
PK_XBC = PK_GATE + N_BRANCH * D_MODEL
PK_CQ = PK_XBC + CONV_CH
PK_DIL = PK_CQ + MLA_Q_RANK
PK_NA = PK_DIL + DIL_N_GROUPS * 3 * DIL_DIM
PK_KR = PK_NA + 3 * NA_DIM
PK_Z = PK_KR + 2 * MLA_HG
PK_CKV = PK_Z + SSM_INNER
PK_DT = PK_CKV + MLA_KV_RANK
PK_DT_W = LANES
PK_WIDTH = PK_DT + PK_DT_W
INPROJ_TM = 512
INPROJ_TN = PK_WIDTH // 2
MERGE_TM = 512


def _pack_w_in(w_in_l):
    gate, a_cq, a_ckv, a_kr, b_qkv, c_z, c_xbc, c_dt, d_qkv = jnp.split(w_in_l, IN_SPLITS, axis=-1)
    zeros = lambda n: jnp.zeros((D_MODEL, n), w_in_l.dtype)
    cols = [gate, c_xbc, a_cq, b_qkv, d_qkv, mla_rotary_key_columns(a_kr), c_z, a_ckv,
            c_dt, zeros(PK_DT_W - 2 * SSM_HEADS)]
    packed = jnp.concatenate(cols, axis=-1)
    assert packed.shape[1] == PK_DT + PK_DT_W
    return jnp.concatenate([packed, zeros(PK_WIDTH - packed.shape[1])], axis=-1).astype(MXU_DTYPE)


def _inproj_kernel(x_ref, g_ref, w_ref, out_ref):
    x = x_ref[...]
    h = (x * lax.rsqrt(jnp.mean(x * x, axis=-1, keepdims=True) + EPS) * g_ref[...]).astype(MXU_DTYPE)
    out_ref[...] = jnp.dot(h, w_ref[...], preferred_element_type=F32)


def _inproj(x2d, norm_g, w_packed):
    n_tok = x2d.shape[0]
    tm, tn = INPROJ_TM, INPROJ_TN
    return pl.pallas_call(
        _inproj_kernel,
        grid=(PK_WIDTH // tn, n_tok // tm),
        in_specs=[pl.BlockSpec((tm, D_MODEL), lambda j, i: (i, 0)),
                  pl.BlockSpec((1, D_MODEL), lambda j, i: (0, 0)),
                  pl.BlockSpec((D_MODEL, tn), lambda j, i: (0, j))],
        out_specs=pl.BlockSpec((tm, tn), lambda j, i: (i, j)),
        out_shape=jax.ShapeDtypeStruct((n_tok, PK_WIDTH), F32),
        compiler_params=pltpu.CompilerParams(
            dimension_semantics=("arbitrary", "arbitrary"), vmem_limit_bytes=VMEM_LIMIT_BYTES),
        name="in_projection",
    )(x2d, norm_g.reshape(1, D_MODEL), w_packed)


def _merge_kernel(x_ref, gate_ref, bg_ref, ya_ref, yb_ref, yc_ref, yd_ref, wb_ref, wo_ref, out_ref):
    merged = jnp.zeros(x_ref.shape, F32)
    for i, y_ref in enumerate((ya_ref, yb_ref, yc_ref, yd_ref)):
        proj = jnp.dot(y_ref[...].astype(MXU_DTYPE), wb_ref[BRANCH_ROWS[i]:BRANCH_ROWS[i + 1], :],
                       preferred_element_type=F32)
        lanes = slice(i * D_MODEL, (i + 1) * D_MODEL)
        merged = merged + jax.nn.sigmoid(gate_ref[:, lanes] + bg_ref[:, lanes]) * proj
    out_ref[...] = x_ref[...] + jnp.dot(merged.astype(MXU_DTYPE), wo_ref[...], preferred_element_type=F32)


def _merge(x2d, packed, b_gate, ys, w_branch, w_out):
    n_tok = x2d.shape[0]
    tm = MERGE_TM
    tok = lambda w: pl.BlockSpec((tm, w), lambda i: (i, 0))
    full = lambda r, c: pl.BlockSpec((r, c), lambda i: (0, 0))
    n_gate = N_BRANCH * D_MODEL
    return pl.pallas_call(
        _merge_kernel,
        grid=(n_tok // tm,),
        in_specs=[tok(D_MODEL), tok(n_gate), full(1, n_gate)] + [tok(w) for w in BRANCH_WIDTHS]
                 + [full(BRANCH_ROWS[-1], D_MODEL), full(D_MODEL, D_MODEL)],
        out_specs=tok(D_MODEL),
        out_shape=jax.ShapeDtypeStruct((n_tok, D_MODEL), F32),
        compiler_params=pltpu.CompilerParams(
            dimension_semantics=("arbitrary",), vmem_limit_bytes=VMEM_LIMIT_BYTES),
        name="branch_merge",
    )(x2d, packed, b_gate.reshape(1, n_gate), *[y.reshape(n_tok, -1) for y in ys], w_branch, w_out)


def encoder(x, norm1_g, w_in_packed, b_gate, mla_q_norm, mla_w_qb, mla_kv_norm, mla_w_kvb, t5_table,
            ssm_conv_w, ssm_conv_b, ssm_A_log, ssm_dt_bias, ssm_D, ssm_norm_g, na_rpb,
            w_branch, w_out, norm2_g, peer_wq, peer_keys, peer_u, peer_vt, final_g):
    b, s, _ = x.shape
    x2d = x.reshape(b * s, D_MODEL)
    for l in range(DEPTH):
        packed = _inproj(x2d, norm1_g[l], w_in_packed[l])
        p3 = packed.reshape(b, s, PK_WIDTH)
        y_a = mla_mixer(p3, mla_q_norm[l], mla_w_qb[l], mla_kv_norm[l], mla_w_kvb[l],
                        col_q=PK_CQ, col_kv=PK_CKV, col_kr=PK_KR)
        y_b = dilated_mixer(p3, t5_table, col=PK_DIL)
        y_c = mamba2_mixer(p3, ssm_conv_w[l], ssm_conv_b[l], ssm_A_log[l], ssm_dt_bias[l], ssm_D[l],
                           ssm_norm_g[l], col_z=PK_Z, col_xbc=PK_XBC, col_dt=PK_DT, dt_w=PK_DT_W)
        y_d = na_mixer(p3, na_rpb[l], col=PK_NA)
        x2d = _merge(x2d, packed, b_gate[l], (y_a, y_b, y_c, y_d), w_branch[l], w_out[l])
        x2d = peer_block(x2d, norm2_g[l], peer_wq[l], peer_keys[l], peer_u[l], peer_vt[l],
                         final_g, final_norm=(l == DEPTH - 1))
    return x2d.reshape(b, s, D_MODEL)


def kernel(x_prompt, x_sample, norm1_g, w_in, b_gate, mla_q_norm, mla_w_qb, mla_kv_norm, mla_w_kvb, t5_table, ssm_conv_w, ssm_conv_b, ssm_A_log, ssm_dt_bias, ssm_D, ssm_norm_g, na_rpb, w_branch, w_out, norm2_g, peer_wq, peer_keys, peer_u, peer_v, final_g):
    peer_u16 = peer_u.astype(MXU_DTYPE)
    peer_vt16 = jnp.swapaxes(peer_v, 1, 2).astype(MXU_DTYPE)
    w_in_packed = jnp.stack([_pack_w_in(w_in[l]) for l in range(DEPTH)])
    shared = (norm1_g, w_in_packed, b_gate, mla_q_norm, mla_w_qb, mla_kv_norm, mla_w_kvb, t5_table,
              ssm_conv_w, ssm_conv_b, ssm_A_log, ssm_dt_bias, ssm_D, ssm_norm_g, na_rpb,
              w_branch.astype(MXU_DTYPE), w_out.astype(MXU_DTYPE), norm2_g, peer_wq, peer_keys,
              peer_u16, peer_vt16, final_g)
    y_prompt = encoder(x_prompt, *shared)
    y_sample = encoder(x_sample, *shared)
    return (y_prompt, y_sample)
```
